```python
import math
import jax, jax.numpy as jnp
from jax import lax
import numpy as np

D_MODEL = 1024
BATCH = 16
SEQ = 2048
DEPTH = 1

HEAD_DIM = 64
D_MIX = D_MODEL
A_GROUPS = 4
A_WIDTH = A_GROUPS * HEAD_DIM
B_HEADS = 12
B_WIDTH = B_HEADS * HEAD_DIM
CHUNK = 128
DILATED_CONFIGS = ((128, 1), (512, 4), (2048, 16))
NUM_BUCKETS = 32
MAX_DISTANCE = 2048
D_FF = 2816
CONV_WIDTH = 3
IN_COLS = 2 * A_WIDTH + 3 * B_WIDTH
NORM_EPS = 1e-6
NEG_INF = -1e30

kernel_name = "hybrid_gmlp_dilated_attn_convffn"


def rms_norm(x, g):
    xf = x.astype(jnp.float32)
    y = xf * lax.rsqrt(jnp.mean(xf * xf, axis=-1, keepdims=True) + NORM_EPS)
    return (y * g.astype(jnp.float32)).astype(x.dtype)


def t5_bucket(dist):
    max_exact = NUM_BUCKETS // 2
    d = jnp.maximum(dist, 1).astype(jnp.float32)
    large = max_exact + (jnp.log(d / max_exact) / math.log(MAX_DISTANCE / max_exact)
                         * (NUM_BUCKETS - max_exact))
    large = jnp.minimum(large.astype(jnp.int32), NUM_BUCKETS - 1)
    return jnp.where(dist < max_exact, dist, large)


def spatial_gating(u, v, ln_g, ln_b, w_s, b_s):
    B, T, G, hd = u.shape
    u = jax.nn.gelu(u)
    vf = jax.nn.gelu(v).astype(jnp.float32)
    mu = jnp.mean(vf, axis=-1, keepdims=True)
    var = jnp.mean(jnp.square(vf - mu), axis=-1, keepdims=True)
    vn = (vf - mu) * lax.rsqrt(var + NORM_EPS) * ln_g.astype(jnp.float32) + ln_b.astype(jnp.float32)
    vc = vn.reshape(B, T // CHUNK, CHUNK, G, hd)
    tril = jnp.tril(jnp.ones((CHUNK, CHUNK), jnp.float32))
    w = w_s.astype(jnp.float32) * tril[None]
    z = jnp.einsum('gts,bcsgd->bctgd', w, vc) + b_s.astype(jnp.float32).T[None, None, :, :, None]
    return u * z.reshape(B, T, G, hd).astype(u.dtype)


def dilated_segment(q, k, v, rel_bias, window, dil):
    B, T, H, D = q.shape
    nw = window // dil
    seg = nw * dil
    nb = -(-T // seg)
    Tp = nb * seg

    def blocks(a):
        a = jnp.pad(a, ((0, 0), (0, Tp - T), (0, 0), (0, 0)))
        return a.reshape(B, nb, nw, dil, H, D)

    def with_prev(a):
        prev = jnp.pad(a, ((0, 0), (1, 0), (0, 0), (0, 0), (0, 0), (0, 0)))[:, :-1]
        return jnp.concatenate([prev, a], axis=2)

    qb = blocks(q)
    kc = with_prev(blocks(k))
    vc = with_prev(blocks(v))

    i = jnp.arange(nw)[:, None]
    j = jnp.arange(2 * nw)[None, :]
    rel = nw + i - j
    band = (rel >= 0) & (rel <= nw)
    key_ok = (jnp.arange(nb)[:, None] * nw + jnp.arange(2 * nw)[None, :] - nw) >= 0
    mask = band[None] & key_ok[:, None, :]
    bias = rel_bias.astype(jnp.float32)[t5_bucket(jnp.maximum(rel, 0) * dil)]
    bias = bias.transpose(2, 0, 1)

    scale = 1.0 / math.sqrt(D)
    logits = jnp.einsum('bnirhd,bnjrhd->bnrhij', qb, kc) * scale + bias[None, None, None]
    logits = jnp.where(mask[None, :, None, None], logits, NEG_INF)
    lse = jax.nn.logsumexp(logits, axis=-1)
    p = jnp.exp(logits - lse[..., None])
    o = jnp.einsum('bnrhij,bnjrhd->bnirhd', p, vc)
    o = o.reshape(B, Tp, H, D)[:, :T]
    lse = lse.transpose(0, 1, 4, 2, 3).reshape(B, Tp, H)[:, :T]
    return o, lse


def dilated_attention(q, k, v, rel_bias):
    outs, lses = [], []
    for window, dil in DILATED_CONFIGS:
        o, lse = dilated_segment(q, k, v, rel_bias, window, dil)
        outs.append(o)
        lses.append(lse)
    o = jnp.stack(outs, axis=0)
    w = jax.nn.softmax(jnp.stack(lses, axis=0), axis=0)
    return jnp.sum(w[..., None] * o, axis=0)


def causal_dwconv(h, w, b):
    T = h.shape[1]
    hp = jnp.pad(h, ((0, 0), (CONV_WIDTH - 1, 0), (0, 0)))
    out = b
    for kk in range(CONV_WIDTH):
        out = out + hp[:, kk:kk + T] * w[kk]
    return out


def _fwd_setup_inputs(seed: int = 0) -> dict:
    key = jax.random.key(seed)
    ks = jax.random.split(key, 20)
    f32 = jnp.float32

    def nrm(k, shape, s):
        return jax.random.normal(k, shape, f32) * s

    L = DEPTH
    return {
        "x": jax.random.normal(ks[0], (BATCH, SEQ, D_MODEL), f32),
        "norm_mix_pre": 1.0 + nrm(ks[1], (L, D_MODEL), 0.01),
        "norm_mix_post": 1.0 + nrm(ks[2], (L, D_MODEL), 0.01),
        "norm_ffn_pre": 1.0 + nrm(ks[3], (L, D_MODEL), 0.01),
        "norm_ffn_post": 1.0 + nrm(ks[4], (L, D_MODEL), 0.01),
        "w_in": nrm(ks[5], (L, D_MODEL, IN_COLS), D_MODEL ** -0.5),
        "ln_v_gain": 1.0 + nrm(ks[6], (L, A_GROUPS, HEAD_DIM), 0.01),
        "ln_v_bias": nrm(ks[7], (L, A_GROUPS, HEAD_DIM), 0.01),
        "spatial_w": nrm(ks[8], (L, A_GROUPS, CHUNK, CHUNK), CHUNK ** -0.5),
        "spatial_b": 1.0 + nrm(ks[9], (L, A_GROUPS, CHUNK), 0.01),
        "rel_bias": nrm(ks[10], (NUM_BUCKETS, B_HEADS), 0.5),
        "w_out": nrm(ks[11], (L, D_MIX, D_MODEL), D_MIX ** -0.5),
        "w_gate": nrm(ks[12], (L, D_MODEL, D_FF), D_MODEL ** -0.5),
        "w_up": nrm(ks[13], (L, D_MODEL, D_FF), D_MODEL ** -0.5),
        "conv_w": nrm(ks[14], (L, CONV_WIDTH, D_FF), CONV_WIDTH ** -0.5),
        "conv_b": nrm(ks[15], (L, D_FF), 0.01),
        "w_down": nrm(ks[16], (L, D_FF, D_MODEL), D_FF ** -0.5),
    }


def _fwd_reference(x, norm_mix_pre, norm_mix_post, norm_ffn_pre, norm_ffn_post, w_in,
              ln_v_gain, ln_v_bias, spatial_w, spatial_b, rel_bias, w_out,
              w_gate, w_up, conv_w, conv_b, w_down):
    B, T, _ = x.shape
    for l in range(DEPTH):
        h = rms_norm(x, norm_mix_pre[l])
        proj = h @ w_in[l]
        o0 = 0
        ua = proj[..., o0:o0 + A_WIDTH].reshape(B, T, A_GROUPS, HEAD_DIM); o0 += A_WIDTH
        va = proj[..., o0:o0 + A_WIDTH].reshape(B, T, A_GROUPS, HEAD_DIM); o0 += A_WIDTH
        qb = proj[..., o0:o0 + B_WIDTH].reshape(B, T, B_HEADS, HEAD_DIM); o0 += B_WIDTH
        kb = proj[..., o0:o0 + B_WIDTH].reshape(B, T, B_HEADS, HEAD_DIM); o0 += B_WIDTH
        vb = proj[..., o0:o0 + B_WIDTH].reshape(B, T, B_HEADS, HEAD_DIM)

        a_out = spatial_gating(ua, va, ln_v_gain[l], ln_v_bias[l], spatial_w[l], spatial_b[l])
        b_out = dilated_attention(qb.astype(jnp.float32), kb.astype(jnp.float32),
                                  vb.astype(jnp.float32), rel_bias)
        mix = jnp.concatenate([a_out.reshape(B, T, A_WIDTH),
                               b_out.reshape(B, T, B_WIDTH).astype(x.dtype)], axis=-1)
        x = x + rms_norm(mix @ w_out[l], norm_mix_post[l])

        h = rms_norm(x, norm_ffn_pre[l])
        g = jax.nn.gelu(causal_dwconv(h @ w_gate[l], conv_w[l], conv_b[l]))
        y = (g * (h @ w_up[l])) @ w_down[l]
        x = x + rms_norm(y, norm_ffn_post[l])
    return x


import jax as _jax
import jax.numpy as _jnp

TWIN_FORMAT = 'train_step'
FWD_PARAMS = ['x', 'norm_mix_pre', 'norm_mix_post', 'norm_ffn_pre', 'norm_ffn_post', 'w_in', 'ln_v_gain', 'ln_v_bias', 'spatial_w', 'spatial_b', 'rel_bias', 'w_out', 'w_gate', 'w_up', 'conv_w', 'conv_b', 'w_down']
TWIN_WEIGHTS = ['norm_mix_pre', 'norm_mix_post', 'norm_ffn_pre', 'norm_ffn_post', 'w_in', 'ln_v_gain', 'ln_v_bias', 'spatial_w', 'spatial_b', 'rel_bias', 'w_out', 'w_gate', 'w_up', 'conv_w', 'conv_b', 'w_down']
TWIN_DIFF_INPUT = 'x'
TWIN_INPUTS = ['x', 'norm_mix_pre', 'norm_mix_post', 'norm_ffn_pre', 'norm_ffn_post', 'w_in', 'ln_v_gain', 'ln_v_bias', 'spatial_w', 'spatial_b', 'rel_bias', 'w_out', 'w_gate', 'w_up', 'conv_w', 'conv_b', 'w_down', 'loss_target', 'm_norm_mix_pre', 'm_norm_mix_post', 'm_norm_ffn_pre', 'm_norm_ffn_post', 'm_w_in', 'm_ln_v_gain', 'm_ln_v_bias', 'm_spatial_w', 'm_spatial_b', 'm_rel_bias', 'm_w_out', 'm_w_gate', 'm_w_up', 'm_conv_w', 'm_conv_b', 'm_w_down', 'v_norm_mix_pre', 'v_norm_mix_post', 'v_norm_ffn_pre', 'v_norm_ffn_post', 'v_w_in', 'v_ln_v_gain', 'v_ln_v_bias', 'v_spatial_w', 'v_spatial_b', 'v_rel_bias', 'v_w_out', 'v_w_gate', 'v_w_up', 'v_conv_w', 'v_conv_b', 'v_w_down']
TWIN_OUTPUTS = ['loss', 'grad_x', 'grad_norm_mix_pre', 'grad_norm_mix_post', 'grad_norm_ffn_pre', 'grad_norm_ffn_post', 'grad_w_in', 'grad_ln_v_gain', 'grad_ln_v_bias', 'grad_spatial_w', 'grad_spatial_b', 'grad_rel_bias', 'grad_w_out', 'grad_w_gate', 'grad_w_up', 'grad_conv_w', 'grad_conv_b', 'grad_w_down', 'delta_norm_mix_pre', 'delta_norm_mix_post', 'delta_norm_ffn_pre', 'delta_norm_ffn_post', 'delta_w_in', 'delta_ln_v_gain', 'delta_ln_v_bias', 'delta_spatial_w', 'delta_spatial_b', 'delta_rel_bias', 'delta_w_out', 'delta_w_gate', 'delta_w_up', 'delta_conv_w', 'delta_conv_b', 'delta_w_down', 'new_m_norm_mix_pre', 'new_m_norm_mix_post', 'new_m_norm_ffn_pre', 'new_m_norm_ffn_post', 'new_m_w_in', 'new_m_ln_v_gain', 'new_m_ln_v_bias', 'new_m_spatial_w', 'new_m_spatial_b', 'new_m_rel_bias', 'new_m_w_out', 'new_m_w_gate', 'new_m_w_up', 'new_m_conv_w', 'new_m_conv_b', 'new_m_w_down', 'new_v_norm_mix_pre', 'new_v_norm_mix_post', 'new_v_norm_ffn_pre', 'new_v_norm_ffn_post', 'new_v_w_in', 'new_v_ln_v_gain', 'new_v_ln_v_bias', 'new_v_spatial_w', 'new_v_spatial_b', 'new_v_rel_bias', 'new_v_w_out', 'new_v_w_gate', 'new_v_w_up', 'new_v_conv_w', 'new_v_conv_b', 'new_v_w_down']
TWIN_LEAF_KINDS = {'loss': 'loss', 'grad_x': 'grad_x', 'grad_norm_mix_pre': 'grad_w', 'grad_norm_mix_post': 'grad_w', 'grad_norm_ffn_pre': 'grad_w', 'grad_norm_ffn_post': 'grad_w', 'grad_w_in': 'grad_w', 'grad_ln_v_gain': 'grad_w', 'grad_ln_v_bias': 'grad_w', 'grad_spatial_w': 'grad_w', 'grad_spatial_b': 'grad_w', 'grad_rel_bias': 'grad_w', 'grad_w_out': 'grad_w', 'grad_w_gate': 'grad_w', 'grad_w_up': 'grad_w', 'grad_conv_w': 'grad_w', 'grad_conv_b': 'grad_w', 'grad_w_down': 'grad_w', 'delta_norm_mix_pre': 'delta_w', 'delta_norm_mix_post': 'delta_w', 'delta_norm_ffn_pre': 'delta_w', 'delta_norm_ffn_post': 'delta_w', 'delta_w_in': 'delta_w', 'delta_ln_v_gain': 'delta_w', 'delta_ln_v_bias': 'delta_w', 'delta_spatial_w': 'delta_w', 'delta_spatial_b': 'delta_w', 'delta_rel_bias': 'delta_w', 'delta_w_out': 'delta_w', 'delta_w_gate': 'delta_w', 'delta_w_up': 'delta_w', 'delta_conv_w': 'delta_w', 'delta_conv_b': 'delta_w', 'delta_w_down': 'delta_w', 'new_m_norm_mix_pre': 'new_m', 'new_m_norm_mix_post': 'new_m', 'new_m_norm_ffn_pre': 'new_m', 'new_m_norm_ffn_post': 'new_m', 'new_m_w_in': 'new_m', 'new_m_ln_v_gain': 'new_m', 'new_m_ln_v_bias': 'new_m', 'new_m_spatial_w': 'new_m', 'new_m_spatial_b': 'new_m', 'new_m_rel_bias': 'new_m', 'new_m_w_out': 'new_m', 'new_m_w_gate': 'new_m', 'new_m_w_up': 'new_m', 'new_m_conv_w': 'new_m', 'new_m_conv_b': 'new_m', 'new_m_w_down': 'new_m', 'new_v_norm_mix_pre': 'new_v', 'new_v_norm_mix_post': 'new_v', 'new_v_norm_ffn_pre': 'new_v', 'new_v_norm_ffn_post': 'new_v', 'new_v_w_in': 'new_v', 'new_v_ln_v_gain': 'new_v', 'new_v_ln_v_bias': 'new_v', 'new_v_spatial_w': 'new_v', 'new_v_spatial_b': 'new_v', 'new_v_rel_bias': 'new_v', 'new_v_w_out': 'new_v', 'new_v_w_gate': 'new_v', 'new_v_w_up': 'new_v', 'new_v_conv_w': 'new_v', 'new_v_conv_b': 'new_v', 'new_v_w_down': 'new_v'}


def _forward(args):
    return _fwd_reference(*[args[k] for k in FWD_PARAMS])


def _output_shape():
    out = _jax.eval_shape(lambda: _forward(_fwd_setup_inputs(0)))
    return out.shape, out.dtype

N_MICROBATCH = 1
ADAM_LR = 0.001
ADAM_B1 = 0.9
ADAM_B2 = 0.999
ADAM_EPS = 1e-08
ADAM_WD = 0.01
ADAM_STEP = 10
PER_EXAMPLE_BATCH_AXIS = {'x': 0, 'loss_target': 0}
SHARED_INPUTS = []
_WEIGHT_DTYPES = {'norm_mix_pre': _jnp.float32, 'norm_mix_post': _jnp.float32, 'norm_ffn_pre': _jnp.float32, 'norm_ffn_post': _jnp.float32, 'w_in': _jnp.float32, 'ln_v_gain': _jnp.float32, 'ln_v_bias': _jnp.float32, 'spatial_w': _jnp.float32, 'spatial_b': _jnp.float32, 'rel_bias': _jnp.float32, 'w_out': _jnp.float32, 'w_gate': _jnp.float32, 'w_up': _jnp.float32, 'conv_w': _jnp.float32, 'conv_b': _jnp.float32, 'w_down': _jnp.float32}
MOMENT_SCALE = {'norm_mix_pre': 6.760916e-01, 'norm_mix_post': 3.242591e+01, 'norm_ffn_pre': 1.313264e+00, 'norm_ffn_post': 3.192304e+01, 'w_in': 4.049713e-01, 'ln_v_gain': 6.882329e-01, 'ln_v_bias': 5.504812e-01, 'spatial_w': 3.512888e-01, 'spatial_b': 5.406960e-01, 'rel_bias': 2.526240e-01, 'w_out': 1.713766e+00, 'w_gate': 2.991223e-01, 'w_up': 7.651509e-01, 'conv_w': 6.130878e-01, 'conv_b': 1.417195e+00, 'w_down': 1.255218e+00}


def _to_microbatches(a, axis):
    t = _jnp.moveaxis(a, axis, 0)
    t = t.reshape((N_MICROBATCH, t.shape[0] // N_MICROBATCH) + t.shape[1:])
    return _jnp.moveaxis(t, 1, axis + 1)


def setup_inputs(seed: int = 0) -> dict:
    inp = _fwd_setup_inputs(seed)
    key = _jax.random.fold_in(_jax.random.key(seed), 7919)
    shape, _ = _output_shape()
    out = dict(inp)
    out["loss_target"] = _jax.random.normal(_jax.random.fold_in(key, 0), shape, _jnp.float32)
    for i, name in enumerate(TWIN_WEIGHTS):
        w = inp[name].astype(_jnp.float32)
        if MOMENT_SCALE is None:
            s = _jnp.sqrt(_jnp.mean(_jnp.square(w)) + 1e-30)
        else:
            s = MOMENT_SCALE[name]
        km, kv = _jax.random.split(_jax.random.fold_in(key, i + 1))
        out[name] = w
        out["m_" + name] = s * _jax.random.normal(km, w.shape, _jnp.float32)
        out["v_" + name] = (s * s) * _jax.random.uniform(kv, w.shape, _jnp.float32, 0.5, 1.5)
    if N_MICROBATCH > 1:
        for name, axis in PER_EXAMPLE_BATCH_AXIS.items():
            out[name] = _to_microbatches(out[name], axis)
    return {'x': out['x'], 'norm_mix_pre': out['norm_mix_pre'], 'norm_mix_post': out['norm_mix_post'], 'norm_ffn_pre': out['norm_ffn_pre'], 'norm_ffn_post': out['norm_ffn_post'], 'w_in': out['w_in'], 'ln_v_gain': out['ln_v_gain'], 'ln_v_bias': out['ln_v_bias'], 'spatial_w': out['spatial_w'], 'spatial_b': out['spatial_b'], 'rel_bias': out['rel_bias'], 'w_out': out['w_out'], 'w_gate': out['w_gate'], 'w_up': out['w_up'], 'conv_w': out['conv_w'], 'conv_b': out['conv_b'], 'w_down': out['w_down'], 'loss_target': out['loss_target'], 'm_norm_mix_pre': out['m_norm_mix_pre'], 'm_norm_mix_post': out['m_norm_mix_post'], 'm_norm_ffn_pre': out['m_norm_ffn_pre'], 'm_norm_ffn_post': out['m_norm_ffn_post'], 'm_w_in': out['m_w_in'], 'm_ln_v_gain': out['m_ln_v_gain'], 'm_ln_v_bias': out['m_ln_v_bias'], 'm_spatial_w': out['m_spatial_w'], 'm_spatial_b': out['m_spatial_b'], 'm_rel_bias': out['m_rel_bias'], 'm_w_out': out['m_w_out'], 'm_w_gate': out['m_w_gate'], 'm_w_up': out['m_w_up'], 'm_conv_w': out['m_conv_w'], 'm_conv_b': out['m_conv_b'], 'm_w_down': out['m_w_down'], 'v_norm_mix_pre': out['v_norm_mix_pre'], 'v_norm_mix_post': out['v_norm_mix_post'], 'v_norm_ffn_pre': out['v_norm_ffn_pre'], 'v_norm_ffn_post': out['v_norm_ffn_post'], 'v_w_in': out['v_w_in'], 'v_ln_v_gain': out['v_ln_v_gain'], 'v_ln_v_bias': out['v_ln_v_bias'], 'v_spatial_w': out['v_spatial_w'], 'v_spatial_b': out['v_spatial_b'], 'v_rel_bias': out['v_rel_bias'], 'v_w_out': out['v_w_out'], 'v_w_gate': out['v_w_gate'], 'v_w_up': out['v_w_up'], 'v_conv_w': out['v_conv_w'], 'v_conv_b': out['v_conv_b'], 'v_w_down': out['v_w_down']}


def _loss(weights, diff, rest, loss_target):
    with _jax.named_scope("forward"):
        args = {**rest, TWIN_DIFF_INPUT: diff, **{k: w.astype(_WEIGHT_DTYPES[k]) for k, w in weights.items()}}
        y = _forward(args)
    with _jax.named_scope("loss_head"):
        err = _jnp.square(y.astype(_jnp.float32) - loss_target)
        return 0.5 * _jnp.sum(_jnp.mean(err, axis=-1)) if err.ndim else 0.5 * err


def _adamw(w, g, m, v):
    m = ADAM_B1 * m + (1.0 - ADAM_B1) * g
    v = ADAM_B2 * v + (1.0 - ADAM_B2) * _jnp.square(g)
    m_hat = m / (1.0 - ADAM_B1 ** ADAM_STEP)
    v_hat = v / (1.0 - ADAM_B2 ** ADAM_STEP)
    delta = -ADAM_LR * (m_hat / (_jnp.sqrt(v_hat) + ADAM_EPS) + ADAM_WD * w)
    return delta, m, v


def reference(x, norm_mix_pre, norm_mix_post, norm_ffn_pre, norm_ffn_post, w_in, ln_v_gain, ln_v_bias, spatial_w, spatial_b, rel_bias, w_out, w_gate, w_up, conv_w, conv_b, w_down, loss_target, m_norm_mix_pre, m_norm_mix_post, m_norm_ffn_pre, m_norm_ffn_post, m_w_in, m_ln_v_gain, m_ln_v_bias, m_spatial_w, m_spatial_b, m_rel_bias, m_w_out, m_w_gate, m_w_up, m_conv_w, m_conv_b, m_w_down, v_norm_mix_pre, v_norm_mix_post, v_norm_ffn_pre, v_norm_ffn_post, v_w_in, v_ln_v_gain, v_ln_v_bias, v_spatial_w, v_spatial_b, v_rel_bias, v_w_out, v_w_gate, v_w_up, v_conv_w, v_conv_b, v_w_down):
    given = dict(x=x, norm_mix_pre=norm_mix_pre, norm_mix_post=norm_mix_post, norm_ffn_pre=norm_ffn_pre, norm_ffn_post=norm_ffn_post, w_in=w_in, ln_v_gain=ln_v_gain, ln_v_bias=ln_v_bias, spatial_w=spatial_w, spatial_b=spatial_b, rel_bias=rel_bias, w_out=w_out, w_gate=w_gate, w_up=w_up, conv_w=conv_w, conv_b=conv_b, w_down=w_down, loss_target=loss_target, m_norm_mix_pre=m_norm_mix_pre, m_norm_mix_post=m_norm_mix_post, m_norm_ffn_pre=m_norm_ffn_pre, m_norm_ffn_post=m_norm_ffn_post, m_w_in=m_w_in, m_ln_v_gain=m_ln_v_gain, m_ln_v_bias=m_ln_v_bias, m_spatial_w=m_spatial_w, m_spatial_b=m_spatial_b, m_rel_bias=m_rel_bias, m_w_out=m_w_out, m_w_gate=m_w_gate, m_w_up=m_w_up, m_conv_w=m_conv_w, m_conv_b=m_conv_b, m_w_down=m_w_down, v_norm_mix_pre=v_norm_mix_pre, v_norm_mix_post=v_norm_mix_post, v_norm_ffn_pre=v_norm_ffn_pre, v_norm_ffn_post=v_norm_ffn_post, v_w_in=v_w_in, v_ln_v_gain=v_ln_v_gain, v_ln_v_bias=v_ln_v_bias, v_spatial_w=v_spatial_w, v_spatial_b=v_spatial_b, v_rel_bias=v_rel_bias, v_w_out=v_w_out, v_w_gate=v_w_gate, v_w_up=v_w_up, v_conv_w=v_conv_w, v_conv_b=v_conv_b, v_w_down=v_w_down)
    weights = {n: given[n] for n in TWIN_WEIGHTS}
    shared = {n: given[n] for n in SHARED_INPUTS}
    per_example = {n: given[n] for n in ['x']}
    grad_fn = _jax.value_and_grad(_loss, argnums=(0, 1))

    def one_microbatch(ex, loss_target):
        ex = dict(ex)
        diff = ex.pop(TWIN_DIFF_INPUT)
        return grad_fn(weights, diff, {**shared, **ex}, loss_target)

    if N_MICROBATCH == 1:
        loss, (grad_w, grad_x) = one_microbatch(per_example, given["loss_target"])
    else:
        def body(carry, xs):
            loss_sum, grad_sum = carry
            l_k, (gw_k, gx_k) = one_microbatch(xs[0], xs[1])
            with _jax.named_scope("update"):
                return (loss_sum + l_k, _jax.tree.map(_jnp.add, grad_sum, gw_k)), gx_k

        init = (_jnp.zeros((), _jnp.float32), _jax.tree.map(_jnp.zeros_like, weights))
        (loss, grad_w), grad_x = _jax.lax.scan(body, init, (per_example, given["loss_target"]))
    with _jax.named_scope("update"):
        delta_w, new_m, new_v = {}, {}, {}
        for n in TWIN_WEIGHTS:
            delta_w[n], new_m[n], new_v[n] = _adamw(weights[n], grad_w[n], given["m_" + n], given["v_" + n])
    return (loss, grad_x, *[grad_w[n] for n in TWIN_WEIGHTS], *[delta_w[n] for n in TWIN_WEIGHTS],
            *[new_m[n] for n in TWIN_WEIGHTS], *[new_v[n] for n in TWIN_WEIGHTS])
```

```python
import functools
import math

import numpy as np
import jax
import jax.numpy as jnp
from jax import lax
from jax.experimental import pallas as pl
from jax.experimental.pallas import tpu as pltpu

F32 = jnp.float32
BF16 = jnp.bfloat16
SDS = jax.ShapeDtypeStruct

D_MODEL = 1024
SEQ = 2048
HEAD_DIM = 64
A_GROUPS = 4
A_WIDTH = A_GROUPS * HEAD_DIM
B_HEADS = 12
B_WIDTH = B_HEADS * HEAD_DIM
HEAD_PAIRS = B_HEADS // 2
CHUNK = 128
ATTN_BLOCK = 128
DILATIONS = (1, 4, 16)
NUM_BUCKETS = 32
MAX_DISTANCE = 2048
D_FF = 2816
IN_COLS = 2 * A_WIDTH + 3 * B_WIDTH
Q_OFF = 2 * A_WIDTH
K_OFF = Q_OFF + B_WIDTH
V_OFF = K_OFF + B_WIDTH
NORM_EPS = 1e-6
NEG_INF = -1e30
N_DEV = 8
LANE = 128

ADAM_LR = 0.001
ADAM_B1 = 0.9
ADAM_B2 = 0.999
ADAM_EPS = 1e-08
ADAM_WD = 0.01
ADAM_STEP = 10

GELU_C0 = math.sqrt(2.0 / math.pi)
GELU_C1 = 0.044715

VMEM_LIMIT = 56 * 1024 * 1024


def _params(sem=None):
    if sem is None:
        return pltpu.CompilerParams(vmem_limit_bytes=VMEM_LIMIT)
    return pltpu.CompilerParams(dimension_semantics=sem, vmem_limit_bytes=VMEM_LIMIT)


def _gelu(x):
    t = jnp.tanh(GELU_C0 * (x + GELU_C1 * x * x * x))
    return 0.5 * x * (1.0 + t)


def _gelu_and_grad(x):
    x2 = x * x
    t = jnp.tanh(GELU_C0 * (x + GELU_C1 * x * x2))
    g = 0.5 * x * (1.0 + t)
    dg = 0.5 * (1.0 + t) + 0.5 * x * (1.0 - t * t) * (GELU_C0 * (1.0 + 3.0 * GELU_C1 * x2))
    return g, dg


def _dot(a, b):
    return jnp.dot(a, b, preferred_element_type=F32)


def _dot_nt(a, b):
    return lax.dot_general(a, b, (((1,), (1,)), ((), ())), preferred_element_type=F32)


def _dot_tn(a, b):
    return lax.dot_general(a, b, (((0,), (0,)), ((), ())), preferred_element_type=F32)


def _rms_bwd(d, xin, g):
    r = lax.rsqrt(jnp.mean(xin * xin, axis=-1, keepdims=True) + NORM_EPS)
    xh = xin * r
    gd = g * d
    dx = r * (gd - xh * jnp.mean(gd * xh, axis=-1, keepdims=True))
    return dx, d * xh


def norm_mm(x, g, ws, name, tm=512, tn=1408):
    n, d = x.shape
    f = ws[0].shape[1]
    nw = len(ws)

    def body(x_ref, g_ref, *refs):
        w_refs = refs[:nw]
        h_ref = refs[nw]
        o_refs = refs[nw + 1:]

        @pl.when(pl.program_id(1) == 0)
        def _():
            xv = x_ref[...]
            r = lax.rsqrt(jnp.mean(xv * xv, axis=-1, keepdims=True) + NORM_EPS)
            h_ref[...] = (xv * r * g_ref[...]).astype(BF16)

        h = h_ref[...]
        for w_ref, o_ref in zip(w_refs, o_refs):
            o_ref[...] = _dot(h, w_ref[...])

    return pl.pallas_call(
        body, name=name, grid=(n // tm, f // tn),
        in_specs=[pl.BlockSpec((tm, d), lambda i, j: (i, 0)), pl.BlockSpec((1, d), lambda i, j: (0, 0))]
        + [pl.BlockSpec((d, tn), lambda i, j: (0, j)) for _ in ws],
        out_specs=[pl.BlockSpec((tm, d), lambda i, j: (i, 0))]
        + [pl.BlockSpec((tm, tn), lambda i, j: (i, j)) for _ in ws],
        out_shape=[SDS((n, d), BF16)] + [SDS((n, f), F32) for _ in ws],
        compiler_params=_params(("parallel", "arbitrary")),
    )(x, g, *ws)


def mm_res_norm(a, w, res, g, name, tm=512):
    n, k = a.shape
    d = w.shape[1]

    def body(a_ref, w_ref, res_ref, g_ref, y_ref, o_ref):
        y = _dot(a_ref[...].astype(BF16), w_ref[...])
        r = lax.rsqrt(jnp.mean(y * y, axis=-1, keepdims=True) + NORM_EPS)
        y_ref[...] = y
        o_ref[...] = res_ref[...] + y * r * g_ref[...]

    return pl.pallas_call(
        body, name=name, grid=(n // tm,),
        in_specs=[pl.BlockSpec((tm, k), lambda i: (i, 0)), pl.BlockSpec((k, d), lambda i: (0, 0)),
                  pl.BlockSpec((tm, d), lambda i: (i, 0)), pl.BlockSpec((1, d), lambda i: (0, 0))],
        out_specs=[pl.BlockSpec((tm, d), lambda i: (i, 0)), pl.BlockSpec((tm, d), lambda i: (i, 0))],
        out_shape=[SDS((n, d), F32), SDS((n, d), F32)],
        compiler_params=_params(("parallel",)),
    )(a, w, res, g)


def down_loss(a, w, res, g, target, name, tm=256):
    n, k = a.shape
    d = w.shape[1]
    inv_d = 1.0 / d

    def body(a_ref, w_ref, res_ref, g_ref, t_ref, dy_ref, dout_ref, dg_ref, loss_ref):
        i = pl.program_id(0)
        y = _dot(a_ref[...], w_ref[...])
        gv = g_ref[...]
        r = lax.rsqrt(jnp.mean(y * y, axis=-1, keepdims=True) + NORM_EPS)
        yh = y * r
        e = res_ref[...] + yh * gv - t_ref[...]
        part = 0.5 * inv_d * jnp.sum(jnp.sum(e * e, axis=-1, keepdims=True), axis=0, keepdims=True)
        dout = e * inv_d
        dout_ref[...] = dout
        gd = gv * dout
        dy_ref[...] = (r * (gd - yh * jnp.mean(gd * yh, axis=-1, keepdims=True))).astype(BF16)
        dgp = jnp.sum(dout * yh, axis=0, keepdims=True)
        lane0 = lax.broadcasted_iota(jnp.int32, (1, LANE), 1) == 0
        lp = jnp.where(lane0, part, 0.0)

        @pl.when(i == 0)
        def _():
            dg_ref[...] = dgp
            loss_ref[...] = lp

        @pl.when(i > 0)
        def _():
            dg_ref[...] += dgp
            loss_ref[...] += lp

    return pl.pallas_call(
        body, name=name, grid=(n // tm,),
        in_specs=[pl.BlockSpec((tm, k), lambda i: (i, 0)), pl.BlockSpec((k, d), lambda i: (0, 0)),
                  pl.BlockSpec((tm, d), lambda i: (i, 0)), pl.BlockSpec((1, d), lambda i: (0, 0)),
                  pl.BlockSpec((tm, d), lambda i: (i, 0))],
        out_specs=[pl.BlockSpec((tm, d), lambda i: (i, 0)), pl.BlockSpec((tm, d), lambda i: (i, 0)),
                   pl.BlockSpec((1, d), lambda i: (0, 0)), pl.BlockSpec((1, LANE), lambda i: (0, 0))],
        out_shape=[SDS((n, d), BF16), SDS((n, d), F32), SDS((1, d), F32), SDS((1, LANE), F32)],
        compiler_params=_params(("arbitrary",)),
    )(a, w, res, g, target)


def mm_nt(terms, ws, name, tm=512, out_dtype=F32):
    n = terms[0][0].shape[0]
    r = ws[0].shape[0]
    na = len(terms)
    meta = [(widx, off, a.shape[1]) for a, widx, off in terms]

    def body(*refs):
        a_refs = refs[:na]
        w_refs = refs[na:na + len(ws)]
        o_ref = refs[-1]
        acc = None
        for a_ref, (widx, off, k) in zip(a_refs, meta):
            p = _dot_nt(a_ref[...].astype(BF16), w_refs[widx][:, off:off + k])
            acc = p if acc is None else acc + p
        o_ref[...] = acc.astype(out_dtype)

    return pl.pallas_call(
        body, name=name, grid=(n // tm,),
        in_specs=[pl.BlockSpec((tm, a.shape[1]), lambda i: (i, 0)) for a, _, _ in terms]
        + [pl.BlockSpec(w.shape, lambda i: (0, 0)) for w in ws],
        out_specs=pl.BlockSpec((tm, r), lambda i: (i, 0)),
        out_shape=SDS((n, r), out_dtype),
        compiler_params=_params(("parallel",)),
    )(*[a for a, _, _ in terms], *ws)


def mm_tn(lhs, rhs_list, name, t1=512, tn=512, out_dtype=BF16):
    n, k1 = lhs.shape
    widths = [r.shape[1] for r in rhs_list]
    k2 = sum(widths)
    nr = len(rhs_list)
    nk = n // tn

    def body(l_ref, *refs):
        r_refs = refs[:nr]
        o_ref = refs[nr]
        acc = refs[nr + 1]
        k = pl.program_id(1)
        a = l_ref[...].astype(BF16)
        parts = [_dot_tn(a, r_ref[...].astype(BF16)) for r_ref in r_refs]
        val = parts[0] if nr == 1 else jnp.concatenate(parts, axis=1)

        @pl.when(k == 0)
        def _():
            acc[...] = val

        @pl.when(k > 0)
        def _():
            acc[...] += val

        @pl.when(k == nk - 1)
        def _():
            o_ref[...] = acc[...].astype(out_dtype)

    return pl.pallas_call(
        body, name=name, grid=(k1 // t1, nk),
        in_specs=[pl.BlockSpec((tn, t1), lambda i, k: (k, i))]
        + [pl.BlockSpec((tn, w), lambda i, k: (k, 0)) for w in widths],
        out_specs=pl.BlockSpec((t1, k2), lambda i, k: (i, 0)),
        out_shape=SDS((k1, k2), out_dtype),
        scratch_shapes=[pltpu.VMEM((t1, k2), F32)],
        compiler_params=_params(("parallel", "arbitrary")),
    )(lhs, *rhs_list)


GATE_ROWS = 512


def _tril_mask():
    row = lax.broadcasted_iota(jnp.int32, (CHUNK, CHUNK), 0)
    col = lax.broadcasted_iota(jnp.int32, (CHUNK, CHUNK), 1)
    return row >= col


def _layer_norm_parts(gv):
    mu = jnp.mean(gv, axis=-1, keepdims=True)
    xc = gv - mu
    rstd = lax.rsqrt(jnp.mean(xc * xc, axis=-1, keepdims=True) + NORM_EPS)
    return xc * rstd, rstd


def gating_fwd(proj, lng, lnb, ws, sbt, name):
    n = proj.shape[0]
    nchunk = GATE_ROWS // CHUNK

    def body(u_ref, v_ref, lng_ref, lnb_ref, ws_ref, sbt_ref, a_ref):
        tril = _tril_mask()
        for g in range(A_GROUPS):
            cs = slice(g * HEAD_DIM, (g + 1) * HEAD_DIM)
            wt = jnp.where(tril, ws_ref[g], 0.0).astype(BF16)
            for c in range(nchunk):
                rs_ = slice(c * CHUNK, (c + 1) * CHUNK)
                vhat, _ = _layer_norm_parts(_gelu(v_ref[rs_, cs]))
                vn = vhat * lng_ref[:, cs] + lnb_ref[:, cs]
                z = _dot(wt, vn.astype(BF16)) + sbt_ref[:, g:g + 1]
                a_ref[rs_, cs] = _gelu(u_ref[rs_, cs]) * z

    return pl.pallas_call(
        body, name=name, grid=(n // GATE_ROWS,),
        in_specs=[pl.BlockSpec((GATE_ROWS, A_WIDTH), lambda i: (i, 0)),
                  pl.BlockSpec((GATE_ROWS, A_WIDTH), lambda i: (i, 1)),
                  pl.BlockSpec((1, A_WIDTH), lambda i: (0, 0)), pl.BlockSpec((1, A_WIDTH), lambda i: (0, 0)),
                  pl.BlockSpec((A_GROUPS, CHUNK, CHUNK), lambda i: (0, 0, 0)),
                  pl.BlockSpec((CHUNK, A_GROUPS), lambda i: (0, 0))],
        out_specs=pl.BlockSpec((GATE_ROWS, A_WIDTH), lambda i: (i, 0)),
        out_shape=SDS((n, A_WIDTH), F32),
        compiler_params=_params(("parallel",)),
    )(proj, proj, lng, lnb, ws, sbt)


def gating_bwd(proj, dmix, lng, lnb, ws, sbt, name):
    n = proj.shape[0]
    nchunk = GATE_ROWS // CHUNK

    def body(u_ref, v_ref, da_ref, lng_ref, lnb_ref, ws_ref, sbt_ref,
             duv_ref, dws_ref, dsbt_ref, dlng_ref, dlnb_ref):
        @pl.when(pl.program_id(0) == 0)
        def _():
            dws_ref[...] = jnp.zeros_like(dws_ref)
            dsbt_ref[...] = jnp.zeros_like(dsbt_ref)
            dlng_ref[...] = jnp.zeros_like(dlng_ref)
            dlnb_ref[...] = jnp.zeros_like(dlnb_ref)

        tril = _tril_mask()
        for g in range(A_GROUPS):
            cs = slice(g * HEAD_DIM, (g + 1) * HEAD_DIM)
            wt = jnp.where(tril, ws_ref[g], 0.0).astype(BF16)
            lg = lng_ref[:, cs]
            dw = jnp.zeros((CHUNK, CHUNK), F32)
            dsb = jnp.zeros((CHUNK, 1), F32)
            dlg = jnp.zeros((1, HEAD_DIM), F32)
            dlb = jnp.zeros((1, HEAD_DIM), F32)
            for c in range(nchunk):
                rs_ = slice(c * CHUNK, (c + 1) * CHUNK)
                gu, dgu_dx = _gelu_and_grad(u_ref[rs_, cs])
                gv, dgv_dx = _gelu_and_grad(v_ref[rs_, cs])
                vhat, rstd = _layer_norm_parts(gv)
                vn = (vhat * lg + lnb_ref[:, cs]).astype(BF16)
                z = _dot(wt, vn) + sbt_ref[:, g:g + 1]
                da = da_ref[rs_, cs]
                dz = da * gu
                dzb = dz.astype(BF16)
                duv_ref[rs_, cs] = da * z * dgu_dx
                dsb = dsb + jnp.sum(dz, axis=-1, keepdims=True)
                dw = dw + _dot_nt(dzb, vn)
                dvn = _dot_tn(wt, dzb)
                dlg = dlg + jnp.sum(dvn * vhat, axis=0, keepdims=True)
                dlb = dlb + jnp.sum(dvn, axis=0, keepdims=True)
                dvh = dvn * lg
                dgv = rstd * (dvh - jnp.mean(dvh, axis=-1, keepdims=True)
                              - vhat * jnp.mean(dvh * vhat, axis=-1, keepdims=True))
                duv_ref[rs_, A_WIDTH + g * HEAD_DIM:A_WIDTH + (g + 1) * HEAD_DIM] = dgv * dgv_dx
            dws_ref[g] += jnp.where(tril, dw, 0.0)
            dsbt_ref[:, g:g + 1] += dsb
            dlng_ref[:, cs] += dlg
            dlnb_ref[:, cs] += dlb

    return pl.pallas_call(
        body, name=name, grid=(n // GATE_ROWS,),
        in_specs=[pl.BlockSpec((GATE_ROWS, A_WIDTH), lambda i: (i, 0)),
                  pl.BlockSpec((GATE_ROWS, A_WIDTH), lambda i: (i, 1)),
                  pl.BlockSpec((GATE_ROWS, A_WIDTH), lambda i: (i, 0)),
                  pl.BlockSpec((1, A_WIDTH), lambda i: (0, 0)), pl.BlockSpec((1, A_WIDTH), lambda i: (0, 0)),
                  pl.BlockSpec((A_GROUPS, CHUNK, CHUNK), lambda i: (0, 0, 0)),
                  pl.BlockSpec((CHUNK, A_GROUPS), lambda i: (0, 0))],
        out_specs=[pl.BlockSpec((GATE_ROWS, 2 * A_WIDTH), lambda i: (i, 0)),
                   pl.BlockSpec((A_GROUPS, CHUNK, CHUNK), lambda i: (0, 0, 0)),
                   pl.BlockSpec((CHUNK, A_GROUPS), lambda i: (0, 0)),
                   pl.BlockSpec((1, A_WIDTH), lambda i: (0, 0)), pl.BlockSpec((1, A_WIDTH), lambda i: (0, 0))],
        out_shape=[SDS((n, 2 * A_WIDTH), F32), SDS((A_GROUPS, CHUNK, CHUNK), F32), SDS((CHUNK, A_GROUPS), F32),
                   SDS((1, A_WIDTH), F32), SDS((1, A_WIDTH), F32)],
        compiler_params=_params(("arbitrary",)),
    )(proj, proj, dmix, lng, lnb, ws, sbt)


def _t5_bucket_np(dist):
    max_exact = NUM_BUCKETS // 2
    dd = np.maximum(dist, 1).astype(np.float64)
    large = max_exact + np.log(dd / max_exact) / math.log(MAX_DISTANCE / max_exact) * (NUM_BUCKETS - max_exact)
    large = np.minimum(large.astype(np.int64), NUM_BUCKETS - 1)
    return np.where(dist < max_exact, dist, large)


def _bucket_tables():
    i = np.arange(ATTN_BLOCK)[:, None]
    j = np.arange(2 * ATTN_BLOCK)[None, :]
    rel = ATTN_BLOCK + i - j
    band = (rel >= 0) & (rel <= ATTN_BLOCK)
    tabs = []
    for dil in DILATIONS:
        b = _t5_bucket_np(np.maximum(rel, 0) * dil)
        tabs.append(np.where(band, b, -1).reshape(1, -1))
    return np.stack(tabs).astype(np.float32)


BIAS_SIZE = ATTN_BLOCK * 2 * ATTN_BLOCK


def bias_tables(rel_bias_t, name):
    idx = jnp.asarray(_bucket_tables())

    def body(rb_ref, idx_ref, o_ref):
        iv = idx_ref[0]
        bk = lax.broadcasted_iota(jnp.int32, (NUM_BUCKETS, BIAS_SIZE), 0).astype(F32)
        onehot = (bk == iv).astype(F32)
        t = jnp.dot(rb_ref[...], onehot, preferred_element_type=F32, precision=lax.Precision.HIGHEST)
        o_ref[0] = jnp.where(iv < 0.0, NEG_INF, t)

    return pl.pallas_call(
        body, name=name, grid=(len(DILATIONS),),
        in_specs=[pl.BlockSpec((B_HEADS, NUM_BUCKETS), lambda d: (0, 0)),
                  pl.BlockSpec((1, 1, BIAS_SIZE), lambda d: (d, 0, 0))],
        out_specs=pl.BlockSpec((1, B_HEADS, BIAS_SIZE), lambda d: (d, 0, 0)),
        out_shape=SDS((len(DILATIONS), B_HEADS, BIAS_SIZE), F32),
        compiler_params=_params(("parallel",)),
    )(rel_bias_t, idx)


def rel_bias_grad(dbias, name):
    idx = jnp.asarray(_bucket_tables())

    def body(db_ref, idx_ref, o_ref):
        d = pl.program_id(0)
        iv = idx_ref[0]
        bk = lax.broadcasted_iota(jnp.int32, (NUM_BUCKETS, BIAS_SIZE), 0).astype(F32)
        onehot = (bk == iv).astype(F32)
        part = lax.dot_general(db_ref[0], onehot, (((1,), (1,)), ((), ())),
                               preferred_element_type=F32, precision=lax.Precision.HIGHEST)

        @pl.when(d == 0)
        def _():
            o_ref[...] = part

        @pl.when(d > 0)
        def _():
            o_ref[...] += part

    return pl.pallas_call(
        body, name=name, grid=(len(DILATIONS),),
        in_specs=[pl.BlockSpec((1, B_HEADS, BIAS_SIZE), lambda d: (d, 0, 0)),
                  pl.BlockSpec((1, 1, BIAS_SIZE), lambda d: (d, 0, 0))],
        out_specs=pl.BlockSpec((B_HEADS, NUM_BUCKETS), lambda d: (0, 0)),
        out_shape=SDS((B_HEADS, NUM_BUCKETS), F32),
        compiler_params=_params(("arbitrary",)),
    )(dbias, idx)


def _attn_scores(q, kk, bias):
    return _dot_nt(q, kk) * (1.0 / math.sqrt(HEAD_DIM)) + bias


def attn_fwd(proj, bias, dil, nb_local, name):
    n = proj.shape[0]
    ln = SEQ // dil
    nblk = ln // ATTN_BLOCK
    pv = proj.reshape(nb_local, ln, dil * IN_COLS)
    cb = IN_COLS // LANE

    def body(q_ref, k_ref, v_ref, b_ref, o_ref, lse_ref):
        def block(nq, first):
            row0 = pl.multiple_of(nq * ATTN_BLOCK, ATTN_BLOCK)
            qrows = pl.ds(row0, ATTN_BLOCK)
            if first:
                krows = pl.ds(0, ATTN_BLOCK)
            else:
                krows = pl.ds(pl.multiple_of(row0 - ATTN_BLOCK, ATTN_BLOCK), 2 * ATTN_BLOCK)
            outs, lses = [], []
            for hh in range(2):
                cs = slice(hh * HEAD_DIM, (hh + 1) * HEAD_DIM)
                q = q_ref[qrows, cs].astype(BF16)
                kk = k_ref[krows, cs].astype(BF16)
                vv = v_ref[krows, cs].astype(BF16)
                bb = b_ref[hh, :, ATTN_BLOCK:] if first else b_ref[hh]
                s = _attn_scores(q, kk, bb)
                m = jnp.max(s, axis=-1, keepdims=True)
                p = jnp.exp(s - m)
                l = jnp.sum(p, axis=-1, keepdims=True)
                outs.append(_dot(p.astype(BF16), vv) / l)
                lses.append(jnp.broadcast_to(m + jnp.log(l), (ATTN_BLOCK, HEAD_DIM)))
            o_ref[qrows, :] = jnp.concatenate(outs, axis=1)
            lse_ref[qrows, :] = jnp.concatenate(lses, axis=1)

        block(0, True)
        if nblk > 1:
            def loop_body(nq, carry):
                block(nq, False)
                return carry
            lax.fori_loop(1, nblk, loop_body, 0)

    def in_spec(off):
        return pl.BlockSpec((None, ln, LANE), lambda b, r, p: (b, 0, r * cb + off // LANE + p))

    ob = B_WIDTH // LANE
    out_spec = pl.BlockSpec((None, ln, LANE), lambda b, r, p: (b, 0, r * ob + p))
    o, lse = pl.pallas_call(
        body, name=name, grid=(nb_local, dil, HEAD_PAIRS),
        in_specs=[in_spec(Q_OFF), in_spec(K_OFF), in_spec(V_OFF),
                  pl.BlockSpec((2, ATTN_BLOCK, 2 * ATTN_BLOCK), lambda b, r, p: (p, 0, 0))],
        out_specs=[out_spec, out_spec],
        out_shape=[SDS((nb_local, ln, dil * B_WIDTH), F32), SDS((nb_local, ln, dil * B_WIDTH), F32)],
        compiler_params=_params(("parallel", "parallel", "arbitrary")),
    )(pv, pv, pv, bias)
    return o.reshape(n, B_WIDTH), lse.reshape(n, B_WIDTH)


def attn_merge(a, os_, lses, name, tm=512):
    n = a.shape[0]
    ns = len(os_)

    def body(a_ref, *refs):
        o_refs = refs[:ns]
        l_refs = refs[ns:2 * ns]
        mix_ref, lt_ref = refs[2 * ns], refs[2 * ns + 1]
        ls = [r[...] for r in l_refs]
        m = functools.reduce(jnp.maximum, ls)
        ws = [jnp.exp(l - m) for l in ls]
        den = functools.reduce(lambda x, y: x + y, ws)
        num = functools.reduce(lambda x, y: x + y, [w * r[...] for w, r in zip(ws, o_refs)])
        mix_ref[:, :A_WIDTH] = a_ref[...]
        mix_ref[:, A_WIDTH:] = num / den
        lt_ref[...] = m + jnp.log(den)

    row = lambda w: pl.BlockSpec((tm, w), lambda i: (i, 0))
    return pl.pallas_call(
        body, name=name, grid=(n // tm,),
        in_specs=[row(A_WIDTH)] + [row(B_WIDTH)] * (2 * ns),
        out_specs=[row(D_MODEL), row(B_WIDTH)],
        out_shape=[SDS((n, D_MODEL), F32), SDS((n, B_WIDTH), F32)],
        compiler_params=_params(("parallel",)),
    )(a, *os_, *lses)


def attn_bwd(proj, mix, dmix, lse_tot, bias, dil, nb_local, acc, name):
    n = proj.shape[0]
    ln = SEQ // dil
    nblk = ln // ATTN_BLOCK
    cb = IN_COLS // LANE
    mb = D_MODEL // LANE
    ob = B_WIDTH // LANE
    a_blocks = A_WIDTH // LANE
    pv = proj.reshape(nb_local, ln, dil * IN_COLS)
    mixv = mix.reshape(nb_local, ln, dil * D_MODEL)
    dmixv = dmix.reshape(nb_local, ln, dil * D_MODEL)
    lsev = lse_tot.reshape(nb_local, ln, dil * B_WIDTH)
    has_acc = acc is not None
    scale = 1.0 / math.sqrt(HEAD_DIM)

    def body(*refs):
        q_ref, k_ref, v_ref, o_ref, do_ref, lse_ref, b_ref = refs[:7]
        refs = refs[7:]
        if has_acc:
            aq_ref, ak_ref, av_ref = refs[:3]
            refs = refs[3:]
        dq_ref, dk_ref, dv_ref, db_ref = refs

        @pl.when((pl.program_id(1) == 0) & (pl.program_id(2) == 0))
        def _():
            db_ref[...] = jnp.zeros_like(db_ref)

        if has_acc:
            dk_ref[...] = ak_ref[...]
            dv_ref[...] = av_ref[...]
        else:
            dk_ref[...] = jnp.zeros_like(dk_ref)
            dv_ref[...] = jnp.zeros_like(dv_ref)

        def block(nq, first):
            row0 = pl.multiple_of(nq * ATTN_BLOCK, ATTN_BLOCK)
            qrows = pl.ds(row0, ATTN_BLOCK)
            if first:
                krows = pl.ds(0, ATTN_BLOCK)
            else:
                krows = pl.ds(pl.multiple_of(row0 - ATTN_BLOCK, ATTN_BLOCK), 2 * ATTN_BLOCK)
            dqs, dks, dvs = [], [], []
            for hh in range(2):
                cs = slice(hh * HEAD_DIM, (hh + 1) * HEAD_DIM)
                q = q_ref[qrows, cs].astype(BF16)
                kk = k_ref[krows, cs].astype(BF16)
                vv = v_ref[krows, cs].astype(BF16)
                do = do_ref[qrows, cs]
                delta = jnp.sum(do * o_ref[qrows, cs], axis=-1, keepdims=True)
                lse = lse_ref[qrows, hh * HEAD_DIM:hh * HEAD_DIM + 1]
                bb = b_ref[hh, :, ATTN_BLOCK:] if first else b_ref[hh]
                p = jnp.exp(_attn_scores(q, kk, bb) - lse)
                dob = do.astype(BF16)
                dvs.append(_dot_tn(p.astype(BF16), dob))
                ds = p * (_dot_nt(dob, vv) - delta)
                if first:
                    db_ref[hh, :, ATTN_BLOCK:] += ds
                else:
                    db_ref[hh] += ds
                dsb = ds.astype(BF16)
                dqs.append(_dot(dsb, kk) * scale)
                dks.append(_dot_tn(dsb, q) * scale)
            dq_new = jnp.concatenate(dqs, axis=1)
            if has_acc:
                dq_new = dq_new + aq_ref[qrows, :]
            dq_ref[qrows, :] = dq_new
            dk_ref[krows, :] += jnp.concatenate(dks, axis=1)
            dv_ref[krows, :] += jnp.concatenate(dvs, axis=1)

        block(0, True)
        if nblk > 1:
            def loop_body(nq, carry):
                block(nq, False)
                return carry
            lax.fori_loop(1, nblk, loop_body, 0)

    def pspec(off):
        return pl.BlockSpec((None, ln, LANE), lambda p, b, r: (b, 0, r * cb + off // LANE + p))

    mspec = pl.BlockSpec((None, ln, LANE), lambda p, b, r: (b, 0, r * mb + a_blocks + p))
    ospec = pl.BlockSpec((None, ln, LANE), lambda p, b, r: (b, 0, r * ob + p))
    bspec = pl.BlockSpec((2, ATTN_BLOCK, 2 * ATTN_BLOCK), lambda p, b, r: (p, 0, 0))
    gshape = SDS((nb_local, ln, dil * B_WIDTH), F32)
    in_specs = [pspec(Q_OFF), pspec(K_OFF), pspec(V_OFF), mspec, mspec, ospec, bspec]
    args = [pv, pv, pv, mixv, dmixv, lsev, bias]
    aliases = {}
    if has_acc:
        in_specs += [ospec, ospec, ospec]
        aliases = {len(args) + t: t for t in range(3)}
        args += [a.reshape(nb_local, ln, dil * B_WIDTH) for a in acc]
    dq, dk, dv, db = pl.pallas_call(
        body, name=name, grid=(HEAD_PAIRS, nb_local, dil),
        in_specs=in_specs,
        out_specs=[ospec, ospec, ospec, bspec],
        out_shape=[gshape, gshape, gshape, SDS((B_HEADS, ATTN_BLOCK, 2 * ATTN_BLOCK), F32)],
        input_output_aliases=aliases,
        compiler_params=_params(("arbitrary", "arbitrary", "arbitrary")),
    )(*args)
    return (dq.reshape(n, B_WIDTH), dk.reshape(n, B_WIDTH), dv.reshape(n, B_WIDTH)), db


PAD = 8


def conv_gelu_fwd(gp, up, cw, cb, nb_local, name):
    n, f = gp.shape

    def body(gp_ref, up_ref, cw_ref, cb_ref, o_ref, pad_ref):
        pad_ref[0:PAD, :] = jnp.zeros((PAD, LANE), F32)
        pad_ref[PAD:PAD + SEQ, :] = gp_ref[...]
        c = (cb_ref[...] + cw_ref[0:1, :] * pad_ref[PAD - 2:PAD - 2 + SEQ, :]
             + cw_ref[1:2, :] * pad_ref[PAD - 1:PAD - 1 + SEQ, :] + cw_ref[2:3, :] * gp_ref[...])
        o_ref[...] = (_gelu(c) * up_ref[...]).astype(BF16)

    blk = pl.BlockSpec((SEQ, LANE), lambda b, j: (b, j))
    return pl.pallas_call(
        body, name=name, grid=(nb_local, f // LANE),
        in_specs=[blk, blk, pl.BlockSpec((3, LANE), lambda b, j: (0, j)), pl.BlockSpec((1, LANE), lambda b, j: (0, j))],
        out_specs=blk,
        out_shape=SDS((n, f), BF16),
        scratch_shapes=[pltpu.VMEM((SEQ + PAD, LANE), F32)],
        compiler_params=_params(("parallel", "parallel")),
    )(gp, up, cw, cb)


def conv_gelu_bwd(dgu, gp, up, cw, cb, nb_local, name):
    n, f = gp.shape

    def body(dgu_ref, gp_ref, up_ref, cw_ref, cb_ref, dgp_ref, dup_ref, dcw_ref, dcb_ref, pad_ref, dpad_ref):
        b = pl.program_id(1)
        pad_ref[0:PAD, :] = jnp.zeros((PAD, LANE), F32)
        pad_ref[PAD:PAD + SEQ, :] = gp_ref[...]
        g0 = gp_ref[...]
        g1 = pad_ref[PAD - 1:PAD - 1 + SEQ, :]
        g2 = pad_ref[PAD - 2:PAD - 2 + SEQ, :]
        c = cb_ref[...] + cw_ref[0:1, :] * g2 + cw_ref[1:2, :] * g1 + cw_ref[2:3, :] * g0
        gg, dgg = _gelu_and_grad(c)
        dgu = dgu_ref[...].astype(F32)
        dup_ref[...] = (dgu * gg).astype(BF16)
        dc = dgu * up_ref[...] * dgg
        dpad_ref[SEQ:SEQ + PAD, :] = jnp.zeros((PAD, LANE), F32)
        dpad_ref[0:SEQ, :] = dc
        dgp_ref[...] = (cw_ref[2:3, :] * dc + cw_ref[1:2, :] * dpad_ref[1:1 + SEQ, :]
                        + cw_ref[0:1, :] * dpad_ref[2:2 + SEQ, :]).astype(BF16)
        dcw = jnp.concatenate([jnp.sum(dc * g2, axis=0, keepdims=True), jnp.sum(dc * g1, axis=0, keepdims=True),
                               jnp.sum(dc * g0, axis=0, keepdims=True)], axis=0)
        dcb = jnp.sum(dc, axis=0, keepdims=True)

        @pl.when(b == 0)
        def _():
            dcw_ref[...] = dcw
            dcb_ref[...] = dcb

        @pl.when(b > 0)
        def _():
            dcw_ref[...] += dcw
            dcb_ref[...] += dcb

    blk = pl.BlockSpec((SEQ, LANE), lambda j, b: (b, j))
    return pl.pallas_call(
        body, name=name, grid=(f // LANE, nb_local),
        in_specs=[blk, blk, blk, pl.BlockSpec((3, LANE), lambda j, b: (0, j)), pl.BlockSpec((1, LANE), lambda j, b: (0, j))],
        out_specs=[blk, blk, pl.BlockSpec((3, LANE), lambda j, b: (0, j)), pl.BlockSpec((1, LANE), lambda j, b: (0, j))],
        out_shape=[SDS((n, f), BF16), SDS((n, f), BF16), SDS((3, f), F32), SDS((1, f), F32)],
        scratch_shapes=[pltpu.VMEM((SEQ + PAD, LANE), F32), pltpu.VMEM((SEQ + PAD, LANE), F32)],
        compiler_params=_params(("parallel", "arbitrary")),
    )(dgu, gp, up, cw, cb)


def norm_bwd_mid(dh2, x1, dout, z2, g3, g2, name, tm=256):
    n, d = x1.shape

    def body(dh_ref, x1_ref, dout_ref, z2_ref, g3_ref, g2_ref, dx1_ref, dz2_ref, dg3_ref, dg2_ref):
        i = pl.program_id(0)
        dxa, dg3r = _rms_bwd(dh_ref[...], x1_ref[...], g3_ref[...])
        dx1 = dout_ref[...] + dxa
        dx1_ref[...] = dx1
        dz2, dg2r = _rms_bwd(dx1, z2_ref[...], g2_ref[...])
        dz2_ref[...] = dz2.astype(BF16)
        s3 = jnp.sum(dg3r, axis=0, keepdims=True)
        s2 = jnp.sum(dg2r, axis=0, keepdims=True)

        @pl.when(i == 0)
        def _():
            dg3_ref[...] = s3
            dg2_ref[...] = s2

        @pl.when(i > 0)
        def _():
            dg3_ref[...] += s3
            dg2_ref[...] += s2

    row = pl.BlockSpec((tm, d), lambda i: (i, 0))
    vec = pl.BlockSpec((1, d), lambda i: (0, 0))
    return pl.pallas_call(
        body, name=name, grid=(n // tm,),
        in_specs=[row, row, row, row, vec, vec],
        out_specs=[row, row, vec, vec],
        out_shape=[SDS((n, d), F32), SDS((n, d), BF16), SDS((1, d), F32), SDS((1, d), F32)],
        compiler_params=_params(("arbitrary",)),
    )(dh2, x1, dout, z2, g3, g2)


def norm_bwd_in(dh1, x, dx1, g1, name, tm=256):
    n, d = x.shape

    def body(dh_ref, x_ref, dx1_ref, g1_ref, dx_ref, dg1_ref):
        i = pl.program_id(0)
        dxa, dgr = _rms_bwd(dh_ref[...], x_ref[...], g1_ref[...])
        dx_ref[...] = dx1_ref[...] + dxa
        s = jnp.sum(dgr, axis=0, keepdims=True)

        @pl.when(i == 0)
        def _():
            dg1_ref[...] = s

        @pl.when(i > 0)
        def _():
            dg1_ref[...] += s

    row = pl.BlockSpec((tm, d), lambda i: (i, 0))
    vec = pl.BlockSpec((1, d), lambda i: (0, 0))
    return pl.pallas_call(
        body, name=name, grid=(n // tm,),
        in_specs=[row, row, row, vec],
        out_specs=[row, vec],
        out_shape=[SDS((n, d), F32), SDS((1, d), F32)],
        compiler_params=_params(("arbitrary",)),
    )(dh1, x, dx1, g1)


def local_step(x, target, norm_mix_pre, norm_mix_post, norm_ffn_pre, norm_ffn_post, w_in, ln_g, ln_b,
               spatial_w, spatial_bt, rel_bias_t, w_out, w_gate, w_up, conv_w, conv_b, w_down):
    n = x.shape[0]
    nb_local = n // SEQ

    h1, proj = norm_mm(x, norm_mix_pre, [w_in], "fwd_norm_in")
    a = gating_fwd(proj, ln_g, ln_b, spatial_w, spatial_bt, "fwd_gating")
    bias = bias_tables(rel_bias_t, "bias_tables").reshape(len(DILATIONS), B_HEADS, ATTN_BLOCK, 2 * ATTN_BLOCK)
    os_, lses = [], []
    for s, dil in enumerate(DILATIONS):
        o, lse = attn_fwd(proj, bias[s], dil, nb_local, f"fwd_attn_d{dil}")
        os_.append(o)
        lses.append(lse)
    mix, lse_tot = attn_merge(a, os_, lses, "fwd_attn_merge")
    z2, x1 = mm_res_norm(mix, w_out, x, norm_mix_post, "fwd_out_norm")
    h2, gp, up = norm_mm(x1, norm_ffn_pre, [w_gate, w_up], "fwd_norm_ffn", tm=256)
    gu = conv_gelu_fwd(gp, up, conv_w, conv_b, nb_local, "fwd_conv_gelu")
    dy, dout, dg4, loss = down_loss(gu, w_down, x1, norm_ffn_post, target, "fwd_down_loss")

    g_w_down = mm_tn(gu, [dy], "bwd_dw_down", t1=1408)
    dgu = mm_nt([(dy, 0, 0)], [w_down], "bwd_dgu", out_dtype=BF16)
    dgp, dup, d_conv_w, d_conv_b = conv_gelu_bwd(dgu, gp, up, conv_w, conv_b, nb_local, "bwd_conv_gelu")
    g_w_gate = mm_tn(h2, [dgp], "bwd_dw_gate")
    g_w_up = mm_tn(h2, [dup], "bwd_dw_up")
    dh2 = mm_nt([(dgp, 0, 0), (dup, 1, 0)], [w_gate, w_up], "bwd_dh2", tm=256)
    dx1, dz2, dg3, dg2 = norm_bwd_mid(dh2, x1, dout, z2, norm_ffn_pre, norm_mix_post, "bwd_norm_mid")
    g_w_out = mm_tn(mix, [dz2], "bwd_dw_out")
    dmix = mm_nt([(dz2, 0, 0)], [w_out], "bwd_dmix")
    duv, d_ws, d_sbt, d_lng, d_lnb = gating_bwd(proj, dmix, ln_g, ln_b, spatial_w, spatial_bt, "bwd_gating")
    acc, dbs = None, []
    for s, dil in enumerate(DILATIONS):
        acc, db = attn_bwd(proj, mix, dmix, lse_tot, bias[s], dil, nb_local, acc, f"bwd_attn_d{dil}")
        dbs.append(db.reshape(B_HEADS, BIAS_SIZE))
    d_rel_bias_t = rel_bias_grad(jnp.stack(dbs), "bwd_rel_bias")
    dq, dk, dv = acc
    g_w_in = mm_tn(h1, [duv, dq, dk, dv], "bwd_dw_in")
    dh1 = mm_nt([(duv, 0, 0), (dq, 0, Q_OFF), (dk, 0, K_OFF), (dv, 0, V_OFF)], [w_in], "bwd_dh1", tm=256)
    grad_x, dg1 = norm_bwd_in(dh1, x, dx1, norm_mix_pre, "bwd_norm_in")

    return dict(loss=loss, grad_x=grad_x, norm_mix_pre=dg1, norm_mix_post=dg2, norm_ffn_pre=dg3, norm_ffn_post=dg4,
                w_in=g_w_in, ln_v_gain=d_lng, ln_v_bias=d_lnb, spatial_w=d_ws, spatial_bt=d_sbt,
                rel_bias_t=d_rel_bias_t, w_out=g_w_out, w_gate=g_w_gate, w_up=g_w_up, conv_w=d_conv_w,
                conv_b=d_conv_b, w_down=g_w_down)


MESH = pl.DeviceIdType.MESH
ANY = pl.BlockSpec(memory_space=pl.ANY)
PEER_MASKS = tuple(range(1, N_DEV))


def _my_index():
    return lax.axis_index("x") * 4 + lax.axis_index("y") * 2 + lax.axis_index("c")


def _peer(mask):
    x, y, c = lax.axis_index("x"), lax.axis_index("y"), lax.axis_index("c")
    px = 1 - x if mask & 4 else x
    py = 1 - y if mask & 2 else y
    pc = 1 - c if mask & 1 else c
    return (px, py, pc), px * 4 + py * 2 + pc


def cast_bf16(arrays, name):
    def body(*refs):
        for i_ref, o_ref in zip(refs[:len(arrays)], refs[len(arrays):]):
            o_ref[...] = i_ref[...].astype(BF16)

    return pl.pallas_call(body, name=name, out_shape=[SDS(a.shape, BF16) for a in arrays],
                          compiler_params=_params())(*arrays)


def all_gather(shards, name):
    ns = len(shards)

    def body(*refs):
        in_refs, out_refs = refs[:ns], refs[ns:2 * ns]
        send_sems, recv_sems, local_sems = refs[2 * ns:]
        me = _my_index()
        local, sends, recvs = [], [], []
        for a in range(ns):
            cp = pltpu.make_async_copy(in_refs[a], out_refs[a].at[me], local_sems.at[a])
            cp.start()
            local.append(cp)
            for mask in PEER_MASKS:
                peer, pidx = _peer(mask)
                cp = pltpu.make_async_remote_copy(
                    src_ref=in_refs[a], dst_ref=out_refs[a].at[me], send_sem=send_sems.at[a, mask - 1],
                    recv_sem=recv_sems.at[a, mask - 1], device_id=peer, device_id_type=MESH)
                cp.start()
                sends.append(cp)
                recvs.append(pltpu.make_async_remote_copy(
                    src_ref=in_refs[a], dst_ref=out_refs[a].at[pidx], send_sem=send_sems.at[a, mask - 1],
                    recv_sem=recv_sems.at[a, mask - 1], device_id=peer, device_id_type=MESH))
        for cp in sends:
            cp.wait_send()
        for cp in recvs:
            cp.wait_recv()
        for cp in local:
            cp.wait()

    return pl.pallas_call(
        body, name=name, in_specs=[ANY] * ns, out_specs=[ANY] * ns,
        out_shape=[SDS((N_DEV,) + s.shape, s.dtype) for s in shards],
        scratch_shapes=[pltpu.SemaphoreType.DMA((ns, N_DEV - 1)), pltpu.SemaphoreType.DMA((ns, N_DEV - 1)),
                        pltpu.SemaphoreType.DMA((ns,))],
    )(*shards)


def exchange_partials(blocked, whole, name):
    nb, nw = len(blocked), len(whole)
    ns = nb + nw

    def body(*refs):
        in_refs, out_refs = refs[:ns], refs[ns:2 * ns]
        send_sems, recv_sems, local_sems = refs[2 * ns:]
        me = _my_index()
        local, sends, recvs = [], [], []
        for a in range(ns):
            src_of = (lambda idx, a=a: in_refs[a].at[idx]) if a < nb else (lambda idx, a=a: in_refs[a])
            cp = pltpu.make_async_copy(src_of(me), out_refs[a].at[me], local_sems.at[a])
            cp.start()
            local.append(cp)
            for mask in PEER_MASKS:
                peer, pidx = _peer(mask)
                cp = pltpu.make_async_remote_copy(
                    src_ref=src_of(pidx), dst_ref=out_refs[a].at[me], send_sem=send_sems.at[a, mask - 1],
                    recv_sem=recv_sems.at[a, mask - 1], device_id=peer, device_id_type=MESH)
                cp.start()
                sends.append(cp)
                recvs.append(pltpu.make_async_remote_copy(
                    src_ref=src_of(pidx), dst_ref=out_refs[a].at[pidx], send_sem=send_sems.at[a, mask - 1],
                    recv_sem=recv_sems.at[a, mask - 1], device_id=peer, device_id_type=MESH))
        for cp in sends:
            cp.wait_send()
        for cp in recvs:
            cp.wait_recv()
        for cp in local:
            cp.wait()

    return pl.pallas_call(
        body, name=name, in_specs=[ANY] * ns, out_specs=[ANY] * ns,
        out_shape=[SDS(b.shape, b.dtype) for b in blocked] + [SDS((N_DEV,) + w.shape, w.dtype) for w in whole],
        scratch_shapes=[pltpu.SemaphoreType.DMA((ns, N_DEV - 1)), pltpu.SemaphoreType.DMA((ns, N_DEV - 1)),
                        pltpu.SemaphoreType.DMA((ns,))],
    )(*blocked, *whole)


def adam_update(parts, w, m, v, name, tr=None):
    s, r, c = parts.shape
    tr = r if tr is None else tr
    bc1 = 1.0 - ADAM_B1 ** ADAM_STEP
    bc2 = 1.0 - ADAM_B2 ** ADAM_STEP

    def body(p_ref, w_ref, m_ref, v_ref, g_ref, d_ref, nm_ref, nv_ref):
        g = p_ref[0].astype(F32)
        for j in range(1, s):
            g = g + p_ref[j].astype(F32)
        nm = ADAM_B1 * m_ref[...] + (1.0 - ADAM_B1) * g
        nv = ADAM_B2 * v_ref[...] + (1.0 - ADAM_B2) * (g * g)
        g_ref[...] = g
        nm_ref[...] = nm
        nv_ref[...] = nv
        d_ref[...] = -ADAM_LR * ((nm / bc1) / (jnp.sqrt(nv / bc2) + ADAM_EPS) + ADAM_WD * w_ref[...])

    blk = pl.BlockSpec((tr, c), lambda i: (i, 0))
    return pl.pallas_call(
        body, name=name, grid=(r // tr,),
        in_specs=[pl.BlockSpec((s, tr, c), lambda i: (0, i, 0)), blk, blk, blk],
        out_specs=[blk] * 4, out_shape=[SDS((r, c), F32)] * 4,
        compiler_params=_params(("parallel",)),
    )(parts, w, m, v)


SMALL_NAMES = ("spatial_w", "norm_mix_pre", "norm_mix_post", "norm_ffn_pre", "norm_ffn_post", "conv_b",
               "ln_v_gain", "ln_v_bias", "spatial_b", "rel_bias")
PACK_ROW_ALIGN = 8


def _pack_rows(size):
    rows = -(-size // LANE)
    return -(-rows // PACK_ROW_ALIGN) * PACK_ROW_ALIGN


def _pack(arrays):
    flat = []
    for a in arrays:
        rows = _pack_rows(a.size)
        flat.append(jnp.pad(a.reshape(-1), (0, rows * LANE - a.size)))
    return jnp.concatenate(flat).reshape(-1, LANE)


def _unpack(packed, shapes):
    out, row = [], 0
    for shp in shapes:
        size = int(np.prod(shp))
        out.append(packed[row:row + _pack_rows(size)].reshape(-1)[:size].reshape(shp))
        row += _pack_rows(size)
    return out


def kernel(x, norm_mix_pre, norm_mix_post, norm_ffn_pre, norm_ffn_post, w_in, ln_v_gain, ln_v_bias, spatial_w, spatial_b, rel_bias, w_out, w_gate, w_up, conv_w, conv_b, w_down, loss_target, m_norm_mix_pre, m_norm_mix_post, m_norm_ffn_pre, m_norm_ffn_post, m_w_in, m_ln_v_gain, m_ln_v_bias, m_spatial_w, m_spatial_b, m_rel_bias, m_w_out, m_w_gate, m_w_up, m_conv_w, m_conv_b, m_w_down, v_norm_mix_pre, v_norm_mix_post, v_norm_ffn_pre, v_norm_ffn_post, v_w_in, v_ln_v_gain, v_ln_v_bias, v_spatial_w, v_spatial_b, v_rel_bias, v_w_out, v_w_gate, v_w_up, v_conv_w, v_conv_b, v_w_down):
    given = dict(locals())
    nb_local, seq, d = x.shape
    n = nb_local * seq
    cols = w_in.shape[2]

    shards = cast_bf16([w_in[0], w_out[0], w_gate[0], w_up[0], w_down[0]], "cast_shards")
    g_in, g_out, g_gate, g_up, g_down, g_cw = all_gather(list(shards) + [conv_w[0]], "gather_weights")

    def by_columns(g):
        return g.transpose(1, 0, 2).reshape(g.shape[1], N_DEV * g.shape[2])

    part = local_step(
        x.reshape(n, d), loss_target.reshape(n, d), norm_mix_pre, norm_mix_post, norm_ffn_pre, norm_ffn_post,
        by_columns(g_in), ln_v_gain.reshape(1, A_WIDTH), ln_v_bias.reshape(1, A_WIDTH), spatial_w[0], spatial_b[0].T,
        rel_bias.T, g_out.reshape(D_MODEL, D_MODEL), by_columns(g_gate), by_columns(g_up), by_columns(g_cw), conv_b,
        g_down.reshape(D_FF, D_MODEL))

    def to_blocks(g):
        return g.reshape(g.shape[0], N_DEV, cols).transpose(1, 0, 2)

    small = dict(part)
    small["spatial_b"] = part["spatial_bt"].T
    small["rel_bias"] = part["rel_bias_t"].T
    pack = _pack([small[k] for k in SMALL_NAMES] + [part["conv_w"], part["loss"]])
    r_in, r_out, r_gate, r_up, r_down, r_small = exchange_partials(
        [to_blocks(part["w_in"]), part["w_out"].reshape(N_DEV, D_MODEL // N_DEV, D_MODEL), to_blocks(part["w_gate"]),
         to_blocks(part["w_up"]), part["w_down"].reshape(N_DEV, cols, D_MODEL)], [pack], "exchange_grads")

    res = {}
    res["w_in"] = adam_update(r_in, w_in[0], m_w_in[0], v_w_in[0], "adam_w_in", tr=256)
    res["w_out"] = adam_update(r_out, w_out[0], m_w_out[0], v_w_out[0], "adam_w_out")
    res["w_gate"] = adam_update(r_gate, w_gate[0], m_w_gate[0], v_w_gate[0], "adam_w_gate", tr=256)
    res["w_up"] = adam_update(r_up, w_up[0], m_w_up[0], v_w_up[0], "adam_w_up", tr=256)
    res["w_down"] = adam_update(r_down, w_down[0], m_w_down[0], v_w_down[0], "adam_w_down", tr=176)

    tail = [jnp.zeros_like(part["conv_w"]), jnp.zeros_like(part["loss"])]
    packs = [_pack([given[pre + k] for k in SMALL_NAMES] + tail) for pre in ("", "m_", "v_")]
    small_res = adam_update(r_small, *packs, "adam_small")
    small_shapes = [given[k].shape for k in SMALL_NAMES] + [part["conv_w"].shape, part["loss"].shape]
    unpacked = [_unpack(p, small_shapes) for p in small_res]
    for i, k in enumerate(SMALL_NAMES):
        res[k] = [u[i] for u in unpacked]
    g_conv_w = lax.dynamic_slice_in_dim(unpacked[0][len(SMALL_NAMES)], _my_index() * cols, cols, axis=1)
    res["conv_w"] = adam_update(g_conv_w[None], conv_w[0], m_conv_w[0], v_conv_w[0], "adam_conv_w")
    loss = unpacked[0][len(SMALL_NAMES) + 1][0, 0]

    names = ("norm_mix_pre", "norm_mix_post", "norm_ffn_pre", "norm_ffn_post", "w_in", "ln_v_gain", "ln_v_bias",
             "spatial_w", "spatial_b", "rel_bias", "w_out", "w_gate", "w_up", "conv_w", "conv_b", "w_down")
    outs = [loss, part["grad_x"].reshape(x.shape)]
    for t in range(4):
        outs += [res[k][t].reshape(given[k].shape) for k in names]
    return tuple(outs)
```

```python
import functools
import math

import numpy as np
import jax
import jax.numpy as jnp
from jax import lax
from jax.experimental import pallas as pl
from jax.experimental.pallas import tpu as pltpu

F32 = jnp.float32
BF16 = jnp.bfloat16
SDS = jax.ShapeDtypeStruct

D_MODEL = 1024
SEQ = 2048
HEAD_DIM = 64
A_GROUPS = 4
A_WIDTH = A_GROUPS * HEAD_DIM
B_HEADS = 12
B_WIDTH = B_HEADS * HEAD_DIM
HEAD_PAIRS = B_HEADS // 2
CHUNK = 128
ATTN_BLOCK = 128
DILATIONS = (1, 4, 16)
NUM_BUCKETS = 32
MAX_DISTANCE = 2048
D_FF = 2816
IN_COLS = 2 * A_WIDTH + 3 * B_WIDTH
Q_OFF = 2 * A_WIDTH
K_OFF = Q_OFF + B_WIDTH
V_OFF = K_OFF + B_WIDTH
NORM_EPS = 1e-6
NEG_INF = -1e30
N_DEV = 8
LANE = 128

ADAM_LR = 0.001
ADAM_B1 = 0.9
ADAM_B2 = 0.999
ADAM_EPS = 1e-08
ADAM_WD = 0.01
ADAM_STEP = 10

GELU_C0 = math.sqrt(2.0 / math.pi)
GELU_C1 = 0.044715

VMEM_LIMIT = 56 * 1024 * 1024


def _params(sem=None):
    if sem is None:
        return pltpu.CompilerParams(vmem_limit_bytes=VMEM_LIMIT)
    return pltpu.CompilerParams(dimension_semantics=sem, vmem_limit_bytes=VMEM_LIMIT)


def _gelu(x):
    t = jnp.tanh(GELU_C0 * (x + GELU_C1 * x * x * x))
    return 0.5 * x * (1.0 + t)


def _gelu_and_grad(x):
    x2 = x * x
    t = jnp.tanh(GELU_C0 * (x + GELU_C1 * x * x2))
    g = 0.5 * x * (1.0 + t)
    dg = 0.5 * (1.0 + t) + 0.5 * x * (1.0 - t * t) * (GELU_C0 * (1.0 + 3.0 * GELU_C1 * x2))
    return g, dg


def _dot(a, b):
    return jnp.dot(a, b, preferred_element_type=F32)


def _dot_nt(a, b):
    return lax.dot_general(a, b, (((1,), (1,)), ((), ())), preferred_element_type=F32)


def _dot_tn(a, b):
    return lax.dot_general(a, b, (((0,), (0,)), ((), ())), preferred_element_type=F32)


def _rms_bwd(d, xin, g):
    r = lax.rsqrt(jnp.mean(xin * xin, axis=-1, keepdims=True) + NORM_EPS)
    xh = xin * r
    gd = g * d
    dx = r * (gd - xh * jnp.mean(gd * xh, axis=-1, keepdims=True))
    return dx, d * xh


def norm_mm(x, g, ws, name, tm=512, tn=1408):
    n, d = x.shape
    f = ws[0].shape[1]
    nw = len(ws)

    def body(x_ref, g_ref, *refs):
        w_refs = refs[:nw]
        h_ref = refs[nw]
        o_refs = refs[nw + 1:]

        @pl.when(pl.program_id(1) == 0)
        def _():
            xv = x_ref[...]
            r = lax.rsqrt(jnp.mean(xv * xv, axis=-1, keepdims=True) + NORM_EPS)
            h_ref[...] = (xv * r * g_ref[...]).astype(BF16)

        h = h_ref[...]
        for w_ref, o_ref in zip(w_refs, o_refs):
            o_ref[...] = _dot(h, w_ref[...])

    return pl.pallas_call(
        body, name=name, grid=(n // tm, f // tn),
        in_specs=[pl.BlockSpec((tm, d), lambda i, j: (i, 0)), pl.BlockSpec((1, d), lambda i, j: (0, 0))]
        + [pl.BlockSpec((d, tn), lambda i, j: (0, j)) for _ in ws],
        out_specs=[pl.BlockSpec((tm, d), lambda i, j: (i, 0))]
        + [pl.BlockSpec((tm, tn), lambda i, j: (i, j)) for _ in ws],
        out_shape=[SDS((n, d), BF16)] + [SDS((n, f), F32) for _ in ws],
        compiler_params=_params(("parallel", "arbitrary")),
    )(x, g, *ws)


def _lane_concat(refs):
    vals = [r[...].astype(BF16) for r in refs]
    return vals[0] if len(vals) == 1 else jnp.concatenate(vals, axis=1)


def mm_res_norm(a_list, w, res, g, name, tm=512):
    n = a_list[0].shape[0]
    k, d = w.shape
    na = len(a_list)

    def body(*refs):
        w_ref, res_ref, g_ref, y_ref, o_ref = refs[na:]
        y = _dot(_lane_concat(refs[:na]), w_ref[...])
        r = lax.rsqrt(jnp.mean(y * y, axis=-1, keepdims=True) + NORM_EPS)
        y_ref[...] = y
        o_ref[...] = res_ref[...] + y * r * g_ref[...]

    return pl.pallas_call(
        body, name=name, grid=(n // tm,),
        in_specs=[pl.BlockSpec((tm, a.shape[1]), lambda i: (i, 0)) for a in a_list]
        + [pl.BlockSpec((k, d), lambda i: (0, 0)),
           pl.BlockSpec((tm, d), lambda i: (i, 0)), pl.BlockSpec((1, d), lambda i: (0, 0))],
        out_specs=[pl.BlockSpec((tm, d), lambda i: (i, 0)), pl.BlockSpec((tm, d), lambda i: (i, 0))],
        out_shape=[SDS((n, d), F32), SDS((n, d), F32)],
        compiler_params=_params(("parallel",)),
    )(*a_list, w, res, g)


def down_loss(a, w, res, g, target, name, tm=256):
    n, k = a.shape
    d = w.shape[1]
    inv_d = 1.0 / d

    def body(a_ref, w_ref, res_ref, g_ref, t_ref, dy_ref, dout_ref, dg_ref, loss_ref):
        i = pl.program_id(0)
        y = _dot(a_ref[...], w_ref[...])
        gv = g_ref[...]
        r = lax.rsqrt(jnp.mean(y * y, axis=-1, keepdims=True) + NORM_EPS)
        yh = y * r
        e = res_ref[...] + yh * gv - t_ref[...]
        part = 0.5 * inv_d * jnp.sum(jnp.sum(e * e, axis=-1, keepdims=True), axis=0, keepdims=True)
        dout = e * inv_d
        dout_ref[...] = dout
        gd = gv * dout
        dy_ref[...] = (r * (gd - yh * jnp.mean(gd * yh, axis=-1, keepdims=True))).astype(BF16)
        dgp = jnp.sum(dout * yh, axis=0, keepdims=True)
        lane0 = lax.broadcasted_iota(jnp.int32, (1, LANE), 1) == 0
        lp = jnp.where(lane0, part, 0.0)

        @pl.when(i == 0)
        def _():
            dg_ref[...] = dgp
            loss_ref[...] = lp

        @pl.when(i > 0)
        def _():
            dg_ref[...] += dgp
            loss_ref[...] += lp

    return pl.pallas_call(
        body, name=name, grid=(n // tm,),
        in_specs=[pl.BlockSpec((tm, k), lambda i: (i, 0)), pl.BlockSpec((k, d), lambda i: (0, 0)),
                  pl.BlockSpec((tm, d), lambda i: (i, 0)), pl.BlockSpec((1, d), lambda i: (0, 0)),
                  pl.BlockSpec((tm, d), lambda i: (i, 0))],
        out_specs=[pl.BlockSpec((tm, d), lambda i: (i, 0)), pl.BlockSpec((tm, d), lambda i: (i, 0)),
                   pl.BlockSpec((1, d), lambda i: (0, 0)), pl.BlockSpec((1, LANE), lambda i: (0, 0))],
        out_shape=[SDS((n, d), BF16), SDS((n, d), F32), SDS((1, d), F32), SDS((1, LANE), F32)],
        compiler_params=_params(("arbitrary",)),
    )(a, w, res, g, target)


def mm_nt(terms, ws, name, tm=512, out_dtype=F32):
    n = terms[0][0].shape[0]
    r = ws[0].shape[0]
    na = len(terms)
    meta = [(widx, off, a.shape[1]) for a, widx, off in terms]

    def body(*refs):
        a_refs = refs[:na]
        w_refs = refs[na:na + len(ws)]
        o_ref = refs[-1]
        acc = None
        for a_ref, (widx, off, k) in zip(a_refs, meta):
            p = _dot_nt(a_ref[...].astype(BF16), w_refs[widx][:, off:off + k])
            acc = p if acc is None else acc + p
        o_ref[...] = acc.astype(out_dtype)

    return pl.pallas_call(
        body, name=name, grid=(n // tm,),
        in_specs=[pl.BlockSpec((tm, a.shape[1]), lambda i: (i, 0)) for a, _, _ in terms]
        + [pl.BlockSpec(w.shape, lambda i: (0, 0)) for w in ws],
        out_specs=pl.BlockSpec((tm, r), lambda i: (i, 0)),
        out_shape=SDS((n, r), out_dtype),
        compiler_params=_params(("parallel",)),
    )(*[a for a, _, _ in terms], *ws)


def mm_tn(lhs_list, rhs_list, name, t1=512, tn=512, out_dtype=BF16):
    n = lhs_list[0].shape[0]
    k1 = sum(l.shape[1] for l in lhs_list)
    nl = len(lhs_list)
    if nl > 1:
        t1 = k1
    widths = [r.shape[1] for r in rhs_list]
    k2 = sum(widths)
    nr = len(rhs_list)
    nk = n // tn

    def body(*refs):
        r_refs = refs[nl:nl + nr]
        o_ref = refs[nl + nr]
        acc = refs[nl + nr + 1]
        k = pl.program_id(1)
        a = _lane_concat(refs[:nl])
        parts = [_dot_tn(a, r_ref[...].astype(BF16)) for r_ref in r_refs]
        val = parts[0] if nr == 1 else jnp.concatenate(parts, axis=1)

        @pl.when(k == 0)
        def _():
            acc[...] = val

        @pl.when(k > 0)
        def _():
            acc[...] += val

        @pl.when(k == nk - 1)
        def _():
            o_ref[...] = acc[...].astype(out_dtype)

    if nl == 1:
        lhs_specs = [pl.BlockSpec((tn, t1), lambda i, k: (k, i))]
    else:
        lhs_specs = [pl.BlockSpec((tn, l.shape[1]), lambda i, k: (k, 0)) for l in lhs_list]
    return pl.pallas_call(
        body, name=name, grid=(k1 // t1, nk),
        in_specs=lhs_specs + [pl.BlockSpec((tn, w), lambda i, k: (k, 0)) for w in widths],
        out_specs=pl.BlockSpec((t1, k2), lambda i, k: (i, 0)),
        out_shape=SDS((k1, k2), out_dtype),
        scratch_shapes=[pltpu.VMEM((t1, k2), F32)],
        compiler_params=_params(("parallel", "arbitrary")),
    )(*lhs_list, *rhs_list)


GATE_ROWS = 512


def _tril_mask():
    row = lax.broadcasted_iota(jnp.int32, (CHUNK, CHUNK), 0)
    col = lax.broadcasted_iota(jnp.int32, (CHUNK, CHUNK), 1)
    return row >= col


def _layer_norm_parts(gv):
    mu = jnp.mean(gv, axis=-1, keepdims=True)
    xc = gv - mu
    rstd = lax.rsqrt(jnp.mean(xc * xc, axis=-1, keepdims=True) + NORM_EPS)
    return xc * rstd, rstd


def gating_fwd(proj, lng, lnb, ws, sbt, name):
    n = proj.shape[0]
    nchunk = GATE_ROWS // CHUNK

    def body(u_ref, v_ref, lng_ref, lnb_ref, ws_ref, sbt_ref, a_ref):
        tril = _tril_mask()
        for g in range(A_GROUPS):
            cs = slice(g * HEAD_DIM, (g + 1) * HEAD_DIM)
            wt = jnp.where(tril, ws_ref[g], 0.0).astype(BF16)
            for c in range(nchunk):
                rs_ = slice(c * CHUNK, (c + 1) * CHUNK)
                vhat, _ = _layer_norm_parts(_gelu(v_ref[rs_, cs]))
                vn = vhat * lng_ref[:, cs] + lnb_ref[:, cs]
                z = _dot(wt, vn.astype(BF16)) + sbt_ref[:, g:g + 1]
                a_ref[rs_, cs] = _gelu(u_ref[rs_, cs]) * z

    return pl.pallas_call(
        body, name=name, grid=(n // GATE_ROWS,),
        in_specs=[pl.BlockSpec((GATE_ROWS, A_WIDTH), lambda i: (i, 0)),
                  pl.BlockSpec((GATE_ROWS, A_WIDTH), lambda i: (i, 1)),
                  pl.BlockSpec((1, A_WIDTH), lambda i: (0, 0)), pl.BlockSpec((1, A_WIDTH), lambda i: (0, 0)),
                  pl.BlockSpec((A_GROUPS, CHUNK, CHUNK), lambda i: (0, 0, 0)),
                  pl.BlockSpec((CHUNK, A_GROUPS), lambda i: (0, 0))],
        out_specs=pl.BlockSpec((GATE_ROWS, A_WIDTH), lambda i: (i, 0)),
        out_shape=SDS((n, A_WIDTH), F32),
        compiler_params=_params(("parallel",)),
    )(proj, proj, lng, lnb, ws, sbt)


def gating_bwd(proj, dmix, lng, lnb, ws, sbt, name):
    n = proj.shape[0]
    nchunk = GATE_ROWS // CHUNK

    def body(u_ref, v_ref, da_ref, lng_ref, lnb_ref, ws_ref, sbt_ref,
             duv_ref, dws_ref, dsbt_ref, dlng_ref, dlnb_ref):
        @pl.when(pl.program_id(0) == 0)
        def _():
            dws_ref[...] = jnp.zeros_like(dws_ref)
            dsbt_ref[...] = jnp.zeros_like(dsbt_ref)
            dlng_ref[...] = jnp.zeros_like(dlng_ref)
            dlnb_ref[...] = jnp.zeros_like(dlnb_ref)

        tril = _tril_mask()
        for g in range(A_GROUPS):
            cs = slice(g * HEAD_DIM, (g + 1) * HEAD_DIM)
            wt = jnp.where(tril, ws_ref[g], 0.0).astype(BF16)
            lg = lng_ref[:, cs]
            dw = jnp.zeros((CHUNK, CHUNK), F32)
            dsb = jnp.zeros((CHUNK, 1), F32)
            dlg = jnp.zeros((1, HEAD_DIM), F32)
            dlb = jnp.zeros((1, HEAD_DIM), F32)
            for c in range(nchunk):
                rs_ = slice(c * CHUNK, (c + 1) * CHUNK)
                gu, dgu_dx = _gelu_and_grad(u_ref[rs_, cs])
                gv, dgv_dx = _gelu_and_grad(v_ref[rs_, cs])
                vhat, rstd = _layer_norm_parts(gv)
                vn = (vhat * lg + lnb_ref[:, cs]).astype(BF16)
                z = _dot(wt, vn) + sbt_ref[:, g:g + 1]
                da = da_ref[rs_, cs]
                dz = da * gu
                dzb = dz.astype(BF16)
                duv_ref[rs_, cs] = da * z * dgu_dx
                dsb = dsb + jnp.sum(dz, axis=-1, keepdims=True)
                dw = dw + _dot_nt(dzb, vn)
                dvn = _dot_tn(wt, dzb)
                dlg = dlg + jnp.sum(dvn * vhat, axis=0, keepdims=True)
                dlb = dlb + jnp.sum(dvn, axis=0, keepdims=True)
                dvh = dvn * lg
                dgv = rstd * (dvh - jnp.mean(dvh, axis=-1, keepdims=True)
                              - vhat * jnp.mean(dvh * vhat, axis=-1, keepdims=True))
                duv_ref[rs_, A_WIDTH + g * HEAD_DIM:A_WIDTH + (g + 1) * HEAD_DIM] = dgv * dgv_dx
            dws_ref[g] += jnp.where(tril, dw, 0.0)
            dsbt_ref[:, g:g + 1] += dsb
            dlng_ref[:, cs] += dlg
            dlnb_ref[:, cs] += dlb

    return pl.pallas_call(
        body, name=name, grid=(n // GATE_ROWS,),
        in_specs=[pl.BlockSpec((GATE_ROWS, A_WIDTH), lambda i: (i, 0)),
                  pl.BlockSpec((GATE_ROWS, A_WIDTH), lambda i: (i, 1)),
                  pl.BlockSpec((GATE_ROWS, A_WIDTH), lambda i: (i, 0)),
                  pl.BlockSpec((1, A_WIDTH), lambda i: (0, 0)), pl.BlockSpec((1, A_WIDTH), lambda i: (0, 0)),
                  pl.BlockSpec((A_GROUPS, CHUNK, CHUNK), lambda i: (0, 0, 0)),
                  pl.BlockSpec((CHUNK, A_GROUPS), lambda i: (0, 0))],
        out_specs=[pl.BlockSpec((GATE_ROWS, 2 * A_WIDTH), lambda i: (i, 0)),
                   pl.BlockSpec((A_GROUPS, CHUNK, CHUNK), lambda i: (0, 0, 0)),
                   pl.BlockSpec((CHUNK, A_GROUPS), lambda i: (0, 0)),
                   pl.BlockSpec((1, A_WIDTH), lambda i: (0, 0)), pl.BlockSpec((1, A_WIDTH), lambda i: (0, 0))],
        out_shape=[SDS((n, 2 * A_WIDTH), F32), SDS((A_GROUPS, CHUNK, CHUNK), F32), SDS((CHUNK, A_GROUPS), F32),
                   SDS((1, A_WIDTH), F32), SDS((1, A_WIDTH), F32)],
        compiler_params=_params(("arbitrary",)),
    )(proj, proj, dmix, lng, lnb, ws, sbt)


def _t5_bucket_np(dist):
    max_exact = NUM_BUCKETS // 2
    dd = np.maximum(dist, 1).astype(np.float64)
    large = max_exact + np.log(dd / max_exact) / math.log(MAX_DISTANCE / max_exact) * (NUM_BUCKETS - max_exact)
    large = np.minimum(large.astype(np.int64), NUM_BUCKETS - 1)
    return np.where(dist < max_exact, dist, large)


def _bucket_tables():
    i = np.arange(ATTN_BLOCK)[:, None]
    j = np.arange(2 * ATTN_BLOCK)[None, :]
    rel = ATTN_BLOCK + i - j
    band = (rel >= 0) & (rel <= ATTN_BLOCK)
    tabs = []
    for dil in DILATIONS:
        b = _t5_bucket_np(np.maximum(rel, 0) * dil)
        tabs.append(np.where(band, b, -1).reshape(1, -1))
    return np.stack(tabs).astype(np.float32)


BIAS_SIZE = ATTN_BLOCK * 2 * ATTN_BLOCK


def bias_tables(rel_bias_t, name):
    idx = jnp.asarray(_bucket_tables())

    def body(rb_ref, idx_ref, o_ref):
        iv = idx_ref[0]
        bk = lax.broadcasted_iota(jnp.int32, (NUM_BUCKETS, BIAS_SIZE), 0).astype(F32)
        onehot = (bk == iv).astype(F32)
        t = jnp.dot(rb_ref[...], onehot, preferred_element_type=F32, precision=lax.Precision.HIGHEST)
        o_ref[0] = jnp.where(iv < 0.0, NEG_INF, t)

    return pl.pallas_call(
        body, name=name, grid=(len(DILATIONS),),
        in_specs=[pl.BlockSpec((B_HEADS, NUM_BUCKETS), lambda d: (0, 0)),
                  pl.BlockSpec((1, 1, BIAS_SIZE), lambda d: (d, 0, 0))],
        out_specs=pl.BlockSpec((1, B_HEADS, BIAS_SIZE), lambda d: (d, 0, 0)),
        out_shape=SDS((len(DILATIONS), B_HEADS, BIAS_SIZE), F32),
        compiler_params=_params(("parallel",)),
    )(rel_bias_t, idx)


def rel_bias_grad(dbias, name):
    idx = jnp.asarray(_bucket_tables())

    def body(db_ref, idx_ref, o_ref):
        d = pl.program_id(0)
        iv = idx_ref[0]
        bk = lax.broadcasted_iota(jnp.int32, (NUM_BUCKETS, BIAS_SIZE), 0).astype(F32)
        onehot = (bk == iv).astype(F32)
        part = lax.dot_general(db_ref[0], onehot, (((1,), (1,)), ((), ())),
                               preferred_element_type=F32, precision=lax.Precision.HIGHEST)

        @pl.when(d == 0)
        def _():
            o_ref[...] = part

        @pl.when(d > 0)
        def _():
            o_ref[...] += part

    return pl.pallas_call(
        body, name=name, grid=(len(DILATIONS),),
        in_specs=[pl.BlockSpec((1, B_HEADS, BIAS_SIZE), lambda d: (d, 0, 0)),
                  pl.BlockSpec((1, 1, BIAS_SIZE), lambda d: (d, 0, 0))],
        out_specs=pl.BlockSpec((B_HEADS, NUM_BUCKETS), lambda d: (0, 0)),
        out_shape=SDS((B_HEADS, NUM_BUCKETS), F32),
        compiler_params=_params(("arbitrary",)),
    )(dbias, idx)


def _attn_scores(q, kk, bias):
    return _dot_nt(q, kk) * (1.0 / math.sqrt(HEAD_DIM)) + bias


def _rows(start, size, dil):
    return pl.ds(start, size) if dil == 1 else pl.ds(start, size, stride=dil)


def _attn_schedule(op):
    span1, span4 = ATTN_BLOCK, 4 * ATTN_BLOCK

    def d16(i, carry):
        for t in range(4):
            op(2, 16, 4 * i + t, True)
        return carry

    lax.fori_loop(0, 4, d16, 0)
    for r in range(4):
        op(1, 4, r, True)

    def d4(nq, carry):
        for r in range(4):
            op(1, 4, r + nq * span4, False)
        return carry

    lax.fori_loop(1, SEQ // span4, d4, 0)
    op(0, 1, 0, True)

    def d1(j, carry):
        for t in range(3):
            op(0, 1, pl.multiple_of((1 + 3 * j + t) * span1, span1), False)
        return carry

    lax.fori_loop(0, (SEQ // span1 - 1) // 3, d1, 0)


def _kv_rows(start, dil, first):
    if first:
        return _rows(start, ATTN_BLOCK, dil)
    return _rows(start - ATTN_BLOCK * dil, 2 * ATTN_BLOCK, dil)


MERGE_ROWS = 256


def attn_fwd(proj, bias, nb_local, name):
    n = proj.shape[0]
    nseg = len(DILATIONS)

    def body(q_ref, k_ref, v_ref, b_ref, o_ref, lse_ref, os_ref, ls_ref):
        def op(seg, dil, start, first):
            qrows = _rows(start, ATTN_BLOCK, dil)
            krows = _kv_rows(start, dil, first)
            q2, k2, v2 = q_ref[qrows, :], k_ref[krows, :], v_ref[krows, :]
            outs, lses = [], []
            for hh in range(2):
                cs = slice(hh * HEAD_DIM, (hh + 1) * HEAD_DIM)
                bb = b_ref[seg, hh, :, ATTN_BLOCK:] if first else b_ref[seg, hh]
                s = _attn_scores(q2[:, cs].astype(BF16), k2[:, cs].astype(BF16), bb)
                m = jnp.max(s, axis=-1, keepdims=True)
                p = jnp.exp(s - m)
                l = jnp.sum(p, axis=-1, keepdims=True)
                outs.append(_dot(p.astype(BF16), v2[:, cs].astype(BF16)) / l)
                lses.append(jnp.broadcast_to(m + jnp.log(l), (ATTN_BLOCK, HEAD_DIM)))
            os_ref[seg, qrows, :] = jnp.concatenate(outs, axis=1)
            ls_ref[seg, qrows, :] = jnp.concatenate(lses, axis=1)

        _attn_schedule(op)

        def merge(i, carry):
            rows = pl.ds(pl.multiple_of(i * MERGE_ROWS, MERGE_ROWS), MERGE_ROWS)
            ls = [ls_ref[s, rows, :] for s in range(nseg)]
            m = functools.reduce(jnp.maximum, ls)
            ws = [jnp.exp(l - m) for l in ls]
            den = ws[0] + ws[1] + ws[2]
            num = ws[0] * os_ref[0, rows, :] + ws[1] * os_ref[1, rows, :] + ws[2] * os_ref[2, rows, :]
            o_ref[rows, :] = num / den
            lse_ref[rows, :] = m + jnp.log(den)
            return carry

        lax.fori_loop(0, SEQ // MERGE_ROWS, merge, 0)

    def in_spec(off):
        return pl.BlockSpec((SEQ, LANE), lambda b, p: (b, off // LANE + p))

    out_spec = pl.BlockSpec((SEQ, LANE), lambda b, p: (b, p))
    return pl.pallas_call(
        body, name=name, grid=(nb_local, HEAD_PAIRS),
        in_specs=[in_spec(Q_OFF), in_spec(K_OFF), in_spec(V_OFF),
                  pl.BlockSpec((nseg, 2, ATTN_BLOCK, 2 * ATTN_BLOCK), lambda b, p: (0, p, 0, 0))],
        out_specs=[out_spec, out_spec],
        out_shape=[SDS((n, B_WIDTH), F32), SDS((n, B_WIDTH), F32)],
        scratch_shapes=[pltpu.VMEM((nseg, SEQ, LANE), F32), pltpu.VMEM((nseg, SEQ, LANE), F32)],
        compiler_params=_params(("parallel", "arbitrary")),
    )(proj, proj, proj, bias)


def attn_bwd(proj, b_out, dmix, lse_tot, bias, nb_local, name):
    n = proj.shape[0]
    nseg = len(DILATIONS)
    a_blocks = A_WIDTH // LANE
    scale = 1.0 / math.sqrt(HEAD_DIM)

    def body(q_ref, k_ref, v_ref, o_ref, do_ref, lse_ref, b_ref, dq_ref, dk_ref, dv_ref, db_ref, dqs_ref):
        @pl.when(pl.program_id(1) == 0)
        def _():
            db_ref[...] = jnp.zeros_like(db_ref)

        dk_ref[...] = jnp.zeros_like(dk_ref)
        dv_ref[...] = jnp.zeros_like(dv_ref)

        def op(seg, dil, start, first):
            qrows = _rows(start, ATTN_BLOCK, dil)
            krows = _kv_rows(start, dil, first)
            q2, k2, v2 = q_ref[qrows, :], k_ref[krows, :], v_ref[krows, :]
            do2, o2, lse2 = do_ref[qrows, :], o_ref[qrows, :], lse_ref[qrows, :]
            dqs, dks, dvs = [], [], []
            for hh in range(2):
                cs = slice(hh * HEAD_DIM, (hh + 1) * HEAD_DIM)
                q, kk, vv = q2[:, cs].astype(BF16), k2[:, cs].astype(BF16), v2[:, cs].astype(BF16)
                do = do2[:, cs]
                delta = jnp.sum(do * o2[:, cs], axis=-1, keepdims=True)
                bb = b_ref[seg, hh, :, ATTN_BLOCK:] if first else b_ref[seg, hh]
                p = jnp.exp(_attn_scores(q, kk, bb) - lse2[:, hh * HEAD_DIM:hh * HEAD_DIM + 1])
                dob = do.astype(BF16)
                dvs.append(_dot_tn(p.astype(BF16), dob))
                ds = p * (_dot_nt(dob, vv) - delta)
                if first:
                    db_ref[seg, hh, :, ATTN_BLOCK:] += ds
                else:
                    db_ref[seg, hh] += ds
                dsb = ds.astype(BF16)
                dqs.append(_dot(dsb, kk) * scale)
                dks.append(_dot_tn(dsb, q) * scale)
            dqs_ref[seg, qrows, :] = jnp.concatenate(dqs, axis=1)
            dk_ref[krows, :] += jnp.concatenate(dks, axis=1)
            dv_ref[krows, :] += jnp.concatenate(dvs, axis=1)

        _attn_schedule(op)

        def merge(i, carry):
            rows = pl.ds(pl.multiple_of(i * MERGE_ROWS, MERGE_ROWS), MERGE_ROWS)
            dq_ref[rows, :] = dqs_ref[0, rows, :] + dqs_ref[1, rows, :] + dqs_ref[2, rows, :]
            return carry

        lax.fori_loop(0, SEQ // MERGE_ROWS, merge, 0)

    def pspec(off):
        return pl.BlockSpec((SEQ, LANE), lambda p, b: (b, off // LANE + p))

    ospec = pl.BlockSpec((SEQ, LANE), lambda p, b: (b, p))
    bspec = pl.BlockSpec((nseg, 2, ATTN_BLOCK, 2 * ATTN_BLOCK), lambda p, b: (0, p, 0, 0))
    gshape = SDS((n, B_WIDTH), F32)
    return pl.pallas_call(
        body, name=name, grid=(HEAD_PAIRS, nb_local),
        in_specs=[pspec(Q_OFF), pspec(K_OFF), pspec(V_OFF), ospec,
                  pl.BlockSpec((SEQ, LANE), lambda p, b: (b, a_blocks + p)), ospec, bspec],
        out_specs=[ospec, ospec, ospec, bspec],
        out_shape=[gshape, gshape, gshape, SDS((nseg, B_HEADS, ATTN_BLOCK, 2 * ATTN_BLOCK), F32)],
        scratch_shapes=[pltpu.VMEM((nseg, SEQ, LANE), F32)],
        compiler_params=_params(("arbitrary", "arbitrary")),
    )(proj, proj, proj, b_out, dmix, lse_tot, bias)


PAD = 8


def conv_gelu_fwd(gp, up, cw, cb, nb_local, name):
    n, f = gp.shape

    def body(gp_ref, up_ref, cw_ref, cb_ref, o_ref, pad_ref):
        pad_ref[0:PAD, :] = jnp.zeros((PAD, LANE), F32)
        pad_ref[PAD:PAD + SEQ, :] = gp_ref[...]
        c = (cb_ref[...] + cw_ref[0:1, :] * pad_ref[PAD - 2:PAD - 2 + SEQ, :]
             + cw_ref[1:2, :] * pad_ref[PAD - 1:PAD - 1 + SEQ, :] + cw_ref[2:3, :] * gp_ref[...])
        o_ref[...] = (_gelu(c) * up_ref[...]).astype(BF16)

    blk = pl.BlockSpec((SEQ, LANE), lambda b, j: (b, j))
    return pl.pallas_call(
        body, name=name, grid=(nb_local, f // LANE),
        in_specs=[blk, blk, pl.BlockSpec((3, LANE), lambda b, j: (0, j)), pl.BlockSpec((1, LANE), lambda b, j: (0, j))],
        out_specs=blk,
        out_shape=SDS((n, f), BF16),
        scratch_shapes=[pltpu.VMEM((SEQ + PAD, LANE), F32)],
        compiler_params=_params(("parallel", "parallel")),
    )(gp, up, cw, cb)


def conv_gelu_bwd(dgu, gp, up, cw, cb, nb_local, name):
    n, f = gp.shape

    def body(dgu_ref, gp_ref, up_ref, cw_ref, cb_ref, dgp_ref, dup_ref, dcw_ref, dcb_ref, pad_ref, dpad_ref):
        b = pl.program_id(1)
        pad_ref[0:PAD, :] = jnp.zeros((PAD, LANE), F32)
        pad_ref[PAD:PAD + SEQ, :] = gp_ref[...]
        g0 = gp_ref[...]
        g1 = pad_ref[PAD - 1:PAD - 1 + SEQ, :]
        g2 = pad_ref[PAD - 2:PAD - 2 + SEQ, :]
        c = cb_ref[...] + cw_ref[0:1, :] * g2 + cw_ref[1:2, :] * g1 + cw_ref[2:3, :] * g0
        gg, dgg = _gelu_and_grad(c)
        dgu = dgu_ref[...].astype(F32)
        dup_ref[...] = (dgu * gg).astype(BF16)
        dc = dgu * up_ref[...] * dgg
        dpad_ref[SEQ:SEQ + PAD, :] = jnp.zeros((PAD, LANE), F32)
        dpad_ref[0:SEQ, :] = dc
        dgp_ref[...] = (cw_ref[2:3, :] * dc + cw_ref[1:2, :] * dpad_ref[1:1 + SEQ, :]
                        + cw_ref[0:1, :] * dpad_ref[2:2 + SEQ, :]).astype(BF16)
        dcw = jnp.concatenate([jnp.sum(dc * g2, axis=0, keepdims=True), jnp.sum(dc * g1, axis=0, keepdims=True),
                               jnp.sum(dc * g0, axis=0, keepdims=True)], axis=0)
        dcb = jnp.sum(dc, axis=0, keepdims=True)

        @pl.when(b == 0)
        def _():
            dcw_ref[...] = dcw
            dcb_ref[...] = dcb

        @pl.when(b > 0)
        def _():
            dcw_ref[...] += dcw
            dcb_ref[...] += dcb

    blk = pl.BlockSpec((SEQ, LANE), lambda j, b: (b, j))
    return pl.pallas_call(
        body, name=name, grid=(f // LANE, nb_local),
        in_specs=[blk, blk, blk, pl.BlockSpec((3, LANE), lambda j, b: (0, j)), pl.BlockSpec((1, LANE), lambda j, b: (0, j))],
        out_specs=[blk, blk, pl.BlockSpec((3, LANE), lambda j, b: (0, j)), pl.BlockSpec((1, LANE), lambda j, b: (0, j))],
        out_shape=[SDS((n, f), BF16), SDS((n, f), BF16), SDS((3, f), F32), SDS((1, f), F32)],
        scratch_shapes=[pltpu.VMEM((SEQ + PAD, LANE), F32), pltpu.VMEM((SEQ + PAD, LANE), F32)],
        compiler_params=_params(("parallel", "arbitrary")),
    )(dgu, gp, up, cw, cb)


def norm_bwd_mid(dh2, x1, dout, z2, g3, g2, name, tm=256):
    n, d = x1.shape

    def body(dh_ref, x1_ref, dout_ref, z2_ref, g3_ref, g2_ref, dx1_ref, dz2_ref, dg3_ref, dg2_ref):
        i = pl.program_id(0)
        dxa, dg3r = _rms_bwd(dh_ref[...], x1_ref[...], g3_ref[...])
        dx1 = dout_ref[...] + dxa
        dx1_ref[...] = dx1
        dz2, dg2r = _rms_bwd(dx1, z2_ref[...], g2_ref[...])
        dz2_ref[...] = dz2.astype(BF16)
        s3 = jnp.sum(dg3r, axis=0, keepdims=True)
        s2 = jnp.sum(dg2r, axis=0, keepdims=True)

        @pl.when(i == 0)
        def _():
            dg3_ref[...] = s3
            dg2_ref[...] = s2

        @pl.when(i > 0)
        def _():
            dg3_ref[...] += s3
            dg2_ref[...] += s2

    row = pl.BlockSpec((tm, d), lambda i: (i, 0))
    vec = pl.BlockSpec((1, d), lambda i: (0, 0))
    return pl.pallas_call(
        body, name=name, grid=(n // tm,),
        in_specs=[row, row, row, row, vec, vec],
        out_specs=[row, row, vec, vec],
        out_shape=[SDS((n, d), F32), SDS((n, d), BF16), SDS((1, d), F32), SDS((1, d), F32)],
        compiler_params=_params(("arbitrary",)),
    )(dh2, x1, dout, z2, g3, g2)


def norm_bwd_in(dh1, x, dx1, g1, name, tm=256):
    n, d = x.shape

    def body(dh_ref, x_ref, dx1_ref, g1_ref, dx_ref, dg1_ref):
        i = pl.program_id(0)
        dxa, dgr = _rms_bwd(dh_ref[...], x_ref[...], g1_ref[...])
        dx_ref[...] = dx1_ref[...] + dxa
        s = jnp.sum(dgr, axis=0, keepdims=True)

        @pl.when(i == 0)
        def _():
            dg1_ref[...] = s

        @pl.when(i > 0)
        def _():
            dg1_ref[...] += s

    row = pl.BlockSpec((tm, d), lambda i: (i, 0))
    vec = pl.BlockSpec((1, d), lambda i: (0, 0))
    return pl.pallas_call(
        body, name=name, grid=(n // tm,),
        in_specs=[row, row, row, vec],
        out_specs=[row, vec],
        out_shape=[SDS((n, d), F32), SDS((1, d), F32)],
        compiler_params=_params(("arbitrary",)),
    )(dh1, x, dx1, g1)


def local_step(x, target, norm_mix_pre, norm_mix_post, norm_ffn_pre, norm_ffn_post, w_in, ln_g, ln_b,
               spatial_w, spatial_bt, rel_bias_t, w_out, w_gate, w_up, conv_w, conv_b, w_down):
    n = x.shape[0]
    nb_local = n // SEQ

    h1, proj = norm_mm(x, norm_mix_pre, [w_in], "fwd_norm_in")
    a = gating_fwd(proj, ln_g, ln_b, spatial_w, spatial_bt, "fwd_gating")
    bias = bias_tables(rel_bias_t, "bias_tables").reshape(len(DILATIONS), B_HEADS, ATTN_BLOCK, 2 * ATTN_BLOCK)
    b_out, lse_tot = attn_fwd(proj, bias, nb_local, "fwd_attn")
    z2, x1 = mm_res_norm([a, b_out], w_out, x, norm_mix_post, "fwd_out_norm")
    h2, gp, up = norm_mm(x1, norm_ffn_pre, [w_gate, w_up], "fwd_norm_ffn", tm=256)
    gu = conv_gelu_fwd(gp, up, conv_w, conv_b, nb_local, "fwd_conv_gelu")
    dy, dout, dg4, loss = down_loss(gu, w_down, x1, norm_ffn_post, target, "fwd_down_loss")

    g_w_down = mm_tn([gu], [dy], "bwd_dw_down", t1=1408)
    dgu = mm_nt([(dy, 0, 0)], [w_down], "bwd_dgu", out_dtype=BF16)
    dgp, dup, d_conv_w, d_conv_b = conv_gelu_bwd(dgu, gp, up, conv_w, conv_b, nb_local, "bwd_conv_gelu")
    g_w_gate = mm_tn([h2], [dgp], "bwd_dw_gate")
    g_w_up = mm_tn([h2], [dup], "bwd_dw_up")
    dh2 = mm_nt([(dgp, 0, 0), (dup, 1, 0)], [w_gate, w_up], "bwd_dh2", tm=256)
    dx1, dz2, dg3, dg2 = norm_bwd_mid(dh2, x1, dout, z2, norm_ffn_pre, norm_mix_post, "bwd_norm_mid")
    g_w_out = mm_tn([a, b_out], [dz2], "bwd_dw_out")
    dmix = mm_nt([(dz2, 0, 0)], [w_out], "bwd_dmix")
    duv, d_ws, d_sbt, d_lng, d_lnb = gating_bwd(proj, dmix, ln_g, ln_b, spatial_w, spatial_bt, "bwd_gating")
    dq, dk, dv, dbias = attn_bwd(proj, b_out, dmix, lse_tot, bias, nb_local, "bwd_attn")
    d_rel_bias_t = rel_bias_grad(dbias.reshape(len(DILATIONS), B_HEADS, BIAS_SIZE), "bwd_rel_bias")
    g_w_in = mm_tn([h1], [duv, dq, dk, dv], "bwd_dw_in")
    dh1 = mm_nt([(duv, 0, 0), (dq, 0, Q_OFF), (dk, 0, K_OFF), (dv, 0, V_OFF)], [w_in], "bwd_dh1", tm=256)
    grad_x, dg1 = norm_bwd_in(dh1, x, dx1, norm_mix_pre, "bwd_norm_in")

    return dict(loss=loss, grad_x=grad_x, norm_mix_pre=dg1, norm_mix_post=dg2, norm_ffn_pre=dg3, norm_ffn_post=dg4,
                w_in=g_w_in, ln_v_gain=d_lng, ln_v_bias=d_lnb, spatial_w=d_ws, spatial_bt=d_sbt,
                rel_bias_t=d_rel_bias_t, w_out=g_w_out, w_gate=g_w_gate, w_up=g_w_up, conv_w=d_conv_w,
                conv_b=d_conv_b, w_down=g_w_down)


MESH = pl.DeviceIdType.MESH
ANY = pl.BlockSpec(memory_space=pl.ANY)
PEER_MASKS = tuple(range(1, N_DEV))


def _my_index():
    return lax.axis_index("x") * 4 + lax.axis_index("y") * 2 + lax.axis_index("c")


def _peer(mask):
    x, y, c = lax.axis_index("x"), lax.axis_index("y"), lax.axis_index("c")
    px = 1 - x if mask & 4 else x
    py = 1 - y if mask & 2 else y
    pc = 1 - c if mask & 1 else c
    return (px, py, pc), px * 4 + py * 2 + pc


def cast_bf16(arrays, name):
    def body(*refs):
        for i_ref, o_ref in zip(refs[:len(arrays)], refs[len(arrays):]):
            o_ref[...] = i_ref[...].astype(BF16)

    return pl.pallas_call(body, name=name, out_shape=[SDS(a.shape, BF16) for a in arrays],
                          compiler_params=_params())(*arrays)


def all_gather(shards, name):
    ns = len(shards)

    def body(*refs):
        in_refs, out_refs = refs[:ns], refs[ns:2 * ns]
        send_sems, recv_sems, local_sems = refs[2 * ns:]
        me = _my_index()
        local, sends, recvs = [], [], []
        for a in range(ns):
            cp = pltpu.make_async_copy(in_refs[a], out_refs[a].at[me], local_sems.at[a])
            cp.start()
            local.append(cp)
            for mask in PEER_MASKS:
                peer, pidx = _peer(mask)
                cp = pltpu.make_async_remote_copy(
                    src_ref=in_refs[a], dst_ref=out_refs[a].at[me], send_sem=send_sems.at[a, mask - 1],
                    recv_sem=recv_sems.at[a, mask - 1], device_id=peer, device_id_type=MESH)
                cp.start()
                sends.append(cp)
                recvs.append(pltpu.make_async_remote_copy(
                    src_ref=in_refs[a], dst_ref=out_refs[a].at[pidx], send_sem=send_sems.at[a, mask - 1],
                    recv_sem=recv_sems.at[a, mask - 1], device_id=peer, device_id_type=MESH))
        for cp in sends:
            cp.wait_send()
        for cp in recvs:
            cp.wait_recv()
        for cp in local:
            cp.wait()

    return pl.pallas_call(
        body, name=name, in_specs=[ANY] * ns, out_specs=[ANY] * ns,
        out_shape=[SDS((N_DEV,) + s.shape, s.dtype) for s in shards],
        scratch_shapes=[pltpu.SemaphoreType.DMA((ns, N_DEV - 1)), pltpu.SemaphoreType.DMA((ns, N_DEV - 1)),
                        pltpu.SemaphoreType.DMA((ns,))],
    )(*shards)


def exchange_partials(blocked, whole, name):
    nb, nw = len(blocked), len(whole)
    ns = nb + nw

    def body(*refs):
        in_refs, out_refs = refs[:ns], refs[ns:2 * ns]
        send_sems, recv_sems, local_sems = refs[2 * ns:]
        me = _my_index()
        local, sends, recvs = [], [], []
        for a in range(ns):
            src_of = (lambda idx, a=a: in_refs[a].at[idx]) if a < nb else (lambda idx, a=a: in_refs[a])
            cp = pltpu.make_async_copy(src_of(me), out_refs[a].at[me], local_sems.at[a])
            cp.start()
            local.append(cp)
            for mask in PEER_MASKS:
                peer, pidx = _peer(mask)
                cp = pltpu.make_async_remote_copy(
                    src_ref=src_of(pidx), dst_ref=out_refs[a].at[me], send_sem=send_sems.at[a, mask - 1],
                    recv_sem=recv_sems.at[a, mask - 1], device_id=peer, device_id_type=MESH)
                cp.start()
                sends.append(cp)
                recvs.append(pltpu.make_async_remote_copy(
                    src_ref=src_of(pidx), dst_ref=out_refs[a].at[pidx], send_sem=send_sems.at[a, mask - 1],
                    recv_sem=recv_sems.at[a, mask - 1], device_id=peer, device_id_type=MESH))
        for cp in sends:
            cp.wait_send()
        for cp in recvs:
            cp.wait_recv()
        for cp in local:
            cp.wait()

    return pl.pallas_call(
        body, name=name, in_specs=[ANY] * ns, out_specs=[ANY] * ns,
        out_shape=[SDS(b.shape, b.dtype) for b in blocked] + [SDS((N_DEV,) + w.shape, w.dtype) for w in whole],
        scratch_shapes=[pltpu.SemaphoreType.DMA((ns, N_DEV - 1)), pltpu.SemaphoreType.DMA((ns, N_DEV - 1)),
                        pltpu.SemaphoreType.DMA((ns,))],
    )(*blocked, *whole)


def adam_update(parts, w, m, v, name, tr=None):
    s, r, c = parts.shape
    tr = r if tr is None else tr
    bc1 = 1.0 - ADAM_B1 ** ADAM_STEP
    bc2 = 1.0 - ADAM_B2 ** ADAM_STEP

    def body(p_ref, w_ref, m_ref, v_ref, g_ref, d_ref, nm_ref, nv_ref):
        g = p_ref[0].astype(F32)
        for j in range(1, s):
            g = g + p_ref[j].astype(F32)
        nm = ADAM_B1 * m_ref[...] + (1.0 - ADAM_B1) * g
        nv = ADAM_B2 * v_ref[...] + (1.0 - ADAM_B2) * (g * g)
        g_ref[...] = g
        nm_ref[...] = nm
        nv_ref[...] = nv
        d_ref[...] = -ADAM_LR * ((nm / bc1) / (jnp.sqrt(nv / bc2) + ADAM_EPS) + ADAM_WD * w_ref[...])

    blk = pl.BlockSpec((tr, c), lambda i: (i, 0))
    return pl.pallas_call(
        body, name=name, grid=(r // tr,),
        in_specs=[pl.BlockSpec((s, tr, c), lambda i: (0, i, 0)), blk, blk, blk],
        out_specs=[blk] * 4, out_shape=[SDS((r, c), F32)] * 4,
        compiler_params=_params(("parallel",)),
    )(parts, w, m, v)


SMALL_NAMES = ("spatial_w", "norm_mix_pre", "norm_mix_post", "norm_ffn_pre", "norm_ffn_post", "conv_b",
               "ln_v_gain", "ln_v_bias", "spatial_b", "rel_bias")
PACK_ROW_ALIGN = 8


def _pack_rows(size):
    rows = -(-size // LANE)
    return -(-rows // PACK_ROW_ALIGN) * PACK_ROW_ALIGN


def _pack(arrays):
    flat = []
    for a in arrays:
        rows = _pack_rows(a.size)
        flat.append(jnp.pad(a.reshape(-1), (0, rows * LANE - a.size)))
    return jnp.concatenate(flat).reshape(-1, LANE)


def _unpack(packed, shapes):
    out, row = [], 0
    for shp in shapes:
        size = int(np.prod(shp))
        out.append(packed[row:row + _pack_rows(size)].reshape(-1)[:size].reshape(shp))
        row += _pack_rows(size)
    return out


def kernel(x, norm_mix_pre, norm_mix_post, norm_ffn_pre, norm_ffn_post, w_in, ln_v_gain, ln_v_bias, spatial_w, spatial_b, rel_bias, w_out, w_gate, w_up, conv_w, conv_b, w_down, loss_target, m_norm_mix_pre, m_norm_mix_post, m_norm_ffn_pre, m_norm_ffn_post, m_w_in, m_ln_v_gain, m_ln_v_bias, m_spatial_w, m_spatial_b, m_rel_bias, m_w_out, m_w_gate, m_w_up, m_conv_w, m_conv_b, m_w_down, v_norm_mix_pre, v_norm_mix_post, v_norm_ffn_pre, v_norm_ffn_post, v_w_in, v_ln_v_gain, v_ln_v_bias, v_spatial_w, v_spatial_b, v_rel_bias, v_w_out, v_w_gate, v_w_up, v_conv_w, v_conv_b, v_w_down):
    given = dict(locals())
    nb_local, seq, d = x.shape
    n = nb_local * seq
    cols = w_in.shape[2]

    shards = cast_bf16([w_in[0], w_out[0], w_gate[0], w_up[0], w_down[0]], "cast_shards")
    g_in, g_out, g_gate, g_up, g_down, g_cw = all_gather(list(shards) + [conv_w[0]], "gather_weights")

    def by_columns(g):
        return g.transpose(1, 0, 2).reshape(g.shape[1], N_DEV * g.shape[2])

    part = local_step(
        x.reshape(n, d), loss_target.reshape(n, d), norm_mix_pre, norm_mix_post, norm_ffn_pre, norm_ffn_post,
        by_columns(g_in), ln_v_gain.reshape(1, A_WIDTH), ln_v_bias.reshape(1, A_WIDTH), spatial_w[0], spatial_b[0].T,
        rel_bias.T, g_out.reshape(D_MODEL, D_MODEL), by_columns(g_gate), by_columns(g_up), by_columns(g_cw), conv_b,
        g_down.reshape(D_FF, D_MODEL))

    def to_blocks(g):
        return g.reshape(g.shape[0], N_DEV, cols).transpose(1, 0, 2)

    small = dict(part)
    small["spatial_b"] = part["spatial_bt"].T
    small["rel_bias"] = part["rel_bias_t"].T
    pack = _pack([small[k] for k in SMALL_NAMES] + [part["conv_w"], part["loss"]])
    r_in, r_out, r_gate, r_up, r_down, r_small = exchange_partials(
        [to_blocks(part["w_in"]), part["w_out"].reshape(N_DEV, D_MODEL // N_DEV, D_MODEL), to_blocks(part["w_gate"]),
         to_blocks(part["w_up"]), part["w_down"].reshape(N_DEV, cols, D_MODEL)], [pack], "exchange_grads")

    res = {}
    res["w_in"] = adam_update(r_in, w_in[0], m_w_in[0], v_w_in[0], "adam_w_in", tr=256)
    res["w_out"] = adam_update(r_out, w_out[0], m_w_out[0], v_w_out[0], "adam_w_out")
    res["w_gate"] = adam_update(r_gate, w_gate[0], m_w_gate[0], v_w_gate[0], "adam_w_gate", tr=256)
    res["w_up"] = adam_update(r_up, w_up[0], m_w_up[0], v_w_up[0], "adam_w_up", tr=256)
    res["w_down"] = adam_update(r_down, w_down[0], m_w_down[0], v_w_down[0], "adam_w_down", tr=176)

    tail = [jnp.zeros_like(part["conv_w"]), jnp.zeros_like(part["loss"])]
    packs = [_pack([given[pre + k] for k in SMALL_NAMES] + tail) for pre in ("", "m_", "v_")]
    small_res = adam_update(r_small, *packs, "adam_small")
    small_shapes = [given[k].shape for k in SMALL_NAMES] + [part["conv_w"].shape, part["loss"].shape]
    unpacked = [_unpack(p, small_shapes) for p in small_res]
    for i, k in enumerate(SMALL_NAMES):
        res[k] = [u[i] for u in unpacked]
    g_conv_w = lax.dynamic_slice_in_dim(unpacked[0][len(SMALL_NAMES)], _my_index() * cols, cols, axis=1)
    res["conv_w"] = adam_update(g_conv_w[None], conv_w[0], m_conv_w[0], v_conv_w[0], "adam_conv_w")
    loss = unpacked[0][len(SMALL_NAMES) + 1][0, 0]

    names = ("norm_mix_pre", "norm_mix_post", "norm_ffn_pre", "norm_ffn_post", "w_in", "ln_v_gain", "ln_v_bias",
             "spatial_w", "spatial_b", "rel_bias", "w_out", "w_gate", "w_up", "conv_w", "conv_b", "w_down")
    outs = [loss, part["grad_x"].reshape(x.shape)]
    for t in range(4):
        outs += [res[k][t].reshape(given[k].shape) for k in names]
    return tuple(outs)
```

```python
import functools
import math

import numpy as np
import jax
import jax.numpy as jnp
from jax import lax
from jax.experimental import pallas as pl
from jax.experimental.pallas import tpu as pltpu

F32 = jnp.float32
BF16 = jnp.bfloat16
SDS = jax.ShapeDtypeStruct

D_MODEL = 1024
SEQ = 2048
HEAD_DIM = 64
A_GROUPS = 4
A_WIDTH = A_GROUPS * HEAD_DIM
B_HEADS = 12
B_WIDTH = B_HEADS * HEAD_DIM
HEAD_PAIRS = B_HEADS // 2
CHUNK = 128
ATTN_BLOCK = 128
DILATIONS = (1, 4, 16)
NUM_BUCKETS = 32
MAX_DISTANCE = 2048
D_FF = 2816
IN_COLS = 2 * A_WIDTH + 3 * B_WIDTH
Q_OFF = 2 * A_WIDTH
K_OFF = Q_OFF + B_WIDTH
V_OFF = K_OFF + B_WIDTH
NORM_EPS = 1e-6
NEG_INF = -1e30
N_DEV = 8
LANE = 128

ADAM_LR = 0.001
ADAM_B1 = 0.9
ADAM_B2 = 0.999
ADAM_EPS = 1e-08
ADAM_WD = 0.01
ADAM_STEP = 10

GELU_C0 = math.sqrt(2.0 / math.pi)
GELU_C1 = 0.044715

VMEM_LIMIT = 56 * 1024 * 1024


def _params(sem=None):
    if sem is None:
        return pltpu.CompilerParams(vmem_limit_bytes=VMEM_LIMIT)
    return pltpu.CompilerParams(dimension_semantics=sem, vmem_limit_bytes=VMEM_LIMIT)


def _gelu(x):
    t = jnp.tanh(GELU_C0 * (x + GELU_C1 * x * x * x))
    return 0.5 * x * (1.0 + t)


def _gelu_and_grad(x):
    x2 = x * x
    t = jnp.tanh(GELU_C0 * (x + GELU_C1 * x * x2))
    g = 0.5 * x * (1.0 + t)
    dg = 0.5 * (1.0 + t) + 0.5 * x * (1.0 - t * t) * (GELU_C0 * (1.0 + 3.0 * GELU_C1 * x2))
    return g, dg


def _dot(a, b):
    return jnp.dot(a, b, preferred_element_type=F32)


def _dot_nt(a, b):
    return lax.dot_general(a, b, (((1,), (1,)), ((), ())), preferred_element_type=F32)


def _dot_tn(a, b):
    return lax.dot_general(a, b, (((0,), (0,)), ((), ())), preferred_element_type=F32)


def _rms_bwd(d, xin, g):
    r = lax.rsqrt(jnp.mean(xin * xin, axis=-1, keepdims=True) + NORM_EPS)
    xh = xin * r
    gd = g * d
    dx = r * (gd - xh * jnp.mean(gd * xh, axis=-1, keepdims=True))
    return dx, d * xh


MESH = pl.DeviceIdType.MESH
ANY = pl.BlockSpec(memory_space=pl.ANY)
PEER_MASKS = tuple(range(1, N_DEV))


def _my_index():
    return lax.axis_index("x") * 4 + lax.axis_index("y") * 2 + lax.axis_index("c")


def _peer(mask):
    x, y, c = lax.axis_index("x"), lax.axis_index("y"), lax.axis_index("c")
    px = 1 - x if mask & 4 else x
    py = 1 - y if mask & 2 else y
    pc = 1 - c if mask & 1 else c
    return (px, py, pc), px * 4 + py * 2 + pc


def _exchange_copies(nblocked, in_refs, out_refs, sems):
    send_sems, recv_sems, local_sems = sems
    me = _my_index()
    local, sends, recvs = [], [], []
    for a, (in_ref, out_ref) in enumerate(zip(in_refs, out_refs)):
        src_of = (lambda idx, r=in_ref: r.at[idx]) if a < nblocked else (lambda idx, r=in_ref: r)
        local.append(pltpu.make_async_copy(src_of(me), out_ref.at[me], local_sems.at[a]))
        for mask in PEER_MASKS:
            peer, pidx = _peer(mask)
            pair = dict(send_sem=send_sems.at[a, mask - 1], recv_sem=recv_sems.at[a, mask - 1],
                        device_id=peer, device_id_type=MESH)
            sends.append(pltpu.make_async_remote_copy(src_ref=src_of(pidx), dst_ref=out_ref.at[me], **pair))
            recvs.append(pltpu.make_async_remote_copy(src_ref=src_of(pidx), dst_ref=out_ref.at[pidx], **pair))
    return local, sends, recvs


def _exchange_start(nblocked, in_refs, out_refs, sems):
    local, sends, _ = _exchange_copies(nblocked, in_refs, out_refs, sems)
    for cp in local + sends:
        cp.start()


def _exchange_wait(nblocked, in_refs, out_refs, sems):
    local, sends, recvs = _exchange_copies(nblocked, in_refs, out_refs, sems)
    for cp in sends:
        cp.wait_send()
    for cp in recvs:
        cp.wait_recv()
    for cp in local:
        cp.wait()


def _exchange_out_shape(blocked, whole):
    return [SDS(b.shape, b.dtype) for b in blocked] + [SDS((N_DEV,) + w.shape, w.dtype) for w in whole]


def _exchange_sems(n):
    return [pltpu.SemaphoreType.DMA((n, N_DEV - 1)), pltpu.SemaphoreType.DMA((n, N_DEV - 1)),
            pltpu.SemaphoreType.DMA((n,))]


def exchange(blocked, whole, name):
    nb, n = len(blocked), len(blocked) + len(whole)

    def body(*refs):
        _exchange_start(nb, refs[:n], refs[n:2 * n], refs[2 * n:])
        _exchange_wait(nb, refs[:n], refs[n:2 * n], refs[2 * n:])

    return pl.pallas_call(
        body, name=name, in_specs=[ANY] * n, out_specs=[ANY] * n, out_shape=_exchange_out_shape(blocked, whole),
        scratch_shapes=_exchange_sems(n),
    )(*blocked, *whole)


def _call(body, *, name, grid, in_specs, out_specs, out_shape, args, scratch_shapes=(), sem=None, ride=None):
    out_shape, out_specs, scratch_shapes = list(out_shape), list(out_specs), list(scratch_shapes)
    if ride is None:
        outs = pl.pallas_call(body, name=name, grid=grid, in_specs=list(in_specs), out_specs=out_specs,
                              out_shape=out_shape, scratch_shapes=scratch_shapes,
                              compiler_params=_params(sem))(*args)
        return list(outs), []
    blocked, whole = ride
    cargs = list(blocked) + list(whole)
    nb, nc = len(blocked), len(cargs)
    n_in, n_out, n_scr = len(args), len(out_shape), len(scratch_shapes)

    def riding(*refs):
        ins, refs = refs[:n_in], refs[n_in:]
        cins, refs = refs[:nc], refs[nc:]
        outs, refs = refs[:n_out], refs[n_out:]
        couts, refs = refs[:nc], refs[nc:]
        scr, sems = refs[:n_scr], refs[n_scr:]
        ids = [pl.program_id(k) for k in range(len(grid))]
        first = functools.reduce(lambda p, q: p & q, [i == 0 for i in ids])
        last = functools.reduce(lambda p, q: p & q, [i == g - 1 for i, g in zip(ids, grid)])

        @pl.when(first)
        def _():
            _exchange_start(nb, cins, couts, sems)

        body(*ins, *outs, *scr)

        @pl.when(last)
        def _():
            _exchange_wait(nb, cins, couts, sems)

    res = pl.pallas_call(
        riding, name=name, grid=grid, in_specs=list(in_specs) + [ANY] * nc, out_specs=out_specs + [ANY] * nc,
        out_shape=out_shape + _exchange_out_shape(blocked, whole),
        scratch_shapes=scratch_shapes + _exchange_sems(nc),
        compiler_params=_params(("arbitrary",) * len(grid)))(*args, *cargs)
    return list(res[:n_out]), list(res[n_out:])


def norm_mm(x, g, ws, name, tm=512, tn=1408, ride=None):
    n, d = x.shape
    f = ws[0].shape[1]
    nw = len(ws)

    def body(x_ref, g_ref, *refs):
        w_refs = refs[:nw]
        h_ref = refs[nw]
        o_refs = refs[nw + 1:]

        @pl.when(pl.program_id(1) == 0)
        def _():
            xv = x_ref[...]
            r = lax.rsqrt(jnp.mean(xv * xv, axis=-1, keepdims=True) + NORM_EPS)
            h_ref[...] = (xv * r * g_ref[...]).astype(BF16)

        h = h_ref[...]
        for w_ref, o_ref in zip(w_refs, o_refs):
            o_ref[...] = _dot(h, w_ref[...])

    return _call(
        body, name=name, grid=(n // tm, f // tn),
        in_specs=[pl.BlockSpec((tm, d), lambda i, j: (i, 0)), pl.BlockSpec((1, d), lambda i, j: (0, 0))]
        + [pl.BlockSpec((d, tn), lambda i, j: (0, j)) for _ in ws],
        out_specs=[pl.BlockSpec((tm, d), lambda i, j: (i, 0))]
        + [pl.BlockSpec((tm, tn), lambda i, j: (i, j)) for _ in ws],
        out_shape=[SDS((n, d), BF16)] + [SDS((n, f), F32) for _ in ws],
        args=[x, g, *ws], sem=("parallel", "arbitrary"), ride=ride)


def _lane_concat(refs):
    vals = [r[...].astype(BF16) for r in refs]
    return vals[0] if len(vals) == 1 else jnp.concatenate(vals, axis=1)


def mm_res_norm(a_list, w, res, g, name, tm=512):
    n = a_list[0].shape[0]
    k, d = w.shape
    na = len(a_list)

    def body(*refs):
        w_ref, res_ref, g_ref, y_ref, o_ref = refs[na:]
        y = _dot(_lane_concat(refs[:na]), w_ref[...])
        r = lax.rsqrt(jnp.mean(y * y, axis=-1, keepdims=True) + NORM_EPS)
        y_ref[...] = y
        o_ref[...] = res_ref[...] + y * r * g_ref[...]

    return pl.pallas_call(
        body, name=name, grid=(n // tm,),
        in_specs=[pl.BlockSpec((tm, a.shape[1]), lambda i: (i, 0)) for a in a_list]
        + [pl.BlockSpec((k, d), lambda i: (0, 0)),
           pl.BlockSpec((tm, d), lambda i: (i, 0)), pl.BlockSpec((1, d), lambda i: (0, 0))],
        out_specs=[pl.BlockSpec((tm, d), lambda i: (i, 0)), pl.BlockSpec((tm, d), lambda i: (i, 0))],
        out_shape=[SDS((n, d), F32), SDS((n, d), F32)],
        compiler_params=_params(("parallel",)),
    )(*a_list, w, res, g)


def down_loss(a, w, res, g, target, name, tm=256):
    n, k = a.shape
    d = w.shape[1]
    inv_d = 1.0 / d

    def body(a_ref, w_ref, res_ref, g_ref, t_ref, dy_ref, dout_ref, dg_ref, loss_ref):
        i = pl.program_id(0)
        y = _dot(a_ref[...], w_ref[...])
        gv = g_ref[...]
        r = lax.rsqrt(jnp.mean(y * y, axis=-1, keepdims=True) + NORM_EPS)
        yh = y * r
        e = res_ref[...] + yh * gv - t_ref[...]
        part = 0.5 * inv_d * jnp.sum(jnp.sum(e * e, axis=-1, keepdims=True), axis=0, keepdims=True)
        dout = e * inv_d
        dout_ref[...] = dout
        gd = gv * dout
        dy_ref[...] = (r * (gd - yh * jnp.mean(gd * yh, axis=-1, keepdims=True))).astype(BF16)
        dgp = jnp.sum(dout * yh, axis=0, keepdims=True)
        lane0 = lax.broadcasted_iota(jnp.int32, (1, LANE), 1) == 0
        lp = jnp.where(lane0, part, 0.0)

        @pl.when(i == 0)
        def _():
            dg_ref[...] = dgp
            loss_ref[...] = lp

        @pl.when(i > 0)
        def _():
            dg_ref[...] += dgp
            loss_ref[...] += lp

    return pl.pallas_call(
        body, name=name, grid=(n // tm,),
        in_specs=[pl.BlockSpec((tm, k), lambda i: (i, 0)), pl.BlockSpec((k, d), lambda i: (0, 0)),
                  pl.BlockSpec((tm, d), lambda i: (i, 0)), pl.BlockSpec((1, d), lambda i: (0, 0)),
                  pl.BlockSpec((tm, d), lambda i: (i, 0))],
        out_specs=[pl.BlockSpec((tm, d), lambda i: (i, 0)), pl.BlockSpec((tm, d), lambda i: (i, 0)),
                   pl.BlockSpec((1, d), lambda i: (0, 0)), pl.BlockSpec((1, LANE), lambda i: (0, 0))],
        out_shape=[SDS((n, d), BF16), SDS((n, d), F32), SDS((1, d), F32), SDS((1, LANE), F32)],
        compiler_params=_params(("arbitrary",)),
    )(a, w, res, g, target)


def mm_nt(terms, ws, name, tm=512, out_dtype=F32, ride=None):
    n = terms[0][0].shape[0]
    r = ws[0].shape[0]
    na = len(terms)
    meta = [(widx, off, a.shape[1]) for a, widx, off in terms]

    def body(*refs):
        a_refs = refs[:na]
        w_refs = refs[na:na + len(ws)]
        o_ref = refs[-1]
        acc = None
        for a_ref, (widx, off, k) in zip(a_refs, meta):
            p = _dot_nt(a_ref[...].astype(BF16), w_refs[widx][:, off:off + k])
            acc = p if acc is None else acc + p
        o_ref[...] = acc.astype(out_dtype)

    return _call(
        body, name=name, grid=(n // tm,),
        in_specs=[pl.BlockSpec((tm, a.shape[1]), lambda i: (i, 0)) for a, _, _ in terms]
        + [pl.BlockSpec(w.shape, lambda i: (0, 0)) for w in ws],
        out_specs=[pl.BlockSpec((tm, r), lambda i: (i, 0))],
        out_shape=[SDS((n, r), out_dtype)],
        args=[a for a, _, _ in terms] + list(ws), sem=("parallel",), ride=ride)


def mm_tn(lhs_list, rhs_list, name, t1=512, tn=512, out_dtype=BF16):
    n = lhs_list[0].shape[0]
    k1 = sum(l.shape[1] for l in lhs_list)
    nl = len(lhs_list)
    if nl > 1:
        t1 = k1
    widths = [r.shape[1] for r in rhs_list]
    k2 = sum(widths)
    nr = len(rhs_list)
    nk = n // tn

    def body(*refs):
        r_refs = refs[nl:nl + nr]
        o_ref = refs[nl + nr]
        acc = refs[nl + nr + 1]
        k = pl.program_id(1)
        a = _lane_concat(refs[:nl])
        parts = [_dot_tn(a, r_ref[...].astype(BF16)) for r_ref in r_refs]
        val = parts[0] if nr == 1 else jnp.concatenate(parts, axis=1)

        @pl.when(k == 0)
        def _():
            acc[...] = val

        @pl.when(k > 0)
        def _():
            acc[...] += val

        @pl.when(k == nk - 1)
        def _():
            o_ref[...] = acc[...].astype(out_dtype)

    if nl == 1:
        lhs_specs = [pl.BlockSpec((tn, t1), lambda i, k: (k, i))]
    else:
        lhs_specs = [pl.BlockSpec((tn, l.shape[1]), lambda i, k: (k, 0)) for l in lhs_list]
    return pl.pallas_call(
        body, name=name, grid=(k1 // t1, nk),
        in_specs=lhs_specs + [pl.BlockSpec((tn, w), lambda i, k: (k, 0)) for w in widths],
        out_specs=pl.BlockSpec((t1, k2), lambda i, k: (i, 0)),
        out_shape=SDS((k1, k2), out_dtype),
        scratch_shapes=[pltpu.VMEM((t1, k2), F32)],
        compiler_params=_params(("parallel", "arbitrary")),
    )(*lhs_list, *rhs_list)


GATE_ROWS = 512


def _tril_mask():
    row = lax.broadcasted_iota(jnp.int32, (CHUNK, CHUNK), 0)
    col = lax.broadcasted_iota(jnp.int32, (CHUNK, CHUNK), 1)
    return row >= col


def _layer_norm_parts(gv):
    mu = jnp.mean(gv, axis=-1, keepdims=True)
    xc = gv - mu
    rstd = lax.rsqrt(jnp.mean(xc * xc, axis=-1, keepdims=True) + NORM_EPS)
    return xc * rstd, rstd


def gating_fwd(proj, lng, lnb, ws, sbt, name):
    n = proj.shape[0]
    nchunk = GATE_ROWS // CHUNK

    def body(u_ref, v_ref, lng_ref, lnb_ref, ws_ref, sbt_ref, a_ref):
        tril = _tril_mask()
        for g in range(A_GROUPS):
            cs = slice(g * HEAD_DIM, (g + 1) * HEAD_DIM)
            wt = jnp.where(tril, ws_ref[g], 0.0).astype(BF16)
            for c in range(nchunk):
                rs_ = slice(c * CHUNK, (c + 1) * CHUNK)
                vhat, _ = _layer_norm_parts(_gelu(v_ref[rs_, cs]))
                vn = vhat * lng_ref[:, cs] + lnb_ref[:, cs]
                z = _dot(wt, vn.astype(BF16)) + sbt_ref[:, g:g + 1]
                a_ref[rs_, cs] = _gelu(u_ref[rs_, cs]) * z

    return pl.pallas_call(
        body, name=name, grid=(n // GATE_ROWS,),
        in_specs=[pl.BlockSpec((GATE_ROWS, A_WIDTH), lambda i: (i, 0)),
                  pl.BlockSpec((GATE_ROWS, A_WIDTH), lambda i: (i, 1)),
                  pl.BlockSpec((1, A_WIDTH), lambda i: (0, 0)), pl.BlockSpec((1, A_WIDTH), lambda i: (0, 0)),
                  pl.BlockSpec((A_GROUPS, CHUNK, CHUNK), lambda i: (0, 0, 0)),
                  pl.BlockSpec((CHUNK, A_GROUPS), lambda i: (0, 0))],
        out_specs=pl.BlockSpec((GATE_ROWS, A_WIDTH), lambda i: (i, 0)),
        out_shape=SDS((n, A_WIDTH), F32),
        compiler_params=_params(("parallel",)),
    )(proj, proj, lng, lnb, ws, sbt)


def gating_bwd(proj, dmix, lng, lnb, ws, sbt, name):
    n = proj.shape[0]
    nchunk = GATE_ROWS // CHUNK

    def body(u_ref, v_ref, da_ref, lng_ref, lnb_ref, ws_ref, sbt_ref,
             duv_ref, dws_ref, dsbt_ref, dlng_ref, dlnb_ref):
        @pl.when(pl.program_id(0) == 0)
        def _():
            dws_ref[...] = jnp.zeros_like(dws_ref)
            dsbt_ref[...] = jnp.zeros_like(dsbt_ref)
            dlng_ref[...] = jnp.zeros_like(dlng_ref)
            dlnb_ref[...] = jnp.zeros_like(dlnb_ref)

        tril = _tril_mask()
        for g in range(A_GROUPS):
            cs = slice(g * HEAD_DIM, (g + 1) * HEAD_DIM)
            wt = jnp.where(tril, ws_ref[g], 0.0).astype(BF16)
            lg = lng_ref[:, cs]
            dw = jnp.zeros((CHUNK, CHUNK), F32)
            dsb = jnp.zeros((CHUNK, 1), F32)
            dlg = jnp.zeros((1, HEAD_DIM), F32)
            dlb = jnp.zeros((1, HEAD_DIM), F32)
            for c in range(nchunk):
                rs_ = slice(c * CHUNK, (c + 1) * CHUNK)
                gu, dgu_dx = _gelu_and_grad(u_ref[rs_, cs])
                gv, dgv_dx = _gelu_and_grad(v_ref[rs_, cs])
                vhat, rstd = _layer_norm_parts(gv)
                vn = (vhat * lg + lnb_ref[:, cs]).astype(BF16)
                z = _dot(wt, vn) + sbt_ref[:, g:g + 1]
                da = da_ref[rs_, cs]
                dz = da * gu
                dzb = dz.astype(BF16)
                duv_ref[rs_, cs] = da * z * dgu_dx
                dsb = dsb + jnp.sum(dz, axis=-1, keepdims=True)
                dw = dw + _dot_nt(dzb, vn)
                dvn = _dot_tn(wt, dzb)
                dlg = dlg + jnp.sum(dvn * vhat, axis=0, keepdims=True)
                dlb = dlb + jnp.sum(dvn, axis=0, keepdims=True)
                dvh = dvn * lg
                dgv = rstd * (dvh - jnp.mean(dvh, axis=-1, keepdims=True)
                              - vhat * jnp.mean(dvh * vhat, axis=-1, keepdims=True))
                duv_ref[rs_, A_WIDTH + g * HEAD_DIM:A_WIDTH + (g + 1) * HEAD_DIM] = dgv * dgv_dx
            dws_ref[g] += jnp.where(tril, dw, 0.0)
            dsbt_ref[:, g:g + 1] += dsb
            dlng_ref[:, cs] += dlg
            dlnb_ref[:, cs] += dlb

    return pl.pallas_call(
        body, name=name, grid=(n // GATE_ROWS,),
        in_specs=[pl.BlockSpec((GATE_ROWS, A_WIDTH), lambda i: (i, 0)),
                  pl.BlockSpec((GATE_ROWS, A_WIDTH), lambda i: (i, 1)),
                  pl.BlockSpec((GATE_ROWS, A_WIDTH), lambda i: (i, 0)),
                  pl.BlockSpec((1, A_WIDTH), lambda i: (0, 0)), pl.BlockSpec((1, A_WIDTH), lambda i: (0, 0)),
                  pl.BlockSpec((A_GROUPS, CHUNK, CHUNK), lambda i: (0, 0, 0)),
                  pl.BlockSpec((CHUNK, A_GROUPS), lambda i: (0, 0))],
        out_specs=[pl.BlockSpec((GATE_ROWS, 2 * A_WIDTH), lambda i: (i, 0)),
                   pl.BlockSpec((A_GROUPS, CHUNK, CHUNK), lambda i: (0, 0, 0)),
                   pl.BlockSpec((CHUNK, A_GROUPS), lambda i: (0, 0)),
                   pl.BlockSpec((1, A_WIDTH), lambda i: (0, 0)), pl.BlockSpec((1, A_WIDTH), lambda i: (0, 0))],
        out_shape=[SDS((n, 2 * A_WIDTH), F32), SDS((A_GROUPS, CHUNK, CHUNK), F32), SDS((CHUNK, A_GROUPS), F32),
                   SDS((1, A_WIDTH), F32), SDS((1, A_WIDTH), F32)],
        compiler_params=_params(("arbitrary",)),
    )(proj, proj, dmix, lng, lnb, ws, sbt)


def _t5_bucket_np(dist):
    max_exact = NUM_BUCKETS // 2
    dd = np.maximum(dist, 1).astype(np.float64)
    large = max_exact + np.log(dd / max_exact) / math.log(MAX_DISTANCE / max_exact) * (NUM_BUCKETS - max_exact)
    large = np.minimum(large.astype(np.int64), NUM_BUCKETS - 1)
    return np.where(dist < max_exact, dist, large)


def _bucket_tables():
    i = np.arange(ATTN_BLOCK)[:, None]
    j = np.arange(2 * ATTN_BLOCK)[None, :]
    rel = ATTN_BLOCK + i - j
    band = (rel >= 0) & (rel <= ATTN_BLOCK)
    tabs = []
    for dil in DILATIONS:
        b = _t5_bucket_np(np.maximum(rel, 0) * dil)
        tabs.append(np.where(band, b, -1).reshape(1, -1))
    return np.stack(tabs).astype(np.float32)


BIAS_SIZE = ATTN_BLOCK * 2 * ATTN_BLOCK


def bias_tables(rel_bias_t, name):
    idx = jnp.asarray(_bucket_tables())

    def body(rb_ref, idx_ref, o_ref):
        iv = idx_ref[0]
        bk = lax.broadcasted_iota(jnp.int32, (NUM_BUCKETS, BIAS_SIZE), 0).astype(F32)
        onehot = (bk == iv).astype(F32)
        t = jnp.dot(rb_ref[...], onehot, preferred_element_type=F32, precision=lax.Precision.HIGHEST)
        o_ref[0] = jnp.where(iv < 0.0, NEG_INF, t)

    return pl.pallas_call(
        body, name=name, grid=(len(DILATIONS),),
        in_specs=[pl.BlockSpec((B_HEADS, NUM_BUCKETS), lambda d: (0, 0)),
                  pl.BlockSpec((1, 1, BIAS_SIZE), lambda d: (d, 0, 0))],
        out_specs=pl.BlockSpec((1, B_HEADS, BIAS_SIZE), lambda d: (d, 0, 0)),
        out_shape=SDS((len(DILATIONS), B_HEADS, BIAS_SIZE), F32),
        compiler_params=_params(("parallel",)),
    )(rel_bias_t, idx)


def rel_bias_grad(dbias, name):
    idx = jnp.asarray(_bucket_tables())

    def body(db_ref, idx_ref, o_ref):
        d = pl.program_id(0)
        iv = idx_ref[0]
        bk = lax.broadcasted_iota(jnp.int32, (NUM_BUCKETS, BIAS_SIZE), 0).astype(F32)
        onehot = (bk == iv).astype(F32)
        part = lax.dot_general(db_ref[0], onehot, (((1,), (1,)), ((), ())),
                               preferred_element_type=F32, precision=lax.Precision.HIGHEST)

        @pl.when(d == 0)
        def _():
            o_ref[...] = part

        @pl.when(d > 0)
        def _():
            o_ref[...] += part

    return pl.pallas_call(
        body, name=name, grid=(len(DILATIONS),),
        in_specs=[pl.BlockSpec((1, B_HEADS, BIAS_SIZE), lambda d: (d, 0, 0)),
                  pl.BlockSpec((1, 1, BIAS_SIZE), lambda d: (d, 0, 0))],
        out_specs=pl.BlockSpec((B_HEADS, NUM_BUCKETS), lambda d: (0, 0)),
        out_shape=SDS((B_HEADS, NUM_BUCKETS), F32),
        compiler_params=_params(("arbitrary",)),
    )(dbias, idx)


def _attn_scores(q, kk, bias):
    return _dot_nt(q, kk) * (1.0 / math.sqrt(HEAD_DIM)) + bias


def _rows(start, size, dil):
    return pl.ds(start, size) if dil == 1 else pl.ds(start, size, stride=dil)


def _attn_schedule(op):
    span1, span4 = ATTN_BLOCK, 4 * ATTN_BLOCK

    def d16(i, carry):
        for t in range(4):
            op(2, 16, 4 * i + t, True)
        return carry

    lax.fori_loop(0, 4, d16, 0)
    for r in range(4):
        op(1, 4, r, True)

    def d4(nq, carry):
        for r in range(4):
            op(1, 4, r + nq * span4, False)
        return carry

    lax.fori_loop(1, SEQ // span4, d4, 0)
    op(0, 1, 0, True)

    def d1(j, carry):
        for t in range(3):
            op(0, 1, pl.multiple_of((1 + 3 * j + t) * span1, span1), False)
        return carry

    lax.fori_loop(0, (SEQ // span1 - 1) // 3, d1, 0)


def _kv_rows(start, dil, first):
    if first:
        return _rows(start, ATTN_BLOCK, dil)
    return _rows(start - ATTN_BLOCK * dil, 2 * ATTN_BLOCK, dil)


MERGE_ROWS = 256


def attn_fwd(proj, bias, nb_local, name, ride=None):
    n = proj.shape[0]
    nseg = len(DILATIONS)

    def body(q_ref, k_ref, v_ref, b_ref, o_ref, lse_ref, os_ref, ls_ref):
        def op(seg, dil, start, first):
            qrows = _rows(start, ATTN_BLOCK, dil)
            krows = _kv_rows(start, dil, first)
            q2, k2, v2 = q_ref[qrows, :], k_ref[krows, :], v_ref[krows, :]
            outs, lses = [], []
            for hh in range(2):
                cs = slice(hh * HEAD_DIM, (hh + 1) * HEAD_DIM)
                bb = b_ref[seg, hh, :, ATTN_BLOCK:] if first else b_ref[seg, hh]
                s = _attn_scores(q2[:, cs].astype(BF16), k2[:, cs].astype(BF16), bb)
                m = jnp.max(s, axis=-1, keepdims=True)
                p = jnp.exp(s - m)
                l = jnp.sum(p, axis=-1, keepdims=True)
                outs.append(_dot(p.astype(BF16), v2[:, cs].astype(BF16)) / l)
                lses.append(jnp.broadcast_to(m + jnp.log(l), (ATTN_BLOCK, HEAD_DIM)))
            os_ref[seg, qrows, :] = jnp.concatenate(outs, axis=1)
            ls_ref[seg, qrows, :] = jnp.concatenate(lses, axis=1)

        _attn_schedule(op)

        def merge(i, carry):
            rows = pl.ds(pl.multiple_of(i * MERGE_ROWS, MERGE_ROWS), MERGE_ROWS)
            ls = [ls_ref[s, rows, :] for s in range(nseg)]
            m = functools.reduce(jnp.maximum, ls)
            ws = [jnp.exp(l - m) for l in ls]
            den = ws[0] + ws[1] + ws[2]
            num = ws[0] * os_ref[0, rows, :] + ws[1] * os_ref[1, rows, :] + ws[2] * os_ref[2, rows, :]
            o_ref[rows, :] = num / den
            lse_ref[rows, :] = m + jnp.log(den)
            return carry

        lax.fori_loop(0, SEQ // MERGE_ROWS, merge, 0)

    def in_spec(off):
        return pl.BlockSpec((SEQ, LANE), lambda b, p: (b, off // LANE + p))

    out_spec = pl.BlockSpec((SEQ, LANE), lambda b, p: (b, p))
    return _call(
        body, name=name, grid=(nb_local, HEAD_PAIRS),
        in_specs=[in_spec(Q_OFF), in_spec(K_OFF), in_spec(V_OFF),
                  pl.BlockSpec((nseg, 2, ATTN_BLOCK, 2 * ATTN_BLOCK), lambda b, p: (0, p, 0, 0))],
        out_specs=[out_spec, out_spec],
        out_shape=[SDS((n, B_WIDTH), F32), SDS((n, B_WIDTH), F32)],
        scratch_shapes=[pltpu.VMEM((nseg, SEQ, LANE), F32), pltpu.VMEM((nseg, SEQ, LANE), F32)],
        args=[proj, proj, proj, bias], sem=("parallel", "arbitrary"), ride=ride)


def attn_bwd(proj, b_out, dmix, lse_tot, bias, nb_local, name, ride=None):
    n = proj.shape[0]
    nseg = len(DILATIONS)
    a_blocks = A_WIDTH // LANE
    scale = 1.0 / math.sqrt(HEAD_DIM)

    def body(q_ref, k_ref, v_ref, o_ref, do_ref, lse_ref, b_ref, dq_ref, dk_ref, dv_ref, db_ref, dqs_ref):
        @pl.when(pl.program_id(1) == 0)
        def _():
            db_ref[...] = jnp.zeros_like(db_ref)

        dk_ref[...] = jnp.zeros_like(dk_ref)
        dv_ref[...] = jnp.zeros_like(dv_ref)

        def op(seg, dil, start, first):
            qrows = _rows(start, ATTN_BLOCK, dil)
            krows = _kv_rows(start, dil, first)
            q2, k2, v2 = q_ref[qrows, :], k_ref[krows, :], v_ref[krows, :]
            do2, o2, lse2 = do_ref[qrows, :], o_ref[qrows, :], lse_ref[qrows, :]
            dqs, dks, dvs = [], [], []
            for hh in range(2):
                cs = slice(hh * HEAD_DIM, (hh + 1) * HEAD_DIM)
                q, kk, vv = q2[:, cs].astype(BF16), k2[:, cs].astype(BF16), v2[:, cs].astype(BF16)
                do = do2[:, cs]
                delta = jnp.sum(do * o2[:, cs], axis=-1, keepdims=True)
                bb = b_ref[seg, hh, :, ATTN_BLOCK:] if first else b_ref[seg, hh]
                p = jnp.exp(_attn_scores(q, kk, bb) - lse2[:, hh * HEAD_DIM:hh * HEAD_DIM + 1])
                dob = do.astype(BF16)
                dvs.append(_dot_tn(p.astype(BF16), dob))
                ds = p * (_dot_nt(dob, vv) - delta)
                if first:
                    db_ref[seg, hh, :, ATTN_BLOCK:] += ds
                else:
                    db_ref[seg, hh] += ds
                dsb = ds.astype(BF16)
                dqs.append(_dot(dsb, kk) * scale)
                dks.append(_dot_tn(dsb, q) * scale)
            dqs_ref[seg, qrows, :] = jnp.concatenate(dqs, axis=1)
            dk_ref[krows, :] += jnp.concatenate(dks, axis=1)
            dv_ref[krows, :] += jnp.concatenate(dvs, axis=1)

        _attn_schedule(op)

        def merge(i, carry):
            rows = pl.ds(pl.multiple_of(i * MERGE_ROWS, MERGE_ROWS), MERGE_ROWS)
            dq_ref[rows, :] = dqs_ref[0, rows, :] + dqs_ref[1, rows, :] + dqs_ref[2, rows, :]
            return carry

        lax.fori_loop(0, SEQ // MERGE_ROWS, merge, 0)

    def pspec(off):
        return pl.BlockSpec((SEQ, LANE), lambda p, b: (b, off // LANE + p))

    ospec = pl.BlockSpec((SEQ, LANE), lambda p, b: (b, p))
    bspec = pl.BlockSpec((nseg, 2, ATTN_BLOCK, 2 * ATTN_BLOCK), lambda p, b: (0, p, 0, 0))
    gshape = SDS((n, B_WIDTH), F32)
    return _call(
        body, name=name, grid=(HEAD_PAIRS, nb_local),
        in_specs=[pspec(Q_OFF), pspec(K_OFF), pspec(V_OFF), ospec,
                  pl.BlockSpec((SEQ, LANE), lambda p, b: (b, a_blocks + p)), ospec, bspec],
        out_specs=[ospec, ospec, ospec, bspec],
        out_shape=[gshape, gshape, gshape, SDS((nseg, B_HEADS, ATTN_BLOCK, 2 * ATTN_BLOCK), F32)],
        scratch_shapes=[pltpu.VMEM((nseg, SEQ, LANE), F32)],
        args=[proj, proj, proj, b_out, dmix, lse_tot, bias], sem=("arbitrary", "arbitrary"), ride=ride)


PAD = 8


def conv_gelu_fwd(gp, up, cw, cb, nb_local, name):
    n, f = gp.shape

    def body(gp_ref, up_ref, cw_ref, cb_ref, o_ref, pad_ref):
        pad_ref[0:PAD, :] = jnp.zeros((PAD, LANE), F32)
        pad_ref[PAD:PAD + SEQ, :] = gp_ref[...]
        c = (cb_ref[...] + cw_ref[0:1, :] * pad_ref[PAD - 2:PAD - 2 + SEQ, :]
             + cw_ref[1:2, :] * pad_ref[PAD - 1:PAD - 1 + SEQ, :] + cw_ref[2:3, :] * gp_ref[...])
        o_ref[...] = (_gelu(c) * up_ref[...]).astype(BF16)

    blk = pl.BlockSpec((SEQ, LANE), lambda b, j: (b, j))
    return pl.pallas_call(
        body, name=name, grid=(nb_local, f // LANE),
        in_specs=[blk, blk, pl.BlockSpec((3, LANE), lambda b, j: (0, j)), pl.BlockSpec((1, LANE), lambda b, j: (0, j))],
        out_specs=blk,
        out_shape=SDS((n, f), BF16),
        scratch_shapes=[pltpu.VMEM((SEQ + PAD, LANE), F32)],
        compiler_params=_params(("parallel", "parallel")),
    )(gp, up, cw, cb)


def conv_gelu_bwd(dgu, gp, up, cw, cb, nb_local, name, ride=None):
    n, f = gp.shape

    def body(dgu_ref, gp_ref, up_ref, cw_ref, cb_ref, dgp_ref, dup_ref, dcw_ref, dcb_ref, pad_ref, dpad_ref):
        b = pl.program_id(1)
        pad_ref[0:PAD, :] = jnp.zeros((PAD, LANE), F32)
        pad_ref[PAD:PAD + SEQ, :] = gp_ref[...]
        g0 = gp_ref[...]
        g1 = pad_ref[PAD - 1:PAD - 1 + SEQ, :]
        g2 = pad_ref[PAD - 2:PAD - 2 + SEQ, :]
        c = cb_ref[...] + cw_ref[0:1, :] * g2 + cw_ref[1:2, :] * g1 + cw_ref[2:3, :] * g0
        gg, dgg = _gelu_and_grad(c)
        dgu = dgu_ref[...].astype(F32)
        dup_ref[...] = (dgu * gg).astype(BF16)
        dc = dgu * up_ref[...] * dgg
        dpad_ref[SEQ:SEQ + PAD, :] = jnp.zeros((PAD, LANE), F32)
        dpad_ref[0:SEQ, :] = dc
        dgp_ref[...] = (cw_ref[2:3, :] * dc + cw_ref[1:2, :] * dpad_ref[1:1 + SEQ, :]
                        + cw_ref[0:1, :] * dpad_ref[2:2 + SEQ, :]).astype(BF16)
        dcw = jnp.concatenate([jnp.sum(dc * g2, axis=0, keepdims=True), jnp.sum(dc * g1, axis=0, keepdims=True),
                               jnp.sum(dc * g0, axis=0, keepdims=True)], axis=0)
        dcb = jnp.sum(dc, axis=0, keepdims=True)

        @pl.when(b == 0)
        def _():
            dcw_ref[...] = dcw
            dcb_ref[...] = dcb

        @pl.when(b > 0)
        def _():
            dcw_ref[...] += dcw
            dcb_ref[...] += dcb

    blk = pl.BlockSpec((SEQ, LANE), lambda j, b: (b, j))
    return _call(
        body, name=name, grid=(f // LANE, nb_local),
        in_specs=[blk, blk, blk, pl.BlockSpec((3, LANE), lambda j, b: (0, j)), pl.BlockSpec((1, LANE), lambda j, b: (0, j))],
        out_specs=[blk, blk, pl.BlockSpec((3, LANE), lambda j, b: (0, j)), pl.BlockSpec((1, LANE), lambda j, b: (0, j))],
        out_shape=[SDS((n, f), BF16), SDS((n, f), BF16), SDS((3, f), F32), SDS((1, f), F32)],
        scratch_shapes=[pltpu.VMEM((SEQ + PAD, LANE), F32), pltpu.VMEM((SEQ + PAD, LANE), F32)],
        args=[dgu, gp, up, cw, cb], sem=("parallel", "arbitrary"), ride=ride)


def norm_bwd_mid(dh2, x1, dout, z2, g3, g2, name, tm=256):
    n, d = x1.shape

    def body(dh_ref, x1_ref, dout_ref, z2_ref, g3_ref, g2_ref, dx1_ref, dz2_ref, dg3_ref, dg2_ref):
        i = pl.program_id(0)
        dxa, dg3r = _rms_bwd(dh_ref[...], x1_ref[...], g3_ref[...])
        dx1 = dout_ref[...] + dxa
        dx1_ref[...] = dx1
        dz2, dg2r = _rms_bwd(dx1, z2_ref[...], g2_ref[...])
        dz2_ref[...] = dz2.astype(BF16)
        s3 = jnp.sum(dg3r, axis=0, keepdims=True)
        s2 = jnp.sum(dg2r, axis=0, keepdims=True)

        @pl.when(i == 0)
        def _():
            dg3_ref[...] = s3
            dg2_ref[...] = s2

        @pl.when(i > 0)
        def _():
            dg3_ref[...] += s3
            dg2_ref[...] += s2

    row = pl.BlockSpec((tm, d), lambda i: (i, 0))
    vec = pl.BlockSpec((1, d), lambda i: (0, 0))
    return pl.pallas_call(
        body, name=name, grid=(n // tm,),
        in_specs=[row, row, row, row, vec, vec],
        out_specs=[row, row, vec, vec],
        out_shape=[SDS((n, d), F32), SDS((n, d), BF16), SDS((1, d), F32), SDS((1, d), F32)],
        compiler_params=_params(("arbitrary",)),
    )(dh2, x1, dout, z2, g3, g2)


def norm_bwd_in(dh1, x, dx1, g1, name, tm=256):
    n, d = x.shape

    def body(dh_ref, x_ref, dx1_ref, g1_ref, dx_ref, dg1_ref):
        i = pl.program_id(0)
        dxa, dgr = _rms_bwd(dh_ref[...], x_ref[...], g1_ref[...])
        dx_ref[...] = dx1_ref[...] + dxa
        s = jnp.sum(dgr, axis=0, keepdims=True)

        @pl.when(i == 0)
        def _():
            dg1_ref[...] = s

        @pl.when(i > 0)
        def _():
            dg1_ref[...] += s

    row = pl.BlockSpec((tm, d), lambda i: (i, 0))
    vec = pl.BlockSpec((1, d), lambda i: (0, 0))
    return pl.pallas_call(
        body, name=name, grid=(n // tm,),
        in_specs=[row, row, row, vec],
        out_specs=[row, vec],
        out_shape=[SDS((n, d), F32), SDS((1, d), F32)],
        compiler_params=_params(("arbitrary",)),
    )(dh1, x, dx1, g1)


def cast_bf16(arrays, name):
    def body(*refs):
        for i_ref, o_ref in zip(refs[:len(arrays)], refs[len(arrays):]):
            o_ref[...] = i_ref[...].astype(BF16)

    return pl.pallas_call(body, name=name, out_shape=[SDS(a.shape, BF16) for a in arrays],
                          compiler_params=_params())(*arrays)


def adam_update(parts, w, m, v, name, tr=None):
    s, r, c = parts.shape
    tr = r if tr is None else tr
    bc1 = 1.0 - ADAM_B1 ** ADAM_STEP
    bc2 = 1.0 - ADAM_B2 ** ADAM_STEP

    def body(p_ref, w_ref, m_ref, v_ref, g_ref, d_ref, nm_ref, nv_ref):
        g = p_ref[0].astype(F32)
        for j in range(1, s):
            g = g + p_ref[j].astype(F32)
        nm = ADAM_B1 * m_ref[...] + (1.0 - ADAM_B1) * g
        nv = ADAM_B2 * v_ref[...] + (1.0 - ADAM_B2) * (g * g)
        g_ref[...] = g
        nm_ref[...] = nm
        nv_ref[...] = nv
        d_ref[...] = -ADAM_LR * ((nm / bc1) / (jnp.sqrt(nv / bc2) + ADAM_EPS) + ADAM_WD * w_ref[...])

    blk = pl.BlockSpec((tr, c), lambda i: (i, 0))
    return pl.pallas_call(
        body, name=name, grid=(r // tr,),
        in_specs=[pl.BlockSpec((s, tr, c), lambda i: (0, i, 0)), blk, blk, blk],
        out_specs=[blk] * 4, out_shape=[SDS((r, c), F32)] * 4,
        compiler_params=_params(("parallel",)),
    )(parts, w, m, v)


SMALL_NAMES = ("spatial_w", "norm_mix_pre", "norm_mix_post", "norm_ffn_pre", "norm_ffn_post", "conv_b",
               "ln_v_gain", "ln_v_bias", "spatial_b", "rel_bias")
PACK_ROW_ALIGN = 8


def _pack_rows(size):
    rows = -(-size // LANE)
    return -(-rows // PACK_ROW_ALIGN) * PACK_ROW_ALIGN


def _pack(arrays):
    flat = []
    for a in arrays:
        rows = _pack_rows(a.size)
        flat.append(jnp.pad(a.reshape(-1), (0, rows * LANE - a.size)))
    return jnp.concatenate(flat).reshape(-1, LANE)


def _unpack(packed, shapes):
    out, row = [], 0
    for shp in shapes:
        size = int(np.prod(shp))
        out.append(packed[row:row + _pack_rows(size)].reshape(-1)[:size].reshape(shp))
        row += _pack_rows(size)
    return out


def kernel(x, norm_mix_pre, norm_mix_post, norm_ffn_pre, norm_ffn_post, w_in, ln_v_gain, ln_v_bias, spatial_w, spatial_b, rel_bias, w_out, w_gate, w_up, conv_w, conv_b, w_down, loss_target, m_norm_mix_pre, m_norm_mix_post, m_norm_ffn_pre, m_norm_ffn_post, m_w_in, m_ln_v_gain, m_ln_v_bias, m_spatial_w, m_spatial_b, m_rel_bias, m_w_out, m_w_gate, m_w_up, m_conv_w, m_conv_b, m_w_down, v_norm_mix_pre, v_norm_mix_post, v_norm_ffn_pre, v_norm_ffn_post, v_w_in, v_ln_v_gain, v_ln_v_bias, v_spatial_w, v_spatial_b, v_rel_bias, v_w_out, v_w_gate, v_w_up, v_conv_w, v_conv_b, v_w_down):
    given = dict(locals())
    nb_local, seq, d = x.shape
    n = nb_local * seq
    cols = w_in.shape[2]

    def by_columns(g):
        return g.transpose(1, 0, 2).reshape(g.shape[1], N_DEV * g.shape[2])

    def to_blocks(g):
        return g.reshape(g.shape[0], N_DEV, cols).transpose(1, 0, 2)

    xf, target = x.reshape(n, d), loss_target.reshape(n, d)
    ln_g, ln_b = ln_v_gain.reshape(1, A_WIDTH), ln_v_bias.reshape(1, A_WIDTH)
    spatial_bt, rel_bias_t = spatial_b[0].T, rel_bias.T

    s_in, s_out, s_gate, s_up, s_down = cast_bf16([w_in[0], w_out[0], w_gate[0], w_up[0], w_down[0]], "cast_shards")
    g_in, g_cw = exchange([], [s_in, conv_w[0]], "gather_w_in")
    w_in_f, conv_w_f = by_columns(g_in), by_columns(g_cw)

    (h1, proj), _ = norm_mm(xf, norm_mix_pre, [w_in_f], "fwd_norm_in")
    a = gating_fwd(proj, ln_g, ln_b, spatial_w[0], spatial_bt, "fwd_gating")
    bias = bias_tables(rel_bias_t, "bias_tables").reshape(len(DILATIONS), B_HEADS, ATTN_BLOCK, 2 * ATTN_BLOCK)
    (b_out, lse_tot), (g_out, g_gate, g_up) = attn_fwd(proj, bias, nb_local, "fwd_attn",
                                                       ride=([], [s_out, s_gate, s_up]))
    w_out_f, w_gate_f, w_up_f = g_out.reshape(D_MODEL, D_MODEL), by_columns(g_gate), by_columns(g_up)
    z2, x1 = mm_res_norm([a, b_out], w_out_f, xf, norm_mix_post, "fwd_out_norm")
    (h2, gp, up), (g_down,) = norm_mm(x1, norm_ffn_pre, [w_gate_f, w_up_f], "fwd_norm_ffn", tm=256,
                                      ride=([], [s_down]))
    w_down_f = g_down.reshape(D_FF, D_MODEL)
    gu = conv_gelu_fwd(gp, up, conv_w_f, conv_b, nb_local, "fwd_conv_gelu")
    dy, dout, dg4, loss_part = down_loss(gu, w_down_f, x1, norm_ffn_post, target, "fwd_down_loss")

    p_down = mm_tn([gu], [dy], "bwd_dw_down", t1=1408)
    (dgu,), _ = mm_nt([(dy, 0, 0)], [w_down_f], "bwd_dgu", out_dtype=BF16)
    (dgp, dup, p_conv_w, p_conv_b), (r_down,) = conv_gelu_bwd(
        dgu, gp, up, conv_w_f, conv_b, nb_local, "bwd_conv_gelu", ride=([p_down.reshape(N_DEV, cols, D_MODEL)], []))
    p_gate = mm_tn([h2], [dgp], "bwd_dw_gate")
    p_up = mm_tn([h2], [dup], "bwd_dw_up")
    (dh2,), _ = mm_nt([(dgp, 0, 0), (dup, 1, 0)], [w_gate_f, w_up_f], "bwd_dh2", tm=256)
    dx1, dz2, dg3, dg2 = norm_bwd_mid(dh2, x1, dout, z2, norm_ffn_pre, norm_mix_post, "bwd_norm_mid")
    p_out = mm_tn([a, b_out], [dz2], "bwd_dw_out")
    (dmix,), _ = mm_nt([(dz2, 0, 0)], [w_out_f], "bwd_dmix")
    duv, p_ws, p_sbt, p_lng, p_lnb = gating_bwd(proj, dmix, ln_g, ln_b, spatial_w[0], spatial_bt, "bwd_gating")
    (dq, dk, dv, dbias), (r_gate, r_up, r_out) = attn_bwd(
        proj, b_out, dmix, lse_tot, bias, nb_local, "bwd_attn",
        ride=([to_blocks(p_gate), to_blocks(p_up), p_out.reshape(N_DEV, D_MODEL // N_DEV, D_MODEL)], []))
    p_rel_bias_t = rel_bias_grad(dbias.reshape(len(DILATIONS), B_HEADS, BIAS_SIZE), "bwd_rel_bias")
    p_in = mm_tn([h1], [duv, dq, dk, dv], "bwd_dw_in")
    (dh1,), (r_in,) = mm_nt([(duv, 0, 0), (dq, 0, Q_OFF), (dk, 0, K_OFF), (dv, 0, V_OFF)], [w_in_f], "bwd_dh1",
                            tm=256, ride=([to_blocks(p_in)], []))
    grad_x, dg1 = norm_bwd_in(dh1, xf, dx1, norm_mix_pre, "bwd_norm_in")

    part = dict(grad_x=grad_x, conv_w=p_conv_w, loss=loss_part)
    small = dict(spatial_w=p_ws, norm_mix_pre=dg1, norm_mix_post=dg2, norm_ffn_pre=dg3, norm_ffn_post=dg4,
                 conv_b=p_conv_b, ln_v_gain=p_lng, ln_v_bias=p_lnb, spatial_b=p_sbt.T, rel_bias=p_rel_bias_t.T)
    pack = _pack([small[k] for k in SMALL_NAMES] + [part["conv_w"], part["loss"]])
    (r_small,) = exchange([], [pack], "exchange_small")

    res = {}
    res["w_in"] = adam_update(r_in, w_in[0], m_w_in[0], v_w_in[0], "adam_w_in", tr=256)
    res["w_out"] = adam_update(r_out, w_out[0], m_w_out[0], v_w_out[0], "adam_w_out")
    res["w_gate"] = adam_update(r_gate, w_gate[0], m_w_gate[0], v_w_gate[0], "adam_w_gate", tr=256)
    res["w_up"] = adam_update(r_up, w_up[0], m_w_up[0], v_w_up[0], "adam_w_up", tr=256)
    res["w_down"] = adam_update(r_down, w_down[0], m_w_down[0], v_w_down[0], "adam_w_down", tr=176)

    tail = [jnp.zeros_like(part["conv_w"]), jnp.zeros_like(part["loss"])]
    packs = [_pack([given[pre + k] for k in SMALL_NAMES] + tail) for pre in ("", "m_", "v_")]
    small_res = adam_update(r_small, *packs, "adam_small")
    small_shapes = [given[k].shape for k in SMALL_NAMES] + [part["conv_w"].shape, part["loss"].shape]
    unpacked = [_unpack(p, small_shapes) for p in small_res]
    for i, k in enumerate(SMALL_NAMES):
        res[k] = [u[i] for u in unpacked]
    g_conv_w = lax.dynamic_slice_in_dim(unpacked[0][len(SMALL_NAMES)], _my_index() * cols, cols, axis=1)
    res["conv_w"] = adam_update(g_conv_w[None], conv_w[0], m_conv_w[0], v_conv_w[0], "adam_conv_w")
    loss = unpacked[0][len(SMALL_NAMES) + 1][0, 0]

    names = ("norm_mix_pre", "norm_mix_post", "norm_ffn_pre", "norm_ffn_post", "w_in", "ln_v_gain", "ln_v_bias",
             "spatial_w", "spatial_b", "rel_bias", "w_out", "w_gate", "w_up", "conv_w", "conv_b", "w_down")
    outs = [loss, part["grad_x"].reshape(x.shape)]
    for t in range(4):
        outs += [res[k][t].reshape(given[k].shape) for k in names]
    return tuple(outs)
```

```python
import functools
import math

import numpy as np
import jax
import jax.numpy as jnp
from jax import lax
from jax.experimental import pallas as pl
from jax.experimental.pallas import tpu as pltpu

F32 = jnp.float32
BF16 = jnp.bfloat16
SDS = jax.ShapeDtypeStruct

D_MODEL = 1024
SEQ = 2048
HEAD_DIM = 64
A_GROUPS = 4
A_WIDTH = A_GROUPS * HEAD_DIM
B_HEADS = 12
B_WIDTH = B_HEADS * HEAD_DIM
HEAD_PAIRS = B_HEADS // 2
CHUNK = 128
ATTN_BLOCK = 128
DILATIONS = (1, 4, 16)
NUM_BUCKETS = 32
MAX_DISTANCE = 2048
D_FF = 2816
IN_COLS = 2 * A_WIDTH + 3 * B_WIDTH
Q_OFF = 2 * A_WIDTH
K_OFF = Q_OFF + B_WIDTH
V_OFF = K_OFF + B_WIDTH
NORM_EPS = 1e-6
NEG_INF = -1e30
N_DEV = 8
LANE = 128

ADAM_LR = 0.001
ADAM_B1 = 0.9
ADAM_B2 = 0.999
ADAM_EPS = 1e-08
ADAM_WD = 0.01
ADAM_STEP = 10

GELU_C0 = math.sqrt(2.0 / math.pi)
GELU_C1 = 0.044715

VMEM_LIMIT = 56 * 1024 * 1024


def _params(sem=None):
    if sem is None:
        return pltpu.CompilerParams(vmem_limit_bytes=VMEM_LIMIT)
    return pltpu.CompilerParams(dimension_semantics=sem, vmem_limit_bytes=VMEM_LIMIT)


def _gelu(x):
    t = jnp.tanh(GELU_C0 * (x + GELU_C1 * x * x * x))
    return 0.5 * x * (1.0 + t)


def _gelu_and_grad(x):
    x2 = x * x
    t = jnp.tanh(GELU_C0 * (x + GELU_C1 * x * x2))
    g = 0.5 * x * (1.0 + t)
    dg = 0.5 * (1.0 + t) + 0.5 * x * (1.0 - t * t) * (GELU_C0 * (1.0 + 3.0 * GELU_C1 * x2))
    return g, dg


def _dot(a, b):
    return jnp.dot(a, b, preferred_element_type=F32)


def _dot_nt(a, b):
    return lax.dot_general(a, b, (((1,), (1,)), ((), ())), preferred_element_type=F32)


def _dot_tn(a, b):
    return lax.dot_general(a, b, (((0,), (0,)), ((), ())), preferred_element_type=F32)


def _rms_bwd(d, xin, g):
    r = lax.rsqrt(jnp.mean(xin * xin, axis=-1, keepdims=True) + NORM_EPS)
    xh = xin * r
    gd = g * d
    dx = r * (gd - xh * jnp.mean(gd * xh, axis=-1, keepdims=True))
    return dx, d * xh


MESH = pl.DeviceIdType.MESH
ANY = pl.BlockSpec(memory_space=pl.ANY)
PEER_MASKS = tuple(range(1, N_DEV))


def _my_index():
    return lax.axis_index("x") * 4 + lax.axis_index("y") * 2 + lax.axis_index("c")


def _peer(mask):
    x, y, c = lax.axis_index("x"), lax.axis_index("y"), lax.axis_index("c")
    px = 1 - x if mask & 4 else x
    py = 1 - y if mask & 2 else y
    pc = 1 - c if mask & 1 else c
    return (px, py, pc), px * 4 + py * 2 + pc


def _exchange_copies(nblocked, in_refs, out_refs, sems):
    send_sems, recv_sems, local_sems = sems
    me = _my_index()
    local, sends, recvs = [], [], []
    for a, (in_ref, out_ref) in enumerate(zip(in_refs, out_refs)):
        src_of = (lambda idx, r=in_ref: r.at[idx]) if a < nblocked else (lambda idx, r=in_ref: r)
        local.append(pltpu.make_async_copy(src_of(me), out_ref.at[me], local_sems.at[a]))
        for mask in PEER_MASKS:
            peer, pidx = _peer(mask)
            pair = dict(send_sem=send_sems.at[a, mask - 1], recv_sem=recv_sems.at[a, mask - 1],
                        device_id=peer, device_id_type=MESH)
            sends.append(pltpu.make_async_remote_copy(src_ref=src_of(pidx), dst_ref=out_ref.at[me], **pair))
            recvs.append(pltpu.make_async_remote_copy(src_ref=src_of(pidx), dst_ref=out_ref.at[pidx], **pair))
    return local, sends, recvs


def _exchange_start(nblocked, in_refs, out_refs, sems):
    local, sends, _ = _exchange_copies(nblocked, in_refs, out_refs, sems)
    for cp in local + sends:
        cp.start()


def _exchange_wait(nblocked, in_refs, out_refs, sems):
    local, sends, recvs = _exchange_copies(nblocked, in_refs, out_refs, sems)
    for cp in sends:
        cp.wait_send()
    for cp in recvs:
        cp.wait_recv()
    for cp in local:
        cp.wait()


def _exchange_out_shape(blocked, whole):
    return [SDS(b.shape, b.dtype) for b in blocked] + [SDS((N_DEV,) + w.shape, w.dtype) for w in whole]


def _exchange_sems(n):
    return [pltpu.SemaphoreType.DMA((n, N_DEV - 1)), pltpu.SemaphoreType.DMA((n, N_DEV - 1)),
            pltpu.SemaphoreType.DMA((n,))]


def exchange(blocked, whole, name):
    nb, n = len(blocked), len(blocked) + len(whole)

    def body(*refs):
        _exchange_start(nb, refs[:n], refs[n:2 * n], refs[2 * n:])
        _exchange_wait(nb, refs[:n], refs[n:2 * n], refs[2 * n:])

    return pl.pallas_call(
        body, name=name, in_specs=[ANY] * n, out_specs=[ANY] * n, out_shape=_exchange_out_shape(blocked, whole),
        scratch_shapes=_exchange_sems(n),
    )(*blocked, *whole)


def _call(body, *, name, grid, in_specs, out_specs, out_shape, args, scratch_shapes=(), sem=None, ride=None):
    out_shape, out_specs, scratch_shapes = list(out_shape), list(out_specs), list(scratch_shapes)
    if ride is None:
        outs = pl.pallas_call(body, name=name, grid=grid, in_specs=list(in_specs), out_specs=out_specs,
                              out_shape=out_shape, scratch_shapes=scratch_shapes,
                              compiler_params=_params(sem))(*args)
        return list(outs), []
    blocked, whole = ride
    cargs = list(blocked) + list(whole)
    nb, nc = len(blocked), len(cargs)
    n_in, n_out, n_scr = len(args), len(out_shape), len(scratch_shapes)

    def riding(*refs):
        ins, refs = refs[:n_in], refs[n_in:]
        cins, refs = refs[:nc], refs[nc:]
        outs, refs = refs[:n_out], refs[n_out:]
        couts, refs = refs[:nc], refs[nc:]
        scr, sems = refs[:n_scr], refs[n_scr:]
        ids = [pl.program_id(k) for k in range(len(grid))]
        first = functools.reduce(lambda p, q: p & q, [i == 0 for i in ids])
        last = functools.reduce(lambda p, q: p & q, [i == g - 1 for i, g in zip(ids, grid)])

        @pl.when(first)
        def _():
            _exchange_start(nb, cins, couts, sems)

        body(*ins, *outs, *scr)

        @pl.when(last)
        def _():
            _exchange_wait(nb, cins, couts, sems)

    res = pl.pallas_call(
        riding, name=name, grid=grid, in_specs=list(in_specs) + [ANY] * nc, out_specs=out_specs + [ANY] * nc,
        out_shape=out_shape + _exchange_out_shape(blocked, whole),
        scratch_shapes=scratch_shapes + _exchange_sems(nc),
        compiler_params=_params(("arbitrary",) * len(grid)))(*args, *cargs)
    return list(res[:n_out]), list(res[n_out:])


def norm_mm(x, g, ws, name, tm=512, tn=1408, ride=None):
    n, d = x.shape
    f = ws[0].shape[1]
    nw = len(ws)

    def body(x_ref, g_ref, *refs):
        w_refs = refs[:nw]
        h_ref = refs[nw]
        o_refs = refs[nw + 1:]

        @pl.when(pl.program_id(1) == 0)
        def _():
            xv = x_ref[...]
            r = lax.rsqrt(jnp.mean(xv * xv, axis=-1, keepdims=True) + NORM_EPS)
            h_ref[...] = (xv * r * g_ref[...]).astype(BF16)

        h = h_ref[...]
        for w_ref, o_ref in zip(w_refs, o_refs):
            o_ref[...] = _dot(h, w_ref[...])

    return _call(
        body, name=name, grid=(n // tm, f // tn),
        in_specs=[pl.BlockSpec((tm, d), lambda i, j: (i, 0)), pl.BlockSpec((1, d), lambda i, j: (0, 0))]
        + [pl.BlockSpec((d, tn), lambda i, j: (0, j)) for _ in ws],
        out_specs=[pl.BlockSpec((tm, d), lambda i, j: (i, 0))]
        + [pl.BlockSpec((tm, tn), lambda i, j: (i, j)) for _ in ws],
        out_shape=[SDS((n, d), BF16)] + [SDS((n, f), F32) for _ in ws],
        args=[x, g, *ws], sem=("parallel", "arbitrary"), ride=ride)


def _lane_concat(refs):
    vals = [r[...].astype(BF16) for r in refs]
    return vals[0] if len(vals) == 1 else jnp.concatenate(vals, axis=1)


def mm_res_norm(a_list, w, res, g, name, tm=512):
    n = a_list[0].shape[0]
    k, d = w.shape
    na = len(a_list)

    def body(*refs):
        w_ref, res_ref, g_ref, y_ref, o_ref = refs[na:]
        y = _dot(_lane_concat(refs[:na]), w_ref[...])
        r = lax.rsqrt(jnp.mean(y * y, axis=-1, keepdims=True) + NORM_EPS)
        y_ref[...] = y
        o_ref[...] = res_ref[...] + y * r * g_ref[...]

    return pl.pallas_call(
        body, name=name, grid=(n // tm,),
        in_specs=[pl.BlockSpec((tm, a.shape[1]), lambda i: (i, 0)) for a in a_list]
        + [pl.BlockSpec((k, d), lambda i: (0, 0)),
           pl.BlockSpec((tm, d), lambda i: (i, 0)), pl.BlockSpec((1, d), lambda i: (0, 0))],
        out_specs=[pl.BlockSpec((tm, d), lambda i: (i, 0)), pl.BlockSpec((tm, d), lambda i: (i, 0))],
        out_shape=[SDS((n, d), F32), SDS((n, d), F32)],
        compiler_params=_params(("parallel",)),
    )(*a_list, w, res, g)


def down_loss(a, w, res, g, target, name, tm=256):
    n, k = a.shape
    d = w.shape[1]
    inv_d = 1.0 / d

    def body(a_ref, w_ref, res_ref, g_ref, t_ref, dy_ref, dout_ref, dg_ref, loss_ref):
        i = pl.program_id(0)
        y = _dot(a_ref[...], w_ref[...])
        gv = g_ref[...]
        r = lax.rsqrt(jnp.mean(y * y, axis=-1, keepdims=True) + NORM_EPS)
        yh = y * r
        e = res_ref[...] + yh * gv - t_ref[...]
        part = 0.5 * inv_d * jnp.sum(jnp.sum(e * e, axis=-1, keepdims=True), axis=0, keepdims=True)
        dout = e * inv_d
        dout_ref[...] = dout
        gd = gv * dout
        dy_ref[...] = (r * (gd - yh * jnp.mean(gd * yh, axis=-1, keepdims=True))).astype(BF16)
        dgp = jnp.sum(dout * yh, axis=0, keepdims=True)
        lane0 = lax.broadcasted_iota(jnp.int32, (1, LANE), 1) == 0
        lp = jnp.where(lane0, part, 0.0)

        @pl.when(i == 0)
        def _():
            dg_ref[...] = dgp
            loss_ref[...] = lp

        @pl.when(i > 0)
        def _():
            dg_ref[...] += dgp
            loss_ref[...] += lp

    return pl.pallas_call(
        body, name=name, grid=(n // tm,),
        in_specs=[pl.BlockSpec((tm, k), lambda i: (i, 0)), pl.BlockSpec((k, d), lambda i: (0, 0)),
                  pl.BlockSpec((tm, d), lambda i: (i, 0)), pl.BlockSpec((1, d), lambda i: (0, 0)),
                  pl.BlockSpec((tm, d), lambda i: (i, 0))],
        out_specs=[pl.BlockSpec((tm, d), lambda i: (i, 0)), pl.BlockSpec((tm, d), lambda i: (i, 0)),
                   pl.BlockSpec((1, d), lambda i: (0, 0)), pl.BlockSpec((1, LANE), lambda i: (0, 0))],
        out_shape=[SDS((n, d), BF16), SDS((n, d), F32), SDS((1, d), F32), SDS((1, LANE), F32)],
        compiler_params=_params(("arbitrary",)),
    )(a, w, res, g, target)


def mm_nt(terms, ws, name, tm=512, out_dtype=F32, ride=None):
    n = terms[0][0].shape[0]
    r = ws[0].shape[0]
    na = len(terms)
    meta = [(widx, off, a.shape[1]) for a, widx, off in terms]

    def body(*refs):
        a_refs = refs[:na]
        w_refs = refs[na:na + len(ws)]
        o_ref = refs[-1]
        acc = None
        for a_ref, (widx, off, k) in zip(a_refs, meta):
            p = _dot_nt(a_ref[...].astype(BF16), w_refs[widx][:, off:off + k])
            acc = p if acc is None else acc + p
        o_ref[...] = acc.astype(out_dtype)

    return _call(
        body, name=name, grid=(n // tm,),
        in_specs=[pl.BlockSpec((tm, a.shape[1]), lambda i: (i, 0)) for a, _, _ in terms]
        + [pl.BlockSpec(w.shape, lambda i: (0, 0)) for w in ws],
        out_specs=[pl.BlockSpec((tm, r), lambda i: (i, 0))],
        out_shape=[SDS((n, r), out_dtype)],
        args=[a for a, _, _ in terms] + list(ws), sem=("parallel",), ride=ride)


def mm_tn(lhs_list, rhs_list, name, t1=512, tn=512, out_dtype=BF16):
    n = lhs_list[0].shape[0]
    k1 = sum(l.shape[1] for l in lhs_list)
    nl = len(lhs_list)
    if nl > 1:
        t1 = k1
    widths = [r.shape[1] for r in rhs_list]
    k2 = sum(widths)
    nr = len(rhs_list)
    nk = n // tn

    def body(*refs):
        r_refs = refs[nl:nl + nr]
        o_ref = refs[nl + nr]
        acc = refs[nl + nr + 1]
        k = pl.program_id(1)
        a = _lane_concat(refs[:nl])
        parts = [_dot_tn(a, r_ref[...].astype(BF16)) for r_ref in r_refs]
        val = parts[0] if nr == 1 else jnp.concatenate(parts, axis=1)

        @pl.when(k == 0)
        def _():
            acc[...] = val

        @pl.when(k > 0)
        def _():
            acc[...] += val

        @pl.when(k == nk - 1)
        def _():
            o_ref[...] = acc[...].astype(out_dtype)

    if nl == 1:
        lhs_specs = [pl.BlockSpec((tn, t1), lambda i, k: (k, i))]
    else:
        lhs_specs = [pl.BlockSpec((tn, l.shape[1]), lambda i, k: (k, 0)) for l in lhs_list]
    return pl.pallas_call(
        body, name=name, grid=(k1 // t1, nk),
        in_specs=lhs_specs + [pl.BlockSpec((tn, w), lambda i, k: (k, 0)) for w in widths],
        out_specs=pl.BlockSpec((t1, k2), lambda i, k: (i, 0)),
        out_shape=SDS((k1, k2), out_dtype),
        scratch_shapes=[pltpu.VMEM((t1, k2), F32)],
        compiler_params=_params(("parallel", "arbitrary")),
    )(*lhs_list, *rhs_list)


GATE_ROWS = 512


def _tril_mask():
    row = lax.broadcasted_iota(jnp.int32, (CHUNK, CHUNK), 0)
    col = lax.broadcasted_iota(jnp.int32, (CHUNK, CHUNK), 1)
    return row >= col


def _layer_norm_parts(gv):
    mu = jnp.mean(gv, axis=-1, keepdims=True)
    xc = gv - mu
    rstd = lax.rsqrt(jnp.mean(xc * xc, axis=-1, keepdims=True) + NORM_EPS)
    return xc * rstd, rstd


def gating_fwd(proj, lng, lnb, ws, sbt, name):
    n = proj.shape[0]
    nchunk = GATE_ROWS // CHUNK

    def body(u_ref, v_ref, lng_ref, lnb_ref, ws_ref, sbt_ref, a_ref):
        tril = _tril_mask()
        for g in range(A_GROUPS):
            cs = slice(g * HEAD_DIM, (g + 1) * HEAD_DIM)
            wt = jnp.where(tril, ws_ref[g], 0.0).astype(BF16)
            for c in range(nchunk):
                rs_ = slice(c * CHUNK, (c + 1) * CHUNK)
                vhat, _ = _layer_norm_parts(_gelu(v_ref[rs_, cs]))
                vn = vhat * lng_ref[:, cs] + lnb_ref[:, cs]
                z = _dot(wt, vn.astype(BF16)) + sbt_ref[:, g:g + 1]
                a_ref[rs_, cs] = _gelu(u_ref[rs_, cs]) * z

    return pl.pallas_call(
        body, name=name, grid=(n // GATE_ROWS,),
        in_specs=[pl.BlockSpec((GATE_ROWS, A_WIDTH), lambda i: (i, 0)),
                  pl.BlockSpec((GATE_ROWS, A_WIDTH), lambda i: (i, 1)),
                  pl.BlockSpec((1, A_WIDTH), lambda i: (0, 0)), pl.BlockSpec((1, A_WIDTH), lambda i: (0, 0)),
                  pl.BlockSpec((A_GROUPS, CHUNK, CHUNK), lambda i: (0, 0, 0)),
                  pl.BlockSpec((CHUNK, A_GROUPS), lambda i: (0, 0))],
        out_specs=pl.BlockSpec((GATE_ROWS, A_WIDTH), lambda i: (i, 0)),
        out_shape=SDS((n, A_WIDTH), F32),
        compiler_params=_params(("parallel",)),
    )(proj, proj, lng, lnb, ws, sbt)


def gating_bwd(proj, dmix, lng, lnb, ws, sbt, name):
    n = proj.shape[0]
    nchunk = GATE_ROWS // CHUNK

    def body(u_ref, v_ref, da_ref, lng_ref, lnb_ref, ws_ref, sbt_ref,
             duv_ref, dws_ref, dsbt_ref, dlng_ref, dlnb_ref):
        @pl.when(pl.program_id(0) == 0)
        def _():
            dws_ref[...] = jnp.zeros_like(dws_ref)
            dsbt_ref[...] = jnp.zeros_like(dsbt_ref)
            dlng_ref[...] = jnp.zeros_like(dlng_ref)
            dlnb_ref[...] = jnp.zeros_like(dlnb_ref)

        tril = _tril_mask()
        for g in range(A_GROUPS):
            cs = slice(g * HEAD_DIM, (g + 1) * HEAD_DIM)
            wt = jnp.where(tril, ws_ref[g], 0.0).astype(BF16)
            lg = lng_ref[:, cs]
            dw = jnp.zeros((CHUNK, CHUNK), F32)
            dsb = jnp.zeros((CHUNK, 1), F32)
            dlg = jnp.zeros((1, HEAD_DIM), F32)
            dlb = jnp.zeros((1, HEAD_DIM), F32)
            for c in range(nchunk):
                rs_ = slice(c * CHUNK, (c + 1) * CHUNK)
                gu, dgu_dx = _gelu_and_grad(u_ref[rs_, cs])
                gv, dgv_dx = _gelu_and_grad(v_ref[rs_, cs])
                vhat, rstd = _layer_norm_parts(gv)
                vn = (vhat * lg + lnb_ref[:, cs]).astype(BF16)
                z = _dot(wt, vn) + sbt_ref[:, g:g + 1]
                da = da_ref[rs_, cs]
                dz = da * gu
                dzb = dz.astype(BF16)
                duv_ref[rs_, cs] = da * z * dgu_dx
                dsb = dsb + jnp.sum(dz, axis=-1, keepdims=True)
                dw = dw + _dot_nt(dzb, vn)
                dvn = _dot_tn(wt, dzb)
                dlg = dlg + jnp.sum(dvn * vhat, axis=0, keepdims=True)
                dlb = dlb + jnp.sum(dvn, axis=0, keepdims=True)
                dvh = dvn * lg
                dgv = rstd * (dvh - jnp.mean(dvh, axis=-1, keepdims=True)
                              - vhat * jnp.mean(dvh * vhat, axis=-1, keepdims=True))
                duv_ref[rs_, A_WIDTH + g * HEAD_DIM:A_WIDTH + (g + 1) * HEAD_DIM] = dgv * dgv_dx
            dws_ref[g] += jnp.where(tril, dw, 0.0)
            dsbt_ref[:, g:g + 1] += dsb
            dlng_ref[:, cs] += dlg
            dlnb_ref[:, cs] += dlb

    return pl.pallas_call(
        body, name=name, grid=(n // GATE_ROWS,),
        in_specs=[pl.BlockSpec((GATE_ROWS, A_WIDTH), lambda i: (i, 0)),
                  pl.BlockSpec((GATE_ROWS, A_WIDTH), lambda i: (i, 1)),
                  pl.BlockSpec((GATE_ROWS, A_WIDTH), lambda i: (i, 0)),
                  pl.BlockSpec((1, A_WIDTH), lambda i: (0, 0)), pl.BlockSpec((1, A_WIDTH), lambda i: (0, 0)),
                  pl.BlockSpec((A_GROUPS, CHUNK, CHUNK), lambda i: (0, 0, 0)),
                  pl.BlockSpec((CHUNK, A_GROUPS), lambda i: (0, 0))],
        out_specs=[pl.BlockSpec((GATE_ROWS, 2 * A_WIDTH), lambda i: (i, 0)),
                   pl.BlockSpec((A_GROUPS, CHUNK, CHUNK), lambda i: (0, 0, 0)),
                   pl.BlockSpec((CHUNK, A_GROUPS), lambda i: (0, 0)),
                   pl.BlockSpec((1, A_WIDTH), lambda i: (0, 0)), pl.BlockSpec((1, A_WIDTH), lambda i: (0, 0))],
        out_shape=[SDS((n, 2 * A_WIDTH), F32), SDS((A_GROUPS, CHUNK, CHUNK), F32), SDS((CHUNK, A_GROUPS), F32),
                   SDS((1, A_WIDTH), F32), SDS((1, A_WIDTH), F32)],
        compiler_params=_params(("arbitrary",)),
    )(proj, proj, dmix, lng, lnb, ws, sbt)


def _t5_bucket_np(dist):
    max_exact = NUM_BUCKETS // 2
    dd = np.maximum(dist, 1).astype(np.float64)
    large = max_exact + np.log(dd / max_exact) / math.log(MAX_DISTANCE / max_exact) * (NUM_BUCKETS - max_exact)
    large = np.minimum(large.astype(np.int64), NUM_BUCKETS - 1)
    return np.where(dist < max_exact, dist, large)


def _bucket_tables():
    i = np.arange(ATTN_BLOCK)[:, None]
    j = np.arange(2 * ATTN_BLOCK)[None, :]
    rel = ATTN_BLOCK + i - j
    band = (rel >= 0) & (rel <= ATTN_BLOCK)
    tabs = []
    for dil in DILATIONS:
        b = _t5_bucket_np(np.maximum(rel, 0) * dil)
        tabs.append(np.where(band, b, -1).reshape(1, -1))
    return np.stack(tabs).astype(np.float32)


BIAS_SIZE = ATTN_BLOCK * 2 * ATTN_BLOCK


def bias_tables(rel_bias_t, name):
    idx = jnp.asarray(_bucket_tables())

    def body(rb_ref, idx_ref, o_ref):
        iv = idx_ref[0]
        bk = lax.broadcasted_iota(jnp.int32, (NUM_BUCKETS, BIAS_SIZE), 0).astype(F32)
        onehot = (bk == iv).astype(F32)
        t = jnp.dot(rb_ref[...], onehot, preferred_element_type=F32, precision=lax.Precision.HIGHEST)
        o_ref[0] = jnp.where(iv < 0.0, NEG_INF, t)

    return pl.pallas_call(
        body, name=name, grid=(len(DILATIONS),),
        in_specs=[pl.BlockSpec((B_HEADS, NUM_BUCKETS), lambda d: (0, 0)),
                  pl.BlockSpec((1, 1, BIAS_SIZE), lambda d: (d, 0, 0))],
        out_specs=pl.BlockSpec((1, B_HEADS, BIAS_SIZE), lambda d: (d, 0, 0)),
        out_shape=SDS((len(DILATIONS), B_HEADS, BIAS_SIZE), F32),
        compiler_params=_params(("parallel",)),
    )(rel_bias_t, idx)


def rel_bias_grad(dbias, name):
    idx = jnp.asarray(_bucket_tables())

    def body(db_ref, idx_ref, o_ref):
        d = pl.program_id(0)
        iv = idx_ref[0]
        bk = lax.broadcasted_iota(jnp.int32, (NUM_BUCKETS, BIAS_SIZE), 0).astype(F32)
        onehot = (bk == iv).astype(F32)
        part = lax.dot_general(db_ref[0], onehot, (((1,), (1,)), ((), ())),
                               preferred_element_type=F32, precision=lax.Precision.HIGHEST)

        @pl.when(d == 0)
        def _():
            o_ref[...] = part

        @pl.when(d > 0)
        def _():
            o_ref[...] += part

    return pl.pallas_call(
        body, name=name, grid=(len(DILATIONS),),
        in_specs=[pl.BlockSpec((1, B_HEADS, BIAS_SIZE), lambda d: (d, 0, 0)),
                  pl.BlockSpec((1, 1, BIAS_SIZE), lambda d: (d, 0, 0))],
        out_specs=pl.BlockSpec((B_HEADS, NUM_BUCKETS), lambda d: (0, 0)),
        out_shape=SDS((B_HEADS, NUM_BUCKETS), F32),
        compiler_params=_params(("arbitrary",)),
    )(dbias, idx)


def _attn_scores(q, kk, bias):
    return _dot_nt(q, kk) * (1.0 / math.sqrt(HEAD_DIM)) + bias


def _head0_lanes():
    return lax.broadcasted_iota(jnp.int32, (ATTN_BLOCK, LANE), 1) < HEAD_DIM


def _one_head(x2, head0, hh):
    return jnp.where(head0 if hh == 0 else jnp.logical_not(head0), x2, 0.0).astype(BF16)


def _rows(start, size, dil):
    return pl.ds(start, size) if dil == 1 else pl.ds(start, size, stride=dil)


def _attn_schedule(op):
    span1, span4 = ATTN_BLOCK, 4 * ATTN_BLOCK

    def d16(i, carry):
        for t in range(4):
            op(2, 16, 4 * i + t, True)
        return carry

    lax.fori_loop(0, 4, d16, 0)
    for r in range(4):
        op(1, 4, r, True)

    def d4(nq, carry):
        for r in range(4):
            op(1, 4, r + nq * span4, False)
        return carry

    lax.fori_loop(1, SEQ // span4, d4, 0)
    op(0, 1, 0, True)

    def d1(j, carry):
        for t in range(3):
            op(0, 1, pl.multiple_of((1 + 3 * j + t) * span1, span1), False)
        return carry

    lax.fori_loop(0, (SEQ // span1 - 1) // 3, d1, 0)


def _kv_rows(start, dil, first):
    if first:
        return _rows(start, ATTN_BLOCK, dil)
    return _rows(start - ATTN_BLOCK * dil, 2 * ATTN_BLOCK, dil)


MERGE_ROWS = 256


def attn_fwd(proj, bias, nb_local, name, ride=None):
    n = proj.shape[0]
    nseg = len(DILATIONS)

    def body(q_ref, k_ref, v_ref, b_ref, o_ref, lse_ref, os_ref, ls_ref):
        def op(seg, dil, start, first):
            qrows = _rows(start, ATTN_BLOCK, dil)
            krows = _kv_rows(start, dil, first)
            q2, kb, vb = q_ref[qrows, :], k_ref[krows, :].astype(BF16), v_ref[krows, :].astype(BF16)
            head0 = _head0_lanes()
            outs, lses = [], []
            for hh in range(2):
                bb = b_ref[seg, hh, :, ATTN_BLOCK:] if first else b_ref[seg, hh]
                s = _attn_scores(_one_head(q2, head0, hh), kb, bb)
                m = jnp.max(s, axis=-1, keepdims=True)
                p = jnp.exp(s - m)
                l = jnp.sum(p, axis=-1, keepdims=True)
                outs.append(_dot(p.astype(BF16), vb) / l)
                lses.append(jnp.broadcast_to(m + jnp.log(l), (ATTN_BLOCK, LANE)))
            os_ref[seg, qrows, :] = jnp.where(head0, outs[0], outs[1])
            ls_ref[seg, qrows, :] = jnp.where(head0, lses[0], lses[1])

        _attn_schedule(op)

        def merge(i, carry):
            rows = pl.ds(pl.multiple_of(i * MERGE_ROWS, MERGE_ROWS), MERGE_ROWS)
            ls = [ls_ref[s, rows, :] for s in range(nseg)]
            m = functools.reduce(jnp.maximum, ls)
            ws = [jnp.exp(l - m) for l in ls]
            den = ws[0] + ws[1] + ws[2]
            num = ws[0] * os_ref[0, rows, :] + ws[1] * os_ref[1, rows, :] + ws[2] * os_ref[2, rows, :]
            o_ref[rows, :] = num / den
            lse_ref[rows, :] = m + jnp.log(den)
            return carry

        lax.fori_loop(0, SEQ // MERGE_ROWS, merge, 0)

    def in_spec(off):
        return pl.BlockSpec((SEQ, LANE), lambda b, p: (b, off // LANE + p))

    out_spec = pl.BlockSpec((SEQ, LANE), lambda b, p: (b, p))
    return _call(
        body, name=name, grid=(nb_local, HEAD_PAIRS),
        in_specs=[in_spec(Q_OFF), in_spec(K_OFF), in_spec(V_OFF),
                  pl.BlockSpec((nseg, 2, ATTN_BLOCK, 2 * ATTN_BLOCK), lambda b, p: (0, p, 0, 0))],
        out_specs=[out_spec, out_spec],
        out_shape=[SDS((n, B_WIDTH), F32), SDS((n, B_WIDTH), F32)],
        scratch_shapes=[pltpu.VMEM((nseg, SEQ, LANE), F32), pltpu.VMEM((nseg, SEQ, LANE), F32)],
        args=[proj, proj, proj, bias], sem=("parallel", "arbitrary"), ride=ride)


def attn_bwd(proj, b_out, dmix, lse_tot, bias, nb_local, name, ride=None):
    n = proj.shape[0]
    nseg = len(DILATIONS)
    a_blocks = A_WIDTH // LANE
    scale = 1.0 / math.sqrt(HEAD_DIM)

    def body(q_ref, k_ref, v_ref, o_ref, do_ref, lse_ref, b_ref, dq_ref, dk_ref, dv_ref, db_ref, dqs_ref, delta_ref):
        @pl.when(pl.program_id(1) == 0)
        def _():
            db_ref[...] = jnp.zeros_like(db_ref)

        dk_ref[...] = jnp.zeros_like(dk_ref)
        dv_ref[...] = jnp.zeros_like(dv_ref)

        def row_dots(i, carry):
            rows = pl.ds(pl.multiple_of(i * ATTN_BLOCK, ATTN_BLOCK), ATTN_BLOCK)
            head0 = _head0_lanes()
            prod = do_ref[rows, :] * o_ref[rows, :]
            d0 = jnp.sum(jnp.where(head0, prod, 0.0), axis=-1, keepdims=True)
            d1 = jnp.sum(jnp.where(head0, 0.0, prod), axis=-1, keepdims=True)
            delta_ref[rows, :] = jnp.where(head0, d0, d1)
            return carry

        lax.fori_loop(0, SEQ // ATTN_BLOCK, row_dots, 0)

        def op(seg, dil, start, first):
            qrows = _rows(start, ATTN_BLOCK, dil)
            krows = _kv_rows(start, dil, first)
            q2, kb, vb = q_ref[qrows, :], k_ref[krows, :].astype(BF16), v_ref[krows, :].astype(BF16)
            do2, lse2, delta2 = do_ref[qrows, :], lse_ref[qrows, :], delta_ref[qrows, :]
            head0 = _head0_lanes()
            dqs, dk, dv = [], None, None
            for hh in range(2):
                col = slice(hh * HEAD_DIM, hh * HEAD_DIM + 1)
                q, dob = _one_head(q2, head0, hh), _one_head(do2, head0, hh)
                bb = b_ref[seg, hh, :, ATTN_BLOCK:] if first else b_ref[seg, hh]
                p = jnp.exp(_attn_scores(q, kb, bb) - lse2[:, col])
                dvh = _dot_tn(p.astype(BF16), dob)
                ds = p * (_dot_nt(dob, vb) - delta2[:, col])
                if first:
                    db_ref[seg, hh, :, ATTN_BLOCK:] += ds
                else:
                    db_ref[seg, hh] += ds
                dsb = ds.astype(BF16)
                dqs.append(_dot(dsb, kb))
                dkh = _dot_tn(dsb, q)
                dk = dkh if dk is None else dk + dkh
                dv = dvh if dv is None else dv + dvh
            dqs_ref[seg, qrows, :] = jnp.where(head0, dqs[0], dqs[1]) * scale
            dk_ref[krows, :] += dk * scale
            dv_ref[krows, :] += dv

        _attn_schedule(op)

        def merge(i, carry):
            rows = pl.ds(pl.multiple_of(i * MERGE_ROWS, MERGE_ROWS), MERGE_ROWS)
            dq_ref[rows, :] = dqs_ref[0, rows, :] + dqs_ref[1, rows, :] + dqs_ref[2, rows, :]
            return carry

        lax.fori_loop(0, SEQ // MERGE_ROWS, merge, 0)

    def pspec(off):
        return pl.BlockSpec((SEQ, LANE), lambda p, b: (b, off // LANE + p))

    ospec = pl.BlockSpec((SEQ, LANE), lambda p, b: (b, p))
    bspec = pl.BlockSpec((nseg, 2, ATTN_BLOCK, 2 * ATTN_BLOCK), lambda p, b: (0, p, 0, 0))
    gshape = SDS((n, B_WIDTH), F32)
    return _call(
        body, name=name, grid=(HEAD_PAIRS, nb_local),
        in_specs=[pspec(Q_OFF), pspec(K_OFF), pspec(V_OFF), ospec,
                  pl.BlockSpec((SEQ, LANE), lambda p, b: (b, a_blocks + p)), ospec, bspec],
        out_specs=[ospec, ospec, ospec, bspec],
        out_shape=[gshape, gshape, gshape, SDS((nseg, B_HEADS, ATTN_BLOCK, 2 * ATTN_BLOCK), F32)],
        scratch_shapes=[pltpu.VMEM((nseg, SEQ, LANE), F32), pltpu.VMEM((SEQ, LANE), F32)],
        args=[proj, proj, proj, b_out, dmix, lse_tot, bias], sem=("arbitrary", "arbitrary"), ride=ride)


PAD = 8


def conv_gelu_fwd(gp, up, cw, cb, nb_local, name):
    n, f = gp.shape

    def body(gp_ref, up_ref, cw_ref, cb_ref, o_ref, pad_ref):
        pad_ref[0:PAD, :] = jnp.zeros((PAD, LANE), F32)
        pad_ref[PAD:PAD + SEQ, :] = gp_ref[...]
        c = (cb_ref[...] + cw_ref[0:1, :] * pad_ref[PAD - 2:PAD - 2 + SEQ, :]
             + cw_ref[1:2, :] * pad_ref[PAD - 1:PAD - 1 + SEQ, :] + cw_ref[2:3, :] * gp_ref[...])
        o_ref[...] = (_gelu(c) * up_ref[...]).astype(BF16)

    blk = pl.BlockSpec((SEQ, LANE), lambda b, j: (b, j))
    return pl.pallas_call(
        body, name=name, grid=(nb_local, f // LANE),
        in_specs=[blk, blk, pl.BlockSpec((3, LANE), lambda b, j: (0, j)), pl.BlockSpec((1, LANE), lambda b, j: (0, j))],
        out_specs=blk,
        out_shape=SDS((n, f), BF16),
        scratch_shapes=[pltpu.VMEM((SEQ + PAD, LANE), F32)],
        compiler_params=_params(("parallel", "parallel")),
    )(gp, up, cw, cb)


def conv_gelu_bwd(dgu, gp, up, cw, cb, nb_local, name, ride=None):
    n, f = gp.shape

    def body(dgu_ref, gp_ref, up_ref, cw_ref, cb_ref, dgp_ref, dup_ref, dcw_ref, dcb_ref, pad_ref, dpad_ref):
        b = pl.program_id(1)
        pad_ref[0:PAD, :] = jnp.zeros((PAD, LANE), F32)
        pad_ref[PAD:PAD + SEQ, :] = gp_ref[...]
        g0 = gp_ref[...]
        g1 = pad_ref[PAD - 1:PAD - 1 + SEQ, :]
        g2 = pad_ref[PAD - 2:PAD - 2 + SEQ, :]
        c = cb_ref[...] + cw_ref[0:1, :] * g2 + cw_ref[1:2, :] * g1 + cw_ref[2:3, :] * g0
        gg, dgg = _gelu_and_grad(c)
        dgu = dgu_ref[...].astype(F32)
        dup_ref[...] = (dgu * gg).astype(BF16)
        dc = dgu * up_ref[...] * dgg
        dpad_ref[SEQ:SEQ + PAD, :] = jnp.zeros((PAD, LANE), F32)
        dpad_ref[0:SEQ, :] = dc
        dgp_ref[...] = (cw_ref[2:3, :] * dc + cw_ref[1:2, :] * dpad_ref[1:1 + SEQ, :]
                        + cw_ref[0:1, :] * dpad_ref[2:2 + SEQ, :]).astype(BF16)
        dcw = jnp.concatenate([jnp.sum(dc * g2, axis=0, keepdims=True), jnp.sum(dc * g1, axis=0, keepdims=True),
                               jnp.sum(dc * g0, axis=0, keepdims=True)], axis=0)
        dcb = jnp.sum(dc, axis=0, keepdims=True)

        @pl.when(b == 0)
        def _():
            dcw_ref[...] = dcw
            dcb_ref[...] = dcb

        @pl.when(b > 0)
        def _():
            dcw_ref[...] += dcw
            dcb_ref[...] += dcb

    blk = pl.BlockSpec((SEQ, LANE), lambda j, b: (b, j))
    return _call(
        body, name=name, grid=(f // LANE, nb_local),
        in_specs=[blk, blk, blk, pl.BlockSpec((3, LANE), lambda j, b: (0, j)), pl.BlockSpec((1, LANE), lambda j, b: (0, j))],
        out_specs=[blk, blk, pl.BlockSpec((3, LANE), lambda j, b: (0, j)), pl.BlockSpec((1, LANE), lambda j, b: (0, j))],
        out_shape=[SDS((n, f), BF16), SDS((n, f), BF16), SDS((3, f), F32), SDS((1, f), F32)],
        scratch_shapes=[pltpu.VMEM((SEQ + PAD, LANE), F32), pltpu.VMEM((SEQ + PAD, LANE), F32)],
        args=[dgu, gp, up, cw, cb], sem=("parallel", "arbitrary"), ride=ride)


def norm_bwd_mid(dh2, x1, dout, z2, g3, g2, name, tm=256):
    n, d = x1.shape

    def body(dh_ref, x1_ref, dout_ref, z2_ref, g3_ref, g2_ref, dx1_ref, dz2_ref, dg3_ref, dg2_ref):
        i = pl.program_id(0)
        dxa, dg3r = _rms_bwd(dh_ref[...], x1_ref[...], g3_ref[...])
        dx1 = dout_ref[...] + dxa
        dx1_ref[...] = dx1
        dz2, dg2r = _rms_bwd(dx1, z2_ref[...], g2_ref[...])
        dz2_ref[...] = dz2.astype(BF16)
        s3 = jnp.sum(dg3r, axis=0, keepdims=True)
        s2 = jnp.sum(dg2r, axis=0, keepdims=True)

        @pl.when(i == 0)
        def _():
            dg3_ref[...] = s3
            dg2_ref[...] = s2

        @pl.when(i > 0)
        def _():
            dg3_ref[...] += s3
            dg2_ref[...] += s2

    row = pl.BlockSpec((tm, d), lambda i: (i, 0))
    vec = pl.BlockSpec((1, d), lambda i: (0, 0))
    return pl.pallas_call(
        body, name=name, grid=(n // tm,),
        in_specs=[row, row, row, row, vec, vec],
        out_specs=[row, row, vec, vec],
        out_shape=[SDS((n, d), F32), SDS((n, d), BF16), SDS((1, d), F32), SDS((1, d), F32)],
        compiler_params=_params(("arbitrary",)),
    )(dh2, x1, dout, z2, g3, g2)


def norm_bwd_in(dh1, x, dx1, g1, name, tm=256):
    n, d = x.shape

    def body(dh_ref, x_ref, dx1_ref, g1_ref, dx_ref, dg1_ref):
        i = pl.program_id(0)
        dxa, dgr = _rms_bwd(dh_ref[...], x_ref[...], g1_ref[...])
        dx_ref[...] = dx1_ref[...] + dxa
        s = jnp.sum(dgr, axis=0, keepdims=True)

        @pl.when(i == 0)
        def _():
            dg1_ref[...] = s

        @pl.when(i > 0)
        def _():
            dg1_ref[...] += s

    row = pl.BlockSpec((tm, d), lambda i: (i, 0))
    vec = pl.BlockSpec((1, d), lambda i: (0, 0))
    return pl.pallas_call(
        body, name=name, grid=(n // tm,),
        in_specs=[row, row, row, vec],
        out_specs=[row, vec],
        out_shape=[SDS((n, d), F32), SDS((1, d), F32)],
        compiler_params=_params(("arbitrary",)),
    )(dh1, x, dx1, g1)


def cast_bf16(arrays, name):
    def body(*refs):
        for i_ref, o_ref in zip(refs[:len(arrays)], refs[len(arrays):]):
            o_ref[...] = i_ref[...].astype(BF16)

    return pl.pallas_call(body, name=name, out_shape=[SDS(a.shape, BF16) for a in arrays],
                          compiler_params=_params())(*arrays)


def adam_update(parts, w, m, v, name, tr=None):
    s, r, c = parts.shape
    tr = r if tr is None else tr
    bc1 = 1.0 - ADAM_B1 ** ADAM_STEP
    bc2 = 1.0 - ADAM_B2 ** ADAM_STEP

    def body(p_ref, w_ref, m_ref, v_ref, g_ref, d_ref, nm_ref, nv_ref):
        g = p_ref[0].astype(F32)
        for j in range(1, s):
            g = g + p_ref[j].astype(F32)
        nm = ADAM_B1 * m_ref[...] + (1.0 - ADAM_B1) * g
        nv = ADAM_B2 * v_ref[...] + (1.0 - ADAM_B2) * (g * g)
        g_ref[...] = g
        nm_ref[...] = nm
        nv_ref[...] = nv
        d_ref[...] = -ADAM_LR * ((nm / bc1) / (jnp.sqrt(nv / bc2) + ADAM_EPS) + ADAM_WD * w_ref[...])

    blk = pl.BlockSpec((tr, c), lambda i: (i, 0))
    return pl.pallas_call(
        body, name=name, grid=(r // tr,),
        in_specs=[pl.BlockSpec((s, tr, c), lambda i: (0, i, 0)), blk, blk, blk],
        out_specs=[blk] * 4, out_shape=[SDS((r, c), F32)] * 4,
        compiler_params=_params(("parallel",)),
    )(parts, w, m, v)


SMALL_NAMES = ("spatial_w", "norm_mix_pre", "norm_mix_post", "norm_ffn_pre", "norm_ffn_post", "conv_b",
               "ln_v_gain", "ln_v_bias", "spatial_b", "rel_bias")
PACK_ROW_ALIGN = 8


def _pack_rows(size):
    rows = -(-size // LANE)
    return -(-rows // PACK_ROW_ALIGN) * PACK_ROW_ALIGN


def _pack(arrays):
    flat = []
    for a in arrays:
        rows = _pack_rows(a.size)
        flat.append(jnp.pad(a.reshape(-1), (0, rows * LANE - a.size)))
    return jnp.concatenate(flat).reshape(-1, LANE)


def _unpack(packed, shapes):
    out, row = [], 0
    for shp in shapes:
        size = int(np.prod(shp))
        out.append(packed[row:row + _pack_rows(size)].reshape(-1)[:size].reshape(shp))
        row += _pack_rows(size)
    return out


def kernel(x, norm_mix_pre, norm_mix_post, norm_ffn_pre, norm_ffn_post, w_in, ln_v_gain, ln_v_bias, spatial_w, spatial_b, rel_bias, w_out, w_gate, w_up, conv_w, conv_b, w_down, loss_target, m_norm_mix_pre, m_norm_mix_post, m_norm_ffn_pre, m_norm_ffn_post, m_w_in, m_ln_v_gain, m_ln_v_bias, m_spatial_w, m_spatial_b, m_rel_bias, m_w_out, m_w_gate, m_w_up, m_conv_w, m_conv_b, m_w_down, v_norm_mix_pre, v_norm_mix_post, v_norm_ffn_pre, v_norm_ffn_post, v_w_in, v_ln_v_gain, v_ln_v_bias, v_spatial_w, v_spatial_b, v_rel_bias, v_w_out, v_w_gate, v_w_up, v_conv_w, v_conv_b, v_w_down):
    given = dict(locals())
    nb_local, seq, d = x.shape
    n = nb_local * seq
    cols = w_in.shape[2]

    def by_columns(g):
        return g.transpose(1, 0, 2).reshape(g.shape[1], N_DEV * g.shape[2])

    def to_blocks(g):
        return g.reshape(g.shape[0], N_DEV, cols).transpose(1, 0, 2)

    xf, target = x.reshape(n, d), loss_target.reshape(n, d)
    ln_g, ln_b = ln_v_gain.reshape(1, A_WIDTH), ln_v_bias.reshape(1, A_WIDTH)
    spatial_bt, rel_bias_t = spatial_b[0].T, rel_bias.T

    s_in, s_out, s_gate, s_up, s_down = cast_bf16([w_in[0], w_out[0], w_gate[0], w_up[0], w_down[0]], "cast_shards")
    g_in, g_cw = exchange([], [s_in, conv_w[0]], "gather_w_in")
    w_in_f, conv_w_f = by_columns(g_in), by_columns(g_cw)

    (h1, proj), _ = norm_mm(xf, norm_mix_pre, [w_in_f], "fwd_norm_in")
    a = gating_fwd(proj, ln_g, ln_b, spatial_w[0], spatial_bt, "fwd_gating")
    bias = bias_tables(rel_bias_t, "bias_tables").reshape(len(DILATIONS), B_HEADS, ATTN_BLOCK, 2 * ATTN_BLOCK)
    (b_out, lse_tot), (g_out, g_gate, g_up) = attn_fwd(proj, bias, nb_local, "fwd_attn",
                                                       ride=([], [s_out, s_gate, s_up]))
    w_out_f, w_gate_f, w_up_f = g_out.reshape(D_MODEL, D_MODEL), by_columns(g_gate), by_columns(g_up)
    z2, x1 = mm_res_norm([a, b_out], w_out_f, xf, norm_mix_post, "fwd_out_norm")
    (h2, gp, up), (g_down,) = norm_mm(x1, norm_ffn_pre, [w_gate_f, w_up_f], "fwd_norm_ffn", ride=([], [s_down]))
    w_down_f = g_down.reshape(D_FF, D_MODEL)
    gu = conv_gelu_fwd(gp, up, conv_w_f, conv_b, nb_local, "fwd_conv_gelu")
    dy, dout, dg4, loss_part = down_loss(gu, w_down_f, x1, norm_ffn_post, target, "fwd_down_loss")

    p_down = mm_tn([gu], [dy], "bwd_dw_down", t1=1408)
    (dgu,), _ = mm_nt([(dy, 0, 0)], [w_down_f], "bwd_dgu", out_dtype=BF16)
    (dgp, dup, p_conv_w, p_conv_b), (r_down,) = conv_gelu_bwd(
        dgu, gp, up, conv_w_f, conv_b, nb_local, "bwd_conv_gelu", ride=([p_down.reshape(N_DEV, cols, D_MODEL)], []))
    p_gate = mm_tn([h2], [dgp], "bwd_dw_gate")
    p_up = mm_tn([h2], [dup], "bwd_dw_up")
    (dh2,), _ = mm_nt([(dgp, 0, 0), (dup, 1, 0)], [w_gate_f, w_up_f], "bwd_dh2")
    dx1, dz2, dg3, dg2 = norm_bwd_mid(dh2, x1, dout, z2, norm_ffn_pre, norm_mix_post, "bwd_norm_mid")
    p_out = mm_tn([a, b_out], [dz2], "bwd_dw_out")
    (dmix,), _ = mm_nt([(dz2, 0, 0)], [w_out_f], "bwd_dmix")
    duv, p_ws, p_sbt, p_lng, p_lnb = gating_bwd(proj, dmix, ln_g, ln_b, spatial_w[0], spatial_bt, "bwd_gating")
    (dq, dk, dv, dbias), (r_gate, r_up, r_out) = attn_bwd(
        proj, b_out, dmix, lse_tot, bias, nb_local, "bwd_attn",
        ride=([to_blocks(p_gate), to_blocks(p_up), p_out.reshape(N_DEV, D_MODEL // N_DEV, D_MODEL)], []))
    p_rel_bias_t = rel_bias_grad(dbias.reshape(len(DILATIONS), B_HEADS, BIAS_SIZE), "bwd_rel_bias")
    p_in = mm_tn([h1], [duv, dq, dk, dv], "bwd_dw_in")
    (dh1,), (r_in,) = mm_nt([(duv, 0, 0), (dq, 0, Q_OFF), (dk, 0, K_OFF), (dv, 0, V_OFF)], [w_in_f], "bwd_dh1",
                            ride=([to_blocks(p_in)], []))
    grad_x, dg1 = norm_bwd_in(dh1, xf, dx1, norm_mix_pre, "bwd_norm_in")

    part = dict(grad_x=grad_x, conv_w=p_conv_w, loss=loss_part)
    small = dict(spatial_w=p_ws, norm_mix_pre=dg1, norm_mix_post=dg2, norm_ffn_pre=dg3, norm_ffn_post=dg4,
                 conv_b=p_conv_b, ln_v_gain=p_lng, ln_v_bias=p_lnb, spatial_b=p_sbt.T, rel_bias=p_rel_bias_t.T)
    pack = _pack([small[k] for k in SMALL_NAMES] + [part["conv_w"], part["loss"]])
    (r_small,) = exchange([], [pack], "exchange_small")

    res = {}
    res["w_in"] = adam_update(r_in, w_in[0], m_w_in[0], v_w_in[0], "adam_w_in", tr=256)
    res["w_out"] = adam_update(r_out, w_out[0], m_w_out[0], v_w_out[0], "adam_w_out")
    res["w_gate"] = adam_update(r_gate, w_gate[0], m_w_gate[0], v_w_gate[0], "adam_w_gate", tr=256)
    res["w_up"] = adam_update(r_up, w_up[0], m_w_up[0], v_w_up[0], "adam_w_up", tr=256)
    res["w_down"] = adam_update(r_down, w_down[0], m_w_down[0], v_w_down[0], "adam_w_down", tr=176)

    tail = [jnp.zeros_like(part["conv_w"]), jnp.zeros_like(part["loss"])]
    packs = [_pack([given[pre + k] for k in SMALL_NAMES] + tail) for pre in ("", "m_", "v_")]
    small_res = adam_update(r_small, *packs, "adam_small")
    small_shapes = [given[k].shape for k in SMALL_NAMES] + [part["conv_w"].shape, part["loss"].shape]
    unpacked = [_unpack(p, small_shapes) for p in small_res]
    for i, k in enumerate(SMALL_NAMES):
        res[k] = [u[i] for u in unpacked]
    g_conv_w = lax.dynamic_slice_in_dim(unpacked[0][len(SMALL_NAMES)], _my_index() * cols, cols, axis=1)
    res["conv_w"] = adam_update(g_conv_w[None], conv_w[0], m_conv_w[0], v_conv_w[0], "adam_conv_w")
    loss = unpacked[0][len(SMALL_NAMES) + 1][0, 0]

    names = ("norm_mix_pre", "norm_mix_post", "norm_ffn_pre", "norm_ffn_post", "w_in", "ln_v_gain", "ln_v_bias",
             "spatial_w", "spatial_b", "rel_bias", "w_out", "w_gate", "w_up", "conv_w", "conv_b", "w_down")
    outs = [loss, part["grad_x"].reshape(x.shape)]
    for t in range(4):
        outs += [res[k][t].reshape(given[k].shape) for k in names]
    return tuple(outs)
```

```python
import functools
import math

import numpy as np
import jax
import jax.numpy as jnp
from jax import lax
from jax.experimental import pallas as pl
from jax.experimental.pallas import tpu as pltpu

F32 = jnp.float32
BF16 = jnp.bfloat16
SDS = jax.ShapeDtypeStruct

D_MODEL = 1024
SEQ = 2048
HEAD_DIM = 64
A_GROUPS = 4
A_WIDTH = A_GROUPS * HEAD_DIM
B_HEADS = 12
B_WIDTH = B_HEADS * HEAD_DIM
HEAD_PAIRS = B_HEADS // 2
CHUNK = 128
ATTN_BLOCK = 128
DILATIONS = (1, 4, 16)
NUM_BUCKETS = 32
MAX_DISTANCE = 2048
D_FF = 2816
IN_COLS = 2 * A_WIDTH + 3 * B_WIDTH
Q_OFF = 2 * A_WIDTH
K_OFF = Q_OFF + B_WIDTH
V_OFF = K_OFF + B_WIDTH
NORM_EPS = 1e-6
NEG_INF = -1e30
N_DEV = 8
LANE = 128

ADAM_LR = 0.001
ADAM_B1 = 0.9
ADAM_B2 = 0.999
ADAM_EPS = 1e-08
ADAM_WD = 0.01
ADAM_STEP = 10

GELU_C0 = math.sqrt(2.0 / math.pi)
GELU_C1 = 0.044715

VMEM_LIMIT = 56 * 1024 * 1024


def _params(sem=None):
    if sem is None:
        return pltpu.CompilerParams(vmem_limit_bytes=VMEM_LIMIT)
    return pltpu.CompilerParams(dimension_semantics=sem, vmem_limit_bytes=VMEM_LIMIT)


def _gelu(x):
    t = jnp.tanh(GELU_C0 * (x + GELU_C1 * x * x * x))
    return 0.5 * x * (1.0 + t)


def _gelu_and_grad(x):
    x2 = x * x
    t = jnp.tanh(GELU_C0 * (x + GELU_C1 * x * x2))
    g = 0.5 * x * (1.0 + t)
    dg = 0.5 * (1.0 + t) + 0.5 * x * (1.0 - t * t) * (GELU_C0 * (1.0 + 3.0 * GELU_C1 * x2))
    return g, dg


def _dot(a, b):
    return jnp.dot(a, b, preferred_element_type=F32)


def _dot_nt(a, b):
    return lax.dot_general(a, b, (((1,), (1,)), ((), ())), preferred_element_type=F32)


def _dot_tn(a, b):
    return lax.dot_general(a, b, (((0,), (0,)), ((), ())), preferred_element_type=F32)


def _rms_bwd(d, xin, g):
    r = lax.rsqrt(jnp.mean(xin * xin, axis=-1, keepdims=True) + NORM_EPS)
    xh = xin * r
    gd = g * d
    dx = r * (gd - xh * jnp.mean(gd * xh, axis=-1, keepdims=True))
    return dx, d * xh


MESH = pl.DeviceIdType.MESH
ANY = pl.BlockSpec(memory_space=pl.ANY)
PEER_MASKS = tuple(range(1, N_DEV))


def _my_index():
    return lax.axis_index("x") * 4 + lax.axis_index("y") * 2 + lax.axis_index("c")


def _peer(mask):
    x, y, c = lax.axis_index("x"), lax.axis_index("y"), lax.axis_index("c")
    px = 1 - x if mask & 4 else x
    py = 1 - y if mask & 2 else y
    pc = 1 - c if mask & 1 else c
    return (px, py, pc), px * 4 + py * 2 + pc


def _exchange_copies(nblocked, in_refs, out_refs, sems):
    send_sems, recv_sems, local_sems = sems
    me = _my_index()
    local, sends, recvs = [], [], []
    for a, (in_ref, out_ref) in enumerate(zip(in_refs, out_refs)):
        src_of = (lambda idx, r=in_ref: r.at[idx]) if a < nblocked else (lambda idx, r=in_ref: r)
        local.append(pltpu.make_async_copy(src_of(me), out_ref.at[me], local_sems.at[a]))
        for mask in PEER_MASKS:
            peer, pidx = _peer(mask)
            pair = dict(send_sem=send_sems.at[a, mask - 1], recv_sem=recv_sems.at[a, mask - 1],
                        device_id=peer, device_id_type=MESH)
            sends.append(pltpu.make_async_remote_copy(src_ref=src_of(pidx), dst_ref=out_ref.at[me], **pair))
            recvs.append(pltpu.make_async_remote_copy(src_ref=src_of(pidx), dst_ref=out_ref.at[pidx], **pair))
    return local, sends, recvs


def _exchange_start(nblocked, in_refs, out_refs, sems):
    local, sends, _ = _exchange_copies(nblocked, in_refs, out_refs, sems)
    for cp in local + sends:
        cp.start()


def _exchange_wait(nblocked, in_refs, out_refs, sems):
    local, sends, recvs = _exchange_copies(nblocked, in_refs, out_refs, sems)
    for cp in sends:
        cp.wait_send()
    for cp in recvs:
        cp.wait_recv()
    for cp in local:
        cp.wait()


def _exchange_out_shape(blocked, whole):
    return [SDS(b.shape, b.dtype) for b in blocked] + [SDS((N_DEV,) + w.shape, w.dtype) for w in whole]


def _exchange_sems(n):
    return [pltpu.SemaphoreType.DMA((n, N_DEV - 1)), pltpu.SemaphoreType.DMA((n, N_DEV - 1)),
            pltpu.SemaphoreType.DMA((n,))]


def exchange(blocked, whole, name):
    nb, n = len(blocked), len(blocked) + len(whole)

    def body(*refs):
        _exchange_start(nb, refs[:n], refs[n:2 * n], refs[2 * n:])
        _exchange_wait(nb, refs[:n], refs[n:2 * n], refs[2 * n:])

    return pl.pallas_call(
        body, name=name, in_specs=[ANY] * n, out_specs=[ANY] * n, out_shape=_exchange_out_shape(blocked, whole),
        scratch_shapes=_exchange_sems(n),
    )(*blocked, *whole)


def _call(body, *, name, grid, in_specs, out_specs, out_shape, args, scratch_shapes=(), sem=None, ride=None):
    out_shape, out_specs, scratch_shapes = list(out_shape), list(out_specs), list(scratch_shapes)
    if ride is None:
        outs = pl.pallas_call(body, name=name, grid=grid, in_specs=list(in_specs), out_specs=out_specs,
                              out_shape=out_shape, scratch_shapes=scratch_shapes,
                              compiler_params=_params(sem))(*args)
        return list(outs), []
    blocked, whole = ride
    cargs = list(blocked) + list(whole)
    nb, nc = len(blocked), len(cargs)
    n_in, n_out, n_scr = len(args), len(out_shape), len(scratch_shapes)

    def riding(*refs):
        ins, refs = refs[:n_in], refs[n_in:]
        cins, refs = refs[:nc], refs[nc:]
        outs, refs = refs[:n_out], refs[n_out:]
        couts, refs = refs[:nc], refs[nc:]
        scr, sems = refs[:n_scr], refs[n_scr:]
        ids = [pl.program_id(k) for k in range(len(grid))]
        first = functools.reduce(lambda p, q: p & q, [i == 0 for i in ids])
        last = functools.reduce(lambda p, q: p & q, [i == g - 1 for i, g in zip(ids, grid)])

        @pl.when(first)
        def _():
            _exchange_start(nb, cins, couts, sems)

        body(*ins, *outs, *scr)

        @pl.when(last)
        def _():
            _exchange_wait(nb, cins, couts, sems)

    res = pl.pallas_call(
        riding, name=name, grid=grid, in_specs=list(in_specs) + [ANY] * nc, out_specs=out_specs + [ANY] * nc,
        out_shape=out_shape + _exchange_out_shape(blocked, whole),
        scratch_shapes=scratch_shapes + _exchange_sems(nc),
        compiler_params=_params(("arbitrary",) * len(grid)))(*args, *cargs)
    return list(res[:n_out]), list(res[n_out:])


def norm_mm(x, g, ws, name, tm=512, tn=1408, ride=None):
    n, d = x.shape
    f = ws[0].shape[1]
    nw = len(ws)

    def body(x_ref, g_ref, *refs):
        w_refs = refs[:nw]
        h_ref = refs[nw]
        o_refs = refs[nw + 1:]

        @pl.when(pl.program_id(1) == 0)
        def _():
            xv = x_ref[...]
            r = lax.rsqrt(jnp.mean(xv * xv, axis=-1, keepdims=True) + NORM_EPS)
            h_ref[...] = (xv * r * g_ref[...]).astype(BF16)

        h = h_ref[...]
        for w_ref, o_ref in zip(w_refs, o_refs):
            o_ref[...] = _dot(h, w_ref[...])

    return _call(
        body, name=name, grid=(n // tm, f // tn),
        in_specs=[pl.BlockSpec((tm, d), lambda i, j: (i, 0)), pl.BlockSpec((1, d), lambda i, j: (0, 0))]
        + [pl.BlockSpec((d, tn), lambda i, j: (0, j)) for _ in ws],
        out_specs=[pl.BlockSpec((tm, d), lambda i, j: (i, 0))]
        + [pl.BlockSpec((tm, tn), lambda i, j: (i, j)) for _ in ws],
        out_shape=[SDS((n, d), BF16)] + [SDS((n, f), F32) for _ in ws],
        args=[x, g, *ws], sem=("parallel", "arbitrary"), ride=ride)


def _lane_concat(refs):
    vals = [r[...].astype(BF16) for r in refs]
    return vals[0] if len(vals) == 1 else jnp.concatenate(vals, axis=1)


def mm_res_norm(a_list, w, res, g, name, tm=512):
    n = a_list[0].shape[0]
    k, d = w.shape
    na = len(a_list)

    def body(*refs):
        w_ref, res_ref, g_ref, y_ref, o_ref = refs[na:]
        y = _dot(_lane_concat(refs[:na]), w_ref[...])
        r = lax.rsqrt(jnp.mean(y * y, axis=-1, keepdims=True) + NORM_EPS)
        y_ref[...] = y
        o_ref[...] = res_ref[...] + y * r * g_ref[...]

    return pl.pallas_call(
        body, name=name, grid=(n // tm,),
        in_specs=[pl.BlockSpec((tm, a.shape[1]), lambda i: (i, 0)) for a in a_list]
        + [pl.BlockSpec((k, d), lambda i: (0, 0)),
           pl.BlockSpec((tm, d), lambda i: (i, 0)), pl.BlockSpec((1, d), lambda i: (0, 0))],
        out_specs=[pl.BlockSpec((tm, d), lambda i: (i, 0)), pl.BlockSpec((tm, d), lambda i: (i, 0))],
        out_shape=[SDS((n, d), F32), SDS((n, d), F32)],
        compiler_params=_params(("parallel",)),
    )(*a_list, w, res, g)


def down_loss(a, w, res, g, target, name, tm=256):
    n, k = a.shape
    d = w.shape[1]
    inv_d = 1.0 / d

    def body(a_ref, w_ref, res_ref, g_ref, t_ref, dy_ref, dout_ref, dg_ref, loss_ref):
        i = pl.program_id(0)
        y = _dot(a_ref[...], w_ref[...])
        gv = g_ref[...]
        r = lax.rsqrt(jnp.mean(y * y, axis=-1, keepdims=True) + NORM_EPS)
        yh = y * r
        e = res_ref[...] + yh * gv - t_ref[...]
        part = 0.5 * inv_d * jnp.sum(jnp.sum(e * e, axis=-1, keepdims=True), axis=0, keepdims=True)
        dout = e * inv_d
        dout_ref[...] = dout
        gd = gv * dout
        dy_ref[...] = (r * (gd - yh * jnp.mean(gd * yh, axis=-1, keepdims=True))).astype(BF16)
        dgp = jnp.sum(dout * yh, axis=0, keepdims=True)
        lane0 = lax.broadcasted_iota(jnp.int32, (1, LANE), 1) == 0
        lp = jnp.where(lane0, part, 0.0)

        @pl.when(i == 0)
        def _():
            dg_ref[...] = dgp
            loss_ref[...] = lp

        @pl.when(i > 0)
        def _():
            dg_ref[...] += dgp
            loss_ref[...] += lp

    return pl.pallas_call(
        body, name=name, grid=(n // tm,),
        in_specs=[pl.BlockSpec((tm, k), lambda i: (i, 0)), pl.BlockSpec((k, d), lambda i: (0, 0)),
                  pl.BlockSpec((tm, d), lambda i: (i, 0)), pl.BlockSpec((1, d), lambda i: (0, 0)),
                  pl.BlockSpec((tm, d), lambda i: (i, 0))],
        out_specs=[pl.BlockSpec((tm, d), lambda i: (i, 0)), pl.BlockSpec((tm, d), lambda i: (i, 0)),
                   pl.BlockSpec((1, d), lambda i: (0, 0)), pl.BlockSpec((1, LANE), lambda i: (0, 0))],
        out_shape=[SDS((n, d), BF16), SDS((n, d), F32), SDS((1, d), F32), SDS((1, LANE), F32)],
        compiler_params=_params(("arbitrary",)),
    )(a, w, res, g, target)


def _accumulate(ref, val, step):
    @pl.when(step == 0)
    def _():
        ref[...] = val

    @pl.when(step > 0)
    def _():
        ref[...] += val


def mm_nt(terms, ws, name, tm=512, out_dtype=F32, ride=None, epilogue=None):
    n = terms[0][0].shape[0]
    r = ws[0].shape[0]
    na = len(terms)
    meta = [(widx, off, a.shape[1]) for a, widx, off in terms]
    fn, extras, out_shape = epilogue if epilogue else (None, [], [SDS((n, r), out_dtype)])
    n_fixed = na + len(ws)

    def body(*refs):
        a_refs = refs[:na]
        w_refs = refs[na:n_fixed]
        acc = None
        for a_ref, (widx, off, k) in zip(a_refs, meta):
            p = _dot_nt(a_ref[...].astype(BF16), w_refs[widx][:, off:off + k])
            acc = p if acc is None else acc + p
        if fn is None:
            refs[-1][...] = acc.astype(out_dtype)
        else:
            fn(acc, pl.program_id(0), *refs[n_fixed:])

    def spec(a):
        if a.shape[0] == 1:
            return pl.BlockSpec(a.shape, lambda i: (0, 0))
        return pl.BlockSpec((tm, a.shape[1]), lambda i: (i, 0))

    return _call(
        body, name=name, grid=(n // tm,),
        in_specs=[spec(a) for a, _, _ in terms] + [pl.BlockSpec(w.shape, lambda i: (0, 0)) for w in ws]
        + [spec(e) for e in extras],
        out_specs=[spec(o) for o in out_shape], out_shape=out_shape,
        args=[a for a, _, _ in terms] + list(ws) + list(extras),
        sem=("parallel",) if fn is None else ("arbitrary",), ride=ride)


def mm_tn(lhs_list, rhs_list, name, t1=512, tn=512, out_dtype=BF16):
    n = lhs_list[0].shape[0]
    k1 = sum(l.shape[1] for l in lhs_list)
    nl = len(lhs_list)
    if nl > 1:
        t1 = k1
    widths = [r.shape[1] for r in rhs_list]
    k2 = sum(widths)
    nr = len(rhs_list)
    nk = n // tn

    def body(*refs):
        r_refs = refs[nl:nl + nr]
        o_ref = refs[nl + nr]
        acc = refs[nl + nr + 1]
        k = pl.program_id(1)
        a = _lane_concat(refs[:nl])
        parts = [_dot_tn(a, r_ref[...].astype(BF16)) for r_ref in r_refs]
        val = parts[0] if nr == 1 else jnp.concatenate(parts, axis=1)

        @pl.when(k == 0)
        def _():
            acc[...] = val

        @pl.when(k > 0)
        def _():
            acc[...] += val

        @pl.when(k == nk - 1)
        def _():
            o_ref[...] = acc[...].astype(out_dtype)

    if nl == 1:
        lhs_specs = [pl.BlockSpec((tn, t1), lambda i, k: (k, i))]
    else:
        lhs_specs = [pl.BlockSpec((tn, l.shape[1]), lambda i, k: (k, 0)) for l in lhs_list]
    return pl.pallas_call(
        body, name=name, grid=(k1 // t1, nk),
        in_specs=lhs_specs + [pl.BlockSpec((tn, w), lambda i, k: (k, 0)) for w in widths],
        out_specs=pl.BlockSpec((t1, k2), lambda i, k: (i, 0)),
        out_shape=SDS((k1, k2), out_dtype),
        scratch_shapes=[pltpu.VMEM((t1, k2), F32)],
        compiler_params=_params(("parallel", "arbitrary")),
    )(*lhs_list, *rhs_list)


GATE_ROWS = 512


def _tril_mask():
    row = lax.broadcasted_iota(jnp.int32, (CHUNK, CHUNK), 0)
    col = lax.broadcasted_iota(jnp.int32, (CHUNK, CHUNK), 1)
    return row >= col


def _layer_norm_parts(gv):
    mu = jnp.mean(gv, axis=-1, keepdims=True)
    xc = gv - mu
    rstd = lax.rsqrt(jnp.mean(xc * xc, axis=-1, keepdims=True) + NORM_EPS)
    return xc * rstd, rstd


def gating_fwd(proj, lng, lnb, ws, sbt, name):
    n = proj.shape[0]
    nchunk = GATE_ROWS // CHUNK

    def body(u_ref, v_ref, lng_ref, lnb_ref, ws_ref, sbt_ref, a_ref):
        tril = _tril_mask()
        for g in range(A_GROUPS):
            cs = slice(g * HEAD_DIM, (g + 1) * HEAD_DIM)
            wt = jnp.where(tril, ws_ref[g], 0.0).astype(BF16)
            for c in range(nchunk):
                rs_ = slice(c * CHUNK, (c + 1) * CHUNK)
                vhat, _ = _layer_norm_parts(_gelu(v_ref[rs_, cs]))
                vn = vhat * lng_ref[:, cs] + lnb_ref[:, cs]
                z = _dot(wt, vn.astype(BF16)) + sbt_ref[:, g:g + 1]
                a_ref[rs_, cs] = _gelu(u_ref[rs_, cs]) * z

    return pl.pallas_call(
        body, name=name, grid=(n // GATE_ROWS,),
        in_specs=[pl.BlockSpec((GATE_ROWS, A_WIDTH), lambda i: (i, 0)),
                  pl.BlockSpec((GATE_ROWS, A_WIDTH), lambda i: (i, 1)),
                  pl.BlockSpec((1, A_WIDTH), lambda i: (0, 0)), pl.BlockSpec((1, A_WIDTH), lambda i: (0, 0)),
                  pl.BlockSpec((A_GROUPS, CHUNK, CHUNK), lambda i: (0, 0, 0)),
                  pl.BlockSpec((CHUNK, A_GROUPS), lambda i: (0, 0))],
        out_specs=pl.BlockSpec((GATE_ROWS, A_WIDTH), lambda i: (i, 0)),
        out_shape=SDS((n, A_WIDTH), F32),
        compiler_params=_params(("parallel",)),
    )(proj, proj, lng, lnb, ws, sbt)


def gating_bwd(proj, dmix, lng, lnb, ws, sbt, name):
    n = proj.shape[0]
    nchunk = GATE_ROWS // CHUNK

    def body(u_ref, v_ref, da_ref, lng_ref, lnb_ref, ws_ref, sbt_ref,
             duv_ref, dws_ref, dsbt_ref, dlng_ref, dlnb_ref):
        @pl.when(pl.program_id(0) == 0)
        def _():
            dws_ref[...] = jnp.zeros_like(dws_ref)
            dsbt_ref[...] = jnp.zeros_like(dsbt_ref)
            dlng_ref[...] = jnp.zeros_like(dlng_ref)
            dlnb_ref[...] = jnp.zeros_like(dlnb_ref)

        tril = _tril_mask()
        for g in range(A_GROUPS):
            cs = slice(g * HEAD_DIM, (g + 1) * HEAD_DIM)
            wt = jnp.where(tril, ws_ref[g], 0.0).astype(BF16)
            lg = lng_ref[:, cs]
            dw = jnp.zeros((CHUNK, CHUNK), F32)
            dsb = jnp.zeros((CHUNK, 1), F32)
            dlg = jnp.zeros((1, HEAD_DIM), F32)
            dlb = jnp.zeros((1, HEAD_DIM), F32)
            for c in range(nchunk):
                rs_ = slice(c * CHUNK, (c + 1) * CHUNK)
                gu, dgu_dx = _gelu_and_grad(u_ref[rs_, cs])
                gv, dgv_dx = _gelu_and_grad(v_ref[rs_, cs])
                vhat, rstd = _layer_norm_parts(gv)
                vn = (vhat * lg + lnb_ref[:, cs]).astype(BF16)
                z = _dot(wt, vn) + sbt_ref[:, g:g + 1]
                da = da_ref[rs_, cs]
                dz = da * gu
                dzb = dz.astype(BF16)
                duv_ref[rs_, cs] = da * z * dgu_dx
                dsb = dsb + jnp.sum(dz, axis=-1, keepdims=True)
                dw = dw + _dot_nt(dzb, vn)
                dvn = _dot_tn(wt, dzb)
                dlg = dlg + jnp.sum(dvn * vhat, axis=0, keepdims=True)
                dlb = dlb + jnp.sum(dvn, axis=0, keepdims=True)
                dvh = dvn * lg
                dgv = rstd * (dvh - jnp.mean(dvh, axis=-1, keepdims=True)
                              - vhat * jnp.mean(dvh * vhat, axis=-1, keepdims=True))
                duv_ref[rs_, A_WIDTH + g * HEAD_DIM:A_WIDTH + (g + 1) * HEAD_DIM] = dgv * dgv_dx
            dws_ref[g] += jnp.where(tril, dw, 0.0)
            dsbt_ref[:, g:g + 1] += dsb
            dlng_ref[:, cs] += dlg
            dlnb_ref[:, cs] += dlb

    return pl.pallas_call(
        body, name=name, grid=(n // GATE_ROWS,),
        in_specs=[pl.BlockSpec((GATE_ROWS, A_WIDTH), lambda i: (i, 0)),
                  pl.BlockSpec((GATE_ROWS, A_WIDTH), lambda i: (i, 1)),
                  pl.BlockSpec((GATE_ROWS, A_WIDTH), lambda i: (i, 0)),
                  pl.BlockSpec((1, A_WIDTH), lambda i: (0, 0)), pl.BlockSpec((1, A_WIDTH), lambda i: (0, 0)),
                  pl.BlockSpec((A_GROUPS, CHUNK, CHUNK), lambda i: (0, 0, 0)),
                  pl.BlockSpec((CHUNK, A_GROUPS), lambda i: (0, 0))],
        out_specs=[pl.BlockSpec((GATE_ROWS, 2 * A_WIDTH), lambda i: (i, 0)),
                   pl.BlockSpec((A_GROUPS, CHUNK, CHUNK), lambda i: (0, 0, 0)),
                   pl.BlockSpec((CHUNK, A_GROUPS), lambda i: (0, 0)),
                   pl.BlockSpec((1, A_WIDTH), lambda i: (0, 0)), pl.BlockSpec((1, A_WIDTH), lambda i: (0, 0))],
        out_shape=[SDS((n, 2 * A_WIDTH), F32), SDS((A_GROUPS, CHUNK, CHUNK), F32), SDS((CHUNK, A_GROUPS), F32),
                   SDS((1, A_WIDTH), F32), SDS((1, A_WIDTH), F32)],
        compiler_params=_params(("arbitrary",)),
    )(proj, proj, dmix, lng, lnb, ws, sbt)


def _t5_bucket_np(dist):
    max_exact = NUM_BUCKETS // 2
    dd = np.maximum(dist, 1).astype(np.float64)
    large = max_exact + np.log(dd / max_exact) / math.log(MAX_DISTANCE / max_exact) * (NUM_BUCKETS - max_exact)
    large = np.minimum(large.astype(np.int64), NUM_BUCKETS - 1)
    return np.where(dist < max_exact, dist, large)


def _bucket_tables():
    i = np.arange(ATTN_BLOCK)[:, None]
    j = np.arange(2 * ATTN_BLOCK)[None, :]
    rel = ATTN_BLOCK + i - j
    band = (rel >= 0) & (rel <= ATTN_BLOCK)
    tabs = []
    for dil in DILATIONS:
        b = _t5_bucket_np(np.maximum(rel, 0) * dil)
        tabs.append(np.where(band, b, -1).reshape(1, -1))
    return np.stack(tabs).astype(np.float32)


BIAS_SIZE = ATTN_BLOCK * 2 * ATTN_BLOCK


def bias_tables(rel_bias_t, name):
    idx = jnp.asarray(_bucket_tables())

    def body(rb_ref, idx_ref, o_ref):
        iv = idx_ref[0]
        bk = lax.broadcasted_iota(jnp.int32, (NUM_BUCKETS, BIAS_SIZE), 0).astype(F32)
        onehot = (bk == iv).astype(F32)
        t = jnp.dot(rb_ref[...], onehot, preferred_element_type=F32, precision=lax.Precision.HIGHEST)
        o_ref[0] = jnp.where(iv < 0.0, NEG_INF, t)

    return pl.pallas_call(
        body, name=name, grid=(len(DILATIONS),),
        in_specs=[pl.BlockSpec((B_HEADS, NUM_BUCKETS), lambda d: (0, 0)),
                  pl.BlockSpec((1, 1, BIAS_SIZE), lambda d: (d, 0, 0))],
        out_specs=pl.BlockSpec((1, B_HEADS, BIAS_SIZE), lambda d: (d, 0, 0)),
        out_shape=SDS((len(DILATIONS), B_HEADS, BIAS_SIZE), F32),
        compiler_params=_params(("parallel",)),
    )(rel_bias_t, idx)


def rel_bias_grad(dbias, name):
    idx = jnp.asarray(_bucket_tables())

    def body(db_ref, idx_ref, o_ref):
        d = pl.program_id(0)
        iv = idx_ref[0]
        bk = lax.broadcasted_iota(jnp.int32, (NUM_BUCKETS, BIAS_SIZE), 0).astype(F32)
        onehot = (bk == iv).astype(F32)
        part = lax.dot_general(db_ref[0], onehot, (((1,), (1,)), ((), ())),
                               preferred_element_type=F32, precision=lax.Precision.HIGHEST)

        @pl.when(d == 0)
        def _():
            o_ref[...] = part

        @pl.when(d > 0)
        def _():
            o_ref[...] += part

    return pl.pallas_call(
        body, name=name, grid=(len(DILATIONS),),
        in_specs=[pl.BlockSpec((1, B_HEADS, BIAS_SIZE), lambda d: (d, 0, 0)),
                  pl.BlockSpec((1, 1, BIAS_SIZE), lambda d: (d, 0, 0))],
        out_specs=pl.BlockSpec((B_HEADS, NUM_BUCKETS), lambda d: (0, 0)),
        out_shape=SDS((B_HEADS, NUM_BUCKETS), F32),
        compiler_params=_params(("arbitrary",)),
    )(dbias, idx)


def _attn_scores(q, kk, bias):
    return _dot_nt(q, kk) * (1.0 / math.sqrt(HEAD_DIM)) + bias


def _head0_lanes():
    return lax.broadcasted_iota(jnp.int32, (ATTN_BLOCK, LANE), 1) < HEAD_DIM


def _one_head(x2, head0, hh):
    return jnp.where(head0 if hh == 0 else jnp.logical_not(head0), x2, 0.0).astype(BF16)


def _rows(start, size, dil):
    return pl.ds(start, size) if dil == 1 else pl.ds(start, size, stride=dil)


def _attn_schedule(op):
    span1, span4 = ATTN_BLOCK, 4 * ATTN_BLOCK

    def d16(i, carry):
        for t in range(4):
            op(2, 16, 4 * i + t, True)
        return carry

    lax.fori_loop(0, 4, d16, 0)
    for r in range(4):
        op(1, 4, r, True)

    def d4(nq, carry):
        for r in range(4):
            op(1, 4, r + nq * span4, False)
        return carry

    lax.fori_loop(1, SEQ // span4, d4, 0)
    op(0, 1, 0, True)

    def d1(j, carry):
        for t in range(3):
            op(0, 1, pl.multiple_of((1 + 3 * j + t) * span1, span1), False)
        return carry

    lax.fori_loop(0, (SEQ // span1 - 1) // 3, d1, 0)


def _kv_rows(start, dil, first):
    if first:
        return _rows(start, ATTN_BLOCK, dil)
    return _rows(start - ATTN_BLOCK * dil, 2 * ATTN_BLOCK, dil)


MERGE_ROWS = 256


def attn_fwd(proj, bias, nb_local, name, ride=None):
    n = proj.shape[0]
    nseg = len(DILATIONS)

    def body(q_ref, k_ref, v_ref, b_ref, o_ref, lse_ref, os_ref, ls_ref):
        def op(seg, dil, start, first):
            qrows = _rows(start, ATTN_BLOCK, dil)
            krows = _kv_rows(start, dil, first)
            q2, kb, vb = q_ref[qrows, :], k_ref[krows, :].astype(BF16), v_ref[krows, :].astype(BF16)
            head0 = _head0_lanes()
            outs, lses = [], []
            for hh in range(2):
                bb = b_ref[seg, hh, :, ATTN_BLOCK:] if first else b_ref[seg, hh]
                s = _attn_scores(_one_head(q2, head0, hh), kb, bb)
                m = jnp.max(s, axis=-1, keepdims=True)
                p = jnp.exp(s - m)
                l = jnp.sum(p, axis=-1, keepdims=True)
                outs.append(_dot(p.astype(BF16), vb) / l)
                lses.append(jnp.broadcast_to(m + jnp.log(l), (ATTN_BLOCK, LANE)))
            os_ref[seg, qrows, :] = jnp.where(head0, outs[0], outs[1])
            ls_ref[seg, qrows, :] = jnp.where(head0, lses[0], lses[1])

        _attn_schedule(op)

        def merge(i, carry):
            rows = pl.ds(pl.multiple_of(i * MERGE_ROWS, MERGE_ROWS), MERGE_ROWS)
            ls = [ls_ref[s, rows, :] for s in range(nseg)]
            m = functools.reduce(jnp.maximum, ls)
            ws = [jnp.exp(l - m) for l in ls]
            den = ws[0] + ws[1] + ws[2]
            num = ws[0] * os_ref[0, rows, :] + ws[1] * os_ref[1, rows, :] + ws[2] * os_ref[2, rows, :]
            o_ref[rows, :] = num / den
            lse_ref[rows, :] = m + jnp.log(den)
            return carry

        lax.fori_loop(0, SEQ // MERGE_ROWS, merge, 0)

    def in_spec(off):
        return pl.BlockSpec((SEQ, LANE), lambda b, p: (b, off // LANE + p))

    out_spec = pl.BlockSpec((SEQ, LANE), lambda b, p: (b, p))
    return _call(
        body, name=name, grid=(nb_local, HEAD_PAIRS),
        in_specs=[in_spec(Q_OFF), in_spec(K_OFF), in_spec(V_OFF),
                  pl.BlockSpec((nseg, 2, ATTN_BLOCK, 2 * ATTN_BLOCK), lambda b, p: (0, p, 0, 0))],
        out_specs=[out_spec, out_spec],
        out_shape=[SDS((n, B_WIDTH), F32), SDS((n, B_WIDTH), F32)],
        scratch_shapes=[pltpu.VMEM((nseg, SEQ, LANE), F32), pltpu.VMEM((nseg, SEQ, LANE), F32)],
        args=[proj, proj, proj, bias], sem=("parallel", "arbitrary"), ride=ride)


def attn_bwd(proj, b_out, dmix, lse_tot, bias, nb_local, name, ride=None):
    n = proj.shape[0]
    nseg = len(DILATIONS)
    a_blocks = A_WIDTH // LANE
    scale = 1.0 / math.sqrt(HEAD_DIM)

    def body(q_ref, k_ref, v_ref, o_ref, do_ref, lse_ref, b_ref, dq_ref, dk_ref, dv_ref, db_ref, dqs_ref, delta_ref):
        @pl.when(pl.program_id(1) == 0)
        def _():
            db_ref[...] = jnp.zeros_like(db_ref)

        dk_ref[...] = jnp.zeros_like(dk_ref)
        dv_ref[...] = jnp.zeros_like(dv_ref)

        def row_dots(i, carry):
            rows = pl.ds(pl.multiple_of(i * ATTN_BLOCK, ATTN_BLOCK), ATTN_BLOCK)
            head0 = _head0_lanes()
            prod = do_ref[rows, :] * o_ref[rows, :]
            d0 = jnp.sum(jnp.where(head0, prod, 0.0), axis=-1, keepdims=True)
            d1 = jnp.sum(jnp.where(head0, 0.0, prod), axis=-1, keepdims=True)
            delta_ref[rows, :] = jnp.where(head0, d0, d1)
            return carry

        lax.fori_loop(0, SEQ // ATTN_BLOCK, row_dots, 0)

        def op(seg, dil, start, first):
            qrows = _rows(start, ATTN_BLOCK, dil)
            krows = _kv_rows(start, dil, first)
            q2, kb, vb = q_ref[qrows, :], k_ref[krows, :].astype(BF16), v_ref[krows, :].astype(BF16)
            do2, lse2, delta2 = do_ref[qrows, :], lse_ref[qrows, :], delta_ref[qrows, :]
            head0 = _head0_lanes()
            dqs, dk, dv = [], None, None
            for hh in range(2):
                col = slice(hh * HEAD_DIM, hh * HEAD_DIM + 1)
                q, dob = _one_head(q2, head0, hh), _one_head(do2, head0, hh)
                bb = b_ref[seg, hh, :, ATTN_BLOCK:] if first else b_ref[seg, hh]
                p = jnp.exp(_attn_scores(q, kb, bb) - lse2[:, col])
                dvh = _dot_tn(p.astype(BF16), dob)
                ds = p * (_dot_nt(dob, vb) - delta2[:, col])
                if first:
                    db_ref[seg, hh, :, ATTN_BLOCK:] += ds
                else:
                    db_ref[seg, hh] += ds
                dsb = ds.astype(BF16)
                dqs.append(_dot(dsb, kb))
                dkh = _dot_tn(dsb, q)
                dk = dkh if dk is None else dk + dkh
                dv = dvh if dv is None else dv + dvh
            dqs_ref[seg, qrows, :] = jnp.where(head0, dqs[0], dqs[1]) * scale
            dk_ref[krows, :] += dk * scale
            dv_ref[krows, :] += dv

        _attn_schedule(op)

        def merge(i, carry):
            rows = pl.ds(pl.multiple_of(i * MERGE_ROWS, MERGE_ROWS), MERGE_ROWS)
            dq_ref[rows, :] = dqs_ref[0, rows, :] + dqs_ref[1, rows, :] + dqs_ref[2, rows, :]
            return carry

        lax.fori_loop(0, SEQ // MERGE_ROWS, merge, 0)

    def pspec(off):
        return pl.BlockSpec((SEQ, LANE), lambda p, b: (b, off // LANE + p))

    ospec = pl.BlockSpec((SEQ, LANE), lambda p, b: (b, p))
    bspec = pl.BlockSpec((nseg, 2, ATTN_BLOCK, 2 * ATTN_BLOCK), lambda p, b: (0, p, 0, 0))
    gshape = SDS((n, B_WIDTH), F32)
    return _call(
        body, name=name, grid=(HEAD_PAIRS, nb_local),
        in_specs=[pspec(Q_OFF), pspec(K_OFF), pspec(V_OFF), ospec,
                  pl.BlockSpec((SEQ, LANE), lambda p, b: (b, a_blocks + p)), ospec, bspec],
        out_specs=[ospec, ospec, ospec, bspec],
        out_shape=[gshape, gshape, gshape, SDS((nseg, B_HEADS, ATTN_BLOCK, 2 * ATTN_BLOCK), F32)],
        scratch_shapes=[pltpu.VMEM((nseg, SEQ, LANE), F32), pltpu.VMEM((SEQ, LANE), F32)],
        args=[proj, proj, proj, b_out, dmix, lse_tot, bias], sem=("arbitrary", "arbitrary"), ride=ride)


PAD = 8
CONV_ROWS = 64


def _conv_taps(gp_ref, head_ref, r0):
    g0 = gp_ref[r0:r0 + CONV_ROWS, :]
    if r0 == 0:
        return g0, head_ref[PAD - 1:PAD - 1 + CONV_ROWS, :], head_ref[PAD - 2:PAD - 2 + CONV_ROWS, :]
    return g0, gp_ref[r0 - 1:r0 - 1 + CONV_ROWS, :], gp_ref[r0 - 2:r0 - 2 + CONV_ROWS, :]


def _fill_head(gp_ref, head_ref):
    head_ref[0:PAD, :] = jnp.zeros((PAD, LANE), F32)
    head_ref[PAD:PAD + CONV_ROWS, :] = gp_ref[0:CONV_ROWS, :]


def conv_gelu_fwd(gp, up, cw, cb, nb_local, name):
    n, f = gp.shape

    def body(gp_ref, up_ref, cw_ref, cb_ref, o_ref, head_ref):
        _fill_head(gp_ref, head_ref)
        w0, w1, w2, bias = cw_ref[0:1, :], cw_ref[1:2, :], cw_ref[2:3, :], cb_ref[...]
        for r0 in range(0, SEQ, CONV_ROWS):
            g0, g1, g2 = _conv_taps(gp_ref, head_ref, r0)
            c = bias + w0 * g2 + w1 * g1 + w2 * g0
            o_ref[r0:r0 + CONV_ROWS, :] = (_gelu(c) * up_ref[r0:r0 + CONV_ROWS, :]).astype(BF16)

    blk = pl.BlockSpec((SEQ, LANE), lambda b, j: (b, j))
    return pl.pallas_call(
        body, name=name, grid=(nb_local, f // LANE),
        in_specs=[blk, blk, pl.BlockSpec((3, LANE), lambda b, j: (0, j)), pl.BlockSpec((1, LANE), lambda b, j: (0, j))],
        out_specs=blk,
        out_shape=SDS((n, f), BF16),
        scratch_shapes=[pltpu.VMEM((PAD + CONV_ROWS, LANE), F32)],
        compiler_params=_params(("parallel", "parallel")),
    )(gp, up, cw, cb)


def conv_gelu_bwd(dgu, gp, up, cw, cb, nb_local, name, ride=None):
    n, f = gp.shape

    def fold(v):
        return jnp.sum(v.reshape(CONV_ROWS // 8, 8, LANE), axis=0)

    def body(dgu_ref, gp_ref, up_ref, cw_ref, cb_ref, dgp_ref, dup_ref, dcw_ref, dcb_ref, head_ref, dc_ref):
        b = pl.program_id(1)
        _fill_head(gp_ref, head_ref)
        dc_ref[SEQ:SEQ + PAD, :] = jnp.zeros((PAD, LANE), F32)
        w0, w1, w2, bias = cw_ref[0:1, :], cw_ref[1:2, :], cw_ref[2:3, :], cb_ref[...]
        sums = [jnp.zeros((8, LANE), F32) for _ in range(4)]
        for r0 in range(0, SEQ, CONV_ROWS):
            rows = slice(r0, r0 + CONV_ROWS)
            g0, g1, g2 = _conv_taps(gp_ref, head_ref, r0)
            gg, dgg = _gelu_and_grad(bias + w0 * g2 + w1 * g1 + w2 * g0)
            dgu = dgu_ref[rows, :].astype(F32)
            dup_ref[rows, :] = (dgu * gg).astype(BF16)
            dc = dgu * up_ref[rows, :] * dgg
            dc_ref[rows, :] = dc
            sums = [sums[0] + fold(dc * g2), sums[1] + fold(dc * g1), sums[2] + fold(dc * g0), sums[3] + fold(dc)]
        for r0 in range(0, SEQ, CONV_ROWS):
            dgp_ref[r0:r0 + CONV_ROWS, :] = (
                w2 * dc_ref[r0:r0 + CONV_ROWS, :] + w1 * dc_ref[r0 + 1:r0 + 1 + CONV_ROWS, :]
                + w0 * dc_ref[r0 + 2:r0 + 2 + CONV_ROWS, :]).astype(BF16)
        dcw = jnp.concatenate([jnp.sum(s, axis=0, keepdims=True) for s in sums[:3]], axis=0)
        dcb = jnp.sum(sums[3], axis=0, keepdims=True)

        @pl.when(b == 0)
        def _():
            dcw_ref[...] = dcw
            dcb_ref[...] = dcb

        @pl.when(b > 0)
        def _():
            dcw_ref[...] += dcw
            dcb_ref[...] += dcb

    blk = pl.BlockSpec((SEQ, LANE), lambda j, b: (b, j))
    return _call(
        body, name=name, grid=(f // LANE, nb_local),
        in_specs=[blk, blk, blk, pl.BlockSpec((3, LANE), lambda j, b: (0, j)), pl.BlockSpec((1, LANE), lambda j, b: (0, j))],
        out_specs=[blk, blk, pl.BlockSpec((3, LANE), lambda j, b: (0, j)), pl.BlockSpec((1, LANE), lambda j, b: (0, j))],
        out_shape=[SDS((n, f), BF16), SDS((n, f), BF16), SDS((3, f), F32), SDS((1, f), F32)],
        scratch_shapes=[pltpu.VMEM((PAD + CONV_ROWS, LANE), F32), pltpu.VMEM((SEQ + PAD, LANE), F32)],
        args=[dgu, gp, up, cw, cb], sem=("parallel", "arbitrary"), ride=ride)


def norm_mid_epilogue(x1, dout, z2, g3, g2):
    n, d = x1.shape

    def fn(dh2, step, x1_ref, dout_ref, z2_ref, g3_ref, g2_ref, dx1_ref, dz2_ref, dg3_ref, dg2_ref):
        dxa, dg3r = _rms_bwd(dh2, x1_ref[...], g3_ref[...])
        dx1 = dout_ref[...] + dxa
        dx1_ref[...] = dx1
        dz2, dg2r = _rms_bwd(dx1, z2_ref[...], g2_ref[...])
        dz2_ref[...] = dz2.astype(BF16)
        _accumulate(dg3_ref, jnp.sum(dg3r, axis=0, keepdims=True), step)
        _accumulate(dg2_ref, jnp.sum(dg2r, axis=0, keepdims=True), step)

    return fn, [x1, dout, z2, g3, g2], [SDS((n, d), F32), SDS((n, d), BF16), SDS((1, d), F32), SDS((1, d), F32)]


def norm_in_epilogue(x, dx1, g1):
    n, d = x.shape

    def fn(dh1, step, x_ref, dx1_ref, g1_ref, dx_ref, dg1_ref):
        dxa, dgr = _rms_bwd(dh1, x_ref[...], g1_ref[...])
        dx_ref[...] = dx1_ref[...] + dxa
        _accumulate(dg1_ref, jnp.sum(dgr, axis=0, keepdims=True), step)

    return fn, [x, dx1, g1], [SDS((n, d), F32), SDS((1, d), F32)]


def cast_bf16(arrays, name):
    def body(*refs):
        for i_ref, o_ref in zip(refs[:len(arrays)], refs[len(arrays):]):
            o_ref[...] = i_ref[...].astype(BF16)

    return pl.pallas_call(body, name=name, out_shape=[SDS(a.shape, BF16) for a in arrays],
                          compiler_params=_params())(*arrays)


def adam_update(parts, w, m, v, name, tr=None):
    s, r, c = parts.shape
    tr = r if tr is None else tr
    bc1 = 1.0 - ADAM_B1 ** ADAM_STEP
    bc2 = 1.0 - ADAM_B2 ** ADAM_STEP

    def body(p_ref, w_ref, m_ref, v_ref, g_ref, d_ref, nm_ref, nv_ref):
        g = p_ref[0].astype(F32)
        for j in range(1, s):
            g = g + p_ref[j].astype(F32)
        nm = ADAM_B1 * m_ref[...] + (1.0 - ADAM_B1) * g
        nv = ADAM_B2 * v_ref[...] + (1.0 - ADAM_B2) * (g * g)
        g_ref[...] = g
        nm_ref[...] = nm
        nv_ref[...] = nv
        d_ref[...] = -ADAM_LR * ((nm / bc1) / (jnp.sqrt(nv / bc2) + ADAM_EPS) + ADAM_WD * w_ref[...])

    blk = pl.BlockSpec((tr, c), lambda i: (i, 0))
    return pl.pallas_call(
        body, name=name, grid=(r // tr,),
        in_specs=[pl.BlockSpec((s, tr, c), lambda i: (0, i, 0)), blk, blk, blk],
        out_specs=[blk] * 4, out_shape=[SDS((r, c), F32)] * 4,
        compiler_params=_params(("parallel",)),
    )(parts, w, m, v)


EARLY_NAMES = ("spatial_w", "norm_mix_post", "norm_ffn_pre", "norm_ffn_post", "conv_b", "ln_v_gain", "ln_v_bias",
               "spatial_b")
LATE_NAMES = ("norm_mix_pre", "rel_bias")
PACK_ROW_ALIGN = 8


def _pack_rows(size):
    rows = -(-size // LANE)
    return -(-rows // PACK_ROW_ALIGN) * PACK_ROW_ALIGN


def _pack(arrays):
    flat = []
    for a in arrays:
        rows = _pack_rows(a.size)
        flat.append(jnp.pad(a.reshape(-1), (0, rows * LANE - a.size)))
    return jnp.concatenate(flat).reshape(-1, LANE)


def _unpack(packed, shapes):
    out, row = [], 0
    for shp in shapes:
        size = int(np.prod(shp))
        out.append(packed[row:row + _pack_rows(size)].reshape(-1)[:size].reshape(shp))
        row += _pack_rows(size)
    return out


def kernel(x, norm_mix_pre, norm_mix_post, norm_ffn_pre, norm_ffn_post, w_in, ln_v_gain, ln_v_bias, spatial_w, spatial_b, rel_bias, w_out, w_gate, w_up, conv_w, conv_b, w_down, loss_target, m_norm_mix_pre, m_norm_mix_post, m_norm_ffn_pre, m_norm_ffn_post, m_w_in, m_ln_v_gain, m_ln_v_bias, m_spatial_w, m_spatial_b, m_rel_bias, m_w_out, m_w_gate, m_w_up, m_conv_w, m_conv_b, m_w_down, v_norm_mix_pre, v_norm_mix_post, v_norm_ffn_pre, v_norm_ffn_post, v_w_in, v_ln_v_gain, v_ln_v_bias, v_spatial_w, v_spatial_b, v_rel_bias, v_w_out, v_w_gate, v_w_up, v_conv_w, v_conv_b, v_w_down):
    given = dict(locals())
    nb_local, seq, d = x.shape
    n = nb_local * seq
    cols = w_in.shape[2]

    def by_columns(g):
        return g.transpose(1, 0, 2).reshape(g.shape[1], N_DEV * g.shape[2])

    def to_blocks(g):
        return g.reshape(g.shape[0], N_DEV, cols).transpose(1, 0, 2)

    xf, target = x.reshape(n, d), loss_target.reshape(n, d)
    ln_g, ln_b = ln_v_gain.reshape(1, A_WIDTH), ln_v_bias.reshape(1, A_WIDTH)
    spatial_bt, rel_bias_t = spatial_b[0].T, rel_bias.T

    s_in, s_out, s_gate, s_up, s_down = cast_bf16([w_in[0], w_out[0], w_gate[0], w_up[0], w_down[0]], "cast_shards")
    g_in, g_cw = exchange([], [s_in, conv_w[0]], "gather_w_in")
    w_in_f, conv_w_f = by_columns(g_in), by_columns(g_cw)

    (h1, proj), _ = norm_mm(xf, norm_mix_pre, [w_in_f], "fwd_norm_in")
    a = gating_fwd(proj, ln_g, ln_b, spatial_w[0], spatial_bt, "fwd_gating")
    bias = bias_tables(rel_bias_t, "bias_tables").reshape(len(DILATIONS), B_HEADS, ATTN_BLOCK, 2 * ATTN_BLOCK)
    (b_out, lse_tot), (g_out, g_gate, g_up) = attn_fwd(proj, bias, nb_local, "fwd_attn",
                                                       ride=([], [s_out, s_gate, s_up]))
    w_out_f, w_gate_f, w_up_f = g_out.reshape(D_MODEL, D_MODEL), by_columns(g_gate), by_columns(g_up)
    z2, x1 = mm_res_norm([a, b_out], w_out_f, xf, norm_mix_post, "fwd_out_norm")
    (h2, gp, up), (g_down,) = norm_mm(x1, norm_ffn_pre, [w_gate_f, w_up_f], "fwd_norm_ffn", ride=([], [s_down]))
    w_down_f = g_down.reshape(D_FF, D_MODEL)
    gu = conv_gelu_fwd(gp, up, conv_w_f, conv_b, nb_local, "fwd_conv_gelu")
    dy, dout, dg4, loss_part = down_loss(gu, w_down_f, x1, norm_ffn_post, target, "fwd_down_loss")

    p_down = mm_tn([gu], [dy], "bwd_dw_down", t1=1408)
    (dgu,), _ = mm_nt([(dy, 0, 0)], [w_down_f], "bwd_dgu", out_dtype=BF16)
    (dgp, dup, p_conv_w, p_conv_b), (r_down,) = conv_gelu_bwd(
        dgu, gp, up, conv_w_f, conv_b, nb_local, "bwd_conv_gelu", ride=([p_down.reshape(N_DEV, cols, D_MODEL)], []))
    p_gate = mm_tn([h2], [dgp], "bwd_dw_gate")
    p_up = mm_tn([h2], [dup], "bwd_dw_up")
    (dx1, dz2, dg3, dg2), _ = mm_nt([(dgp, 0, 0), (dup, 1, 0)], [w_gate_f, w_up_f], "bwd_dh2_norm_mid", tm=256,
                                    epilogue=norm_mid_epilogue(x1, dout, z2, norm_ffn_pre, norm_mix_post))
    p_out = mm_tn([a, b_out], [dz2], "bwd_dw_out")
    (dmix,), _ = mm_nt([(dz2, 0, 0)], [w_out_f], "bwd_dmix")
    duv, p_ws, p_sbt, p_lng, p_lnb = gating_bwd(proj, dmix, ln_g, ln_b, spatial_w[0], spatial_bt, "bwd_gating")
    small = dict(spatial_w=p_ws, norm_mix_post=dg2, norm_ffn_pre=dg3, norm_ffn_post=dg4, conv_b=p_conv_b,
                 ln_v_gain=p_lng, ln_v_bias=p_lnb, spatial_b=p_sbt.T)
    pack_early = _pack([small[k] for k in EARLY_NAMES] + [p_conv_w, loss_part])
    (dq, dk, dv, dbias), (r_gate, r_up, r_out, r_early) = attn_bwd(
        proj, b_out, dmix, lse_tot, bias, nb_local, "bwd_attn",
        ride=([to_blocks(p_gate), to_blocks(p_up), p_out.reshape(N_DEV, D_MODEL // N_DEV, D_MODEL)], [pack_early]))
    p_rel_bias_t = rel_bias_grad(dbias.reshape(len(DILATIONS), B_HEADS, BIAS_SIZE), "bwd_rel_bias")
    p_in = mm_tn([h1], [duv, dq, dk, dv], "bwd_dw_in")
    (grad_x, dg1), (r_in,) = mm_nt(
        [(duv, 0, 0), (dq, 0, Q_OFF), (dk, 0, K_OFF), (dv, 0, V_OFF)], [w_in_f], "bwd_dh1_norm_in",
        epilogue=norm_in_epilogue(xf, dx1, norm_mix_pre), ride=([to_blocks(p_in)], []))
    small.update(norm_mix_pre=dg1, rel_bias=p_rel_bias_t.T)
    (r_late,) = exchange([], [_pack([small[k] for k in LATE_NAMES])], "exchange_late")

    res = {}
    res["w_in"] = adam_update(r_in, w_in[0], m_w_in[0], v_w_in[0], "adam_w_in", tr=256)
    res["w_out"] = adam_update(r_out, w_out[0], m_w_out[0], v_w_out[0], "adam_w_out")
    res["w_gate"] = adam_update(r_gate, w_gate[0], m_w_gate[0], v_w_gate[0], "adam_w_gate", tr=256)
    res["w_up"] = adam_update(r_up, w_up[0], m_w_up[0], v_w_up[0], "adam_w_up", tr=256)
    res["w_down"] = adam_update(r_down, w_down[0], m_w_down[0], v_w_down[0], "adam_w_down", tr=176)

    def adam_packed(received, names, tail, name):
        zeros = [jnp.zeros_like(t) for t in tail]
        packs = [_pack([given[pre + k] for k in names] + zeros) for pre in ("", "m_", "v_")]
        shapes = [given[k].shape for k in names] + [t.shape for t in tail]
        unpacked = [_unpack(p, shapes) for p in adam_update(received, *packs, name)]
        for i, k in enumerate(names):
            res[k] = [u[i] for u in unpacked]
        return unpacked[0][len(names):]

    g_conv_w_full, loss_sum = adam_packed(r_early, EARLY_NAMES, [p_conv_w, loss_part], "adam_small_early")
    adam_packed(r_late, LATE_NAMES, [], "adam_small_late")
    g_conv_w = lax.dynamic_slice_in_dim(g_conv_w_full, _my_index() * cols, cols, axis=1)
    res["conv_w"] = adam_update(g_conv_w[None], conv_w[0], m_conv_w[0], v_conv_w[0], "adam_conv_w")
    loss = loss_sum[0, 0]

    names = ("norm_mix_pre", "norm_mix_post", "norm_ffn_pre", "norm_ffn_post", "w_in", "ln_v_gain", "ln_v_bias",
             "spatial_w", "spatial_b", "rel_bias", "w_out", "w_gate", "w_up", "conv_w", "conv_b", "w_down")
    outs = [loss, grad_x.reshape(x.shape)]
    for t in range(4):
        outs += [res[k][t].reshape(given[k].shape) for k in names]
    return tuple(outs)
```

```python
import functools
import math

import numpy as np
import jax
import jax.numpy as jnp
from jax import lax
from jax.experimental import pallas as pl
from jax.experimental.pallas import tpu as pltpu

F32 = jnp.float32
BF16 = jnp.bfloat16
SDS = jax.ShapeDtypeStruct

D_MODEL = 1024
SEQ = 2048
HEAD_DIM = 64
A_GROUPS = 4
A_WIDTH = A_GROUPS * HEAD_DIM
B_HEADS = 12
B_WIDTH = B_HEADS * HEAD_DIM
HEAD_PAIRS = B_HEADS // 2
CHUNK = 128
ATTN_BLOCK = 128
DILATIONS = (1, 4, 16)
NUM_BUCKETS = 32
MAX_DISTANCE = 2048
D_FF = 2816
IN_COLS = 2 * A_WIDTH + 3 * B_WIDTH
Q_OFF = 2 * A_WIDTH
K_OFF = Q_OFF + B_WIDTH
V_OFF = K_OFF + B_WIDTH
NORM_EPS = 1e-6
NEG_INF = -1e30
N_DEV = 8
LANE = 128

ADAM_LR = 0.001
ADAM_B1 = 0.9
ADAM_B2 = 0.999
ADAM_EPS = 1e-08
ADAM_WD = 0.01
ADAM_STEP = 10

GELU_C0 = math.sqrt(2.0 / math.pi)
GELU_C1 = 0.044715

VMEM_LIMIT = 56 * 1024 * 1024


def _params(sem=None):
    if sem is None:
        return pltpu.CompilerParams(vmem_limit_bytes=VMEM_LIMIT)
    return pltpu.CompilerParams(dimension_semantics=sem, vmem_limit_bytes=VMEM_LIMIT)


def _gelu(x):
    t = jnp.tanh(GELU_C0 * (x + GELU_C1 * x * x * x))
    return 0.5 * x * (1.0 + t)


def _gelu_and_grad(x):
    x2 = x * x
    t = jnp.tanh(GELU_C0 * (x + GELU_C1 * x * x2))
    g = 0.5 * x * (1.0 + t)
    dg = 0.5 * (1.0 + t) + 0.5 * x * (1.0 - t * t) * (GELU_C0 * (1.0 + 3.0 * GELU_C1 * x2))
    return g, dg


def _dot(a, b):
    return jnp.dot(a, b, preferred_element_type=F32)


def _dot_nt(a, b):
    return lax.dot_general(a, b, (((1,), (1,)), ((), ())), preferred_element_type=F32)


def _dot_tn(a, b):
    return lax.dot_general(a, b, (((0,), (0,)), ((), ())), preferred_element_type=F32)


def _rms_bwd(d, xin, g):
    r = lax.rsqrt(jnp.mean(xin * xin, axis=-1, keepdims=True) + NORM_EPS)
    xh = xin * r
    gd = g * d
    dx = r * (gd - xh * jnp.mean(gd * xh, axis=-1, keepdims=True))
    return dx, d * xh


MESH = pl.DeviceIdType.MESH
ANY = pl.BlockSpec(memory_space=pl.ANY)
PEER_MASKS = tuple(range(1, N_DEV))


def _my_index():
    return lax.axis_index("x") * 4 + lax.axis_index("y") * 2 + lax.axis_index("c")


def _peer(mask):
    x, y, c = lax.axis_index("x"), lax.axis_index("y"), lax.axis_index("c")
    px = 1 - x if mask & 4 else x
    py = 1 - y if mask & 2 else y
    pc = 1 - c if mask & 1 else c
    return (px, py, pc), px * 4 + py * 2 + pc


def _exchange_copies(nblocked, in_refs, out_refs, sems):
    send_sems, recv_sems, local_sems = sems
    me = _my_index()
    local, sends, recvs = [], [], []
    for a, (in_ref, out_ref) in enumerate(zip(in_refs, out_refs)):
        src_of = (lambda idx, r=in_ref: r.at[idx]) if a < nblocked else (lambda idx, r=in_ref: r)
        local.append(pltpu.make_async_copy(src_of(me), out_ref.at[me], local_sems.at[a]))
        for mask in PEER_MASKS:
            peer, pidx = _peer(mask)
            pair = dict(send_sem=send_sems.at[a, mask - 1], recv_sem=recv_sems.at[a, mask - 1],
                        device_id=peer, device_id_type=MESH)
            sends.append(pltpu.make_async_remote_copy(src_ref=src_of(pidx), dst_ref=out_ref.at[me], **pair))
            recvs.append(pltpu.make_async_remote_copy(src_ref=src_of(pidx), dst_ref=out_ref.at[pidx], **pair))
    return local, sends, recvs


def _exchange_start(nblocked, in_refs, out_refs, sems):
    local, sends, _ = _exchange_copies(nblocked, in_refs, out_refs, sems)
    for cp in local + sends:
        cp.start()


def _exchange_wait(nblocked, in_refs, out_refs, sems):
    local, sends, recvs = _exchange_copies(nblocked, in_refs, out_refs, sems)
    for cp in sends:
        cp.wait_send()
    for cp in recvs:
        cp.wait_recv()
    for cp in local:
        cp.wait()


def _exchange_out_shape(blocked, whole):
    return [SDS(b.shape, b.dtype) for b in blocked] + [SDS((N_DEV,) + w.shape, w.dtype) for w in whole]


def _exchange_sems(n):
    return [pltpu.SemaphoreType.DMA((n, N_DEV - 1)), pltpu.SemaphoreType.DMA((n, N_DEV - 1)),
            pltpu.SemaphoreType.DMA((n,))]


def exchange(blocked, whole, name):
    nb, n = len(blocked), len(blocked) + len(whole)

    def body(*refs):
        _exchange_start(nb, refs[:n], refs[n:2 * n], refs[2 * n:])
        _exchange_wait(nb, refs[:n], refs[n:2 * n], refs[2 * n:])

    return pl.pallas_call(
        body, name=name, in_specs=[ANY] * n, out_specs=[ANY] * n, out_shape=_exchange_out_shape(blocked, whole),
        scratch_shapes=_exchange_sems(n),
    )(*blocked, *whole)


def _call(body, *, name, grid, in_specs, out_specs, out_shape, args, scratch_shapes=(), sem=None, ride=None):
    out_shape, out_specs, scratch_shapes = list(out_shape), list(out_specs), list(scratch_shapes)
    if ride is None:
        outs = pl.pallas_call(body, name=name, grid=grid, in_specs=list(in_specs), out_specs=out_specs,
                              out_shape=out_shape, scratch_shapes=scratch_shapes,
                              compiler_params=_params(sem))(*args)
        return list(outs), []
    blocked, whole = ride
    cargs = list(blocked) + list(whole)
    nb, nc = len(blocked), len(cargs)
    n_in, n_out, n_scr = len(args), len(out_shape), len(scratch_shapes)

    def riding(*refs):
        ins, refs = refs[:n_in], refs[n_in:]
        cins, refs = refs[:nc], refs[nc:]
        outs, refs = refs[:n_out], refs[n_out:]
        couts, refs = refs[:nc], refs[nc:]
        scr, sems = refs[:n_scr], refs[n_scr:]
        ids = [pl.program_id(k) for k in range(len(grid))]
        first = functools.reduce(lambda p, q: p & q, [i == 0 for i in ids])
        last = functools.reduce(lambda p, q: p & q, [i == g - 1 for i, g in zip(ids, grid)])

        @pl.when(first)
        def _():
            _exchange_start(nb, cins, couts, sems)

        body(*ins, *outs, *scr)

        @pl.when(last)
        def _():
            _exchange_wait(nb, cins, couts, sems)

    res = pl.pallas_call(
        riding, name=name, grid=grid, in_specs=list(in_specs) + [ANY] * nc, out_specs=out_specs + [ANY] * nc,
        out_shape=out_shape + _exchange_out_shape(blocked, whole),
        scratch_shapes=scratch_shapes + _exchange_sems(nc),
        compiler_params=_params(("arbitrary",) * len(grid)))(*args, *cargs)
    return list(res[:n_out]), list(res[n_out:])


def norm_mm(x, g, ws, name, tm=512, tn=1408, ride=None):
    n, d = x.shape
    f = ws[0].shape[1]
    nw = len(ws)

    def body(x_ref, g_ref, *refs):
        w_refs = refs[:nw]
        h_ref = refs[nw]
        o_refs = refs[nw + 1:]

        @pl.when(pl.program_id(1) == 0)
        def _():
            xv = x_ref[...]
            r = lax.rsqrt(jnp.mean(xv * xv, axis=-1, keepdims=True) + NORM_EPS)
            h_ref[...] = (xv * r * g_ref[...]).astype(BF16)

        h = h_ref[...]
        for w_ref, o_ref in zip(w_refs, o_refs):
            o_ref[...] = _dot(h, w_ref[...])

    return _call(
        body, name=name, grid=(n // tm, f // tn),
        in_specs=[pl.BlockSpec((tm, d), lambda i, j: (i, 0)), pl.BlockSpec((1, d), lambda i, j: (0, 0))]
        + [pl.BlockSpec((d, tn), lambda i, j: (0, j)) for _ in ws],
        out_specs=[pl.BlockSpec((tm, d), lambda i, j: (i, 0))]
        + [pl.BlockSpec((tm, tn), lambda i, j: (i, j)) for _ in ws],
        out_shape=[SDS((n, d), BF16)] + [SDS((n, f), F32) for _ in ws],
        args=[x, g, *ws], sem=("parallel", "arbitrary"), ride=ride)


def _lane_concat(refs):
    vals = [r[...].astype(BF16) for r in refs]
    return vals[0] if len(vals) == 1 else jnp.concatenate(vals, axis=1)


def mm_res_norm(a_list, w, res, g, name, tm=512):
    n = a_list[0].shape[0]
    k, d = w.shape
    na = len(a_list)

    def body(*refs):
        w_ref, res_ref, g_ref, y_ref, o_ref = refs[na:]
        y = _dot(_lane_concat(refs[:na]), w_ref[...])
        r = lax.rsqrt(jnp.mean(y * y, axis=-1, keepdims=True) + NORM_EPS)
        y_ref[...] = y
        o_ref[...] = res_ref[...] + y * r * g_ref[...]

    return pl.pallas_call(
        body, name=name, grid=(n // tm,),
        in_specs=[pl.BlockSpec((tm, a.shape[1]), lambda i: (i, 0)) for a in a_list]
        + [pl.BlockSpec((k, d), lambda i: (0, 0)),
           pl.BlockSpec((tm, d), lambda i: (i, 0)), pl.BlockSpec((1, d), lambda i: (0, 0))],
        out_specs=[pl.BlockSpec((tm, d), lambda i: (i, 0)), pl.BlockSpec((tm, d), lambda i: (i, 0))],
        out_shape=[SDS((n, d), F32), SDS((n, d), F32)],
        compiler_params=_params(("parallel",)),
    )(*a_list, w, res, g)


def down_loss(a, w, res, g, target, name, tm=256):
    n, k = a.shape
    d = w.shape[1]
    inv_d = 1.0 / d

    def body(a_ref, w_ref, res_ref, g_ref, t_ref, dy_ref, dout_ref, dg_ref, loss_ref):
        i = pl.program_id(0)
        y = _dot(a_ref[...], w_ref[...])
        gv = g_ref[...]
        r = lax.rsqrt(jnp.mean(y * y, axis=-1, keepdims=True) + NORM_EPS)
        yh = y * r
        e = res_ref[...] + yh * gv - t_ref[...]
        part = 0.5 * inv_d * jnp.sum(jnp.sum(e * e, axis=-1, keepdims=True), axis=0, keepdims=True)
        dout = e * inv_d
        dout_ref[...] = dout
        gd = gv * dout
        dy_ref[...] = (r * (gd - yh * jnp.mean(gd * yh, axis=-1, keepdims=True))).astype(BF16)
        dgp = jnp.sum(dout * yh, axis=0, keepdims=True)
        lane0 = lax.broadcasted_iota(jnp.int32, (1, LANE), 1) == 0
        lp = jnp.where(lane0, part, 0.0)

        @pl.when(i == 0)
        def _():
            dg_ref[...] = dgp
            loss_ref[...] = lp

        @pl.when(i > 0)
        def _():
            dg_ref[...] += dgp
            loss_ref[...] += lp

    return pl.pallas_call(
        body, name=name, grid=(n // tm,),
        in_specs=[pl.BlockSpec((tm, k), lambda i: (i, 0)), pl.BlockSpec((k, d), lambda i: (0, 0)),
                  pl.BlockSpec((tm, d), lambda i: (i, 0)), pl.BlockSpec((1, d), lambda i: (0, 0)),
                  pl.BlockSpec((tm, d), lambda i: (i, 0))],
        out_specs=[pl.BlockSpec((tm, d), lambda i: (i, 0)), pl.BlockSpec((tm, d), lambda i: (i, 0)),
                   pl.BlockSpec((1, d), lambda i: (0, 0)), pl.BlockSpec((1, LANE), lambda i: (0, 0))],
        out_shape=[SDS((n, d), BF16), SDS((n, d), F32), SDS((1, d), F32), SDS((1, LANE), F32)],
        compiler_params=_params(("arbitrary",)),
    )(a, w, res, g, target)


def _accumulate(ref, val, step):
    @pl.when(step == 0)
    def _():
        ref[...] = val

    @pl.when(step > 0)
    def _():
        ref[...] += val


def mm_nt(terms, ws, name, tm=512, out_dtype=F32, ride=None, epilogue=None):
    n = terms[0][0].shape[0]
    r = ws[0].shape[0]
    na = len(terms)
    meta = [(widx, off, a.shape[1]) for a, widx, off in terms]
    fn, extras, out_shape = epilogue if epilogue else (None, [], [SDS((n, r), out_dtype)])
    n_fixed = na + len(ws)

    def body(*refs):
        a_refs = refs[:na]
        w_refs = refs[na:n_fixed]
        acc = None
        for a_ref, (widx, off, k) in zip(a_refs, meta):
            p = _dot_nt(a_ref[...].astype(BF16), w_refs[widx][:, off:off + k])
            acc = p if acc is None else acc + p
        if fn is None:
            refs[-1][...] = acc.astype(out_dtype)
        else:
            fn(acc, pl.program_id(0), *refs[n_fixed:])

    def spec(a):
        if a.shape[0] == 1:
            return pl.BlockSpec(a.shape, lambda i: (0, 0))
        return pl.BlockSpec((tm, a.shape[1]), lambda i: (i, 0))

    return _call(
        body, name=name, grid=(n // tm,),
        in_specs=[spec(a) for a, _, _ in terms] + [pl.BlockSpec(w.shape, lambda i: (0, 0)) for w in ws]
        + [spec(e) for e in extras],
        out_specs=[spec(o) for o in out_shape], out_shape=out_shape,
        args=[a for a, _, _ in terms] + list(ws) + list(extras),
        sem=("parallel",) if fn is None else ("arbitrary",), ride=ride)


def _piece_blocks(pieces, tile):
    out, first = [], 0
    for p in pieces:
        nblk, rem = divmod(p.shape[1], tile)
        assert rem == 0, (p.shape, tile)
        out.append((first, nblk))
        first += nblk
    return out, first


def mm_tn(lhs_list, rhs_list, name, t1, t2, out_dtype=BF16):
    n = lhs_list[0].shape[0]
    lblocks, nbl = _piece_blocks(lhs_list, t1)
    rblocks, nbr = _piece_blocks(rhs_list, t2)
    nl = len(lhs_list)

    def body(*refs):
        l_refs, r_refs, o_ref = refs[:nl], refs[nl:-1], refs[-1]
        i, j = pl.program_id(0), pl.program_id(1)
        for l_ref, (ls, ln) in zip(l_refs, lblocks):
            for r_ref, (rs, rn) in zip(r_refs, rblocks):
                @pl.when((i >= ls) & (i < ls + ln) & (j >= rs) & (j < rs + rn))
                def _(l_ref=l_ref, r_ref=r_ref):
                    o_ref[...] = _dot_tn(l_ref[...].astype(BF16), r_ref[...].astype(BF16)).astype(out_dtype)

    def piece_spec(tile, axis, first, nblk):
        def index(i, j):
            return 0, jnp.clip((i, j)[axis] - first, 0, nblk - 1)
        return pl.BlockSpec((n, tile), index)

    return pl.pallas_call(
        body, name=name, grid=(nbl, nbr),
        in_specs=[piece_spec(t1, 0, *b) for b in lblocks] + [piece_spec(t2, 1, *b) for b in rblocks],
        out_specs=pl.BlockSpec((t1, t2), lambda i, j: (i, j)),
        out_shape=SDS((nbl * t1, nbr * t2), out_dtype),
        compiler_params=_params(("parallel", "arbitrary")),
    )(*lhs_list, *rhs_list)


GATE_ROWS = 512


def _tril_mask():
    row = lax.broadcasted_iota(jnp.int32, (CHUNK, CHUNK), 0)
    col = lax.broadcasted_iota(jnp.int32, (CHUNK, CHUNK), 1)
    return row >= col


def _layer_norm_parts(gv):
    mu = jnp.mean(gv, axis=-1, keepdims=True)
    xc = gv - mu
    rstd = lax.rsqrt(jnp.mean(xc * xc, axis=-1, keepdims=True) + NORM_EPS)
    return xc * rstd, rstd


def gating_fwd(proj, lng, lnb, ws, sbt, name):
    n = proj.shape[0]
    nchunk = GATE_ROWS // CHUNK

    def body(u_ref, v_ref, lng_ref, lnb_ref, ws_ref, sbt_ref, a_ref):
        tril = _tril_mask()
        for g in range(A_GROUPS):
            cs = slice(g * HEAD_DIM, (g + 1) * HEAD_DIM)
            wt = jnp.where(tril, ws_ref[g], 0.0).astype(BF16)
            for c in range(nchunk):
                rs_ = slice(c * CHUNK, (c + 1) * CHUNK)
                vhat, _ = _layer_norm_parts(_gelu(v_ref[rs_, cs]))
                vn = vhat * lng_ref[:, cs] + lnb_ref[:, cs]
                z = _dot(wt, vn.astype(BF16)) + sbt_ref[:, g:g + 1]
                a_ref[rs_, cs] = _gelu(u_ref[rs_, cs]) * z

    return pl.pallas_call(
        body, name=name, grid=(n // GATE_ROWS,),
        in_specs=[pl.BlockSpec((GATE_ROWS, A_WIDTH), lambda i: (i, 0)),
                  pl.BlockSpec((GATE_ROWS, A_WIDTH), lambda i: (i, 1)),
                  pl.BlockSpec((1, A_WIDTH), lambda i: (0, 0)), pl.BlockSpec((1, A_WIDTH), lambda i: (0, 0)),
                  pl.BlockSpec((A_GROUPS, CHUNK, CHUNK), lambda i: (0, 0, 0)),
                  pl.BlockSpec((CHUNK, A_GROUPS), lambda i: (0, 0))],
        out_specs=pl.BlockSpec((GATE_ROWS, A_WIDTH), lambda i: (i, 0)),
        out_shape=SDS((n, A_WIDTH), F32),
        compiler_params=_params(("parallel",)),
    )(proj, proj, lng, lnb, ws, sbt)


def gating_bwd(proj, dmix, lng, lnb, ws, sbt, name):
    n = proj.shape[0]
    nchunk = GATE_ROWS // CHUNK

    def body(u_ref, v_ref, da_ref, lng_ref, lnb_ref, ws_ref, sbt_ref,
             duv_ref, dws_ref, dsbt_ref, dlng_ref, dlnb_ref):
        @pl.when(pl.program_id(0) == 0)
        def _():
            dws_ref[...] = jnp.zeros_like(dws_ref)
            dsbt_ref[...] = jnp.zeros_like(dsbt_ref)
            dlng_ref[...] = jnp.zeros_like(dlng_ref)
            dlnb_ref[...] = jnp.zeros_like(dlnb_ref)

        tril = _tril_mask()
        for g in range(A_GROUPS):
            cs = slice(g * HEAD_DIM, (g + 1) * HEAD_DIM)
            wt = jnp.where(tril, ws_ref[g], 0.0).astype(BF16)
            lg = lng_ref[:, cs]
            dw = jnp.zeros((CHUNK, CHUNK), F32)
            dsb = jnp.zeros((CHUNK, 1), F32)
            dlg = jnp.zeros((1, HEAD_DIM), F32)
            dlb = jnp.zeros((1, HEAD_DIM), F32)
            for c in range(nchunk):
                rs_ = slice(c * CHUNK, (c + 1) * CHUNK)
                gu, dgu_dx = _gelu_and_grad(u_ref[rs_, cs])
                gv, dgv_dx = _gelu_and_grad(v_ref[rs_, cs])
                vhat, rstd = _layer_norm_parts(gv)
                vn = (vhat * lg + lnb_ref[:, cs]).astype(BF16)
                z = _dot(wt, vn) + sbt_ref[:, g:g + 1]
                da = da_ref[rs_, cs]
                dz = da * gu
                dzb = dz.astype(BF16)
                duv_ref[rs_, cs] = (da * z * dgu_dx).astype(BF16)
                dsb = dsb + jnp.sum(dz, axis=-1, keepdims=True)
                dw = dw + _dot_nt(dzb, vn)
                dvn = _dot_tn(wt, dzb)
                dlg = dlg + jnp.sum(dvn * vhat, axis=0, keepdims=True)
                dlb = dlb + jnp.sum(dvn, axis=0, keepdims=True)
                dvh = dvn * lg
                dgv = rstd * (dvh - jnp.mean(dvh, axis=-1, keepdims=True)
                              - vhat * jnp.mean(dvh * vhat, axis=-1, keepdims=True))
                duv_ref[rs_, A_WIDTH + g * HEAD_DIM:A_WIDTH + (g + 1) * HEAD_DIM] = (dgv * dgv_dx).astype(BF16)
            dws_ref[g] += jnp.where(tril, dw, 0.0)
            dsbt_ref[:, g:g + 1] += dsb
            dlng_ref[:, cs] += dlg
            dlnb_ref[:, cs] += dlb

    return pl.pallas_call(
        body, name=name, grid=(n // GATE_ROWS,),
        in_specs=[pl.BlockSpec((GATE_ROWS, A_WIDTH), lambda i: (i, 0)),
                  pl.BlockSpec((GATE_ROWS, A_WIDTH), lambda i: (i, 1)),
                  pl.BlockSpec((GATE_ROWS, A_WIDTH), lambda i: (i, 0)),
                  pl.BlockSpec((1, A_WIDTH), lambda i: (0, 0)), pl.BlockSpec((1, A_WIDTH), lambda i: (0, 0)),
                  pl.BlockSpec((A_GROUPS, CHUNK, CHUNK), lambda i: (0, 0, 0)),
                  pl.BlockSpec((CHUNK, A_GROUPS), lambda i: (0, 0))],
        out_specs=[pl.BlockSpec((GATE_ROWS, 2 * A_WIDTH), lambda i: (i, 0)),
                   pl.BlockSpec((A_GROUPS, CHUNK, CHUNK), lambda i: (0, 0, 0)),
                   pl.BlockSpec((CHUNK, A_GROUPS), lambda i: (0, 0)),
                   pl.BlockSpec((1, A_WIDTH), lambda i: (0, 0)), pl.BlockSpec((1, A_WIDTH), lambda i: (0, 0))],
        out_shape=[SDS((n, 2 * A_WIDTH), BF16), SDS((A_GROUPS, CHUNK, CHUNK), F32), SDS((CHUNK, A_GROUPS), F32),
                   SDS((1, A_WIDTH), F32), SDS((1, A_WIDTH), F32)],
        compiler_params=_params(("arbitrary",)),
    )(proj, proj, dmix, lng, lnb, ws, sbt)


def _t5_bucket_np(dist):
    max_exact = NUM_BUCKETS // 2
    dd = np.maximum(dist, 1).astype(np.float64)
    large = max_exact + np.log(dd / max_exact) / math.log(MAX_DISTANCE / max_exact) * (NUM_BUCKETS - max_exact)
    large = np.minimum(large.astype(np.int64), NUM_BUCKETS - 1)
    return np.where(dist < max_exact, dist, large)


def _bucket_tables():
    i = np.arange(ATTN_BLOCK)[:, None]
    j = np.arange(2 * ATTN_BLOCK)[None, :]
    rel = ATTN_BLOCK + i - j
    band = (rel >= 0) & (rel <= ATTN_BLOCK)
    tabs = []
    for dil in DILATIONS:
        b = _t5_bucket_np(np.maximum(rel, 0) * dil)
        tabs.append(np.where(band, b, -1).reshape(1, -1))
    return np.stack(tabs).astype(np.float32)


BIAS_SIZE = ATTN_BLOCK * 2 * ATTN_BLOCK


def bias_tables(rel_bias_t, name):
    idx = jnp.asarray(_bucket_tables())

    def body(rb_ref, idx_ref, o_ref):
        iv = idx_ref[0]
        bk = lax.broadcasted_iota(jnp.int32, (NUM_BUCKETS, BIAS_SIZE), 0).astype(F32)
        onehot = (bk == iv).astype(F32)
        t = jnp.dot(rb_ref[...], onehot, preferred_element_type=F32, precision=lax.Precision.HIGHEST)
        o_ref[0] = jnp.where(iv < 0.0, NEG_INF, t)

    return pl.pallas_call(
        body, name=name, grid=(len(DILATIONS),),
        in_specs=[pl.BlockSpec((B_HEADS, NUM_BUCKETS), lambda d: (0, 0)),
                  pl.BlockSpec((1, 1, BIAS_SIZE), lambda d: (d, 0, 0))],
        out_specs=pl.BlockSpec((1, B_HEADS, BIAS_SIZE), lambda d: (d, 0, 0)),
        out_shape=SDS((len(DILATIONS), B_HEADS, BIAS_SIZE), F32),
        compiler_params=_params(("parallel",)),
    )(rel_bias_t, idx)


def rel_bias_grad(dbias, name):
    idx = jnp.asarray(_bucket_tables())

    def body(db_ref, idx_ref, o_ref):
        d = pl.program_id(0)
        iv = idx_ref[0]
        bk = lax.broadcasted_iota(jnp.int32, (NUM_BUCKETS, BIAS_SIZE), 0).astype(F32)
        onehot = (bk == iv).astype(F32)
        part = lax.dot_general(db_ref[0], onehot, (((1,), (1,)), ((), ())),
                               preferred_element_type=F32, precision=lax.Precision.HIGHEST)

        @pl.when(d == 0)
        def _():
            o_ref[...] = part

        @pl.when(d > 0)
        def _():
            o_ref[...] += part

    return pl.pallas_call(
        body, name=name, grid=(len(DILATIONS),),
        in_specs=[pl.BlockSpec((1, B_HEADS, BIAS_SIZE), lambda d: (d, 0, 0)),
                  pl.BlockSpec((1, 1, BIAS_SIZE), lambda d: (d, 0, 0))],
        out_specs=pl.BlockSpec((B_HEADS, NUM_BUCKETS), lambda d: (0, 0)),
        out_shape=SDS((B_HEADS, NUM_BUCKETS), F32),
        compiler_params=_params(("arbitrary",)),
    )(dbias, idx)


def _attn_scores(q, kk, bias):
    return _dot_nt(q, kk) * (1.0 / math.sqrt(HEAD_DIM)) + bias


def _head0_lanes():
    return lax.broadcasted_iota(jnp.int32, (ATTN_BLOCK, LANE), 1) < HEAD_DIM


def _one_head(x2, head0, hh):
    return jnp.where(head0 if hh == 0 else jnp.logical_not(head0), x2, 0.0).astype(BF16)


def _rows(start, size, dil):
    return pl.ds(start, size) if dil == 1 else pl.ds(start, size, stride=dil)


def _attn_schedule(op):
    span1, span4 = ATTN_BLOCK, 4 * ATTN_BLOCK

    def d16(i, carry):
        for t in range(4):
            op(2, 16, 4 * i + t, True)
        return carry

    lax.fori_loop(0, 4, d16, 0)
    for r in range(4):
        op(1, 4, r, True)

    def d4(nq, carry):
        for r in range(4):
            op(1, 4, r + nq * span4, False)
        return carry

    lax.fori_loop(1, SEQ // span4, d4, 0)
    op(0, 1, 0, True)

    def d1(j, carry):
        for t in range(3):
            op(0, 1, pl.multiple_of((1 + 3 * j + t) * span1, span1), False)
        return carry

    lax.fori_loop(0, (SEQ // span1 - 1) // 3, d1, 0)


def _kv_rows(start, dil, first):
    if first:
        return _rows(start, ATTN_BLOCK, dil)
    return _rows(start - ATTN_BLOCK * dil, 2 * ATTN_BLOCK, dil)


MERGE_ROWS = 256


def attn_fwd(proj, bias, nb_local, name, ride=None):
    n = proj.shape[0]
    nseg = len(DILATIONS)

    def body(q_ref, k_ref, v_ref, b_ref, o_ref, lse_ref, os_ref, ls_ref):
        def op(seg, dil, start, first):
            qrows = _rows(start, ATTN_BLOCK, dil)
            krows = _kv_rows(start, dil, first)
            q2, kb, vb = q_ref[qrows, :], k_ref[krows, :].astype(BF16), v_ref[krows, :].astype(BF16)
            head0 = _head0_lanes()
            outs, lses = [], []
            for hh in range(2):
                bb = b_ref[seg, hh, :, ATTN_BLOCK:] if first else b_ref[seg, hh]
                s = _attn_scores(_one_head(q2, head0, hh), kb, bb)
                m = jnp.max(s, axis=-1, keepdims=True)
                p = jnp.exp(s - m)
                l = jnp.sum(p, axis=-1, keepdims=True)
                outs.append(_dot(p.astype(BF16), vb) / l)
                lses.append(jnp.broadcast_to(m + jnp.log(l), (ATTN_BLOCK, LANE)))
            os_ref[seg, qrows, :] = jnp.where(head0, outs[0], outs[1])
            ls_ref[seg, qrows, :] = jnp.where(head0, lses[0], lses[1])

        _attn_schedule(op)

        def merge(i, carry):
            rows = pl.ds(pl.multiple_of(i * MERGE_ROWS, MERGE_ROWS), MERGE_ROWS)
            ls = [ls_ref[s, rows, :] for s in range(nseg)]
            m = functools.reduce(jnp.maximum, ls)
            ws = [jnp.exp(l - m) for l in ls]
            den = ws[0] + ws[1] + ws[2]
            num = ws[0] * os_ref[0, rows, :] + ws[1] * os_ref[1, rows, :] + ws[2] * os_ref[2, rows, :]
            o_ref[rows, :] = num / den
            lse_ref[rows, :] = m + jnp.log(den)
            return carry

        lax.fori_loop(0, SEQ // MERGE_ROWS, merge, 0)

    def in_spec(off):
        return pl.BlockSpec((SEQ, LANE), lambda b, p: (b, off // LANE + p))

    out_spec = pl.BlockSpec((SEQ, LANE), lambda b, p: (b, p))
    return _call(
        body, name=name, grid=(nb_local, HEAD_PAIRS),
        in_specs=[in_spec(Q_OFF), in_spec(K_OFF), in_spec(V_OFF),
                  pl.BlockSpec((nseg, 2, ATTN_BLOCK, 2 * ATTN_BLOCK), lambda b, p: (0, p, 0, 0))],
        out_specs=[out_spec, out_spec],
        out_shape=[SDS((n, B_WIDTH), F32), SDS((n, B_WIDTH), F32)],
        scratch_shapes=[pltpu.VMEM((nseg, SEQ, LANE), F32), pltpu.VMEM((nseg, SEQ, LANE), F32)],
        args=[proj, proj, proj, bias], sem=("parallel", "arbitrary"), ride=ride)


def attn_bwd(proj, b_out, dmix, lse_tot, bias, nb_local, name, ride=None):
    n = proj.shape[0]
    nseg = len(DILATIONS)
    a_blocks = A_WIDTH // LANE
    scale = 1.0 / math.sqrt(HEAD_DIM)

    def body(q_ref, k_ref, v_ref, o_ref, do_ref, lse_ref, b_ref, dq_ref, dk_ref, dv_ref, db_ref,
             dqs_ref, delta_ref, dka_ref, dva_ref):
        @pl.when(pl.program_id(1) == 0)
        def _():
            db_ref[...] = jnp.zeros_like(db_ref)

        dka_ref[...] = jnp.zeros_like(dka_ref)
        dva_ref[...] = jnp.zeros_like(dva_ref)

        def row_dots(i, carry):
            rows = pl.ds(pl.multiple_of(i * ATTN_BLOCK, ATTN_BLOCK), ATTN_BLOCK)
            head0 = _head0_lanes()
            prod = do_ref[rows, :] * o_ref[rows, :]
            d0 = jnp.sum(jnp.where(head0, prod, 0.0), axis=-1, keepdims=True)
            d1 = jnp.sum(jnp.where(head0, 0.0, prod), axis=-1, keepdims=True)
            delta_ref[rows, :] = jnp.where(head0, d0, d1)
            return carry

        lax.fori_loop(0, SEQ // ATTN_BLOCK, row_dots, 0)

        def op(seg, dil, start, first):
            qrows = _rows(start, ATTN_BLOCK, dil)
            krows = _kv_rows(start, dil, first)
            q2, kb, vb = q_ref[qrows, :], k_ref[krows, :].astype(BF16), v_ref[krows, :].astype(BF16)
            do2, lse2, delta2 = do_ref[qrows, :], lse_ref[qrows, :], delta_ref[qrows, :]
            head0 = _head0_lanes()
            dqs, dk, dv = [], None, None
            for hh in range(2):
                col = slice(hh * HEAD_DIM, hh * HEAD_DIM + 1)
                q, dob = _one_head(q2, head0, hh), _one_head(do2, head0, hh)
                bb = b_ref[seg, hh, :, ATTN_BLOCK:] if first else b_ref[seg, hh]
                p = jnp.exp(_attn_scores(q, kb, bb) - lse2[:, col])
                dvh = _dot_tn(p.astype(BF16), dob)
                ds = p * (_dot_nt(dob, vb) - delta2[:, col])
                if first:
                    db_ref[seg, hh, :, ATTN_BLOCK:] += ds
                else:
                    db_ref[seg, hh] += ds
                dsb = ds.astype(BF16)
                dqs.append(_dot(dsb, kb))
                dkh = _dot_tn(dsb, q)
                dk = dkh if dk is None else dk + dkh
                dv = dvh if dv is None else dv + dvh
            dqs_ref[seg, qrows, :] = jnp.where(head0, dqs[0], dqs[1]) * scale
            dka_ref[krows, :] += dk * scale
            dva_ref[krows, :] += dv

        _attn_schedule(op)

        def merge(i, carry):
            rows = pl.ds(pl.multiple_of(i * MERGE_ROWS, MERGE_ROWS), MERGE_ROWS)
            dq_ref[rows, :] = (dqs_ref[0, rows, :] + dqs_ref[1, rows, :] + dqs_ref[2, rows, :]).astype(BF16)
            dk_ref[rows, :] = dka_ref[rows, :].astype(BF16)
            dv_ref[rows, :] = dva_ref[rows, :].astype(BF16)
            return carry

        lax.fori_loop(0, SEQ // MERGE_ROWS, merge, 0)

    def pspec(off):
        return pl.BlockSpec((SEQ, LANE), lambda p, b: (b, off // LANE + p))

    ospec = pl.BlockSpec((SEQ, LANE), lambda p, b: (b, p))
    bspec = pl.BlockSpec((nseg, 2, ATTN_BLOCK, 2 * ATTN_BLOCK), lambda p, b: (0, p, 0, 0))
    gshape = SDS((n, B_WIDTH), BF16)
    return _call(
        body, name=name, grid=(HEAD_PAIRS, nb_local),
        in_specs=[pspec(Q_OFF), pspec(K_OFF), pspec(V_OFF), ospec,
                  pl.BlockSpec((SEQ, LANE), lambda p, b: (b, a_blocks + p)), ospec, bspec],
        out_specs=[ospec, ospec, ospec, bspec],
        out_shape=[gshape, gshape, gshape, SDS((nseg, B_HEADS, ATTN_BLOCK, 2 * ATTN_BLOCK), F32)],
        scratch_shapes=[pltpu.VMEM((nseg, SEQ, LANE), F32)] + [pltpu.VMEM((SEQ, LANE), F32)] * 3,
        args=[proj, proj, proj, b_out, dmix, lse_tot, bias], sem=("arbitrary", "arbitrary"), ride=ride)


PAD = 8
CONV_ROWS = 64


def _conv_taps(gp_ref, head_ref, r0):
    g0 = gp_ref[r0:r0 + CONV_ROWS, :]
    if r0 == 0:
        return g0, head_ref[PAD - 1:PAD - 1 + CONV_ROWS, :], head_ref[PAD - 2:PAD - 2 + CONV_ROWS, :]
    return g0, gp_ref[r0 - 1:r0 - 1 + CONV_ROWS, :], gp_ref[r0 - 2:r0 - 2 + CONV_ROWS, :]


def _fill_head(gp_ref, head_ref):
    head_ref[0:PAD, :] = jnp.zeros((PAD, LANE), F32)
    head_ref[PAD:PAD + CONV_ROWS, :] = gp_ref[0:CONV_ROWS, :]


def conv_gelu_fwd(gp, up, cw, cb, nb_local, name):
    n, f = gp.shape

    def body(gp_ref, up_ref, cw_ref, cb_ref, o_ref, head_ref):
        _fill_head(gp_ref, head_ref)
        w0, w1, w2, bias = cw_ref[0:1, :], cw_ref[1:2, :], cw_ref[2:3, :], cb_ref[...]
        for r0 in range(0, SEQ, CONV_ROWS):
            g0, g1, g2 = _conv_taps(gp_ref, head_ref, r0)
            c = bias + w0 * g2 + w1 * g1 + w2 * g0
            o_ref[r0:r0 + CONV_ROWS, :] = (_gelu(c) * up_ref[r0:r0 + CONV_ROWS, :]).astype(BF16)

    blk = pl.BlockSpec((SEQ, LANE), lambda b, j: (b, j))
    return pl.pallas_call(
        body, name=name, grid=(nb_local, f // LANE),
        in_specs=[blk, blk, pl.BlockSpec((3, LANE), lambda b, j: (0, j)), pl.BlockSpec((1, LANE), lambda b, j: (0, j))],
        out_specs=blk,
        out_shape=SDS((n, f), BF16),
        scratch_shapes=[pltpu.VMEM((PAD + CONV_ROWS, LANE), F32)],
        compiler_params=_params(("parallel", "parallel")),
    )(gp, up, cw, cb)


def conv_gelu_bwd(dgu, gp, up, cw, cb, nb_local, name, ride=None):
    n, f = gp.shape

    def fold(v):
        return jnp.sum(v.reshape(CONV_ROWS // 8, 8, LANE), axis=0)

    def body(dgu_ref, gp_ref, up_ref, cw_ref, cb_ref, dgp_ref, dup_ref, dcw_ref, dcb_ref, head_ref, dc_ref):
        b = pl.program_id(1)
        _fill_head(gp_ref, head_ref)
        dc_ref[SEQ:SEQ + PAD, :] = jnp.zeros((PAD, LANE), F32)
        w0, w1, w2, bias = cw_ref[0:1, :], cw_ref[1:2, :], cw_ref[2:3, :], cb_ref[...]
        sums = [jnp.zeros((8, LANE), F32) for _ in range(4)]
        for r0 in range(0, SEQ, CONV_ROWS):
            rows = slice(r0, r0 + CONV_ROWS)
            g0, g1, g2 = _conv_taps(gp_ref, head_ref, r0)
            gg, dgg = _gelu_and_grad(bias + w0 * g2 + w1 * g1 + w2 * g0)
            dgu = dgu_ref[rows, :].astype(F32)
            dup_ref[rows, :] = (dgu * gg).astype(BF16)
            dc = dgu * up_ref[rows, :] * dgg
            dc_ref[rows, :] = dc
            sums = [sums[0] + fold(dc * g2), sums[1] + fold(dc * g1), sums[2] + fold(dc * g0), sums[3] + fold(dc)]
        for r0 in range(0, SEQ, CONV_ROWS):
            dgp_ref[r0:r0 + CONV_ROWS, :] = (
                w2 * dc_ref[r0:r0 + CONV_ROWS, :] + w1 * dc_ref[r0 + 1:r0 + 1 + CONV_ROWS, :]
                + w0 * dc_ref[r0 + 2:r0 + 2 + CONV_ROWS, :]).astype(BF16)
        dcw = jnp.concatenate([jnp.sum(s, axis=0, keepdims=True) for s in sums[:3]], axis=0)
        dcb = jnp.sum(sums[3], axis=0, keepdims=True)

        @pl.when(b == 0)
        def _():
            dcw_ref[...] = dcw
            dcb_ref[...] = dcb

        @pl.when(b > 0)
        def _():
            dcw_ref[...] += dcw
            dcb_ref[...] += dcb

    blk = pl.BlockSpec((SEQ, LANE), lambda j, b: (b, j))
    return _call(
        body, name=name, grid=(f // LANE, nb_local),
        in_specs=[blk, blk, blk, pl.BlockSpec((3, LANE), lambda j, b: (0, j)), pl.BlockSpec((1, LANE), lambda j, b: (0, j))],
        out_specs=[blk, blk, pl.BlockSpec((3, LANE), lambda j, b: (0, j)), pl.BlockSpec((1, LANE), lambda j, b: (0, j))],
        out_shape=[SDS((n, f), BF16), SDS((n, f), BF16), SDS((3, f), F32), SDS((1, f), F32)],
        scratch_shapes=[pltpu.VMEM((PAD + CONV_ROWS, LANE), F32), pltpu.VMEM((SEQ + PAD, LANE), F32)],
        args=[dgu, gp, up, cw, cb], sem=("parallel", "arbitrary"), ride=ride)


def norm_mid_epilogue(x1, dout, z2, g3, g2):
    n, d = x1.shape

    def fn(dh2, step, x1_ref, dout_ref, z2_ref, g3_ref, g2_ref, dx1_ref, dz2_ref, dg3_ref, dg2_ref):
        dxa, dg3r = _rms_bwd(dh2, x1_ref[...], g3_ref[...])
        dx1 = dout_ref[...] + dxa
        dx1_ref[...] = dx1
        dz2, dg2r = _rms_bwd(dx1, z2_ref[...], g2_ref[...])
        dz2_ref[...] = dz2.astype(BF16)
        _accumulate(dg3_ref, jnp.sum(dg3r, axis=0, keepdims=True), step)
        _accumulate(dg2_ref, jnp.sum(dg2r, axis=0, keepdims=True), step)

    return fn, [x1, dout, z2, g3, g2], [SDS((n, d), F32), SDS((n, d), BF16), SDS((1, d), F32), SDS((1, d), F32)]


def norm_in_epilogue(x, dx1, g1):
    n, d = x.shape

    def fn(dh1, step, x_ref, dx1_ref, g1_ref, dx_ref, dg1_ref):
        dxa, dgr = _rms_bwd(dh1, x_ref[...], g1_ref[...])
        dx_ref[...] = dx1_ref[...] + dxa
        _accumulate(dg1_ref, jnp.sum(dgr, axis=0, keepdims=True), step)

    return fn, [x, dx1, g1], [SDS((n, d), F32), SDS((1, d), F32)]


def cast_bf16(arrays, name):
    def body(*refs):
        for i_ref, o_ref in zip(refs[:len(arrays)], refs[len(arrays):]):
            o_ref[...] = i_ref[...].astype(BF16)

    return pl.pallas_call(body, name=name, out_shape=[SDS(a.shape, BF16) for a in arrays],
                          compiler_params=_params())(*arrays)


def adam_update(parts, w, m, v, name, tr=None):
    s, r, c = parts.shape
    tr = r if tr is None else tr
    bc1 = 1.0 - ADAM_B1 ** ADAM_STEP
    bc2 = 1.0 - ADAM_B2 ** ADAM_STEP

    def body(p_ref, w_ref, m_ref, v_ref, g_ref, d_ref, nm_ref, nv_ref):
        g = p_ref[0].astype(F32)
        for j in range(1, s):
            g = g + p_ref[j].astype(F32)
        nm = ADAM_B1 * m_ref[...] + (1.0 - ADAM_B1) * g
        nv = ADAM_B2 * v_ref[...] + (1.0 - ADAM_B2) * (g * g)
        g_ref[...] = g
        nm_ref[...] = nm
        nv_ref[...] = nv
        d_ref[...] = -ADAM_LR * ((nm / bc1) / (jnp.sqrt(nv / bc2) + ADAM_EPS) + ADAM_WD * w_ref[...])

    blk = pl.BlockSpec((tr, c), lambda i: (i, 0))
    return pl.pallas_call(
        body, name=name, grid=(r // tr,),
        in_specs=[pl.BlockSpec((s, tr, c), lambda i: (0, i, 0)), blk, blk, blk],
        out_specs=[blk] * 4, out_shape=[SDS((r, c), F32)] * 4,
        compiler_params=_params(("parallel",)),
    )(parts, w, m, v)


EARLY_NAMES = ("spatial_w", "norm_mix_post", "norm_ffn_pre", "norm_ffn_post", "conv_b", "ln_v_gain", "ln_v_bias",
               "spatial_b")
LATE_NAMES = ("norm_mix_pre", "rel_bias")
PACK_ROW_ALIGN = 8


def _pack_rows(size):
    rows = -(-size // LANE)
    return -(-rows // PACK_ROW_ALIGN) * PACK_ROW_ALIGN


def _pack(arrays):
    flat = []
    for a in arrays:
        rows = _pack_rows(a.size)
        flat.append(jnp.pad(a.reshape(-1), (0, rows * LANE - a.size)))
    return jnp.concatenate(flat).reshape(-1, LANE)


def _unpack(packed, shapes):
    out, row = [], 0
    for shp in shapes:
        size = int(np.prod(shp))
        out.append(packed[row:row + _pack_rows(size)].reshape(-1)[:size].reshape(shp))
        row += _pack_rows(size)
    return out


def kernel(x, norm_mix_pre, norm_mix_post, norm_ffn_pre, norm_ffn_post, w_in, ln_v_gain, ln_v_bias, spatial_w, spatial_b, rel_bias, w_out, w_gate, w_up, conv_w, conv_b, w_down, loss_target, m_norm_mix_pre, m_norm_mix_post, m_norm_ffn_pre, m_norm_ffn_post, m_w_in, m_ln_v_gain, m_ln_v_bias, m_spatial_w, m_spatial_b, m_rel_bias, m_w_out, m_w_gate, m_w_up, m_conv_w, m_conv_b, m_w_down, v_norm_mix_pre, v_norm_mix_post, v_norm_ffn_pre, v_norm_ffn_post, v_w_in, v_ln_v_gain, v_ln_v_bias, v_spatial_w, v_spatial_b, v_rel_bias, v_w_out, v_w_gate, v_w_up, v_conv_w, v_conv_b, v_w_down):
    given = dict(locals())
    nb_local, seq, d = x.shape
    n = nb_local * seq
    cols = w_in.shape[2]

    def by_columns(g):
        return g.transpose(1, 0, 2).reshape(g.shape[1], N_DEV * g.shape[2])

    def to_blocks(g):
        return g.reshape(g.shape[0], N_DEV, cols).transpose(1, 0, 2)

    xf, target = x.reshape(n, d), loss_target.reshape(n, d)
    ln_g, ln_b = ln_v_gain.reshape(1, A_WIDTH), ln_v_bias.reshape(1, A_WIDTH)
    spatial_bt, rel_bias_t = spatial_b[0].T, rel_bias.T

    s_in, s_out, s_gate, s_up, s_down = cast_bf16([w_in[0], w_out[0], w_gate[0], w_up[0], w_down[0]], "cast_shards")
    g_in, g_cw = exchange([], [s_in, conv_w[0]], "gather_w_in")
    w_in_f, conv_w_f = by_columns(g_in), by_columns(g_cw)

    (h1, proj), _ = norm_mm(xf, norm_mix_pre, [w_in_f], "fwd_norm_in", tn=IN_COLS)
    a = gating_fwd(proj, ln_g, ln_b, spatial_w[0], spatial_bt, "fwd_gating")
    bias = bias_tables(rel_bias_t, "bias_tables").reshape(len(DILATIONS), B_HEADS, ATTN_BLOCK, 2 * ATTN_BLOCK)
    (b_out, lse_tot), (g_out, g_gate, g_up) = attn_fwd(proj, bias, nb_local, "fwd_attn",
                                                       ride=([], [s_out, s_gate, s_up]))
    w_out_f, w_gate_f, w_up_f = g_out.reshape(D_MODEL, D_MODEL), by_columns(g_gate), by_columns(g_up)
    z2, x1 = mm_res_norm([a, b_out], w_out_f, xf, norm_mix_post, "fwd_out_norm")
    (h2, gp, up), (g_down,) = norm_mm(x1, norm_ffn_pre, [w_gate_f, w_up_f], "fwd_norm_ffn", tm=256, tn=D_FF,
                                      ride=([], [s_down]))
    w_down_f = g_down.reshape(D_FF, D_MODEL)
    gu = conv_gelu_fwd(gp, up, conv_w_f, conv_b, nb_local, "fwd_conv_gelu")
    dy, dout, dg4, loss_part = down_loss(gu, w_down_f, x1, norm_ffn_post, target, "fwd_down_loss")

    p_down = mm_tn([gu], [dy], "bwd_dw_down", t1=256, t2=D_MODEL)
    (dgu,), _ = mm_nt([(dy, 0, 0)], [w_down_f], "bwd_dgu", out_dtype=BF16)
    (dgp, dup, p_conv_w, p_conv_b), (r_down,) = conv_gelu_bwd(
        dgu, gp, up, conv_w_f, conv_b, nb_local, "bwd_conv_gelu", ride=([p_down.reshape(N_DEV, cols, D_MODEL)], []))
    p_gate = mm_tn([h2], [dgp], "bwd_dw_gate", t1=D_MODEL, t2=256)
    p_up = mm_tn([h2], [dup], "bwd_dw_up", t1=D_MODEL, t2=256)
    (dx1, dz2, dg3, dg2), _ = mm_nt([(dgp, 0, 0), (dup, 1, 0)], [w_gate_f, w_up_f], "bwd_dh2_norm_mid", tm=256,
                                    epilogue=norm_mid_epilogue(x1, dout, z2, norm_ffn_pre, norm_mix_post))
    p_out = mm_tn([a, b_out], [dz2], "bwd_dw_out", t1=256, t2=D_MODEL)
    (dmix,), _ = mm_nt([(dz2, 0, 0)], [w_out_f], "bwd_dmix")
    duv, p_ws, p_sbt, p_lng, p_lnb = gating_bwd(proj, dmix, ln_g, ln_b, spatial_w[0], spatial_bt, "bwd_gating")
    small = dict(spatial_w=p_ws, norm_mix_post=dg2, norm_ffn_pre=dg3, norm_ffn_post=dg4, conv_b=p_conv_b,
                 ln_v_gain=p_lng, ln_v_bias=p_lnb, spatial_b=p_sbt.T)
    pack_early = _pack([small[k] for k in EARLY_NAMES] + [p_conv_w, loss_part])
    (dq, dk, dv, dbias), (r_gate, r_up, r_out, r_early) = attn_bwd(
        proj, b_out, dmix, lse_tot, bias, nb_local, "bwd_attn",
        ride=([to_blocks(p_gate), to_blocks(p_up), p_out.reshape(N_DEV, D_MODEL // N_DEV, D_MODEL)], [pack_early]))
    p_rel_bias_t = rel_bias_grad(dbias.reshape(len(DILATIONS), B_HEADS, BIAS_SIZE), "bwd_rel_bias")
    p_in = mm_tn([h1], [duv, dq, dk, dv], "bwd_dw_in", t1=D_MODEL, t2=256)
    (grad_x, dg1), (r_in,) = mm_nt(
        [(duv, 0, 0), (dq, 0, Q_OFF), (dk, 0, K_OFF), (dv, 0, V_OFF)], [w_in_f], "bwd_dh1_norm_in",
        epilogue=norm_in_epilogue(xf, dx1, norm_mix_pre), ride=([to_blocks(p_in)], []))
    small.update(norm_mix_pre=dg1, rel_bias=p_rel_bias_t.T)
    (r_late,) = exchange([], [_pack([small[k] for k in LATE_NAMES])], "exchange_late")

    res = {}
    res["w_in"] = adam_update(r_in, w_in[0], m_w_in[0], v_w_in[0], "adam_w_in", tr=256)
    res["w_out"] = adam_update(r_out, w_out[0], m_w_out[0], v_w_out[0], "adam_w_out")
    res["w_gate"] = adam_update(r_gate, w_gate[0], m_w_gate[0], v_w_gate[0], "adam_w_gate", tr=256)
    res["w_up"] = adam_update(r_up, w_up[0], m_w_up[0], v_w_up[0], "adam_w_up", tr=256)
    res["w_down"] = adam_update(r_down, w_down[0], m_w_down[0], v_w_down[0], "adam_w_down", tr=176)

    def adam_packed(received, names, tail, name):
        zeros = [jnp.zeros_like(t) for t in tail]
        packs = [_pack([given[pre + k] for k in names] + zeros) for pre in ("", "m_", "v_")]
        shapes = [given[k].shape for k in names] + [t.shape for t in tail]
        unpacked = [_unpack(p, shapes) for p in adam_update(received, *packs, name)]
        for i, k in enumerate(names):
            res[k] = [u[i] for u in unpacked]
        return unpacked[0][len(names):]

    g_conv_w_full, loss_sum = adam_packed(r_early, EARLY_NAMES, [p_conv_w, loss_part], "adam_small_early")
    adam_packed(r_late, LATE_NAMES, [], "adam_small_late")
    g_conv_w = lax.dynamic_slice_in_dim(g_conv_w_full, _my_index() * cols, cols, axis=1)
    res["conv_w"] = adam_update(g_conv_w[None], conv_w[0], m_conv_w[0], v_conv_w[0], "adam_conv_w")
    loss = loss_sum[0, 0]

    names = ("norm_mix_pre", "norm_mix_post", "norm_ffn_pre", "norm_ffn_post", "w_in", "ln_v_gain", "ln_v_bias",
             "spatial_w", "spatial_b", "rel_bias", "w_out", "w_gate", "w_up", "conv_w", "conv_b", "w_down")
    outs = [loss, grad_x.reshape(x.shape)]
    for t in range(4):
        outs += [res[k][t].reshape(given[k].shape) for k in names]
    return tuple(outs)
```

```python
import functools
import math

import numpy as np
import jax
import jax.numpy as jnp
from jax import lax
from jax.experimental import pallas as pl
from jax.experimental.pallas import tpu as pltpu

F32 = jnp.float32
BF16 = jnp.bfloat16
SDS = jax.ShapeDtypeStruct

D_MODEL = 1024
SEQ = 2048
HEAD_DIM = 64
A_GROUPS = 4
A_WIDTH = A_GROUPS * HEAD_DIM
B_HEADS = 12
B_WIDTH = B_HEADS * HEAD_DIM
HEAD_PAIRS = B_HEADS // 2
CHUNK = 128
ATTN_BLOCK = 128
DILATIONS = (1, 4, 16)
NUM_BUCKETS = 32
MAX_DISTANCE = 2048
D_FF = 2816
IN_COLS = 2 * A_WIDTH + 3 * B_WIDTH
Q_OFF = 2 * A_WIDTH
K_OFF = Q_OFF + B_WIDTH
V_OFF = K_OFF + B_WIDTH
NORM_EPS = 1e-6
NEG_INF = -1e30
N_DEV = 8
LANE = 128

ADAM_LR = 0.001
ADAM_B1 = 0.9
ADAM_B2 = 0.999
ADAM_EPS = 1e-08
ADAM_WD = 0.01
ADAM_STEP = 10

GELU_C0 = math.sqrt(2.0 / math.pi)
GELU_C1 = 0.044715

VMEM_LIMIT = 56 * 1024 * 1024


def _params(sem=None):
    if sem is None:
        return pltpu.CompilerParams(vmem_limit_bytes=VMEM_LIMIT)
    return pltpu.CompilerParams(dimension_semantics=sem, vmem_limit_bytes=VMEM_LIMIT)


def _gelu(x):
    t = jnp.tanh(GELU_C0 * (x + GELU_C1 * x * x * x))
    return 0.5 * x * (1.0 + t)


def _gelu_and_grad(x):
    x2 = x * x
    t = jnp.tanh(GELU_C0 * (x + GELU_C1 * x * x2))
    g = 0.5 * x * (1.0 + t)
    dg = 0.5 * (1.0 + t) + 0.5 * x * (1.0 - t * t) * (GELU_C0 * (1.0 + 3.0 * GELU_C1 * x2))
    return g, dg


def _dot(a, b):
    return jnp.dot(a, b, preferred_element_type=F32)


def _dot_nt(a, b):
    return lax.dot_general(a, b, (((1,), (1,)), ((), ())), preferred_element_type=F32)


def _dot_tn(a, b):
    return lax.dot_general(a, b, (((0,), (0,)), ((), ())), preferred_element_type=F32)


def _rms_bwd(d, xin, g):
    r = lax.rsqrt(jnp.mean(xin * xin, axis=-1, keepdims=True) + NORM_EPS)
    xh = xin * r
    gd = g * d
    dx = r * (gd - xh * jnp.mean(gd * xh, axis=-1, keepdims=True))
    return dx, d * xh


MESH = pl.DeviceIdType.MESH
ANY = pl.BlockSpec(memory_space=pl.ANY)
PEER_MASKS = tuple(range(1, N_DEV))


def _my_index():
    return lax.axis_index("x") * 4 + lax.axis_index("y") * 2 + lax.axis_index("c")


def _peer(mask):
    x, y, c = lax.axis_index("x"), lax.axis_index("y"), lax.axis_index("c")
    px = 1 - x if mask & 4 else x
    py = 1 - y if mask & 2 else y
    pc = 1 - c if mask & 1 else c
    return (px, py, pc), px * 4 + py * 2 + pc


def _exchange_copies(nblocked, in_refs, out_refs, sems):
    send_sems, recv_sems, local_sems = sems
    me = _my_index()
    local, sends, recvs = [], [], []
    for a, (in_ref, out_ref) in enumerate(zip(in_refs, out_refs)):
        src_of = (lambda idx, r=in_ref: r.at[idx]) if a < nblocked else (lambda idx, r=in_ref: r)
        local.append(pltpu.make_async_copy(src_of(me), out_ref.at[me], local_sems.at[a]))
        for mask in PEER_MASKS:
            peer, pidx = _peer(mask)
            pair = dict(send_sem=send_sems.at[a, mask - 1], recv_sem=recv_sems.at[a, mask - 1],
                        device_id=peer, device_id_type=MESH)
            sends.append(pltpu.make_async_remote_copy(src_ref=src_of(pidx), dst_ref=out_ref.at[me], **pair))
            recvs.append(pltpu.make_async_remote_copy(src_ref=src_of(pidx), dst_ref=out_ref.at[pidx], **pair))
    return local, sends, recvs


def _exchange_start(nblocked, in_refs, out_refs, sems):
    local, sends, _ = _exchange_copies(nblocked, in_refs, out_refs, sems)
    for cp in local + sends:
        cp.start()


def _exchange_wait(nblocked, in_refs, out_refs, sems):
    local, sends, recvs = _exchange_copies(nblocked, in_refs, out_refs, sems)
    for cp in sends:
        cp.wait_send()
    for cp in recvs:
        cp.wait_recv()
    for cp in local:
        cp.wait()


def _exchange_out_shape(blocked, whole):
    return [SDS(b.shape, b.dtype) for b in blocked] + [SDS((N_DEV,) + w.shape, w.dtype) for w in whole]


def _exchange_sems(n):
    return [pltpu.SemaphoreType.DMA((n, N_DEV - 1)), pltpu.SemaphoreType.DMA((n, N_DEV - 1)),
            pltpu.SemaphoreType.DMA((n,))]


def exchange(blocked, whole, name):
    nb, n = len(blocked), len(blocked) + len(whole)

    def body(*refs):
        _exchange_start(nb, refs[:n], refs[n:2 * n], refs[2 * n:])
        _exchange_wait(nb, refs[:n], refs[n:2 * n], refs[2 * n:])

    return pl.pallas_call(
        body, name=name, in_specs=[ANY] * n, out_specs=[ANY] * n, out_shape=_exchange_out_shape(blocked, whole),
        scratch_shapes=_exchange_sems(n),
    )(*blocked, *whole)


def _call(body, *, name, grid, in_specs, out_specs, out_shape, args, scratch_shapes=(), sem=None, ride=None):
    out_shape, out_specs, scratch_shapes = list(out_shape), list(out_specs), list(scratch_shapes)
    if ride is None:
        outs = pl.pallas_call(body, name=name, grid=grid, in_specs=list(in_specs), out_specs=out_specs,
                              out_shape=out_shape, scratch_shapes=scratch_shapes,
                              compiler_params=_params(sem))(*args)
        return list(outs), []
    blocked, whole = ride
    cargs = list(blocked) + list(whole)
    nb, nc = len(blocked), len(cargs)
    n_in, n_out, n_scr = len(args), len(out_shape), len(scratch_shapes)

    def riding(*refs):
        ins, refs = refs[:n_in], refs[n_in:]
        cins, refs = refs[:nc], refs[nc:]
        outs, refs = refs[:n_out], refs[n_out:]
        couts, refs = refs[:nc], refs[nc:]
        scr, sems = refs[:n_scr], refs[n_scr:]
        ids = [pl.program_id(k) for k in range(len(grid))]
        first = functools.reduce(lambda p, q: p & q, [i == 0 for i in ids])
        last = functools.reduce(lambda p, q: p & q, [i == g - 1 for i, g in zip(ids, grid)])

        @pl.when(first)
        def _():
            _exchange_start(nb, cins, couts, sems)

        body(*ins, *outs, *scr)

        @pl.when(last)
        def _():
            _exchange_wait(nb, cins, couts, sems)

    res = pl.pallas_call(
        riding, name=name, grid=grid, in_specs=list(in_specs) + [ANY] * nc, out_specs=out_specs + [ANY] * nc,
        out_shape=out_shape + _exchange_out_shape(blocked, whole),
        scratch_shapes=scratch_shapes + _exchange_sems(nc),
        compiler_params=_params(("arbitrary",) * len(grid)))(*args, *cargs)
    return list(res[:n_out]), list(res[n_out:])


def norm_mm(x, g, ws, name, tm=512, tn=1408, ride=None):
    n, d = x.shape
    f = ws[0].shape[1]
    nw = len(ws)

    def body(x_ref, g_ref, *refs):
        w_refs = refs[:nw]
        h_ref = refs[nw]
        o_refs = refs[nw + 1:]

        @pl.when(pl.program_id(1) == 0)
        def _():
            xv = x_ref[...]
            r = lax.rsqrt(jnp.mean(xv * xv, axis=-1, keepdims=True) + NORM_EPS)
            h_ref[...] = (xv * r * g_ref[...]).astype(BF16)

        h = h_ref[...]
        for w_ref, o_ref in zip(w_refs, o_refs):
            o_ref[...] = _dot(h, w_ref[...])

    return _call(
        body, name=name, grid=(n // tm, f // tn),
        in_specs=[pl.BlockSpec((tm, d), lambda i, j: (i, 0)), pl.BlockSpec((1, d), lambda i, j: (0, 0))]
        + [pl.BlockSpec((d, tn), lambda i, j: (0, j)) for _ in ws],
        out_specs=[pl.BlockSpec((tm, d), lambda i, j: (i, 0))]
        + [pl.BlockSpec((tm, tn), lambda i, j: (i, j)) for _ in ws],
        out_shape=[SDS((n, d), BF16)] + [SDS((n, f), F32) for _ in ws],
        args=[x, g, *ws], sem=("parallel", "arbitrary"), ride=ride)


def _lane_concat(refs):
    vals = [r[...].astype(BF16) for r in refs]
    return vals[0] if len(vals) == 1 else jnp.concatenate(vals, axis=1)


def mm_res_norm(a_list, w, res, g, name, tm=512):
    n = a_list[0].shape[0]
    k, d = w.shape
    na = len(a_list)

    def body(*refs):
        w_ref, res_ref, g_ref, y_ref, o_ref = refs[na:]
        y = _dot(_lane_concat(refs[:na]), w_ref[...])
        r = lax.rsqrt(jnp.mean(y * y, axis=-1, keepdims=True) + NORM_EPS)
        y_ref[...] = y
        o_ref[...] = res_ref[...] + y * r * g_ref[...]

    return pl.pallas_call(
        body, name=name, grid=(n // tm,),
        in_specs=[pl.BlockSpec((tm, a.shape[1]), lambda i: (i, 0)) for a in a_list]
        + [pl.BlockSpec((k, d), lambda i: (0, 0)),
           pl.BlockSpec((tm, d), lambda i: (i, 0)), pl.BlockSpec((1, d), lambda i: (0, 0))],
        out_specs=[pl.BlockSpec((tm, d), lambda i: (i, 0)), pl.BlockSpec((tm, d), lambda i: (i, 0))],
        out_shape=[SDS((n, d), F32), SDS((n, d), F32)],
        compiler_params=_params(("parallel",)),
    )(*a_list, w, res, g)


def down_loss(a, w, res, g, target, name, tm=256):
    n, k = a.shape
    d = w.shape[1]
    inv_d = 1.0 / d

    def body(a_ref, w_ref, res_ref, g_ref, t_ref, dy_ref, dout_ref, dg_ref, loss_ref):
        i = pl.program_id(0)
        y = _dot(a_ref[...], w_ref[...])
        gv = g_ref[...]
        r = lax.rsqrt(jnp.mean(y * y, axis=-1, keepdims=True) + NORM_EPS)
        yh = y * r
        e = res_ref[...] + yh * gv - t_ref[...]
        part = 0.5 * inv_d * jnp.sum(jnp.sum(e * e, axis=-1, keepdims=True), axis=0, keepdims=True)
        dout = e * inv_d
        dout_ref[...] = dout
        gd = gv * dout
        dy_ref[...] = (r * (gd - yh * jnp.mean(gd * yh, axis=-1, keepdims=True))).astype(BF16)
        dgp = jnp.sum(dout * yh, axis=0, keepdims=True)
        lane0 = lax.broadcasted_iota(jnp.int32, (1, LANE), 1) == 0
        lp = jnp.where(lane0, part, 0.0)

        @pl.when(i == 0)
        def _():
            dg_ref[...] = dgp
            loss_ref[...] = lp

        @pl.when(i > 0)
        def _():
            dg_ref[...] += dgp
            loss_ref[...] += lp

    return pl.pallas_call(
        body, name=name, grid=(n // tm,),
        in_specs=[pl.BlockSpec((tm, k), lambda i: (i, 0)), pl.BlockSpec((k, d), lambda i: (0, 0)),
                  pl.BlockSpec((tm, d), lambda i: (i, 0)), pl.BlockSpec((1, d), lambda i: (0, 0)),
                  pl.BlockSpec((tm, d), lambda i: (i, 0))],
        out_specs=[pl.BlockSpec((tm, d), lambda i: (i, 0)), pl.BlockSpec((tm, d), lambda i: (i, 0)),
                   pl.BlockSpec((1, d), lambda i: (0, 0)), pl.BlockSpec((1, LANE), lambda i: (0, 0))],
        out_shape=[SDS((n, d), BF16), SDS((n, d), F32), SDS((1, d), F32), SDS((1, LANE), F32)],
        compiler_params=_params(("arbitrary",)),
    )(a, w, res, g, target)


def _accumulate(ref, val, step):
    @pl.when(step == 0)
    def _():
        ref[...] = val

    @pl.when(step > 0)
    def _():
        ref[...] += val


def mm_nt(terms, ws, name, tm=512, out_dtype=F32, ride=None, epilogue=None):
    n = terms[0][0].shape[0]
    r = ws[0].shape[0]
    na = len(terms)
    meta = [(widx, off, a.shape[1]) for a, widx, off in terms]
    fn, extras, out_shape = epilogue if epilogue else (None, [], [SDS((n, r), out_dtype)])
    n_fixed = na + len(ws)

    def body(*refs):
        a_refs = refs[:na]
        w_refs = refs[na:n_fixed]
        acc = None
        for a_ref, (widx, off, k) in zip(a_refs, meta):
            p = _dot_nt(a_ref[...].astype(BF16), w_refs[widx][:, off:off + k])
            acc = p if acc is None else acc + p
        if fn is None:
            refs[-1][...] = acc.astype(out_dtype)
        else:
            fn(acc, pl.program_id(0), *refs[n_fixed:])

    def spec(a):
        if a.shape[0] == 1:
            return pl.BlockSpec(a.shape, lambda i: (0, 0))
        return pl.BlockSpec((tm, a.shape[1]), lambda i: (i, 0))

    return _call(
        body, name=name, grid=(n // tm,),
        in_specs=[spec(a) for a, _, _ in terms] + [pl.BlockSpec(w.shape, lambda i: (0, 0)) for w in ws]
        + [spec(e) for e in extras],
        out_specs=[spec(o) for o in out_shape], out_shape=out_shape,
        args=[a for a, _, _ in terms] + list(ws) + list(extras),
        sem=("parallel",) if fn is None else ("arbitrary",), ride=ride)


def _piece_blocks(pieces, tile):
    out, first = [], 0
    for p in pieces:
        nblk, rem = divmod(p.shape[1], tile)
        assert rem == 0, (p.shape, tile)
        out.append((first, nblk))
        first += nblk
    return out, first


def mm_tn(lhs_list, rhs_list, name, t1, t2, out_dtype=BF16):
    n = lhs_list[0].shape[0]
    lblocks, nbl = _piece_blocks(lhs_list, t1)
    rblocks, nbr = _piece_blocks(rhs_list, t2)
    nl = len(lhs_list)

    def body(*refs):
        l_refs, r_refs, o_ref = refs[:nl], refs[nl:-1], refs[-1]
        i, j = pl.program_id(0), pl.program_id(1)
        for l_ref, (ls, ln) in zip(l_refs, lblocks):
            for r_ref, (rs, rn) in zip(r_refs, rblocks):
                @pl.when((i >= ls) & (i < ls + ln) & (j >= rs) & (j < rs + rn))
                def _(l_ref=l_ref, r_ref=r_ref):
                    o_ref[...] = _dot_tn(l_ref[...].astype(BF16), r_ref[...].astype(BF16)).astype(out_dtype)

    def piece_spec(tile, axis, first, nblk):
        def index(i, j):
            return 0, jnp.clip((i, j)[axis] - first, 0, nblk - 1)
        return pl.BlockSpec((n, tile), index)

    return pl.pallas_call(
        body, name=name, grid=(nbl, nbr),
        in_specs=[piece_spec(t1, 0, *b) for b in lblocks] + [piece_spec(t2, 1, *b) for b in rblocks],
        out_specs=pl.BlockSpec((t1, t2), lambda i, j: (i, j)),
        out_shape=SDS((nbl * t1, nbr * t2), out_dtype),
        compiler_params=_params(("parallel", "arbitrary")),
    )(*lhs_list, *rhs_list)


GATE_ROWS = 512


def _tril_mask():
    row = lax.broadcasted_iota(jnp.int32, (CHUNK, CHUNK), 0)
    col = lax.broadcasted_iota(jnp.int32, (CHUNK, CHUNK), 1)
    return row >= col


def _layer_norm_parts(gv):
    mu = jnp.mean(gv, axis=-1, keepdims=True)
    xc = gv - mu
    rstd = lax.rsqrt(jnp.mean(xc * xc, axis=-1, keepdims=True) + NORM_EPS)
    return xc * rstd, rstd


def gating_fwd(proj, lng, lnb, ws, sbt, name):
    n = proj.shape[0]
    nchunk = GATE_ROWS // CHUNK

    def body(u_ref, v_ref, lng_ref, lnb_ref, ws_ref, sbt_ref, a_ref):
        tril = _tril_mask()
        for g in range(A_GROUPS):
            cs = slice(g * HEAD_DIM, (g + 1) * HEAD_DIM)
            wt = jnp.where(tril, ws_ref[g], 0.0).astype(BF16)
            for c in range(nchunk):
                rs_ = slice(c * CHUNK, (c + 1) * CHUNK)
                vhat, _ = _layer_norm_parts(_gelu(v_ref[rs_, cs]))
                vn = vhat * lng_ref[:, cs] + lnb_ref[:, cs]
                z = _dot(wt, vn.astype(BF16)) + sbt_ref[:, g:g + 1]
                a_ref[rs_, cs] = _gelu(u_ref[rs_, cs]) * z

    return pl.pallas_call(
        body, name=name, grid=(n // GATE_ROWS,),
        in_specs=[pl.BlockSpec((GATE_ROWS, A_WIDTH), lambda i: (i, 0)),
                  pl.BlockSpec((GATE_ROWS, A_WIDTH), lambda i: (i, 1)),
                  pl.BlockSpec((1, A_WIDTH), lambda i: (0, 0)), pl.BlockSpec((1, A_WIDTH), lambda i: (0, 0)),
                  pl.BlockSpec((A_GROUPS, CHUNK, CHUNK), lambda i: (0, 0, 0)),
                  pl.BlockSpec((CHUNK, A_GROUPS), lambda i: (0, 0))],
        out_specs=pl.BlockSpec((GATE_ROWS, A_WIDTH), lambda i: (i, 0)),
        out_shape=SDS((n, A_WIDTH), F32),
        compiler_params=_params(("parallel",)),
    )(proj, proj, lng, lnb, ws, sbt)


def gating_bwd(proj, dmix, lng, lnb, ws, sbt, name):
    n = proj.shape[0]
    nchunk = GATE_ROWS // CHUNK

    def body(u_ref, v_ref, da_ref, lng_ref, lnb_ref, ws_ref, sbt_ref,
             duv_ref, dws_ref, dsbt_ref, dlng_ref, dlnb_ref):
        @pl.when(pl.program_id(0) == 0)
        def _():
            dws_ref[...] = jnp.zeros_like(dws_ref)
            dsbt_ref[...] = jnp.zeros_like(dsbt_ref)
            dlng_ref[...] = jnp.zeros_like(dlng_ref)
            dlnb_ref[...] = jnp.zeros_like(dlnb_ref)

        tril = _tril_mask()
        for g in range(A_GROUPS):
            cs = slice(g * HEAD_DIM, (g + 1) * HEAD_DIM)
            wt = jnp.where(tril, ws_ref[g], 0.0).astype(BF16)
            lg = lng_ref[:, cs]
            dw = jnp.zeros((CHUNK, CHUNK), F32)
            dsb = jnp.zeros((CHUNK, 1), F32)
            dlg = jnp.zeros((1, HEAD_DIM), F32)
            dlb = jnp.zeros((1, HEAD_DIM), F32)
            for c in range(nchunk):
                rs_ = slice(c * CHUNK, (c + 1) * CHUNK)
                gu, dgu_dx = _gelu_and_grad(u_ref[rs_, cs])
                gv, dgv_dx = _gelu_and_grad(v_ref[rs_, cs])
                vhat, rstd = _layer_norm_parts(gv)
                vn = (vhat * lg + lnb_ref[:, cs]).astype(BF16)
                z = _dot(wt, vn) + sbt_ref[:, g:g + 1]
                da = da_ref[rs_, cs]
                dz = da * gu
                dzb = dz.astype(BF16)
                duv_ref[rs_, cs] = (da * z * dgu_dx).astype(BF16)
                dsb = dsb + jnp.sum(dz, axis=-1, keepdims=True)
                dw = dw + _dot_nt(dzb, vn)
                dvn = _dot_tn(wt, dzb)
                dlg = dlg + jnp.sum(dvn * vhat, axis=0, keepdims=True)
                dlb = dlb + jnp.sum(dvn, axis=0, keepdims=True)
                dvh = dvn * lg
                dgv = rstd * (dvh - jnp.mean(dvh, axis=-1, keepdims=True)
                              - vhat * jnp.mean(dvh * vhat, axis=-1, keepdims=True))
                duv_ref[rs_, A_WIDTH + g * HEAD_DIM:A_WIDTH + (g + 1) * HEAD_DIM] = (dgv * dgv_dx).astype(BF16)
            dws_ref[g] += jnp.where(tril, dw, 0.0)
            dsbt_ref[:, g:g + 1] += dsb
            dlng_ref[:, cs] += dlg
            dlnb_ref[:, cs] += dlb

    return pl.pallas_call(
        body, name=name, grid=(n // GATE_ROWS,),
        in_specs=[pl.BlockSpec((GATE_ROWS, A_WIDTH), lambda i: (i, 0)),
                  pl.BlockSpec((GATE_ROWS, A_WIDTH), lambda i: (i, 1)),
                  pl.BlockSpec((GATE_ROWS, A_WIDTH), lambda i: (i, 0)),
                  pl.BlockSpec((1, A_WIDTH), lambda i: (0, 0)), pl.BlockSpec((1, A_WIDTH), lambda i: (0, 0)),
                  pl.BlockSpec((A_GROUPS, CHUNK, CHUNK), lambda i: (0, 0, 0)),
                  pl.BlockSpec((CHUNK, A_GROUPS), lambda i: (0, 0))],
        out_specs=[pl.BlockSpec((GATE_ROWS, 2 * A_WIDTH), lambda i: (i, 0)),
                   pl.BlockSpec((A_GROUPS, CHUNK, CHUNK), lambda i: (0, 0, 0)),
                   pl.BlockSpec((CHUNK, A_GROUPS), lambda i: (0, 0)),
                   pl.BlockSpec((1, A_WIDTH), lambda i: (0, 0)), pl.BlockSpec((1, A_WIDTH), lambda i: (0, 0))],
        out_shape=[SDS((n, 2 * A_WIDTH), BF16), SDS((A_GROUPS, CHUNK, CHUNK), F32), SDS((CHUNK, A_GROUPS), F32),
                   SDS((1, A_WIDTH), F32), SDS((1, A_WIDTH), F32)],
        compiler_params=_params(("arbitrary",)),
    )(proj, proj, dmix, lng, lnb, ws, sbt)


def _t5_bucket_np(dist):
    max_exact = NUM_BUCKETS // 2
    dd = np.maximum(dist, 1).astype(np.float64)
    large = max_exact + np.log(dd / max_exact) / math.log(MAX_DISTANCE / max_exact) * (NUM_BUCKETS - max_exact)
    large = np.minimum(large.astype(np.int64), NUM_BUCKETS - 1)
    return np.where(dist < max_exact, dist, large)


def _bucket_tables(with_first):
    i = np.arange(ATTN_BLOCK)[:, None]
    j = np.arange(2 * ATTN_BLOCK)[None, :]
    rel = ATTN_BLOCK + i - j
    band = (rel >= 0) & (rel <= ATTN_BLOCK)
    tabs = []
    for own_only in (False, True) if with_first else (False,):
        for dil in DILATIONS:
            b = _t5_bucket_np(np.maximum(rel, 0) * dil)
            tabs.append(np.where(band & (j >= ATTN_BLOCK) if own_only else band, b, -1).reshape(1, -1))
    return np.stack(tabs).astype(np.float32)


BIAS_SIZE = ATTN_BLOCK * 2 * ATTN_BLOCK


def bias_tables(rel_bias_t, name):
    idx = jnp.asarray(_bucket_tables(True))
    ntab = idx.shape[0]

    def body(rb_ref, idx_ref, o_ref):
        iv = idx_ref[0]
        bk = lax.broadcasted_iota(jnp.int32, (NUM_BUCKETS, BIAS_SIZE), 0).astype(F32)
        onehot = (bk == iv).astype(F32)
        t = jnp.dot(rb_ref[...], onehot, preferred_element_type=F32, precision=lax.Precision.HIGHEST)
        o_ref[0] = jnp.where(iv < 0.0, NEG_INF, t)

    return pl.pallas_call(
        body, name=name, grid=(ntab,),
        in_specs=[pl.BlockSpec((B_HEADS, NUM_BUCKETS), lambda d: (0, 0)),
                  pl.BlockSpec((1, 1, BIAS_SIZE), lambda d: (d, 0, 0))],
        out_specs=pl.BlockSpec((1, B_HEADS, BIAS_SIZE), lambda d: (d, 0, 0)),
        out_shape=SDS((ntab, B_HEADS, BIAS_SIZE), F32),
        compiler_params=_params(("parallel",)),
    )(rel_bias_t, idx)


def rel_bias_grad(dbias, name):
    idx = jnp.asarray(_bucket_tables(False))

    def body(db_ref, idx_ref, o_ref):
        d = pl.program_id(0)
        iv = idx_ref[0]
        bk = lax.broadcasted_iota(jnp.int32, (NUM_BUCKETS, BIAS_SIZE), 0).astype(F32)
        onehot = (bk == iv).astype(F32)
        part = lax.dot_general(db_ref[0], onehot, (((1,), (1,)), ((), ())),
                               preferred_element_type=F32, precision=lax.Precision.HIGHEST)

        @pl.when(d == 0)
        def _():
            o_ref[...] = part

        @pl.when(d > 0)
        def _():
            o_ref[...] += part

    return pl.pallas_call(
        body, name=name, grid=(len(DILATIONS),),
        in_specs=[pl.BlockSpec((1, B_HEADS, BIAS_SIZE), lambda d: (d, 0, 0)),
                  pl.BlockSpec((1, 1, BIAS_SIZE), lambda d: (d, 0, 0))],
        out_specs=pl.BlockSpec((B_HEADS, NUM_BUCKETS), lambda d: (0, 0)),
        out_shape=SDS((B_HEADS, NUM_BUCKETS), F32),
        compiler_params=_params(("arbitrary",)),
    )(dbias, idx)


def _attn_scores(q, kk, bias):
    return _dot_nt(q, kk) * (1.0 / math.sqrt(HEAD_DIM)) + bias


def _head0_lanes():
    return lax.broadcasted_iota(jnp.int32, (ATTN_BLOCK, LANE), 1) < HEAD_DIM


def _one_head(x2, head0, hh):
    return jnp.where(head0 if hh == 0 else jnp.logical_not(head0), x2, 0.0).astype(BF16)


def _rows(start, size, dil):
    return pl.ds(start, size) if dil == 1 else pl.ds(start, size, stride=dil)


QUAD = 4
QUAD_ROWS = SEQ // QUAD


def _deinterleave(src_ref, dst_ref):
    for r in range(QUAD):
        for c in range(QUAD_ROWS // ATTN_BLOCK):
            dst_ref[r, c * ATTN_BLOCK:(c + 1) * ATTN_BLOCK, :] = src_ref[
                pl.ds(r + c * QUAD * ATTN_BLOCK, ATTN_BLOCK, stride=QUAD), :]


def _deinterleave_again(src_ref, dst_ref):
    for r in range(QUAD):
        for s in range(QUAD):
            dst_ref[r + QUAD * s] = src_ref[r, pl.ds(s, ATTN_BLOCK, stride=QUAD), :]


def _interleave_back(src_ref, dst_ref, slot0, accumulate=False):
    for r in range(QUAD):
        for s in range(QUAD):
            rows = pl.ds(s, ATTN_BLOCK, stride=QUAD)
            if accumulate:
                dst_ref[slot0 + r, rows, :] += src_ref[r + QUAD * s]
            else:
                dst_ref[slot0 + r, rows, :] = src_ref[r + QUAD * s]


def _quad_tiles():
    return [(r, pl.ds(r + c * QUAD * ATTN_BLOCK, ATTN_BLOCK, stride=QUAD), slice(c * ATTN_BLOCK, (c + 1) * ATTN_BLOCK))
            for r in range(QUAD) for c in range(QUAD_ROWS // ATTN_BLOCK)]


def _attn_schedule(op):
    def d16(i, carry):
        for t in range(QUAD):
            op(2, QUAD * i + t, 0, 1, True)
        return carry

    lax.fori_loop(0, QUAD, d16, 0)
    for r in range(QUAD):
        op(1, r, 0, 1, True)

    def d4(nq, carry):
        for r in range(QUAD):
            op(1, r, pl.multiple_of(nq * ATTN_BLOCK, ATTN_BLOCK), 1, False)
        return carry

    lax.fori_loop(1, QUAD_ROWS // ATTN_BLOCK, d4, 0)
    op(0, None, 0, 1, True)

    def d1(j, carry):
        for t in range(3):
            op(0, None, pl.multiple_of((1 + 3 * j + t) * ATTN_BLOCK, ATTN_BLOCK), 1, False)
        return carry

    lax.fori_loop(0, (SEQ // ATTN_BLOCK - 1) // 3, d1, 0)


def _keys(src, krows, first):
    kb = src[krows, :].astype(BF16)
    return jnp.concatenate([kb, kb], axis=0) if first else kb


def _table(seg, first):
    return len(DILATIONS) + seg if first else seg


def _kv_rows(start, dil, first):
    if first:
        return _rows(start, ATTN_BLOCK, dil)
    return _rows(start - ATTN_BLOCK * dil, 2 * ATTN_BLOCK, dil)


MERGE_ROWS = 256


def attn_fwd(proj, bias, nb_local, name, ride=None):
    n = proj.shape[0]
    nseg = len(DILATIONS)

    def body(q_ref, k_ref, v_ref, b_ref, o_ref, lse_ref, q4_ref, k4_ref, v4_ref, os0_ref, ls0_ref, os4_ref, ls4_ref,
             q16_ref, k16_ref, v16_ref, os16_ref, ls16_ref):
        for src, mid, dst in ((q_ref, q4_ref, q16_ref), (k_ref, k4_ref, k16_ref), (v_ref, v4_ref, v16_ref)):
            _deinterleave(src, mid)
            _deinterleave_again(mid, dst)

        def op(seg, r, start, stride, first):
            qrows = _rows(start, ATTN_BLOCK, stride)
            krows = _kv_rows(start, stride, first)
            if seg == 0:
                q_src, k_src, v_src, o_dst, l_dst = q_ref, k_ref, v_ref, os0_ref, ls0_ref
            elif seg == 1:
                q_src, k_src, v_src = q4_ref.at[r], k4_ref.at[r], v4_ref.at[r]
                o_dst, l_dst = os4_ref.at[r], ls4_ref.at[r]
            else:
                q_src, k_src, v_src = q16_ref.at[r], k16_ref.at[r], v16_ref.at[r]
                o_dst, l_dst = os16_ref.at[r], ls16_ref.at[r]
            q2, kb, vb = q_src[qrows, :], _keys(k_src, krows, first), _keys(v_src, krows, first)
            head0 = _head0_lanes()
            outs, lses = [], []
            for hh in range(2):
                s = _attn_scores(_one_head(q2, head0, hh), kb, b_ref[_table(seg, first), hh])
                m = jnp.max(s, axis=-1, keepdims=True)
                p = jnp.exp(s - m)
                l = jnp.sum(p, axis=-1, keepdims=True)
                outs.append(_dot(p.astype(BF16), vb) / l)
                lses.append(jnp.broadcast_to(m + jnp.log(l), (ATTN_BLOCK, LANE)))
            o_dst[qrows, :] = jnp.where(head0, outs[0], outs[1])
            l_dst[qrows, :] = jnp.where(head0, lses[0], lses[1])

        _attn_schedule(op)
        _interleave_back(os16_ref, os4_ref, QUAD)
        _interleave_back(ls16_ref, ls4_ref, QUAD)

        for r, nat, quad in _quad_tiles():
            ls = [ls0_ref[nat, :], ls4_ref[r, quad, :], ls4_ref[QUAD + r, quad, :]]
            m = functools.reduce(jnp.maximum, ls)
            ws = [jnp.exp(l - m) for l in ls]
            den = ws[0] + ws[1] + ws[2]
            num = ws[0] * os0_ref[nat, :] + ws[1] * os4_ref[r, quad, :] + ws[2] * os4_ref[QUAD + r, quad, :]
            o_ref[nat, :] = num / den
            lse_ref[nat, :] = m + jnp.log(den)

    def in_spec(off):
        return pl.BlockSpec((SEQ, LANE), lambda b, p: (b, off // LANE + p))

    out_spec = pl.BlockSpec((SEQ, LANE), lambda b, p: (b, p))
    return _call(
        body, name=name, grid=(nb_local, HEAD_PAIRS),
        in_specs=[in_spec(Q_OFF), in_spec(K_OFF), in_spec(V_OFF),
                  pl.BlockSpec((2 * nseg, 2, ATTN_BLOCK, 2 * ATTN_BLOCK), lambda b, p: (0, p, 0, 0))],
        out_specs=[out_spec, out_spec],
        out_shape=[SDS((n, B_WIDTH), F32), SDS((n, B_WIDTH), F32)],
        scratch_shapes=[pltpu.VMEM((QUAD, QUAD_ROWS, LANE), F32)] * 3 + [pltpu.VMEM((SEQ, LANE), F32)] * 2
        + [pltpu.VMEM((2 * QUAD, QUAD_ROWS, LANE), F32)] * 2 + [pltpu.VMEM((QUAD * QUAD, ATTN_BLOCK, LANE), F32)] * 5,
        args=[proj, proj, proj, bias], sem=("parallel", "arbitrary"), ride=ride)


def attn_bwd(proj, b_out, dmix, lse_tot, bias, nb_local, name, ride=None):
    n = proj.shape[0]
    nseg = len(DILATIONS)
    a_blocks = A_WIDTH // LANE
    scale = 1.0 / math.sqrt(HEAD_DIM)

    def body(q_ref, k_ref, v_ref, o_ref, do_ref, lse_ref, b_ref, dq_ref, dk_ref, dv_ref, db_ref,
             dqs_ref, delta_ref, dka_ref, dva_ref, q4_ref, k4_ref, v4_ref, do4_ref, lse4_ref, delta4_ref,
             dqs4_ref, dk4_ref, dv4_ref, q16_ref, k16_ref, v16_ref, do16_ref, lse16_ref, delta16_ref,
             dqs16_ref, dk16_ref, dv16_ref):
        @pl.when(pl.program_id(1) == 0)
        def _():
            db_ref[...] = jnp.zeros_like(db_ref)

        for acc_ref in (dka_ref, dva_ref, dk4_ref, dv4_ref):
            acc_ref[...] = jnp.zeros_like(acc_ref)
        quads = (q4_ref, k4_ref, v4_ref, do4_ref, lse4_ref, delta4_ref)
        hexes = (q16_ref, k16_ref, v16_ref, do16_ref, lse16_ref, delta16_ref)

        def row_dots(i, carry):
            rows = pl.ds(pl.multiple_of(i * ATTN_BLOCK, ATTN_BLOCK), ATTN_BLOCK)
            head0 = _head0_lanes()
            prod = do_ref[rows, :] * o_ref[rows, :]
            d0 = jnp.sum(jnp.where(head0, prod, 0.0), axis=-1, keepdims=True)
            d1 = jnp.sum(jnp.where(head0, 0.0, prod), axis=-1, keepdims=True)
            delta_ref[rows, :] = jnp.where(head0, d0, d1)
            return carry

        lax.fori_loop(0, SEQ // ATTN_BLOCK, row_dots, 0)
        for src, mid, dst in zip((q_ref, k_ref, v_ref, do_ref, lse_ref, delta_ref), quads, hexes):
            _deinterleave(src, mid)
            _deinterleave_again(mid, dst)

        def op(seg, r, start, stride, first):
            qrows = _rows(start, ATTN_BLOCK, stride)
            krows = _kv_rows(start, stride, first)
            if seg == 0:
                srcs = (q_ref, k_ref, v_ref, do_ref, lse_ref, delta_ref)
                dq_dst, dk_dst, dv_dst = dqs_ref, dka_ref, dva_ref
            elif seg == 1:
                srcs = tuple(x.at[r] for x in quads)
                dq_dst, dk_dst, dv_dst = dqs4_ref.at[r], dk4_ref.at[r], dv4_ref.at[r]
            else:
                srcs = tuple(x.at[r] for x in hexes)
                dq_dst, dk_dst, dv_dst = dqs16_ref.at[r], dk16_ref.at[r], dv16_ref.at[r]
            q_src, k_src, v_src, do_src, lse_src, delta_src = srcs
            q2, kb, vb = q_src[qrows, :], _keys(k_src, krows, first), _keys(v_src, krows, first)
            do2, lse2, delta2 = do_src[qrows, :], lse_src[qrows, :], delta_src[qrows, :]
            head0 = _head0_lanes()
            dqs, dk, dv = [], None, None
            for hh in range(2):
                col = slice(hh * HEAD_DIM, hh * HEAD_DIM + 1)
                q, dob = _one_head(q2, head0, hh), _one_head(do2, head0, hh)
                p = jnp.exp(_attn_scores(q, kb, b_ref[_table(seg, first), hh]) - lse2[:, col])
                dvh = _dot_tn(p.astype(BF16), dob)
                ds = p * (_dot_nt(dob, vb) - delta2[:, col])
                if first:
                    db_ref[seg, hh, :, ATTN_BLOCK:] += ds[:, ATTN_BLOCK:]
                else:
                    db_ref[seg, hh] += ds
                dsb = ds.astype(BF16)
                dqs.append(_dot(dsb, kb))
                dkh = _dot_tn(dsb, q)
                dk = dkh if dk is None else dk + dkh
                dv = dvh if dv is None else dv + dvh
            if first:
                dk, dv = dk[ATTN_BLOCK:], dv[ATTN_BLOCK:]
            dq_dst[qrows, :] = jnp.where(head0, dqs[0], dqs[1]) * scale
            if seg == 2:
                dk_dst[krows, :] = dk * scale
                dv_dst[krows, :] = dv
            else:
                dk_dst[krows, :] += dk * scale
                dv_dst[krows, :] += dv

        _attn_schedule(op)
        _interleave_back(dqs16_ref, dqs4_ref, QUAD)
        _interleave_back(dk16_ref, dk4_ref, 0, accumulate=True)
        _interleave_back(dv16_ref, dv4_ref, 0, accumulate=True)

        for r, nat, quad in _quad_tiles():
            dqs_ref[nat, :] += dqs4_ref[r, quad, :] + dqs4_ref[QUAD + r, quad, :]
            dka_ref[nat, :] += dk4_ref[r, quad, :]
            dva_ref[nat, :] += dv4_ref[r, quad, :]

        def merge(i, carry):
            rows = pl.ds(pl.multiple_of(i * MERGE_ROWS, MERGE_ROWS), MERGE_ROWS)
            dq_ref[rows, :] = dqs_ref[rows, :].astype(BF16)
            dk_ref[rows, :] = dka_ref[rows, :].astype(BF16)
            dv_ref[rows, :] = dva_ref[rows, :].astype(BF16)
            return carry

        lax.fori_loop(0, SEQ // MERGE_ROWS, merge, 0)

    def pspec(off):
        return pl.BlockSpec((SEQ, LANE), lambda p, b: (b, off // LANE + p))

    ospec = pl.BlockSpec((SEQ, LANE), lambda p, b: (b, p))
    bspec = pl.BlockSpec((nseg, 2, ATTN_BLOCK, 2 * ATTN_BLOCK), lambda p, b: (0, p, 0, 0))
    gshape = SDS((n, B_WIDTH), BF16)
    return _call(
        body, name=name, grid=(HEAD_PAIRS, nb_local),
        in_specs=[pspec(Q_OFF), pspec(K_OFF), pspec(V_OFF), ospec,
                  pl.BlockSpec((SEQ, LANE), lambda p, b: (b, a_blocks + p)), ospec,
                  pl.BlockSpec((2 * nseg, 2, ATTN_BLOCK, 2 * ATTN_BLOCK), lambda p, b: (0, p, 0, 0))],
        out_specs=[ospec, ospec, ospec, bspec],
        out_shape=[gshape, gshape, gshape, SDS((nseg, B_HEADS, ATTN_BLOCK, 2 * ATTN_BLOCK), F32)],
        scratch_shapes=[pltpu.VMEM((SEQ, LANE), F32)] * 4 + [pltpu.VMEM((QUAD, QUAD_ROWS, LANE), F32)] * 6
        + [pltpu.VMEM((2 * QUAD, QUAD_ROWS, LANE), F32)] + [pltpu.VMEM((QUAD, QUAD_ROWS, LANE), F32)] * 2
        + [pltpu.VMEM((QUAD * QUAD, ATTN_BLOCK, LANE), F32)] * 9,
        args=[proj, proj, proj, b_out, dmix, lse_tot, bias], sem=("arbitrary", "arbitrary"), ride=ride)


PAD = 8
CONV_ROWS = 64


def _conv_taps(gp_ref, head_ref, r0):
    g0 = gp_ref[r0:r0 + CONV_ROWS, :]
    if r0 == 0:
        return g0, head_ref[PAD - 1:PAD - 1 + CONV_ROWS, :], head_ref[PAD - 2:PAD - 2 + CONV_ROWS, :]
    return g0, gp_ref[r0 - 1:r0 - 1 + CONV_ROWS, :], gp_ref[r0 - 2:r0 - 2 + CONV_ROWS, :]


def _fill_head(gp_ref, head_ref):
    head_ref[0:PAD, :] = jnp.zeros((PAD, LANE), F32)
    head_ref[PAD:PAD + CONV_ROWS, :] = gp_ref[0:CONV_ROWS, :]


def conv_gelu_fwd(gp, up, cw, cb, nb_local, name):
    n, f = gp.shape

    def body(gp_ref, up_ref, cw_ref, cb_ref, o_ref, head_ref):
        _fill_head(gp_ref, head_ref)
        w0, w1, w2, bias = cw_ref[0:1, :], cw_ref[1:2, :], cw_ref[2:3, :], cb_ref[...]
        for r0 in range(0, SEQ, CONV_ROWS):
            g0, g1, g2 = _conv_taps(gp_ref, head_ref, r0)
            c = bias + w0 * g2 + w1 * g1 + w2 * g0
            o_ref[r0:r0 + CONV_ROWS, :] = (_gelu(c) * up_ref[r0:r0 + CONV_ROWS, :]).astype(BF16)

    blk = pl.BlockSpec((SEQ, LANE), lambda b, j: (b, j))
    return pl.pallas_call(
        body, name=name, grid=(nb_local, f // LANE),
        in_specs=[blk, blk, pl.BlockSpec((3, LANE), lambda b, j: (0, j)), pl.BlockSpec((1, LANE), lambda b, j: (0, j))],
        out_specs=blk,
        out_shape=SDS((n, f), BF16),
        scratch_shapes=[pltpu.VMEM((PAD + CONV_ROWS, LANE), F32)],
        compiler_params=_params(("parallel", "parallel")),
    )(gp, up, cw, cb)


def conv_gelu_bwd(dgu, gp, up, cw, cb, nb_local, name, ride=None):
    n, f = gp.shape

    def fold(v):
        return jnp.sum(v.reshape(CONV_ROWS // 8, 8, LANE), axis=0)

    def body(dgu_ref, gp_ref, up_ref, cw_ref, cb_ref, dgp_ref, dup_ref, dcw_ref, dcb_ref, head_ref, dc_ref):
        b = pl.program_id(1)
        _fill_head(gp_ref, head_ref)
        dc_ref[SEQ:SEQ + PAD, :] = jnp.zeros((PAD, LANE), F32)
        w0, w1, w2, bias = cw_ref[0:1, :], cw_ref[1:2, :], cw_ref[2:3, :], cb_ref[...]
        sums = [jnp.zeros((8, LANE), F32) for _ in range(4)]
        for r0 in range(0, SEQ, CONV_ROWS):
            rows = slice(r0, r0 + CONV_ROWS)
            g0, g1, g2 = _conv_taps(gp_ref, head_ref, r0)
            gg, dgg = _gelu_and_grad(bias + w0 * g2 + w1 * g1 + w2 * g0)
            dgu = dgu_ref[rows, :].astype(F32)
            dup_ref[rows, :] = (dgu * gg).astype(BF16)
            dc = dgu * up_ref[rows, :] * dgg
            dc_ref[rows, :] = dc
            sums = [sums[0] + fold(dc * g2), sums[1] + fold(dc * g1), sums[2] + fold(dc * g0), sums[3] + fold(dc)]
        for r0 in range(0, SEQ, CONV_ROWS):
            dgp_ref[r0:r0 + CONV_ROWS, :] = (
                w2 * dc_ref[r0:r0 + CONV_ROWS, :] + w1 * dc_ref[r0 + 1:r0 + 1 + CONV_ROWS, :]
                + w0 * dc_ref[r0 + 2:r0 + 2 + CONV_ROWS, :]).astype(BF16)
        dcw = jnp.concatenate([jnp.sum(s, axis=0, keepdims=True) for s in sums[:3]], axis=0)
        dcb = jnp.sum(sums[3], axis=0, keepdims=True)

        @pl.when(b == 0)
        def _():
            dcw_ref[...] = dcw
            dcb_ref[...] = dcb

        @pl.when(b > 0)
        def _():
            dcw_ref[...] += dcw
            dcb_ref[...] += dcb

    blk = pl.BlockSpec((SEQ, LANE), lambda j, b: (b, j))
    return _call(
        body, name=name, grid=(f // LANE, nb_local),
        in_specs=[blk, blk, blk, pl.BlockSpec((3, LANE), lambda j, b: (0, j)), pl.BlockSpec((1, LANE), lambda j, b: (0, j))],
        out_specs=[blk, blk, pl.BlockSpec((3, LANE), lambda j, b: (0, j)), pl.BlockSpec((1, LANE), lambda j, b: (0, j))],
        out_shape=[SDS((n, f), BF16), SDS((n, f), BF16), SDS((3, f), F32), SDS((1, f), F32)],
        scratch_shapes=[pltpu.VMEM((PAD + CONV_ROWS, LANE), F32), pltpu.VMEM((SEQ + PAD, LANE), F32)],
        args=[dgu, gp, up, cw, cb], sem=("parallel", "arbitrary"), ride=ride)


def norm_mid_epilogue(x1, dout, z2, g3, g2):
    n, d = x1.shape

    def fn(dh2, step, x1_ref, dout_ref, z2_ref, g3_ref, g2_ref, dx1_ref, dz2_ref, dg3_ref, dg2_ref):
        dxa, dg3r = _rms_bwd(dh2, x1_ref[...], g3_ref[...])
        dx1 = dout_ref[...] + dxa
        dx1_ref[...] = dx1
        dz2, dg2r = _rms_bwd(dx1, z2_ref[...], g2_ref[...])
        dz2_ref[...] = dz2.astype(BF16)
        _accumulate(dg3_ref, jnp.sum(dg3r, axis=0, keepdims=True), step)
        _accumulate(dg2_ref, jnp.sum(dg2r, axis=0, keepdims=True), step)

    return fn, [x1, dout, z2, g3, g2], [SDS((n, d), F32), SDS((n, d), BF16), SDS((1, d), F32), SDS((1, d), F32)]


def norm_in_epilogue(x, dx1, g1):
    n, d = x.shape

    def fn(dh1, step, x_ref, dx1_ref, g1_ref, dx_ref, dg1_ref):
        dxa, dgr = _rms_bwd(dh1, x_ref[...], g1_ref[...])
        dx_ref[...] = dx1_ref[...] + dxa
        _accumulate(dg1_ref, jnp.sum(dgr, axis=0, keepdims=True), step)

    return fn, [x, dx1, g1], [SDS((n, d), F32), SDS((1, d), F32)]


def cast_bf16(arrays, name):
    def body(*refs):
        for i_ref, o_ref in zip(refs[:len(arrays)], refs[len(arrays):]):
            o_ref[...] = i_ref[...].astype(BF16)

    return pl.pallas_call(body, name=name, out_shape=[SDS(a.shape, BF16) for a in arrays],
                          compiler_params=_params())(*arrays)


def adam_update(parts, w, m, v, name, tr=None):
    s, r, c = parts.shape
    tr = r if tr is None else tr
    bc1 = 1.0 - ADAM_B1 ** ADAM_STEP
    bc2 = 1.0 - ADAM_B2 ** ADAM_STEP

    def body(p_ref, w_ref, m_ref, v_ref, g_ref, d_ref, nm_ref, nv_ref):
        g = p_ref[0].astype(F32)
        for j in range(1, s):
            g = g + p_ref[j].astype(F32)
        nm = ADAM_B1 * m_ref[...] + (1.0 - ADAM_B1) * g
        nv = ADAM_B2 * v_ref[...] + (1.0 - ADAM_B2) * (g * g)
        g_ref[...] = g
        nm_ref[...] = nm
        nv_ref[...] = nv
        d_ref[...] = -ADAM_LR * ((nm / bc1) / (jnp.sqrt(nv / bc2) + ADAM_EPS) + ADAM_WD * w_ref[...])

    blk = pl.BlockSpec((tr, c), lambda i: (i, 0))
    return pl.pallas_call(
        body, name=name, grid=(r // tr,),
        in_specs=[pl.BlockSpec((s, tr, c), lambda i: (0, i, 0)), blk, blk, blk],
        out_specs=[blk] * 4, out_shape=[SDS((r, c), F32)] * 4,
        compiler_params=_params(("parallel",)),
    )(parts, w, m, v)


EARLY_NAMES = ("spatial_w", "norm_mix_post", "norm_ffn_pre", "norm_ffn_post", "conv_b", "ln_v_gain", "ln_v_bias",
               "spatial_b")
LATE_NAMES = ("norm_mix_pre", "rel_bias")
PACK_ROW_ALIGN = 8


def _pack_rows(size):
    rows = -(-size // LANE)
    return -(-rows // PACK_ROW_ALIGN) * PACK_ROW_ALIGN


def _pack(arrays):
    flat = []
    for a in arrays:
        rows = _pack_rows(a.size)
        flat.append(jnp.pad(a.reshape(-1), (0, rows * LANE - a.size)))
    return jnp.concatenate(flat).reshape(-1, LANE)


def _unpack(packed, shapes):
    out, row = [], 0
    for shp in shapes:
        size = int(np.prod(shp))
        out.append(packed[row:row + _pack_rows(size)].reshape(-1)[:size].reshape(shp))
        row += _pack_rows(size)
    return out


def kernel(x, norm_mix_pre, norm_mix_post, norm_ffn_pre, norm_ffn_post, w_in, ln_v_gain, ln_v_bias, spatial_w, spatial_b, rel_bias, w_out, w_gate, w_up, conv_w, conv_b, w_down, loss_target, m_norm_mix_pre, m_norm_mix_post, m_norm_ffn_pre, m_norm_ffn_post, m_w_in, m_ln_v_gain, m_ln_v_bias, m_spatial_w, m_spatial_b, m_rel_bias, m_w_out, m_w_gate, m_w_up, m_conv_w, m_conv_b, m_w_down, v_norm_mix_pre, v_norm_mix_post, v_norm_ffn_pre, v_norm_ffn_post, v_w_in, v_ln_v_gain, v_ln_v_bias, v_spatial_w, v_spatial_b, v_rel_bias, v_w_out, v_w_gate, v_w_up, v_conv_w, v_conv_b, v_w_down):
    given = dict(locals())
    nb_local, seq, d = x.shape
    n = nb_local * seq
    cols = w_in.shape[2]

    def by_columns(g):
        return g.transpose(1, 0, 2).reshape(g.shape[1], N_DEV * g.shape[2])

    def to_blocks(g):
        return g.reshape(g.shape[0], N_DEV, cols).transpose(1, 0, 2)

    xf, target = x.reshape(n, d), loss_target.reshape(n, d)
    ln_g, ln_b = ln_v_gain.reshape(1, A_WIDTH), ln_v_bias.reshape(1, A_WIDTH)
    spatial_bt, rel_bias_t = spatial_b[0].T, rel_bias.T

    s_in, s_out, s_gate, s_up, s_down = cast_bf16([w_in[0], w_out[0], w_gate[0], w_up[0], w_down[0]], "cast_shards")
    g_in, g_cw = exchange([], [s_in, conv_w[0]], "gather_w_in")
    w_in_f, conv_w_f = by_columns(g_in), by_columns(g_cw)

    (h1, proj), _ = norm_mm(xf, norm_mix_pre, [w_in_f], "fwd_norm_in", tn=IN_COLS)
    a = gating_fwd(proj, ln_g, ln_b, spatial_w[0], spatial_bt, "fwd_gating")
    bias = bias_tables(rel_bias_t, "bias_tables").reshape(2 * len(DILATIONS), B_HEADS, ATTN_BLOCK, 2 * ATTN_BLOCK)
    (b_out, lse_tot), (g_out, g_gate, g_up) = attn_fwd(proj, bias, nb_local, "fwd_attn",
                                                       ride=([], [s_out, s_gate, s_up]))
    w_out_f, w_gate_f, w_up_f = g_out.reshape(D_MODEL, D_MODEL), by_columns(g_gate), by_columns(g_up)
    z2, x1 = mm_res_norm([a, b_out], w_out_f, xf, norm_mix_post, "fwd_out_norm")
    (h2, gp, up), (g_down,) = norm_mm(x1, norm_ffn_pre, [w_gate_f, w_up_f], "fwd_norm_ffn", tm=256, tn=D_FF,
                                      ride=([], [s_down]))
    w_down_f = g_down.reshape(D_FF, D_MODEL)
    gu = conv_gelu_fwd(gp, up, conv_w_f, conv_b, nb_local, "fwd_conv_gelu")
    dy, dout, dg4, loss_part = down_loss(gu, w_down_f, x1, norm_ffn_post, target, "fwd_down_loss")

    p_down = mm_tn([gu], [dy], "bwd_dw_down", t1=256, t2=D_MODEL)
    (dgu,), _ = mm_nt([(dy, 0, 0)], [w_down_f], "bwd_dgu", out_dtype=BF16)
    (dgp, dup, p_conv_w, p_conv_b), (r_down,) = conv_gelu_bwd(
        dgu, gp, up, conv_w_f, conv_b, nb_local, "bwd_conv_gelu", ride=([p_down.reshape(N_DEV, cols, D_MODEL)], []))
    p_gate = mm_tn([h2], [dgp], "bwd_dw_gate", t1=D_MODEL, t2=256)
    p_up = mm_tn([h2], [dup], "bwd_dw_up", t1=D_MODEL, t2=256)
    (dx1, dz2, dg3, dg2), _ = mm_nt([(dgp, 0, 0), (dup, 1, 0)], [w_gate_f, w_up_f], "bwd_dh2_norm_mid", tm=256,
                                    epilogue=norm_mid_epilogue(x1, dout, z2, norm_ffn_pre, norm_mix_post))
    p_out = mm_tn([a, b_out], [dz2], "bwd_dw_out", t1=256, t2=D_MODEL)
    (dmix,), _ = mm_nt([(dz2, 0, 0)], [w_out_f], "bwd_dmix")
    duv, p_ws, p_sbt, p_lng, p_lnb = gating_bwd(proj, dmix, ln_g, ln_b, spatial_w[0], spatial_bt, "bwd_gating")
    small = dict(spatial_w=p_ws, norm_mix_post=dg2, norm_ffn_pre=dg3, norm_ffn_post=dg4, conv_b=p_conv_b,
                 ln_v_gain=p_lng, ln_v_bias=p_lnb, spatial_b=p_sbt.T)
    pack_early = _pack([small[k] for k in EARLY_NAMES] + [p_conv_w, loss_part])
    (dq, dk, dv, dbias), (r_gate, r_up, r_out, r_early) = attn_bwd(
        proj, b_out, dmix, lse_tot, bias, nb_local, "bwd_attn",
        ride=([to_blocks(p_gate), to_blocks(p_up), p_out.reshape(N_DEV, D_MODEL // N_DEV, D_MODEL)], [pack_early]))
    p_rel_bias_t = rel_bias_grad(dbias.reshape(len(DILATIONS), B_HEADS, BIAS_SIZE), "bwd_rel_bias")
    p_in = mm_tn([h1], [duv, dq, dk, dv], "bwd_dw_in", t1=D_MODEL, t2=256)
    (grad_x, dg1), (r_in,) = mm_nt(
        [(duv, 0, 0), (dq, 0, Q_OFF), (dk, 0, K_OFF), (dv, 0, V_OFF)], [w_in_f], "bwd_dh1_norm_in",
        epilogue=norm_in_epilogue(xf, dx1, norm_mix_pre), ride=([to_blocks(p_in)], []))
    small.update(norm_mix_pre=dg1, rel_bias=p_rel_bias_t.T)
    (r_late,) = exchange([], [_pack([small[k] for k in LATE_NAMES])], "exchange_late")

    res = {}
    res["w_in"] = adam_update(r_in, w_in[0], m_w_in[0], v_w_in[0], "adam_w_in", tr=256)
    res["w_out"] = adam_update(r_out, w_out[0], m_w_out[0], v_w_out[0], "adam_w_out")
    res["w_gate"] = adam_update(r_gate, w_gate[0], m_w_gate[0], v_w_gate[0], "adam_w_gate", tr=256)
    res["w_up"] = adam_update(r_up, w_up[0], m_w_up[0], v_w_up[0], "adam_w_up", tr=256)
    res["w_down"] = adam_update(r_down, w_down[0], m_w_down[0], v_w_down[0], "adam_w_down", tr=176)

    def adam_packed(received, names, tail, name):
        zeros = [jnp.zeros_like(t) for t in tail]
        packs = [_pack([given[pre + k] for k in names] + zeros) for pre in ("", "m_", "v_")]
        shapes = [given[k].shape for k in names] + [t.shape for t in tail]
        unpacked = [_unpack(p, shapes) for p in adam_update(received, *packs, name)]
        for i, k in enumerate(names):
            res[k] = [u[i] for u in unpacked]
        return unpacked[0][len(names):]

    g_conv_w_full, loss_sum = adam_packed(r_early, EARLY_NAMES, [p_conv_w, loss_part], "adam_small_early")
    adam_packed(r_late, LATE_NAMES, [], "adam_small_late")
    g_conv_w = lax.dynamic_slice_in_dim(g_conv_w_full, _my_index() * cols, cols, axis=1)
    res["conv_w"] = adam_update(g_conv_w[None], conv_w[0], m_conv_w[0], v_conv_w[0], "adam_conv_w")
    loss = loss_sum[0, 0]

    names = ("norm_mix_pre", "norm_mix_post", "norm_ffn_pre", "norm_ffn_post", "w_in", "ln_v_gain", "ln_v_bias",
             "spatial_w", "spatial_b", "rel_bias", "w_out", "w_gate", "w_up", "conv_w", "conv_b", "w_down")
    outs = [loss, grad_x.reshape(x.shape)]
    for t in range(4):
        outs += [res[k][t].reshape(given[k].shape) for k in names]
    return tuple(outs)
```

```python
import functools
import math

import numpy as np
import jax
import jax.numpy as jnp
from jax import lax
from jax.experimental import pallas as pl
from jax.experimental.pallas import tpu as pltpu

F32 = jnp.float32
BF16 = jnp.bfloat16
SDS = jax.ShapeDtypeStruct

D_MODEL = 1024
SEQ = 2048
HEAD_DIM = 64
A_GROUPS = 4
A_WIDTH = A_GROUPS * HEAD_DIM
B_HEADS = 12
B_WIDTH = B_HEADS * HEAD_DIM
HEAD_PAIRS = B_HEADS // 2
CHUNK = 128
ATTN_BLOCK = 128
DILATIONS = (1, 4, 16)
NUM_BUCKETS = 32
MAX_DISTANCE = 2048
D_FF = 2816
IN_COLS = 2 * A_WIDTH + 3 * B_WIDTH
Q_OFF = 2 * A_WIDTH
K_OFF = Q_OFF + B_WIDTH
V_OFF = K_OFF + B_WIDTH
NORM_EPS = 1e-6
NEG_INF = -1e30
N_DEV = 8
LANE = 128

ADAM_LR = 0.001
ADAM_B1 = 0.9
ADAM_B2 = 0.999
ADAM_EPS = 1e-08
ADAM_WD = 0.01
ADAM_STEP = 10

GELU_C0 = math.sqrt(2.0 / math.pi)
GELU_C1 = 0.044715

VMEM_LIMIT = 56 * 1024 * 1024


def _params(sem=None):
    if sem is None:
        return pltpu.CompilerParams(vmem_limit_bytes=VMEM_LIMIT)
    return pltpu.CompilerParams(dimension_semantics=sem, vmem_limit_bytes=VMEM_LIMIT)


def _gelu(x):
    t = jnp.tanh(GELU_C0 * (x + GELU_C1 * x * x * x))
    return 0.5 * x * (1.0 + t)


def _gelu_and_grad(x):
    x2 = x * x
    t = jnp.tanh(GELU_C0 * (x + GELU_C1 * x * x2))
    g = 0.5 * x * (1.0 + t)
    dg = 0.5 * (1.0 + t) + 0.5 * x * (1.0 - t * t) * (GELU_C0 * (1.0 + 3.0 * GELU_C1 * x2))
    return g, dg


def _dot(a, b):
    return jnp.dot(a, b, preferred_element_type=F32)


def _dot_nt(a, b):
    return lax.dot_general(a, b, (((1,), (1,)), ((), ())), preferred_element_type=F32)


def _dot_tn(a, b):
    return lax.dot_general(a, b, (((0,), (0,)), ((), ())), preferred_element_type=F32)


def _rms_bwd(d, xin, g):
    r = lax.rsqrt(jnp.mean(xin * xin, axis=-1, keepdims=True) + NORM_EPS)
    xh = xin * r
    gd = g * d
    dx = r * (gd - xh * jnp.mean(gd * xh, axis=-1, keepdims=True))
    return dx, d * xh


MESH = pl.DeviceIdType.MESH
ANY = pl.BlockSpec(memory_space=pl.ANY)
PEER_MASKS = tuple(range(1, N_DEV))


def _my_index():
    return lax.axis_index("x") * 4 + lax.axis_index("y") * 2 + lax.axis_index("c")


def _peer(mask):
    x, y, c = lax.axis_index("x"), lax.axis_index("y"), lax.axis_index("c")
    px = 1 - x if mask & 4 else x
    py = 1 - y if mask & 2 else y
    pc = 1 - c if mask & 1 else c
    return (px, py, pc), px * 4 + py * 2 + pc


def _exchange_copies(nblocked, in_refs, out_refs, sems):
    send_sems, recv_sems, local_sems = sems
    me = _my_index()
    local, sends, recvs = [], [], []
    for a, (in_ref, out_ref) in enumerate(zip(in_refs, out_refs)):
        src_of = (lambda idx, r=in_ref: r.at[idx]) if a < nblocked else (lambda idx, r=in_ref: r)
        local.append(pltpu.make_async_copy(src_of(me), out_ref.at[me], local_sems.at[a]))
        for mask in PEER_MASKS:
            peer, pidx = _peer(mask)
            pair = dict(send_sem=send_sems.at[a, mask - 1], recv_sem=recv_sems.at[a, mask - 1],
                        device_id=peer, device_id_type=MESH)
            sends.append(pltpu.make_async_remote_copy(src_ref=src_of(pidx), dst_ref=out_ref.at[me], **pair))
            recvs.append(pltpu.make_async_remote_copy(src_ref=src_of(pidx), dst_ref=out_ref.at[pidx], **pair))
    return local, sends, recvs


def _exchange_start(nblocked, in_refs, out_refs, sems):
    local, sends, _ = _exchange_copies(nblocked, in_refs, out_refs, sems)
    for cp in local + sends:
        cp.start()


def _exchange_wait(nblocked, in_refs, out_refs, sems):
    local, sends, recvs = _exchange_copies(nblocked, in_refs, out_refs, sems)
    for cp in sends:
        cp.wait_send()
    for cp in recvs:
        cp.wait_recv()
    for cp in local:
        cp.wait()


def _exchange_out_shape(blocked, whole):
    return [SDS(b.shape, b.dtype) for b in blocked] + [SDS((N_DEV,) + w.shape, w.dtype) for w in whole]


def _exchange_sems(n):
    return [pltpu.SemaphoreType.DMA((n, N_DEV - 1)), pltpu.SemaphoreType.DMA((n, N_DEV - 1)),
            pltpu.SemaphoreType.DMA((n,))]


def exchange(blocked, whole, name):
    nb, n = len(blocked), len(blocked) + len(whole)

    def body(*refs):
        _exchange_start(nb, refs[:n], refs[n:2 * n], refs[2 * n:])
        _exchange_wait(nb, refs[:n], refs[n:2 * n], refs[2 * n:])

    return pl.pallas_call(
        body, name=name, in_specs=[ANY] * n, out_specs=[ANY] * n, out_shape=_exchange_out_shape(blocked, whole),
        scratch_shapes=_exchange_sems(n),
    )(*blocked, *whole)


def _call(body, *, name, grid, in_specs, out_specs, out_shape, args, scratch_shapes=(), sem=None, ride=None):
    out_shape, out_specs, scratch_shapes = list(out_shape), list(out_specs), list(scratch_shapes)
    if ride is None:
        outs = pl.pallas_call(body, name=name, grid=grid, in_specs=list(in_specs), out_specs=out_specs,
                              out_shape=out_shape, scratch_shapes=scratch_shapes,
                              compiler_params=_params(sem))(*args)
        return list(outs), []
    blocked, whole = ride
    cargs = list(blocked) + list(whole)
    nb, nc = len(blocked), len(cargs)
    n_in, n_out, n_scr = len(args), len(out_shape), len(scratch_shapes)

    def riding(*refs):
        ins, refs = refs[:n_in], refs[n_in:]
        cins, refs = refs[:nc], refs[nc:]
        outs, refs = refs[:n_out], refs[n_out:]
        couts, refs = refs[:nc], refs[nc:]
        scr, sems = refs[:n_scr], refs[n_scr:]
        ids = [pl.program_id(k) for k in range(len(grid))]
        first = functools.reduce(lambda p, q: p & q, [i == 0 for i in ids])
        last = functools.reduce(lambda p, q: p & q, [i == g - 1 for i, g in zip(ids, grid)])

        @pl.when(first)
        def _():
            _exchange_start(nb, cins, couts, sems)

        body(*ins, *outs, *scr)

        @pl.when(last)
        def _():
            _exchange_wait(nb, cins, couts, sems)

    res = pl.pallas_call(
        riding, name=name, grid=grid, in_specs=list(in_specs) + [ANY] * nc, out_specs=out_specs + [ANY] * nc,
        out_shape=out_shape + _exchange_out_shape(blocked, whole),
        scratch_shapes=scratch_shapes + _exchange_sems(nc),
        compiler_params=_params(("arbitrary",) * len(grid)))(*args, *cargs)
    return list(res[:n_out]), list(res[n_out:])


def norm_mm(x, g, ws, name, tm=512, tn=1408, ride=None):
    n, d = x.shape
    f = ws[0].shape[1]
    nw = len(ws)

    def body(x_ref, g_ref, *refs):
        w_refs = refs[:nw]
        h_ref = refs[nw]
        o_refs = refs[nw + 1:]

        @pl.when(pl.program_id(1) == 0)
        def _():
            xv = x_ref[...]
            r = lax.rsqrt(jnp.mean(xv * xv, axis=-1, keepdims=True) + NORM_EPS)
            h_ref[...] = (xv * r * g_ref[...]).astype(BF16)

        h = h_ref[...]
        for w_ref, o_ref in zip(w_refs, o_refs):
            o_ref[...] = _dot(h, w_ref[...])

    return _call(
        body, name=name, grid=(n // tm, f // tn),
        in_specs=[pl.BlockSpec((tm, d), lambda i, j: (i, 0)), pl.BlockSpec((1, d), lambda i, j: (0, 0))]
        + [pl.BlockSpec((d, tn), lambda i, j: (0, j)) for _ in ws],
        out_specs=[pl.BlockSpec((tm, d), lambda i, j: (i, 0))]
        + [pl.BlockSpec((tm, tn), lambda i, j: (i, j)) for _ in ws],
        out_shape=[SDS((n, d), BF16)] + [SDS((n, f), F32) for _ in ws],
        args=[x, g, *ws], sem=("parallel", "arbitrary"), ride=ride)


def _lane_concat(refs):
    vals = [r[...].astype(BF16) for r in refs]
    return vals[0] if len(vals) == 1 else jnp.concatenate(vals, axis=1)


def mm_res_norm(a_list, w, res, g, name, tm=512):
    n = a_list[0].shape[0]
    k, d = w.shape
    na = len(a_list)

    def body(*refs):
        w_ref, res_ref, g_ref, y_ref, o_ref = refs[na:]
        y = _dot(_lane_concat(refs[:na]), w_ref[...])
        r = lax.rsqrt(jnp.mean(y * y, axis=-1, keepdims=True) + NORM_EPS)
        y_ref[...] = y
        o_ref[...] = res_ref[...] + y * r * g_ref[...]

    return pl.pallas_call(
        body, name=name, grid=(n // tm,),
        in_specs=[pl.BlockSpec((tm, a.shape[1]), lambda i: (i, 0)) for a in a_list]
        + [pl.BlockSpec((k, d), lambda i: (0, 0)),
           pl.BlockSpec((tm, d), lambda i: (i, 0)), pl.BlockSpec((1, d), lambda i: (0, 0))],
        out_specs=[pl.BlockSpec((tm, d), lambda i: (i, 0)), pl.BlockSpec((tm, d), lambda i: (i, 0))],
        out_shape=[SDS((n, d), F32), SDS((n, d), F32)],
        compiler_params=_params(("parallel",)),
    )(*a_list, w, res, g)


def down_loss(a, w, res, g, target, name, tm=256):
    n, k = a.shape
    d = w.shape[1]
    inv_d = 1.0 / d

    def body(a_ref, w_ref, res_ref, g_ref, t_ref, dy_ref, dout_ref, dg_ref, loss_ref):
        i = pl.program_id(0)
        y = _dot(a_ref[...], w_ref[...])
        gv = g_ref[...]
        r = lax.rsqrt(jnp.mean(y * y, axis=-1, keepdims=True) + NORM_EPS)
        yh = y * r
        e = res_ref[...] + yh * gv - t_ref[...]
        part = 0.5 * inv_d * jnp.sum(jnp.sum(e * e, axis=-1, keepdims=True), axis=0, keepdims=True)
        dout = e * inv_d
        dout_ref[...] = dout
        gd = gv * dout
        dy_ref[...] = (r * (gd - yh * jnp.mean(gd * yh, axis=-1, keepdims=True))).astype(BF16)
        dgp = jnp.sum(dout * yh, axis=0, keepdims=True)
        lane0 = lax.broadcasted_iota(jnp.int32, (1, LANE), 1) == 0
        lp = jnp.where(lane0, part, 0.0)

        @pl.when(i == 0)
        def _():
            dg_ref[...] = dgp
            loss_ref[...] = lp

        @pl.when(i > 0)
        def _():
            dg_ref[...] += dgp
            loss_ref[...] += lp

    return pl.pallas_call(
        body, name=name, grid=(n // tm,),
        in_specs=[pl.BlockSpec((tm, k), lambda i: (i, 0)), pl.BlockSpec((k, d), lambda i: (0, 0)),
                  pl.BlockSpec((tm, d), lambda i: (i, 0)), pl.BlockSpec((1, d), lambda i: (0, 0)),
                  pl.BlockSpec((tm, d), lambda i: (i, 0))],
        out_specs=[pl.BlockSpec((tm, d), lambda i: (i, 0)), pl.BlockSpec((tm, d), lambda i: (i, 0)),
                   pl.BlockSpec((1, d), lambda i: (0, 0)), pl.BlockSpec((1, LANE), lambda i: (0, 0))],
        out_shape=[SDS((n, d), BF16), SDS((n, d), F32), SDS((1, d), F32), SDS((1, LANE), F32)],
        compiler_params=_params(("arbitrary",)),
    )(a, w, res, g, target)


def _accumulate(ref, val, step):
    @pl.when(step == 0)
    def _():
        ref[...] = val

    @pl.when(step > 0)
    def _():
        ref[...] += val


def mm_nt(terms, ws, name, tm=512, out_dtype=F32, ride=None, epilogue=None):
    n = terms[0][0].shape[0]
    r = ws[0].shape[0]
    na = len(terms)
    meta = [(widx, off, a.shape[1]) for a, widx, off in terms]
    fn, extras, out_shape = epilogue if epilogue else (None, [], [SDS((n, r), out_dtype)])
    n_fixed = na + len(ws)

    def body(*refs):
        a_refs = refs[:na]
        w_refs = refs[na:n_fixed]
        acc = None
        for a_ref, (widx, off, k) in zip(a_refs, meta):
            p = _dot_nt(a_ref[...].astype(BF16), w_refs[widx][:, off:off + k])
            acc = p if acc is None else acc + p
        if fn is None:
            refs[-1][...] = acc.astype(out_dtype)
        else:
            fn(acc, pl.program_id(0), *refs[n_fixed:])

    def spec(a):
        if a.shape[0] == 1:
            return pl.BlockSpec(a.shape, lambda i: (0, 0))
        return pl.BlockSpec((tm, a.shape[1]), lambda i: (i, 0))

    return _call(
        body, name=name, grid=(n // tm,),
        in_specs=[spec(a) for a, _, _ in terms] + [pl.BlockSpec(w.shape, lambda i: (0, 0)) for w in ws]
        + [spec(e) for e in extras],
        out_specs=[spec(o) for o in out_shape], out_shape=out_shape,
        args=[a for a, _, _ in terms] + list(ws) + list(extras),
        sem=("parallel",) if fn is None else ("arbitrary",), ride=ride)


def _piece_blocks(pieces, tile):
    out, first = [], 0
    for p in pieces:
        nblk, rem = divmod(p.shape[1], tile)
        assert rem == 0, (p.shape, tile)
        out.append((first, nblk))
        first += nblk
    return out, first


def mm_tn(lhs_list, rhs_list, name, t1, t2, out_dtype=BF16):
    n = lhs_list[0].shape[0]
    lblocks, nbl = _piece_blocks(lhs_list, t1)
    rblocks, nbr = _piece_blocks(rhs_list, t2)
    nl = len(lhs_list)

    def body(*refs):
        l_refs, r_refs, o_ref = refs[:nl], refs[nl:-1], refs[-1]
        i, j = pl.program_id(0), pl.program_id(1)
        for l_ref, (ls, ln) in zip(l_refs, lblocks):
            for r_ref, (rs, rn) in zip(r_refs, rblocks):
                @pl.when((i >= ls) & (i < ls + ln) & (j >= rs) & (j < rs + rn))
                def _(l_ref=l_ref, r_ref=r_ref):
                    o_ref[...] = _dot_tn(l_ref[...].astype(BF16), r_ref[...].astype(BF16)).astype(out_dtype)

    def piece_spec(tile, axis, first, nblk):
        def index(i, j):
            return 0, jnp.clip((i, j)[axis] - first, 0, nblk - 1)
        return pl.BlockSpec((n, tile), index)

    return pl.pallas_call(
        body, name=name, grid=(nbl, nbr),
        in_specs=[piece_spec(t1, 0, *b) for b in lblocks] + [piece_spec(t2, 1, *b) for b in rblocks],
        out_specs=pl.BlockSpec((t1, t2), lambda i, j: (i, j)),
        out_shape=SDS((nbl * t1, nbr * t2), out_dtype),
        compiler_params=_params(("parallel", "arbitrary")),
    )(*lhs_list, *rhs_list)


GATE_ROWS = 512


def _tril_mask():
    row = lax.broadcasted_iota(jnp.int32, (CHUNK, CHUNK), 0)
    col = lax.broadcasted_iota(jnp.int32, (CHUNK, CHUNK), 1)
    return row >= col


def _layer_norm_parts(gv):
    mu = jnp.mean(gv, axis=-1, keepdims=True)
    xc = gv - mu
    rstd = lax.rsqrt(jnp.mean(xc * xc, axis=-1, keepdims=True) + NORM_EPS)
    return xc * rstd, rstd


def gating_fwd(proj, lng, lnb, ws, sbt, name):
    n = proj.shape[0]
    nchunk = GATE_ROWS // CHUNK

    def body(u_ref, v_ref, lng_ref, lnb_ref, ws_ref, sbt_ref, a_ref):
        tril = _tril_mask()
        for g in range(A_GROUPS):
            cs = slice(g * HEAD_DIM, (g + 1) * HEAD_DIM)
            wt = jnp.where(tril, ws_ref[g], 0.0).astype(BF16)
            for c in range(nchunk):
                rs_ = slice(c * CHUNK, (c + 1) * CHUNK)
                vhat, _ = _layer_norm_parts(_gelu(v_ref[rs_, cs]))
                vn = vhat * lng_ref[:, cs] + lnb_ref[:, cs]
                z = _dot(wt, vn.astype(BF16)) + sbt_ref[:, g:g + 1]
                a_ref[rs_, cs] = _gelu(u_ref[rs_, cs]) * z

    return pl.pallas_call(
        body, name=name, grid=(n // GATE_ROWS,),
        in_specs=[pl.BlockSpec((GATE_ROWS, A_WIDTH), lambda i: (i, 0)),
                  pl.BlockSpec((GATE_ROWS, A_WIDTH), lambda i: (i, 1)),
                  pl.BlockSpec((1, A_WIDTH), lambda i: (0, 0)), pl.BlockSpec((1, A_WIDTH), lambda i: (0, 0)),
                  pl.BlockSpec((A_GROUPS, CHUNK, CHUNK), lambda i: (0, 0, 0)),
                  pl.BlockSpec((CHUNK, A_GROUPS), lambda i: (0, 0))],
        out_specs=pl.BlockSpec((GATE_ROWS, A_WIDTH), lambda i: (i, 0)),
        out_shape=SDS((n, A_WIDTH), F32),
        compiler_params=_params(("parallel",)),
    )(proj, proj, lng, lnb, ws, sbt)


def gating_bwd(proj, dmix, lng, lnb, ws, sbt, name):
    n = proj.shape[0]
    nchunk = GATE_ROWS // CHUNK

    def body(u_ref, v_ref, da_ref, lng_ref, lnb_ref, ws_ref, sbt_ref,
             duv_ref, dws_ref, dsbt_ref, dlng_ref, dlnb_ref):
        @pl.when(pl.program_id(0) == 0)
        def _():
            dws_ref[...] = jnp.zeros_like(dws_ref)
            dsbt_ref[...] = jnp.zeros_like(dsbt_ref)
            dlng_ref[...] = jnp.zeros_like(dlng_ref)
            dlnb_ref[...] = jnp.zeros_like(dlnb_ref)

        tril = _tril_mask()
        for g in range(A_GROUPS):
            cs = slice(g * HEAD_DIM, (g + 1) * HEAD_DIM)
            wt = jnp.where(tril, ws_ref[g], 0.0).astype(BF16)
            lg = lng_ref[:, cs]
            dw = jnp.zeros((CHUNK, CHUNK), F32)
            dsb = jnp.zeros((CHUNK, 1), F32)
            dlg = jnp.zeros((1, HEAD_DIM), F32)
            dlb = jnp.zeros((1, HEAD_DIM), F32)
            for c in range(nchunk):
                rs_ = slice(c * CHUNK, (c + 1) * CHUNK)
                gu, dgu_dx = _gelu_and_grad(u_ref[rs_, cs])
                gv, dgv_dx = _gelu_and_grad(v_ref[rs_, cs])
                vhat, rstd = _layer_norm_parts(gv)
                vn = (vhat * lg + lnb_ref[:, cs]).astype(BF16)
                z = _dot(wt, vn) + sbt_ref[:, g:g + 1]
                da = da_ref[rs_, cs]
                dz = da * gu
                dzb = dz.astype(BF16)
                duv_ref[rs_, cs] = (da * z * dgu_dx).astype(BF16)
                dsb = dsb + jnp.sum(dz, axis=-1, keepdims=True)
                dw = dw + _dot_nt(dzb, vn)
                dvn = _dot_tn(wt, dzb)
                dlg = dlg + jnp.sum(dvn * vhat, axis=0, keepdims=True)
                dlb = dlb + jnp.sum(dvn, axis=0, keepdims=True)
                dvh = dvn * lg
                dgv = rstd * (dvh - jnp.mean(dvh, axis=-1, keepdims=True)
                              - vhat * jnp.mean(dvh * vhat, axis=-1, keepdims=True))
                duv_ref[rs_, A_WIDTH + g * HEAD_DIM:A_WIDTH + (g + 1) * HEAD_DIM] = (dgv * dgv_dx).astype(BF16)
            dws_ref[g] += jnp.where(tril, dw, 0.0)
            dsbt_ref[:, g:g + 1] += dsb
            dlng_ref[:, cs] += dlg
            dlnb_ref[:, cs] += dlb

    return pl.pallas_call(
        body, name=name, grid=(n // GATE_ROWS,),
        in_specs=[pl.BlockSpec((GATE_ROWS, A_WIDTH), lambda i: (i, 0)),
                  pl.BlockSpec((GATE_ROWS, A_WIDTH), lambda i: (i, 1)),
                  pl.BlockSpec((GATE_ROWS, A_WIDTH), lambda i: (i, 0)),
                  pl.BlockSpec((1, A_WIDTH), lambda i: (0, 0)), pl.BlockSpec((1, A_WIDTH), lambda i: (0, 0)),
                  pl.BlockSpec((A_GROUPS, CHUNK, CHUNK), lambda i: (0, 0, 0)),
                  pl.BlockSpec((CHUNK, A_GROUPS), lambda i: (0, 0))],
        out_specs=[pl.BlockSpec((GATE_ROWS, 2 * A_WIDTH), lambda i: (i, 0)),
                   pl.BlockSpec((A_GROUPS, CHUNK, CHUNK), lambda i: (0, 0, 0)),
                   pl.BlockSpec((CHUNK, A_GROUPS), lambda i: (0, 0)),
                   pl.BlockSpec((1, A_WIDTH), lambda i: (0, 0)), pl.BlockSpec((1, A_WIDTH), lambda i: (0, 0))],
        out_shape=[SDS((n, 2 * A_WIDTH), BF16), SDS((A_GROUPS, CHUNK, CHUNK), F32), SDS((CHUNK, A_GROUPS), F32),
                   SDS((1, A_WIDTH), F32), SDS((1, A_WIDTH), F32)],
        compiler_params=_params(("arbitrary",)),
    )(proj, proj, dmix, lng, lnb, ws, sbt)


def _t5_bucket_np(dist):
    max_exact = NUM_BUCKETS // 2
    dd = np.maximum(dist, 1).astype(np.float64)
    large = max_exact + np.log(dd / max_exact) / math.log(MAX_DISTANCE / max_exact) * (NUM_BUCKETS - max_exact)
    large = np.minimum(large.astype(np.int64), NUM_BUCKETS - 1)
    return np.where(dist < max_exact, dist, large)


def _bucket_tables(with_first):
    i = np.arange(ATTN_BLOCK)[:, None]
    j = np.arange(2 * ATTN_BLOCK)[None, :]
    rel = ATTN_BLOCK + i - j
    band = (rel >= 0) & (rel <= ATTN_BLOCK)
    tabs = []
    for own_only in (False, True) if with_first else (False,):
        for dil in DILATIONS:
            b = _t5_bucket_np(np.maximum(rel, 0) * dil)
            tabs.append(np.where(band & (j >= ATTN_BLOCK) if own_only else band, b, -1).reshape(1, -1))
    return np.stack(tabs).astype(np.float32)


BIAS_SIZE = ATTN_BLOCK * 2 * ATTN_BLOCK


def bias_tables(rel_bias_t, name):
    idx = jnp.asarray(_bucket_tables(True))
    ntab = idx.shape[0]

    def body(rb_ref, idx_ref, o_ref):
        iv = idx_ref[0]
        bk = lax.broadcasted_iota(jnp.int32, (NUM_BUCKETS, BIAS_SIZE), 0).astype(F32)
        onehot = (bk == iv).astype(F32)
        t = jnp.dot(rb_ref[...], onehot, preferred_element_type=F32, precision=lax.Precision.HIGHEST)
        o_ref[0] = jnp.where(iv < 0.0, NEG_INF, t)

    return pl.pallas_call(
        body, name=name, grid=(ntab,),
        in_specs=[pl.BlockSpec((B_HEADS, NUM_BUCKETS), lambda d: (0, 0)),
                  pl.BlockSpec((1, 1, BIAS_SIZE), lambda d: (d, 0, 0))],
        out_specs=pl.BlockSpec((1, B_HEADS, BIAS_SIZE), lambda d: (d, 0, 0)),
        out_shape=SDS((ntab, B_HEADS, BIAS_SIZE), F32),
        compiler_params=_params(("parallel",)),
    )(rel_bias_t, idx)


def rel_bias_grad(dbias, name):
    idx = jnp.asarray(_bucket_tables(False))

    def body(db_ref, idx_ref, o_ref):
        d = pl.program_id(0)
        iv = idx_ref[0]
        bk = lax.broadcasted_iota(jnp.int32, (NUM_BUCKETS, BIAS_SIZE), 0).astype(F32)
        onehot = (bk == iv).astype(F32)
        part = lax.dot_general(db_ref[0], onehot, (((1,), (1,)), ((), ())),
                               preferred_element_type=F32, precision=lax.Precision.HIGHEST)

        @pl.when(d == 0)
        def _():
            o_ref[...] = part

        @pl.when(d > 0)
        def _():
            o_ref[...] += part

    return pl.pallas_call(
        body, name=name, grid=(len(DILATIONS),),
        in_specs=[pl.BlockSpec((1, B_HEADS, BIAS_SIZE), lambda d: (d, 0, 0)),
                  pl.BlockSpec((1, 1, BIAS_SIZE), lambda d: (d, 0, 0))],
        out_specs=pl.BlockSpec((B_HEADS, NUM_BUCKETS), lambda d: (0, 0)),
        out_shape=SDS((B_HEADS, NUM_BUCKETS), F32),
        compiler_params=_params(("arbitrary",)),
    )(dbias, idx)


def _attn_scores(q, kk, bias):
    return _dot_nt(q, kk) * (1.0 / math.sqrt(HEAD_DIM)) + bias


def _head0_lanes():
    return lax.broadcasted_iota(jnp.int32, (ATTN_BLOCK, LANE), 1) < HEAD_DIM


def _one_head(x2, head0, hh):
    return jnp.where(head0 if hh == 0 else jnp.logical_not(head0), x2, 0.0).astype(BF16)


def _rows(start, size, dil):
    return pl.ds(start, size) if dil == 1 else pl.ds(start, size, stride=dil)


QUAD = 4
QUAD_ROWS = SEQ // QUAD


def _deinterleave(src_ref, dst_ref):
    for r in range(QUAD):
        for c in range(QUAD_ROWS // ATTN_BLOCK):
            dst_ref[r, c * ATTN_BLOCK:(c + 1) * ATTN_BLOCK, :] = src_ref[
                pl.ds(r + c * QUAD * ATTN_BLOCK, ATTN_BLOCK, stride=QUAD), :]


def _deinterleave_again(src_ref, dst_ref):
    for r in range(QUAD):
        for s in range(QUAD):
            dst_ref[r + QUAD * s] = src_ref[r, pl.ds(s, ATTN_BLOCK, stride=QUAD), :]


def _interleave_back(src_ref, dst_ref, slot0, accumulate=False):
    for r in range(QUAD):
        for s in range(QUAD):
            rows = pl.ds(s, ATTN_BLOCK, stride=QUAD)
            if accumulate:
                dst_ref[slot0 + r, rows, :] += src_ref[r + QUAD * s]
            else:
                dst_ref[slot0 + r, rows, :] = src_ref[r + QUAD * s]


def _quad_tiles():
    return [(r, pl.ds(r + c * QUAD * ATTN_BLOCK, ATTN_BLOCK, stride=QUAD), slice(c * ATTN_BLOCK, (c + 1) * ATTN_BLOCK))
            for r in range(QUAD) for c in range(QUAD_ROWS // ATTN_BLOCK)]


def _attn_schedule(op):
    def d16(i, carry):
        for t in range(2 * QUAD):
            op(2, 2 * QUAD * i + t, 0, 1, True)
        return carry

    lax.fori_loop(0, QUAD // 2, d16, 0)

    def d4(i, carry):
        for u in range(2):
            for nq in range(QUAD_ROWS // ATTN_BLOCK):
                op(1, 2 * i + u, nq * ATTN_BLOCK, 1, nq == 0)
        return carry

    lax.fori_loop(0, QUAD // 2, d4, 0)
    op(0, None, 0, 1, True)
    per_pass = 5

    def d1(j, carry):
        for t in range(per_pass):
            op(0, None, pl.multiple_of((1 + per_pass * j + t) * ATTN_BLOCK, ATTN_BLOCK), 1, False)
        return carry

    lax.fori_loop(0, (SEQ // ATTN_BLOCK - 1) // per_pass, d1, 0)


def _keys(src, krows, first):
    kb = src[krows, :].astype(BF16)
    return jnp.concatenate([kb, kb], axis=0) if first else kb


def _table(seg, first):
    return len(DILATIONS) + seg if first else seg


def _kv_rows(start, dil, first):
    if first:
        return _rows(start, ATTN_BLOCK, dil)
    return _rows(start - ATTN_BLOCK * dil, 2 * ATTN_BLOCK, dil)


MERGE_ROWS = 256


def attn_fwd(proj, bias, nb_local, name, ride=None):
    n = proj.shape[0]
    nseg = len(DILATIONS)

    def body(q_ref, k_ref, v_ref, b_ref, o_ref, lse_ref, q4_ref, k4_ref, v4_ref, os0_ref, ls0_ref, os4_ref, ls4_ref,
             q16_ref, k16_ref, v16_ref, os16_ref, ls16_ref):
        for src, mid, dst in ((q_ref, q4_ref, q16_ref), (k_ref, k4_ref, k16_ref), (v_ref, v4_ref, v16_ref)):
            _deinterleave(src, mid)
            _deinterleave_again(mid, dst)

        def op(seg, r, start, stride, first):
            qrows = _rows(start, ATTN_BLOCK, stride)
            krows = _kv_rows(start, stride, first)
            if seg == 0:
                q_src, k_src, v_src, o_dst, l_dst = q_ref, k_ref, v_ref, os0_ref, ls0_ref
            elif seg == 1:
                q_src, k_src, v_src = q4_ref.at[r], k4_ref.at[r], v4_ref.at[r]
                o_dst, l_dst = os4_ref.at[r], ls4_ref.at[r]
            else:
                q_src, k_src, v_src = q16_ref.at[r], k16_ref.at[r], v16_ref.at[r]
                o_dst, l_dst = os16_ref.at[r], ls16_ref.at[r]
            q2, kb, vb = q_src[qrows, :], _keys(k_src, krows, first), _keys(v_src, krows, first)
            head0 = _head0_lanes()
            outs, lses = [], []
            for hh in range(2):
                s = _attn_scores(_one_head(q2, head0, hh), kb, b_ref[_table(seg, first), hh])
                m = jnp.max(s, axis=-1, keepdims=True)
                p = jnp.exp(s - m)
                l = jnp.sum(p, axis=-1, keepdims=True)
                outs.append(_dot(p.astype(BF16), vb) / l)
                lses.append(jnp.broadcast_to(m + jnp.log(l), (ATTN_BLOCK, LANE)))
            o_dst[qrows, :] = jnp.where(head0, outs[0], outs[1])
            l_dst[qrows, :] = jnp.where(head0, lses[0], lses[1])

        _attn_schedule(op)
        _interleave_back(os16_ref, os4_ref, QUAD)
        _interleave_back(ls16_ref, ls4_ref, QUAD)

        for r, nat, quad in _quad_tiles():
            ls = [ls0_ref[nat, :], ls4_ref[r, quad, :], ls4_ref[QUAD + r, quad, :]]
            m = functools.reduce(jnp.maximum, ls)
            ws = [jnp.exp(l - m) for l in ls]
            den = ws[0] + ws[1] + ws[2]
            num = ws[0] * os0_ref[nat, :] + ws[1] * os4_ref[r, quad, :] + ws[2] * os4_ref[QUAD + r, quad, :]
            o_ref[nat, :] = num / den
            lse_ref[nat, :] = m + jnp.log(den)

    def in_spec(off):
        return pl.BlockSpec((SEQ, LANE), lambda b, p: (b, off // LANE + p))

    out_spec = pl.BlockSpec((SEQ, LANE), lambda b, p: (b, p))
    return _call(
        body, name=name, grid=(nb_local, HEAD_PAIRS),
        in_specs=[in_spec(Q_OFF), in_spec(K_OFF), in_spec(V_OFF),
                  pl.BlockSpec((2 * nseg, 2, ATTN_BLOCK, 2 * ATTN_BLOCK), lambda b, p: (0, p, 0, 0))],
        out_specs=[out_spec, out_spec],
        out_shape=[SDS((n, B_WIDTH), F32), SDS((n, B_WIDTH), F32)],
        scratch_shapes=[pltpu.VMEM((QUAD, QUAD_ROWS, LANE), F32)] * 3 + [pltpu.VMEM((SEQ, LANE), F32)] * 2
        + [pltpu.VMEM((2 * QUAD, QUAD_ROWS, LANE), F32)] * 2 + [pltpu.VMEM((QUAD * QUAD, ATTN_BLOCK, LANE), F32)] * 5,
        args=[proj, proj, proj, bias], sem=("parallel", "arbitrary"), ride=ride)


def attn_bwd(proj, b_out, dmix, lse_tot, bias, nb_local, name, ride=None):
    n = proj.shape[0]
    nseg = len(DILATIONS)
    a_blocks = A_WIDTH // LANE
    scale = 1.0 / math.sqrt(HEAD_DIM)

    def body(q_ref, k_ref, v_ref, o_ref, do_ref, lse_ref, b_ref, dq_ref, dk_ref, dv_ref, db_ref,
             dqs_ref, delta_ref, dka_ref, dva_ref, q4_ref, k4_ref, v4_ref, do4_ref, lse4_ref, delta4_ref,
             dqs4_ref, dk4_ref, dv4_ref, q16_ref, k16_ref, v16_ref, do16_ref, lse16_ref, delta16_ref,
             dqs16_ref, dk16_ref, dv16_ref):
        @pl.when(pl.program_id(1) == 0)
        def _():
            db_ref[...] = jnp.zeros_like(db_ref)

        for acc_ref in (dka_ref, dva_ref, dk4_ref, dv4_ref):
            acc_ref[...] = jnp.zeros_like(acc_ref)
        quads = (q4_ref, k4_ref, v4_ref, do4_ref, lse4_ref, delta4_ref)
        hexes = (q16_ref, k16_ref, v16_ref, do16_ref, lse16_ref, delta16_ref)

        def row_dots(i, carry):
            rows = pl.ds(pl.multiple_of(i * ATTN_BLOCK, ATTN_BLOCK), ATTN_BLOCK)
            head0 = _head0_lanes()
            prod = do_ref[rows, :] * o_ref[rows, :]
            d0 = jnp.sum(jnp.where(head0, prod, 0.0), axis=-1, keepdims=True)
            d1 = jnp.sum(jnp.where(head0, 0.0, prod), axis=-1, keepdims=True)
            delta_ref[rows, :] = jnp.where(head0, d0, d1)
            return carry

        lax.fori_loop(0, SEQ // ATTN_BLOCK, row_dots, 0)
        for src, mid, dst in zip((q_ref, k_ref, v_ref, do_ref, lse_ref, delta_ref), quads, hexes):
            _deinterleave(src, mid)
            _deinterleave_again(mid, dst)

        def op(seg, r, start, stride, first):
            qrows = _rows(start, ATTN_BLOCK, stride)
            krows = _kv_rows(start, stride, first)
            if seg == 0:
                srcs = (q_ref, k_ref, v_ref, do_ref, lse_ref, delta_ref)
                dq_dst, dk_dst, dv_dst = dqs_ref, dka_ref, dva_ref
            elif seg == 1:
                srcs = tuple(x.at[r] for x in quads)
                dq_dst, dk_dst, dv_dst = dqs4_ref.at[r], dk4_ref.at[r], dv4_ref.at[r]
            else:
                srcs = tuple(x.at[r] for x in hexes)
                dq_dst, dk_dst, dv_dst = dqs16_ref.at[r], dk16_ref.at[r], dv16_ref.at[r]
            q_src, k_src, v_src, do_src, lse_src, delta_src = srcs
            q2, kb, vb = q_src[qrows, :], _keys(k_src, krows, first), _keys(v_src, krows, first)
            do2, lse2, delta2 = do_src[qrows, :], lse_src[qrows, :], delta_src[qrows, :]
            head0 = _head0_lanes()
            dqs, dk, dv = [], None, None
            for hh in range(2):
                col = slice(hh * HEAD_DIM, hh * HEAD_DIM + 1)
                q, dob = _one_head(q2, head0, hh), _one_head(do2, head0, hh)
                p = jnp.exp(_attn_scores(q, kb, b_ref[_table(seg, first), hh]) - lse2[:, col])
                dvh = _dot_tn(p.astype(BF16), dob)
                ds = p * (_dot_nt(dob, vb) - delta2[:, col])
                if first:
                    db_ref[seg, hh, :, ATTN_BLOCK:] += ds[:, ATTN_BLOCK:]
                else:
                    db_ref[seg, hh] += ds
                dsb = ds.astype(BF16)
                dqs.append(_dot(dsb, kb))
                dkh = _dot_tn(dsb, q)
                dk = dkh if dk is None else dk + dkh
                dv = dvh if dv is None else dv + dvh
            if first:
                dk, dv = dk[ATTN_BLOCK:], dv[ATTN_BLOCK:]
            dq_dst[qrows, :] = jnp.where(head0, dqs[0], dqs[1]) * scale
            if seg == 2:
                dk_dst[krows, :] = dk * scale
                dv_dst[krows, :] = dv
            else:
                dk_dst[krows, :] += dk * scale
                dv_dst[krows, :] += dv

        _attn_schedule(op)
        _interleave_back(dqs16_ref, dqs4_ref, QUAD)
        _interleave_back(dk16_ref, dk4_ref, 0, accumulate=True)
        _interleave_back(dv16_ref, dv4_ref, 0, accumulate=True)

        for r, nat, quad in _quad_tiles():
            dqs_ref[nat, :] += dqs4_ref[r, quad, :] + dqs4_ref[QUAD + r, quad, :]
            dka_ref[nat, :] += dk4_ref[r, quad, :]
            dva_ref[nat, :] += dv4_ref[r, quad, :]

        def merge(i, carry):
            rows = pl.ds(pl.multiple_of(i * MERGE_ROWS, MERGE_ROWS), MERGE_ROWS)
            dq_ref[rows, :] = dqs_ref[rows, :].astype(BF16)
            dk_ref[rows, :] = dka_ref[rows, :].astype(BF16)
            dv_ref[rows, :] = dva_ref[rows, :].astype(BF16)
            return carry

        lax.fori_loop(0, SEQ // MERGE_ROWS, merge, 0)

    def pspec(off):
        return pl.BlockSpec((SEQ, LANE), lambda p, b: (b, off // LANE + p))

    ospec = pl.BlockSpec((SEQ, LANE), lambda p, b: (b, p))
    bspec = pl.BlockSpec((nseg, 2, ATTN_BLOCK, 2 * ATTN_BLOCK), lambda p, b: (0, p, 0, 0))
    gshape = SDS((n, B_WIDTH), BF16)
    return _call(
        body, name=name, grid=(HEAD_PAIRS, nb_local),
        in_specs=[pspec(Q_OFF), pspec(K_OFF), pspec(V_OFF), ospec,
                  pl.BlockSpec((SEQ, LANE), lambda p, b: (b, a_blocks + p)), ospec,
                  pl.BlockSpec((2 * nseg, 2, ATTN_BLOCK, 2 * ATTN_BLOCK), lambda p, b: (0, p, 0, 0))],
        out_specs=[ospec, ospec, ospec, bspec],
        out_shape=[gshape, gshape, gshape, SDS((nseg, B_HEADS, ATTN_BLOCK, 2 * ATTN_BLOCK), F32)],
        scratch_shapes=[pltpu.VMEM((SEQ, LANE), F32)] * 4 + [pltpu.VMEM((QUAD, QUAD_ROWS, LANE), F32)] * 6
        + [pltpu.VMEM((2 * QUAD, QUAD_ROWS, LANE), F32)] + [pltpu.VMEM((QUAD, QUAD_ROWS, LANE), F32)] * 2
        + [pltpu.VMEM((QUAD * QUAD, ATTN_BLOCK, LANE), F32)] * 9,
        args=[proj, proj, proj, b_out, dmix, lse_tot, bias], sem=("arbitrary", "arbitrary"), ride=ride)


PAD = 8
CONV_ROWS = 64


def _conv_taps(gp_ref, head_ref, r0):
    g0 = gp_ref[r0:r0 + CONV_ROWS, :]
    if r0 == 0:
        return g0, head_ref[PAD - 1:PAD - 1 + CONV_ROWS, :], head_ref[PAD - 2:PAD - 2 + CONV_ROWS, :]
    return g0, gp_ref[r0 - 1:r0 - 1 + CONV_ROWS, :], gp_ref[r0 - 2:r0 - 2 + CONV_ROWS, :]


def _fill_head(gp_ref, head_ref):
    head_ref[0:PAD, :] = jnp.zeros((PAD, LANE), F32)
    head_ref[PAD:PAD + CONV_ROWS, :] = gp_ref[0:CONV_ROWS, :]


def conv_gelu_fwd(gp, up, cw, cb, nb_local, name):
    n, f = gp.shape

    def body(gp_ref, up_ref, cw_ref, cb_ref, o_ref, head_ref):
        _fill_head(gp_ref, head_ref)
        w0, w1, w2, bias = cw_ref[0:1, :], cw_ref[1:2, :], cw_ref[2:3, :], cb_ref[...]
        for r0 in range(0, SEQ, CONV_ROWS):
            g0, g1, g2 = _conv_taps(gp_ref, head_ref, r0)
            c = bias + w0 * g2 + w1 * g1 + w2 * g0
            o_ref[r0:r0 + CONV_ROWS, :] = (_gelu(c) * up_ref[r0:r0 + CONV_ROWS, :]).astype(BF16)

    blk = pl.BlockSpec((SEQ, LANE), lambda b, j: (b, j))
    return pl.pallas_call(
        body, name=name, grid=(nb_local, f // LANE),
        in_specs=[blk, blk, pl.BlockSpec((3, LANE), lambda b, j: (0, j)), pl.BlockSpec((1, LANE), lambda b, j: (0, j))],
        out_specs=blk,
        out_shape=SDS((n, f), BF16),
        scratch_shapes=[pltpu.VMEM((PAD + CONV_ROWS, LANE), F32)],
        compiler_params=_params(("parallel", "parallel")),
    )(gp, up, cw, cb)


def conv_gelu_bwd(dgu, gp, up, cw, cb, nb_local, name, ride=None):
    n, f = gp.shape

    def fold(v):
        return jnp.sum(v.reshape(CONV_ROWS // 8, 8, LANE), axis=0)

    def body(dgu_ref, gp_ref, up_ref, cw_ref, cb_ref, dgp_ref, dup_ref, dcw_ref, dcb_ref, head_ref, dc_ref):
        b = pl.program_id(1)
        _fill_head(gp_ref, head_ref)
        dc_ref[SEQ:SEQ + PAD, :] = jnp.zeros((PAD, LANE), F32)
        w0, w1, w2, bias = cw_ref[0:1, :], cw_ref[1:2, :], cw_ref[2:3, :], cb_ref[...]
        sums = [jnp.zeros((8, LANE), F32) for _ in range(4)]
        for r0 in range(0, SEQ, CONV_ROWS):
            rows = slice(r0, r0 + CONV_ROWS)
            g0, g1, g2 = _conv_taps(gp_ref, head_ref, r0)
            gg, dgg = _gelu_and_grad(bias + w0 * g2 + w1 * g1 + w2 * g0)
            dgu = dgu_ref[rows, :].astype(F32)
            dup_ref[rows, :] = (dgu * gg).astype(BF16)
            dc = dgu * up_ref[rows, :] * dgg
            dc_ref[rows, :] = dc
            sums = [sums[0] + fold(dc * g2), sums[1] + fold(dc * g1), sums[2] + fold(dc * g0), sums[3] + fold(dc)]
        for r0 in range(0, SEQ, CONV_ROWS):
            dgp_ref[r0:r0 + CONV_ROWS, :] = (
                w2 * dc_ref[r0:r0 + CONV_ROWS, :] + w1 * dc_ref[r0 + 1:r0 + 1 + CONV_ROWS, :]
                + w0 * dc_ref[r0 + 2:r0 + 2 + CONV_ROWS, :]).astype(BF16)
        dcw = jnp.concatenate([jnp.sum(s, axis=0, keepdims=True) for s in sums[:3]], axis=0)
        dcb = jnp.sum(sums[3], axis=0, keepdims=True)

        @pl.when(b == 0)
        def _():
            dcw_ref[...] = dcw
            dcb_ref[...] = dcb

        @pl.when(b > 0)
        def _():
            dcw_ref[...] += dcw
            dcb_ref[...] += dcb

    blk = pl.BlockSpec((SEQ, LANE), lambda j, b: (b, j))
    return _call(
        body, name=name, grid=(f // LANE, nb_local),
        in_specs=[blk, blk, blk, pl.BlockSpec((3, LANE), lambda j, b: (0, j)), pl.BlockSpec((1, LANE), lambda j, b: (0, j))],
        out_specs=[blk, blk, pl.BlockSpec((3, LANE), lambda j, b: (0, j)), pl.BlockSpec((1, LANE), lambda j, b: (0, j))],
        out_shape=[SDS((n, f), BF16), SDS((n, f), BF16), SDS((3, f), F32), SDS((1, f), F32)],
        scratch_shapes=[pltpu.VMEM((PAD + CONV_ROWS, LANE), F32), pltpu.VMEM((SEQ + PAD, LANE), F32)],
        args=[dgu, gp, up, cw, cb], sem=("parallel", "arbitrary"), ride=ride)


def norm_mid_epilogue(x1, dout, z2, g3, g2):
    n, d = x1.shape

    def fn(dh2, step, x1_ref, dout_ref, z2_ref, g3_ref, g2_ref, dx1_ref, dz2_ref, dg3_ref, dg2_ref):
        dxa, dg3r = _rms_bwd(dh2, x1_ref[...], g3_ref[...])
        dx1 = dout_ref[...] + dxa
        dx1_ref[...] = dx1
        dz2, dg2r = _rms_bwd(dx1, z2_ref[...], g2_ref[...])
        dz2_ref[...] = dz2.astype(BF16)
        _accumulate(dg3_ref, jnp.sum(dg3r, axis=0, keepdims=True), step)
        _accumulate(dg2_ref, jnp.sum(dg2r, axis=0, keepdims=True), step)

    return fn, [x1, dout, z2, g3, g2], [SDS((n, d), F32), SDS((n, d), BF16), SDS((1, d), F32), SDS((1, d), F32)]


def norm_in_epilogue(x, dx1, g1):
    n, d = x.shape

    def fn(dh1, step, x_ref, dx1_ref, g1_ref, dx_ref, dg1_ref):
        dxa, dgr = _rms_bwd(dh1, x_ref[...], g1_ref[...])
        dx_ref[...] = dx1_ref[...] + dxa
        _accumulate(dg1_ref, jnp.sum(dgr, axis=0, keepdims=True), step)

    return fn, [x, dx1, g1], [SDS((n, d), F32), SDS((1, d), F32)]


def cast_bf16(arrays, name):
    def body(*refs):
        for i_ref, o_ref in zip(refs[:len(arrays)], refs[len(arrays):]):
            o_ref[...] = i_ref[...].astype(BF16)

    return pl.pallas_call(body, name=name, out_shape=[SDS(a.shape, BF16) for a in arrays],
                          compiler_params=_params())(*arrays)


def adam_update(parts, w, m, v, name, tr=None):
    s, r, c = parts.shape
    tr = r if tr is None else tr
    bc1 = 1.0 - ADAM_B1 ** ADAM_STEP
    bc2 = 1.0 - ADAM_B2 ** ADAM_STEP

    def body(p_ref, w_ref, m_ref, v_ref, g_ref, d_ref, nm_ref, nv_ref):
        g = p_ref[0].astype(F32)
        for j in range(1, s):
            g = g + p_ref[j].astype(F32)
        nm = ADAM_B1 * m_ref[...] + (1.0 - ADAM_B1) * g
        nv = ADAM_B2 * v_ref[...] + (1.0 - ADAM_B2) * (g * g)
        g_ref[...] = g
        nm_ref[...] = nm
        nv_ref[...] = nv
        d_ref[...] = -ADAM_LR * ((nm / bc1) / (jnp.sqrt(nv / bc2) + ADAM_EPS) + ADAM_WD * w_ref[...])

    blk = pl.BlockSpec((tr, c), lambda i: (i, 0))
    return pl.pallas_call(
        body, name=name, grid=(r // tr,),
        in_specs=[pl.BlockSpec((s, tr, c), lambda i: (0, i, 0)), blk, blk, blk],
        out_specs=[blk] * 4, out_shape=[SDS((r, c), F32)] * 4,
        compiler_params=_params(("parallel",)),
    )(parts, w, m, v)


EARLY_NAMES = ("spatial_w", "norm_mix_post", "norm_ffn_pre", "norm_ffn_post", "conv_b", "ln_v_gain", "ln_v_bias",
               "spatial_b")
LATE_NAMES = ("norm_mix_pre", "rel_bias")
PACK_ROW_ALIGN = 8


def _pack_rows(size):
    rows = -(-size // LANE)
    return -(-rows // PACK_ROW_ALIGN) * PACK_ROW_ALIGN


def _pack(arrays):
    flat = []
    for a in arrays:
        rows = _pack_rows(a.size)
        flat.append(jnp.pad(a.reshape(-1), (0, rows * LANE - a.size)))
    return jnp.concatenate(flat).reshape(-1, LANE)


def _unpack(packed, shapes):
    out, row = [], 0
    for shp in shapes:
        size = int(np.prod(shp))
        out.append(packed[row:row + _pack_rows(size)].reshape(-1)[:size].reshape(shp))
        row += _pack_rows(size)
    return out


def kernel(x, norm_mix_pre, norm_mix_post, norm_ffn_pre, norm_ffn_post, w_in, ln_v_gain, ln_v_bias, spatial_w, spatial_b, rel_bias, w_out, w_gate, w_up, conv_w, conv_b, w_down, loss_target, m_norm_mix_pre, m_norm_mix_post, m_norm_ffn_pre, m_norm_ffn_post, m_w_in, m_ln_v_gain, m_ln_v_bias, m_spatial_w, m_spatial_b, m_rel_bias, m_w_out, m_w_gate, m_w_up, m_conv_w, m_conv_b, m_w_down, v_norm_mix_pre, v_norm_mix_post, v_norm_ffn_pre, v_norm_ffn_post, v_w_in, v_ln_v_gain, v_ln_v_bias, v_spatial_w, v_spatial_b, v_rel_bias, v_w_out, v_w_gate, v_w_up, v_conv_w, v_conv_b, v_w_down):
    given = dict(locals())
    nb_local, seq, d = x.shape
    n = nb_local * seq
    cols = w_in.shape[2]

    def by_columns(g):
        return g.transpose(1, 0, 2).reshape(g.shape[1], N_DEV * g.shape[2])

    def to_blocks(g):
        return g.reshape(g.shape[0], N_DEV, cols).transpose(1, 0, 2)

    xf, target = x.reshape(n, d), loss_target.reshape(n, d)
    ln_g, ln_b = ln_v_gain.reshape(1, A_WIDTH), ln_v_bias.reshape(1, A_WIDTH)
    spatial_bt, rel_bias_t = spatial_b[0].T, rel_bias.T

    s_in, s_out, s_gate, s_up, s_down = cast_bf16([w_in[0], w_out[0], w_gate[0], w_up[0], w_down[0]], "cast_shards")
    g_in, g_cw = exchange([], [s_in, conv_w[0]], "gather_w_in")
    w_in_f, conv_w_f = by_columns(g_in), by_columns(g_cw)

    (h1, proj), _ = norm_mm(xf, norm_mix_pre, [w_in_f], "fwd_norm_in", tn=IN_COLS)
    a = gating_fwd(proj, ln_g, ln_b, spatial_w[0], spatial_bt, "fwd_gating")
    bias = bias_tables(rel_bias_t, "bias_tables").reshape(2 * len(DILATIONS), B_HEADS, ATTN_BLOCK, 2 * ATTN_BLOCK)
    (b_out, lse_tot), (g_out, g_gate, g_up) = attn_fwd(proj, bias, nb_local, "fwd_attn",
                                                       ride=([], [s_out, s_gate, s_up]))
    w_out_f, w_gate_f, w_up_f = g_out.reshape(D_MODEL, D_MODEL), by_columns(g_gate), by_columns(g_up)
    z2, x1 = mm_res_norm([a, b_out], w_out_f, xf, norm_mix_post, "fwd_out_norm")
    (h2, gp, up), (g_down,) = norm_mm(x1, norm_ffn_pre, [w_gate_f, w_up_f], "fwd_norm_ffn", tm=256, tn=D_FF,
                                      ride=([], [s_down]))
    w_down_f = g_down.reshape(D_FF, D_MODEL)
    gu = conv_gelu_fwd(gp, up, conv_w_f, conv_b, nb_local, "fwd_conv_gelu")
    dy, dout, dg4, loss_part = down_loss(gu, w_down_f, x1, norm_ffn_post, target, "fwd_down_loss")

    p_down = mm_tn([gu], [dy], "bwd_dw_down", t1=256, t2=D_MODEL)
    (dgu,), _ = mm_nt([(dy, 0, 0)], [w_down_f], "bwd_dgu", out_dtype=BF16)
    (dgp, dup, p_conv_w, p_conv_b), (r_down,) = conv_gelu_bwd(
        dgu, gp, up, conv_w_f, conv_b, nb_local, "bwd_conv_gelu", ride=([p_down.reshape(N_DEV, cols, D_MODEL)], []))
    p_gate = mm_tn([h2], [dgp], "bwd_dw_gate", t1=D_MODEL, t2=256)
    p_up = mm_tn([h2], [dup], "bwd_dw_up", t1=D_MODEL, t2=256)
    (dx1, dz2, dg3, dg2), _ = mm_nt([(dgp, 0, 0), (dup, 1, 0)], [w_gate_f, w_up_f], "bwd_dh2_norm_mid", tm=256,
                                    epilogue=norm_mid_epilogue(x1, dout, z2, norm_ffn_pre, norm_mix_post))
    p_out = mm_tn([a, b_out], [dz2], "bwd_dw_out", t1=256, t2=D_MODEL)
    (dmix,), _ = mm_nt([(dz2, 0, 0)], [w_out_f], "bwd_dmix")
    duv, p_ws, p_sbt, p_lng, p_lnb = gating_bwd(proj, dmix, ln_g, ln_b, spatial_w[0], spatial_bt, "bwd_gating")
    small = dict(spatial_w=p_ws, norm_mix_post=dg2, norm_ffn_pre=dg3, norm_ffn_post=dg4, conv_b=p_conv_b,
                 ln_v_gain=p_lng, ln_v_bias=p_lnb, spatial_b=p_sbt.T)
    pack_early = _pack([small[k] for k in EARLY_NAMES] + [p_conv_w, loss_part])
    (dq, dk, dv, dbias), (r_gate, r_up, r_out, r_early) = attn_bwd(
        proj, b_out, dmix, lse_tot, bias, nb_local, "bwd_attn",
        ride=([to_blocks(p_gate), to_blocks(p_up), p_out.reshape(N_DEV, D_MODEL // N_DEV, D_MODEL)], [pack_early]))
    p_rel_bias_t = rel_bias_grad(dbias.reshape(len(DILATIONS), B_HEADS, BIAS_SIZE), "bwd_rel_bias")
    p_in = mm_tn([h1], [duv, dq, dk, dv], "bwd_dw_in", t1=D_MODEL, t2=256)
    (grad_x, dg1), (r_in,) = mm_nt(
        [(duv, 0, 0), (dq, 0, Q_OFF), (dk, 0, K_OFF), (dv, 0, V_OFF)], [w_in_f], "bwd_dh1_norm_in",
        epilogue=norm_in_epilogue(xf, dx1, norm_mix_pre), ride=([to_blocks(p_in)], []))
    small.update(norm_mix_pre=dg1, rel_bias=p_rel_bias_t.T)
    (r_late,) = exchange([], [_pack([small[k] for k in LATE_NAMES])], "exchange_late")

    res = {}
    res["w_in"] = adam_update(r_in, w_in[0], m_w_in[0], v_w_in[0], "adam_w_in", tr=256)
    res["w_out"] = adam_update(r_out, w_out[0], m_w_out[0], v_w_out[0], "adam_w_out")
    res["w_gate"] = adam_update(r_gate, w_gate[0], m_w_gate[0], v_w_gate[0], "adam_w_gate", tr=256)
    res["w_up"] = adam_update(r_up, w_up[0], m_w_up[0], v_w_up[0], "adam_w_up", tr=256)
    res["w_down"] = adam_update(r_down, w_down[0], m_w_down[0], v_w_down[0], "adam_w_down", tr=176)

    def adam_packed(received, names, tail, name):
        zeros = [jnp.zeros_like(t) for t in tail]
        packs = [_pack([given[pre + k] for k in names] + zeros) for pre in ("", "m_", "v_")]
        shapes = [given[k].shape for k in names] + [t.shape for t in tail]
        unpacked = [_unpack(p, shapes) for p in adam_update(received, *packs, name)]
        for i, k in enumerate(names):
            res[k] = [u[i] for u in unpacked]
        return unpacked[0][len(names):]

    g_conv_w_full, loss_sum = adam_packed(r_early, EARLY_NAMES, [p_conv_w, loss_part], "adam_small_early")
    adam_packed(r_late, LATE_NAMES, [], "adam_small_late")
    g_conv_w = lax.dynamic_slice_in_dim(g_conv_w_full, _my_index() * cols, cols, axis=1)
    res["conv_w"] = adam_update(g_conv_w[None], conv_w[0], m_conv_w[0], v_conv_w[0], "adam_conv_w")
    loss = loss_sum[0, 0]

    names = ("norm_mix_pre", "norm_mix_post", "norm_ffn_pre", "norm_ffn_post", "w_in", "ln_v_gain", "ln_v_bias",
             "spatial_w", "spatial_b", "rel_bias", "w_out", "w_gate", "w_up", "conv_w", "conv_b", "w_down")
    outs = [loss, grad_x.reshape(x.shape)]
    for t in range(4):
        outs += [res[k][t].reshape(given[k].shape) for k in names]
    return tuple(outs)
```

```python
import functools
import math

import numpy as np
import jax
import jax.numpy as jnp
from jax import lax
from jax.experimental import pallas as pl
from jax.experimental.pallas import tpu as pltpu

F32 = jnp.float32
BF16 = jnp.bfloat16
SDS = jax.ShapeDtypeStruct

D_MODEL = 1024
SEQ = 2048
HEAD_DIM = 64
A_GROUPS = 4
A_WIDTH = A_GROUPS * HEAD_DIM
B_HEADS = 12
B_WIDTH = B_HEADS * HEAD_DIM
HEAD_PAIRS = B_HEADS // 2
CHUNK = 128
ATTN_BLOCK = 128
DILATIONS = (1, 4, 16)
NUM_BUCKETS = 32
MAX_DISTANCE = 2048
D_FF = 2816
IN_COLS = 2 * A_WIDTH + 3 * B_WIDTH
Q_OFF = 2 * A_WIDTH
K_OFF = Q_OFF + B_WIDTH
V_OFF = K_OFF + B_WIDTH
NORM_EPS = 1e-6
NEG_INF = -1e30
N_DEV = 8
LANE = 128

ADAM_LR = 0.001
ADAM_B1 = 0.9
ADAM_B2 = 0.999
ADAM_EPS = 1e-08
ADAM_WD = 0.01
ADAM_STEP = 10

GELU_C0 = math.sqrt(2.0 / math.pi)
GELU_C1 = 0.044715

VMEM_LIMIT = 56 * 1024 * 1024


def _params(sem=None):
    if sem is None:
        return pltpu.CompilerParams(vmem_limit_bytes=VMEM_LIMIT)
    return pltpu.CompilerParams(dimension_semantics=sem, vmem_limit_bytes=VMEM_LIMIT)


def _gelu(x):
    t = jnp.tanh(GELU_C0 * (x + GELU_C1 * x * x * x))
    return 0.5 * x * (1.0 + t)


def _gelu_and_grad(x):
    x2 = x * x
    t = jnp.tanh(GELU_C0 * (x + GELU_C1 * x * x2))
    g = 0.5 * x * (1.0 + t)
    dg = 0.5 * (1.0 + t) + 0.5 * x * (1.0 - t * t) * (GELU_C0 * (1.0 + 3.0 * GELU_C1 * x2))
    return g, dg


def _dot(a, b):
    return jnp.dot(a, b, preferred_element_type=F32)


def _dot_nt(a, b):
    return lax.dot_general(a, b, (((1,), (1,)), ((), ())), preferred_element_type=F32)


def _dot_tn(a, b):
    return lax.dot_general(a, b, (((0,), (0,)), ((), ())), preferred_element_type=F32)


def _rms_bwd(d, xin, g):
    r = lax.rsqrt(jnp.mean(xin * xin, axis=-1, keepdims=True) + NORM_EPS)
    xh = xin * r
    gd = g * d
    dx = r * (gd - xh * jnp.mean(gd * xh, axis=-1, keepdims=True))
    return dx, d * xh


MESH = pl.DeviceIdType.MESH
ANY = pl.BlockSpec(memory_space=pl.ANY)
PEER_MASKS = tuple(range(1, N_DEV))


def _my_index():
    return lax.axis_index("x") * 4 + lax.axis_index("y") * 2 + lax.axis_index("c")


def _peer(mask):
    x, y, c = lax.axis_index("x"), lax.axis_index("y"), lax.axis_index("c")
    px = 1 - x if mask & 4 else x
    py = 1 - y if mask & 2 else y
    pc = 1 - c if mask & 1 else c
    return (px, py, pc), px * 4 + py * 2 + pc


SIBLING = 1
CHIP_MASKS = (2, 4, 6)


class _Exchange:
    def __init__(self, nblocked, in_refs, out_refs, sems):
        send_sems, recv_sems, local_sems = sems
        me = _my_index()
        sibling, _ = _peer(SIBLING)
        self.local, self.first, self.relays, self.relayed_in, self.last_in = [], [], [], [], []
        for a, (in_ref, out_ref) in enumerate(zip(in_refs, out_refs)):
            def copy(src, slot, mask, to):
                return pltpu.make_async_remote_copy(
                    src_ref=src, dst_ref=out_ref.at[slot], send_sem=send_sems.at[a, mask - 1],
                    recv_sem=recv_sems.at[a, mask - 1], device_id=to, device_id_type=MESH)

            if a < nblocked:
                self.local.append(pltpu.make_async_copy(in_ref.at[me], out_ref.at[me], local_sems.at[a]))
                for mask in PEER_MASKS:
                    peer, pidx = _peer(mask)
                    self.first.append(copy(in_ref.at[pidx], me, mask, peer))
                    self.last_in.append(copy(in_ref.at[pidx], pidx, mask, peer))
                continue
            self.local.append(pltpu.make_async_copy(in_ref, out_ref.at[me], local_sems.at[a]))
            for mask in (SIBLING,) + CHIP_MASKS:
                peer, pidx = _peer(mask)
                self.first.append(copy(in_ref, me, mask, peer))
                (self.last_in if mask == SIBLING else self.relayed_in).append(copy(in_ref, pidx, mask, peer))
            for mask in CHIP_MASKS:
                _, origin = _peer(mask)
                _, far = _peer(mask | SIBLING)
                self.relays.append(copy(out_ref.at[origin], origin, mask | SIBLING, sibling))
                self.last_in.append(copy(in_ref, far, mask | SIBLING, sibling))

    def start(self):
        for cp in self.local + self.first:
            cp.start()

    def relay(self):
        for arrived, onward in zip(self.relayed_in, self.relays):
            arrived.wait_recv()
            onward.start()

    def finish(self):
        for cp in self.first + self.relays:
            cp.wait_send()
        for cp in self.last_in:
            cp.wait_recv()
        for cp in self.local:
            cp.wait()


def _exchange_out_shape(blocked, whole):
    return [SDS(b.shape, b.dtype) for b in blocked] + [SDS((N_DEV,) + w.shape, w.dtype) for w in whole]


def _exchange_sems(n):
    return [pltpu.SemaphoreType.DMA((n, N_DEV - 1)), pltpu.SemaphoreType.DMA((n, N_DEV - 1)),
            pltpu.SemaphoreType.DMA((n,))]


def exchange(blocked, whole, name):
    nb, n = len(blocked), len(blocked) + len(whole)

    def body(*refs):
        ex = _Exchange(nb, refs[:n], refs[n:2 * n], refs[2 * n:])
        ex.start()
        ex.relay()
        ex.finish()

    return pl.pallas_call(
        body, name=name, in_specs=[ANY] * n, out_specs=[ANY] * n, out_shape=_exchange_out_shape(blocked, whole),
        scratch_shapes=_exchange_sems(n),
    )(*blocked, *whole)


def _call(body, *, name, grid, in_specs, out_specs, out_shape, args, scratch_shapes=(), sem=None, ride=None):
    out_shape, out_specs, scratch_shapes = list(out_shape), list(out_specs), list(scratch_shapes)
    if ride is None:
        outs = pl.pallas_call(body, name=name, grid=grid, in_specs=list(in_specs), out_specs=out_specs,
                              out_shape=out_shape, scratch_shapes=scratch_shapes,
                              compiler_params=_params(sem))(*args)
        return list(outs), []
    blocked, whole = ride
    cargs = list(blocked) + list(whole)
    nb, nc = len(blocked), len(cargs)
    n_in, n_out, n_scr = len(args), len(out_shape), len(scratch_shapes)
    steps = math.prod(grid)
    assert steps >= 3, grid

    def riding(*refs):
        ins, refs = refs[:n_in], refs[n_in:]
        cins, refs = refs[:nc], refs[nc:]
        outs, refs = refs[:n_out], refs[n_out:]
        couts, refs = refs[:nc], refs[nc:]
        scr, sems = refs[:n_scr], refs[n_scr:]
        step = functools.reduce(lambda acc, k: acc * grid[k] + pl.program_id(k), range(len(grid)), 0)

        @pl.when(step == 0)
        def _():
            _Exchange(nb, cins, couts, sems).start()

        @pl.when(step == steps // 2)
        def _():
            _Exchange(nb, cins, couts, sems).relay()

        body(*ins, *outs, *scr)

        @pl.when(step == steps - 1)
        def _():
            _Exchange(nb, cins, couts, sems).finish()

    res = pl.pallas_call(
        riding, name=name, grid=grid, in_specs=list(in_specs) + [ANY] * nc, out_specs=out_specs + [ANY] * nc,
        out_shape=out_shape + _exchange_out_shape(blocked, whole),
        scratch_shapes=scratch_shapes + _exchange_sems(nc),
        compiler_params=_params(("arbitrary",) * len(grid)))(*args, *cargs)
    return list(res[:n_out]), list(res[n_out:])


def norm_mm(x, g, ws, name, tm=512, tn=1408, ride=None):
    n, d = x.shape
    f = ws[0].shape[1]
    nw = len(ws)

    def body(x_ref, g_ref, *refs):
        w_refs = refs[:nw]
        h_ref = refs[nw]
        o_refs = refs[nw + 1:]

        @pl.when(pl.program_id(1) == 0)
        def _():
            xv = x_ref[...]
            r = lax.rsqrt(jnp.mean(xv * xv, axis=-1, keepdims=True) + NORM_EPS)
            h_ref[...] = (xv * r * g_ref[...]).astype(BF16)

        h = h_ref[...]
        for w_ref, o_ref in zip(w_refs, o_refs):
            o_ref[...] = _dot(h, w_ref[...])

    return _call(
        body, name=name, grid=(n // tm, f // tn),
        in_specs=[pl.BlockSpec((tm, d), lambda i, j: (i, 0)), pl.BlockSpec((1, d), lambda i, j: (0, 0))]
        + [pl.BlockSpec((d, tn), lambda i, j: (0, j)) for _ in ws],
        out_specs=[pl.BlockSpec((tm, d), lambda i, j: (i, 0))]
        + [pl.BlockSpec((tm, tn), lambda i, j: (i, j)) for _ in ws],
        out_shape=[SDS((n, d), BF16)] + [SDS((n, f), F32) for _ in ws],
        args=[x, g, *ws], sem=("parallel", "arbitrary"), ride=ride)


def _lane_concat(refs):
    vals = [r[...].astype(BF16) for r in refs]
    return vals[0] if len(vals) == 1 else jnp.concatenate(vals, axis=1)


def mm_res_norm(a_list, w, res, g, name, tm=512):
    n = a_list[0].shape[0]
    k, d = w.shape
    na = len(a_list)

    def body(*refs):
        w_ref, res_ref, g_ref, y_ref, o_ref = refs[na:]
        y = _dot(_lane_concat(refs[:na]), w_ref[...])
        r = lax.rsqrt(jnp.mean(y * y, axis=-1, keepdims=True) + NORM_EPS)
        y_ref[...] = y
        o_ref[...] = res_ref[...] + y * r * g_ref[...]

    return pl.pallas_call(
        body, name=name, grid=(n // tm,),
        in_specs=[pl.BlockSpec((tm, a.shape[1]), lambda i: (i, 0)) for a in a_list]
        + [pl.BlockSpec((k, d), lambda i: (0, 0)),
           pl.BlockSpec((tm, d), lambda i: (i, 0)), pl.BlockSpec((1, d), lambda i: (0, 0))],
        out_specs=[pl.BlockSpec((tm, d), lambda i: (i, 0)), pl.BlockSpec((tm, d), lambda i: (i, 0))],
        out_shape=[SDS((n, d), F32), SDS((n, d), F32)],
        compiler_params=_params(("parallel",)),
    )(*a_list, w, res, g)


def down_loss(a, w, res, g, target, name, tm=256):
    n, k = a.shape
    d = w.shape[1]
    inv_d = 1.0 / d

    def body(a_ref, w_ref, res_ref, g_ref, t_ref, dy_ref, dout_ref, dg_ref, loss_ref):
        i = pl.program_id(0)
        y = _dot(a_ref[...], w_ref[...])
        gv = g_ref[...]
        r = lax.rsqrt(jnp.mean(y * y, axis=-1, keepdims=True) + NORM_EPS)
        yh = y * r
        e = res_ref[...] + yh * gv - t_ref[...]
        part = 0.5 * inv_d * jnp.sum(jnp.sum(e * e, axis=-1, keepdims=True), axis=0, keepdims=True)
        dout = e * inv_d
        dout_ref[...] = dout
        gd = gv * dout
        dy_ref[...] = (r * (gd - yh * jnp.mean(gd * yh, axis=-1, keepdims=True))).astype(BF16)
        dgp = jnp.sum(dout * yh, axis=0, keepdims=True)
        lane0 = lax.broadcasted_iota(jnp.int32, (1, LANE), 1) == 0
        lp = jnp.where(lane0, part, 0.0)

        @pl.when(i == 0)
        def _():
            dg_ref[...] = dgp
            loss_ref[...] = lp

        @pl.when(i > 0)
        def _():
            dg_ref[...] += dgp
            loss_ref[...] += lp

    return pl.pallas_call(
        body, name=name, grid=(n // tm,),
        in_specs=[pl.BlockSpec((tm, k), lambda i: (i, 0)), pl.BlockSpec((k, d), lambda i: (0, 0)),
                  pl.BlockSpec((tm, d), lambda i: (i, 0)), pl.BlockSpec((1, d), lambda i: (0, 0)),
                  pl.BlockSpec((tm, d), lambda i: (i, 0))],
        out_specs=[pl.BlockSpec((tm, d), lambda i: (i, 0)), pl.BlockSpec((tm, d), lambda i: (i, 0)),
                   pl.BlockSpec((1, d), lambda i: (0, 0)), pl.BlockSpec((1, LANE), lambda i: (0, 0))],
        out_shape=[SDS((n, d), BF16), SDS((n, d), F32), SDS((1, d), F32), SDS((1, LANE), F32)],
        compiler_params=_params(("arbitrary",)),
    )(a, w, res, g, target)


def _accumulate(ref, val, step):
    @pl.when(step == 0)
    def _():
        ref[...] = val

    @pl.when(step > 0)
    def _():
        ref[...] += val


def mm_nt(terms, ws, name, tm=512, out_dtype=F32, ride=None, epilogue=None):
    n = terms[0][0].shape[0]
    r = ws[0].shape[0]
    na = len(terms)
    meta = [(widx, off, a.shape[1]) for a, widx, off in terms]
    fn, extras, out_shape = epilogue if epilogue else (None, [], [SDS((n, r), out_dtype)])
    n_fixed = na + len(ws)

    def body(*refs):
        a_refs = refs[:na]
        w_refs = refs[na:n_fixed]
        acc = None
        for a_ref, (widx, off, k) in zip(a_refs, meta):
            p = _dot_nt(a_ref[...].astype(BF16), w_refs[widx][:, off:off + k])
            acc = p if acc is None else acc + p
        if fn is None:
            refs[-1][...] = acc.astype(out_dtype)
        else:
            fn(acc, pl.program_id(0), *refs[n_fixed:])

    def spec(a):
        if a.shape[0] == 1:
            return pl.BlockSpec(a.shape, lambda i: (0, 0))
        return pl.BlockSpec((tm, a.shape[1]), lambda i: (i, 0))

    return _call(
        body, name=name, grid=(n // tm,),
        in_specs=[spec(a) for a, _, _ in terms] + [pl.BlockSpec(w.shape, lambda i: (0, 0)) for w in ws]
        + [spec(e) for e in extras],
        out_specs=[spec(o) for o in out_shape], out_shape=out_shape,
        args=[a for a, _, _ in terms] + list(ws) + list(extras),
        sem=("parallel",) if fn is None else ("arbitrary",), ride=ride)


def _piece_blocks(pieces, tile):
    out, first = [], 0
    for p in pieces:
        nblk, rem = divmod(p.shape[1], tile)
        assert rem == 0, (p.shape, tile)
        out.append((first, nblk))
        first += nblk
    return out, first


def mm_tn(lhs_list, rhs_list, name, t1, t2, out_dtype=BF16):
    n = lhs_list[0].shape[0]
    lblocks, nbl = _piece_blocks(lhs_list, t1)
    rblocks, nbr = _piece_blocks(rhs_list, t2)
    nl = len(lhs_list)

    def body(*refs):
        l_refs, r_refs, o_ref = refs[:nl], refs[nl:-1], refs[-1]
        i, j = pl.program_id(0), pl.program_id(1)
        for l_ref, (ls, ln) in zip(l_refs, lblocks):
            for r_ref, (rs, rn) in zip(r_refs, rblocks):
                @pl.when((i >= ls) & (i < ls + ln) & (j >= rs) & (j < rs + rn))
                def _(l_ref=l_ref, r_ref=r_ref):
                    o_ref[...] = _dot_tn(l_ref[...].astype(BF16), r_ref[...].astype(BF16)).astype(out_dtype)

    def piece_spec(tile, axis, first, nblk):
        def index(i, j):
            return 0, jnp.clip((i, j)[axis] - first, 0, nblk - 1)
        return pl.BlockSpec((n, tile), index)

    return pl.pallas_call(
        body, name=name, grid=(nbl, nbr),
        in_specs=[piece_spec(t1, 0, *b) for b in lblocks] + [piece_spec(t2, 1, *b) for b in rblocks],
        out_specs=pl.BlockSpec((t1, t2), lambda i, j: (i, j)),
        out_shape=SDS((nbl * t1, nbr * t2), out_dtype),
        compiler_params=_params(("parallel", "arbitrary")),
    )(*lhs_list, *rhs_list)


GATE_ROWS = 512


def _tril_mask():
    row = lax.broadcasted_iota(jnp.int32, (CHUNK, CHUNK), 0)
    col = lax.broadcasted_iota(jnp.int32, (CHUNK, CHUNK), 1)
    return row >= col


def _group_of(shape, axis):
    return lax.broadcasted_iota(jnp.int32, shape, axis) // HEAD_DIM


def _group_mean_matrix():
    same = _group_of((A_WIDTH, A_WIDTH), 0) == _group_of((A_WIDTH, A_WIDTH), 1)
    return jnp.where(same, 1.0 / HEAD_DIM, 0.0).astype(F32)


def _dot_f32(a, b):
    return jnp.dot(a, b, preferred_element_type=F32, precision=lax.Precision.HIGHEST)


def _by_group(parts, lane_group):
    out = parts[A_GROUPS - 1]
    for g in range(A_GROUPS - 2, -1, -1):
        out = jnp.where(lane_group == g, parts[g], out)
    return out


def _group_norm(gv, gmean):
    xc = gv - _dot_f32(gv, gmean)
    rstd = lax.rsqrt(_dot_f32(xc * xc, gmean) + NORM_EPS)
    return xc * rstd, rstd


def gating_fwd(proj, lng, lnb, ws, sbt, name):
    n = proj.shape[0]

    def body(u_ref, v_ref, lng_ref, lnb_ref, ws_ref, sbt_ref, a_ref):
        tril = _tril_mask()
        lane_group = _group_of((CHUNK, A_WIDTH), 1)
        gmean = _group_mean_matrix()
        wts = [jnp.where(tril, ws_ref[g], 0.0).astype(BF16) for g in range(A_GROUPS)]
        sb = _by_group([sbt_ref[:, g:g + 1] for g in range(A_GROUPS)], lane_group)

        def chunk(c, carry):
            rows = pl.ds(pl.multiple_of(c * CHUNK, CHUNK), CHUNK)
            vhat, _ = _group_norm(_gelu(v_ref[rows, :]), gmean)
            vn = (vhat * lng_ref[...] + lnb_ref[...]).astype(BF16)
            z = _by_group([_dot(wt, vn) for wt in wts], lane_group) + sb
            a_ref[rows, :] = _gelu(u_ref[rows, :]) * z
            return carry

        lax.fori_loop(0, GATE_ROWS // CHUNK, chunk, 0)

    return pl.pallas_call(
        body, name=name, grid=(n // GATE_ROWS,),
        in_specs=[pl.BlockSpec((GATE_ROWS, A_WIDTH), lambda i: (i, 0)),
                  pl.BlockSpec((GATE_ROWS, A_WIDTH), lambda i: (i, 1)),
                  pl.BlockSpec((1, A_WIDTH), lambda i: (0, 0)), pl.BlockSpec((1, A_WIDTH), lambda i: (0, 0)),
                  pl.BlockSpec((A_GROUPS, CHUNK, CHUNK), lambda i: (0, 0, 0)),
                  pl.BlockSpec((CHUNK, A_GROUPS), lambda i: (0, 0))],
        out_specs=pl.BlockSpec((GATE_ROWS, A_WIDTH), lambda i: (i, 0)),
        out_shape=SDS((n, A_WIDTH), F32),
        compiler_params=_params(("parallel",)),
    )(proj, proj, lng, lnb, ws, sbt)


def gating_bwd(proj, dmix, lng, lnb, ws, sbt, name):
    n = proj.shape[0]

    def body(u_ref, v_ref, da_ref, lng_ref, lnb_ref, ws_ref, sbt_ref,
             duv_ref, dws_ref, dsbt_ref, dlng_ref, dlnb_ref):
        @pl.when(pl.program_id(0) == 0)
        def _():
            dws_ref[...] = jnp.zeros_like(dws_ref)
            dsbt_ref[...] = jnp.zeros_like(dsbt_ref)
            dlng_ref[...] = jnp.zeros_like(dlng_ref)
            dlnb_ref[...] = jnp.zeros_like(dlnb_ref)

        tril = _tril_mask()
        lane_group = _group_of((CHUNK, A_WIDTH), 1)
        gmean = _group_mean_matrix()
        gsum = (_group_of((A_WIDTH, LANE), 0) == lax.broadcasted_iota(jnp.int32, (A_WIDTH, LANE), 1)).astype(F32)
        wts = [jnp.where(tril, ws_ref[g], 0.0) for g in range(A_GROUPS)]
        wts_b = [w.astype(BF16) for w in wts]
        wts_t = [w.T.astype(BF16) for w in wts]
        sb = _by_group([sbt_ref[:, g:g + 1] for g in range(A_GROUPS)], lane_group)
        lg = lng_ref[...]

        def chunk(c, carry):
            rows = pl.ds(pl.multiple_of(c * CHUNK, CHUNK), CHUNK)
            gu, dgu_dx = _gelu_and_grad(u_ref[rows, :])
            gv, dgv_dx = _gelu_and_grad(v_ref[rows, :])
            vhat, rstd = _group_norm(gv, gmean)
            vn = (vhat * lg + lnb_ref[...]).astype(BF16)
            z = _by_group([_dot(wt, vn) for wt in wts_b], lane_group) + sb
            da = da_ref[rows, :]
            dz = da * gu
            dzb = dz.astype(BF16)
            duv_ref[rows, 0:A_WIDTH] = (da * z * dgu_dx).astype(BF16)
            dsbt_ref[...] += _dot_f32(dz, gsum)[:, 0:A_GROUPS]
            for g in range(A_GROUPS):
                dz_g = jnp.where(lane_group == g, dzb, jnp.zeros_like(dzb))
                dws_ref[g] += jnp.where(tril, _dot_nt(dz_g, vn), 0.0)
            dvn = _by_group([_dot(wt, dzb) for wt in wts_t], lane_group)
            dlng_ref[...] += jnp.sum(dvn * vhat, axis=0, keepdims=True)
            dlnb_ref[...] += jnp.sum(dvn, axis=0, keepdims=True)
            dvh = dvn * lg
            dgv = rstd * (dvh - _dot_f32(dvh, gmean) - vhat * _dot_f32(dvh * vhat, gmean))
            duv_ref[rows, A_WIDTH:2 * A_WIDTH] = (dgv * dgv_dx).astype(BF16)
            return carry

        lax.fori_loop(0, GATE_ROWS // CHUNK, chunk, 0)

    return pl.pallas_call(
        body, name=name, grid=(n // GATE_ROWS,),
        in_specs=[pl.BlockSpec((GATE_ROWS, A_WIDTH), lambda i: (i, 0)),
                  pl.BlockSpec((GATE_ROWS, A_WIDTH), lambda i: (i, 1)),
                  pl.BlockSpec((GATE_ROWS, A_WIDTH), lambda i: (i, 0)),
                  pl.BlockSpec((1, A_WIDTH), lambda i: (0, 0)), pl.BlockSpec((1, A_WIDTH), lambda i: (0, 0)),
                  pl.BlockSpec((A_GROUPS, CHUNK, CHUNK), lambda i: (0, 0, 0)),
                  pl.BlockSpec((CHUNK, A_GROUPS), lambda i: (0, 0))],
        out_specs=[pl.BlockSpec((GATE_ROWS, 2 * A_WIDTH), lambda i: (i, 0)),
                   pl.BlockSpec((A_GROUPS, CHUNK, CHUNK), lambda i: (0, 0, 0)),
                   pl.BlockSpec((CHUNK, A_GROUPS), lambda i: (0, 0)),
                   pl.BlockSpec((1, A_WIDTH), lambda i: (0, 0)), pl.BlockSpec((1, A_WIDTH), lambda i: (0, 0))],
        out_shape=[SDS((n, 2 * A_WIDTH), BF16), SDS((A_GROUPS, CHUNK, CHUNK), F32), SDS((CHUNK, A_GROUPS), F32),
                   SDS((1, A_WIDTH), F32), SDS((1, A_WIDTH), F32)],
        compiler_params=_params(("arbitrary",)),
    )(proj, proj, dmix, lng, lnb, ws, sbt)


def _t5_bucket_np(dist):
    max_exact = NUM_BUCKETS // 2
    dd = np.maximum(dist, 1).astype(np.float64)
    large = max_exact + np.log(dd / max_exact) / math.log(MAX_DISTANCE / max_exact) * (NUM_BUCKETS - max_exact)
    large = np.minimum(large.astype(np.int64), NUM_BUCKETS - 1)
    return np.where(dist < max_exact, dist, large)


def _bucket_tables(with_first):
    i = np.arange(ATTN_BLOCK)[:, None]
    j = np.arange(2 * ATTN_BLOCK)[None, :]
    rel = ATTN_BLOCK + i - j
    band = (rel >= 0) & (rel <= ATTN_BLOCK)
    tabs = []
    for own_only in (False, True) if with_first else (False,):
        for dil in DILATIONS:
            b = _t5_bucket_np(np.maximum(rel, 0) * dil)
            tabs.append(np.where(band & (j >= ATTN_BLOCK) if own_only else band, b, -1).reshape(1, -1))
    return np.stack(tabs).astype(np.float32)


BIAS_SIZE = ATTN_BLOCK * 2 * ATTN_BLOCK


def bias_tables(rel_bias_t, name):
    idx = jnp.asarray(_bucket_tables(True))
    ntab = idx.shape[0]

    def body(rb_ref, idx_ref, o_ref):
        iv = idx_ref[0]
        bk = lax.broadcasted_iota(jnp.int32, (NUM_BUCKETS, BIAS_SIZE), 0).astype(F32)
        onehot = (bk == iv).astype(F32)
        t = jnp.dot(rb_ref[...], onehot, preferred_element_type=F32, precision=lax.Precision.HIGHEST)
        o_ref[0] = jnp.where(iv < 0.0, NEG_INF, t)

    return pl.pallas_call(
        body, name=name, grid=(ntab,),
        in_specs=[pl.BlockSpec((B_HEADS, NUM_BUCKETS), lambda d: (0, 0)),
                  pl.BlockSpec((1, 1, BIAS_SIZE), lambda d: (d, 0, 0))],
        out_specs=pl.BlockSpec((1, B_HEADS, BIAS_SIZE), lambda d: (d, 0, 0)),
        out_shape=SDS((ntab, B_HEADS, BIAS_SIZE), F32),
        compiler_params=_params(("parallel",)),
    )(rel_bias_t, idx)


def rel_bias_grad(dbias, name):
    idx = jnp.asarray(_bucket_tables(False))

    def body(db_ref, idx_ref, o_ref):
        d = pl.program_id(0)
        iv = idx_ref[0]
        bk = lax.broadcasted_iota(jnp.int32, (NUM_BUCKETS, BIAS_SIZE), 0).astype(F32)
        onehot = (bk == iv).astype(F32)
        part = lax.dot_general(db_ref[0], onehot, (((1,), (1,)), ((), ())),
                               preferred_element_type=F32, precision=lax.Precision.HIGHEST)

        @pl.when(d == 0)
        def _():
            o_ref[...] = part

        @pl.when(d > 0)
        def _():
            o_ref[...] += part

    return pl.pallas_call(
        body, name=name, grid=(len(DILATIONS),),
        in_specs=[pl.BlockSpec((1, B_HEADS, BIAS_SIZE), lambda d: (d, 0, 0)),
                  pl.BlockSpec((1, 1, BIAS_SIZE), lambda d: (d, 0, 0))],
        out_specs=pl.BlockSpec((B_HEADS, NUM_BUCKETS), lambda d: (0, 0)),
        out_shape=SDS((B_HEADS, NUM_BUCKETS), F32),
        compiler_params=_params(("arbitrary",)),
    )(dbias, idx)


def _attn_scores(q, kk, bias):
    return _dot_nt(q, kk) * (1.0 / math.sqrt(HEAD_DIM)) + bias


def _head0_lanes():
    return lax.broadcasted_iota(jnp.int32, (ATTN_BLOCK, LANE), 1) < HEAD_DIM


def _one_head(x2, head0, hh):
    return jnp.where(head0 if hh == 0 else jnp.logical_not(head0), x2, 0.0).astype(BF16)


def _rows(start, size, dil):
    return pl.ds(start, size) if dil == 1 else pl.ds(start, size, stride=dil)


QUAD = 4
QUAD_ROWS = SEQ // QUAD


def _deinterleave(src_ref, dst_ref):
    for r in range(QUAD):
        for c in range(QUAD_ROWS // ATTN_BLOCK):
            dst_ref[r, c * ATTN_BLOCK:(c + 1) * ATTN_BLOCK, :] = src_ref[
                pl.ds(r + c * QUAD * ATTN_BLOCK, ATTN_BLOCK, stride=QUAD), :]


def _deinterleave_again(src_ref, dst_ref):
    for r in range(QUAD):
        for s in range(QUAD):
            dst_ref[r + QUAD * s] = src_ref[r, pl.ds(s, ATTN_BLOCK, stride=QUAD), :]


def _interleave_back(src_ref, dst_ref, slot0, accumulate=False):
    for r in range(QUAD):
        for s in range(QUAD):
            rows = pl.ds(s, ATTN_BLOCK, stride=QUAD)
            if accumulate:
                dst_ref[slot0 + r, rows, :] += src_ref[r + QUAD * s]
            else:
                dst_ref[slot0 + r, rows, :] = src_ref[r + QUAD * s]


def _quad_tiles():
    return [(r, pl.ds(r + c * QUAD * ATTN_BLOCK, ATTN_BLOCK, stride=QUAD), slice(c * ATTN_BLOCK, (c + 1) * ATTN_BLOCK))
            for r in range(QUAD) for c in range(QUAD_ROWS // ATTN_BLOCK)]


def _attn_schedule(op):
    def d16(i, carry):
        for t in range(2 * QUAD):
            op(2, 2 * QUAD * i + t, 0, 1, True)
        return carry

    lax.fori_loop(0, QUAD // 2, d16, 0)

    def d4(i, carry):
        for u in range(2):
            for nq in range(QUAD_ROWS // ATTN_BLOCK):
                op(1, 2 * i + u, nq * ATTN_BLOCK, 1, nq == 0)
        return carry

    lax.fori_loop(0, QUAD // 2, d4, 0)
    op(0, None, 0, 1, True)
    per_pass = 5

    def d1(j, carry):
        for t in range(per_pass):
            op(0, None, pl.multiple_of((1 + per_pass * j + t) * ATTN_BLOCK, ATTN_BLOCK), 1, False)
        return carry

    lax.fori_loop(0, (SEQ // ATTN_BLOCK - 1) // per_pass, d1, 0)


def _keys(src, krows, first):
    kb = src[krows, :].astype(BF16)
    return jnp.concatenate([kb, kb], axis=0) if first else kb


def _table(seg, first):
    return len(DILATIONS) + seg if first else seg


def _kv_rows(start, dil, first):
    if first:
        return _rows(start, ATTN_BLOCK, dil)
    return _rows(start - ATTN_BLOCK * dil, 2 * ATTN_BLOCK, dil)


MERGE_ROWS = 256


def attn_fwd(proj, bias, nb_local, name, ride=None):
    n = proj.shape[0]
    nseg = len(DILATIONS)

    def body(q_ref, k_ref, v_ref, b_ref, o_ref, lse_ref, q4_ref, k4_ref, v4_ref, os0_ref, ls0_ref, os4_ref, ls4_ref,
             q16_ref, k16_ref, v16_ref, os16_ref, ls16_ref):
        for src, mid, dst in ((q_ref, q4_ref, q16_ref), (k_ref, k4_ref, k16_ref), (v_ref, v4_ref, v16_ref)):
            _deinterleave(src, mid)
            _deinterleave_again(mid, dst)

        def op(seg, r, start, stride, first):
            qrows = _rows(start, ATTN_BLOCK, stride)
            krows = _kv_rows(start, stride, first)
            if seg == 0:
                q_src, k_src, v_src, o_dst, l_dst = q_ref, k_ref, v_ref, os0_ref, ls0_ref
            elif seg == 1:
                q_src, k_src, v_src = q4_ref.at[r], k4_ref.at[r], v4_ref.at[r]
                o_dst, l_dst = os4_ref.at[r], ls4_ref.at[r]
            else:
                q_src, k_src, v_src = q16_ref.at[r], k16_ref.at[r], v16_ref.at[r]
                o_dst, l_dst = os16_ref.at[r], ls16_ref.at[r]
            q2, kb, vb = q_src[qrows, :], _keys(k_src, krows, first), _keys(v_src, krows, first)
            head0 = _head0_lanes()
            outs, lses = [], []
            for hh in range(2):
                s = _attn_scores(_one_head(q2, head0, hh), kb, b_ref[_table(seg, first), hh])
                m = jnp.max(s, axis=-1, keepdims=True)
                p = jnp.exp(s - m)
                l = jnp.sum(p, axis=-1, keepdims=True)
                outs.append(_dot(p.astype(BF16), vb) / l)
                lses.append(jnp.broadcast_to(m + jnp.log(l), (ATTN_BLOCK, LANE)))
            o_dst[qrows, :] = jnp.where(head0, outs[0], outs[1])
            l_dst[qrows, :] = jnp.where(head0, lses[0], lses[1])

        _attn_schedule(op)
        _interleave_back(os16_ref, os4_ref, QUAD)
        _interleave_back(ls16_ref, ls4_ref, QUAD)

        for r, nat, quad in _quad_tiles():
            ls = [ls0_ref[nat, :], ls4_ref[r, quad, :], ls4_ref[QUAD + r, quad, :]]
            m = functools.reduce(jnp.maximum, ls)
            ws = [jnp.exp(l - m) for l in ls]
            den = ws[0] + ws[1] + ws[2]
            num = ws[0] * os0_ref[nat, :] + ws[1] * os4_ref[r, quad, :] + ws[2] * os4_ref[QUAD + r, quad, :]
            o_ref[nat, :] = num / den
            lse_ref[nat, :] = m + jnp.log(den)

    def in_spec(off):
        return pl.BlockSpec((SEQ, LANE), lambda b, p: (b, off // LANE + p))

    out_spec = pl.BlockSpec((SEQ, LANE), lambda b, p: (b, p))
    return _call(
        body, name=name, grid=(nb_local, HEAD_PAIRS),
        in_specs=[in_spec(Q_OFF), in_spec(K_OFF), in_spec(V_OFF),
                  pl.BlockSpec((2 * nseg, 2, ATTN_BLOCK, 2 * ATTN_BLOCK), lambda b, p: (0, p, 0, 0))],
        out_specs=[out_spec, out_spec],
        out_shape=[SDS((n, B_WIDTH), F32), SDS((n, B_WIDTH), F32)],
        scratch_shapes=[pltpu.VMEM((QUAD, QUAD_ROWS, LANE), F32)] * 3 + [pltpu.VMEM((SEQ, LANE), F32)] * 2
        + [pltpu.VMEM((2 * QUAD, QUAD_ROWS, LANE), F32)] * 2 + [pltpu.VMEM((QUAD * QUAD, ATTN_BLOCK, LANE), F32)] * 5,
        args=[proj, proj, proj, bias], sem=("parallel", "arbitrary"), ride=ride)


def attn_bwd(proj, b_out, dmix, lse_tot, bias, nb_local, name, ride=None):
    n = proj.shape[0]
    nseg = len(DILATIONS)
    a_blocks = A_WIDTH // LANE
    scale = 1.0 / math.sqrt(HEAD_DIM)

    def body(q_ref, k_ref, v_ref, o_ref, do_ref, lse_ref, b_ref, dq_ref, dk_ref, dv_ref, db_ref,
             dqs_ref, delta_ref, dka_ref, dva_ref, q4_ref, k4_ref, v4_ref, do4_ref, lse4_ref, delta4_ref,
             dqs4_ref, dk4_ref, dv4_ref, q16_ref, k16_ref, v16_ref, do16_ref, lse16_ref, delta16_ref,
             dqs16_ref, dk16_ref, dv16_ref):
        @pl.when(pl.program_id(1) == 0)
        def _():
            db_ref[...] = jnp.zeros_like(db_ref)

        for acc_ref in (dka_ref, dva_ref, dk4_ref, dv4_ref):
            acc_ref[...] = jnp.zeros_like(acc_ref)
        quads = (q4_ref, k4_ref, v4_ref, do4_ref, lse4_ref, delta4_ref)
        hexes = (q16_ref, k16_ref, v16_ref, do16_ref, lse16_ref, delta16_ref)

        def row_dots(i, carry):
            rows = pl.ds(pl.multiple_of(i * ATTN_BLOCK, ATTN_BLOCK), ATTN_BLOCK)
            head0 = _head0_lanes()
            prod = do_ref[rows, :] * o_ref[rows, :]
            d0 = jnp.sum(jnp.where(head0, prod, 0.0), axis=-1, keepdims=True)
            d1 = jnp.sum(jnp.where(head0, 0.0, prod), axis=-1, keepdims=True)
            delta_ref[rows, :] = jnp.where(head0, d0, d1)
            return carry

        lax.fori_loop(0, SEQ // ATTN_BLOCK, row_dots, 0)
        for src, mid, dst in zip((q_ref, k_ref, v_ref, do_ref, lse_ref, delta_ref), quads, hexes):
            _deinterleave(src, mid)
            _deinterleave_again(mid, dst)

        def op(seg, r, start, stride, first):
            qrows = _rows(start, ATTN_BLOCK, stride)
            krows = _kv_rows(start, stride, first)
            if seg == 0:
                srcs = (q_ref, k_ref, v_ref, do_ref, lse_ref, delta_ref)
                dq_dst, dk_dst, dv_dst = dqs_ref, dka_ref, dva_ref
            elif seg == 1:
                srcs = tuple(x.at[r] for x in quads)
                dq_dst, dk_dst, dv_dst = dqs4_ref.at[r], dk4_ref.at[r], dv4_ref.at[r]
            else:
                srcs = tuple(x.at[r] for x in hexes)
                dq_dst, dk_dst, dv_dst = dqs16_ref.at[r], dk16_ref.at[r], dv16_ref.at[r]
            q_src, k_src, v_src, do_src, lse_src, delta_src = srcs
            q2, kb, vb = q_src[qrows, :], _keys(k_src, krows, first), _keys(v_src, krows, first)
            do2, lse2, delta2 = do_src[qrows, :], lse_src[qrows, :], delta_src[qrows, :]
            head0 = _head0_lanes()
            dqs, dk, dv = [], None, None
            for hh in range(2):
                col = slice(hh * HEAD_DIM, hh * HEAD_DIM + 1)
                q, dob = _one_head(q2, head0, hh), _one_head(do2, head0, hh)
                p = jnp.exp(_attn_scores(q, kb, b_ref[_table(seg, first), hh]) - lse2[:, col])
                dvh = _dot_tn(p.astype(BF16), dob)
                ds = p * (_dot_nt(dob, vb) - delta2[:, col])
                if first:
                    db_ref[seg, hh, :, ATTN_BLOCK:] += ds[:, ATTN_BLOCK:]
                else:
                    db_ref[seg, hh] += ds
                dsb = ds.astype(BF16)
                dqs.append(_dot(dsb, kb))
                dkh = _dot_tn(dsb, q)
                dk = dkh if dk is None else dk + dkh
                dv = dvh if dv is None else dv + dvh
            if first:
                dk, dv = dk[ATTN_BLOCK:], dv[ATTN_BLOCK:]
            dq_dst[qrows, :] = jnp.where(head0, dqs[0], dqs[1]) * scale
            if seg == 2:
                dk_dst[krows, :] = dk * scale
                dv_dst[krows, :] = dv
            else:
                dk_dst[krows, :] += dk * scale
                dv_dst[krows, :] += dv

        _attn_schedule(op)
        _interleave_back(dqs16_ref, dqs4_ref, QUAD)
        _interleave_back(dk16_ref, dk4_ref, 0, accumulate=True)
        _interleave_back(dv16_ref, dv4_ref, 0, accumulate=True)

        for r, nat, quad in _quad_tiles():
            dqs_ref[nat, :] += dqs4_ref[r, quad, :] + dqs4_ref[QUAD + r, quad, :]
            dka_ref[nat, :] += dk4_ref[r, quad, :]
            dva_ref[nat, :] += dv4_ref[r, quad, :]

        def merge(i, carry):
            rows = pl.ds(pl.multiple_of(i * MERGE_ROWS, MERGE_ROWS), MERGE_ROWS)
            dq_ref[rows, :] = dqs_ref[rows, :].astype(BF16)
            dk_ref[rows, :] = dka_ref[rows, :].astype(BF16)
            dv_ref[rows, :] = dva_ref[rows, :].astype(BF16)
            return carry

        lax.fori_loop(0, SEQ // MERGE_ROWS, merge, 0)

    def pspec(off):
        return pl.BlockSpec((SEQ, LANE), lambda p, b: (b, off // LANE + p))

    ospec = pl.BlockSpec((SEQ, LANE), lambda p, b: (b, p))
    bspec = pl.BlockSpec((nseg, 2, ATTN_BLOCK, 2 * ATTN_BLOCK), lambda p, b: (0, p, 0, 0))
    gshape = SDS((n, B_WIDTH), BF16)
    return _call(
        body, name=name, grid=(HEAD_PAIRS, nb_local),
        in_specs=[pspec(Q_OFF), pspec(K_OFF), pspec(V_OFF), ospec,
                  pl.BlockSpec((SEQ, LANE), lambda p, b: (b, a_blocks + p)), ospec,
                  pl.BlockSpec((2 * nseg, 2, ATTN_BLOCK, 2 * ATTN_BLOCK), lambda p, b: (0, p, 0, 0))],
        out_specs=[ospec, ospec, ospec, bspec],
        out_shape=[gshape, gshape, gshape, SDS((nseg, B_HEADS, ATTN_BLOCK, 2 * ATTN_BLOCK), F32)],
        scratch_shapes=[pltpu.VMEM((SEQ, LANE), F32)] * 4 + [pltpu.VMEM((QUAD, QUAD_ROWS, LANE), F32)] * 6
        + [pltpu.VMEM((2 * QUAD, QUAD_ROWS, LANE), F32)] + [pltpu.VMEM((QUAD, QUAD_ROWS, LANE), F32)] * 2
        + [pltpu.VMEM((QUAD * QUAD, ATTN_BLOCK, LANE), F32)] * 9,
        args=[proj, proj, proj, b_out, dmix, lse_tot, bias], sem=("arbitrary", "arbitrary"), ride=ride)


PAD = 8
CONV_ROWS = 64


def _conv_taps(gp_ref, head_ref, r0):
    g0 = gp_ref[r0:r0 + CONV_ROWS, :]
    if r0 == 0:
        return g0, head_ref[PAD - 1:PAD - 1 + CONV_ROWS, :], head_ref[PAD - 2:PAD - 2 + CONV_ROWS, :]
    return g0, gp_ref[r0 - 1:r0 - 1 + CONV_ROWS, :], gp_ref[r0 - 2:r0 - 2 + CONV_ROWS, :]


def _fill_head(gp_ref, head_ref):
    head_ref[0:PAD, :] = jnp.zeros((PAD, LANE), F32)
    head_ref[PAD:PAD + CONV_ROWS, :] = gp_ref[0:CONV_ROWS, :]


def conv_gelu_fwd(gp, up, cw, cb, nb_local, name):
    n, f = gp.shape

    def body(gp_ref, up_ref, cw_ref, cb_ref, o_ref, head_ref):
        _fill_head(gp_ref, head_ref)
        w0, w1, w2, bias = cw_ref[0:1, :], cw_ref[1:2, :], cw_ref[2:3, :], cb_ref[...]
        for r0 in range(0, SEQ, CONV_ROWS):
            g0, g1, g2 = _conv_taps(gp_ref, head_ref, r0)
            c = bias + w0 * g2 + w1 * g1 + w2 * g0
            o_ref[r0:r0 + CONV_ROWS, :] = (_gelu(c) * up_ref[r0:r0 + CONV_ROWS, :]).astype(BF16)

    blk = pl.BlockSpec((SEQ, LANE), lambda b, j: (b, j))
    return pl.pallas_call(
        body, name=name, grid=(nb_local, f // LANE),
        in_specs=[blk, blk, pl.BlockSpec((3, LANE), lambda b, j: (0, j)), pl.BlockSpec((1, LANE), lambda b, j: (0, j))],
        out_specs=blk,
        out_shape=SDS((n, f), BF16),
        scratch_shapes=[pltpu.VMEM((PAD + CONV_ROWS, LANE), F32)],
        compiler_params=_params(("parallel", "parallel")),
    )(gp, up, cw, cb)


def conv_gelu_bwd(dgu, gp, up, cw, cb, nb_local, name, ride=None):
    n, f = gp.shape

    def fold(v):
        return jnp.sum(v.reshape(CONV_ROWS // 8, 8, LANE), axis=0)

    def body(dgu_ref, gp_ref, up_ref, cw_ref, cb_ref, dgp_ref, dup_ref, dcw_ref, dcb_ref, head_ref, dc_ref):
        b = pl.program_id(1)
        _fill_head(gp_ref, head_ref)
        dc_ref[SEQ:SEQ + PAD, :] = jnp.zeros((PAD, LANE), F32)
        w0, w1, w2, bias = cw_ref[0:1, :], cw_ref[1:2, :], cw_ref[2:3, :], cb_ref[...]
        sums = [jnp.zeros((8, LANE), F32) for _ in range(4)]
        for r0 in range(0, SEQ, CONV_ROWS):
            rows = slice(r0, r0 + CONV_ROWS)
            g0, g1, g2 = _conv_taps(gp_ref, head_ref, r0)
            gg, dgg = _gelu_and_grad(bias + w0 * g2 + w1 * g1 + w2 * g0)
            dgu = dgu_ref[rows, :].astype(F32)
            dup_ref[rows, :] = (dgu * gg).astype(BF16)
            dc = dgu * up_ref[rows, :] * dgg
            dc_ref[rows, :] = dc
            sums = [sums[0] + fold(dc * g2), sums[1] + fold(dc * g1), sums[2] + fold(dc * g0), sums[3] + fold(dc)]
        for r0 in range(0, SEQ, CONV_ROWS):
            dgp_ref[r0:r0 + CONV_ROWS, :] = (
                w2 * dc_ref[r0:r0 + CONV_ROWS, :] + w1 * dc_ref[r0 + 1:r0 + 1 + CONV_ROWS, :]
                + w0 * dc_ref[r0 + 2:r0 + 2 + CONV_ROWS, :]).astype(BF16)
        dcw = jnp.concatenate([jnp.sum(s, axis=0, keepdims=True) for s in sums[:3]], axis=0)
        dcb = jnp.sum(sums[3], axis=0, keepdims=True)

        @pl.when(b == 0)
        def _():
            dcw_ref[...] = dcw
            dcb_ref[...] = dcb

        @pl.when(b > 0)
        def _():
            dcw_ref[...] += dcw
            dcb_ref[...] += dcb

    blk = pl.BlockSpec((SEQ, LANE), lambda j, b: (b, j))
    return _call(
        body, name=name, grid=(f // LANE, nb_local),
        in_specs=[blk, blk, blk, pl.BlockSpec((3, LANE), lambda j, b: (0, j)), pl.BlockSpec((1, LANE), lambda j, b: (0, j))],
        out_specs=[blk, blk, pl.BlockSpec((3, LANE), lambda j, b: (0, j)), pl.BlockSpec((1, LANE), lambda j, b: (0, j))],
        out_shape=[SDS((n, f), BF16), SDS((n, f), BF16), SDS((3, f), F32), SDS((1, f), F32)],
        scratch_shapes=[pltpu.VMEM((PAD + CONV_ROWS, LANE), F32), pltpu.VMEM((SEQ + PAD, LANE), F32)],
        args=[dgu, gp, up, cw, cb], sem=("parallel", "arbitrary"), ride=ride)


def norm_mid_epilogue(x1, dout, z2, g3, g2):
    n, d = x1.shape

    def fn(dh2, step, x1_ref, dout_ref, z2_ref, g3_ref, g2_ref, dx1_ref, dz2_ref, dg3_ref, dg2_ref):
        dxa, dg3r = _rms_bwd(dh2, x1_ref[...], g3_ref[...])
        dx1 = dout_ref[...] + dxa
        dx1_ref[...] = dx1
        dz2, dg2r = _rms_bwd(dx1, z2_ref[...], g2_ref[...])
        dz2_ref[...] = dz2.astype(BF16)
        _accumulate(dg3_ref, jnp.sum(dg3r, axis=0, keepdims=True), step)
        _accumulate(dg2_ref, jnp.sum(dg2r, axis=0, keepdims=True), step)

    return fn, [x1, dout, z2, g3, g2], [SDS((n, d), F32), SDS((n, d), BF16), SDS((1, d), F32), SDS((1, d), F32)]


def norm_in_epilogue(x, dx1, g1):
    n, d = x.shape

    def fn(dh1, step, x_ref, dx1_ref, g1_ref, dx_ref, dg1_ref):
        dxa, dgr = _rms_bwd(dh1, x_ref[...], g1_ref[...])
        dx_ref[...] = dx1_ref[...] + dxa
        _accumulate(dg1_ref, jnp.sum(dgr, axis=0, keepdims=True), step)

    return fn, [x, dx1, g1], [SDS((n, d), F32), SDS((1, d), F32)]


def cast_bf16(arrays, name):
    def body(*refs):
        for i_ref, o_ref in zip(refs[:len(arrays)], refs[len(arrays):]):
            o_ref[...] = i_ref[...].astype(BF16)

    return pl.pallas_call(body, name=name, out_shape=[SDS(a.shape, BF16) for a in arrays],
                          compiler_params=_params())(*arrays)


def adam_update(parts, w, m, v, name, tr=None):
    s, r, c = parts.shape
    tr = r if tr is None else tr
    bc1 = 1.0 - ADAM_B1 ** ADAM_STEP
    bc2 = 1.0 - ADAM_B2 ** ADAM_STEP

    def body(p_ref, w_ref, m_ref, v_ref, g_ref, d_ref, nm_ref, nv_ref):
        g = p_ref[0].astype(F32)
        for j in range(1, s):
            g = g + p_ref[j].astype(F32)
        nm = ADAM_B1 * m_ref[...] + (1.0 - ADAM_B1) * g
        nv = ADAM_B2 * v_ref[...] + (1.0 - ADAM_B2) * (g * g)
        g_ref[...] = g
        nm_ref[...] = nm
        nv_ref[...] = nv
        d_ref[...] = -ADAM_LR * ((nm / bc1) / (jnp.sqrt(nv / bc2) + ADAM_EPS) + ADAM_WD * w_ref[...])

    blk = pl.BlockSpec((tr, c), lambda i: (i, 0))
    return pl.pallas_call(
        body, name=name, grid=(r // tr,),
        in_specs=[pl.BlockSpec((s, tr, c), lambda i: (0, i, 0)), blk, blk, blk],
        out_specs=[blk] * 4, out_shape=[SDS((r, c), F32)] * 4,
        compiler_params=_params(("parallel",)),
    )(parts, w, m, v)


EARLY_NAMES = ("spatial_w", "norm_mix_post", "norm_ffn_pre", "norm_ffn_post", "conv_b", "ln_v_gain", "ln_v_bias",
               "spatial_b")
LATE_NAMES = ("norm_mix_pre", "rel_bias")
PACK_ROW_ALIGN = 8


def _pack_rows(size):
    rows = -(-size // LANE)
    return -(-rows // PACK_ROW_ALIGN) * PACK_ROW_ALIGN


def _pack(arrays):
    flat = []
    for a in arrays:
        rows = _pack_rows(a.size)
        flat.append(jnp.pad(a.reshape(-1), (0, rows * LANE - a.size)))
    return jnp.concatenate(flat).reshape(-1, LANE)


def _unpack(packed, shapes):
    out, row = [], 0
    for shp in shapes:
        size = int(np.prod(shp))
        out.append(packed[row:row + _pack_rows(size)].reshape(-1)[:size].reshape(shp))
        row += _pack_rows(size)
    return out


def kernel(x, norm_mix_pre, norm_mix_post, norm_ffn_pre, norm_ffn_post, w_in, ln_v_gain, ln_v_bias, spatial_w, spatial_b, rel_bias, w_out, w_gate, w_up, conv_w, conv_b, w_down, loss_target, m_norm_mix_pre, m_norm_mix_post, m_norm_ffn_pre, m_norm_ffn_post, m_w_in, m_ln_v_gain, m_ln_v_bias, m_spatial_w, m_spatial_b, m_rel_bias, m_w_out, m_w_gate, m_w_up, m_conv_w, m_conv_b, m_w_down, v_norm_mix_pre, v_norm_mix_post, v_norm_ffn_pre, v_norm_ffn_post, v_w_in, v_ln_v_gain, v_ln_v_bias, v_spatial_w, v_spatial_b, v_rel_bias, v_w_out, v_w_gate, v_w_up, v_conv_w, v_conv_b, v_w_down):
    given = dict(locals())
    nb_local, seq, d = x.shape
    n = nb_local * seq
    cols = w_in.shape[2]

    def by_columns(g):
        return g.transpose(1, 0, 2).reshape(g.shape[1], N_DEV * g.shape[2])

    def to_blocks(g):
        return g.reshape(g.shape[0], N_DEV, cols).transpose(1, 0, 2)

    xf, target = x.reshape(n, d), loss_target.reshape(n, d)
    ln_g, ln_b = ln_v_gain.reshape(1, A_WIDTH), ln_v_bias.reshape(1, A_WIDTH)
    spatial_bt, rel_bias_t = spatial_b[0].T, rel_bias.T

    s_in, s_out, s_gate, s_up, s_down = cast_bf16([w_in[0], w_out[0], w_gate[0], w_up[0], w_down[0]], "cast_shards")
    g_in, g_cw = exchange([], [s_in, conv_w[0]], "gather_w_in")
    w_in_f, conv_w_f = by_columns(g_in), by_columns(g_cw)

    (h1, proj), _ = norm_mm(xf, norm_mix_pre, [w_in_f], "fwd_norm_in", tn=IN_COLS)
    a = gating_fwd(proj, ln_g, ln_b, spatial_w[0], spatial_bt, "fwd_gating")
    bias = bias_tables(rel_bias_t, "bias_tables").reshape(2 * len(DILATIONS), B_HEADS, ATTN_BLOCK, 2 * ATTN_BLOCK)
    (b_out, lse_tot), (g_out, g_gate, g_up) = attn_fwd(proj, bias, nb_local, "fwd_attn",
                                                       ride=([], [s_out, s_gate, s_up]))
    w_out_f, w_gate_f, w_up_f = g_out.reshape(D_MODEL, D_MODEL), by_columns(g_gate), by_columns(g_up)
    z2, x1 = mm_res_norm([a, b_out], w_out_f, xf, norm_mix_post, "fwd_out_norm")
    (h2, gp, up), (g_down,) = norm_mm(x1, norm_ffn_pre, [w_gate_f, w_up_f], "fwd_norm_ffn", tm=256, tn=D_FF,
                                      ride=([], [s_down]))
    w_down_f = g_down.reshape(D_FF, D_MODEL)
    gu = conv_gelu_fwd(gp, up, conv_w_f, conv_b, nb_local, "fwd_conv_gelu")
    dy, dout, dg4, loss_part = down_loss(gu, w_down_f, x1, norm_ffn_post, target, "fwd_down_loss")

    p_down = mm_tn([gu], [dy], "bwd_dw_down", t1=256, t2=D_MODEL)
    (dgu,), _ = mm_nt([(dy, 0, 0)], [w_down_f], "bwd_dgu", out_dtype=BF16)
    (dgp, dup, p_conv_w, p_conv_b), (r_down,) = conv_gelu_bwd(
        dgu, gp, up, conv_w_f, conv_b, nb_local, "bwd_conv_gelu", ride=([p_down.reshape(N_DEV, cols, D_MODEL)], []))
    p_gate = mm_tn([h2], [dgp], "bwd_dw_gate", t1=D_MODEL, t2=256)
    p_up = mm_tn([h2], [dup], "bwd_dw_up", t1=D_MODEL, t2=256)
    (dx1, dz2, dg3, dg2), _ = mm_nt([(dgp, 0, 0), (dup, 1, 0)], [w_gate_f, w_up_f], "bwd_dh2_norm_mid", tm=256,
                                    epilogue=norm_mid_epilogue(x1, dout, z2, norm_ffn_pre, norm_mix_post))
    p_out = mm_tn([a, b_out], [dz2], "bwd_dw_out", t1=256, t2=D_MODEL)
    (dmix,), _ = mm_nt([(dz2, 0, 0)], [w_out_f], "bwd_dmix")
    duv, p_ws, p_sbt, p_lng, p_lnb = gating_bwd(proj, dmix, ln_g, ln_b, spatial_w[0], spatial_bt, "bwd_gating")
    small = dict(spatial_w=p_ws, norm_mix_post=dg2, norm_ffn_pre=dg3, norm_ffn_post=dg4, conv_b=p_conv_b,
                 ln_v_gain=p_lng, ln_v_bias=p_lnb, spatial_b=p_sbt.T)
    pack_early = _pack([small[k] for k in EARLY_NAMES] + [p_conv_w, loss_part])
    (dq, dk, dv, dbias), (r_gate, r_up, r_out, r_early) = attn_bwd(
        proj, b_out, dmix, lse_tot, bias, nb_local, "bwd_attn",
        ride=([to_blocks(p_gate), to_blocks(p_up), p_out.reshape(N_DEV, D_MODEL // N_DEV, D_MODEL)], [pack_early]))
    p_rel_bias_t = rel_bias_grad(dbias.reshape(len(DILATIONS), B_HEADS, BIAS_SIZE), "bwd_rel_bias")
    p_in = mm_tn([h1], [duv, dq, dk, dv], "bwd_dw_in", t1=D_MODEL, t2=256)
    (grad_x, dg1), (r_in,) = mm_nt(
        [(duv, 0, 0), (dq, 0, Q_OFF), (dk, 0, K_OFF), (dv, 0, V_OFF)], [w_in_f], "bwd_dh1_norm_in",
        epilogue=norm_in_epilogue(xf, dx1, norm_mix_pre), ride=([to_blocks(p_in)], []))
    small.update(norm_mix_pre=dg1, rel_bias=p_rel_bias_t.T)
    (r_late,) = exchange([], [_pack([small[k] for k in LATE_NAMES])], "exchange_late")

    res = {}
    res["w_in"] = adam_update(r_in, w_in[0], m_w_in[0], v_w_in[0], "adam_w_in", tr=256)
    res["w_out"] = adam_update(r_out, w_out[0], m_w_out[0], v_w_out[0], "adam_w_out")
    res["w_gate"] = adam_update(r_gate, w_gate[0], m_w_gate[0], v_w_gate[0], "adam_w_gate", tr=256)
    res["w_up"] = adam_update(r_up, w_up[0], m_w_up[0], v_w_up[0], "adam_w_up", tr=256)
    res["w_down"] = adam_update(r_down, w_down[0], m_w_down[0], v_w_down[0], "adam_w_down", tr=176)

    def adam_packed(received, names, tail, name):
        zeros = [jnp.zeros_like(t) for t in tail]
        packs = [_pack([given[pre + k] for k in names] + zeros) for pre in ("", "m_", "v_")]
        shapes = [given[k].shape for k in names] + [t.shape for t in tail]
        unpacked = [_unpack(p, shapes) for p in adam_update(received, *packs, name)]
        for i, k in enumerate(names):
            res[k] = [u[i] for u in unpacked]
        return unpacked[0][len(names):]

    g_conv_w_full, loss_sum = adam_packed(r_early, EARLY_NAMES, [p_conv_w, loss_part], "adam_small_early")
    adam_packed(r_late, LATE_NAMES, [], "adam_small_late")
    g_conv_w = lax.dynamic_slice_in_dim(g_conv_w_full, _my_index() * cols, cols, axis=1)
    res["conv_w"] = adam_update(g_conv_w[None], conv_w[0], m_conv_w[0], v_conv_w[0], "adam_conv_w")
    loss = loss_sum[0, 0]

    names = ("norm_mix_pre", "norm_mix_post", "norm_ffn_pre", "norm_ffn_post", "w_in", "ln_v_gain", "ln_v_bias",
             "spatial_w", "spatial_b", "rel_bias", "w_out", "w_gate", "w_up", "conv_w", "conv_b", "w_down")
    outs = [loss, grad_x.reshape(x.shape)]
    for t in range(4):
        outs += [res[k][t].reshape(given[k].shape) for k in names]
    return tuple(outs)
```

```python
import functools
import math

import numpy as np
import jax
import jax.numpy as jnp
from jax import lax
from jax.experimental import pallas as pl
from jax.experimental.pallas import tpu as pltpu

F32 = jnp.float32
BF16 = jnp.bfloat16
SDS = jax.ShapeDtypeStruct

D_MODEL = 1024
SEQ = 2048
HEAD_DIM = 64
A_GROUPS = 4
A_WIDTH = A_GROUPS * HEAD_DIM
B_HEADS = 12
B_WIDTH = B_HEADS * HEAD_DIM
HEAD_PAIRS = B_HEADS // 2
CHUNK = 128
ATTN_BLOCK = 128
DILATIONS = (1, 4, 16)
NUM_BUCKETS = 32
MAX_DISTANCE = 2048
D_FF = 2816
IN_COLS = 2 * A_WIDTH + 3 * B_WIDTH
Q_OFF = 2 * A_WIDTH
K_OFF = Q_OFF + B_WIDTH
V_OFF = K_OFF + B_WIDTH
NORM_EPS = 1e-6
NEG_INF = -1e30
N_DEV = 8
LANE = 128

ADAM_LR = 0.001
ADAM_B1 = 0.9
ADAM_B2 = 0.999
ADAM_EPS = 1e-08
ADAM_WD = 0.01
ADAM_STEP = 10

GELU_C0 = math.sqrt(2.0 / math.pi)
GELU_C1 = 0.044715

VMEM_LIMIT = 56 * 1024 * 1024


def _params(sem=None):
    if sem is None:
        return pltpu.CompilerParams(vmem_limit_bytes=VMEM_LIMIT)
    return pltpu.CompilerParams(dimension_semantics=sem, vmem_limit_bytes=VMEM_LIMIT)


def _gelu(x):
    t = jnp.tanh(GELU_C0 * (x + GELU_C1 * x * x * x))
    return 0.5 * x * (1.0 + t)


def _gelu_and_grad(x):
    x2 = x * x
    t = jnp.tanh(GELU_C0 * (x + GELU_C1 * x * x2))
    g = 0.5 * x * (1.0 + t)
    dg = 0.5 * (1.0 + t) + 0.5 * x * (1.0 - t * t) * (GELU_C0 * (1.0 + 3.0 * GELU_C1 * x2))
    return g, dg


def _dot(a, b):
    return jnp.dot(a, b, preferred_element_type=F32)


def _dot_nt(a, b):
    return lax.dot_general(a, b, (((1,), (1,)), ((), ())), preferred_element_type=F32)


def _dot_tn(a, b):
    return lax.dot_general(a, b, (((0,), (0,)), ((), ())), preferred_element_type=F32)


def _rms_bwd(d, xin, g):
    r = lax.rsqrt(jnp.mean(xin * xin, axis=-1, keepdims=True) + NORM_EPS)
    xh = xin * r
    gd = g * d
    dx = r * (gd - xh * jnp.mean(gd * xh, axis=-1, keepdims=True))
    return dx, d * xh


MESH = pl.DeviceIdType.MESH
ANY = pl.BlockSpec(memory_space=pl.ANY)
PEER_MASKS = tuple(range(1, N_DEV))


def _my_index():
    return lax.axis_index("x") * 4 + lax.axis_index("y") * 2 + lax.axis_index("c")


def _peer(mask):
    x, y, c = lax.axis_index("x"), lax.axis_index("y"), lax.axis_index("c")
    px = 1 - x if mask & 4 else x
    py = 1 - y if mask & 2 else y
    pc = 1 - c if mask & 1 else c
    return (px, py, pc), px * 4 + py * 2 + pc


RELAY_AT = 3
SIBLING = 1
CHIP_MASKS = (2, 4, 6)


class _Exchange:
    def __init__(self, nblocked, in_refs, out_refs, sems):
        send_sems, recv_sems, local_sems = sems
        me = _my_index()
        sibling, _ = _peer(SIBLING)
        self.local, self.first, self.relays, self.relayed_in, self.last_in = [], [], [], [], []
        for a, (in_ref, out_ref) in enumerate(zip(in_refs, out_refs)):
            def copy(src, slot, mask, to):
                return pltpu.make_async_remote_copy(
                    src_ref=src, dst_ref=out_ref.at[slot], send_sem=send_sems.at[a, mask - 1],
                    recv_sem=recv_sems.at[a, mask - 1], device_id=to, device_id_type=MESH)

            if a < nblocked:
                self.local.append(pltpu.make_async_copy(in_ref.at[me], out_ref.at[me], local_sems.at[a]))
                for mask in PEER_MASKS:
                    peer, pidx = _peer(mask)
                    self.first.append(copy(in_ref.at[pidx], me, mask, peer))
                    self.last_in.append(copy(in_ref.at[pidx], pidx, mask, peer))
                continue
            self.local.append(pltpu.make_async_copy(in_ref, out_ref.at[me], local_sems.at[a]))
            for mask in (SIBLING,) + CHIP_MASKS:
                peer, pidx = _peer(mask)
                self.first.append(copy(in_ref, me, mask, peer))
                (self.last_in if mask == SIBLING else self.relayed_in).append(copy(in_ref, pidx, mask, peer))
            for mask in CHIP_MASKS:
                _, origin = _peer(mask)
                _, far = _peer(mask | SIBLING)
                self.relays.append(copy(out_ref.at[origin], origin, mask | SIBLING, sibling))
                self.last_in.append(copy(in_ref, far, mask | SIBLING, sibling))

    def start(self):
        for cp in self.local + self.first[::-1]:
            cp.start()

    def relay(self):
        for arrived, onward in zip(self.relayed_in, self.relays):
            arrived.wait_recv()
            onward.start()

    def finish(self):
        for cp in self.first + self.relays:
            cp.wait_send()
        for cp in self.last_in:
            cp.wait_recv()
        for cp in self.local:
            cp.wait()


def _exchange_out_shape(blocked, whole):
    return [SDS(b.shape, b.dtype) for b in blocked] + [SDS((N_DEV,) + w.shape, w.dtype) for w in whole]


def _exchange_sems(n):
    return [pltpu.SemaphoreType.DMA((n, N_DEV - 1)), pltpu.SemaphoreType.DMA((n, N_DEV - 1)),
            pltpu.SemaphoreType.DMA((n,))]


def exchange(blocked, whole, name):
    nb, n = len(blocked), len(blocked) + len(whole)

    def body(*refs):
        ex = _Exchange(nb, refs[:n], refs[n:2 * n], refs[2 * n:])
        ex.start()
        ex.relay()
        ex.finish()

    return pl.pallas_call(
        body, name=name, in_specs=[ANY] * n, out_specs=[ANY] * n, out_shape=_exchange_out_shape(blocked, whole),
        scratch_shapes=_exchange_sems(n),
    )(*blocked, *whole)


def _call(body, *, name, grid, in_specs, out_specs, out_shape, args, scratch_shapes=(), sem=None, ride=None):
    out_shape, out_specs, scratch_shapes = list(out_shape), list(out_specs), list(scratch_shapes)
    if ride is None:
        outs = pl.pallas_call(body, name=name, grid=grid, in_specs=list(in_specs), out_specs=out_specs,
                              out_shape=out_shape, scratch_shapes=scratch_shapes,
                              compiler_params=_params(sem))(*args)
        return list(outs), []
    blocked, whole = ride
    cargs = list(blocked) + list(whole)
    nb, nc = len(blocked), len(cargs)
    n_in, n_out, n_scr = len(args), len(out_shape), len(scratch_shapes)
    steps = math.prod(grid)
    assert steps >= 3, grid

    def riding(*refs):
        ins, refs = refs[:n_in], refs[n_in:]
        cins, refs = refs[:nc], refs[nc:]
        outs, refs = refs[:n_out], refs[n_out:]
        couts, refs = refs[:nc], refs[nc:]
        scr, sems = refs[:n_scr], refs[n_scr:]
        step = functools.reduce(lambda acc, k: acc * grid[k] + pl.program_id(k), range(len(grid)), 0)

        @pl.when(step == 0)
        def _():
            _Exchange(nb, cins, couts, sems).start()

        @pl.when(step == RELAY_AT * steps // 4)
        def _():
            _Exchange(nb, cins, couts, sems).relay()

        body(*ins, *outs, *scr)

        @pl.when(step == steps - 1)
        def _():
            _Exchange(nb, cins, couts, sems).finish()

    res = pl.pallas_call(
        riding, name=name, grid=grid, in_specs=list(in_specs) + [ANY] * nc, out_specs=out_specs + [ANY] * nc,
        out_shape=out_shape + _exchange_out_shape(blocked, whole),
        scratch_shapes=scratch_shapes + _exchange_sems(nc),
        compiler_params=_params(("arbitrary",) * len(grid)))(*args, *cargs)
    return list(res[:n_out]), list(res[n_out:])


def norm_mm(x, g, ws, name, tm=512, tn=1408, ride=None):
    n, d = x.shape
    f = ws[0].shape[1]
    nw = len(ws)

    def body(x_ref, g_ref, *refs):
        w_refs = refs[:nw]
        h_ref = refs[nw]
        o_refs = refs[nw + 1:]

        @pl.when(pl.program_id(1) == 0)
        def _():
            xv = x_ref[...]
            r = lax.rsqrt(jnp.mean(xv * xv, axis=-1, keepdims=True) + NORM_EPS)
            h_ref[...] = (xv * r * g_ref[...]).astype(BF16)

        h = h_ref[...]
        for w_ref, o_ref in zip(w_refs, o_refs):
            o_ref[...] = _dot(h, w_ref[...])

    return _call(
        body, name=name, grid=(n // tm, f // tn),
        in_specs=[pl.BlockSpec((tm, d), lambda i, j: (i, 0)), pl.BlockSpec((1, d), lambda i, j: (0, 0))]
        + [pl.BlockSpec((d, tn), lambda i, j: (0, j)) for _ in ws],
        out_specs=[pl.BlockSpec((tm, d), lambda i, j: (i, 0))]
        + [pl.BlockSpec((tm, tn), lambda i, j: (i, j)) for _ in ws],
        out_shape=[SDS((n, d), BF16)] + [SDS((n, f), F32) for _ in ws],
        args=[x, g, *ws], sem=("parallel", "arbitrary"), ride=ride)


def _lane_concat(refs):
    vals = [r[...].astype(BF16) for r in refs]
    return vals[0] if len(vals) == 1 else jnp.concatenate(vals, axis=1)


def mm_res_norm(a_list, w, res, g, name, tm=512):
    n = a_list[0].shape[0]
    k, d = w.shape
    na = len(a_list)

    def body(*refs):
        w_ref, res_ref, g_ref, y_ref, o_ref = refs[na:]
        y = _dot(_lane_concat(refs[:na]), w_ref[...])
        r = lax.rsqrt(jnp.mean(y * y, axis=-1, keepdims=True) + NORM_EPS)
        y_ref[...] = y
        o_ref[...] = res_ref[...] + y * r * g_ref[...]

    return pl.pallas_call(
        body, name=name, grid=(n // tm,),
        in_specs=[pl.BlockSpec((tm, a.shape[1]), lambda i: (i, 0)) for a in a_list]
        + [pl.BlockSpec((k, d), lambda i: (0, 0)),
           pl.BlockSpec((tm, d), lambda i: (i, 0)), pl.BlockSpec((1, d), lambda i: (0, 0))],
        out_specs=[pl.BlockSpec((tm, d), lambda i: (i, 0)), pl.BlockSpec((tm, d), lambda i: (i, 0))],
        out_shape=[SDS((n, d), F32), SDS((n, d), F32)],
        compiler_params=_params(("parallel",)),
    )(*a_list, w, res, g)


def down_loss(a, w, res, g, target, name, tm=256):
    n, k = a.shape
    d = w.shape[1]
    inv_d = 1.0 / d

    def body(a_ref, w_ref, res_ref, g_ref, t_ref, dy_ref, dout_ref, dg_ref, loss_ref):
        i = pl.program_id(0)
        y = _dot(a_ref[...], w_ref[...])
        gv = g_ref[...]
        r = lax.rsqrt(jnp.mean(y * y, axis=-1, keepdims=True) + NORM_EPS)
        yh = y * r
        e = res_ref[...] + yh * gv - t_ref[...]
        part = 0.5 * inv_d * jnp.sum(jnp.sum(e * e, axis=-1, keepdims=True), axis=0, keepdims=True)
        dout = e * inv_d
        dout_ref[...] = dout
        gd = gv * dout
        dy_ref[...] = (r * (gd - yh * jnp.mean(gd * yh, axis=-1, keepdims=True))).astype(BF16)
        dgp = jnp.sum(dout * yh, axis=0, keepdims=True)
        lane0 = lax.broadcasted_iota(jnp.int32, (1, LANE), 1) == 0
        lp = jnp.where(lane0, part, 0.0)

        @pl.when(i == 0)
        def _():
            dg_ref[...] = dgp
            loss_ref[...] = lp

        @pl.when(i > 0)
        def _():
            dg_ref[...] += dgp
            loss_ref[...] += lp

    return pl.pallas_call(
        body, name=name, grid=(n // tm,),
        in_specs=[pl.BlockSpec((tm, k), lambda i: (i, 0)), pl.BlockSpec((k, d), lambda i: (0, 0)),
                  pl.BlockSpec((tm, d), lambda i: (i, 0)), pl.BlockSpec((1, d), lambda i: (0, 0)),
                  pl.BlockSpec((tm, d), lambda i: (i, 0))],
        out_specs=[pl.BlockSpec((tm, d), lambda i: (i, 0)), pl.BlockSpec((tm, d), lambda i: (i, 0)),
                   pl.BlockSpec((1, d), lambda i: (0, 0)), pl.BlockSpec((1, LANE), lambda i: (0, 0))],
        out_shape=[SDS((n, d), BF16), SDS((n, d), F32), SDS((1, d), F32), SDS((1, LANE), F32)],
        compiler_params=_params(("arbitrary",)),
    )(a, w, res, g, target)


def _accumulate(ref, val, step):
    @pl.when(step == 0)
    def _():
        ref[...] = val

    @pl.when(step > 0)
    def _():
        ref[...] += val


def mm_nt(terms, ws, name, tm=512, out_dtype=F32, ride=None, epilogue=None):
    n = terms[0][0].shape[0]
    r = ws[0].shape[0]
    na = len(terms)
    meta = [(widx, off, a.shape[1]) for a, widx, off in terms]
    fn, extras, out_shape = epilogue if epilogue else (None, [], [SDS((n, r), out_dtype)])
    n_fixed = na + len(ws)

    def body(*refs):
        a_refs = refs[:na]
        w_refs = refs[na:n_fixed]
        acc = None
        for a_ref, (widx, off, k) in zip(a_refs, meta):
            p = _dot_nt(a_ref[...].astype(BF16), w_refs[widx][:, off:off + k])
            acc = p if acc is None else acc + p
        if fn is None:
            refs[-1][...] = acc.astype(out_dtype)
        else:
            fn(acc, pl.program_id(0), *refs[n_fixed:])

    def spec(a):
        if a.shape[0] == 1:
            return pl.BlockSpec(a.shape, lambda i: (0, 0))
        return pl.BlockSpec((tm, a.shape[1]), lambda i: (i, 0))

    return _call(
        body, name=name, grid=(n // tm,),
        in_specs=[spec(a) for a, _, _ in terms] + [pl.BlockSpec(w.shape, lambda i: (0, 0)) for w in ws]
        + [spec(e) for e in extras],
        out_specs=[spec(o) for o in out_shape], out_shape=out_shape,
        args=[a for a, _, _ in terms] + list(ws) + list(extras),
        sem=("parallel",) if fn is None else ("arbitrary",), ride=ride)


def _piece_blocks(pieces, tile):
    out, first = [], 0
    for p in pieces:
        nblk, rem = divmod(p.shape[1], tile)
        assert rem == 0, (p.shape, tile)
        out.append((first, nblk))
        first += nblk
    return out, first


def mm_tn(lhs_list, rhs_list, name, t1, t2, out_dtype=BF16):
    n = lhs_list[0].shape[0]
    lblocks, nbl = _piece_blocks(lhs_list, t1)
    rblocks, nbr = _piece_blocks(rhs_list, t2)
    nl = len(lhs_list)

    def body(*refs):
        l_refs, r_refs, o_ref = refs[:nl], refs[nl:-1], refs[-1]
        i, j = pl.program_id(0), pl.program_id(1)
        for l_ref, (ls, ln) in zip(l_refs, lblocks):
            for r_ref, (rs, rn) in zip(r_refs, rblocks):
                @pl.when((i >= ls) & (i < ls + ln) & (j >= rs) & (j < rs + rn))
                def _(l_ref=l_ref, r_ref=r_ref):
                    o_ref[...] = _dot_tn(l_ref[...].astype(BF16), r_ref[...].astype(BF16)).astype(out_dtype)

    def piece_spec(tile, axis, first, nblk):
        def index(i, j):
            return 0, jnp.clip((i, j)[axis] - first, 0, nblk - 1)
        return pl.BlockSpec((n, tile), index)

    return pl.pallas_call(
        body, name=name, grid=(nbl, nbr),
        in_specs=[piece_spec(t1, 0, *b) for b in lblocks] + [piece_spec(t2, 1, *b) for b in rblocks],
        out_specs=pl.BlockSpec((t1, t2), lambda i, j: (i, j)),
        out_shape=SDS((nbl * t1, nbr * t2), out_dtype),
        compiler_params=_params(("parallel", "arbitrary")),
    )(*lhs_list, *rhs_list)


GATE_ROWS = 512


def _tril_mask():
    row = lax.broadcasted_iota(jnp.int32, (CHUNK, CHUNK), 0)
    col = lax.broadcasted_iota(jnp.int32, (CHUNK, CHUNK), 1)
    return row >= col


def _group_of(shape, axis):
    return lax.broadcasted_iota(jnp.int32, shape, axis) // HEAD_DIM


def _group_mean_matrix():
    same = _group_of((A_WIDTH, A_WIDTH), 0) == _group_of((A_WIDTH, A_WIDTH), 1)
    return jnp.where(same, 1.0 / HEAD_DIM, 0.0).astype(F32)


def _dot_f32(a, b):
    return jnp.dot(a, b, preferred_element_type=F32, precision=lax.Precision.HIGHEST)


def _by_group(parts, lane_group):
    out = parts[A_GROUPS - 1]
    for g in range(A_GROUPS - 2, -1, -1):
        out = jnp.where(lane_group == g, parts[g], out)
    return out


def _group_norm(gv, gmean):
    xc = gv - _dot_f32(gv, gmean)
    rstd = lax.rsqrt(_dot_f32(xc * xc, gmean) + NORM_EPS)
    return xc * rstd, rstd


def gating_fwd(proj, lng, lnb, ws, sbt, name):
    n = proj.shape[0]

    def body(u_ref, v_ref, lng_ref, lnb_ref, ws_ref, sbt_ref, a_ref):
        tril = _tril_mask()
        lane_group = _group_of((CHUNK, A_WIDTH), 1)
        gmean = _group_mean_matrix()
        wts = [jnp.where(tril, ws_ref[g], 0.0).astype(BF16) for g in range(A_GROUPS)]
        sb = _by_group([sbt_ref[:, g:g + 1] for g in range(A_GROUPS)], lane_group)

        def chunk(c, carry):
            rows = pl.ds(pl.multiple_of(c * CHUNK, CHUNK), CHUNK)
            vhat, _ = _group_norm(_gelu(v_ref[rows, :]), gmean)
            vn = (vhat * lng_ref[...] + lnb_ref[...]).astype(BF16)
            z = _by_group([_dot(wt, vn) for wt in wts], lane_group) + sb
            a_ref[rows, :] = _gelu(u_ref[rows, :]) * z
            return carry

        lax.fori_loop(0, GATE_ROWS // CHUNK, chunk, 0)

    return pl.pallas_call(
        body, name=name, grid=(n // GATE_ROWS,),
        in_specs=[pl.BlockSpec((GATE_ROWS, A_WIDTH), lambda i: (i, 0)),
                  pl.BlockSpec((GATE_ROWS, A_WIDTH), lambda i: (i, 1)),
                  pl.BlockSpec((1, A_WIDTH), lambda i: (0, 0)), pl.BlockSpec((1, A_WIDTH), lambda i: (0, 0)),
                  pl.BlockSpec((A_GROUPS, CHUNK, CHUNK), lambda i: (0, 0, 0)),
                  pl.BlockSpec((CHUNK, A_GROUPS), lambda i: (0, 0))],
        out_specs=pl.BlockSpec((GATE_ROWS, A_WIDTH), lambda i: (i, 0)),
        out_shape=SDS((n, A_WIDTH), F32),
        compiler_params=_params(("parallel",)),
    )(proj, proj, lng, lnb, ws, sbt)


def gating_bwd(proj, dmix, lng, lnb, ws, sbt, name):
    n = proj.shape[0]

    def body(u_ref, v_ref, da_ref, lng_ref, lnb_ref, ws_ref, sbt_ref,
             duv_ref, dws_ref, dsbt_ref, dlng_ref, dlnb_ref):
        @pl.when(pl.program_id(0) == 0)
        def _():
            dws_ref[...] = jnp.zeros_like(dws_ref)
            dsbt_ref[...] = jnp.zeros_like(dsbt_ref)
            dlng_ref[...] = jnp.zeros_like(dlng_ref)
            dlnb_ref[...] = jnp.zeros_like(dlnb_ref)

        tril = _tril_mask()
        lane_group = _group_of((CHUNK, A_WIDTH), 1)
        gmean = _group_mean_matrix()
        gsum = (_group_of((A_WIDTH, LANE), 0) == lax.broadcasted_iota(jnp.int32, (A_WIDTH, LANE), 1)).astype(F32)
        wts = [jnp.where(tril, ws_ref[g], 0.0) for g in range(A_GROUPS)]
        wts_b = [w.astype(BF16) for w in wts]
        wts_t = [w.T.astype(BF16) for w in wts]
        sb = _by_group([sbt_ref[:, g:g + 1] for g in range(A_GROUPS)], lane_group)
        lg = lng_ref[...]

        def chunk(c, carry):
            rows = pl.ds(pl.multiple_of(c * CHUNK, CHUNK), CHUNK)
            gu, dgu_dx = _gelu_and_grad(u_ref[rows, :])
            gv, dgv_dx = _gelu_and_grad(v_ref[rows, :])
            vhat, rstd = _group_norm(gv, gmean)
            vn = (vhat * lg + lnb_ref[...]).astype(BF16)
            z = _by_group([_dot(wt, vn) for wt in wts_b], lane_group) + sb
            da = da_ref[rows, :]
            dz = da * gu
            dzb = dz.astype(BF16)
            duv_ref[rows, 0:A_WIDTH] = (da * z * dgu_dx).astype(BF16)
            dsbt_ref[...] += _dot_f32(dz, gsum)[:, 0:A_GROUPS]
            for g in range(A_GROUPS):
                dz_g = jnp.where(lane_group == g, dzb, jnp.zeros_like(dzb))
                dws_ref[g] += jnp.where(tril, _dot_nt(dz_g, vn), 0.0)
            dvn = _by_group([_dot(wt, dzb) for wt in wts_t], lane_group)
            dlng_ref[...] += jnp.sum(dvn * vhat, axis=0, keepdims=True)
            dlnb_ref[...] += jnp.sum(dvn, axis=0, keepdims=True)
            dvh = dvn * lg
            dgv = rstd * (dvh - _dot_f32(dvh, gmean) - vhat * _dot_f32(dvh * vhat, gmean))
            duv_ref[rows, A_WIDTH:2 * A_WIDTH] = (dgv * dgv_dx).astype(BF16)
            return carry

        lax.fori_loop(0, GATE_ROWS // CHUNK, chunk, 0)

    return pl.pallas_call(
        body, name=name, grid=(n // GATE_ROWS,),
        in_specs=[pl.BlockSpec((GATE_ROWS, A_WIDTH), lambda i: (i, 0)),
                  pl.BlockSpec((GATE_ROWS, A_WIDTH), lambda i: (i, 1)),
                  pl.BlockSpec((GATE_ROWS, A_WIDTH), lambda i: (i, 0)),
                  pl.BlockSpec((1, A_WIDTH), lambda i: (0, 0)), pl.BlockSpec((1, A_WIDTH), lambda i: (0, 0)),
                  pl.BlockSpec((A_GROUPS, CHUNK, CHUNK), lambda i: (0, 0, 0)),
                  pl.BlockSpec((CHUNK, A_GROUPS), lambda i: (0, 0))],
        out_specs=[pl.BlockSpec((GATE_ROWS, 2 * A_WIDTH), lambda i: (i, 0)),
                   pl.BlockSpec((A_GROUPS, CHUNK, CHUNK), lambda i: (0, 0, 0)),
                   pl.BlockSpec((CHUNK, A_GROUPS), lambda i: (0, 0)),
                   pl.BlockSpec((1, A_WIDTH), lambda i: (0, 0)), pl.BlockSpec((1, A_WIDTH), lambda i: (0, 0))],
        out_shape=[SDS((n, 2 * A_WIDTH), BF16), SDS((A_GROUPS, CHUNK, CHUNK), F32), SDS((CHUNK, A_GROUPS), F32),
                   SDS((1, A_WIDTH), F32), SDS((1, A_WIDTH), F32)],
        compiler_params=_params(("arbitrary",)),
    )(proj, proj, dmix, lng, lnb, ws, sbt)


def _t5_bucket_np(dist):
    max_exact = NUM_BUCKETS // 2
    dd = np.maximum(dist, 1).astype(np.float64)
    large = max_exact + np.log(dd / max_exact) / math.log(MAX_DISTANCE / max_exact) * (NUM_BUCKETS - max_exact)
    large = np.minimum(large.astype(np.int64), NUM_BUCKETS - 1)
    return np.where(dist < max_exact, dist, large)


def _bucket_tables(with_first):
    i = np.arange(ATTN_BLOCK)[:, None]
    j = np.arange(2 * ATTN_BLOCK)[None, :]
    rel = ATTN_BLOCK + i - j
    band = (rel >= 0) & (rel <= ATTN_BLOCK)
    tabs = []
    for own_only in (False, True) if with_first else (False,):
        for dil in DILATIONS:
            b = _t5_bucket_np(np.maximum(rel, 0) * dil)
            tabs.append(np.where(band & (j >= ATTN_BLOCK) if own_only else band, b, -1).reshape(1, -1))
    return np.stack(tabs).astype(np.float32)


BIAS_SIZE = ATTN_BLOCK * 2 * ATTN_BLOCK


def bias_tables(rel_bias_t, name):
    idx = jnp.asarray(_bucket_tables(True))
    ntab = idx.shape[0]

    def body(rb_ref, idx_ref, o_ref):
        iv = idx_ref[0]
        bk = lax.broadcasted_iota(jnp.int32, (NUM_BUCKETS, BIAS_SIZE), 0).astype(F32)
        onehot = (bk == iv).astype(F32)
        t = jnp.dot(rb_ref[...], onehot, preferred_element_type=F32, precision=lax.Precision.HIGHEST)
        o_ref[0] = jnp.where(iv < 0.0, NEG_INF, t)

    return pl.pallas_call(
        body, name=name, grid=(ntab,),
        in_specs=[pl.BlockSpec((B_HEADS, NUM_BUCKETS), lambda d: (0, 0)),
                  pl.BlockSpec((1, 1, BIAS_SIZE), lambda d: (d, 0, 0))],
        out_specs=pl.BlockSpec((1, B_HEADS, BIAS_SIZE), lambda d: (d, 0, 0)),
        out_shape=SDS((ntab, B_HEADS, BIAS_SIZE), F32),
        compiler_params=_params(("parallel",)),
    )(rel_bias_t, idx)


def rel_bias_grad(dbias, name):
    idx = jnp.asarray(_bucket_tables(False))

    def body(db_ref, idx_ref, o_ref):
        d = pl.program_id(0)
        iv = idx_ref[0]
        bk = lax.broadcasted_iota(jnp.int32, (NUM_BUCKETS, BIAS_SIZE), 0).astype(F32)
        onehot = (bk == iv).astype(F32)
        part = lax.dot_general(db_ref[0], onehot, (((1,), (1,)), ((), ())),
                               preferred_element_type=F32, precision=lax.Precision.HIGHEST)

        @pl.when(d == 0)
        def _():
            o_ref[...] = part

        @pl.when(d > 0)
        def _():
            o_ref[...] += part

    return pl.pallas_call(
        body, name=name, grid=(len(DILATIONS),),
        in_specs=[pl.BlockSpec((1, B_HEADS, BIAS_SIZE), lambda d: (d, 0, 0)),
                  pl.BlockSpec((1, 1, BIAS_SIZE), lambda d: (d, 0, 0))],
        out_specs=pl.BlockSpec((B_HEADS, NUM_BUCKETS), lambda d: (0, 0)),
        out_shape=SDS((B_HEADS, NUM_BUCKETS), F32),
        compiler_params=_params(("arbitrary",)),
    )(dbias, idx)


def _attn_scores(q, kk, bias):
    return _dot_nt(q, kk) * (1.0 / math.sqrt(HEAD_DIM)) + bias


def _head0_lanes():
    return lax.broadcasted_iota(jnp.int32, (ATTN_BLOCK, LANE), 1) < HEAD_DIM


def _one_head(x2, head0, hh):
    return jnp.where(head0 if hh == 0 else jnp.logical_not(head0), x2, 0.0).astype(BF16)


def _rows(start, size, dil):
    return pl.ds(start, size) if dil == 1 else pl.ds(start, size, stride=dil)


QUAD = 4
QUAD_ROWS = SEQ // QUAD


def _deinterleave(src_ref, dst_ref):
    for r in range(QUAD):
        for c in range(QUAD_ROWS // ATTN_BLOCK):
            dst_ref[r, c * ATTN_BLOCK:(c + 1) * ATTN_BLOCK, :] = src_ref[
                pl.ds(r + c * QUAD * ATTN_BLOCK, ATTN_BLOCK, stride=QUAD), :]


def _deinterleave_again(src_ref, dst_ref):
    for r in range(QUAD):
        for s in range(QUAD):
            dst_ref[r + QUAD * s] = src_ref[r, pl.ds(s, ATTN_BLOCK, stride=QUAD), :]


def _interleave_back(src_ref, dst_ref, slot0, accumulate=False):
    for r in range(QUAD):
        for s in range(QUAD):
            rows = pl.ds(s, ATTN_BLOCK, stride=QUAD)
            if accumulate:
                dst_ref[slot0 + r, rows, :] += src_ref[r + QUAD * s]
            else:
                dst_ref[slot0 + r, rows, :] = src_ref[r + QUAD * s]


def _quad_tiles():
    return [(r, pl.ds(r + c * QUAD * ATTN_BLOCK, ATTN_BLOCK, stride=QUAD), slice(c * ATTN_BLOCK, (c + 1) * ATTN_BLOCK))
            for r in range(QUAD) for c in range(QUAD_ROWS // ATTN_BLOCK)]


def _attn_schedule(op):
    def d16(i, carry):
        for t in range(2 * QUAD):
            op(2, 2 * QUAD * i + t, 0, 1, True)
        return carry

    lax.fori_loop(0, QUAD // 2, d16, 0)

    def d4(i, carry):
        for u in range(2):
            for nq in range(QUAD_ROWS // ATTN_BLOCK):
                op(1, 2 * i + u, nq * ATTN_BLOCK, 1, nq == 0)
        return carry

    lax.fori_loop(0, QUAD // 2, d4, 0)
    op(0, None, 0, 1, True)
    per_pass = 5

    def d1(j, carry):
        for t in range(per_pass):
            op(0, None, pl.multiple_of((1 + per_pass * j + t) * ATTN_BLOCK, ATTN_BLOCK), 1, False)
        return carry

    lax.fori_loop(0, (SEQ // ATTN_BLOCK - 1) // per_pass, d1, 0)


def _keys(src, krows, first):
    kb = src[krows, :].astype(BF16)
    return jnp.concatenate([kb, kb], axis=0) if first else kb


def _table(seg, first):
    return len(DILATIONS) + seg if first else seg


def _kv_rows(start, dil, first):
    if first:
        return _rows(start, ATTN_BLOCK, dil)
    return _rows(start - ATTN_BLOCK * dil, 2 * ATTN_BLOCK, dil)


MERGE_ROWS = 256


def attn_fwd(proj, bias, nb_local, name, ride=None):
    n = proj.shape[0]
    nseg = len(DILATIONS)

    def body(q_ref, k_ref, v_ref, b_ref, o_ref, lse_ref, q4_ref, k4_ref, v4_ref, os0_ref, ls0_ref, os4_ref, ls4_ref,
             q16_ref, k16_ref, v16_ref, os16_ref, ls16_ref):
        for src, mid, dst in ((q_ref, q4_ref, q16_ref), (k_ref, k4_ref, k16_ref), (v_ref, v4_ref, v16_ref)):
            _deinterleave(src, mid)
            _deinterleave_again(mid, dst)

        def op(seg, r, start, stride, first):
            qrows = _rows(start, ATTN_BLOCK, stride)
            krows = _kv_rows(start, stride, first)
            if seg == 0:
                q_src, k_src, v_src, o_dst, l_dst = q_ref, k_ref, v_ref, os0_ref, ls0_ref
            elif seg == 1:
                q_src, k_src, v_src = q4_ref.at[r], k4_ref.at[r], v4_ref.at[r]
                o_dst, l_dst = os4_ref.at[r], ls4_ref.at[r]
            else:
                q_src, k_src, v_src = q16_ref.at[r], k16_ref.at[r], v16_ref.at[r]
                o_dst, l_dst = os16_ref.at[r], ls16_ref.at[r]
            q2, kb, vb = q_src[qrows, :], _keys(k_src, krows, first), _keys(v_src, krows, first)
            head0 = _head0_lanes()
            outs, lses = [], []
            for hh in range(2):
                s = _attn_scores(_one_head(q2, head0, hh), kb, b_ref[_table(seg, first), hh])
                m = jnp.max(s, axis=-1, keepdims=True)
                p = jnp.exp(s - m)
                l = jnp.sum(p, axis=-1, keepdims=True)
                outs.append(_dot(p.astype(BF16), vb) / l)
                lses.append(jnp.broadcast_to(m + jnp.log(l), (ATTN_BLOCK, LANE)))
            o_dst[qrows, :] = jnp.where(head0, outs[0], outs[1])
            l_dst[qrows, :] = jnp.where(head0, lses[0], lses[1])

        _attn_schedule(op)
        _interleave_back(os16_ref, os4_ref, QUAD)
        _interleave_back(ls16_ref, ls4_ref, QUAD)

        for r, nat, quad in _quad_tiles():
            ls = [ls0_ref[nat, :], ls4_ref[r, quad, :], ls4_ref[QUAD + r, quad, :]]
            m = functools.reduce(jnp.maximum, ls)
            ws = [jnp.exp(l - m) for l in ls]
            den = ws[0] + ws[1] + ws[2]
            num = ws[0] * os0_ref[nat, :] + ws[1] * os4_ref[r, quad, :] + ws[2] * os4_ref[QUAD + r, quad, :]
            o_ref[nat, :] = num / den
            lse_ref[nat, :] = m + jnp.log(den)

    def in_spec(off):
        return pl.BlockSpec((SEQ, LANE), lambda b, p: (b, off // LANE + p))

    out_spec = pl.BlockSpec((SEQ, LANE), lambda b, p: (b, p))
    return _call(
        body, name=name, grid=(nb_local, HEAD_PAIRS),
        in_specs=[in_spec(Q_OFF), in_spec(K_OFF), in_spec(V_OFF),
                  pl.BlockSpec((2 * nseg, 2, ATTN_BLOCK, 2 * ATTN_BLOCK), lambda b, p: (0, p, 0, 0))],
        out_specs=[out_spec, out_spec],
        out_shape=[SDS((n, B_WIDTH), F32), SDS((n, B_WIDTH), F32)],
        scratch_shapes=[pltpu.VMEM((QUAD, QUAD_ROWS, LANE), F32)] * 3 + [pltpu.VMEM((SEQ, LANE), F32)] * 2
        + [pltpu.VMEM((2 * QUAD, QUAD_ROWS, LANE), F32)] * 2 + [pltpu.VMEM((QUAD * QUAD, ATTN_BLOCK, LANE), F32)] * 5,
        args=[proj, proj, proj, bias], sem=("parallel", "arbitrary"), ride=ride)


def attn_bwd(proj, b_out, dmix, lse_tot, bias, nb_local, name, ride=None):
    n = proj.shape[0]
    nseg = len(DILATIONS)
    a_blocks = A_WIDTH // LANE
    scale = 1.0 / math.sqrt(HEAD_DIM)

    def body(q_ref, k_ref, v_ref, o_ref, do_ref, lse_ref, b_ref, dq_ref, dk_ref, dv_ref, db_ref,
             dqs_ref, delta_ref, dka_ref, dva_ref, q4_ref, k4_ref, v4_ref, do4_ref, lse4_ref, delta4_ref,
             dqs4_ref, dk4_ref, dv4_ref, q16_ref, k16_ref, v16_ref, do16_ref, lse16_ref, delta16_ref,
             dqs16_ref, dk16_ref, dv16_ref):
        @pl.when(pl.program_id(1) == 0)
        def _():
            db_ref[...] = jnp.zeros_like(db_ref)

        for acc_ref in (dka_ref, dva_ref, dk4_ref, dv4_ref):
            acc_ref[...] = jnp.zeros_like(acc_ref)
        quads = (q4_ref, k4_ref, v4_ref, do4_ref, lse4_ref, delta4_ref)
        hexes = (q16_ref, k16_ref, v16_ref, do16_ref, lse16_ref, delta16_ref)

        def row_dots(i, carry):
            rows = pl.ds(pl.multiple_of(i * ATTN_BLOCK, ATTN_BLOCK), ATTN_BLOCK)
            head0 = _head0_lanes()
            prod = do_ref[rows, :] * o_ref[rows, :]
            d0 = jnp.sum(jnp.where(head0, prod, 0.0), axis=-1, keepdims=True)
            d1 = jnp.sum(jnp.where(head0, 0.0, prod), axis=-1, keepdims=True)
            delta_ref[rows, :] = jnp.where(head0, d0, d1)
            return carry

        lax.fori_loop(0, SEQ // ATTN_BLOCK, row_dots, 0)
        for src, mid, dst in zip((q_ref, k_ref, v_ref, do_ref, lse_ref, delta_ref), quads, hexes):
            _deinterleave(src, mid)
            _deinterleave_again(mid, dst)

        def op(seg, r, start, stride, first):
            qrows = _rows(start, ATTN_BLOCK, stride)
            krows = _kv_rows(start, stride, first)
            if seg == 0:
                srcs = (q_ref, k_ref, v_ref, do_ref, lse_ref, delta_ref)
                dq_dst, dk_dst, dv_dst = dqs_ref, dka_ref, dva_ref
            elif seg == 1:
                srcs = tuple(x.at[r] for x in quads)
                dq_dst, dk_dst, dv_dst = dqs4_ref.at[r], dk4_ref.at[r], dv4_ref.at[r]
            else:
                srcs = tuple(x.at[r] for x in hexes)
                dq_dst, dk_dst, dv_dst = dqs16_ref.at[r], dk16_ref.at[r], dv16_ref.at[r]
            q_src, k_src, v_src, do_src, lse_src, delta_src = srcs
            q2, kb, vb = q_src[qrows, :], _keys(k_src, krows, first), _keys(v_src, krows, first)
            do2, lse2, delta2 = do_src[qrows, :], lse_src[qrows, :], delta_src[qrows, :]
            head0 = _head0_lanes()
            dqs, dk, dv = [], None, None
            for hh in range(2):
                col = slice(hh * HEAD_DIM, hh * HEAD_DIM + 1)
                q, dob = _one_head(q2, head0, hh), _one_head(do2, head0, hh)
                p = jnp.exp(_attn_scores(q, kb, b_ref[_table(seg, first), hh]) - lse2[:, col])
                dvh = _dot_tn(p.astype(BF16), dob)
                ds = p * (_dot_nt(dob, vb) - delta2[:, col])
                if first:
                    db_ref[seg, hh, :, ATTN_BLOCK:] += ds[:, ATTN_BLOCK:]
                else:
                    db_ref[seg, hh] += ds
                dsb = ds.astype(BF16)
                dqs.append(_dot(dsb, kb))
                dkh = _dot_tn(dsb, q)
                dk = dkh if dk is None else dk + dkh
                dv = dvh if dv is None else dv + dvh
            if first:
                dk, dv = dk[ATTN_BLOCK:], dv[ATTN_BLOCK:]
            dq_dst[qrows, :] = jnp.where(head0, dqs[0], dqs[1]) * scale
            if seg == 2:
                dk_dst[krows, :] = dk * scale
                dv_dst[krows, :] = dv
            else:
                dk_dst[krows, :] += dk * scale
                dv_dst[krows, :] += dv

        _attn_schedule(op)
        _interleave_back(dqs16_ref, dqs4_ref, QUAD)
        _interleave_back(dk16_ref, dk4_ref, 0, accumulate=True)
        _interleave_back(dv16_ref, dv4_ref, 0, accumulate=True)

        for r, nat, quad in _quad_tiles():
            dqs_ref[nat, :] += dqs4_ref[r, quad, :] + dqs4_ref[QUAD + r, quad, :]
            dka_ref[nat, :] += dk4_ref[r, quad, :]
            dva_ref[nat, :] += dv4_ref[r, quad, :]

        def merge(i, carry):
            rows = pl.ds(pl.multiple_of(i * MERGE_ROWS, MERGE_ROWS), MERGE_ROWS)
            dq_ref[rows, :] = dqs_ref[rows, :].astype(BF16)
            dk_ref[rows, :] = dka_ref[rows, :].astype(BF16)
            dv_ref[rows, :] = dva_ref[rows, :].astype(BF16)
            return carry

        lax.fori_loop(0, SEQ // MERGE_ROWS, merge, 0)

    def pspec(off):
        return pl.BlockSpec((SEQ, LANE), lambda p, b: (b, off // LANE + p))

    ospec = pl.BlockSpec((SEQ, LANE), lambda p, b: (b, p))
    bspec = pl.BlockSpec((nseg, 2, ATTN_BLOCK, 2 * ATTN_BLOCK), lambda p, b: (0, p, 0, 0))
    gshape = SDS((n, B_WIDTH), BF16)
    return _call(
        body, name=name, grid=(HEAD_PAIRS, nb_local),
        in_specs=[pspec(Q_OFF), pspec(K_OFF), pspec(V_OFF), ospec,
                  pl.BlockSpec((SEQ, LANE), lambda p, b: (b, a_blocks + p)), ospec,
                  pl.BlockSpec((2 * nseg, 2, ATTN_BLOCK, 2 * ATTN_BLOCK), lambda p, b: (0, p, 0, 0))],
        out_specs=[ospec, ospec, ospec, bspec],
        out_shape=[gshape, gshape, gshape, SDS((nseg, B_HEADS, ATTN_BLOCK, 2 * ATTN_BLOCK), F32)],
        scratch_shapes=[pltpu.VMEM((SEQ, LANE), F32)] * 4 + [pltpu.VMEM((QUAD, QUAD_ROWS, LANE), F32)] * 6
        + [pltpu.VMEM((2 * QUAD, QUAD_ROWS, LANE), F32)] + [pltpu.VMEM((QUAD, QUAD_ROWS, LANE), F32)] * 2
        + [pltpu.VMEM((QUAD * QUAD, ATTN_BLOCK, LANE), F32)] * 9,
        args=[proj, proj, proj, b_out, dmix, lse_tot, bias], sem=("arbitrary", "arbitrary"), ride=ride)


PAD = 8
CONV_ROWS = 64


def _conv_taps(gp_ref, head_ref, r0):
    g0 = gp_ref[r0:r0 + CONV_ROWS, :]
    if r0 == 0:
        return g0, head_ref[PAD - 1:PAD - 1 + CONV_ROWS, :], head_ref[PAD - 2:PAD - 2 + CONV_ROWS, :]
    return g0, gp_ref[r0 - 1:r0 - 1 + CONV_ROWS, :], gp_ref[r0 - 2:r0 - 2 + CONV_ROWS, :]


def _fill_head(gp_ref, head_ref):
    head_ref[0:PAD, :] = jnp.zeros((PAD, LANE), F32)
    head_ref[PAD:PAD + CONV_ROWS, :] = gp_ref[0:CONV_ROWS, :]


def conv_gelu_fwd(gp, up, cw, cb, nb_local, name):
    n, f = gp.shape

    def body(gp_ref, up_ref, cw_ref, cb_ref, o_ref, head_ref):
        _fill_head(gp_ref, head_ref)
        w0, w1, w2, bias = cw_ref[0:1, :], cw_ref[1:2, :], cw_ref[2:3, :], cb_ref[...]
        for r0 in range(0, SEQ, CONV_ROWS):
            g0, g1, g2 = _conv_taps(gp_ref, head_ref, r0)
            c = bias + w0 * g2 + w1 * g1 + w2 * g0
            o_ref[r0:r0 + CONV_ROWS, :] = (_gelu(c) * up_ref[r0:r0 + CONV_ROWS, :]).astype(BF16)

    blk = pl.BlockSpec((SEQ, LANE), lambda b, j: (b, j))
    return pl.pallas_call(
        body, name=name, grid=(nb_local, f // LANE),
        in_specs=[blk, blk, pl.BlockSpec((3, LANE), lambda b, j: (0, j)), pl.BlockSpec((1, LANE), lambda b, j: (0, j))],
        out_specs=blk,
        out_shape=SDS((n, f), BF16),
        scratch_shapes=[pltpu.VMEM((PAD + CONV_ROWS, LANE), F32)],
        compiler_params=_params(("parallel", "parallel")),
    )(gp, up, cw, cb)


def conv_gelu_bwd(dgu, gp, up, cw, cb, nb_local, name, ride=None):
    n, f = gp.shape

    def fold(v):
        return jnp.sum(v.reshape(CONV_ROWS // 8, 8, LANE), axis=0)

    def body(dgu_ref, gp_ref, up_ref, cw_ref, cb_ref, dgp_ref, dup_ref, dcw_ref, dcb_ref, head_ref, dc_ref):
        b = pl.program_id(1)
        _fill_head(gp_ref, head_ref)
        dc_ref[SEQ:SEQ + PAD, :] = jnp.zeros((PAD, LANE), F32)
        w0, w1, w2, bias = cw_ref[0:1, :], cw_ref[1:2, :], cw_ref[2:3, :], cb_ref[...]
        sums = [jnp.zeros((8, LANE), F32) for _ in range(4)]
        for r0 in range(0, SEQ, CONV_ROWS):
            rows = slice(r0, r0 + CONV_ROWS)
            g0, g1, g2 = _conv_taps(gp_ref, head_ref, r0)
            gg, dgg = _gelu_and_grad(bias + w0 * g2 + w1 * g1 + w2 * g0)
            dgu = dgu_ref[rows, :].astype(F32)
            dup_ref[rows, :] = (dgu * gg).astype(BF16)
            dc = dgu * up_ref[rows, :] * dgg
            dc_ref[rows, :] = dc
            sums = [sums[0] + fold(dc * g2), sums[1] + fold(dc * g1), sums[2] + fold(dc * g0), sums[3] + fold(dc)]
        for r0 in range(0, SEQ, CONV_ROWS):
            dgp_ref[r0:r0 + CONV_ROWS, :] = (
                w2 * dc_ref[r0:r0 + CONV_ROWS, :] + w1 * dc_ref[r0 + 1:r0 + 1 + CONV_ROWS, :]
                + w0 * dc_ref[r0 + 2:r0 + 2 + CONV_ROWS, :]).astype(BF16)
        dcw = jnp.concatenate([jnp.sum(s, axis=0, keepdims=True) for s in sums[:3]], axis=0)
        dcb = jnp.sum(sums[3], axis=0, keepdims=True)

        @pl.when(b == 0)
        def _():
            dcw_ref[...] = dcw
            dcb_ref[...] = dcb

        @pl.when(b > 0)
        def _():
            dcw_ref[...] += dcw
            dcb_ref[...] += dcb

    blk = pl.BlockSpec((SEQ, LANE), lambda j, b: (b, j))
    return _call(
        body, name=name, grid=(f // LANE, nb_local),
        in_specs=[blk, blk, blk, pl.BlockSpec((3, LANE), lambda j, b: (0, j)), pl.BlockSpec((1, LANE), lambda j, b: (0, j))],
        out_specs=[blk, blk, pl.BlockSpec((3, LANE), lambda j, b: (0, j)), pl.BlockSpec((1, LANE), lambda j, b: (0, j))],
        out_shape=[SDS((n, f), BF16), SDS((n, f), BF16), SDS((3, f), F32), SDS((1, f), F32)],
        scratch_shapes=[pltpu.VMEM((PAD + CONV_ROWS, LANE), F32), pltpu.VMEM((SEQ + PAD, LANE), F32)],
        args=[dgu, gp, up, cw, cb], sem=("parallel", "arbitrary"), ride=ride)


def norm_mid_epilogue(x1, dout, z2, g3, g2):
    n, d = x1.shape

    def fn(dh2, step, x1_ref, dout_ref, z2_ref, g3_ref, g2_ref, dx1_ref, dz2_ref, dg3_ref, dg2_ref):
        dxa, dg3r = _rms_bwd(dh2, x1_ref[...], g3_ref[...])
        dx1 = dout_ref[...] + dxa
        dx1_ref[...] = dx1
        dz2, dg2r = _rms_bwd(dx1, z2_ref[...], g2_ref[...])
        dz2_ref[...] = dz2.astype(BF16)
        _accumulate(dg3_ref, jnp.sum(dg3r, axis=0, keepdims=True), step)
        _accumulate(dg2_ref, jnp.sum(dg2r, axis=0, keepdims=True), step)

    return fn, [x1, dout, z2, g3, g2], [SDS((n, d), F32), SDS((n, d), BF16), SDS((1, d), F32), SDS((1, d), F32)]


def norm_in_epilogue(x, dx1, g1):
    n, d = x.shape

    def fn(dh1, step, x_ref, dx1_ref, g1_ref, dx_ref, dg1_ref):
        dxa, dgr = _rms_bwd(dh1, x_ref[...], g1_ref[...])
        dx_ref[...] = dx1_ref[...] + dxa
        _accumulate(dg1_ref, jnp.sum(dgr, axis=0, keepdims=True), step)

    return fn, [x, dx1, g1], [SDS((n, d), F32), SDS((1, d), F32)]


def cast_bf16(arrays, name):
    def body(*refs):
        for i_ref, o_ref in zip(refs[:len(arrays)], refs[len(arrays):]):
            o_ref[...] = i_ref[...].astype(BF16)

    return pl.pallas_call(body, name=name, out_shape=[SDS(a.shape, BF16) for a in arrays],
                          compiler_params=_params())(*arrays)


def adam_update(parts, w, m, v, name, tr=None):
    s, r, c = parts.shape
    tr = r if tr is None else tr
    bc1 = 1.0 - ADAM_B1 ** ADAM_STEP
    bc2 = 1.0 - ADAM_B2 ** ADAM_STEP

    def body(p_ref, w_ref, m_ref, v_ref, g_ref, d_ref, nm_ref, nv_ref):
        g = p_ref[0].astype(F32)
        for j in range(1, s):
            g = g + p_ref[j].astype(F32)
        nm = ADAM_B1 * m_ref[...] + (1.0 - ADAM_B1) * g
        nv = ADAM_B2 * v_ref[...] + (1.0 - ADAM_B2) * (g * g)
        g_ref[...] = g
        nm_ref[...] = nm
        nv_ref[...] = nv
        d_ref[...] = -ADAM_LR * ((nm / bc1) / (jnp.sqrt(nv / bc2) + ADAM_EPS) + ADAM_WD * w_ref[...])

    blk = pl.BlockSpec((tr, c), lambda i: (i, 0))
    return pl.pallas_call(
        body, name=name, grid=(r // tr,),
        in_specs=[pl.BlockSpec((s, tr, c), lambda i: (0, i, 0)), blk, blk, blk],
        out_specs=[blk] * 4, out_shape=[SDS((r, c), F32)] * 4,
        compiler_params=_params(("parallel",)),
    )(parts, w, m, v)


EARLY_NAMES = ("spatial_w", "norm_mix_post", "norm_ffn_pre", "norm_ffn_post", "conv_b", "ln_v_gain", "ln_v_bias",
               "spatial_b")
LATE_NAMES = ("norm_mix_pre", "rel_bias")
PACK_ROW_ALIGN = 8


def _pack_rows(size):
    rows = -(-size // LANE)
    return -(-rows // PACK_ROW_ALIGN) * PACK_ROW_ALIGN


def _pack(arrays):
    flat = []
    for a in arrays:
        rows = _pack_rows(a.size)
        flat.append(jnp.pad(a.reshape(-1), (0, rows * LANE - a.size)))
    return jnp.concatenate(flat).reshape(-1, LANE)


def _unpack(packed, shapes):
    out, row = [], 0
    for shp in shapes:
        size = int(np.prod(shp))
        out.append(packed[row:row + _pack_rows(size)].reshape(-1)[:size].reshape(shp))
        row += _pack_rows(size)
    return out


def kernel(x, norm_mix_pre, norm_mix_post, norm_ffn_pre, norm_ffn_post, w_in, ln_v_gain, ln_v_bias, spatial_w, spatial_b, rel_bias, w_out, w_gate, w_up, conv_w, conv_b, w_down, loss_target, m_norm_mix_pre, m_norm_mix_post, m_norm_ffn_pre, m_norm_ffn_post, m_w_in, m_ln_v_gain, m_ln_v_bias, m_spatial_w, m_spatial_b, m_rel_bias, m_w_out, m_w_gate, m_w_up, m_conv_w, m_conv_b, m_w_down, v_norm_mix_pre, v_norm_mix_post, v_norm_ffn_pre, v_norm_ffn_post, v_w_in, v_ln_v_gain, v_ln_v_bias, v_spatial_w, v_spatial_b, v_rel_bias, v_w_out, v_w_gate, v_w_up, v_conv_w, v_conv_b, v_w_down):
    given = dict(locals())
    nb_local, seq, d = x.shape
    n = nb_local * seq
    cols = w_in.shape[2]

    def by_columns(g):
        return g.transpose(1, 0, 2).reshape(g.shape[1], N_DEV * g.shape[2])

    def to_blocks(g):
        return g.reshape(g.shape[0], N_DEV, cols).transpose(1, 0, 2)

    xf, target = x.reshape(n, d), loss_target.reshape(n, d)
    ln_g, ln_b = ln_v_gain.reshape(1, A_WIDTH), ln_v_bias.reshape(1, A_WIDTH)
    spatial_bt, rel_bias_t = spatial_b[0].T, rel_bias.T

    s_in, s_out, s_gate, s_up, s_down = cast_bf16([w_in[0], w_out[0], w_gate[0], w_up[0], w_down[0]], "cast_shards")
    g_in, g_cw = exchange([], [s_in, conv_w[0]], "gather_w_in")
    w_in_f, conv_w_f = by_columns(g_in), by_columns(g_cw)

    (h1, proj), _ = norm_mm(xf, norm_mix_pre, [w_in_f], "fwd_norm_in", tn=IN_COLS)
    a = gating_fwd(proj, ln_g, ln_b, spatial_w[0], spatial_bt, "fwd_gating")
    bias = bias_tables(rel_bias_t, "bias_tables").reshape(2 * len(DILATIONS), B_HEADS, ATTN_BLOCK, 2 * ATTN_BLOCK)
    (b_out, lse_tot), (g_out, g_gate, g_up) = attn_fwd(proj, bias, nb_local, "fwd_attn",
                                                       ride=([], [s_out, s_gate, s_up]))
    w_out_f, w_gate_f, w_up_f = g_out.reshape(D_MODEL, D_MODEL), by_columns(g_gate), by_columns(g_up)
    z2, x1 = mm_res_norm([a, b_out], w_out_f, xf, norm_mix_post, "fwd_out_norm")
    (h2, gp, up), (g_down,) = norm_mm(x1, norm_ffn_pre, [w_gate_f, w_up_f], "fwd_norm_ffn", tm=256, tn=D_FF,
                                      ride=([], [s_down]))
    w_down_f = g_down.reshape(D_FF, D_MODEL)
    gu = conv_gelu_fwd(gp, up, conv_w_f, conv_b, nb_local, "fwd_conv_gelu")
    dy, dout, dg4, loss_part = down_loss(gu, w_down_f, x1, norm_ffn_post, target, "fwd_down_loss")

    p_down = mm_tn([gu], [dy], "bwd_dw_down", t1=256, t2=D_MODEL)
    (dgu,), _ = mm_nt([(dy, 0, 0)], [w_down_f], "bwd_dgu", out_dtype=BF16)
    (dgp, dup, p_conv_w, p_conv_b), (r_down,) = conv_gelu_bwd(
        dgu, gp, up, conv_w_f, conv_b, nb_local, "bwd_conv_gelu", ride=([p_down.reshape(N_DEV, cols, D_MODEL)], []))
    p_gate = mm_tn([h2], [dgp], "bwd_dw_gate", t1=D_MODEL, t2=256)
    p_up = mm_tn([h2], [dup], "bwd_dw_up", t1=D_MODEL, t2=256)
    (dx1, dz2, dg3, dg2), _ = mm_nt([(dgp, 0, 0), (dup, 1, 0)], [w_gate_f, w_up_f], "bwd_dh2_norm_mid", tm=256,
                                    epilogue=norm_mid_epilogue(x1, dout, z2, norm_ffn_pre, norm_mix_post))
    p_out = mm_tn([a, b_out], [dz2], "bwd_dw_out", t1=256, t2=D_MODEL)
    (dmix,), _ = mm_nt([(dz2, 0, 0)], [w_out_f], "bwd_dmix")
    duv, p_ws, p_sbt, p_lng, p_lnb = gating_bwd(proj, dmix, ln_g, ln_b, spatial_w[0], spatial_bt, "bwd_gating")
    small = dict(spatial_w=p_ws, norm_mix_post=dg2, norm_ffn_pre=dg3, norm_ffn_post=dg4, conv_b=p_conv_b,
                 ln_v_gain=p_lng, ln_v_bias=p_lnb, spatial_b=p_sbt.T)
    pack_early = _pack([small[k] for k in EARLY_NAMES] + [p_conv_w, loss_part])
    (dq, dk, dv, dbias), (r_gate, r_up, r_out, r_early) = attn_bwd(
        proj, b_out, dmix, lse_tot, bias, nb_local, "bwd_attn",
        ride=([to_blocks(p_gate), to_blocks(p_up), p_out.reshape(N_DEV, D_MODEL // N_DEV, D_MODEL)], [pack_early]))
    p_rel_bias_t = rel_bias_grad(dbias.reshape(len(DILATIONS), B_HEADS, BIAS_SIZE), "bwd_rel_bias")
    p_in = mm_tn([h1], [duv, dq, dk, dv], "bwd_dw_in", t1=D_MODEL, t2=256)
    (grad_x, dg1), (r_in,) = mm_nt(
        [(duv, 0, 0), (dq, 0, Q_OFF), (dk, 0, K_OFF), (dv, 0, V_OFF)], [w_in_f], "bwd_dh1_norm_in",
        epilogue=norm_in_epilogue(xf, dx1, norm_mix_pre), ride=([to_blocks(p_in)], []))
    small.update(norm_mix_pre=dg1, rel_bias=p_rel_bias_t.T)
    (r_late,) = exchange([], [_pack([small[k] for k in LATE_NAMES])], "exchange_late")

    res = {}
    res["w_in"] = adam_update(r_in, w_in[0], m_w_in[0], v_w_in[0], "adam_w_in", tr=256)
    res["w_out"] = adam_update(r_out, w_out[0], m_w_out[0], v_w_out[0], "adam_w_out")
    res["w_gate"] = adam_update(r_gate, w_gate[0], m_w_gate[0], v_w_gate[0], "adam_w_gate", tr=256)
    res["w_up"] = adam_update(r_up, w_up[0], m_w_up[0], v_w_up[0], "adam_w_up", tr=256)
    res["w_down"] = adam_update(r_down, w_down[0], m_w_down[0], v_w_down[0], "adam_w_down", tr=176)

    def adam_packed(received, names, tail, name):
        zeros = [jnp.zeros_like(t) for t in tail]
        packs = [_pack([given[pre + k] for k in names] + zeros) for pre in ("", "m_", "v_")]
        shapes = [given[k].shape for k in names] + [t.shape for t in tail]
        unpacked = [_unpack(p, shapes) for p in adam_update(received, *packs, name)]
        for i, k in enumerate(names):
            res[k] = [u[i] for u in unpacked]
        return unpacked[0][len(names):]

    g_conv_w_full, loss_sum = adam_packed(r_early, EARLY_NAMES, [p_conv_w, loss_part], "adam_small_early")
    adam_packed(r_late, LATE_NAMES, [], "adam_small_late")
    g_conv_w = lax.dynamic_slice_in_dim(g_conv_w_full, _my_index() * cols, cols, axis=1)
    res["conv_w"] = adam_update(g_conv_w[None], conv_w[0], m_conv_w[0], v_conv_w[0], "adam_conv_w")
    loss = loss_sum[0, 0]

    names = ("norm_mix_pre", "norm_mix_post", "norm_ffn_pre", "norm_ffn_post", "w_in", "ln_v_gain", "ln_v_bias",
             "spatial_w", "spatial_b", "rel_bias", "w_out", "w_gate", "w_up", "conv_w", "conv_b", "w_down")
    outs = [loss, grad_x.reshape(x.shape)]
    for t in range(4):
        outs += [res[k][t].reshape(given[k].shape) for k in names]
    return tuple(outs)
```

```python
import functools
import math

import numpy as np
import jax
import jax.numpy as jnp
from jax import lax
from jax.experimental import pallas as pl
from jax.experimental.pallas import tpu as pltpu

F32 = jnp.float32
BF16 = jnp.bfloat16
SDS = jax.ShapeDtypeStruct

D_MODEL = 1024
SEQ = 2048
HEAD_DIM = 64
A_GROUPS = 4
A_WIDTH = A_GROUPS * HEAD_DIM
B_HEADS = 12
B_WIDTH = B_HEADS * HEAD_DIM
HEAD_PAIRS = B_HEADS // 2
CHUNK = 128
ATTN_BLOCK = 128
DILATIONS = (1, 4, 16)
NUM_BUCKETS = 32
MAX_DISTANCE = 2048
D_FF = 2816
IN_COLS = 2 * A_WIDTH + 3 * B_WIDTH
Q_OFF = 2 * A_WIDTH
K_OFF = Q_OFF + B_WIDTH
V_OFF = K_OFF + B_WIDTH
NORM_EPS = 1e-6
NEG_INF = -1e30
N_DEV = 8
LANE = 128

ADAM_LR = 0.001
ADAM_B1 = 0.9
ADAM_B2 = 0.999
ADAM_EPS = 1e-08
ADAM_WD = 0.01
ADAM_STEP = 10

GELU_C0 = math.sqrt(2.0 / math.pi)
GELU_C1 = 0.044715

VMEM_LIMIT = 56 * 1024 * 1024


def _params(sem=None):
    if sem is None:
        return pltpu.CompilerParams(vmem_limit_bytes=VMEM_LIMIT)
    return pltpu.CompilerParams(dimension_semantics=sem, vmem_limit_bytes=VMEM_LIMIT)


def _gelu(x):
    t = jnp.tanh(GELU_C0 * (x + GELU_C1 * x * x * x))
    return 0.5 * x * (1.0 + t)


def _gelu_and_grad(x):
    x2 = x * x
    t = jnp.tanh(GELU_C0 * (x + GELU_C1 * x * x2))
    g = 0.5 * x * (1.0 + t)
    dg = 0.5 * (1.0 + t) + 0.5 * x * (1.0 - t * t) * (GELU_C0 * (1.0 + 3.0 * GELU_C1 * x2))
    return g, dg


def _dot(a, b):
    return jnp.dot(a, b, preferred_element_type=F32)


def _dot_nt(a, b):
    return lax.dot_general(a, b, (((1,), (1,)), ((), ())), preferred_element_type=F32)


def _dot_tn(a, b):
    return lax.dot_general(a, b, (((0,), (0,)), ((), ())), preferred_element_type=F32)


def _rms_bwd(d, xin, g):
    r = lax.rsqrt(jnp.mean(xin * xin, axis=-1, keepdims=True) + NORM_EPS)
    xh = xin * r
    gd = g * d
    dx = r * (gd - xh * jnp.mean(gd * xh, axis=-1, keepdims=True))
    return dx, d * xh


MESH = pl.DeviceIdType.MESH
ANY = pl.BlockSpec(memory_space=pl.ANY)
PEER_MASKS = tuple(range(1, N_DEV))


def _my_index():
    return lax.axis_index("x") * 4 + lax.axis_index("y") * 2 + lax.axis_index("c")


def _peer(mask):
    x, y, c = lax.axis_index("x"), lax.axis_index("y"), lax.axis_index("c")
    px = 1 - x if mask & 4 else x
    py = 1 - y if mask & 2 else y
    pc = 1 - c if mask & 1 else c
    return (px, py, pc), px * 4 + py * 2 + pc


RELAY_AT = 3
SIBLING = 1
CHIP_MASKS = (2, 4, 6)


class _Exchange:
    def __init__(self, nblocked, in_refs, out_refs, sems):
        send_sems, recv_sems, local_sems = sems
        me = _my_index()
        sibling, _ = _peer(SIBLING)
        self.local, self.first, self.relays, self.relayed_in, self.last_in = [], [], [], [], []
        for a, (in_ref, out_ref) in enumerate(zip(in_refs, out_refs)):
            def copy(src, slot, mask, to):
                return pltpu.make_async_remote_copy(
                    src_ref=src, dst_ref=out_ref.at[slot], send_sem=send_sems.at[a, mask - 1],
                    recv_sem=recv_sems.at[a, mask - 1], device_id=to, device_id_type=MESH)

            if a < nblocked:
                self.local.append(pltpu.make_async_copy(in_ref.at[me], out_ref.at[me], local_sems.at[a]))
                for mask in PEER_MASKS:
                    peer, pidx = _peer(mask)
                    self.first.append(copy(in_ref.at[pidx], me, mask, peer))
                    self.last_in.append(copy(in_ref.at[pidx], pidx, mask, peer))
                continue
            self.local.append(pltpu.make_async_copy(in_ref, out_ref.at[me], local_sems.at[a]))
            for mask in (SIBLING,) + CHIP_MASKS:
                peer, pidx = _peer(mask)
                self.first.append(copy(in_ref, me, mask, peer))
                (self.last_in if mask == SIBLING else self.relayed_in).append(copy(in_ref, pidx, mask, peer))
            for mask in CHIP_MASKS:
                _, origin = _peer(mask)
                _, far = _peer(mask | SIBLING)
                self.relays.append(copy(out_ref.at[origin], origin, mask | SIBLING, sibling))
                self.last_in.append(copy(in_ref, far, mask | SIBLING, sibling))

    def start(self):
        for cp in self.local + self.first[::-1]:
            cp.start()

    def relay(self):
        for arrived, onward in zip(self.relayed_in, self.relays):
            arrived.wait_recv()
            onward.start()

    def finish(self):
        for cp in self.first + self.relays:
            cp.wait_send()
        for cp in self.last_in:
            cp.wait_recv()
        for cp in self.local:
            cp.wait()


def _exchange_out_shape(blocked, whole):
    return [SDS(b.shape, b.dtype) for b in blocked] + [SDS((N_DEV,) + w.shape, w.dtype) for w in whole]


def _exchange_sems(n):
    return [pltpu.SemaphoreType.DMA((n, N_DEV - 1)), pltpu.SemaphoreType.DMA((n, N_DEV - 1)),
            pltpu.SemaphoreType.DMA((n,))]


def exchange(blocked, whole, name):
    nb, n = len(blocked), len(blocked) + len(whole)

    def body(*refs):
        ex = _Exchange(nb, refs[:n], refs[n:2 * n], refs[2 * n:])
        ex.start()
        ex.relay()
        ex.finish()

    return pl.pallas_call(
        body, name=name, in_specs=[ANY] * n, out_specs=[ANY] * n, out_shape=_exchange_out_shape(blocked, whole),
        scratch_shapes=_exchange_sems(n),
    )(*blocked, *whole)


def _call(body, *, name, grid, in_specs, out_specs, out_shape, args, scratch_shapes=(), sem=None, ride=None):
    out_shape, out_specs, scratch_shapes = list(out_shape), list(out_specs), list(scratch_shapes)
    if ride is None:
        outs = pl.pallas_call(body, name=name, grid=grid, in_specs=list(in_specs), out_specs=out_specs,
                              out_shape=out_shape, scratch_shapes=scratch_shapes,
                              compiler_params=_params(sem))(*args)
        return list(outs), []
    blocked, whole = ride
    cargs = list(blocked) + list(whole)
    nb, nc = len(blocked), len(cargs)
    n_in, n_out, n_scr = len(args), len(out_shape), len(scratch_shapes)
    steps = math.prod(grid)
    assert steps >= 3, grid

    def riding(*refs):
        ins, refs = refs[:n_in], refs[n_in:]
        cins, refs = refs[:nc], refs[nc:]
        outs, refs = refs[:n_out], refs[n_out:]
        couts, refs = refs[:nc], refs[nc:]
        scr, sems = refs[:n_scr], refs[n_scr:]
        step = functools.reduce(lambda acc, k: acc * grid[k] + pl.program_id(k), range(len(grid)), 0)

        @pl.when(step == 0)
        def _():
            _Exchange(nb, cins, couts, sems).start()

        @pl.when(step == RELAY_AT * steps // 4)
        def _():
            _Exchange(nb, cins, couts, sems).relay()

        body(*ins, *outs, *scr)

        @pl.when(step == steps - 1)
        def _():
            _Exchange(nb, cins, couts, sems).finish()

    res = pl.pallas_call(
        riding, name=name, grid=grid, in_specs=list(in_specs) + [ANY] * nc, out_specs=out_specs + [ANY] * nc,
        out_shape=out_shape + _exchange_out_shape(blocked, whole),
        scratch_shapes=scratch_shapes + _exchange_sems(nc),
        compiler_params=_params(("arbitrary",) * len(grid)))(*args, *cargs)
    return list(res[:n_out]), list(res[n_out:])


def norm_mm(x, g, ws, name, tm=512, tn=1408, ride=None):
    n, d = x.shape
    f = ws[0].shape[1]
    nw = len(ws)

    def body(x_ref, g_ref, *refs):
        w_refs = refs[:nw]
        h_ref = refs[nw]
        o_refs = refs[nw + 1:]

        @pl.when(pl.program_id(1) == 0)
        def _():
            xv = x_ref[...]
            r = lax.rsqrt(jnp.mean(xv * xv, axis=-1, keepdims=True) + NORM_EPS)
            h_ref[...] = (xv * r * g_ref[...]).astype(BF16)

        h = h_ref[...]
        for w_ref, o_ref in zip(w_refs, o_refs):
            o_ref[...] = _dot(h, w_ref[...])

    return _call(
        body, name=name, grid=(n // tm, f // tn),
        in_specs=[pl.BlockSpec((tm, d), lambda i, j: (i, 0)), pl.BlockSpec((1, d), lambda i, j: (0, 0))]
        + [pl.BlockSpec((d, tn), lambda i, j: (0, j)) for _ in ws],
        out_specs=[pl.BlockSpec((tm, d), lambda i, j: (i, 0))]
        + [pl.BlockSpec((tm, tn), lambda i, j: (i, j)) for _ in ws],
        out_shape=[SDS((n, d), BF16)] + [SDS((n, f), F32) for _ in ws],
        args=[x, g, *ws], sem=("parallel", "arbitrary"), ride=ride)


def _lane_concat(refs):
    vals = [r[...].astype(BF16) for r in refs]
    return vals[0] if len(vals) == 1 else jnp.concatenate(vals, axis=1)


def mm_res_norm(a_list, w, res, g, name, tm=512):
    n = a_list[0].shape[0]
    k, d = w.shape
    na = len(a_list)

    def body(*refs):
        w_ref, res_ref, g_ref, y_ref, o_ref = refs[na:]
        y = _dot(_lane_concat(refs[:na]), w_ref[...])
        r = lax.rsqrt(jnp.mean(y * y, axis=-1, keepdims=True) + NORM_EPS)
        y_ref[...] = y
        o_ref[...] = res_ref[...] + y * r * g_ref[...]

    return pl.pallas_call(
        body, name=name, grid=(n // tm,),
        in_specs=[pl.BlockSpec((tm, a.shape[1]), lambda i: (i, 0)) for a in a_list]
        + [pl.BlockSpec((k, d), lambda i: (0, 0)),
           pl.BlockSpec((tm, d), lambda i: (i, 0)), pl.BlockSpec((1, d), lambda i: (0, 0))],
        out_specs=[pl.BlockSpec((tm, d), lambda i: (i, 0)), pl.BlockSpec((tm, d), lambda i: (i, 0))],
        out_shape=[SDS((n, d), F32), SDS((n, d), F32)],
        compiler_params=_params(("parallel",)),
    )(*a_list, w, res, g)


def down_loss(a, w, res, g, target, name, tm=256):
    n, k = a.shape
    d = w.shape[1]
    inv_d = 1.0 / d

    def body(a_ref, w_ref, res_ref, g_ref, t_ref, dy_ref, dout_ref, dg_ref, loss_ref):
        i = pl.program_id(0)
        y = _dot(a_ref[...], w_ref[...])
        gv = g_ref[...]
        r = lax.rsqrt(jnp.mean(y * y, axis=-1, keepdims=True) + NORM_EPS)
        yh = y * r
        e = res_ref[...] + yh * gv - t_ref[...]
        part = 0.5 * inv_d * jnp.sum(jnp.sum(e * e, axis=-1, keepdims=True), axis=0, keepdims=True)
        dout = e * inv_d
        dout_ref[...] = dout
        gd = gv * dout
        dy_ref[...] = (r * (gd - yh * jnp.mean(gd * yh, axis=-1, keepdims=True))).astype(BF16)
        dgp = jnp.sum(dout * yh, axis=0, keepdims=True)
        lane0 = lax.broadcasted_iota(jnp.int32, (1, LANE), 1) == 0
        lp = jnp.where(lane0, part, 0.0)

        @pl.when(i == 0)
        def _():
            dg_ref[...] = dgp
            loss_ref[...] = lp

        @pl.when(i > 0)
        def _():
            dg_ref[...] += dgp
            loss_ref[...] += lp

    return pl.pallas_call(
        body, name=name, grid=(n // tm,),
        in_specs=[pl.BlockSpec((tm, k), lambda i: (i, 0)), pl.BlockSpec((k, d), lambda i: (0, 0)),
                  pl.BlockSpec((tm, d), lambda i: (i, 0)), pl.BlockSpec((1, d), lambda i: (0, 0)),
                  pl.BlockSpec((tm, d), lambda i: (i, 0))],
        out_specs=[pl.BlockSpec((tm, d), lambda i: (i, 0)), pl.BlockSpec((tm, d), lambda i: (i, 0)),
                   pl.BlockSpec((1, d), lambda i: (0, 0)), pl.BlockSpec((1, LANE), lambda i: (0, 0))],
        out_shape=[SDS((n, d), BF16), SDS((n, d), F32), SDS((1, d), F32), SDS((1, LANE), F32)],
        compiler_params=_params(("arbitrary",)),
    )(a, w, res, g, target)


def _accumulate(ref, val, step):
    @pl.when(step == 0)
    def _():
        ref[...] = val

    @pl.when(step > 0)
    def _():
        ref[...] += val


def mm_nt(terms, ws, name, tm=512, out_dtype=F32, ride=None, epilogue=None):
    n = terms[0][0].shape[0]
    r = ws[0].shape[0]
    na = len(terms)
    meta = [(widx, off, a.shape[1]) for a, widx, off in terms]
    fn, extras, out_shape = epilogue if epilogue else (None, [], [SDS((n, r), out_dtype)])
    n_fixed = na + len(ws)

    def body(*refs):
        a_refs = refs[:na]
        w_refs = refs[na:n_fixed]
        acc = None
        for a_ref, (widx, off, k) in zip(a_refs, meta):
            p = _dot_nt(a_ref[...].astype(BF16), w_refs[widx][:, off:off + k])
            acc = p if acc is None else acc + p
        if fn is None:
            refs[-1][...] = acc.astype(out_dtype)
        else:
            fn(acc, pl.program_id(0), *refs[n_fixed:])

    def spec(a):
        if a.shape[0] == 1:
            return pl.BlockSpec(a.shape, lambda i: (0, 0))
        return pl.BlockSpec((tm, a.shape[1]), lambda i: (i, 0))

    return _call(
        body, name=name, grid=(n // tm,),
        in_specs=[spec(a) for a, _, _ in terms] + [pl.BlockSpec(w.shape, lambda i: (0, 0)) for w in ws]
        + [spec(e) for e in extras],
        out_specs=[spec(o) for o in out_shape], out_shape=out_shape,
        args=[a for a, _, _ in terms] + list(ws) + list(extras),
        sem=("parallel",) if fn is None else ("arbitrary",), ride=ride)


def _piece_blocks(pieces, tile):
    out, first = [], 0
    for p in pieces:
        nblk, rem = divmod(p.shape[1], tile)
        assert rem == 0, (p.shape, tile)
        out.append((first, nblk))
        first += nblk
    return out, first


def mm_tn(lhs_list, rhs_list, name, t1, t2, out_dtype=BF16):
    n = lhs_list[0].shape[0]
    lblocks, nbl = _piece_blocks(lhs_list, t1)
    rblocks, nbr = _piece_blocks(rhs_list, t2)
    nl = len(lhs_list)

    def body(*refs):
        l_refs, r_refs, o_ref = refs[:nl], refs[nl:-1], refs[-1]
        i, j = pl.program_id(0), pl.program_id(1)
        for l_ref, (ls, ln) in zip(l_refs, lblocks):
            for r_ref, (rs, rn) in zip(r_refs, rblocks):
                @pl.when((i >= ls) & (i < ls + ln) & (j >= rs) & (j < rs + rn))
                def _(l_ref=l_ref, r_ref=r_ref):
                    o_ref[...] = _dot_tn(l_ref[...].astype(BF16), r_ref[...].astype(BF16)).astype(out_dtype)

    def piece_spec(tile, axis, first, nblk):
        def index(i, j):
            return 0, jnp.clip((i, j)[axis] - first, 0, nblk - 1)
        return pl.BlockSpec((n, tile), index)

    return pl.pallas_call(
        body, name=name, grid=(nbl, nbr),
        in_specs=[piece_spec(t1, 0, *b) for b in lblocks] + [piece_spec(t2, 1, *b) for b in rblocks],
        out_specs=pl.BlockSpec((t1, t2), lambda i, j: (i, j)),
        out_shape=SDS((nbl * t1, nbr * t2), out_dtype),
        compiler_params=_params(("parallel", "arbitrary")),
    )(*lhs_list, *rhs_list)


GATE_ROWS = 512


def _tril_mask():
    row = lax.broadcasted_iota(jnp.int32, (CHUNK, CHUNK), 0)
    col = lax.broadcasted_iota(jnp.int32, (CHUNK, CHUNK), 1)
    return row >= col


def _group_of(shape, axis):
    return lax.broadcasted_iota(jnp.int32, shape, axis) // HEAD_DIM


def _group_mean_matrix():
    same = _group_of((A_WIDTH, A_WIDTH), 0) == _group_of((A_WIDTH, A_WIDTH), 1)
    return jnp.where(same, 1.0 / HEAD_DIM, 0.0).astype(F32)


def _dot_f32(a, b):
    return jnp.dot(a, b, preferred_element_type=F32, precision=lax.Precision.HIGHEST)


def _by_group(parts, lane_group):
    out = parts[A_GROUPS - 1]
    for g in range(A_GROUPS - 2, -1, -1):
        out = jnp.where(lane_group == g, parts[g], out)
    return out


def _group_norm(gv, gmean):
    xc = gv - _dot_f32(gv, gmean)
    rstd = lax.rsqrt(_dot_f32(xc * xc, gmean) + NORM_EPS)
    return xc * rstd, rstd


def gating_fwd(proj, lng, lnb, ws, sbt, name):
    n = proj.shape[0]

    def body(u_ref, v_ref, lng_ref, lnb_ref, ws_ref, sbt_ref, a_ref):
        tril = _tril_mask()
        lane_group = _group_of((CHUNK, A_WIDTH), 1)
        gmean = _group_mean_matrix()
        wts = [jnp.where(tril, ws_ref[g], 0.0).astype(BF16) for g in range(A_GROUPS)]
        sb = _by_group([sbt_ref[:, g:g + 1] for g in range(A_GROUPS)], lane_group)

        def chunk(c, carry):
            rows = pl.ds(pl.multiple_of(c * CHUNK, CHUNK), CHUNK)
            vhat, _ = _group_norm(_gelu(v_ref[rows, :]), gmean)
            vn = (vhat * lng_ref[...] + lnb_ref[...]).astype(BF16)
            z = _by_group([_dot(wt, vn) for wt in wts], lane_group) + sb
            a_ref[rows, :] = _gelu(u_ref[rows, :]) * z
            return carry

        lax.fori_loop(0, GATE_ROWS // CHUNK, chunk, 0)

    return pl.pallas_call(
        body, name=name, grid=(n // GATE_ROWS,),
        in_specs=[pl.BlockSpec((GATE_ROWS, A_WIDTH), lambda i: (i, 0)),
                  pl.BlockSpec((GATE_ROWS, A_WIDTH), lambda i: (i, 1)),
                  pl.BlockSpec((1, A_WIDTH), lambda i: (0, 0)), pl.BlockSpec((1, A_WIDTH), lambda i: (0, 0)),
                  pl.BlockSpec((A_GROUPS, CHUNK, CHUNK), lambda i: (0, 0, 0)),
                  pl.BlockSpec((CHUNK, A_GROUPS), lambda i: (0, 0))],
        out_specs=pl.BlockSpec((GATE_ROWS, A_WIDTH), lambda i: (i, 0)),
        out_shape=SDS((n, A_WIDTH), F32),
        compiler_params=_params(("parallel",)),
    )(proj, proj, lng, lnb, ws, sbt)


def gating_bwd(proj, dmix, lng, lnb, ws, sbt, name):
    n = proj.shape[0]

    def body(u_ref, v_ref, da_ref, lng_ref, lnb_ref, ws_ref, sbt_ref,
             duv_ref, dws_ref, dsbt_ref, dlng_ref, dlnb_ref):
        @pl.when(pl.program_id(0) == 0)
        def _():
            dws_ref[...] = jnp.zeros_like(dws_ref)
            dsbt_ref[...] = jnp.zeros_like(dsbt_ref)
            dlng_ref[...] = jnp.zeros_like(dlng_ref)
            dlnb_ref[...] = jnp.zeros_like(dlnb_ref)

        tril = _tril_mask()
        lane_group = _group_of((CHUNK, A_WIDTH), 1)
        gmean = _group_mean_matrix()
        gsum = (_group_of((A_WIDTH, LANE), 0) == lax.broadcasted_iota(jnp.int32, (A_WIDTH, LANE), 1)).astype(F32)
        wts = [jnp.where(tril, ws_ref[g], 0.0) for g in range(A_GROUPS)]
        wts_b = [w.astype(BF16) for w in wts]
        wts_t = [w.T.astype(BF16) for w in wts]
        sb = _by_group([sbt_ref[:, g:g + 1] for g in range(A_GROUPS)], lane_group)
        lg = lng_ref[...]

        def chunk(c, carry):
            rows = pl.ds(pl.multiple_of(c * CHUNK, CHUNK), CHUNK)
            gu, dgu_dx = _gelu_and_grad(u_ref[rows, :])
            gv, dgv_dx = _gelu_and_grad(v_ref[rows, :])
            vhat, rstd = _group_norm(gv, gmean)
            vn = (vhat * lg + lnb_ref[...]).astype(BF16)
            z = _by_group([_dot(wt, vn) for wt in wts_b], lane_group) + sb
            da = da_ref[rows, :]
            dz = da * gu
            dzb = dz.astype(BF16)
            duv_ref[rows, 0:A_WIDTH] = (da * z * dgu_dx).astype(BF16)
            dsbt_ref[...] += _dot_f32(dz, gsum)[:, 0:A_GROUPS]
            for g in range(A_GROUPS):
                dz_g = jnp.where(lane_group == g, dzb, jnp.zeros_like(dzb))
                dws_ref[g] += jnp.where(tril, _dot_nt(dz_g, vn), 0.0)
            dvn = _by_group([_dot(wt, dzb) for wt in wts_t], lane_group)
            dlng_ref[...] += jnp.sum(dvn * vhat, axis=0, keepdims=True)
            dlnb_ref[...] += jnp.sum(dvn, axis=0, keepdims=True)
            dvh = dvn * lg
            dgv = rstd * (dvh - _dot_f32(dvh, gmean) - vhat * _dot_f32(dvh * vhat, gmean))
            duv_ref[rows, A_WIDTH:2 * A_WIDTH] = (dgv * dgv_dx).astype(BF16)
            return carry

        lax.fori_loop(0, GATE_ROWS // CHUNK, chunk, 0)

    return pl.pallas_call(
        body, name=name, grid=(n // GATE_ROWS,),
        in_specs=[pl.BlockSpec((GATE_ROWS, A_WIDTH), lambda i: (i, 0)),
                  pl.BlockSpec((GATE_ROWS, A_WIDTH), lambda i: (i, 1)),
                  pl.BlockSpec((GATE_ROWS, A_WIDTH), lambda i: (i, 0)),
                  pl.BlockSpec((1, A_WIDTH), lambda i: (0, 0)), pl.BlockSpec((1, A_WIDTH), lambda i: (0, 0)),
                  pl.BlockSpec((A_GROUPS, CHUNK, CHUNK), lambda i: (0, 0, 0)),
                  pl.BlockSpec((CHUNK, A_GROUPS), lambda i: (0, 0))],
        out_specs=[pl.BlockSpec((GATE_ROWS, 2 * A_WIDTH), lambda i: (i, 0)),
                   pl.BlockSpec((A_GROUPS, CHUNK, CHUNK), lambda i: (0, 0, 0)),
                   pl.BlockSpec((CHUNK, A_GROUPS), lambda i: (0, 0)),
                   pl.BlockSpec((1, A_WIDTH), lambda i: (0, 0)), pl.BlockSpec((1, A_WIDTH), lambda i: (0, 0))],
        out_shape=[SDS((n, 2 * A_WIDTH), BF16), SDS((A_GROUPS, CHUNK, CHUNK), F32), SDS((CHUNK, A_GROUPS), F32),
                   SDS((1, A_WIDTH), F32), SDS((1, A_WIDTH), F32)],
        compiler_params=_params(("arbitrary",)),
    )(proj, proj, dmix, lng, lnb, ws, sbt)


def _t5_bucket_np(dist):
    max_exact = NUM_BUCKETS // 2
    dd = np.maximum(dist, 1).astype(np.float64)
    large = max_exact + np.log(dd / max_exact) / math.log(MAX_DISTANCE / max_exact) * (NUM_BUCKETS - max_exact)
    large = np.minimum(large.astype(np.int64), NUM_BUCKETS - 1)
    return np.where(dist < max_exact, dist, large)


def _bucket_tables(with_first):
    i = np.arange(ATTN_BLOCK)[:, None]
    j = np.arange(2 * ATTN_BLOCK)[None, :]
    rel = ATTN_BLOCK + i - j
    band = (rel >= 0) & (rel <= ATTN_BLOCK)
    tabs = []
    for own_only in (False, True) if with_first else (False,):
        for dil in DILATIONS:
            b = _t5_bucket_np(np.maximum(rel, 0) * dil)
            tabs.append(np.where(band & (j >= ATTN_BLOCK) if own_only else band, b, -1).reshape(1, -1))
    return np.stack(tabs).astype(np.float32)


BIAS_SIZE = ATTN_BLOCK * 2 * ATTN_BLOCK


def bias_tables(rel_bias_t, name):
    idx = jnp.asarray(_bucket_tables(True))
    ntab = idx.shape[0]

    def body(rb_ref, idx_ref, o_ref):
        iv = idx_ref[0]
        bk = lax.broadcasted_iota(jnp.int32, (NUM_BUCKETS, BIAS_SIZE), 0).astype(F32)
        onehot = (bk == iv).astype(F32)
        t = jnp.dot(rb_ref[...], onehot, preferred_element_type=F32, precision=lax.Precision.HIGHEST)
        o_ref[0] = jnp.where(iv < 0.0, NEG_INF, t)

    return pl.pallas_call(
        body, name=name, grid=(ntab,),
        in_specs=[pl.BlockSpec((B_HEADS, NUM_BUCKETS), lambda d: (0, 0)),
                  pl.BlockSpec((1, 1, BIAS_SIZE), lambda d: (d, 0, 0))],
        out_specs=pl.BlockSpec((1, B_HEADS, BIAS_SIZE), lambda d: (d, 0, 0)),
        out_shape=SDS((ntab, B_HEADS, BIAS_SIZE), F32),
        compiler_params=_params(("parallel",)),
    )(rel_bias_t, idx)


def rel_bias_grad(dbias, name):
    idx = jnp.asarray(_bucket_tables(False))

    def body(db_ref, idx_ref, o_ref):
        d = pl.program_id(0)
        iv = idx_ref[0]
        bk = lax.broadcasted_iota(jnp.int32, (NUM_BUCKETS, BIAS_SIZE), 0).astype(F32)
        onehot = (bk == iv).astype(F32)
        part = lax.dot_general(db_ref[0], onehot, (((1,), (1,)), ((), ())),
                               preferred_element_type=F32, precision=lax.Precision.HIGHEST)

        @pl.when(d == 0)
        def _():
            o_ref[...] = part

        @pl.when(d > 0)
        def _():
            o_ref[...] += part

    return pl.pallas_call(
        body, name=name, grid=(len(DILATIONS),),
        in_specs=[pl.BlockSpec((1, B_HEADS, BIAS_SIZE), lambda d: (d, 0, 0)),
                  pl.BlockSpec((1, 1, BIAS_SIZE), lambda d: (d, 0, 0))],
        out_specs=pl.BlockSpec((B_HEADS, NUM_BUCKETS), lambda d: (0, 0)),
        out_shape=SDS((B_HEADS, NUM_BUCKETS), F32),
        compiler_params=_params(("arbitrary",)),
    )(dbias, idx)


def _attn_scores(q, kk, bias):
    return _dot_nt(q, kk) * (1.0 / math.sqrt(HEAD_DIM)) + bias


def _head0_lanes():
    return lax.broadcasted_iota(jnp.int32, (ATTN_BLOCK, LANE), 1) < HEAD_DIM


def _one_head(x2, head0, hh):
    return jnp.where(head0 if hh == 0 else jnp.logical_not(head0), x2, 0.0).astype(BF16)


def _rows(start, size, dil):
    return pl.ds(start, size) if dil == 1 else pl.ds(start, size, stride=dil)


QUAD = 4
QUAD_ROWS = SEQ // QUAD


def _deinterleave(src_ref, dst_ref):
    for r in range(QUAD):
        for c in range(QUAD_ROWS // ATTN_BLOCK):
            dst_ref[r, c * ATTN_BLOCK:(c + 1) * ATTN_BLOCK, :] = src_ref[
                pl.ds(r + c * QUAD * ATTN_BLOCK, ATTN_BLOCK, stride=QUAD), :]


def _deinterleave_again(src_ref, dst_ref):
    for r in range(QUAD):
        for s in range(QUAD):
            dst_ref[r + QUAD * s] = src_ref[r, pl.ds(s, ATTN_BLOCK, stride=QUAD), :]


def _interleave_back(src_ref, dst_ref, slot0, accumulate=False):
    for r in range(QUAD):
        for s in range(QUAD):
            rows = pl.ds(s, ATTN_BLOCK, stride=QUAD)
            if accumulate:
                dst_ref[slot0 + r, rows, :] += src_ref[r + QUAD * s]
            else:
                dst_ref[slot0 + r, rows, :] = src_ref[r + QUAD * s]


def _quad_tiles():
    return [(r, pl.ds(r + c * QUAD * ATTN_BLOCK, ATTN_BLOCK, stride=QUAD), slice(c * ATTN_BLOCK, (c + 1) * ATTN_BLOCK))
            for r in range(QUAD) for c in range(QUAD_ROWS // ATTN_BLOCK)]


def _attn_schedule(op):
    def d16(i, carry):
        for t in range(2 * QUAD):
            op(2, 2 * QUAD * i + t, 0, 1, True)
        return carry

    lax.fori_loop(0, QUAD // 2, d16, 0)

    def d4(i, carry):
        for u in range(2):
            for nq in range(QUAD_ROWS // ATTN_BLOCK):
                op(1, 2 * i + u, nq * ATTN_BLOCK, 1, nq == 0)
        return carry

    lax.fori_loop(0, QUAD // 2, d4, 0)
    op(0, None, 0, 1, True)
    per_pass = 5

    def d1(j, carry):
        for t in range(per_pass):
            op(0, None, pl.multiple_of((1 + per_pass * j + t) * ATTN_BLOCK, ATTN_BLOCK), 1, False)
        return carry

    lax.fori_loop(0, (SEQ // ATTN_BLOCK - 1) // per_pass, d1, 0)


def _keys(src, krows, first):
    kb = src[krows, :].astype(BF16)
    return jnp.concatenate([kb, kb], axis=0) if first else kb


def _table(seg, first):
    return len(DILATIONS) + seg if first else seg


def _kv_rows(start, dil, first):
    if first:
        return _rows(start, ATTN_BLOCK, dil)
    return _rows(start - ATTN_BLOCK * dil, 2 * ATTN_BLOCK, dil)


MERGE_ROWS = 256


def attn_fwd(proj, bias, nb_local, name, ride=None):
    n = proj.shape[0]
    nseg = len(DILATIONS)

    def body(q_ref, k_ref, v_ref, b_ref, o_ref, lse_ref, q4_ref, k4_ref, v4_ref, os0_ref, ls0_ref, os4_ref, ls4_ref,
             q16_ref, k16_ref, v16_ref, os16_ref, ls16_ref):
        for src, mid, dst in ((q_ref, q4_ref, q16_ref), (k_ref, k4_ref, k16_ref), (v_ref, v4_ref, v16_ref)):
            _deinterleave(src, mid)
            _deinterleave_again(mid, dst)

        def op(seg, r, start, stride, first):
            qrows = _rows(start, ATTN_BLOCK, stride)
            krows = _kv_rows(start, stride, first)
            if seg == 0:
                q_src, k_src, v_src, o_dst, l_dst = q_ref, k_ref, v_ref, os0_ref, ls0_ref
            elif seg == 1:
                q_src, k_src, v_src = q4_ref.at[r], k4_ref.at[r], v4_ref.at[r]
                o_dst, l_dst = os4_ref.at[r], ls4_ref.at[r]
            else:
                q_src, k_src, v_src = q16_ref.at[r], k16_ref.at[r], v16_ref.at[r]
                o_dst, l_dst = os16_ref.at[r], ls16_ref.at[r]
            q2, kb, vb = q_src[qrows, :], _keys(k_src, krows, first), _keys(v_src, krows, first)
            head0 = _head0_lanes()
            outs, lses = [], []
            for hh in range(2):
                s = _attn_scores(_one_head(q2, head0, hh), kb, b_ref[_table(seg, first), hh])
                m = jnp.max(s, axis=-1, keepdims=True)
                p = jnp.exp(s - m)
                l = jnp.sum(p, axis=-1, keepdims=True)
                outs.append(_dot(p.astype(BF16), vb) / l)
                lses.append(jnp.broadcast_to(m + jnp.log(l), (ATTN_BLOCK, LANE)))
            o_dst[qrows, :] = jnp.where(head0, outs[0], outs[1])
            l_dst[qrows, :] = jnp.where(head0, lses[0], lses[1])

        _attn_schedule(op)
        _interleave_back(os16_ref, os4_ref, QUAD)
        _interleave_back(ls16_ref, ls4_ref, QUAD)

        for r, nat, quad in _quad_tiles():
            ls = [ls0_ref[nat, :], ls4_ref[r, quad, :], ls4_ref[QUAD + r, quad, :]]
            m = functools.reduce(jnp.maximum, ls)
            ws = [jnp.exp(l - m) for l in ls]
            den = ws[0] + ws[1] + ws[2]
            num = ws[0] * os0_ref[nat, :] + ws[1] * os4_ref[r, quad, :] + ws[2] * os4_ref[QUAD + r, quad, :]
            o_ref[nat, :] = num / den
            lse_ref[nat, :] = m + jnp.log(den)

    def in_spec(off):
        return pl.BlockSpec((SEQ, LANE), lambda b, p: (b, off // LANE + p))

    out_spec = pl.BlockSpec((SEQ, LANE), lambda b, p: (b, p))
    return _call(
        body, name=name, grid=(nb_local, HEAD_PAIRS),
        in_specs=[in_spec(Q_OFF), in_spec(K_OFF), in_spec(V_OFF),
                  pl.BlockSpec((2 * nseg, 2, ATTN_BLOCK, 2 * ATTN_BLOCK), lambda b, p: (0, p, 0, 0))],
        out_specs=[out_spec, out_spec],
        out_shape=[SDS((n, B_WIDTH), F32), SDS((n, B_WIDTH), F32)],
        scratch_shapes=[pltpu.VMEM((QUAD, QUAD_ROWS, LANE), F32)] * 3 + [pltpu.VMEM((SEQ, LANE), F32)] * 2
        + [pltpu.VMEM((2 * QUAD, QUAD_ROWS, LANE), F32)] * 2 + [pltpu.VMEM((QUAD * QUAD, ATTN_BLOCK, LANE), F32)] * 5,
        args=[proj, proj, proj, bias], sem=("parallel", "arbitrary"), ride=ride)


def attn_bwd(proj, b_out, dmix, lse_tot, bias, nb_local, name, ride=None):
    n = proj.shape[0]
    nseg = len(DILATIONS)
    a_blocks = A_WIDTH // LANE
    scale = 1.0 / math.sqrt(HEAD_DIM)

    def body(q_ref, k_ref, v_ref, o_ref, do_ref, lse_ref, b_ref, dq_ref, dk_ref, dv_ref, db_ref,
             dqs_ref, delta_ref, dka_ref, dva_ref, q4_ref, k4_ref, v4_ref, do4_ref, lse4_ref, delta4_ref,
             dqs4_ref, dk4_ref, dv4_ref, q16_ref, k16_ref, v16_ref, do16_ref, lse16_ref, delta16_ref,
             dqs16_ref, dk16_ref, dv16_ref):
        @pl.when(pl.program_id(1) == 0)
        def _():
            db_ref[...] = jnp.zeros_like(db_ref)

        for acc_ref in (dka_ref, dva_ref, dk4_ref, dv4_ref):
            acc_ref[...] = jnp.zeros_like(acc_ref)
        quads = (q4_ref, k4_ref, v4_ref, do4_ref, lse4_ref, delta4_ref)
        hexes = (q16_ref, k16_ref, v16_ref, do16_ref, lse16_ref, delta16_ref)

        def row_dots(i, carry):
            rows = pl.ds(pl.multiple_of(i * ATTN_BLOCK, ATTN_BLOCK), ATTN_BLOCK)
            head0 = _head0_lanes()
            prod = do_ref[rows, :] * o_ref[rows, :]
            d0 = jnp.sum(jnp.where(head0, prod, 0.0), axis=-1, keepdims=True)
            d1 = jnp.sum(jnp.where(head0, 0.0, prod), axis=-1, keepdims=True)
            delta_ref[rows, :] = jnp.where(head0, d0, d1)
            return carry

        lax.fori_loop(0, SEQ // ATTN_BLOCK, row_dots, 0)
        for src, mid, dst in zip((q_ref, k_ref, v_ref, do_ref, lse_ref, delta_ref), quads, hexes):
            _deinterleave(src, mid)
            _deinterleave_again(mid, dst)

        def op(seg, r, start, stride, first):
            qrows = _rows(start, ATTN_BLOCK, stride)
            krows = _kv_rows(start, stride, first)
            if seg == 0:
                srcs = (q_ref, k_ref, v_ref, do_ref, lse_ref, delta_ref)
                dq_dst, dk_dst, dv_dst = dqs_ref, dka_ref, dva_ref
            elif seg == 1:
                srcs = tuple(x.at[r] for x in quads)
                dq_dst, dk_dst, dv_dst = dqs4_ref.at[r], dk4_ref.at[r], dv4_ref.at[r]
            else:
                srcs = tuple(x.at[r] for x in hexes)
                dq_dst, dk_dst, dv_dst = dqs16_ref.at[r], dk16_ref.at[r], dv16_ref.at[r]
            q_src, k_src, v_src, do_src, lse_src, delta_src = srcs
            q2, kb, vb = q_src[qrows, :], _keys(k_src, krows, first), _keys(v_src, krows, first)
            do2, lse2, delta2 = do_src[qrows, :], lse_src[qrows, :], delta_src[qrows, :]
            head0 = _head0_lanes()
            dqs, dk, dv = [], None, None
            for hh in range(2):
                col = slice(hh * HEAD_DIM, hh * HEAD_DIM + 1)
                q, dob = _one_head(q2, head0, hh), _one_head(do2, head0, hh)
                p = jnp.exp(_attn_scores(q, kb, b_ref[_table(seg, first), hh]) - lse2[:, col])
                dvh = _dot_tn(p.astype(BF16), dob)
                ds = p * (_dot_nt(dob, vb) - delta2[:, col])
                if first:
                    db_ref[seg, hh, :, ATTN_BLOCK:] += ds[:, ATTN_BLOCK:]
                else:
                    db_ref[seg, hh] += ds
                dsb = ds.astype(BF16)
                dqs.append(_dot(dsb, kb))
                dkh = _dot_tn(dsb, q)
                dk = dkh if dk is None else dk + dkh
                dv = dvh if dv is None else dv + dvh
            if first:
                dk, dv = dk[ATTN_BLOCK:], dv[ATTN_BLOCK:]
            dq_dst[qrows, :] = jnp.where(head0, dqs[0], dqs[1]) * scale
            if seg == 2:
                dk_dst[krows, :] = dk * scale
                dv_dst[krows, :] = dv
            else:
                dk_dst[krows, :] += dk * scale
                dv_dst[krows, :] += dv

        _attn_schedule(op)
        _interleave_back(dqs16_ref, dqs4_ref, QUAD)
        _interleave_back(dk16_ref, dk4_ref, 0, accumulate=True)
        _interleave_back(dv16_ref, dv4_ref, 0, accumulate=True)

        for r, nat, quad in _quad_tiles():
            dqs_ref[nat, :] += dqs4_ref[r, quad, :] + dqs4_ref[QUAD + r, quad, :]
            dka_ref[nat, :] += dk4_ref[r, quad, :]
            dva_ref[nat, :] += dv4_ref[r, quad, :]

        def merge(i, carry):
            rows = pl.ds(pl.multiple_of(i * MERGE_ROWS, MERGE_ROWS), MERGE_ROWS)
            dq_ref[rows, :] = dqs_ref[rows, :].astype(BF16)
            dk_ref[rows, :] = dka_ref[rows, :].astype(BF16)
            dv_ref[rows, :] = dva_ref[rows, :].astype(BF16)
            return carry

        lax.fori_loop(0, SEQ // MERGE_ROWS, merge, 0)

    def pspec(off):
        return pl.BlockSpec((SEQ, LANE), lambda p, b: (b, off // LANE + p))

    ospec = pl.BlockSpec((SEQ, LANE), lambda p, b: (b, p))
    bspec = pl.BlockSpec((nseg, 2, ATTN_BLOCK, 2 * ATTN_BLOCK), lambda p, b: (0, p, 0, 0))
    gshape = SDS((n, B_WIDTH), BF16)
    return _call(
        body, name=name, grid=(HEAD_PAIRS, nb_local),
        in_specs=[pspec(Q_OFF), pspec(K_OFF), pspec(V_OFF), ospec,
                  pl.BlockSpec((SEQ, LANE), lambda p, b: (b, a_blocks + p)), ospec,
                  pl.BlockSpec((2 * nseg, 2, ATTN_BLOCK, 2 * ATTN_BLOCK), lambda p, b: (0, p, 0, 0))],
        out_specs=[ospec, ospec, ospec, bspec],
        out_shape=[gshape, gshape, gshape, SDS((nseg, B_HEADS, ATTN_BLOCK, 2 * ATTN_BLOCK), F32)],
        scratch_shapes=[pltpu.VMEM((SEQ, LANE), F32)] * 4 + [pltpu.VMEM((QUAD, QUAD_ROWS, LANE), F32)] * 6
        + [pltpu.VMEM((2 * QUAD, QUAD_ROWS, LANE), F32)] + [pltpu.VMEM((QUAD, QUAD_ROWS, LANE), F32)] * 2
        + [pltpu.VMEM((QUAD * QUAD, ATTN_BLOCK, LANE), F32)] * 9,
        args=[proj, proj, proj, b_out, dmix, lse_tot, bias], sem=("arbitrary", "arbitrary"), ride=ride)


PAD = 8
CONV_ROWS = 64


CONV_LANES = 256


def _conv_taps(gp_ref, head_ref, r0, ls):
    g0 = gp_ref[r0:r0 + CONV_ROWS, ls]
    if r0 == 0:
        return g0, head_ref[PAD - 1:PAD - 1 + CONV_ROWS, ls], head_ref[PAD - 2:PAD - 2 + CONV_ROWS, ls]
    return g0, gp_ref[r0 - 1:r0 - 1 + CONV_ROWS, ls], gp_ref[r0 - 2:r0 - 2 + CONV_ROWS, ls]


def _fill_head(gp_ref, head_ref):
    head_ref[0:PAD, :] = jnp.zeros((PAD, CONV_LANES), F32)
    head_ref[PAD:PAD + CONV_ROWS, :] = gp_ref[0:CONV_ROWS, :]


def _lane_passes():
    return [slice(l0, l0 + LANE) for l0 in range(0, CONV_LANES, LANE)]


def conv_gelu_fwd(gp, up, cw, cb, nb_local, name):
    n, f = gp.shape

    def body(gp_ref, up_ref, cw_ref, cb_ref, o_ref, head_ref):
        _fill_head(gp_ref, head_ref)
        for ls in _lane_passes():
            w0, w1, w2, bias = cw_ref[0:1, ls], cw_ref[1:2, ls], cw_ref[2:3, ls], cb_ref[:, ls]
            for r0 in range(0, SEQ, CONV_ROWS):
                g0, g1, g2 = _conv_taps(gp_ref, head_ref, r0, ls)
                c = bias + w0 * g2 + w1 * g1 + w2 * g0
                o_ref[r0:r0 + CONV_ROWS, ls] = (_gelu(c) * up_ref[r0:r0 + CONV_ROWS, ls]).astype(BF16)

    blk = pl.BlockSpec((SEQ, CONV_LANES), lambda b, j: (b, j))
    return pl.pallas_call(
        body, name=name, grid=(nb_local, f // CONV_LANES),
        in_specs=[blk, blk, pl.BlockSpec((3, CONV_LANES), lambda b, j: (0, j)),
                  pl.BlockSpec((1, CONV_LANES), lambda b, j: (0, j))],
        out_specs=blk,
        out_shape=SDS((n, f), BF16),
        scratch_shapes=[pltpu.VMEM((PAD + CONV_ROWS, CONV_LANES), F32)],
        compiler_params=_params(("parallel", "parallel")),
    )(gp, up, cw, cb)


def conv_gelu_bwd(dgu, gp, up, cw, cb, nb_local, name, ride=None):
    n, f = gp.shape

    def fold(v):
        return jnp.sum(v.reshape(CONV_ROWS // 8, 8, LANE), axis=0)

    def body(dgu_ref, gp_ref, up_ref, cw_ref, cb_ref, dgp_ref, dup_ref, dcw_ref, dcb_ref, head_ref, dc_ref):
        b = pl.program_id(1)
        _fill_head(gp_ref, head_ref)
        dc_ref[SEQ:SEQ + PAD, :] = jnp.zeros((PAD, CONV_LANES), F32)
        for ls in _lane_passes():
            w0, w1, w2, bias = cw_ref[0:1, ls], cw_ref[1:2, ls], cw_ref[2:3, ls], cb_ref[:, ls]
            sums = [jnp.zeros((8, LANE), F32) for _ in range(4)]
            for r0 in range(0, SEQ, CONV_ROWS):
                rows = slice(r0, r0 + CONV_ROWS)
                g0, g1, g2 = _conv_taps(gp_ref, head_ref, r0, ls)
                gg, dgg = _gelu_and_grad(bias + w0 * g2 + w1 * g1 + w2 * g0)
                dgu = dgu_ref[rows, ls].astype(F32)
                dup_ref[rows, ls] = (dgu * gg).astype(BF16)
                dc = dgu * up_ref[rows, ls] * dgg
                dc_ref[rows, ls] = dc
                sums = [sums[0] + fold(dc * g2), sums[1] + fold(dc * g1), sums[2] + fold(dc * g0), sums[3] + fold(dc)]
            for r0 in range(0, SEQ, CONV_ROWS):
                dgp_ref[r0:r0 + CONV_ROWS, ls] = (
                    w2 * dc_ref[r0:r0 + CONV_ROWS, ls] + w1 * dc_ref[r0 + 1:r0 + 1 + CONV_ROWS, ls]
                    + w0 * dc_ref[r0 + 2:r0 + 2 + CONV_ROWS, ls]).astype(BF16)
            dcw = jnp.concatenate([jnp.sum(s, axis=0, keepdims=True) for s in sums[:3]], axis=0)
            dcb = jnp.sum(sums[3], axis=0, keepdims=True)

            @pl.when(b == 0)
            def _(dcw=dcw, dcb=dcb, ls=ls):
                dcw_ref[:, ls] = dcw
                dcb_ref[:, ls] = dcb

            @pl.when(b > 0)
            def _(dcw=dcw, dcb=dcb, ls=ls):
                dcw_ref[:, ls] += dcw
                dcb_ref[:, ls] += dcb

    blk = pl.BlockSpec((SEQ, CONV_LANES), lambda j, b: (b, j))
    wspec = pl.BlockSpec((3, CONV_LANES), lambda j, b: (0, j))
    bspec = pl.BlockSpec((1, CONV_LANES), lambda j, b: (0, j))
    return _call(
        body, name=name, grid=(f // CONV_LANES, nb_local),
        in_specs=[blk, blk, blk, wspec, bspec], out_specs=[blk, blk, wspec, bspec],
        out_shape=[SDS((n, f), BF16), SDS((n, f), BF16), SDS((3, f), F32), SDS((1, f), F32)],
        scratch_shapes=[pltpu.VMEM((PAD + CONV_ROWS, CONV_LANES), F32), pltpu.VMEM((SEQ + PAD, CONV_LANES), F32)],
        args=[dgu, gp, up, cw, cb], sem=("parallel", "arbitrary"), ride=ride)


def norm_mid_epilogue(x1, dout, z2, g3, g2):
    n, d = x1.shape

    def fn(dh2, step, x1_ref, dout_ref, z2_ref, g3_ref, g2_ref, dx1_ref, dz2_ref, dg3_ref, dg2_ref):
        dxa, dg3r = _rms_bwd(dh2, x1_ref[...], g3_ref[...])
        dx1 = dout_ref[...] + dxa
        dx1_ref[...] = dx1
        dz2, dg2r = _rms_bwd(dx1, z2_ref[...], g2_ref[...])
        dz2_ref[...] = dz2.astype(BF16)
        _accumulate(dg3_ref, jnp.sum(dg3r, axis=0, keepdims=True), step)
        _accumulate(dg2_ref, jnp.sum(dg2r, axis=0, keepdims=True), step)

    return fn, [x1, dout, z2, g3, g2], [SDS((n, d), F32), SDS((n, d), BF16), SDS((1, d), F32), SDS((1, d), F32)]


def norm_in_epilogue(x, dx1, g1):
    n, d = x.shape

    def fn(dh1, step, x_ref, dx1_ref, g1_ref, dx_ref, dg1_ref):
        dxa, dgr = _rms_bwd(dh1, x_ref[...], g1_ref[...])
        dx_ref[...] = dx1_ref[...] + dxa
        _accumulate(dg1_ref, jnp.sum(dgr, axis=0, keepdims=True), step)

    return fn, [x, dx1, g1], [SDS((n, d), F32), SDS((1, d), F32)]


def cast_bf16(arrays, name):
    def body(*refs):
        for i_ref, o_ref in zip(refs[:len(arrays)], refs[len(arrays):]):
            o_ref[...] = i_ref[...].astype(BF16)

    return pl.pallas_call(body, name=name, out_shape=[SDS(a.shape, BF16) for a in arrays],
                          compiler_params=_params())(*arrays)


def adam_update(parts, w, m, v, name, tr=None):
    s, r, c = parts.shape
    tr = r if tr is None else tr
    bc1 = 1.0 - ADAM_B1 ** ADAM_STEP
    bc2 = 1.0 - ADAM_B2 ** ADAM_STEP

    def body(p_ref, w_ref, m_ref, v_ref, g_ref, d_ref, nm_ref, nv_ref):
        g = p_ref[0].astype(F32)
        for j in range(1, s):
            g = g + p_ref[j].astype(F32)
        nm = ADAM_B1 * m_ref[...] + (1.0 - ADAM_B1) * g
        nv = ADAM_B2 * v_ref[...] + (1.0 - ADAM_B2) * (g * g)
        g_ref[...] = g
        nm_ref[...] = nm
        nv_ref[...] = nv
        d_ref[...] = -ADAM_LR * ((nm / bc1) / (jnp.sqrt(nv / bc2) + ADAM_EPS) + ADAM_WD * w_ref[...])

    blk = pl.BlockSpec((tr, c), lambda i: (i, 0))
    return pl.pallas_call(
        body, name=name, grid=(r // tr,),
        in_specs=[pl.BlockSpec((s, tr, c), lambda i: (0, i, 0)), blk, blk, blk],
        out_specs=[blk] * 4, out_shape=[SDS((r, c), F32)] * 4,
        compiler_params=_params(("parallel",)),
    )(parts, w, m, v)


EARLY_NAMES = ("spatial_w", "norm_mix_post", "norm_ffn_pre", "norm_ffn_post", "conv_b", "ln_v_gain", "ln_v_bias",
               "spatial_b")
LATE_NAMES = ("norm_mix_pre", "rel_bias")
PACK_ROW_ALIGN = 8


def _pack_rows(size):
    rows = -(-size // LANE)
    return -(-rows // PACK_ROW_ALIGN) * PACK_ROW_ALIGN


def _pack(arrays):
    flat = []
    for a in arrays:
        rows = _pack_rows(a.size)
        flat.append(jnp.pad(a.reshape(-1), (0, rows * LANE - a.size)))
    return jnp.concatenate(flat).reshape(-1, LANE)


def _unpack(packed, shapes):
    out, row = [], 0
    for shp in shapes:
        size = int(np.prod(shp))
        out.append(packed[row:row + _pack_rows(size)].reshape(-1)[:size].reshape(shp))
        row += _pack_rows(size)
    return out


def kernel(x, norm_mix_pre, norm_mix_post, norm_ffn_pre, norm_ffn_post, w_in, ln_v_gain, ln_v_bias, spatial_w, spatial_b, rel_bias, w_out, w_gate, w_up, conv_w, conv_b, w_down, loss_target, m_norm_mix_pre, m_norm_mix_post, m_norm_ffn_pre, m_norm_ffn_post, m_w_in, m_ln_v_gain, m_ln_v_bias, m_spatial_w, m_spatial_b, m_rel_bias, m_w_out, m_w_gate, m_w_up, m_conv_w, m_conv_b, m_w_down, v_norm_mix_pre, v_norm_mix_post, v_norm_ffn_pre, v_norm_ffn_post, v_w_in, v_ln_v_gain, v_ln_v_bias, v_spatial_w, v_spatial_b, v_rel_bias, v_w_out, v_w_gate, v_w_up, v_conv_w, v_conv_b, v_w_down):
    given = dict(locals())
    nb_local, seq, d = x.shape
    n = nb_local * seq
    cols = w_in.shape[2]

    def by_columns(g):
        return g.transpose(1, 0, 2).reshape(g.shape[1], N_DEV * g.shape[2])

    def to_blocks(g):
        return g.reshape(g.shape[0], N_DEV, cols).transpose(1, 0, 2)

    xf, target = x.reshape(n, d), loss_target.reshape(n, d)
    ln_g, ln_b = ln_v_gain.reshape(1, A_WIDTH), ln_v_bias.reshape(1, A_WIDTH)
    spatial_bt, rel_bias_t = spatial_b[0].T, rel_bias.T

    s_in, s_out, s_gate, s_up, s_down = cast_bf16([w_in[0], w_out[0], w_gate[0], w_up[0], w_down[0]], "cast_shards")
    g_in, g_cw = exchange([], [s_in, conv_w[0]], "gather_w_in")
    w_in_f, conv_w_f = by_columns(g_in), by_columns(g_cw)

    (h1, proj), _ = norm_mm(xf, norm_mix_pre, [w_in_f], "fwd_norm_in", tn=IN_COLS)
    a = gating_fwd(proj, ln_g, ln_b, spatial_w[0], spatial_bt, "fwd_gating")
    bias = bias_tables(rel_bias_t, "bias_tables").reshape(2 * len(DILATIONS), B_HEADS, ATTN_BLOCK, 2 * ATTN_BLOCK)
    (b_out, lse_tot), (g_out, g_gate, g_up) = attn_fwd(proj, bias, nb_local, "fwd_attn",
                                                       ride=([], [s_out, s_gate, s_up]))
    w_out_f, w_gate_f, w_up_f = g_out.reshape(D_MODEL, D_MODEL), by_columns(g_gate), by_columns(g_up)
    z2, x1 = mm_res_norm([a, b_out], w_out_f, xf, norm_mix_post, "fwd_out_norm")
    (h2, gp, up), (g_down,) = norm_mm(x1, norm_ffn_pre, [w_gate_f, w_up_f], "fwd_norm_ffn", tm=256, tn=D_FF,
                                      ride=([], [s_down]))
    w_down_f = g_down.reshape(D_FF, D_MODEL)
    gu = conv_gelu_fwd(gp, up, conv_w_f, conv_b, nb_local, "fwd_conv_gelu")
    dy, dout, dg4, loss_part = down_loss(gu, w_down_f, x1, norm_ffn_post, target, "fwd_down_loss")

    p_down = mm_tn([gu], [dy], "bwd_dw_down", t1=256, t2=D_MODEL)
    (dgu,), _ = mm_nt([(dy, 0, 0)], [w_down_f], "bwd_dgu", out_dtype=BF16)
    (dgp, dup, p_conv_w, p_conv_b), (r_down,) = conv_gelu_bwd(
        dgu, gp, up, conv_w_f, conv_b, nb_local, "bwd_conv_gelu", ride=([p_down.reshape(N_DEV, cols, D_MODEL)], []))
    p_gate = mm_tn([h2], [dgp], "bwd_dw_gate", t1=D_MODEL, t2=256)
    p_up = mm_tn([h2], [dup], "bwd_dw_up", t1=D_MODEL, t2=256)
    (dx1, dz2, dg3, dg2), _ = mm_nt([(dgp, 0, 0), (dup, 1, 0)], [w_gate_f, w_up_f], "bwd_dh2_norm_mid", tm=256,
                                    epilogue=norm_mid_epilogue(x1, dout, z2, norm_ffn_pre, norm_mix_post))
    p_out = mm_tn([a, b_out], [dz2], "bwd_dw_out", t1=256, t2=D_MODEL)
    (dmix,), _ = mm_nt([(dz2, 0, 0)], [w_out_f], "bwd_dmix")
    duv, p_ws, p_sbt, p_lng, p_lnb = gating_bwd(proj, dmix, ln_g, ln_b, spatial_w[0], spatial_bt, "bwd_gating")
    small = dict(spatial_w=p_ws, norm_mix_post=dg2, norm_ffn_pre=dg3, norm_ffn_post=dg4, conv_b=p_conv_b,
                 ln_v_gain=p_lng, ln_v_bias=p_lnb, spatial_b=p_sbt.T)
    pack_early = _pack([small[k] for k in EARLY_NAMES] + [p_conv_w, loss_part])
    (dq, dk, dv, dbias), (r_gate, r_up, r_out, r_early) = attn_bwd(
        proj, b_out, dmix, lse_tot, bias, nb_local, "bwd_attn",
        ride=([to_blocks(p_gate), to_blocks(p_up), p_out.reshape(N_DEV, D_MODEL // N_DEV, D_MODEL)], [pack_early]))
    p_rel_bias_t = rel_bias_grad(dbias.reshape(len(DILATIONS), B_HEADS, BIAS_SIZE), "bwd_rel_bias")
    p_in = mm_tn([h1], [duv, dq, dk, dv], "bwd_dw_in", t1=D_MODEL, t2=256)
    (grad_x, dg1), (r_in,) = mm_nt(
        [(duv, 0, 0), (dq, 0, Q_OFF), (dk, 0, K_OFF), (dv, 0, V_OFF)], [w_in_f], "bwd_dh1_norm_in",
        epilogue=norm_in_epilogue(xf, dx1, norm_mix_pre), ride=([to_blocks(p_in)], []))
    small.update(norm_mix_pre=dg1, rel_bias=p_rel_bias_t.T)
    (r_late,) = exchange([], [_pack([small[k] for k in LATE_NAMES])], "exchange_late")

    res = {}
    res["w_in"] = adam_update(r_in, w_in[0], m_w_in[0], v_w_in[0], "adam_w_in", tr=256)
    res["w_out"] = adam_update(r_out, w_out[0], m_w_out[0], v_w_out[0], "adam_w_out")
    res["w_gate"] = adam_update(r_gate, w_gate[0], m_w_gate[0], v_w_gate[0], "adam_w_gate", tr=256)
    res["w_up"] = adam_update(r_up, w_up[0], m_w_up[0], v_w_up[0], "adam_w_up", tr=256)
    res["w_down"] = adam_update(r_down, w_down[0], m_w_down[0], v_w_down[0], "adam_w_down", tr=176)

    def adam_packed(received, names, tail, name):
        zeros = [jnp.zeros_like(t) for t in tail]
        packs = [_pack([given[pre + k] for k in names] + zeros) for pre in ("", "m_", "v_")]
        shapes = [given[k].shape for k in names] + [t.shape for t in tail]
        unpacked = [_unpack(p, shapes) for p in adam_update(received, *packs, name)]
        for i, k in enumerate(names):
            res[k] = [u[i] for u in unpacked]
        return unpacked[0][len(names):]

    g_conv_w_full, loss_sum = adam_packed(r_early, EARLY_NAMES, [p_conv_w, loss_part], "adam_small_early")
    adam_packed(r_late, LATE_NAMES, [], "adam_small_late")
    g_conv_w = lax.dynamic_slice_in_dim(g_conv_w_full, _my_index() * cols, cols, axis=1)
    res["conv_w"] = adam_update(g_conv_w[None], conv_w[0], m_conv_w[0], v_conv_w[0], "adam_conv_w")
    loss = loss_sum[0, 0]

    names = ("norm_mix_pre", "norm_mix_post", "norm_ffn_pre", "norm_ffn_post", "w_in", "ln_v_gain", "ln_v_bias",
             "spatial_w", "spatial_b", "rel_bias", "w_out", "w_gate", "w_up", "conv_w", "conv_b", "w_down")
    outs = [loss, grad_x.reshape(x.shape)]
    for t in range(4):
        outs += [res[k][t].reshape(given[k].shape) for k in names]
    return tuple(outs)
```

```python
import functools
import math

import numpy as np
import jax
import jax.numpy as jnp
from jax import lax
from jax.experimental import pallas as pl
from jax.experimental.pallas import tpu as pltpu

F32 = jnp.float32
BF16 = jnp.bfloat16
SDS = jax.ShapeDtypeStruct

D_MODEL = 1024
SEQ = 2048
HEAD_DIM = 64
A_GROUPS = 4
A_WIDTH = A_GROUPS * HEAD_DIM
B_HEADS = 12
B_WIDTH = B_HEADS * HEAD_DIM
HEAD_PAIRS = B_HEADS // 2
CHUNK = 128
ATTN_BLOCK = 128
DILATIONS = (1, 4, 16)
NUM_BUCKETS = 32
MAX_DISTANCE = 2048
D_FF = 2816
IN_COLS = 2 * A_WIDTH + 3 * B_WIDTH
Q_OFF = 2 * A_WIDTH
K_OFF = Q_OFF + B_WIDTH
V_OFF = K_OFF + B_WIDTH
NORM_EPS = 1e-6
NEG_INF = -1e30
N_DEV = 8
LANE = 128

ADAM_LR = 0.001
ADAM_B1 = 0.9
ADAM_B2 = 0.999
ADAM_EPS = 1e-08
ADAM_WD = 0.01
ADAM_STEP = 10

GELU_C0 = math.sqrt(2.0 / math.pi)
GELU_C1 = 0.044715

VMEM_LIMIT = 56 * 1024 * 1024


def _params(sem=None):
    if sem is None:
        return pltpu.CompilerParams(vmem_limit_bytes=VMEM_LIMIT)
    return pltpu.CompilerParams(dimension_semantics=sem, vmem_limit_bytes=VMEM_LIMIT)


def _gelu(x):
    t = jnp.tanh(GELU_C0 * (x + GELU_C1 * x * x * x))
    return 0.5 * x * (1.0 + t)


def _gelu_and_grad(x):
    x2 = x * x
    t = jnp.tanh(GELU_C0 * (x + GELU_C1 * x * x2))
    g = 0.5 * x * (1.0 + t)
    dg = 0.5 * (1.0 + t) + 0.5 * x * (1.0 - t * t) * (GELU_C0 * (1.0 + 3.0 * GELU_C1 * x2))
    return g, dg


def _dot(a, b):
    return jnp.dot(a, b, preferred_element_type=F32)


def _dot_nt(a, b):
    return lax.dot_general(a, b, (((1,), (1,)), ((), ())), preferred_element_type=F32)


def _dot_tn(a, b):
    return lax.dot_general(a, b, (((0,), (0,)), ((), ())), preferred_element_type=F32)


def _rms_bwd(d, xin, g):
    r = lax.rsqrt(jnp.mean(xin * xin, axis=-1, keepdims=True) + NORM_EPS)
    xh = xin * r
    gd = g * d
    dx = r * (gd - xh * jnp.mean(gd * xh, axis=-1, keepdims=True))
    return dx, d * xh


MESH = pl.DeviceIdType.MESH
ANY = pl.BlockSpec(memory_space=pl.ANY)
PEER_MASKS = tuple(range(1, N_DEV))


def _my_index():
    return lax.axis_index("x") * 4 + lax.axis_index("y") * 2 + lax.axis_index("c")


def _peer(mask):
    x, y, c = lax.axis_index("x"), lax.axis_index("y"), lax.axis_index("c")
    px = 1 - x if mask & 4 else x
    py = 1 - y if mask & 2 else y
    pc = 1 - c if mask & 1 else c
    return (px, py, pc), px * 4 + py * 2 + pc


RELAY_AT = 3
SIBLING = 1
CHIP_MASKS = (2, 4, 6)


class _Exchange:
    def __init__(self, nblocked, in_refs, out_refs, sems):
        send_sems, recv_sems, local_sems = sems
        me = _my_index()
        sibling, _ = _peer(SIBLING)
        self.local, self.first, self.relays, self.relayed_in, self.last_in = [], [], [], [], []
        for a, (in_ref, out_ref) in enumerate(zip(in_refs, out_refs)):
            def copy(src, slot, mask, to):
                return pltpu.make_async_remote_copy(
                    src_ref=src, dst_ref=out_ref.at[slot], send_sem=send_sems.at[a, mask - 1],
                    recv_sem=recv_sems.at[a, mask - 1], device_id=to, device_id_type=MESH)

            if a < nblocked:
                self.local.append(pltpu.make_async_copy(in_ref.at[me], out_ref.at[me], local_sems.at[a]))
                for mask in PEER_MASKS:
                    peer, pidx = _peer(mask)
                    self.first.append(copy(in_ref.at[pidx], me, mask, peer))
                    self.last_in.append(copy(in_ref.at[pidx], pidx, mask, peer))
                continue
            self.local.append(pltpu.make_async_copy(in_ref, out_ref.at[me], local_sems.at[a]))
            for mask in (SIBLING,) + CHIP_MASKS:
                peer, pidx = _peer(mask)
                self.first.append(copy(in_ref, me, mask, peer))
                (self.last_in if mask == SIBLING else self.relayed_in).append(copy(in_ref, pidx, mask, peer))
            for mask in CHIP_MASKS:
                _, origin = _peer(mask)
                _, far = _peer(mask | SIBLING)
                self.relays.append(copy(out_ref.at[origin], origin, mask | SIBLING, sibling))
                self.last_in.append(copy(in_ref, far, mask | SIBLING, sibling))

    def start(self):
        for cp in self.local + self.first[::-1]:
            cp.start()

    def relay(self):
        for arrived, onward in zip(self.relayed_in, self.relays):
            arrived.wait_recv()
            onward.start()

    def finish(self):
        for cp in self.first + self.relays:
            cp.wait_send()
        for cp in self.last_in:
            cp.wait_recv()
        for cp in self.local:
            cp.wait()


def _exchange_out_shape(blocked, whole):
    return [SDS(b.shape, b.dtype) for b in blocked] + [SDS((N_DEV,) + w.shape, w.dtype) for w in whole]


def _exchange_sems(n):
    return [pltpu.SemaphoreType.DMA((n, N_DEV - 1)), pltpu.SemaphoreType.DMA((n, N_DEV - 1)),
            pltpu.SemaphoreType.DMA((n,))]


def exchange(blocked, whole, name):
    nb, n = len(blocked), len(blocked) + len(whole)

    def body(*refs):
        ex = _Exchange(nb, refs[:n], refs[n:2 * n], refs[2 * n:])
        ex.start()
        ex.relay()
        ex.finish()

    return pl.pallas_call(
        body, name=name, in_specs=[ANY] * n, out_specs=[ANY] * n, out_shape=_exchange_out_shape(blocked, whole),
        scratch_shapes=_exchange_sems(n),
    )(*blocked, *whole)


def _call(body, *, name, grid, in_specs, out_specs, out_shape, args, scratch_shapes=(), sem=None, ride=None):
    out_shape, out_specs, scratch_shapes = list(out_shape), list(out_specs), list(scratch_shapes)
    if ride is None:
        outs = pl.pallas_call(body, name=name, grid=grid, in_specs=list(in_specs), out_specs=out_specs,
                              out_shape=out_shape, scratch_shapes=scratch_shapes,
                              compiler_params=_params(sem))(*args)
        return list(outs), []
    blocked, whole = ride
    cargs = list(blocked) + list(whole)
    nb, nc = len(blocked), len(cargs)
    n_in, n_out, n_scr = len(args), len(out_shape), len(scratch_shapes)
    steps = math.prod(grid)
    assert steps >= 3, grid

    def riding(*refs):
        ins, refs = refs[:n_in], refs[n_in:]
        cins, refs = refs[:nc], refs[nc:]
        outs, refs = refs[:n_out], refs[n_out:]
        couts, refs = refs[:nc], refs[nc:]
        scr, sems = refs[:n_scr], refs[n_scr:]
        step = functools.reduce(lambda acc, k: acc * grid[k] + pl.program_id(k), range(len(grid)), 0)

        @pl.when(step == 0)
        def _():
            _Exchange(nb, cins, couts, sems).start()

        @pl.when(step == RELAY_AT * steps // 4)
        def _():
            _Exchange(nb, cins, couts, sems).relay()

        body(*ins, *outs, *scr)

        @pl.when(step == steps - 1)
        def _():
            _Exchange(nb, cins, couts, sems).finish()

    res = pl.pallas_call(
        riding, name=name, grid=grid, in_specs=list(in_specs) + [ANY] * nc, out_specs=out_specs + [ANY] * nc,
        out_shape=out_shape + _exchange_out_shape(blocked, whole),
        scratch_shapes=scratch_shapes + _exchange_sems(nc),
        compiler_params=_params(("arbitrary",) * len(grid)))(*args, *cargs)
    return list(res[:n_out]), list(res[n_out:])


def norm_mm(x, g, ws, name, tm=512, tn=1408, ride=None):
    n, d = x.shape
    f = ws[0].shape[0]
    nw = len(ws)

    def body(x_ref, g_ref, *refs):
        w_refs = refs[:nw]
        h_ref = refs[nw]
        o_refs = refs[nw + 1:]

        @pl.when(pl.program_id(1) == 0)
        def _():
            xv = x_ref[...]
            r = lax.rsqrt(jnp.mean(xv * xv, axis=-1, keepdims=True) + NORM_EPS)
            h_ref[...] = (xv * r * g_ref[...]).astype(BF16)

        h = h_ref[...]
        for w_ref, o_ref in zip(w_refs, o_refs):
            o_ref[...] = _dot_nt(h, w_ref[...])

    return _call(
        body, name=name, grid=(n // tm, f // tn),
        in_specs=[pl.BlockSpec((tm, d), lambda i, j: (i, 0)), pl.BlockSpec((1, d), lambda i, j: (0, 0))]
        + [pl.BlockSpec((tn, d), lambda i, j: (j, 0)) for _ in ws],
        out_specs=[pl.BlockSpec((tm, d), lambda i, j: (i, 0))]
        + [pl.BlockSpec((tm, tn), lambda i, j: (i, j)) for _ in ws],
        out_shape=[SDS((n, d), BF16)] + [SDS((n, f), F32) for _ in ws],
        args=[x, g, *ws], sem=("parallel", "arbitrary"), ride=ride)


def _lane_concat(refs):
    vals = [r[...].astype(BF16) for r in refs]
    return vals[0] if len(vals) == 1 else jnp.concatenate(vals, axis=1)


def mm_res_norm(a_list, w, res, g, name, tm=512):
    n = a_list[0].shape[0]
    k, d = w.shape
    na = len(a_list)

    def body(*refs):
        w_ref, res_ref, g_ref, y_ref, o_ref = refs[na:]
        y = _dot(_lane_concat(refs[:na]), w_ref[...])
        r = lax.rsqrt(jnp.mean(y * y, axis=-1, keepdims=True) + NORM_EPS)
        y_ref[...] = y
        o_ref[...] = res_ref[...] + y * r * g_ref[...]

    return pl.pallas_call(
        body, name=name, grid=(n // tm,),
        in_specs=[pl.BlockSpec((tm, a.shape[1]), lambda i: (i, 0)) for a in a_list]
        + [pl.BlockSpec((k, d), lambda i: (0, 0)),
           pl.BlockSpec((tm, d), lambda i: (i, 0)), pl.BlockSpec((1, d), lambda i: (0, 0))],
        out_specs=[pl.BlockSpec((tm, d), lambda i: (i, 0)), pl.BlockSpec((tm, d), lambda i: (i, 0))],
        out_shape=[SDS((n, d), F32), SDS((n, d), F32)],
        compiler_params=_params(("parallel",)),
    )(*a_list, w, res, g)


def down_loss(a, w, res, g, target, name, tm=256):
    n, k = a.shape
    d = w.shape[1]
    inv_d = 1.0 / d

    def body(a_ref, w_ref, res_ref, g_ref, t_ref, dy_ref, dout_ref, dg_ref, loss_ref):
        i = pl.program_id(0)
        y = _dot(a_ref[...], w_ref[...])
        gv = g_ref[...]
        r = lax.rsqrt(jnp.mean(y * y, axis=-1, keepdims=True) + NORM_EPS)
        yh = y * r
        e = res_ref[...] + yh * gv - t_ref[...]
        part = 0.5 * inv_d * jnp.sum(jnp.sum(e * e, axis=-1, keepdims=True), axis=0, keepdims=True)
        dout = e * inv_d
        dout_ref[...] = dout
        gd = gv * dout
        dy_ref[...] = (r * (gd - yh * jnp.mean(gd * yh, axis=-1, keepdims=True))).astype(BF16)
        dgp = jnp.sum(dout * yh, axis=0, keepdims=True)
        lane0 = lax.broadcasted_iota(jnp.int32, (1, LANE), 1) == 0
        lp = jnp.where(lane0, part, 0.0)

        @pl.when(i == 0)
        def _():
            dg_ref[...] = dgp
            loss_ref[...] = lp

        @pl.when(i > 0)
        def _():
            dg_ref[...] += dgp
            loss_ref[...] += lp

    return pl.pallas_call(
        body, name=name, grid=(n // tm,),
        in_specs=[pl.BlockSpec((tm, k), lambda i: (i, 0)), pl.BlockSpec((k, d), lambda i: (0, 0)),
                  pl.BlockSpec((tm, d), lambda i: (i, 0)), pl.BlockSpec((1, d), lambda i: (0, 0)),
                  pl.BlockSpec((tm, d), lambda i: (i, 0))],
        out_specs=[pl.BlockSpec((tm, d), lambda i: (i, 0)), pl.BlockSpec((tm, d), lambda i: (i, 0)),
                   pl.BlockSpec((1, d), lambda i: (0, 0)), pl.BlockSpec((1, LANE), lambda i: (0, 0))],
        out_shape=[SDS((n, d), BF16), SDS((n, d), F32), SDS((1, d), F32), SDS((1, LANE), F32)],
        compiler_params=_params(("arbitrary",)),
    )(a, w, res, g, target)


def _accumulate(ref, val, step):
    @pl.when(step == 0)
    def _():
        ref[...] = val

    @pl.when(step > 0)
    def _():
        ref[...] += val


def mm_nt(terms, ws, name, tm=512, out_dtype=F32, ride=None, epilogue=None, by_rows=False):
    n = terms[0][0].shape[0]
    r = ws[0].shape[1 if by_rows else 0]
    na = len(terms)
    meta = [(widx, off, a.shape[1]) for a, widx, off in terms]
    fn, extras, out_shape = epilogue if epilogue else (None, [], [SDS((n, r), out_dtype)])
    n_fixed = na + len(ws)

    def body(*refs):
        a_refs = refs[:na]
        w_refs = refs[na:n_fixed]
        acc = None
        for a_ref, (widx, off, k) in zip(a_refs, meta):
            a = a_ref[...].astype(BF16)
            p = _dot(a, w_refs[widx][off:off + k, :]) if by_rows else _dot_nt(a, w_refs[widx][:, off:off + k])
            acc = p if acc is None else acc + p
        if fn is None:
            refs[-1][...] = acc.astype(out_dtype)
        else:
            fn(acc, pl.program_id(0), *refs[n_fixed:])

    def spec(a):
        if a.shape[0] == 1:
            return pl.BlockSpec(a.shape, lambda i: (0, 0))
        return pl.BlockSpec((tm, a.shape[1]), lambda i: (i, 0))

    return _call(
        body, name=name, grid=(n // tm,),
        in_specs=[spec(a) for a, _, _ in terms] + [pl.BlockSpec(w.shape, lambda i: (0, 0)) for w in ws]
        + [spec(e) for e in extras],
        out_specs=[spec(o) for o in out_shape], out_shape=out_shape,
        args=[a for a, _, _ in terms] + list(ws) + list(extras),
        sem=("parallel",) if fn is None else ("arbitrary",), ride=ride)


def _piece_blocks(pieces, tile):
    out, first = [], 0
    for p in pieces:
        nblk, rem = divmod(p.shape[1], tile)
        assert rem == 0, (p.shape, tile)
        out.append((first, nblk))
        first += nblk
    return out, first


def mm_tn(lhs_list, rhs_list, name, t1, t2, out_dtype=BF16):
    n = lhs_list[0].shape[0]
    lblocks, nbl = _piece_blocks(lhs_list, t1)
    rblocks, nbr = _piece_blocks(rhs_list, t2)
    nl = len(lhs_list)

    def body(*refs):
        l_refs, r_refs, o_ref = refs[:nl], refs[nl:-1], refs[-1]
        i, j = pl.program_id(0), pl.program_id(1)
        for l_ref, (ls, ln) in zip(l_refs, lblocks):
            for r_ref, (rs, rn) in zip(r_refs, rblocks):
                @pl.when((i >= ls) & (i < ls + ln) & (j >= rs) & (j < rs + rn))
                def _(l_ref=l_ref, r_ref=r_ref):
                    o_ref[...] = _dot_tn(l_ref[...].astype(BF16), r_ref[...].astype(BF16)).astype(out_dtype)

    def piece_spec(tile, axis, first, nblk):
        def index(i, j):
            return 0, jnp.clip((i, j)[axis] - first, 0, nblk - 1)
        return pl.BlockSpec((n, tile), index)

    return pl.pallas_call(
        body, name=name, grid=(nbl, nbr),
        in_specs=[piece_spec(t1, 0, *b) for b in lblocks] + [piece_spec(t2, 1, *b) for b in rblocks],
        out_specs=pl.BlockSpec((t1, t2), lambda i, j: (i, j)),
        out_shape=SDS((nbl * t1, nbr * t2), out_dtype),
        compiler_params=_params(("parallel", "arbitrary")),
    )(*lhs_list, *rhs_list)


GATE_ROWS = 512


def _tril_mask():
    row = lax.broadcasted_iota(jnp.int32, (CHUNK, CHUNK), 0)
    col = lax.broadcasted_iota(jnp.int32, (CHUNK, CHUNK), 1)
    return row >= col


def _group_of(shape, axis):
    return lax.broadcasted_iota(jnp.int32, shape, axis) // HEAD_DIM


def _group_mean_matrix():
    same = _group_of((A_WIDTH, A_WIDTH), 0) == _group_of((A_WIDTH, A_WIDTH), 1)
    return jnp.where(same, 1.0 / HEAD_DIM, 0.0).astype(F32)


def _dot_f32(a, b):
    return jnp.dot(a, b, preferred_element_type=F32, precision=lax.Precision.HIGHEST)


def _by_group(parts, lane_group):
    out = parts[A_GROUPS - 1]
    for g in range(A_GROUPS - 2, -1, -1):
        out = jnp.where(lane_group == g, parts[g], out)
    return out


def _group_norm(gv, gmean):
    xc = gv - _dot_f32(gv, gmean)
    rstd = lax.rsqrt(_dot_f32(xc * xc, gmean) + NORM_EPS)
    return xc * rstd, rstd


def gating_fwd(proj, lng, lnb, ws, sbt, name):
    n = proj.shape[0]

    def body(u_ref, v_ref, lng_ref, lnb_ref, ws_ref, sbt_ref, a_ref):
        tril = _tril_mask()
        lane_group = _group_of((CHUNK, A_WIDTH), 1)
        gmean = _group_mean_matrix()
        wts = [jnp.where(tril, ws_ref[g], 0.0).astype(BF16) for g in range(A_GROUPS)]
        sb = _by_group([sbt_ref[:, g:g + 1] for g in range(A_GROUPS)], lane_group)

        def chunk(c, carry):
            rows = pl.ds(pl.multiple_of(c * CHUNK, CHUNK), CHUNK)
            vhat, _ = _group_norm(_gelu(v_ref[rows, :]), gmean)
            vn = (vhat * lng_ref[...] + lnb_ref[...]).astype(BF16)
            z = _by_group([_dot(wt, vn) for wt in wts], lane_group) + sb
            a_ref[rows, :] = _gelu(u_ref[rows, :]) * z
            return carry

        lax.fori_loop(0, GATE_ROWS // CHUNK, chunk, 0)

    return pl.pallas_call(
        body, name=name, grid=(n // GATE_ROWS,),
        in_specs=[pl.BlockSpec((GATE_ROWS, A_WIDTH), lambda i: (i, 0)),
                  pl.BlockSpec((GATE_ROWS, A_WIDTH), lambda i: (i, 1)),
                  pl.BlockSpec((1, A_WIDTH), lambda i: (0, 0)), pl.BlockSpec((1, A_WIDTH), lambda i: (0, 0)),
                  pl.BlockSpec((A_GROUPS, CHUNK, CHUNK), lambda i: (0, 0, 0)),
                  pl.BlockSpec((CHUNK, A_GROUPS), lambda i: (0, 0))],
        out_specs=pl.BlockSpec((GATE_ROWS, A_WIDTH), lambda i: (i, 0)),
        out_shape=SDS((n, A_WIDTH), F32),
        compiler_params=_params(("parallel",)),
    )(proj, proj, lng, lnb, ws, sbt)


def gating_bwd(proj, dmix, lng, lnb, ws, sbt, name):
    n = proj.shape[0]

    def body(u_ref, v_ref, da_ref, lng_ref, lnb_ref, ws_ref, sbt_ref,
             duv_ref, dws_ref, dsbt_ref, dlng_ref, dlnb_ref):
        @pl.when(pl.program_id(0) == 0)
        def _():
            dws_ref[...] = jnp.zeros_like(dws_ref)
            dsbt_ref[...] = jnp.zeros_like(dsbt_ref)
            dlng_ref[...] = jnp.zeros_like(dlng_ref)
            dlnb_ref[...] = jnp.zeros_like(dlnb_ref)

        tril = _tril_mask()
        lane_group = _group_of((CHUNK, A_WIDTH), 1)
        gmean = _group_mean_matrix()
        gsum = (_group_of((A_WIDTH, LANE), 0) == lax.broadcasted_iota(jnp.int32, (A_WIDTH, LANE), 1)).astype(F32)
        wts = [jnp.where(tril, ws_ref[g], 0.0) for g in range(A_GROUPS)]
        wts_b = [w.astype(BF16) for w in wts]
        wts_t = [w.T.astype(BF16) for w in wts]
        sb = _by_group([sbt_ref[:, g:g + 1] for g in range(A_GROUPS)], lane_group)
        lg = lng_ref[...]

        def chunk(c, carry):
            rows = pl.ds(pl.multiple_of(c * CHUNK, CHUNK), CHUNK)
            gu, dgu_dx = _gelu_and_grad(u_ref[rows, :])
            gv, dgv_dx = _gelu_and_grad(v_ref[rows, :])
            vhat, rstd = _group_norm(gv, gmean)
            vn = (vhat * lg + lnb_ref[...]).astype(BF16)
            z = _by_group([_dot(wt, vn) for wt in wts_b], lane_group) + sb
            da = da_ref[rows, :]
            dz = da * gu
            dzb = dz.astype(BF16)
            duv_ref[rows, 0:A_WIDTH] = (da * z * dgu_dx).astype(BF16)
            dsbt_ref[...] += _dot_f32(dz, gsum)[:, 0:A_GROUPS]
            for g in range(A_GROUPS):
                dz_g = jnp.where(lane_group == g, dzb, jnp.zeros_like(dzb))
                dws_ref[g] += jnp.where(tril, _dot_nt(dz_g, vn), 0.0)
            dvn = _by_group([_dot(wt, dzb) for wt in wts_t], lane_group)
            dlng_ref[...] += jnp.sum(dvn * vhat, axis=0, keepdims=True)
            dlnb_ref[...] += jnp.sum(dvn, axis=0, keepdims=True)
            dvh = dvn * lg
            dgv = rstd * (dvh - _dot_f32(dvh, gmean) - vhat * _dot_f32(dvh * vhat, gmean))
            duv_ref[rows, A_WIDTH:2 * A_WIDTH] = (dgv * dgv_dx).astype(BF16)
            return carry

        lax.fori_loop(0, GATE_ROWS // CHUNK, chunk, 0)

    return pl.pallas_call(
        body, name=name, grid=(n // GATE_ROWS,),
        in_specs=[pl.BlockSpec((GATE_ROWS, A_WIDTH), lambda i: (i, 0)),
                  pl.BlockSpec((GATE_ROWS, A_WIDTH), lambda i: (i, 1)),
                  pl.BlockSpec((GATE_ROWS, A_WIDTH), lambda i: (i, 0)),
                  pl.BlockSpec((1, A_WIDTH), lambda i: (0, 0)), pl.BlockSpec((1, A_WIDTH), lambda i: (0, 0)),
                  pl.BlockSpec((A_GROUPS, CHUNK, CHUNK), lambda i: (0, 0, 0)),
                  pl.BlockSpec((CHUNK, A_GROUPS), lambda i: (0, 0))],
        out_specs=[pl.BlockSpec((GATE_ROWS, 2 * A_WIDTH), lambda i: (i, 0)),
                   pl.BlockSpec((A_GROUPS, CHUNK, CHUNK), lambda i: (0, 0, 0)),
                   pl.BlockSpec((CHUNK, A_GROUPS), lambda i: (0, 0)),
                   pl.BlockSpec((1, A_WIDTH), lambda i: (0, 0)), pl.BlockSpec((1, A_WIDTH), lambda i: (0, 0))],
        out_shape=[SDS((n, 2 * A_WIDTH), BF16), SDS((A_GROUPS, CHUNK, CHUNK), F32), SDS((CHUNK, A_GROUPS), F32),
                   SDS((1, A_WIDTH), F32), SDS((1, A_WIDTH), F32)],
        compiler_params=_params(("arbitrary",)),
    )(proj, proj, dmix, lng, lnb, ws, sbt)


def _t5_bucket_np(dist):
    max_exact = NUM_BUCKETS // 2
    dd = np.maximum(dist, 1).astype(np.float64)
    large = max_exact + np.log(dd / max_exact) / math.log(MAX_DISTANCE / max_exact) * (NUM_BUCKETS - max_exact)
    large = np.minimum(large.astype(np.int64), NUM_BUCKETS - 1)
    return np.where(dist < max_exact, dist, large)


def _bucket_tables(with_first):
    i = np.arange(ATTN_BLOCK)[:, None]
    j = np.arange(2 * ATTN_BLOCK)[None, :]
    rel = ATTN_BLOCK + i - j
    band = (rel >= 0) & (rel <= ATTN_BLOCK)
    tabs = []
    for own_only in (False, True) if with_first else (False,):
        for dil in DILATIONS:
            b = _t5_bucket_np(np.maximum(rel, 0) * dil)
            tabs.append(np.where(band & (j >= ATTN_BLOCK) if own_only else band, b, -1).reshape(1, -1))
    return np.stack(tabs).astype(np.float32)


BIAS_SIZE = ATTN_BLOCK * 2 * ATTN_BLOCK


def bias_tables(rel_bias_t, name):
    idx = jnp.asarray(_bucket_tables(True))
    ntab = idx.shape[0]

    def body(rb_ref, idx_ref, o_ref):
        iv = idx_ref[0]
        bk = lax.broadcasted_iota(jnp.int32, (NUM_BUCKETS, BIAS_SIZE), 0).astype(F32)
        onehot = (bk == iv).astype(F32)
        t = jnp.dot(rb_ref[...], onehot, preferred_element_type=F32, precision=lax.Precision.HIGHEST)
        o_ref[0] = jnp.where(iv < 0.0, NEG_INF, t)

    return pl.pallas_call(
        body, name=name, grid=(ntab,),
        in_specs=[pl.BlockSpec((B_HEADS, NUM_BUCKETS), lambda d: (0, 0)),
                  pl.BlockSpec((1, 1, BIAS_SIZE), lambda d: (d, 0, 0))],
        out_specs=pl.BlockSpec((1, B_HEADS, BIAS_SIZE), lambda d: (d, 0, 0)),
        out_shape=SDS((ntab, B_HEADS, BIAS_SIZE), F32),
        compiler_params=_params(("parallel",)),
    )(rel_bias_t, idx)


def rel_bias_grad(dbias, name):
    idx = jnp.asarray(_bucket_tables(False))

    def body(db_ref, idx_ref, o_ref):
        d = pl.program_id(0)
        iv = idx_ref[0]
        bk = lax.broadcasted_iota(jnp.int32, (NUM_BUCKETS, BIAS_SIZE), 0).astype(F32)
        onehot = (bk == iv).astype(F32)
        part = lax.dot_general(db_ref[0], onehot, (((1,), (1,)), ((), ())),
                               preferred_element_type=F32, precision=lax.Precision.HIGHEST)

        @pl.when(d == 0)
        def _():
            o_ref[...] = part

        @pl.when(d > 0)
        def _():
            o_ref[...] += part

    return pl.pallas_call(
        body, name=name, grid=(len(DILATIONS),),
        in_specs=[pl.BlockSpec((1, B_HEADS, BIAS_SIZE), lambda d: (d, 0, 0)),
                  pl.BlockSpec((1, 1, BIAS_SIZE), lambda d: (d, 0, 0))],
        out_specs=pl.BlockSpec((B_HEADS, NUM_BUCKETS), lambda d: (0, 0)),
        out_shape=SDS((B_HEADS, NUM_BUCKETS), F32),
        compiler_params=_params(("arbitrary",)),
    )(dbias, idx)


def _attn_scores(q, kk, bias):
    return _dot_nt(q, kk) * (1.0 / math.sqrt(HEAD_DIM)) + bias


def _head0_lanes():
    return lax.broadcasted_iota(jnp.int32, (ATTN_BLOCK, LANE), 1) < HEAD_DIM


def _one_head(x2, head0, hh):
    return jnp.where(head0 if hh == 0 else jnp.logical_not(head0), x2, 0.0).astype(BF16)


def _rows(start, size, dil):
    return pl.ds(start, size) if dil == 1 else pl.ds(start, size, stride=dil)


QUAD = 4
QUAD_ROWS = SEQ // QUAD


def _deinterleave(src_ref, dst_ref):
    for r in range(QUAD):
        for c in range(QUAD_ROWS // ATTN_BLOCK):
            dst_ref[r, c * ATTN_BLOCK:(c + 1) * ATTN_BLOCK, :] = src_ref[
                pl.ds(r + c * QUAD * ATTN_BLOCK, ATTN_BLOCK, stride=QUAD), :]


def _deinterleave_again(src_ref, dst_ref):
    for r in range(QUAD):
        for s in range(QUAD):
            dst_ref[r + QUAD * s] = src_ref[r, pl.ds(s, ATTN_BLOCK, stride=QUAD), :]


def _interleave_back(src_ref, dst_ref, slot0, accumulate=False):
    for r in range(QUAD):
        for s in range(QUAD):
            rows = pl.ds(s, ATTN_BLOCK, stride=QUAD)
            if accumulate:
                dst_ref[slot0 + r, rows, :] += src_ref[r + QUAD * s]
            else:
                dst_ref[slot0 + r, rows, :] = src_ref[r + QUAD * s]


def _quad_tiles():
    return [(r, pl.ds(r + c * QUAD * ATTN_BLOCK, ATTN_BLOCK, stride=QUAD), slice(c * ATTN_BLOCK, (c + 1) * ATTN_BLOCK))
            for r in range(QUAD) for c in range(QUAD_ROWS // ATTN_BLOCK)]


def _attn_schedule(op):
    def d16(i, carry):
        for t in range(2 * QUAD):
            op(2, 2 * QUAD * i + t, 0, 1, True)
        return carry

    lax.fori_loop(0, QUAD // 2, d16, 0)

    def d4(i, carry):
        for u in range(2):
            for nq in range(QUAD_ROWS // ATTN_BLOCK):
                op(1, 2 * i + u, nq * ATTN_BLOCK, 1, nq == 0)
        return carry

    lax.fori_loop(0, QUAD // 2, d4, 0)
    op(0, None, 0, 1, True)
    per_pass = 5

    def d1(j, carry):
        for t in range(per_pass):
            op(0, None, pl.multiple_of((1 + per_pass * j + t) * ATTN_BLOCK, ATTN_BLOCK), 1, False)
        return carry

    lax.fori_loop(0, (SEQ // ATTN_BLOCK - 1) // per_pass, d1, 0)


def _keys(src, krows, first):
    kb = src[krows, :].astype(BF16)
    return jnp.concatenate([kb, kb], axis=0) if first else kb


def _table(seg, first):
    return len(DILATIONS) + seg if first else seg


def _kv_rows(start, dil, first):
    if first:
        return _rows(start, ATTN_BLOCK, dil)
    return _rows(start - ATTN_BLOCK * dil, 2 * ATTN_BLOCK, dil)


MERGE_ROWS = 256


def attn_fwd(proj, bias, nb_local, name, ride=None):
    n = proj.shape[0]
    nseg = len(DILATIONS)

    def body(q_ref, k_ref, v_ref, b_ref, o_ref, lse_ref, q4_ref, k4_ref, v4_ref, os0_ref, ls0_ref, os4_ref, ls4_ref,
             q16_ref, k16_ref, v16_ref, os16_ref, ls16_ref):
        for src, mid, dst in ((q_ref, q4_ref, q16_ref), (k_ref, k4_ref, k16_ref), (v_ref, v4_ref, v16_ref)):
            _deinterleave(src, mid)
            _deinterleave_again(mid, dst)

        def op(seg, r, start, stride, first):
            qrows = _rows(start, ATTN_BLOCK, stride)
            krows = _kv_rows(start, stride, first)
            if seg == 0:
                q_src, k_src, v_src, o_dst, l_dst = q_ref, k_ref, v_ref, os0_ref, ls0_ref
            elif seg == 1:
                q_src, k_src, v_src = q4_ref.at[r], k4_ref.at[r], v4_ref.at[r]
                o_dst, l_dst = os4_ref.at[r], ls4_ref.at[r]
            else:
                q_src, k_src, v_src = q16_ref.at[r], k16_ref.at[r], v16_ref.at[r]
                o_dst, l_dst = os16_ref.at[r], ls16_ref.at[r]
            q2, kb, vb = q_src[qrows, :], _keys(k_src, krows, first), _keys(v_src, krows, first)
            head0 = _head0_lanes()
            outs, lses = [], []
            for hh in range(2):
                s = _attn_scores(_one_head(q2, head0, hh), kb, b_ref[_table(seg, first), hh])
                m = jnp.max(s, axis=-1, keepdims=True)
                p = jnp.exp(s - m)
                l = jnp.sum(p, axis=-1, keepdims=True)
                outs.append(_dot(p.astype(BF16), vb) / l)
                lses.append(jnp.broadcast_to(m + jnp.log(l), (ATTN_BLOCK, LANE)))
            o_dst[qrows, :] = jnp.where(head0, outs[0], outs[1])
            l_dst[qrows, :] = jnp.where(head0, lses[0], lses[1])

        _attn_schedule(op)
        _interleave_back(os16_ref, os4_ref, QUAD)
        _interleave_back(ls16_ref, ls4_ref, QUAD)

        for r, nat, quad in _quad_tiles():
            ls = [ls0_ref[nat, :], ls4_ref[r, quad, :], ls4_ref[QUAD + r, quad, :]]
            m = functools.reduce(jnp.maximum, ls)
            ws = [jnp.exp(l - m) for l in ls]
            den = ws[0] + ws[1] + ws[2]
            num = ws[0] * os0_ref[nat, :] + ws[1] * os4_ref[r, quad, :] + ws[2] * os4_ref[QUAD + r, quad, :]
            o_ref[nat, :] = num / den
            lse_ref[nat, :] = m + jnp.log(den)

    def in_spec(off):
        return pl.BlockSpec((SEQ, LANE), lambda b, p: (b, off // LANE + p))

    out_spec = pl.BlockSpec((SEQ, LANE), lambda b, p: (b, p))
    return _call(
        body, name=name, grid=(nb_local, HEAD_PAIRS),
        in_specs=[in_spec(Q_OFF), in_spec(K_OFF), in_spec(V_OFF),
                  pl.BlockSpec((2 * nseg, 2, ATTN_BLOCK, 2 * ATTN_BLOCK), lambda b, p: (0, p, 0, 0))],
        out_specs=[out_spec, out_spec],
        out_shape=[SDS((n, B_WIDTH), F32), SDS((n, B_WIDTH), F32)],
        scratch_shapes=[pltpu.VMEM((QUAD, QUAD_ROWS, LANE), F32)] * 3 + [pltpu.VMEM((SEQ, LANE), F32)] * 2
        + [pltpu.VMEM((2 * QUAD, QUAD_ROWS, LANE), F32)] * 2 + [pltpu.VMEM((QUAD * QUAD, ATTN_BLOCK, LANE), F32)] * 5,
        args=[proj, proj, proj, bias], sem=("parallel", "arbitrary"), ride=ride)


def attn_bwd(proj, b_out, dmix, lse_tot, bias, nb_local, name, ride=None):
    n = proj.shape[0]
    nseg = len(DILATIONS)
    a_blocks = A_WIDTH // LANE
    scale = 1.0 / math.sqrt(HEAD_DIM)

    def body(q_ref, k_ref, v_ref, o_ref, do_ref, lse_ref, b_ref, dq_ref, dk_ref, dv_ref, db_ref,
             dqs_ref, delta_ref, dka_ref, dva_ref, q4_ref, k4_ref, v4_ref, do4_ref, lse4_ref, delta4_ref,
             dqs4_ref, dk4_ref, dv4_ref, q16_ref, k16_ref, v16_ref, do16_ref, lse16_ref, delta16_ref,
             dqs16_ref, dk16_ref, dv16_ref):
        @pl.when(pl.program_id(1) == 0)
        def _():
            db_ref[...] = jnp.zeros_like(db_ref)

        for acc_ref in (dka_ref, dva_ref, dk4_ref, dv4_ref):
            acc_ref[...] = jnp.zeros_like(acc_ref)
        quads = (q4_ref, k4_ref, v4_ref, do4_ref, lse4_ref, delta4_ref)
        hexes = (q16_ref, k16_ref, v16_ref, do16_ref, lse16_ref, delta16_ref)

        def row_dots(i, carry):
            rows = pl.ds(pl.multiple_of(i * ATTN_BLOCK, ATTN_BLOCK), ATTN_BLOCK)
            head0 = _head0_lanes()
            prod = do_ref[rows, :] * o_ref[rows, :]
            d0 = jnp.sum(jnp.where(head0, prod, 0.0), axis=-1, keepdims=True)
            d1 = jnp.sum(jnp.where(head0, 0.0, prod), axis=-1, keepdims=True)
            delta_ref[rows, :] = jnp.where(head0, d0, d1)
            return carry

        lax.fori_loop(0, SEQ // ATTN_BLOCK, row_dots, 0)
        for src, mid, dst in zip((q_ref, k_ref, v_ref, do_ref, lse_ref, delta_ref), quads, hexes):
            _deinterleave(src, mid)
            _deinterleave_again(mid, dst)

        def op(seg, r, start, stride, first):
            qrows = _rows(start, ATTN_BLOCK, stride)
            krows = _kv_rows(start, stride, first)
            if seg == 0:
                srcs = (q_ref, k_ref, v_ref, do_ref, lse_ref, delta_ref)
                dq_dst, dk_dst, dv_dst = dqs_ref, dka_ref, dva_ref
            elif seg == 1:
                srcs = tuple(x.at[r] for x in quads)
                dq_dst, dk_dst, dv_dst = dqs4_ref.at[r], dk4_ref.at[r], dv4_ref.at[r]
            else:
                srcs = tuple(x.at[r] for x in hexes)
                dq_dst, dk_dst, dv_dst = dqs16_ref.at[r], dk16_ref.at[r], dv16_ref.at[r]
            q_src, k_src, v_src, do_src, lse_src, delta_src = srcs
            q2, kb, vb = q_src[qrows, :], _keys(k_src, krows, first), _keys(v_src, krows, first)
            do2, lse2, delta2 = do_src[qrows, :], lse_src[qrows, :], delta_src[qrows, :]
            head0 = _head0_lanes()
            dqs, dk, dv = [], None, None
            for hh in range(2):
                col = slice(hh * HEAD_DIM, hh * HEAD_DIM + 1)
                q, dob = _one_head(q2, head0, hh), _one_head(do2, head0, hh)
                p = jnp.exp(_attn_scores(q, kb, b_ref[_table(seg, first), hh]) - lse2[:, col])
                dvh = _dot_tn(p.astype(BF16), dob)
                ds = p * (_dot_nt(dob, vb) - delta2[:, col])
                if first:
                    db_ref[seg, hh, :, ATTN_BLOCK:] += ds[:, ATTN_BLOCK:]
                else:
                    db_ref[seg, hh] += ds
                dsb = ds.astype(BF16)
                dqs.append(_dot(dsb, kb))
                dkh = _dot_tn(dsb, q)
                dk = dkh if dk is None else dk + dkh
                dv = dvh if dv is None else dv + dvh
            if first:
                dk, dv = dk[ATTN_BLOCK:], dv[ATTN_BLOCK:]
            dq_dst[qrows, :] = jnp.where(head0, dqs[0], dqs[1]) * scale
            if seg == 2:
                dk_dst[krows, :] = dk * scale
                dv_dst[krows, :] = dv
            else:
                dk_dst[krows, :] += dk * scale
                dv_dst[krows, :] += dv

        _attn_schedule(op)
        _interleave_back(dqs16_ref, dqs4_ref, QUAD)
        _interleave_back(dk16_ref, dk4_ref, 0, accumulate=True)
        _interleave_back(dv16_ref, dv4_ref, 0, accumulate=True)

        for r, nat, quad in _quad_tiles():
            dqs_ref[nat, :] += dqs4_ref[r, quad, :] + dqs4_ref[QUAD + r, quad, :]
            dka_ref[nat, :] += dk4_ref[r, quad, :]
            dva_ref[nat, :] += dv4_ref[r, quad, :]

        def merge(i, carry):
            rows = pl.ds(pl.multiple_of(i * MERGE_ROWS, MERGE_ROWS), MERGE_ROWS)
            dq_ref[rows, :] = dqs_ref[rows, :].astype(BF16)
            dk_ref[rows, :] = dka_ref[rows, :].astype(BF16)
            dv_ref[rows, :] = dva_ref[rows, :].astype(BF16)
            return carry

        lax.fori_loop(0, SEQ // MERGE_ROWS, merge, 0)

    def pspec(off):
        return pl.BlockSpec((SEQ, LANE), lambda p, b: (b, off // LANE + p))

    ospec = pl.BlockSpec((SEQ, LANE), lambda p, b: (b, p))
    bspec = pl.BlockSpec((nseg, 2, ATTN_BLOCK, 2 * ATTN_BLOCK), lambda p, b: (0, p, 0, 0))
    gshape = SDS((n, B_WIDTH), BF16)
    return _call(
        body, name=name, grid=(HEAD_PAIRS, nb_local),
        in_specs=[pspec(Q_OFF), pspec(K_OFF), pspec(V_OFF), ospec,
                  pl.BlockSpec((SEQ, LANE), lambda p, b: (b, a_blocks + p)), ospec,
                  pl.BlockSpec((2 * nseg, 2, ATTN_BLOCK, 2 * ATTN_BLOCK), lambda p, b: (0, p, 0, 0))],
        out_specs=[ospec, ospec, ospec, bspec],
        out_shape=[gshape, gshape, gshape, SDS((nseg, B_HEADS, ATTN_BLOCK, 2 * ATTN_BLOCK), F32)],
        scratch_shapes=[pltpu.VMEM((SEQ, LANE), F32)] * 4 + [pltpu.VMEM((QUAD, QUAD_ROWS, LANE), F32)] * 6
        + [pltpu.VMEM((2 * QUAD, QUAD_ROWS, LANE), F32)] + [pltpu.VMEM((QUAD, QUAD_ROWS, LANE), F32)] * 2
        + [pltpu.VMEM((QUAD * QUAD, ATTN_BLOCK, LANE), F32)] * 9,
        args=[proj, proj, proj, b_out, dmix, lse_tot, bias], sem=("arbitrary", "arbitrary"), ride=ride)


PAD = 8
CONV_ROWS = 64


CONV_LANES = 128


def _conv_taps(gp_ref, head_ref, r0, ls):
    g0 = gp_ref[r0:r0 + CONV_ROWS, ls]
    if r0 == 0:
        return g0, head_ref[PAD - 1:PAD - 1 + CONV_ROWS, ls], head_ref[PAD - 2:PAD - 2 + CONV_ROWS, ls]
    return g0, gp_ref[r0 - 1:r0 - 1 + CONV_ROWS, ls], gp_ref[r0 - 2:r0 - 2 + CONV_ROWS, ls]


def _fill_head(gp_ref, head_ref):
    head_ref[0:PAD, :] = jnp.zeros((PAD, CONV_LANES), F32)
    head_ref[PAD:PAD + CONV_ROWS, :] = gp_ref[0:CONV_ROWS, :]


def _lane_passes():
    return [slice(l0, l0 + LANE) for l0 in range(0, CONV_LANES, LANE)]


def conv_gelu_fwd(gp, up, cw, cb, nb_local, name):
    n, f = gp.shape

    def body(gp_ref, up_ref, cw_ref, cb_ref, o_ref, head_ref):
        _fill_head(gp_ref, head_ref)
        for ls in _lane_passes():
            w0, w1, w2, bias = cw_ref[0:1, ls], cw_ref[1:2, ls], cw_ref[2:3, ls], cb_ref[:, ls]
            for r0 in range(0, SEQ, CONV_ROWS):
                g0, g1, g2 = _conv_taps(gp_ref, head_ref, r0, ls)
                c = bias + w0 * g2 + w1 * g1 + w2 * g0
                o_ref[r0:r0 + CONV_ROWS, ls] = (_gelu(c) * up_ref[r0:r0 + CONV_ROWS, ls]).astype(BF16)

    blk = pl.BlockSpec((SEQ, CONV_LANES), lambda b, j: (b, j))
    return pl.pallas_call(
        body, name=name, grid=(nb_local, f // CONV_LANES),
        in_specs=[blk, blk, pl.BlockSpec((3, CONV_LANES), lambda b, j: (0, j)),
                  pl.BlockSpec((1, CONV_LANES), lambda b, j: (0, j))],
        out_specs=blk,
        out_shape=SDS((n, f), BF16),
        scratch_shapes=[pltpu.VMEM((PAD + CONV_ROWS, CONV_LANES), F32)],
        compiler_params=_params(("parallel", "parallel")),
    )(gp, up, cw, cb)


def conv_gelu_bwd(dgu, gp, up, cw, cb, nb_local, name, ride=None):
    n, f = gp.shape

    def fold(v):
        return jnp.sum(v.reshape(CONV_ROWS // 8, 8, LANE), axis=0)

    def body(dgu_ref, gp_ref, up_ref, cw_ref, cb_ref, dgp_ref, dup_ref, dcw_ref, dcb_ref, head_ref, dc_ref):
        b = pl.program_id(1)
        _fill_head(gp_ref, head_ref)
        dc_ref[SEQ:SEQ + PAD, :] = jnp.zeros((PAD, CONV_LANES), F32)
        for ls in _lane_passes():
            w0, w1, w2, bias = cw_ref[0:1, ls], cw_ref[1:2, ls], cw_ref[2:3, ls], cb_ref[:, ls]
            sums = [jnp.zeros((8, LANE), F32) for _ in range(4)]
            for r0 in range(0, SEQ, CONV_ROWS):
                rows = slice(r0, r0 + CONV_ROWS)
                g0, g1, g2 = _conv_taps(gp_ref, head_ref, r0, ls)
                gg, dgg = _gelu_and_grad(bias + w0 * g2 + w1 * g1 + w2 * g0)
                dgu = dgu_ref[rows, ls].astype(F32)
                dup_ref[rows, ls] = (dgu * gg).astype(BF16)
                dc = dgu * up_ref[rows, ls] * dgg
                dc_ref[rows, ls] = dc
                sums = [sums[0] + fold(dc * g2), sums[1] + fold(dc * g1), sums[2] + fold(dc * g0), sums[3] + fold(dc)]
            for r0 in range(0, SEQ, CONV_ROWS):
                dgp_ref[r0:r0 + CONV_ROWS, ls] = (
                    w2 * dc_ref[r0:r0 + CONV_ROWS, ls] + w1 * dc_ref[r0 + 1:r0 + 1 + CONV_ROWS, ls]
                    + w0 * dc_ref[r0 + 2:r0 + 2 + CONV_ROWS, ls]).astype(BF16)
            dcw = jnp.concatenate([jnp.sum(s, axis=0, keepdims=True) for s in sums[:3]], axis=0)
            dcb = jnp.sum(sums[3], axis=0, keepdims=True)

            @pl.when(b == 0)
            def _(dcw=dcw, dcb=dcb, ls=ls):
                dcw_ref[:, ls] = dcw
                dcb_ref[:, ls] = dcb

            @pl.when(b > 0)
            def _(dcw=dcw, dcb=dcb, ls=ls):
                dcw_ref[:, ls] += dcw
                dcb_ref[:, ls] += dcb

    blk = pl.BlockSpec((SEQ, CONV_LANES), lambda j, b: (b, j))
    wspec = pl.BlockSpec((3, CONV_LANES), lambda j, b: (0, j))
    bspec = pl.BlockSpec((1, CONV_LANES), lambda j, b: (0, j))
    return _call(
        body, name=name, grid=(f // CONV_LANES, nb_local),
        in_specs=[blk, blk, blk, wspec, bspec], out_specs=[blk, blk, wspec, bspec],
        out_shape=[SDS((n, f), BF16), SDS((n, f), BF16), SDS((3, f), F32), SDS((1, f), F32)],
        scratch_shapes=[pltpu.VMEM((PAD + CONV_ROWS, CONV_LANES), F32), pltpu.VMEM((SEQ + PAD, CONV_LANES), F32)],
        args=[dgu, gp, up, cw, cb], sem=("parallel", "arbitrary"), ride=ride)


def norm_mid_epilogue(x1, dout, z2, g3, g2):
    n, d = x1.shape

    def fn(dh2, step, x1_ref, dout_ref, z2_ref, g3_ref, g2_ref, dx1_ref, dz2_ref, dg3_ref, dg2_ref):
        dxa, dg3r = _rms_bwd(dh2, x1_ref[...], g3_ref[...])
        dx1 = dout_ref[...] + dxa
        dx1_ref[...] = dx1
        dz2, dg2r = _rms_bwd(dx1, z2_ref[...], g2_ref[...])
        dz2_ref[...] = dz2.astype(BF16)
        _accumulate(dg3_ref, jnp.sum(dg3r, axis=0, keepdims=True), step)
        _accumulate(dg2_ref, jnp.sum(dg2r, axis=0, keepdims=True), step)

    return fn, [x1, dout, z2, g3, g2], [SDS((n, d), F32), SDS((n, d), BF16), SDS((1, d), F32), SDS((1, d), F32)]


def norm_in_epilogue(x, dx1, g1):
    n, d = x.shape

    def fn(dh1, step, x_ref, dx1_ref, g1_ref, dx_ref, dg1_ref):
        dxa, dgr = _rms_bwd(dh1, x_ref[...], g1_ref[...])
        dx_ref[...] = dx1_ref[...] + dxa
        _accumulate(dg1_ref, jnp.sum(dgr, axis=0, keepdims=True), step)

    return fn, [x, dx1, g1], [SDS((n, d), F32), SDS((1, d), F32)]


def cast_bf16(arrays, name):
    def body(*refs):
        for i_ref, o_ref in zip(refs[:len(arrays)], refs[len(arrays):]):
            o_ref[...] = i_ref[...].astype(BF16)

    return pl.pallas_call(body, name=name, out_shape=[SDS(a.shape, BF16) for a in arrays],
                          compiler_params=_params())(*arrays)


def adam_update(parts, w, m, v, name, tr=None):
    s, r, c = parts.shape
    tr = r if tr is None else tr
    bc1 = 1.0 - ADAM_B1 ** ADAM_STEP
    bc2 = 1.0 - ADAM_B2 ** ADAM_STEP

    def body(p_ref, w_ref, m_ref, v_ref, g_ref, d_ref, nm_ref, nv_ref):
        g = p_ref[0].astype(F32)
        for j in range(1, s):
            g = g + p_ref[j].astype(F32)
        nm = ADAM_B1 * m_ref[...] + (1.0 - ADAM_B1) * g
        nv = ADAM_B2 * v_ref[...] + (1.0 - ADAM_B2) * (g * g)
        g_ref[...] = g
        nm_ref[...] = nm
        nv_ref[...] = nv
        d_ref[...] = -ADAM_LR * ((nm / bc1) / (jnp.sqrt(nv / bc2) + ADAM_EPS) + ADAM_WD * w_ref[...])

    blk = pl.BlockSpec((tr, c), lambda i: (i, 0))
    return pl.pallas_call(
        body, name=name, grid=(r // tr,),
        in_specs=[pl.BlockSpec((s, tr, c), lambda i: (0, i, 0)), blk, blk, blk],
        out_specs=[blk] * 4, out_shape=[SDS((r, c), F32)] * 4,
        compiler_params=_params(("parallel",)),
    )(parts, w, m, v)


EARLY_NAMES = ("spatial_w", "norm_mix_post", "norm_ffn_pre", "norm_ffn_post", "conv_b", "ln_v_gain", "ln_v_bias",
               "spatial_b")
LATE_NAMES = ("norm_mix_pre", "rel_bias")
PACK_ROW_ALIGN = 8


def _pack_rows(size):
    rows = -(-size // LANE)
    return -(-rows // PACK_ROW_ALIGN) * PACK_ROW_ALIGN


def _pack(arrays):
    flat = []
    for a in arrays:
        rows = _pack_rows(a.size)
        flat.append(jnp.pad(a.reshape(-1), (0, rows * LANE - a.size)))
    return jnp.concatenate(flat).reshape(-1, LANE)


def _unpack(packed, shapes):
    out, row = [], 0
    for shp in shapes:
        size = int(np.prod(shp))
        out.append(packed[row:row + _pack_rows(size)].reshape(-1)[:size].reshape(shp))
        row += _pack_rows(size)
    return out


def kernel(x, norm_mix_pre, norm_mix_post, norm_ffn_pre, norm_ffn_post, w_in, ln_v_gain, ln_v_bias, spatial_w, spatial_b, rel_bias, w_out, w_gate, w_up, conv_w, conv_b, w_down, loss_target, m_norm_mix_pre, m_norm_mix_post, m_norm_ffn_pre, m_norm_ffn_post, m_w_in, m_ln_v_gain, m_ln_v_bias, m_spatial_w, m_spatial_b, m_rel_bias, m_w_out, m_w_gate, m_w_up, m_conv_w, m_conv_b, m_w_down, v_norm_mix_pre, v_norm_mix_post, v_norm_ffn_pre, v_norm_ffn_post, v_w_in, v_ln_v_gain, v_ln_v_bias, v_spatial_w, v_spatial_b, v_rel_bias, v_w_out, v_w_gate, v_w_up, v_conv_w, v_conv_b, v_w_down):
    given = dict(locals())
    nb_local, seq, d = x.shape
    n = nb_local * seq
    cols = w_in.shape[2]

    def by_columns(g):
        return g.transpose(1, 0, 2).reshape(g.shape[1], N_DEV * g.shape[2])

    def by_rows(g):
        return g.reshape(N_DEV * g.shape[1], g.shape[2])

    def blocks(g):
        return g.reshape(N_DEV, g.shape[0] // N_DEV, g.shape[1])

    xf, target = x.reshape(n, d), loss_target.reshape(n, d)
    ln_g, ln_b = ln_v_gain.reshape(1, A_WIDTH), ln_v_bias.reshape(1, A_WIDTH)
    spatial_bt, rel_bias_t = spatial_b[0].T, rel_bias.T

    s_in, s_out, s_gate, s_up, s_down = cast_bf16(
        [w_in[0].T, w_out[0], w_gate[0].T, w_up[0].T, w_down[0]], "cast_shards")
    g_in, g_cw = exchange([], [s_in, conv_w[0]], "gather_w_in")
    w_in_t, conv_w_f = by_rows(g_in), by_columns(g_cw)

    (h1, proj), _ = norm_mm(xf, norm_mix_pre, [w_in_t], "fwd_norm_in", tn=IN_COLS)
    a = gating_fwd(proj, ln_g, ln_b, spatial_w[0], spatial_bt, "fwd_gating")
    bias = bias_tables(rel_bias_t, "bias_tables").reshape(2 * len(DILATIONS), B_HEADS, ATTN_BLOCK, 2 * ATTN_BLOCK)
    (b_out, lse_tot), (g_out, g_gate, g_up) = attn_fwd(proj, bias, nb_local, "fwd_attn",
                                                       ride=([], [s_out, s_gate, s_up]))
    w_out_f, w_gate_t, w_up_t = by_rows(g_out), by_rows(g_gate), by_rows(g_up)
    z2, x1 = mm_res_norm([a, b_out], w_out_f, xf, norm_mix_post, "fwd_out_norm")
    (h2, gp, up), (g_down,) = norm_mm(x1, norm_ffn_pre, [w_gate_t, w_up_t], "fwd_norm_ffn", tm=256, tn=D_FF,
                                      ride=([], [s_down]))
    w_down_f = by_rows(g_down)
    gu = conv_gelu_fwd(gp, up, conv_w_f, conv_b, nb_local, "fwd_conv_gelu")
    dy, dout, dg4, loss_part = down_loss(gu, w_down_f, x1, norm_ffn_post, target, "fwd_down_loss")

    p_down = mm_tn([gu], [dy], "bwd_dw_down", t1=256, t2=D_MODEL)
    (dgu,), _ = mm_nt([(dy, 0, 0)], [w_down_f], "bwd_dgu", out_dtype=BF16)
    (dgp, dup, p_conv_w, p_conv_b), (r_down,) = conv_gelu_bwd(
        dgu, gp, up, conv_w_f, conv_b, nb_local, "bwd_conv_gelu", ride=([blocks(p_down)], []))
    p_gate = mm_tn([dgp], [h2], "bwd_dw_gate", t1=256, t2=D_MODEL)
    p_up = mm_tn([dup], [h2], "bwd_dw_up", t1=256, t2=D_MODEL)
    (dx1, dz2, dg3, dg2), _ = mm_nt([(dgp, 0, 0), (dup, 1, 0)], [w_gate_t, w_up_t], "bwd_dh2_norm_mid", tm=256,
                                    by_rows=True,
                                    epilogue=norm_mid_epilogue(x1, dout, z2, norm_ffn_pre, norm_mix_post))
    p_out = mm_tn([a, b_out], [dz2], "bwd_dw_out", t1=256, t2=D_MODEL)
    (dmix,), _ = mm_nt([(dz2, 0, 0)], [w_out_f], "bwd_dmix")
    duv, p_ws, p_sbt, p_lng, p_lnb = gating_bwd(proj, dmix, ln_g, ln_b, spatial_w[0], spatial_bt, "bwd_gating")
    small = dict(spatial_w=p_ws, norm_mix_post=dg2, norm_ffn_pre=dg3, norm_ffn_post=dg4, conv_b=p_conv_b,
                 ln_v_gain=p_lng, ln_v_bias=p_lnb, spatial_b=p_sbt.T)
    pack_early = _pack([small[k] for k in EARLY_NAMES] + [p_conv_w, loss_part])
    (dq, dk, dv, dbias), (r_gate, r_up, r_out, r_early) = attn_bwd(
        proj, b_out, dmix, lse_tot, bias, nb_local, "bwd_attn",
        ride=([blocks(p_gate), blocks(p_up), blocks(p_out)], [pack_early]))
    p_rel_bias_t = rel_bias_grad(dbias.reshape(len(DILATIONS), B_HEADS, BIAS_SIZE), "bwd_rel_bias")
    p_in = mm_tn([duv, dq, dk, dv], [h1], "bwd_dw_in", t1=256, t2=D_MODEL)
    (grad_x, dg1), (r_in,) = mm_nt(
        [(duv, 0, 0), (dq, 0, Q_OFF), (dk, 0, K_OFF), (dv, 0, V_OFF)], [w_in_t], "bwd_dh1_norm_in", by_rows=True,
        epilogue=norm_in_epilogue(xf, dx1, norm_mix_pre), ride=([blocks(p_in)], []))
    small.update(norm_mix_pre=dg1, rel_bias=p_rel_bias_t.T)
    (r_late,) = exchange([], [_pack([small[k] for k in LATE_NAMES])], "exchange_late")

    res = {}
    for k, received in (("w_in", r_in), ("w_gate", r_gate), ("w_up", r_up)):
        res[k] = [o.T for o in adam_update(received, given[k][0].T, given["m_" + k][0].T, given["v_" + k][0].T,
                                           "adam_" + k, tr=cols // 2)]
    res["w_out"] = adam_update(r_out, w_out[0], m_w_out[0], v_w_out[0], "adam_w_out")
    res["w_down"] = adam_update(r_down, w_down[0], m_w_down[0], v_w_down[0], "adam_w_down", tr=cols // 2)

    def adam_packed(received, names, tail, name):
        zeros = [jnp.zeros_like(t) for t in tail]
        packs = [_pack([given[pre + k] for k in names] + zeros) for pre in ("", "m_", "v_")]
        shapes = [given[k].shape for k in names] + [t.shape for t in tail]
        unpacked = [_unpack(p, shapes) for p in adam_update(received, *packs, name)]
        for i, k in enumerate(names):
            res[k] = [u[i] for u in unpacked]
        return unpacked[0][len(names):]

    g_conv_w_full, loss_sum = adam_packed(r_early, EARLY_NAMES, [p_conv_w, loss_part], "adam_small_early")
    adam_packed(r_late, LATE_NAMES, [], "adam_small_late")
    g_conv_w = lax.dynamic_slice_in_dim(g_conv_w_full, _my_index() * cols, cols, axis=1)
    res["conv_w"] = adam_update(g_conv_w[None], conv_w[0], m_conv_w[0], v_conv_w[0], "adam_conv_w")
    loss = loss_sum[0, 0]

    names = ("norm_mix_pre", "norm_mix_post", "norm_ffn_pre", "norm_ffn_post", "w_in", "ln_v_gain", "ln_v_bias",
             "spatial_w", "spatial_b", "rel_bias", "w_out", "w_gate", "w_up", "conv_w", "conv_b", "w_down")
    outs = [loss, grad_x.reshape(x.shape)]
    for t in range(4):
        outs += [res[k][t].reshape(given[k].shape) for k in names]
    return tuple(outs)
```

```python
import functools
import math

import numpy as np
import jax
import jax.numpy as jnp
from jax import lax
from jax.experimental import pallas as pl
from jax.experimental.pallas import tpu as pltpu

F32 = jnp.float32
BF16 = jnp.bfloat16
SDS = jax.ShapeDtypeStruct

D_MODEL = 1024
SEQ = 2048
HEAD_DIM = 64
A_GROUPS = 4
A_WIDTH = A_GROUPS * HEAD_DIM
B_HEADS = 12
B_WIDTH = B_HEADS * HEAD_DIM
HEAD_PAIRS = B_HEADS // 2
CHUNK = 128
ATTN_BLOCK = 128
DILATIONS = (1, 4, 16)
NUM_BUCKETS = 32
MAX_DISTANCE = 2048
D_FF = 2816
IN_COLS = 2 * A_WIDTH + 3 * B_WIDTH
Q_OFF = 2 * A_WIDTH
K_OFF = Q_OFF + B_WIDTH
V_OFF = K_OFF + B_WIDTH
NORM_EPS = 1e-6
NEG_INF = -1e30
N_DEV = 8
LANE = 128

ADAM_LR = 0.001
ADAM_B1 = 0.9
ADAM_B2 = 0.999
ADAM_EPS = 1e-08
ADAM_WD = 0.01
ADAM_STEP = 10

GELU_C0 = math.sqrt(2.0 / math.pi)
GELU_C1 = 0.044715

VMEM_LIMIT = 56 * 1024 * 1024


def _params(sem=None):
    if sem is None:
        return pltpu.CompilerParams(vmem_limit_bytes=VMEM_LIMIT)
    return pltpu.CompilerParams(dimension_semantics=sem, vmem_limit_bytes=VMEM_LIMIT)


def _gelu(x):
    t = jnp.tanh(GELU_C0 * (x + GELU_C1 * x * x * x))
    return 0.5 * x * (1.0 + t)


def _gelu_and_grad(x):
    x2 = x * x
    t = jnp.tanh(GELU_C0 * (x + GELU_C1 * x * x2))
    g = 0.5 * x * (1.0 + t)
    dg = 0.5 * (1.0 + t) + 0.5 * x * (1.0 - t * t) * (GELU_C0 * (1.0 + 3.0 * GELU_C1 * x2))
    return g, dg


def _dot(a, b):
    return jnp.dot(a, b, preferred_element_type=F32)


def _dot_nt(a, b):
    return lax.dot_general(a, b, (((1,), (1,)), ((), ())), preferred_element_type=F32)


def _dot_tn(a, b):
    return lax.dot_general(a, b, (((0,), (0,)), ((), ())), preferred_element_type=F32)


def _rms_bwd(d, xin, g):
    r = lax.rsqrt(jnp.mean(xin * xin, axis=-1, keepdims=True) + NORM_EPS)
    xh = xin * r
    gd = g * d
    dx = r * (gd - xh * jnp.mean(gd * xh, axis=-1, keepdims=True))
    return dx, d * xh


MESH = pl.DeviceIdType.MESH
ANY = pl.BlockSpec(memory_space=pl.ANY)
PEER_MASKS = tuple(range(1, N_DEV))


def _my_index():
    return lax.axis_index("x") * 4 + lax.axis_index("y") * 2 + lax.axis_index("c")


def _peer(mask):
    x, y, c = lax.axis_index("x"), lax.axis_index("y"), lax.axis_index("c")
    px = 1 - x if mask & 4 else x
    py = 1 - y if mask & 2 else y
    pc = 1 - c if mask & 1 else c
    return (px, py, pc), px * 4 + py * 2 + pc


RELAY_AT = 3
SIBLING = 1
CHIP_MASKS = (2, 4, 6)


class _Exchange:
    def __init__(self, nblocked, in_refs, out_refs, sems):
        send_sems, recv_sems, local_sems = sems
        me = _my_index()
        sibling, _ = _peer(SIBLING)
        self.local, self.first, self.relays, self.relayed_in, self.last_in = [], [], [], [], []
        for a, (in_ref, out_ref) in enumerate(zip(in_refs, out_refs)):
            def copy(src, slot, mask, to):
                return pltpu.make_async_remote_copy(
                    src_ref=src, dst_ref=out_ref.at[slot], send_sem=send_sems.at[a, mask - 1],
                    recv_sem=recv_sems.at[a, mask - 1], device_id=to, device_id_type=MESH)

            if a < nblocked:
                self.local.append(pltpu.make_async_copy(in_ref.at[me], out_ref.at[me], local_sems.at[a]))
                for mask in PEER_MASKS:
                    peer, pidx = _peer(mask)
                    self.first.append(copy(in_ref.at[pidx], me, mask, peer))
                    self.last_in.append(copy(in_ref.at[pidx], pidx, mask, peer))
                continue
            self.local.append(pltpu.make_async_copy(in_ref, out_ref.at[me], local_sems.at[a]))
            for mask in (SIBLING,) + CHIP_MASKS:
                peer, pidx = _peer(mask)
                self.first.append(copy(in_ref, me, mask, peer))
                (self.last_in if mask == SIBLING else self.relayed_in).append(copy(in_ref, pidx, mask, peer))
            for mask in CHIP_MASKS:
                _, origin = _peer(mask)
                _, far = _peer(mask | SIBLING)
                self.relays.append(copy(out_ref.at[origin], origin, mask | SIBLING, sibling))
                self.last_in.append(copy(in_ref, far, mask | SIBLING, sibling))

    def start(self):
        for cp in self.local + self.first[::-1]:
            cp.start()

    def relay(self):
        for arrived, onward in zip(self.relayed_in, self.relays):
            arrived.wait_recv()
            onward.start()

    def finish(self):
        for cp in self.first + self.relays:
            cp.wait_send()
        for cp in self.last_in:
            cp.wait_recv()
        for cp in self.local:
            cp.wait()


def _exchange_out_shape(blocked, whole):
    return [SDS(b.shape, b.dtype) for b in blocked] + [SDS((N_DEV,) + w.shape, w.dtype) for w in whole]


def _exchange_sems(n):
    return [pltpu.SemaphoreType.DMA((n, N_DEV - 1)), pltpu.SemaphoreType.DMA((n, N_DEV - 1)),
            pltpu.SemaphoreType.DMA((n,))]


def exchange(blocked, whole, name):
    nb, n = len(blocked), len(blocked) + len(whole)

    def body(*refs):
        ex = _Exchange(nb, refs[:n], refs[n:2 * n], refs[2 * n:])
        ex.start()
        ex.relay()
        ex.finish()

    return pl.pallas_call(
        body, name=name, in_specs=[ANY] * n, out_specs=[ANY] * n, out_shape=_exchange_out_shape(blocked, whole),
        scratch_shapes=_exchange_sems(n),
    )(*blocked, *whole)


def _call(body, *, name, grid, in_specs, out_specs, out_shape, args, scratch_shapes=(), sem=None, ride=None):
    out_shape, out_specs, scratch_shapes = list(out_shape), list(out_specs), list(scratch_shapes)
    if ride is None:
        outs = pl.pallas_call(body, name=name, grid=grid, in_specs=list(in_specs), out_specs=out_specs,
                              out_shape=out_shape, scratch_shapes=scratch_shapes,
                              compiler_params=_params(sem))(*args)
        return list(outs), []
    blocked, whole = ride
    cargs = list(blocked) + list(whole)
    nb, nc = len(blocked), len(cargs)
    n_in, n_out, n_scr = len(args), len(out_shape), len(scratch_shapes)
    steps = math.prod(grid)
    assert steps >= 3, grid

    def riding(*refs):
        ins, refs = refs[:n_in], refs[n_in:]
        cins, refs = refs[:nc], refs[nc:]
        outs, refs = refs[:n_out], refs[n_out:]
        couts, refs = refs[:nc], refs[nc:]
        scr, sems = refs[:n_scr], refs[n_scr:]
        step = functools.reduce(lambda acc, k: acc * grid[k] + pl.program_id(k), range(len(grid)), 0)

        @pl.when(step == 0)
        def _():
            _Exchange(nb, cins, couts, sems).start()

        @pl.when(step == RELAY_AT * steps // 4)
        def _():
            _Exchange(nb, cins, couts, sems).relay()

        body(*ins, *outs, *scr)

        @pl.when(step == steps - 1)
        def _():
            _Exchange(nb, cins, couts, sems).finish()

    res = pl.pallas_call(
        riding, name=name, grid=grid, in_specs=list(in_specs) + [ANY] * nc, out_specs=out_specs + [ANY] * nc,
        out_shape=out_shape + _exchange_out_shape(blocked, whole),
        scratch_shapes=scratch_shapes + _exchange_sems(nc),
        compiler_params=_params(("arbitrary",) * len(grid)))(*args, *cargs)
    return list(res[:n_out]), list(res[n_out:])


def norm_mm(x, g, ws, name, tm=512, tn=1408, ride=None):
    n, d = x.shape
    f = ws[0].shape[0]
    nw = len(ws)

    def body(x_ref, g_ref, *refs):
        w_refs = refs[:nw]
        h_ref = refs[nw]
        o_refs = refs[nw + 1:]

        @pl.when(pl.program_id(1) == 0)
        def _():
            xv = x_ref[...]
            r = lax.rsqrt(jnp.mean(xv * xv, axis=-1, keepdims=True) + NORM_EPS)
            h_ref[...] = (xv * r * g_ref[...]).astype(BF16)

        h = h_ref[...]
        for w_ref, o_ref in zip(w_refs, o_refs):
            o_ref[...] = _dot_nt(h, w_ref[...])

    return _call(
        body, name=name, grid=(n // tm, f // tn),
        in_specs=[pl.BlockSpec((tm, d), lambda i, j: (i, 0)), pl.BlockSpec((1, d), lambda i, j: (0, 0))]
        + [pl.BlockSpec((tn, d), lambda i, j: (j, 0)) for _ in ws],
        out_specs=[pl.BlockSpec((tm, d), lambda i, j: (i, 0))]
        + [pl.BlockSpec((tm, tn), lambda i, j: (i, j)) for _ in ws],
        out_shape=[SDS((n, d), BF16)] + [SDS((n, f), F32) for _ in ws],
        args=[x, g, *ws], sem=("parallel", "arbitrary"), ride=ride)


def _lane_concat(refs):
    vals = [r[...].astype(BF16) for r in refs]
    return vals[0] if len(vals) == 1 else jnp.concatenate(vals, axis=1)


def mm_res_norm(a_list, w, res, g, name, tm=512):
    n = a_list[0].shape[0]
    k, d = w.shape
    na = len(a_list)

    def body(*refs):
        w_ref, res_ref, g_ref, y_ref, o_ref = refs[na:]
        y = _dot(_lane_concat(refs[:na]), w_ref[...])
        r = lax.rsqrt(jnp.mean(y * y, axis=-1, keepdims=True) + NORM_EPS)
        y_ref[...] = y
        o_ref[...] = res_ref[...] + y * r * g_ref[...]

    return pl.pallas_call(
        body, name=name, grid=(n // tm,),
        in_specs=[pl.BlockSpec((tm, a.shape[1]), lambda i: (i, 0)) for a in a_list]
        + [pl.BlockSpec((k, d), lambda i: (0, 0)),
           pl.BlockSpec((tm, d), lambda i: (i, 0)), pl.BlockSpec((1, d), lambda i: (0, 0))],
        out_specs=[pl.BlockSpec((tm, d), lambda i: (i, 0)), pl.BlockSpec((tm, d), lambda i: (i, 0))],
        out_shape=[SDS((n, d), F32), SDS((n, d), F32)],
        compiler_params=_params(("parallel",)),
    )(*a_list, w, res, g)


def down_loss(a, w, res, g, target, name, tm=256):
    n, k = a.shape
    d = w.shape[1]
    inv_d = 1.0 / d

    def body(a_ref, w_ref, res_ref, g_ref, t_ref, dy_ref, dout_ref, dg_ref, loss_ref):
        i = pl.program_id(0)
        y = _dot(a_ref[...], w_ref[...])
        gv = g_ref[...]
        r = lax.rsqrt(jnp.mean(y * y, axis=-1, keepdims=True) + NORM_EPS)
        yh = y * r
        e = res_ref[...] + yh * gv - t_ref[...]
        part = 0.5 * inv_d * jnp.sum(jnp.sum(e * e, axis=-1, keepdims=True), axis=0, keepdims=True)
        dout = e * inv_d
        dout_ref[...] = dout
        gd = gv * dout
        dy_ref[...] = (r * (gd - yh * jnp.mean(gd * yh, axis=-1, keepdims=True))).astype(BF16)
        dgp = jnp.sum(dout * yh, axis=0, keepdims=True)
        lane0 = lax.broadcasted_iota(jnp.int32, (1, LANE), 1) == 0
        lp = jnp.where(lane0, part, 0.0)

        @pl.when(i == 0)
        def _():
            dg_ref[...] = dgp
            loss_ref[...] = lp

        @pl.when(i > 0)
        def _():
            dg_ref[...] += dgp
            loss_ref[...] += lp

    return pl.pallas_call(
        body, name=name, grid=(n // tm,),
        in_specs=[pl.BlockSpec((tm, k), lambda i: (i, 0)), pl.BlockSpec((k, d), lambda i: (0, 0)),
                  pl.BlockSpec((tm, d), lambda i: (i, 0)), pl.BlockSpec((1, d), lambda i: (0, 0)),
                  pl.BlockSpec((tm, d), lambda i: (i, 0))],
        out_specs=[pl.BlockSpec((tm, d), lambda i: (i, 0)), pl.BlockSpec((tm, d), lambda i: (i, 0)),
                   pl.BlockSpec((1, d), lambda i: (0, 0)), pl.BlockSpec((1, LANE), lambda i: (0, 0))],
        out_shape=[SDS((n, d), BF16), SDS((n, d), F32), SDS((1, d), F32), SDS((1, LANE), F32)],
        compiler_params=_params(("arbitrary",)),
    )(a, w, res, g, target)


def _accumulate(ref, val, step):
    @pl.when(step == 0)
    def _():
        ref[...] = val

    @pl.when(step > 0)
    def _():
        ref[...] += val


def mm_nt(terms, ws, name, tm=512, out_dtype=F32, ride=None, epilogue=None, by_rows=False):
    n = terms[0][0].shape[0]
    r = ws[0].shape[1 if by_rows else 0]
    na = len(terms)
    meta = [(widx, off, a.shape[1]) for a, widx, off in terms]
    fn, extras, out_shape = epilogue if epilogue else (None, [], [SDS((n, r), out_dtype)])
    n_fixed = na + len(ws)

    def body(*refs):
        a_refs = refs[:na]
        w_refs = refs[na:n_fixed]
        acc = None
        for a_ref, (widx, off, k) in zip(a_refs, meta):
            a = a_ref[...].astype(BF16)
            p = _dot(a, w_refs[widx][off:off + k, :]) if by_rows else _dot_nt(a, w_refs[widx][:, off:off + k])
            acc = p if acc is None else acc + p
        if fn is None:
            refs[-1][...] = acc.astype(out_dtype)
        else:
            fn(acc, pl.program_id(0), *refs[n_fixed:])

    def spec(a):
        if a.shape[0] == 1:
            return pl.BlockSpec(a.shape, lambda i: (0, 0))
        return pl.BlockSpec((tm, a.shape[1]), lambda i: (i, 0))

    return _call(
        body, name=name, grid=(n // tm,),
        in_specs=[spec(a) for a, _, _ in terms] + [pl.BlockSpec(w.shape, lambda i: (0, 0)) for w in ws]
        + [spec(e) for e in extras],
        out_specs=[spec(o) for o in out_shape], out_shape=out_shape,
        args=[a for a, _, _ in terms] + list(ws) + list(extras),
        sem=("parallel",) if fn is None else ("arbitrary",), ride=ride)


def _piece_blocks(pieces, tile):
    out, first = [], 0
    for p in pieces:
        nblk, rem = divmod(p.shape[1], tile)
        assert rem == 0, (p.shape, tile)
        out.append((first, nblk))
        first += nblk
    return out, first


def mm_tn(lhs_list, rhs_list, name, t1, t2, out_dtype=BF16):
    n = lhs_list[0].shape[0]
    lblocks, nbl = _piece_blocks(lhs_list, t1)
    rblocks, nbr = _piece_blocks(rhs_list, t2)
    nl = len(lhs_list)

    def body(*refs):
        l_refs, r_refs, o_ref = refs[:nl], refs[nl:-1], refs[-1]
        i, j = pl.program_id(0), pl.program_id(1)
        for l_ref, (ls, ln) in zip(l_refs, lblocks):
            for r_ref, (rs, rn) in zip(r_refs, rblocks):
                @pl.when((i >= ls) & (i < ls + ln) & (j >= rs) & (j < rs + rn))
                def _(l_ref=l_ref, r_ref=r_ref):
                    o_ref[...] = _dot_tn(l_ref[...].astype(BF16), r_ref[...].astype(BF16)).astype(out_dtype)

    def piece_spec(tile, axis, first, nblk):
        def index(i, j):
            return 0, jnp.clip((i, j)[axis] - first, 0, nblk - 1)
        return pl.BlockSpec((n, tile), index)

    return pl.pallas_call(
        body, name=name, grid=(nbl, nbr),
        in_specs=[piece_spec(t1, 0, *b) for b in lblocks] + [piece_spec(t2, 1, *b) for b in rblocks],
        out_specs=pl.BlockSpec((t1, t2), lambda i, j: (i, j)),
        out_shape=SDS((nbl * t1, nbr * t2), out_dtype),
        compiler_params=_params(("parallel", "arbitrary")),
    )(*lhs_list, *rhs_list)


GATE_ROWS = 512


def _tril_mask():
    row = lax.broadcasted_iota(jnp.int32, (CHUNK, CHUNK), 0)
    col = lax.broadcasted_iota(jnp.int32, (CHUNK, CHUNK), 1)
    return row >= col


def _group_of(shape, axis):
    return lax.broadcasted_iota(jnp.int32, shape, axis) // HEAD_DIM


def _group_mean_matrix():
    same = _group_of((A_WIDTH, A_WIDTH), 0) == _group_of((A_WIDTH, A_WIDTH), 1)
    return jnp.where(same, 1.0 / HEAD_DIM, 0.0).astype(BF16)


def _dot_sum(a, b):
    hi = a.astype(BF16)
    lo = (a - hi.astype(F32)).astype(BF16)
    return _dot(hi, b) + _dot(lo, b)


def _by_group(parts, lane_group):
    out = parts[A_GROUPS - 1]
    for g in range(A_GROUPS - 2, -1, -1):
        out = jnp.where(lane_group == g, parts[g], out)
    return out


def _group_norm(gv, gmean):
    xc = gv - _dot_sum(gv, gmean)
    rstd = lax.rsqrt(_dot_sum(xc * xc, gmean) + NORM_EPS)
    return xc * rstd, rstd


def gating_fwd(proj, lng, lnb, ws, sbt, name):
    n = proj.shape[0]

    def body(u_ref, v_ref, lng_ref, lnb_ref, ws_ref, sbt_ref, a_ref):
        tril = _tril_mask()
        lane_group = _group_of((CHUNK, A_WIDTH), 1)
        gmean = _group_mean_matrix()
        wts = [jnp.where(tril, ws_ref[g], 0.0).astype(BF16) for g in range(A_GROUPS)]
        sb = _by_group([sbt_ref[:, g:g + 1] for g in range(A_GROUPS)], lane_group)

        def chunk(c, carry):
            rows = pl.ds(pl.multiple_of(c * CHUNK, CHUNK), CHUNK)
            vhat, _ = _group_norm(_gelu(v_ref[rows, :]), gmean)
            vn = (vhat * lng_ref[...] + lnb_ref[...]).astype(BF16)
            z = _by_group([_dot(wt, vn) for wt in wts], lane_group) + sb
            a_ref[rows, :] = _gelu(u_ref[rows, :]) * z
            return carry

        lax.fori_loop(0, GATE_ROWS // CHUNK, chunk, 0)

    return pl.pallas_call(
        body, name=name, grid=(n // GATE_ROWS,),
        in_specs=[pl.BlockSpec((GATE_ROWS, A_WIDTH), lambda i: (i, 0)),
                  pl.BlockSpec((GATE_ROWS, A_WIDTH), lambda i: (i, 1)),
                  pl.BlockSpec((1, A_WIDTH), lambda i: (0, 0)), pl.BlockSpec((1, A_WIDTH), lambda i: (0, 0)),
                  pl.BlockSpec((A_GROUPS, CHUNK, CHUNK), lambda i: (0, 0, 0)),
                  pl.BlockSpec((CHUNK, A_GROUPS), lambda i: (0, 0))],
        out_specs=pl.BlockSpec((GATE_ROWS, A_WIDTH), lambda i: (i, 0)),
        out_shape=SDS((n, A_WIDTH), F32),
        compiler_params=_params(("parallel",)),
    )(proj, proj, lng, lnb, ws, sbt)


def gating_bwd(proj, dmix, lng, lnb, ws, sbt, name):
    n = proj.shape[0]

    def body(u_ref, v_ref, da_ref, lng_ref, lnb_ref, ws_ref, sbt_ref,
             duv_ref, dws_ref, dsbt_ref, dlng_ref, dlnb_ref):
        @pl.when(pl.program_id(0) == 0)
        def _():
            dws_ref[...] = jnp.zeros_like(dws_ref)
            dsbt_ref[...] = jnp.zeros_like(dsbt_ref)
            dlng_ref[...] = jnp.zeros_like(dlng_ref)
            dlnb_ref[...] = jnp.zeros_like(dlnb_ref)

        tril = _tril_mask()
        lane_group = _group_of((CHUNK, A_WIDTH), 1)
        gmean = _group_mean_matrix()
        gsum = (_group_of((A_WIDTH, LANE), 0) == lax.broadcasted_iota(jnp.int32, (A_WIDTH, LANE), 1)).astype(BF16)
        wts = [jnp.where(tril, ws_ref[g], 0.0) for g in range(A_GROUPS)]
        wts_b = [w.astype(BF16) for w in wts]
        wts_t = [w.T.astype(BF16) for w in wts]
        sb = _by_group([sbt_ref[:, g:g + 1] for g in range(A_GROUPS)], lane_group)
        lg = lng_ref[...]

        def chunk(c, carry):
            rows = pl.ds(pl.multiple_of(c * CHUNK, CHUNK), CHUNK)
            gu, dgu_dx = _gelu_and_grad(u_ref[rows, :])
            gv, dgv_dx = _gelu_and_grad(v_ref[rows, :])
            vhat, rstd = _group_norm(gv, gmean)
            vn = (vhat * lg + lnb_ref[...]).astype(BF16)
            z = _by_group([_dot(wt, vn) for wt in wts_b], lane_group) + sb
            da = da_ref[rows, :]
            dz = da * gu
            dzb = dz.astype(BF16)
            duv_ref[rows, 0:A_WIDTH] = (da * z * dgu_dx).astype(BF16)
            dsbt_ref[...] += _dot_sum(dz, gsum)[:, 0:A_GROUPS]
            for g in range(A_GROUPS):
                dz_g = jnp.where(lane_group == g, dzb, jnp.zeros_like(dzb))
                dws_ref[g] += jnp.where(tril, _dot_nt(dz_g, vn), 0.0)
            dvn = _by_group([_dot(wt, dzb) for wt in wts_t], lane_group)
            dlng_ref[...] += jnp.sum(dvn * vhat, axis=0, keepdims=True)
            dlnb_ref[...] += jnp.sum(dvn, axis=0, keepdims=True)
            dvh = dvn * lg
            dgv = rstd * (dvh - _dot_sum(dvh, gmean) - vhat * _dot_sum(dvh * vhat, gmean))
            duv_ref[rows, A_WIDTH:2 * A_WIDTH] = (dgv * dgv_dx).astype(BF16)
            return carry

        lax.fori_loop(0, GATE_ROWS // CHUNK, chunk, 0)

    return pl.pallas_call(
        body, name=name, grid=(n // GATE_ROWS,),
        in_specs=[pl.BlockSpec((GATE_ROWS, A_WIDTH), lambda i: (i, 0)),
                  pl.BlockSpec((GATE_ROWS, A_WIDTH), lambda i: (i, 1)),
                  pl.BlockSpec((GATE_ROWS, A_WIDTH), lambda i: (i, 0)),
                  pl.BlockSpec((1, A_WIDTH), lambda i: (0, 0)), pl.BlockSpec((1, A_WIDTH), lambda i: (0, 0)),
                  pl.BlockSpec((A_GROUPS, CHUNK, CHUNK), lambda i: (0, 0, 0)),
                  pl.BlockSpec((CHUNK, A_GROUPS), lambda i: (0, 0))],
        out_specs=[pl.BlockSpec((GATE_ROWS, 2 * A_WIDTH), lambda i: (i, 0)),
                   pl.BlockSpec((A_GROUPS, CHUNK, CHUNK), lambda i: (0, 0, 0)),
                   pl.BlockSpec((CHUNK, A_GROUPS), lambda i: (0, 0)),
                   pl.BlockSpec((1, A_WIDTH), lambda i: (0, 0)), pl.BlockSpec((1, A_WIDTH), lambda i: (0, 0))],
        out_shape=[SDS((n, 2 * A_WIDTH), BF16), SDS((A_GROUPS, CHUNK, CHUNK), F32), SDS((CHUNK, A_GROUPS), F32),
                   SDS((1, A_WIDTH), F32), SDS((1, A_WIDTH), F32)],
        compiler_params=_params(("arbitrary",)),
    )(proj, proj, dmix, lng, lnb, ws, sbt)


def _t5_bucket_np(dist):
    max_exact = NUM_BUCKETS // 2
    dd = np.maximum(dist, 1).astype(np.float64)
    large = max_exact + np.log(dd / max_exact) / math.log(MAX_DISTANCE / max_exact) * (NUM_BUCKETS - max_exact)
    large = np.minimum(large.astype(np.int64), NUM_BUCKETS - 1)
    return np.where(dist < max_exact, dist, large)


def _bucket_tables(with_first):
    i = np.arange(ATTN_BLOCK)[:, None]
    j = np.arange(2 * ATTN_BLOCK)[None, :]
    rel = ATTN_BLOCK + i - j
    band = (rel >= 0) & (rel <= ATTN_BLOCK)
    tabs = []
    for own_only in (False, True) if with_first else (False,):
        for dil in DILATIONS:
            b = _t5_bucket_np(np.maximum(rel, 0) * dil)
            tabs.append(np.where(band & (j >= ATTN_BLOCK) if own_only else band, b, -1).reshape(1, -1))
    return np.stack(tabs).astype(np.float32)


BIAS_SIZE = ATTN_BLOCK * 2 * ATTN_BLOCK


def bias_tables(rel_bias_t, name, ride=None):
    idx = jnp.asarray(_bucket_tables(True))
    ntab = idx.shape[0]

    def body(rb_ref, idx_ref, o_ref):
        iv = idx_ref[0]
        bk = lax.broadcasted_iota(jnp.int32, (NUM_BUCKETS, BIAS_SIZE), 0).astype(F32)
        onehot = (bk == iv).astype(F32)
        t = jnp.dot(rb_ref[...], onehot, preferred_element_type=F32, precision=lax.Precision.HIGHEST)
        o_ref[0] = jnp.where(iv < 0.0, NEG_INF, t)

    return _call(
        body, name=name, grid=(ntab,),
        in_specs=[pl.BlockSpec((B_HEADS, NUM_BUCKETS), lambda d: (0, 0)),
                  pl.BlockSpec((1, 1, BIAS_SIZE), lambda d: (d, 0, 0))],
        out_specs=[pl.BlockSpec((1, B_HEADS, BIAS_SIZE), lambda d: (d, 0, 0))],
        out_shape=[SDS((ntab, B_HEADS, BIAS_SIZE), F32)],
        args=[rel_bias_t, idx], sem=("parallel",), ride=ride)


def rel_bias_grad(dbias, name):
    idx = jnp.asarray(_bucket_tables(False))

    def body(db_ref, idx_ref, o_ref):
        d = pl.program_id(0)
        iv = idx_ref[0]
        bk = lax.broadcasted_iota(jnp.int32, (NUM_BUCKETS, BIAS_SIZE), 0).astype(F32)
        onehot = (bk == iv).astype(F32)
        part = lax.dot_general(db_ref[0], onehot, (((1,), (1,)), ((), ())),
                               preferred_element_type=F32, precision=lax.Precision.HIGHEST)

        @pl.when(d == 0)
        def _():
            o_ref[...] = part

        @pl.when(d > 0)
        def _():
            o_ref[...] += part

    return pl.pallas_call(
        body, name=name, grid=(len(DILATIONS),),
        in_specs=[pl.BlockSpec((1, B_HEADS, BIAS_SIZE), lambda d: (d, 0, 0)),
                  pl.BlockSpec((1, 1, BIAS_SIZE), lambda d: (d, 0, 0))],
        out_specs=pl.BlockSpec((B_HEADS, NUM_BUCKETS), lambda d: (0, 0)),
        out_shape=SDS((B_HEADS, NUM_BUCKETS), F32),
        compiler_params=_params(("arbitrary",)),
    )(dbias, idx)


QK_SCALE = 1.0 / math.sqrt(HEAD_DIM)


def _attn_scores(q_scaled, kk, bias):
    return _dot_nt(q_scaled, kk) + bias


def _head0_lanes():
    return lax.broadcasted_iota(jnp.int32, (ATTN_BLOCK, LANE), 1) < HEAD_DIM


def _one_head(x2, head0, hh):
    return jnp.where(head0 if hh == 0 else jnp.logical_not(head0), x2, 0.0).astype(BF16)


def _rows(start, size, dil):
    return pl.ds(start, size) if dil == 1 else pl.ds(start, size, stride=dil)


QUAD = 4
QUAD_ROWS = SEQ // QUAD


def _deinterleave(src_ref, dst_ref):
    for r in range(QUAD):
        for c in range(QUAD_ROWS // ATTN_BLOCK):
            dst_ref[r, c * ATTN_BLOCK:(c + 1) * ATTN_BLOCK, :] = src_ref[
                pl.ds(r + c * QUAD * ATTN_BLOCK, ATTN_BLOCK, stride=QUAD), :]


def _deinterleave_again(src_ref, dst_ref):
    for r in range(QUAD):
        for s in range(QUAD):
            dst_ref[r + QUAD * s] = src_ref[r, pl.ds(s, ATTN_BLOCK, stride=QUAD), :]


def _interleave_back(src_ref, dst_ref, slot0, accumulate=False):
    for r in range(QUAD):
        for s in range(QUAD):
            rows = pl.ds(s, ATTN_BLOCK, stride=QUAD)
            if accumulate:
                dst_ref[slot0 + r, rows, :] += src_ref[r + QUAD * s]
            else:
                dst_ref[slot0 + r, rows, :] = src_ref[r + QUAD * s]


def _quad_tiles():
    return [(r, pl.ds(r + c * QUAD * ATTN_BLOCK, ATTN_BLOCK, stride=QUAD), slice(c * ATTN_BLOCK, (c + 1) * ATTN_BLOCK))
            for r in range(QUAD) for c in range(QUAD_ROWS // ATTN_BLOCK)]


def _attn_schedule(op):
    def d16(i, carry):
        for t in range(2 * QUAD):
            op(2, 2 * QUAD * i + t, 0, 1, True)
        return carry

    lax.fori_loop(0, QUAD // 2, d16, 0)

    def d4(i, carry):
        for u in range(2):
            for nq in range(QUAD_ROWS // ATTN_BLOCK):
                op(1, 2 * i + u, nq * ATTN_BLOCK, 1, nq == 0)
        return carry

    lax.fori_loop(0, QUAD // 2, d4, 0)
    op(0, None, 0, 1, True)
    per_pass = 5

    def d1(j, carry):
        for t in range(per_pass):
            op(0, None, pl.multiple_of((1 + per_pass * j + t) * ATTN_BLOCK, ATTN_BLOCK), 1, False)
        return carry

    lax.fori_loop(0, (SEQ // ATTN_BLOCK - 1) // per_pass, d1, 0)


def _keys(src, krows, first):
    kb = src[krows, :].astype(BF16)
    return jnp.concatenate([kb, kb], axis=0) if first else kb


def _table(seg, first):
    return len(DILATIONS) + seg if first else seg


def _kv_rows(start, dil, first):
    if first:
        return _rows(start, ATTN_BLOCK, dil)
    return _rows(start - ATTN_BLOCK * dil, 2 * ATTN_BLOCK, dil)


MERGE_ROWS = 256


def attn_fwd(proj, bias, nb_local, name, ride=None):
    n = proj.shape[0]
    nseg = len(DILATIONS)

    def body(q_ref, k_ref, v_ref, b_ref, o_ref, lse_ref, q4_ref, k4_ref, v4_ref, os0_ref, ls0_ref, os4_ref, ls4_ref,
             q16_ref, k16_ref, v16_ref, os16_ref, ls16_ref):
        for src, mid, dst in ((q_ref, q4_ref, q16_ref), (k_ref, k4_ref, k16_ref), (v_ref, v4_ref, v16_ref)):
            _deinterleave(src, mid)
            _deinterleave_again(mid, dst)

        def op(seg, r, start, stride, first):
            qrows = _rows(start, ATTN_BLOCK, stride)
            krows = _kv_rows(start, stride, first)
            if seg == 0:
                q_src, k_src, v_src, o_dst, l_dst = q_ref, k_ref, v_ref, os0_ref, ls0_ref
            elif seg == 1:
                q_src, k_src, v_src = q4_ref.at[r], k4_ref.at[r], v4_ref.at[r]
                o_dst, l_dst = os4_ref.at[r], ls4_ref.at[r]
            else:
                q_src, k_src, v_src = q16_ref.at[r], k16_ref.at[r], v16_ref.at[r]
                o_dst, l_dst = os16_ref.at[r], ls16_ref.at[r]
            q2, kb, vb = q_src[qrows, :] * QK_SCALE, _keys(k_src, krows, first), _keys(v_src, krows, first)
            head0 = _head0_lanes()
            outs, lses = [], []
            for hh in range(2):
                s = _attn_scores(_one_head(q2, head0, hh), kb, b_ref[_table(seg, first), hh])
                m = jnp.max(s, axis=-1, keepdims=True)
                p = jnp.exp(s - m)
                l = jnp.sum(p, axis=-1, keepdims=True)
                outs.append(_dot(p.astype(BF16), vb) / l)
                lses.append(jnp.broadcast_to(m + jnp.log(l), (ATTN_BLOCK, LANE)))
            o_dst[qrows, :] = jnp.where(head0, outs[0], outs[1])
            l_dst[qrows, :] = jnp.where(head0, lses[0], lses[1])

        _attn_schedule(op)
        _interleave_back(os16_ref, os4_ref, QUAD)
        _interleave_back(ls16_ref, ls4_ref, QUAD)

        for r, nat, quad in _quad_tiles():
            ls = [ls0_ref[nat, :], ls4_ref[r, quad, :], ls4_ref[QUAD + r, quad, :]]
            m = functools.reduce(jnp.maximum, ls)
            ws = [jnp.exp(l - m) for l in ls]
            den = ws[0] + ws[1] + ws[2]
            num = ws[0] * os0_ref[nat, :] + ws[1] * os4_ref[r, quad, :] + ws[2] * os4_ref[QUAD + r, quad, :]
            o_ref[nat, :] = num / den
            lse_ref[nat, :] = m + jnp.log(den)

    def in_spec(off):
        return pl.BlockSpec((SEQ, LANE), lambda b, p: (b, off // LANE + p))

    out_spec = pl.BlockSpec((SEQ, LANE), lambda b, p: (b, p))
    return _call(
        body, name=name, grid=(nb_local, HEAD_PAIRS),
        in_specs=[in_spec(Q_OFF), in_spec(K_OFF), in_spec(V_OFF),
                  pl.BlockSpec((2 * nseg, 2, ATTN_BLOCK, 2 * ATTN_BLOCK), lambda b, p: (0, p, 0, 0))],
        out_specs=[out_spec, out_spec],
        out_shape=[SDS((n, B_WIDTH), F32), SDS((n, B_WIDTH), F32)],
        scratch_shapes=[pltpu.VMEM((QUAD, QUAD_ROWS, LANE), F32)] * 3 + [pltpu.VMEM((SEQ, LANE), F32)] * 2
        + [pltpu.VMEM((2 * QUAD, QUAD_ROWS, LANE), F32)] * 2 + [pltpu.VMEM((QUAD * QUAD, ATTN_BLOCK, LANE), F32)] * 5,
        args=[proj, proj, proj, bias], sem=("parallel", "arbitrary"), ride=ride)


def attn_bwd(proj, b_out, dmix, lse_tot, bias, nb_local, name, ride=None):
    n = proj.shape[0]
    nseg = len(DILATIONS)
    a_blocks = A_WIDTH // LANE

    def body(q_ref, k_ref, v_ref, o_ref, do_ref, lse_ref, b_ref, dq_ref, dk_ref, dv_ref, db_ref,
             dqs_ref, delta_ref, dka_ref, dva_ref, q4_ref, k4_ref, v4_ref, do4_ref, lse4_ref, delta4_ref,
             dqs4_ref, dk4_ref, dv4_ref, q16_ref, k16_ref, v16_ref, do16_ref, lse16_ref, delta16_ref,
             dqs16_ref, dk16_ref, dv16_ref):
        @pl.when(pl.program_id(1) == 0)
        def _():
            db_ref[...] = jnp.zeros_like(db_ref)

        for acc_ref in (dka_ref, dva_ref, dk4_ref, dv4_ref):
            acc_ref[...] = jnp.zeros_like(acc_ref)
        quads = (q4_ref, k4_ref, v4_ref, do4_ref, lse4_ref, delta4_ref)
        hexes = (q16_ref, k16_ref, v16_ref, do16_ref, lse16_ref, delta16_ref)

        head_sum = (lax.broadcasted_iota(jnp.int32, (LANE, LANE), 0) // HEAD_DIM
                    == lax.broadcasted_iota(jnp.int32, (LANE, LANE), 1) // HEAD_DIM).astype(BF16)

        def row_dots(i, carry):
            rows = pl.ds(pl.multiple_of(i * MERGE_ROWS, MERGE_ROWS), MERGE_ROWS)
            delta_ref[rows, :] = _dot_sum(do_ref[rows, :] * o_ref[rows, :], head_sum)
            return carry

        lax.fori_loop(0, SEQ // MERGE_ROWS, row_dots, 0)
        for src, mid, dst in zip((q_ref, k_ref, v_ref, do_ref, lse_ref, delta_ref), quads, hexes):
            _deinterleave(src, mid)
            _deinterleave_again(mid, dst)

        def op(seg, r, start, stride, first):
            qrows = _rows(start, ATTN_BLOCK, stride)
            krows = _kv_rows(start, stride, first)
            if seg == 0:
                srcs = (q_ref, k_ref, v_ref, do_ref, lse_ref, delta_ref)
                dq_dst, dk_dst, dv_dst = dqs_ref, dka_ref, dva_ref
            elif seg == 1:
                srcs = tuple(x.at[r] for x in quads)
                dq_dst, dk_dst, dv_dst = dqs4_ref.at[r], dk4_ref.at[r], dv4_ref.at[r]
            else:
                srcs = tuple(x.at[r] for x in hexes)
                dq_dst, dk_dst, dv_dst = dqs16_ref.at[r], dk16_ref.at[r], dv16_ref.at[r]
            q_src, k_src, v_src, do_src, lse_src, delta_src = srcs
            q2, kb, vb = q_src[qrows, :] * QK_SCALE, _keys(k_src, krows, first), _keys(v_src, krows, first)
            do2, lse2, delta2 = do_src[qrows, :], lse_src[qrows, :], delta_src[qrows, :]
            head0 = _head0_lanes()
            dqs, dk, dv = [], None, None
            for hh in range(2):
                col = slice(hh * HEAD_DIM, hh * HEAD_DIM + 1)
                q, dob = _one_head(q2, head0, hh), _one_head(do2, head0, hh)
                p = jnp.exp(_attn_scores(q, kb, b_ref[_table(seg, first), hh]) - lse2[:, col])
                dvh = _dot_tn(p.astype(BF16), dob)
                ds = p * (_dot_nt(dob, vb) - delta2[:, col])
                if first:
                    db_ref[seg, hh, :, ATTN_BLOCK:] += ds[:, ATTN_BLOCK:]
                else:
                    db_ref[seg, hh] += ds
                dsb = ds.astype(BF16)
                dqs.append(_dot(dsb, kb))
                dkh = _dot_tn(dsb, q)
                dk = dkh if dk is None else dk + dkh
                dv = dvh if dv is None else dv + dvh
            if first:
                dk, dv = dk[ATTN_BLOCK:], dv[ATTN_BLOCK:]
            dq_dst[qrows, :] = jnp.where(head0, dqs[0], dqs[1]) * QK_SCALE
            if seg == 2:
                dk_dst[krows, :] = dk
                dv_dst[krows, :] = dv
            else:
                dk_dst[krows, :] += dk
                dv_dst[krows, :] += dv

        _attn_schedule(op)
        _interleave_back(dqs16_ref, dqs4_ref, QUAD)
        _interleave_back(dk16_ref, dk4_ref, 0, accumulate=True)
        _interleave_back(dv16_ref, dv4_ref, 0, accumulate=True)

        for r, nat, quad in _quad_tiles():
            dqs_ref[nat, :] += dqs4_ref[r, quad, :] + dqs4_ref[QUAD + r, quad, :]
            dka_ref[nat, :] += dk4_ref[r, quad, :]
            dva_ref[nat, :] += dv4_ref[r, quad, :]

        def merge(i, carry):
            rows = pl.ds(pl.multiple_of(i * MERGE_ROWS, MERGE_ROWS), MERGE_ROWS)
            dq_ref[rows, :] = dqs_ref[rows, :].astype(BF16)
            dk_ref[rows, :] = dka_ref[rows, :].astype(BF16)
            dv_ref[rows, :] = dva_ref[rows, :].astype(BF16)
            return carry

        lax.fori_loop(0, SEQ // MERGE_ROWS, merge, 0)

    def pspec(off):
        return pl.BlockSpec((SEQ, LANE), lambda p, b: (b, off // LANE + p))

    ospec = pl.BlockSpec((SEQ, LANE), lambda p, b: (b, p))
    bspec = pl.BlockSpec((nseg, 2, ATTN_BLOCK, 2 * ATTN_BLOCK), lambda p, b: (0, p, 0, 0))
    gshape = SDS((n, B_WIDTH), BF16)
    return _call(
        body, name=name, grid=(HEAD_PAIRS, nb_local),
        in_specs=[pspec(Q_OFF), pspec(K_OFF), pspec(V_OFF), ospec,
                  pl.BlockSpec((SEQ, LANE), lambda p, b: (b, a_blocks + p)), ospec,
                  pl.BlockSpec((2 * nseg, 2, ATTN_BLOCK, 2 * ATTN_BLOCK), lambda p, b: (0, p, 0, 0))],
        out_specs=[ospec, ospec, ospec, bspec],
        out_shape=[gshape, gshape, gshape, SDS((nseg, B_HEADS, ATTN_BLOCK, 2 * ATTN_BLOCK), F32)],
        scratch_shapes=[pltpu.VMEM((SEQ, LANE), F32)] * 4 + [pltpu.VMEM((QUAD, QUAD_ROWS, LANE), F32)] * 6
        + [pltpu.VMEM((2 * QUAD, QUAD_ROWS, LANE), F32)] + [pltpu.VMEM((QUAD, QUAD_ROWS, LANE), F32)] * 2
        + [pltpu.VMEM((QUAD * QUAD, ATTN_BLOCK, LANE), F32)] * 9,
        args=[proj, proj, proj, b_out, dmix, lse_tot, bias], sem=("arbitrary", "arbitrary"), ride=ride)


PAD = 8
CONV_ROWS = 64


CONV_LANES = 128


def _conv_taps(gp_ref, head_ref, r0, ls):
    g0 = gp_ref[r0:r0 + CONV_ROWS, ls]
    if r0 == 0:
        return g0, head_ref[PAD - 1:PAD - 1 + CONV_ROWS, ls], head_ref[PAD - 2:PAD - 2 + CONV_ROWS, ls]
    return g0, gp_ref[r0 - 1:r0 - 1 + CONV_ROWS, ls], gp_ref[r0 - 2:r0 - 2 + CONV_ROWS, ls]


def _fill_head(gp_ref, head_ref):
    head_ref[0:PAD, :] = jnp.zeros((PAD, CONV_LANES), F32)
    head_ref[PAD:PAD + CONV_ROWS, :] = gp_ref[0:CONV_ROWS, :]


def _lane_passes():
    return [slice(l0, l0 + LANE) for l0 in range(0, CONV_LANES, LANE)]


def conv_gelu_fwd(gp, up, cw, cb, nb_local, name):
    n, f = gp.shape

    def body(gp_ref, up_ref, cw_ref, cb_ref, o_ref, head_ref):
        _fill_head(gp_ref, head_ref)
        for ls in _lane_passes():
            w0, w1, w2, bias = cw_ref[0:1, ls], cw_ref[1:2, ls], cw_ref[2:3, ls], cb_ref[:, ls]
            for r0 in range(0, SEQ, CONV_ROWS):
                g0, g1, g2 = _conv_taps(gp_ref, head_ref, r0, ls)
                c = bias + w0 * g2 + w1 * g1 + w2 * g0
                o_ref[r0:r0 + CONV_ROWS, ls] = (_gelu(c) * up_ref[r0:r0 + CONV_ROWS, ls]).astype(BF16)

    blk = pl.BlockSpec((SEQ, CONV_LANES), lambda b, j: (b, j))
    return pl.pallas_call(
        body, name=name, grid=(nb_local, f // CONV_LANES),
        in_specs=[blk, blk, pl.BlockSpec((3, CONV_LANES), lambda b, j: (0, j)),
                  pl.BlockSpec((1, CONV_LANES), lambda b, j: (0, j))],
        out_specs=blk,
        out_shape=SDS((n, f), BF16),
        scratch_shapes=[pltpu.VMEM((PAD + CONV_ROWS, CONV_LANES), F32)],
        compiler_params=_params(("parallel", "parallel")),
    )(gp, up, cw, cb)


def conv_gelu_bwd(dgu, gp, up, cw, cb, nb_local, name, ride=None):
    n, f = gp.shape

    def fold(v):
        return jnp.sum(v.reshape(CONV_ROWS // 8, 8, LANE), axis=0)

    def body(dgu_ref, gp_ref, up_ref, cw_ref, cb_ref, dgp_ref, dup_ref, dcw_ref, dcb_ref, head_ref, dc_ref):
        b = pl.program_id(1)
        _fill_head(gp_ref, head_ref)
        dc_ref[SEQ:SEQ + PAD, :] = jnp.zeros((PAD, CONV_LANES), F32)
        for ls in _lane_passes():
            w0, w1, w2, bias = cw_ref[0:1, ls], cw_ref[1:2, ls], cw_ref[2:3, ls], cb_ref[:, ls]
            sums = [jnp.zeros((8, LANE), F32) for _ in range(4)]
            for r0 in range(0, SEQ, CONV_ROWS):
                rows = slice(r0, r0 + CONV_ROWS)
                g0, g1, g2 = _conv_taps(gp_ref, head_ref, r0, ls)
                gg, dgg = _gelu_and_grad(bias + w0 * g2 + w1 * g1 + w2 * g0)
                dgu = dgu_ref[rows, ls].astype(F32)
                dup_ref[rows, ls] = (dgu * gg).astype(BF16)
                dc = dgu * up_ref[rows, ls] * dgg
                dc_ref[rows, ls] = dc
                sums = [sums[0] + fold(dc * g2), sums[1] + fold(dc * g1), sums[2] + fold(dc * g0), sums[3] + fold(dc)]
            for r0 in range(0, SEQ, CONV_ROWS):
                dgp_ref[r0:r0 + CONV_ROWS, ls] = (
                    w2 * dc_ref[r0:r0 + CONV_ROWS, ls] + w1 * dc_ref[r0 + 1:r0 + 1 + CONV_ROWS, ls]
                    + w0 * dc_ref[r0 + 2:r0 + 2 + CONV_ROWS, ls]).astype(BF16)
            dcw = jnp.concatenate([jnp.sum(s, axis=0, keepdims=True) for s in sums[:3]], axis=0)
            dcb = jnp.sum(sums[3], axis=0, keepdims=True)

            @pl.when(b == 0)
            def _(dcw=dcw, dcb=dcb, ls=ls):
                dcw_ref[:, ls] = dcw
                dcb_ref[:, ls] = dcb

            @pl.when(b > 0)
            def _(dcw=dcw, dcb=dcb, ls=ls):
                dcw_ref[:, ls] += dcw
                dcb_ref[:, ls] += dcb

    blk = pl.BlockSpec((SEQ, CONV_LANES), lambda j, b: (b, j))
    wspec = pl.BlockSpec((3, CONV_LANES), lambda j, b: (0, j))
    bspec = pl.BlockSpec((1, CONV_LANES), lambda j, b: (0, j))
    return _call(
        body, name=name, grid=(f // CONV_LANES, nb_local),
        in_specs=[blk, blk, blk, wspec, bspec], out_specs=[blk, blk, wspec, bspec],
        out_shape=[SDS((n, f), BF16), SDS((n, f), BF16), SDS((3, f), F32), SDS((1, f), F32)],
        scratch_shapes=[pltpu.VMEM((PAD + CONV_ROWS, CONV_LANES), F32), pltpu.VMEM((SEQ + PAD, CONV_LANES), F32)],
        args=[dgu, gp, up, cw, cb], sem=("parallel", "arbitrary"), ride=ride)


def norm_mid_epilogue(x1, dout, z2, g3, g2):
    n, d = x1.shape

    def fn(dh2, step, x1_ref, dout_ref, z2_ref, g3_ref, g2_ref, dx1_ref, dz2_ref, dg3_ref, dg2_ref):
        dxa, dg3r = _rms_bwd(dh2, x1_ref[...], g3_ref[...])
        dx1 = dout_ref[...] + dxa
        dx1_ref[...] = dx1
        dz2, dg2r = _rms_bwd(dx1, z2_ref[...], g2_ref[...])
        dz2_ref[...] = dz2.astype(BF16)
        _accumulate(dg3_ref, jnp.sum(dg3r, axis=0, keepdims=True), step)
        _accumulate(dg2_ref, jnp.sum(dg2r, axis=0, keepdims=True), step)

    return fn, [x1, dout, z2, g3, g2], [SDS((n, d), F32), SDS((n, d), BF16), SDS((1, d), F32), SDS((1, d), F32)]


def norm_in_epilogue(x, dx1, g1):
    n, d = x.shape

    def fn(dh1, step, x_ref, dx1_ref, g1_ref, dx_ref, dg1_ref):
        dxa, dgr = _rms_bwd(dh1, x_ref[...], g1_ref[...])
        dx_ref[...] = dx1_ref[...] + dxa
        _accumulate(dg1_ref, jnp.sum(dgr, axis=0, keepdims=True), step)

    return fn, [x, dx1, g1], [SDS((n, d), F32), SDS((1, d), F32)]


def cast_bf16(arrays, name):
    def body(*refs):
        for i_ref, o_ref in zip(refs[:len(arrays)], refs[len(arrays):]):
            o_ref[...] = i_ref[...].astype(BF16)

    return pl.pallas_call(body, name=name, out_shape=[SDS(a.shape, BF16) for a in arrays],
                          compiler_params=_params())(*arrays)


def adam_update(parts, w, m, v, name, tr=None):
    s, r, c = parts.shape
    tr = r if tr is None else tr
    bc1 = 1.0 - ADAM_B1 ** ADAM_STEP
    bc2 = 1.0 - ADAM_B2 ** ADAM_STEP

    def body(p_ref, w_ref, m_ref, v_ref, g_ref, d_ref, nm_ref, nv_ref):
        g = p_ref[0].astype(F32)
        for j in range(1, s):
            g = g + p_ref[j].astype(F32)
        nm = ADAM_B1 * m_ref[...] + (1.0 - ADAM_B1) * g
        nv = ADAM_B2 * v_ref[...] + (1.0 - ADAM_B2) * (g * g)
        g_ref[...] = g
        nm_ref[...] = nm
        nv_ref[...] = nv
        d_ref[...] = -ADAM_LR * ((nm / bc1) / (jnp.sqrt(nv / bc2) + ADAM_EPS) + ADAM_WD * w_ref[...])

    blk = pl.BlockSpec((tr, c), lambda i: (i, 0))
    return pl.pallas_call(
        body, name=name, grid=(r // tr,),
        in_specs=[pl.BlockSpec((s, tr, c), lambda i: (0, i, 0)), blk, blk, blk],
        out_specs=[blk] * 4, out_shape=[SDS((r, c), F32)] * 4,
        compiler_params=_params(("parallel",)),
    )(parts, w, m, v)


EARLY_NAMES = ("spatial_w", "norm_mix_post", "norm_ffn_pre", "norm_ffn_post", "conv_b", "ln_v_gain", "ln_v_bias",
               "spatial_b")
LATE_NAMES = ("norm_mix_pre", "rel_bias")
PACK_ROW_ALIGN = 8


def _pack_rows(size):
    rows = -(-size // LANE)
    return -(-rows // PACK_ROW_ALIGN) * PACK_ROW_ALIGN


def _pack(arrays):
    flat = []
    for a in arrays:
        rows = _pack_rows(a.size)
        flat.append(jnp.pad(a.reshape(-1), (0, rows * LANE - a.size)))
    return jnp.concatenate(flat).reshape(-1, LANE)


def _unpack(packed, shapes):
    out, row = [], 0
    for shp in shapes:
        size = int(np.prod(shp))
        out.append(packed[row:row + _pack_rows(size)].reshape(-1)[:size].reshape(shp))
        row += _pack_rows(size)
    return out


def kernel(x, norm_mix_pre, norm_mix_post, norm_ffn_pre, norm_ffn_post, w_in, ln_v_gain, ln_v_bias, spatial_w, spatial_b, rel_bias, w_out, w_gate, w_up, conv_w, conv_b, w_down, loss_target, m_norm_mix_pre, m_norm_mix_post, m_norm_ffn_pre, m_norm_ffn_post, m_w_in, m_ln_v_gain, m_ln_v_bias, m_spatial_w, m_spatial_b, m_rel_bias, m_w_out, m_w_gate, m_w_up, m_conv_w, m_conv_b, m_w_down, v_norm_mix_pre, v_norm_mix_post, v_norm_ffn_pre, v_norm_ffn_post, v_w_in, v_ln_v_gain, v_ln_v_bias, v_spatial_w, v_spatial_b, v_rel_bias, v_w_out, v_w_gate, v_w_up, v_conv_w, v_conv_b, v_w_down):
    given = dict(locals())
    nb_local, seq, d = x.shape
    n = nb_local * seq
    cols = w_in.shape[2]

    def by_columns(g):
        return g.transpose(1, 0, 2).reshape(g.shape[1], N_DEV * g.shape[2])

    def by_rows(g):
        return g.reshape(N_DEV * g.shape[1], g.shape[2])

    def blocks(g):
        return g.reshape(N_DEV, g.shape[0] // N_DEV, g.shape[1])

    xf, target = x.reshape(n, d), loss_target.reshape(n, d)
    ln_g, ln_b = ln_v_gain.reshape(1, A_WIDTH), ln_v_bias.reshape(1, A_WIDTH)
    spatial_bt, rel_bias_t = spatial_b[0].T, rel_bias.T

    s_in, s_out, s_gate, s_up, s_down = cast_bf16(
        [w_in[0].T, w_out[0], w_gate[0].T, w_up[0].T, w_down[0]], "cast_shards")
    (bias,), (g_in, g_cw) = bias_tables(rel_bias_t, "bias_tables", ride=([], [s_in, conv_w[0]]))
    bias = bias.reshape(2 * len(DILATIONS), B_HEADS, ATTN_BLOCK, 2 * ATTN_BLOCK)
    w_in_t, conv_w_f = by_rows(g_in), by_columns(g_cw)

    (h1, proj), _ = norm_mm(xf, norm_mix_pre, [w_in_t], "fwd_norm_in", tn=IN_COLS)
    a = gating_fwd(proj, ln_g, ln_b, spatial_w[0], spatial_bt, "fwd_gating")
    (b_out, lse_tot), (g_out, g_gate, g_up) = attn_fwd(proj, bias, nb_local, "fwd_attn",
                                                       ride=([], [s_out, s_gate, s_up]))
    w_out_f, w_gate_t, w_up_t = by_rows(g_out), by_rows(g_gate), by_rows(g_up)
    z2, x1 = mm_res_norm([a, b_out], w_out_f, xf, norm_mix_post, "fwd_out_norm")
    (h2, gp, up), (g_down,) = norm_mm(x1, norm_ffn_pre, [w_gate_t, w_up_t], "fwd_norm_ffn", tm=256, tn=D_FF,
                                      ride=([], [s_down]))
    w_down_f = by_rows(g_down)
    gu = conv_gelu_fwd(gp, up, conv_w_f, conv_b, nb_local, "fwd_conv_gelu")
    dy, dout, dg4, loss_part = down_loss(gu, w_down_f, x1, norm_ffn_post, target, "fwd_down_loss")

    p_down = mm_tn([gu], [dy], "bwd_dw_down", t1=256, t2=D_MODEL)
    (dgu,), _ = mm_nt([(dy, 0, 0)], [w_down_f], "bwd_dgu", out_dtype=BF16)
    (dgp, dup, p_conv_w, p_conv_b), (r_down,) = conv_gelu_bwd(
        dgu, gp, up, conv_w_f, conv_b, nb_local, "bwd_conv_gelu", ride=([blocks(p_down)], []))
    p_gate = mm_tn([dgp], [h2], "bwd_dw_gate", t1=256, t2=D_MODEL)
    p_up = mm_tn([dup], [h2], "bwd_dw_up", t1=256, t2=D_MODEL)
    (dx1, dz2, dg3, dg2), _ = mm_nt([(dgp, 0, 0), (dup, 1, 0)], [w_gate_t, w_up_t], "bwd_dh2_norm_mid", tm=256,
                                    by_rows=True,
                                    epilogue=norm_mid_epilogue(x1, dout, z2, norm_ffn_pre, norm_mix_post))
    p_out = mm_tn([a, b_out], [dz2], "bwd_dw_out", t1=256, t2=D_MODEL)
    (dmix,), _ = mm_nt([(dz2, 0, 0)], [w_out_f], "bwd_dmix")
    duv, p_ws, p_sbt, p_lng, p_lnb = gating_bwd(proj, dmix, ln_g, ln_b, spatial_w[0], spatial_bt, "bwd_gating")
    small = dict(spatial_w=p_ws, norm_mix_post=dg2, norm_ffn_pre=dg3, norm_ffn_post=dg4, conv_b=p_conv_b,
                 ln_v_gain=p_lng, ln_v_bias=p_lnb, spatial_b=p_sbt.T)
    pack_early = _pack([small[k] for k in EARLY_NAMES] + [p_conv_w, loss_part])
    (dq, dk, dv, dbias), (r_gate, r_up, r_out, r_early) = attn_bwd(
        proj, b_out, dmix, lse_tot, bias, nb_local, "bwd_attn",
        ride=([blocks(p_gate), blocks(p_up), blocks(p_out)], [pack_early]))
    p_rel_bias_t = rel_bias_grad(dbias.reshape(len(DILATIONS), B_HEADS, BIAS_SIZE), "bwd_rel_bias")
    p_in = mm_tn([duv, dq, dk, dv], [h1], "bwd_dw_in", t1=256, t2=D_MODEL)
    (grad_x, dg1), (r_in,) = mm_nt(
        [(duv, 0, 0), (dq, 0, Q_OFF), (dk, 0, K_OFF), (dv, 0, V_OFF)], [w_in_t], "bwd_dh1_norm_in", by_rows=True,
        epilogue=norm_in_epilogue(xf, dx1, norm_mix_pre), ride=([blocks(p_in)], []))
    small.update(norm_mix_pre=dg1, rel_bias=p_rel_bias_t.T)
    (r_late,) = exchange([], [_pack([small[k] for k in LATE_NAMES])], "exchange_late")

    res = {}
    for k, received in (("w_in", r_in), ("w_gate", r_gate), ("w_up", r_up)):
        res[k] = [o.T for o in adam_update(received, given[k][0].T, given["m_" + k][0].T, given["v_" + k][0].T,
                                           "adam_" + k, tr=cols // 2)]
    res["w_out"] = adam_update(r_out, w_out[0], m_w_out[0], v_w_out[0], "adam_w_out")
    res["w_down"] = adam_update(r_down, w_down[0], m_w_down[0], v_w_down[0], "adam_w_down", tr=cols // 2)

    def adam_packed(received, names, tail, name):
        zeros = [jnp.zeros_like(t) for t in tail]
        packs = [_pack([given[pre + k] for k in names] + zeros) for pre in ("", "m_", "v_")]
        shapes = [given[k].shape for k in names] + [t.shape for t in tail]
        unpacked = [_unpack(p, shapes) for p in adam_update(received, *packs, name)]
        for i, k in enumerate(names):
            res[k] = [u[i] for u in unpacked]
        return unpacked[0][len(names):]

    g_conv_w_full, loss_sum = adam_packed(r_early, EARLY_NAMES, [p_conv_w, loss_part], "adam_small_early")
    adam_packed(r_late, LATE_NAMES, [], "adam_small_late")
    g_conv_w = lax.dynamic_slice_in_dim(g_conv_w_full, _my_index() * cols, cols, axis=1)
    res["conv_w"] = adam_update(g_conv_w[None], conv_w[0], m_conv_w[0], v_conv_w[0], "adam_conv_w")
    loss = loss_sum[0, 0]

    names = ("norm_mix_pre", "norm_mix_post", "norm_ffn_pre", "norm_ffn_post", "w_in", "ln_v_gain", "ln_v_bias",
             "spatial_w", "spatial_b", "rel_bias", "w_out", "w_gate", "w_up", "conv_w", "conv_b", "w_down")
    outs = [loss, grad_x.reshape(x.shape)]
    for t in range(4):
        outs += [res[k][t].reshape(given[k].shape) for k in names]
    return tuple(outs)
```

```python
import functools
import math

import numpy as np
import jax
import jax.numpy as jnp
from jax import lax
from jax.experimental import pallas as pl
from jax.experimental.pallas import tpu as pltpu

F32 = jnp.float32
BF16 = jnp.bfloat16
SDS = jax.ShapeDtypeStruct

D_MODEL = 1024
SEQ = 2048
HEAD_DIM = 64
A_GROUPS = 4
A_WIDTH = A_GROUPS * HEAD_DIM
B_HEADS = 12
B_WIDTH = B_HEADS * HEAD_DIM
HEAD_PAIRS = B_HEADS // 2
CHUNK = 128
ATTN_BLOCK = 128
DILATIONS = (1, 4, 16)
NUM_BUCKETS = 32
MAX_DISTANCE = 2048
D_FF = 2816
IN_COLS = 2 * A_WIDTH + 3 * B_WIDTH
Q_OFF = 2 * A_WIDTH
K_OFF = Q_OFF + B_WIDTH
V_OFF = K_OFF + B_WIDTH
NORM_EPS = 1e-6
NEG_INF = -1e30
N_DEV = 8
LANE = 128

ADAM_LR = 0.001
ADAM_B1 = 0.9
ADAM_B2 = 0.999
ADAM_EPS = 1e-08
ADAM_WD = 0.01
ADAM_STEP = 10

GELU_C0 = math.sqrt(2.0 / math.pi)
GELU_C1 = 0.044715

VMEM_LIMIT = 56 * 1024 * 1024


def _params(sem=None):
    if sem is None:
        return pltpu.CompilerParams(vmem_limit_bytes=VMEM_LIMIT)
    return pltpu.CompilerParams(dimension_semantics=sem, vmem_limit_bytes=VMEM_LIMIT)


def _gelu(x):
    t = jnp.tanh(GELU_C0 * (x + GELU_C1 * x * x * x))
    return 0.5 * x * (1.0 + t)


def _gelu_and_grad(x):
    x2 = x * x
    t = jnp.tanh(GELU_C0 * (x + GELU_C1 * x * x2))
    g = 0.5 * x * (1.0 + t)
    dg = 0.5 * (1.0 + t) + 0.5 * x * (1.0 - t * t) * (GELU_C0 * (1.0 + 3.0 * GELU_C1 * x2))
    return g, dg


def _dot(a, b):
    return jnp.dot(a, b, preferred_element_type=F32)


def _dot_nt(a, b):
    return lax.dot_general(a, b, (((1,), (1,)), ((), ())), preferred_element_type=F32)


def _dot_tn(a, b):
    return lax.dot_general(a, b, (((0,), (0,)), ((), ())), preferred_element_type=F32)


def _rms_bwd(d, xin, g):
    r = lax.rsqrt(jnp.mean(xin * xin, axis=-1, keepdims=True) + NORM_EPS)
    xh = xin * r
    gd = g * d
    dx = r * (gd - xh * jnp.mean(gd * xh, axis=-1, keepdims=True))
    return dx, d * xh


MESH = pl.DeviceIdType.MESH
ANY = pl.BlockSpec(memory_space=pl.ANY)
PEER_MASKS = tuple(range(1, N_DEV))


def _my_index():
    return lax.axis_index("x") * 4 + lax.axis_index("y") * 2 + lax.axis_index("c")


def _peer(mask):
    x, y, c = lax.axis_index("x"), lax.axis_index("y"), lax.axis_index("c")
    px = 1 - x if mask & 4 else x
    py = 1 - y if mask & 2 else y
    pc = 1 - c if mask & 1 else c
    return (px, py, pc), px * 4 + py * 2 + pc


RELAY_AT = 3
SIBLING = 1
CHIP_MASKS = (2, 4, 6)


class _Exchange:
    def __init__(self, nblocked, in_refs, out_refs, sems):
        send_sems, recv_sems, local_sems = sems
        me = _my_index()
        sibling, _ = _peer(SIBLING)
        self.local, self.first, self.relays, self.relayed_in, self.last_in = [], [], [], [], []
        for a, (in_ref, out_ref) in enumerate(zip(in_refs, out_refs)):
            def copy(src, slot, mask, to):
                return pltpu.make_async_remote_copy(
                    src_ref=src, dst_ref=out_ref.at[slot], send_sem=send_sems.at[a, mask - 1],
                    recv_sem=recv_sems.at[a, mask - 1], device_id=to, device_id_type=MESH)

            if a < nblocked:
                self.local.append(pltpu.make_async_copy(in_ref.at[me], out_ref.at[me], local_sems.at[a]))
                for mask in PEER_MASKS:
                    peer, pidx = _peer(mask)
                    self.first.append(copy(in_ref.at[pidx], me, mask, peer))
                    self.last_in.append(copy(in_ref.at[pidx], pidx, mask, peer))
                continue
            self.local.append(pltpu.make_async_copy(in_ref, out_ref.at[me], local_sems.at[a]))
            for mask in (SIBLING,) + CHIP_MASKS:
                peer, pidx = _peer(mask)
                self.first.append(copy(in_ref, me, mask, peer))
                (self.last_in if mask == SIBLING else self.relayed_in).append(copy(in_ref, pidx, mask, peer))
            for mask in CHIP_MASKS:
                _, origin = _peer(mask)
                _, far = _peer(mask | SIBLING)
                self.relays.append(copy(out_ref.at[origin], origin, mask | SIBLING, sibling))
                self.last_in.append(copy(in_ref, far, mask | SIBLING, sibling))

    def start(self):
        for cp in self.local + self.first[::-1]:
            cp.start()

    def relay(self):
        for arrived, onward in zip(self.relayed_in, self.relays):
            arrived.wait_recv()
            onward.start()

    def finish(self):
        for cp in self.first + self.relays:
            cp.wait_send()
        for cp in self.last_in:
            cp.wait_recv()
        for cp in self.local:
            cp.wait()


def _exchange_out_shape(blocked, whole):
    return [SDS(b.shape, b.dtype) for b in blocked] + [SDS((N_DEV,) + w.shape, w.dtype) for w in whole]


def _exchange_sems(n):
    return [pltpu.SemaphoreType.DMA((n, N_DEV - 1)), pltpu.SemaphoreType.DMA((n, N_DEV - 1)),
            pltpu.SemaphoreType.DMA((n,))]


def exchange(blocked, whole, name):
    nb, n = len(blocked), len(blocked) + len(whole)

    def body(*refs):
        ex = _Exchange(nb, refs[:n], refs[n:2 * n], refs[2 * n:])
        ex.start()
        ex.relay()
        ex.finish()

    return pl.pallas_call(
        body, name=name, in_specs=[ANY] * n, out_specs=[ANY] * n, out_shape=_exchange_out_shape(blocked, whole),
        scratch_shapes=_exchange_sems(n),
    )(*blocked, *whole)


def _call(body, *, name, grid, in_specs, out_specs, out_shape, args, scratch_shapes=(), sem=None, ride=None):
    out_shape, out_specs, scratch_shapes = list(out_shape), list(out_specs), list(scratch_shapes)
    if ride is None:
        outs = pl.pallas_call(body, name=name, grid=grid, in_specs=list(in_specs), out_specs=out_specs,
                              out_shape=out_shape, scratch_shapes=scratch_shapes,
                              compiler_params=_params(sem))(*args)
        return list(outs), []
    blocked, whole = ride
    cargs = list(blocked) + list(whole)
    nb, nc = len(blocked), len(cargs)
    n_in, n_out, n_scr = len(args), len(out_shape), len(scratch_shapes)
    steps = math.prod(grid)
    assert steps >= 3, grid

    def riding(*refs):
        ins, refs = refs[:n_in], refs[n_in:]
        cins, refs = refs[:nc], refs[nc:]
        outs, refs = refs[:n_out], refs[n_out:]
        couts, refs = refs[:nc], refs[nc:]
        scr, sems = refs[:n_scr], refs[n_scr:]
        step = functools.reduce(lambda acc, k: acc * grid[k] + pl.program_id(k), range(len(grid)), 0)

        @pl.when(step == 0)
        def _():
            _Exchange(nb, cins, couts, sems).start()

        @pl.when(step == RELAY_AT * steps // 4)
        def _():
            _Exchange(nb, cins, couts, sems).relay()

        body(*ins, *outs, *scr)

        @pl.when(step == steps - 1)
        def _():
            _Exchange(nb, cins, couts, sems).finish()

    res = pl.pallas_call(
        riding, name=name, grid=grid, in_specs=list(in_specs) + [ANY] * nc, out_specs=out_specs + [ANY] * nc,
        out_shape=out_shape + _exchange_out_shape(blocked, whole),
        scratch_shapes=scratch_shapes + _exchange_sems(nc),
        compiler_params=_params(("arbitrary",) * len(grid)))(*args, *cargs)
    return list(res[:n_out]), list(res[n_out:])


def norm_mm(x, g, ws, name, tm=512, tn=1408, ride=None, conv=None):
    n, d = x.shape
    f = ws[0].shape[0]
    nw = len(ws)
    extra_in, extra_spec, extra_out, extra_out_spec, scratch = [], [], [], [], []
    if conv is not None:
        assert nw == 2 and tn == f and SEQ % tm == 0 and tm % CONV_ROWS == 0
        extra_in = list(conv)
        extra_spec = [pl.BlockSpec((3, f), lambda i, j: (0, 0)), pl.BlockSpec((1, f), lambda i, j: (0, 0))]
        extra_out, extra_out_spec = [SDS((n, f), BF16)], [pl.BlockSpec((tm, f), lambda i, j: (i, 0))]
        scratch = [pltpu.VMEM((PAD + CONV_ROWS, f), F32), pltpu.VMEM((PAD, f), F32)]

    def body(x_ref, g_ref, *refs):
        w_refs, refs = refs[:nw], refs[nw:]
        conv_refs, refs = refs[:len(extra_in)], refs[len(extra_in):]
        h_ref, o_refs, refs = refs[0], refs[1:1 + nw], refs[1 + nw:]

        @pl.when(pl.program_id(1) == 0)
        def _():
            xv = x_ref[...]
            r = lax.rsqrt(jnp.mean(xv * xv, axis=-1, keepdims=True) + NORM_EPS)
            h_ref[...] = (xv * r * g_ref[...]).astype(BF16)

        h = h_ref[...]
        for w_ref, o_ref in zip(w_refs, o_refs):
            o_ref[...] = _dot_nt(h, w_ref[...])
        if conv is None:
            return
        (cw_ref, cb_ref), (gp_ref, up_ref), (gu_ref, head_ref, carry_ref) = conv_refs, o_refs, refs

        @pl.when(pl.program_id(0) % (SEQ // tm) == 0)
        def _():
            carry_ref[...] = jnp.zeros_like(carry_ref)

        head_ref[0:PAD, :] = carry_ref[...]
        head_ref[PAD:PAD + CONV_ROWS, :] = gp_ref[0:CONV_ROWS, :]
        for l0 in range(0, f, LANE):
            ls = slice(l0, l0 + LANE)
            w0, w1, w2, bias = cw_ref[0:1, ls], cw_ref[1:2, ls], cw_ref[2:3, ls], cb_ref[:, ls]
            for r0 in range(0, tm, CONV_ROWS):
                g0, g1, g2 = _conv_taps(gp_ref, head_ref, r0, ls)
                c = bias + w0 * g2 + w1 * g1 + w2 * g0
                gu_ref[r0:r0 + CONV_ROWS, ls] = (_gelu(c) * up_ref[r0:r0 + CONV_ROWS, ls]).astype(BF16)
        carry_ref[...] = gp_ref[tm - PAD:tm, :]

    return _call(
        body, name=name, grid=(n // tm, f // tn),
        in_specs=[pl.BlockSpec((tm, d), lambda i, j: (i, 0)), pl.BlockSpec((1, d), lambda i, j: (0, 0))]
        + [pl.BlockSpec((tn, d), lambda i, j: (j, 0)) for _ in ws] + extra_spec,
        out_specs=[pl.BlockSpec((tm, d), lambda i, j: (i, 0))]
        + [pl.BlockSpec((tm, tn), lambda i, j: (i, j)) for _ in ws] + extra_out_spec,
        out_shape=[SDS((n, d), BF16)] + [SDS((n, f), F32) for _ in ws] + extra_out,
        scratch_shapes=scratch,
        args=[x, g, *ws, *extra_in], sem=("parallel" if conv is None else "arbitrary", "arbitrary"), ride=ride)


def _lane_concat(refs):
    vals = [r[...].astype(BF16) for r in refs]
    return vals[0] if len(vals) == 1 else jnp.concatenate(vals, axis=1)


def mm_res_norm(a_list, w, res, g, name, tm=512):
    n = a_list[0].shape[0]
    k, d = w.shape
    na = len(a_list)

    def body(*refs):
        w_ref, res_ref, g_ref, y_ref, o_ref = refs[na:]
        y = _dot(_lane_concat(refs[:na]), w_ref[...])
        r = lax.rsqrt(jnp.mean(y * y, axis=-1, keepdims=True) + NORM_EPS)
        y_ref[...] = y
        o_ref[...] = res_ref[...] + y * r * g_ref[...]

    return pl.pallas_call(
        body, name=name, grid=(n // tm,),
        in_specs=[pl.BlockSpec((tm, a.shape[1]), lambda i: (i, 0)) for a in a_list]
        + [pl.BlockSpec((k, d), lambda i: (0, 0)),
           pl.BlockSpec((tm, d), lambda i: (i, 0)), pl.BlockSpec((1, d), lambda i: (0, 0))],
        out_specs=[pl.BlockSpec((tm, d), lambda i: (i, 0)), pl.BlockSpec((tm, d), lambda i: (i, 0))],
        out_shape=[SDS((n, d), F32), SDS((n, d), F32)],
        compiler_params=_params(("parallel",)),
    )(*a_list, w, res, g)


def down_loss(a, w, res, g, target, name, tm=256):
    n, k = a.shape
    d = w.shape[1]
    inv_d = 1.0 / d

    def body(a_ref, w_ref, res_ref, g_ref, t_ref, dy_ref, dout_ref, dg_ref, loss_ref):
        i = pl.program_id(0)
        y = _dot(a_ref[...], w_ref[...])
        gv = g_ref[...]
        r = lax.rsqrt(jnp.mean(y * y, axis=-1, keepdims=True) + NORM_EPS)
        yh = y * r
        e = res_ref[...] + yh * gv - t_ref[...]
        part = 0.5 * inv_d * jnp.sum(jnp.sum(e * e, axis=-1, keepdims=True), axis=0, keepdims=True)
        dout = e * inv_d
        dout_ref[...] = dout
        gd = gv * dout
        dy_ref[...] = (r * (gd - yh * jnp.mean(gd * yh, axis=-1, keepdims=True))).astype(BF16)
        dgp = jnp.sum(dout * yh, axis=0, keepdims=True)
        lane0 = lax.broadcasted_iota(jnp.int32, (1, LANE), 1) == 0
        lp = jnp.where(lane0, part, 0.0)

        @pl.when(i == 0)
        def _():
            dg_ref[...] = dgp
            loss_ref[...] = lp

        @pl.when(i > 0)
        def _():
            dg_ref[...] += dgp
            loss_ref[...] += lp

    return pl.pallas_call(
        body, name=name, grid=(n // tm,),
        in_specs=[pl.BlockSpec((tm, k), lambda i: (i, 0)), pl.BlockSpec((k, d), lambda i: (0, 0)),
                  pl.BlockSpec((tm, d), lambda i: (i, 0)), pl.BlockSpec((1, d), lambda i: (0, 0)),
                  pl.BlockSpec((tm, d), lambda i: (i, 0))],
        out_specs=[pl.BlockSpec((tm, d), lambda i: (i, 0)), pl.BlockSpec((tm, d), lambda i: (i, 0)),
                   pl.BlockSpec((1, d), lambda i: (0, 0)), pl.BlockSpec((1, LANE), lambda i: (0, 0))],
        out_shape=[SDS((n, d), BF16), SDS((n, d), F32), SDS((1, d), F32), SDS((1, LANE), F32)],
        compiler_params=_params(("arbitrary",)),
    )(a, w, res, g, target)


def _accumulate(ref, val, step):
    @pl.when(step == 0)
    def _():
        ref[...] = val

    @pl.when(step > 0)
    def _():
        ref[...] += val


def mm_nt(terms, ws, name, tm=512, out_dtype=F32, ride=None, epilogue=None, by_rows=False):
    n = terms[0][0].shape[0]
    r = ws[0].shape[1 if by_rows else 0]
    na = len(terms)
    meta = [(widx, off, a.shape[1]) for a, widx, off in terms]
    fn, extras, out_shape = epilogue if epilogue else (None, [], [SDS((n, r), out_dtype)])
    n_fixed = na + len(ws)

    def body(*refs):
        a_refs = refs[:na]
        w_refs = refs[na:n_fixed]
        acc = None
        for a_ref, (widx, off, k) in zip(a_refs, meta):
            a = a_ref[...].astype(BF16)
            p = _dot(a, w_refs[widx][off:off + k, :]) if by_rows else _dot_nt(a, w_refs[widx][:, off:off + k])
            acc = p if acc is None else acc + p
        if fn is None:
            refs[-1][...] = acc.astype(out_dtype)
        else:
            fn(acc, pl.program_id(0), *refs[n_fixed:])

    def spec(a):
        if a.shape[0] == 1:
            return pl.BlockSpec(a.shape, lambda i: (0, 0))
        return pl.BlockSpec((tm, a.shape[1]), lambda i: (i, 0))

    return _call(
        body, name=name, grid=(n // tm,),
        in_specs=[spec(a) for a, _, _ in terms] + [pl.BlockSpec(w.shape, lambda i: (0, 0)) for w in ws]
        + [spec(e) for e in extras],
        out_specs=[spec(o) for o in out_shape], out_shape=out_shape,
        args=[a for a, _, _ in terms] + list(ws) + list(extras),
        sem=("parallel",) if fn is None else ("arbitrary",), ride=ride)


def _piece_blocks(pieces, tile):
    out, first = [], 0
    for p in pieces:
        nblk, rem = divmod(p.shape[1], tile)
        assert rem == 0, (p.shape, tile)
        out.append((first, nblk))
        first += nblk
    return out, first


def mm_tn(lhs_list, rhs_list, name, t1, t2, out_dtype=BF16):
    n = lhs_list[0].shape[0]
    lblocks, nbl = _piece_blocks(lhs_list, t1)
    rblocks, nbr = _piece_blocks(rhs_list, t2)
    nl = len(lhs_list)

    def body(*refs):
        l_refs, r_refs, o_ref = refs[:nl], refs[nl:-1], refs[-1]
        i, j = pl.program_id(0), pl.program_id(1)
        for l_ref, (ls, ln) in zip(l_refs, lblocks):
            for r_ref, (rs, rn) in zip(r_refs, rblocks):
                @pl.when((i >= ls) & (i < ls + ln) & (j >= rs) & (j < rs + rn))
                def _(l_ref=l_ref, r_ref=r_ref):
                    o_ref[...] = _dot_tn(l_ref[...].astype(BF16), r_ref[...].astype(BF16)).astype(out_dtype)

    def piece_spec(tile, axis, first, nblk):
        def index(i, j):
            return 0, jnp.clip((i, j)[axis] - first, 0, nblk - 1)
        return pl.BlockSpec((n, tile), index)

    return pl.pallas_call(
        body, name=name, grid=(nbl, nbr),
        in_specs=[piece_spec(t1, 0, *b) for b in lblocks] + [piece_spec(t2, 1, *b) for b in rblocks],
        out_specs=pl.BlockSpec((t1, t2), lambda i, j: (i, j)),
        out_shape=SDS((nbl * t1, nbr * t2), out_dtype),
        compiler_params=_params(("parallel", "arbitrary")),
    )(*lhs_list, *rhs_list)


GATE_ROWS = 512


def _tril_mask():
    row = lax.broadcasted_iota(jnp.int32, (CHUNK, CHUNK), 0)
    col = lax.broadcasted_iota(jnp.int32, (CHUNK, CHUNK), 1)
    return row >= col


def _group_of(shape, axis):
    return lax.broadcasted_iota(jnp.int32, shape, axis) // HEAD_DIM


def _group_mean_matrix():
    same = _group_of((A_WIDTH, A_WIDTH), 0) == _group_of((A_WIDTH, A_WIDTH), 1)
    return jnp.where(same, 1.0 / HEAD_DIM, 0.0).astype(BF16)


def _dot_sum(a, b):
    hi = a.astype(BF16)
    lo = (a - hi.astype(F32)).astype(BF16)
    return _dot(hi, b) + _dot(lo, b)


def _by_group(parts, lane_group):
    out = parts[A_GROUPS - 1]
    for g in range(A_GROUPS - 2, -1, -1):
        out = jnp.where(lane_group == g, parts[g], out)
    return out


def _group_norm(gv, gmean):
    xc = gv - _dot_sum(gv, gmean)
    rstd = lax.rsqrt(_dot_sum(xc * xc, gmean) + NORM_EPS)
    return xc * rstd, rstd


def gating_fwd(proj, lng, lnb, ws, sbt, name):
    n = proj.shape[0]

    def body(u_ref, v_ref, lng_ref, lnb_ref, ws_ref, sbt_ref, a_ref):
        tril = _tril_mask()
        lane_group = _group_of((CHUNK, A_WIDTH), 1)
        gmean = _group_mean_matrix()
        wts = [jnp.where(tril, ws_ref[g], 0.0).astype(BF16) for g in range(A_GROUPS)]
        sb = _by_group([sbt_ref[:, g:g + 1] for g in range(A_GROUPS)], lane_group)

        def chunk(c, carry):
            rows = pl.ds(pl.multiple_of(c * CHUNK, CHUNK), CHUNK)
            vhat, _ = _group_norm(_gelu(v_ref[rows, :]), gmean)
            vn = (vhat * lng_ref[...] + lnb_ref[...]).astype(BF16)
            z = _by_group([_dot(wt, vn) for wt in wts], lane_group) + sb
            a_ref[rows, :] = _gelu(u_ref[rows, :]) * z
            return carry

        lax.fori_loop(0, GATE_ROWS // CHUNK, chunk, 0)

    return pl.pallas_call(
        body, name=name, grid=(n // GATE_ROWS,),
        in_specs=[pl.BlockSpec((GATE_ROWS, A_WIDTH), lambda i: (i, 0)),
                  pl.BlockSpec((GATE_ROWS, A_WIDTH), lambda i: (i, 1)),
                  pl.BlockSpec((1, A_WIDTH), lambda i: (0, 0)), pl.BlockSpec((1, A_WIDTH), lambda i: (0, 0)),
                  pl.BlockSpec((A_GROUPS, CHUNK, CHUNK), lambda i: (0, 0, 0)),
                  pl.BlockSpec((CHUNK, A_GROUPS), lambda i: (0, 0))],
        out_specs=pl.BlockSpec((GATE_ROWS, A_WIDTH), lambda i: (i, 0)),
        out_shape=SDS((n, A_WIDTH), F32),
        compiler_params=_params(("parallel",)),
    )(proj, proj, lng, lnb, ws, sbt)


def gating_bwd(proj, dmix, lng, lnb, ws, sbt, name):
    n = proj.shape[0]

    def body(u_ref, v_ref, da_ref, lng_ref, lnb_ref, ws_ref, sbt_ref,
             duv_ref, dws_ref, dsbt_ref, dlng_ref, dlnb_ref):
        @pl.when(pl.program_id(0) == 0)
        def _():
            dws_ref[...] = jnp.zeros_like(dws_ref)
            dsbt_ref[...] = jnp.zeros_like(dsbt_ref)
            dlng_ref[...] = jnp.zeros_like(dlng_ref)
            dlnb_ref[...] = jnp.zeros_like(dlnb_ref)

        tril = _tril_mask()
        lane_group = _group_of((CHUNK, A_WIDTH), 1)
        gmean = _group_mean_matrix()
        gsum = (_group_of((A_WIDTH, LANE), 0) == lax.broadcasted_iota(jnp.int32, (A_WIDTH, LANE), 1)).astype(BF16)
        wts = [jnp.where(tril, ws_ref[g], 0.0) for g in range(A_GROUPS)]
        wts_b = [w.astype(BF16) for w in wts]
        wts_t = [w.T.astype(BF16) for w in wts]
        sb = _by_group([sbt_ref[:, g:g + 1] for g in range(A_GROUPS)], lane_group)
        lg = lng_ref[...]

        def chunk(c, carry):
            rows = pl.ds(pl.multiple_of(c * CHUNK, CHUNK), CHUNK)
            gu, dgu_dx = _gelu_and_grad(u_ref[rows, :])
            gv, dgv_dx = _gelu_and_grad(v_ref[rows, :])
            vhat, rstd = _group_norm(gv, gmean)
            vn = (vhat * lg + lnb_ref[...]).astype(BF16)
            z = _by_group([_dot(wt, vn) for wt in wts_b], lane_group) + sb
            da = da_ref[rows, :]
            dz = da * gu
            dzb = dz.astype(BF16)
            duv_ref[rows, 0:A_WIDTH] = (da * z * dgu_dx).astype(BF16)
            dsbt_ref[...] += _dot_sum(dz, gsum)[:, 0:A_GROUPS]
            for g in range(A_GROUPS):
                dz_g = jnp.where(lane_group == g, dzb, jnp.zeros_like(dzb))
                dws_ref[g] += jnp.where(tril, _dot_nt(dz_g, vn), 0.0)
            dvn = _by_group([_dot(wt, dzb) for wt in wts_t], lane_group)
            dlng_ref[...] += jnp.sum(dvn * vhat, axis=0, keepdims=True)
            dlnb_ref[...] += jnp.sum(dvn, axis=0, keepdims=True)
            dvh = dvn * lg
            dgv = rstd * (dvh - _dot_sum(dvh, gmean) - vhat * _dot_sum(dvh * vhat, gmean))
            duv_ref[rows, A_WIDTH:2 * A_WIDTH] = (dgv * dgv_dx).astype(BF16)
            return carry

        lax.fori_loop(0, GATE_ROWS // CHUNK, chunk, 0)

    return pl.pallas_call(
        body, name=name, grid=(n // GATE_ROWS,),
        in_specs=[pl.BlockSpec((GATE_ROWS, A_WIDTH), lambda i: (i, 0)),
                  pl.BlockSpec((GATE_ROWS, A_WIDTH), lambda i: (i, 1)),
                  pl.BlockSpec((GATE_ROWS, A_WIDTH), lambda i: (i, 0)),
                  pl.BlockSpec((1, A_WIDTH), lambda i: (0, 0)), pl.BlockSpec((1, A_WIDTH), lambda i: (0, 0)),
                  pl.BlockSpec((A_GROUPS, CHUNK, CHUNK), lambda i: (0, 0, 0)),
                  pl.BlockSpec((CHUNK, A_GROUPS), lambda i: (0, 0))],
        out_specs=[pl.BlockSpec((GATE_ROWS, 2 * A_WIDTH), lambda i: (i, 0)),
                   pl.BlockSpec((A_GROUPS, CHUNK, CHUNK), lambda i: (0, 0, 0)),
                   pl.BlockSpec((CHUNK, A_GROUPS), lambda i: (0, 0)),
                   pl.BlockSpec((1, A_WIDTH), lambda i: (0, 0)), pl.BlockSpec((1, A_WIDTH), lambda i: (0, 0))],
        out_shape=[SDS((n, 2 * A_WIDTH), BF16), SDS((A_GROUPS, CHUNK, CHUNK), F32), SDS((CHUNK, A_GROUPS), F32),
                   SDS((1, A_WIDTH), F32), SDS((1, A_WIDTH), F32)],
        compiler_params=_params(("arbitrary",)),
    )(proj, proj, dmix, lng, lnb, ws, sbt)


def _t5_bucket_np(dist):
    max_exact = NUM_BUCKETS // 2
    dd = np.maximum(dist, 1).astype(np.float64)
    large = max_exact + np.log(dd / max_exact) / math.log(MAX_DISTANCE / max_exact) * (NUM_BUCKETS - max_exact)
    large = np.minimum(large.astype(np.int64), NUM_BUCKETS - 1)
    return np.where(dist < max_exact, dist, large)


def _bucket_tables(with_first):
    i = np.arange(ATTN_BLOCK)[:, None]
    j = np.arange(2 * ATTN_BLOCK)[None, :]
    rel = ATTN_BLOCK + i - j
    band = (rel >= 0) & (rel <= ATTN_BLOCK)
    tabs = []
    for own_only in (False, True) if with_first else (False,):
        for dil in DILATIONS:
            b = _t5_bucket_np(np.maximum(rel, 0) * dil)
            tabs.append(np.where(band & (j >= ATTN_BLOCK) if own_only else band, b, -1).reshape(1, -1))
    return np.stack(tabs).astype(np.float32)


BIAS_SIZE = ATTN_BLOCK * 2 * ATTN_BLOCK


def bias_tables(rel_bias_t, name, ride=None):
    idx = jnp.asarray(_bucket_tables(True))
    ntab = idx.shape[0]

    def body(rb_ref, idx_ref, o_ref):
        iv = idx_ref[0]
        bk = lax.broadcasted_iota(jnp.int32, (NUM_BUCKETS, BIAS_SIZE), 0).astype(F32)
        onehot = (bk == iv).astype(F32)
        t = jnp.dot(rb_ref[...], onehot, preferred_element_type=F32, precision=lax.Precision.HIGHEST)
        o_ref[0] = jnp.where(iv < 0.0, NEG_INF, t)

    return _call(
        body, name=name, grid=(ntab,),
        in_specs=[pl.BlockSpec((B_HEADS, NUM_BUCKETS), lambda d: (0, 0)),
                  pl.BlockSpec((1, 1, BIAS_SIZE), lambda d: (d, 0, 0))],
        out_specs=[pl.BlockSpec((1, B_HEADS, BIAS_SIZE), lambda d: (d, 0, 0))],
        out_shape=[SDS((ntab, B_HEADS, BIAS_SIZE), F32)],
        args=[rel_bias_t, idx], sem=("parallel",), ride=ride)


def rel_bias_grad(dbias, name):
    idx = jnp.asarray(_bucket_tables(False))

    def body(db_ref, idx_ref, o_ref):
        d = pl.program_id(0)
        iv = idx_ref[0]
        bk = lax.broadcasted_iota(jnp.int32, (NUM_BUCKETS, BIAS_SIZE), 0).astype(F32)
        onehot = (bk == iv).astype(F32)
        part = lax.dot_general(db_ref[0], onehot, (((1,), (1,)), ((), ())),
                               preferred_element_type=F32, precision=lax.Precision.HIGHEST)

        @pl.when(d == 0)
        def _():
            o_ref[...] = part

        @pl.when(d > 0)
        def _():
            o_ref[...] += part

    return pl.pallas_call(
        body, name=name, grid=(len(DILATIONS),),
        in_specs=[pl.BlockSpec((1, B_HEADS, BIAS_SIZE), lambda d: (d, 0, 0)),
                  pl.BlockSpec((1, 1, BIAS_SIZE), lambda d: (d, 0, 0))],
        out_specs=pl.BlockSpec((B_HEADS, NUM_BUCKETS), lambda d: (0, 0)),
        out_shape=SDS((B_HEADS, NUM_BUCKETS), F32),
        compiler_params=_params(("arbitrary",)),
    )(dbias, idx)


QK_SCALE = 1.0 / math.sqrt(HEAD_DIM)


def _attn_scores(q_scaled, kk, bias):
    return _dot_nt(q_scaled, kk) + bias


def _head0_lanes():
    return lax.broadcasted_iota(jnp.int32, (ATTN_BLOCK, LANE), 1) < HEAD_DIM


def _one_head(x2, head0, hh):
    return jnp.where(head0 if hh == 0 else jnp.logical_not(head0), x2, 0.0).astype(BF16)


def _rows(start, size, dil):
    return pl.ds(start, size) if dil == 1 else pl.ds(start, size, stride=dil)


QUAD = 4
QUAD_ROWS = SEQ // QUAD


def _deinterleave(src_ref, dst_ref):
    for r in range(QUAD):
        for c in range(QUAD_ROWS // ATTN_BLOCK):
            dst_ref[r, c * ATTN_BLOCK:(c + 1) * ATTN_BLOCK, :] = src_ref[
                pl.ds(r + c * QUAD * ATTN_BLOCK, ATTN_BLOCK, stride=QUAD), :]


def _deinterleave_again(src_ref, dst_ref):
    for r in range(QUAD):
        for s in range(QUAD):
            dst_ref[r + QUAD * s] = src_ref[r, pl.ds(s, ATTN_BLOCK, stride=QUAD), :]


def _interleave_back(src_ref, dst_ref, slot0, accumulate=False):
    for r in range(QUAD):
        for s in range(QUAD):
            rows = pl.ds(s, ATTN_BLOCK, stride=QUAD)
            if accumulate:
                dst_ref[slot0 + r, rows, :] += src_ref[r + QUAD * s]
            else:
                dst_ref[slot0 + r, rows, :] = src_ref[r + QUAD * s]


def _quad_tiles():
    return [(r, pl.ds(r + c * QUAD * ATTN_BLOCK, ATTN_BLOCK, stride=QUAD), slice(c * ATTN_BLOCK, (c + 1) * ATTN_BLOCK))
            for r in range(QUAD) for c in range(QUAD_ROWS // ATTN_BLOCK)]


def _attn_schedule(op):
    def d16(i, carry):
        for t in range(2 * QUAD):
            op(2, 2 * QUAD * i + t, 0, 1, True)
        return carry

    lax.fori_loop(0, QUAD // 2, d16, 0)

    def d4(i, carry):
        for u in range(2):
            for nq in range(QUAD_ROWS // ATTN_BLOCK):
                op(1, 2 * i + u, nq * ATTN_BLOCK, 1, nq == 0)
        return carry

    lax.fori_loop(0, QUAD // 2, d4, 0)
    op(0, None, 0, 1, True)
    per_pass = 5

    def d1(j, carry):
        for t in range(per_pass):
            op(0, None, pl.multiple_of((1 + per_pass * j + t) * ATTN_BLOCK, ATTN_BLOCK), 1, False)
        return carry

    lax.fori_loop(0, (SEQ // ATTN_BLOCK - 1) // per_pass, d1, 0)


def _keys(src, krows, first):
    kb = src[krows, :].astype(BF16)
    return jnp.concatenate([kb, kb], axis=0) if first else kb


def _table(seg, first):
    return len(DILATIONS) + seg if first else seg


def _kv_rows(start, dil, first):
    if first:
        return _rows(start, ATTN_BLOCK, dil)
    return _rows(start - ATTN_BLOCK * dil, 2 * ATTN_BLOCK, dil)


MERGE_ROWS = 256


def attn_fwd(proj, bias, nb_local, name, ride=None):
    n = proj.shape[0]
    nseg = len(DILATIONS)

    def body(q_ref, k_ref, v_ref, b_ref, o_ref, lse_ref, q4_ref, k4_ref, v4_ref, os0_ref, ls0_ref, os4_ref, ls4_ref,
             q16_ref, k16_ref, v16_ref, os16_ref, ls16_ref):
        for src, mid, dst in ((q_ref, q4_ref, q16_ref), (k_ref, k4_ref, k16_ref), (v_ref, v4_ref, v16_ref)):
            _deinterleave(src, mid)
            _deinterleave_again(mid, dst)

        def op(seg, r, start, stride, first):
            qrows = _rows(start, ATTN_BLOCK, stride)
            krows = _kv_rows(start, stride, first)
            if seg == 0:
                q_src, k_src, v_src, o_dst, l_dst = q_ref, k_ref, v_ref, os0_ref, ls0_ref
            elif seg == 1:
                q_src, k_src, v_src = q4_ref.at[r], k4_ref.at[r], v4_ref.at[r]
                o_dst, l_dst = os4_ref.at[r], ls4_ref.at[r]
            else:
                q_src, k_src, v_src = q16_ref.at[r], k16_ref.at[r], v16_ref.at[r]
                o_dst, l_dst = os16_ref.at[r], ls16_ref.at[r]
            q2, kb, vb = q_src[qrows, :] * QK_SCALE, _keys(k_src, krows, first), _keys(v_src, krows, first)
            head0 = _head0_lanes()
            outs, lses = [], []
            for hh in range(2):
                s = _attn_scores(_one_head(q2, head0, hh), kb, b_ref[_table(seg, first), hh])
                m = jnp.max(s, axis=-1, keepdims=True)
                p = jnp.exp(s - m)
                l = jnp.sum(p, axis=-1, keepdims=True)
                outs.append(_dot(p.astype(BF16), vb) / l)
                lses.append(jnp.broadcast_to(m + jnp.log(l), (ATTN_BLOCK, LANE)))
            o_dst[qrows, :] = jnp.where(head0, outs[0], outs[1])
            l_dst[qrows, :] = jnp.where(head0, lses[0], lses[1])

        _attn_schedule(op)
        _interleave_back(os16_ref, os4_ref, QUAD)
        _interleave_back(ls16_ref, ls4_ref, QUAD)

        for r, nat, quad in _quad_tiles():
            ls = [ls0_ref[nat, :], ls4_ref[r, quad, :], ls4_ref[QUAD + r, quad, :]]
            m = functools.reduce(jnp.maximum, ls)
            ws = [jnp.exp(l - m) for l in ls]
            den = ws[0] + ws[1] + ws[2]
            num = ws[0] * os0_ref[nat, :] + ws[1] * os4_ref[r, quad, :] + ws[2] * os4_ref[QUAD + r, quad, :]
            o_ref[nat, :] = num / den
            lse_ref[nat, :] = m + jnp.log(den)

    def in_spec(off):
        return pl.BlockSpec((SEQ, LANE), lambda b, p: (b, off // LANE + p))

    out_spec = pl.BlockSpec((SEQ, LANE), lambda b, p: (b, p))
    return _call(
        body, name=name, grid=(nb_local, HEAD_PAIRS),
        in_specs=[in_spec(Q_OFF), in_spec(K_OFF), in_spec(V_OFF),
                  pl.BlockSpec((2 * nseg, 2, ATTN_BLOCK, 2 * ATTN_BLOCK), lambda b, p: (0, p, 0, 0))],
        out_specs=[out_spec, out_spec],
        out_shape=[SDS((n, B_WIDTH), F32), SDS((n, B_WIDTH), F32)],
        scratch_shapes=[pltpu.VMEM((QUAD, QUAD_ROWS, LANE), F32)] * 3 + [pltpu.VMEM((SEQ, LANE), F32)] * 2
        + [pltpu.VMEM((2 * QUAD, QUAD_ROWS, LANE), F32)] * 2 + [pltpu.VMEM((QUAD * QUAD, ATTN_BLOCK, LANE), F32)] * 5,
        args=[proj, proj, proj, bias], sem=("parallel", "arbitrary"), ride=ride)


def attn_bwd(proj, b_out, dmix, lse_tot, bias, nb_local, name, ride=None):
    n = proj.shape[0]
    nseg = len(DILATIONS)
    a_blocks = A_WIDTH // LANE

    def body(q_ref, k_ref, v_ref, o_ref, do_ref, lse_ref, b_ref, dq_ref, dk_ref, dv_ref, db_ref,
             dqs_ref, delta_ref, dka_ref, dva_ref, q4_ref, k4_ref, v4_ref, do4_ref, lse4_ref, delta4_ref,
             dqs4_ref, dk4_ref, dv4_ref, q16_ref, k16_ref, v16_ref, do16_ref, lse16_ref, delta16_ref,
             dqs16_ref, dk16_ref, dv16_ref):
        @pl.when(pl.program_id(1) == 0)
        def _():
            db_ref[...] = jnp.zeros_like(db_ref)

        for acc_ref in (dka_ref, dva_ref, dk4_ref, dv4_ref):
            acc_ref[...] = jnp.zeros_like(acc_ref)
        quads = (q4_ref, k4_ref, v4_ref, do4_ref, lse4_ref, delta4_ref)
        hexes = (q16_ref, k16_ref, v16_ref, do16_ref, lse16_ref, delta16_ref)

        head_sum = (lax.broadcasted_iota(jnp.int32, (LANE, LANE), 0) // HEAD_DIM
                    == lax.broadcasted_iota(jnp.int32, (LANE, LANE), 1) // HEAD_DIM).astype(BF16)

        def row_dots(i, carry):
            rows = pl.ds(pl.multiple_of(i * MERGE_ROWS, MERGE_ROWS), MERGE_ROWS)
            delta_ref[rows, :] = _dot_sum(do_ref[rows, :] * o_ref[rows, :], head_sum)
            return carry

        lax.fori_loop(0, SEQ // MERGE_ROWS, row_dots, 0)
        for src, mid, dst in zip((q_ref, k_ref, v_ref, do_ref, lse_ref, delta_ref), quads, hexes):
            _deinterleave(src, mid)
            _deinterleave_again(mid, dst)

        def op(seg, r, start, stride, first):
            qrows = _rows(start, ATTN_BLOCK, stride)
            krows = _kv_rows(start, stride, first)
            if seg == 0:
                srcs = (q_ref, k_ref, v_ref, do_ref, lse_ref, delta_ref)
                dq_dst, dk_dst, dv_dst = dqs_ref, dka_ref, dva_ref
            elif seg == 1:
                srcs = tuple(x.at[r] for x in quads)
                dq_dst, dk_dst, dv_dst = dqs4_ref.at[r], dk4_ref.at[r], dv4_ref.at[r]
            else:
                srcs = tuple(x.at[r] for x in hexes)
                dq_dst, dk_dst, dv_dst = dqs16_ref.at[r], dk16_ref.at[r], dv16_ref.at[r]
            q_src, k_src, v_src, do_src, lse_src, delta_src = srcs
            q2, kb, vb = q_src[qrows, :] * QK_SCALE, _keys(k_src, krows, first), _keys(v_src, krows, first)
            do2, lse2, delta2 = do_src[qrows, :], lse_src[qrows, :], delta_src[qrows, :]
            head0 = _head0_lanes()
            dqs, dk, dv = [], None, None
            for hh in range(2):
                col = slice(hh * HEAD_DIM, hh * HEAD_DIM + 1)
                q, dob = _one_head(q2, head0, hh), _one_head(do2, head0, hh)
                p = jnp.exp(_attn_scores(q, kb, b_ref[_table(seg, first), hh]) - lse2[:, col])
                dvh = _dot_tn(p.astype(BF16), dob)
                ds = p * (_dot_nt(dob, vb) - delta2[:, col])
                if first:
                    db_ref[seg, hh, :, ATTN_BLOCK:] += ds[:, ATTN_BLOCK:]
                else:
                    db_ref[seg, hh] += ds
                dsb = ds.astype(BF16)
                dqs.append(_dot(dsb, kb))
                dkh = _dot_tn(dsb, q)
                dk = dkh if dk is None else dk + dkh
                dv = dvh if dv is None else dv + dvh
            if first:
                dk, dv = dk[ATTN_BLOCK:], dv[ATTN_BLOCK:]
            dq_dst[qrows, :] = jnp.where(head0, dqs[0], dqs[1]) * QK_SCALE
            if seg == 2:
                dk_dst[krows, :] = dk
                dv_dst[krows, :] = dv
            else:
                dk_dst[krows, :] += dk
                dv_dst[krows, :] += dv

        _attn_schedule(op)
        _interleave_back(dqs16_ref, dqs4_ref, QUAD)
        _interleave_back(dk16_ref, dk4_ref, 0, accumulate=True)
        _interleave_back(dv16_ref, dv4_ref, 0, accumulate=True)

        for r, nat, quad in _quad_tiles():
            dqs_ref[nat, :] += dqs4_ref[r, quad, :] + dqs4_ref[QUAD + r, quad, :]
            dka_ref[nat, :] += dk4_ref[r, quad, :]
            dva_ref[nat, :] += dv4_ref[r, quad, :]

        def merge(i, carry):
            rows = pl.ds(pl.multiple_of(i * MERGE_ROWS, MERGE_ROWS), MERGE_ROWS)
            dq_ref[rows, :] = dqs_ref[rows, :].astype(BF16)
            dk_ref[rows, :] = dka_ref[rows, :].astype(BF16)
            dv_ref[rows, :] = dva_ref[rows, :].astype(BF16)
            return carry

        lax.fori_loop(0, SEQ // MERGE_ROWS, merge, 0)

    def pspec(off):
        return pl.BlockSpec((SEQ, LANE), lambda p, b: (b, off // LANE + p))

    ospec = pl.BlockSpec((SEQ, LANE), lambda p, b: (b, p))
    bspec = pl.BlockSpec((nseg, 2, ATTN_BLOCK, 2 * ATTN_BLOCK), lambda p, b: (0, p, 0, 0))
    gshape = SDS((n, B_WIDTH), BF16)
    return _call(
        body, name=name, grid=(HEAD_PAIRS, nb_local),
        in_specs=[pspec(Q_OFF), pspec(K_OFF), pspec(V_OFF), ospec,
                  pl.BlockSpec((SEQ, LANE), lambda p, b: (b, a_blocks + p)), ospec,
                  pl.BlockSpec((2 * nseg, 2, ATTN_BLOCK, 2 * ATTN_BLOCK), lambda p, b: (0, p, 0, 0))],
        out_specs=[ospec, ospec, ospec, bspec],
        out_shape=[gshape, gshape, gshape, SDS((nseg, B_HEADS, ATTN_BLOCK, 2 * ATTN_BLOCK), F32)],
        scratch_shapes=[pltpu.VMEM((SEQ, LANE), F32)] * 4 + [pltpu.VMEM((QUAD, QUAD_ROWS, LANE), F32)] * 6
        + [pltpu.VMEM((2 * QUAD, QUAD_ROWS, LANE), F32)] + [pltpu.VMEM((QUAD, QUAD_ROWS, LANE), F32)] * 2
        + [pltpu.VMEM((QUAD * QUAD, ATTN_BLOCK, LANE), F32)] * 9,
        args=[proj, proj, proj, b_out, dmix, lse_tot, bias], sem=("arbitrary", "arbitrary"), ride=ride)


PAD = 8
CONV_ROWS = 64


CONV_LANES = 128


def _conv_taps(gp_ref, head_ref, r0, ls):
    g0 = gp_ref[r0:r0 + CONV_ROWS, ls]
    if r0 == 0:
        return g0, head_ref[PAD - 1:PAD - 1 + CONV_ROWS, ls], head_ref[PAD - 2:PAD - 2 + CONV_ROWS, ls]
    return g0, gp_ref[r0 - 1:r0 - 1 + CONV_ROWS, ls], gp_ref[r0 - 2:r0 - 2 + CONV_ROWS, ls]


def _fill_head(gp_ref, head_ref):
    head_ref[0:PAD, :] = jnp.zeros((PAD, CONV_LANES), F32)
    head_ref[PAD:PAD + CONV_ROWS, :] = gp_ref[0:CONV_ROWS, :]


def _lane_passes():
    return [slice(l0, l0 + LANE) for l0 in range(0, CONV_LANES, LANE)]


def conv_gelu_bwd(dgu, gp, up, cw, cb, nb_local, name, ride=None):
    n, f = gp.shape

    def fold(v):
        return jnp.sum(v.reshape(CONV_ROWS // 8, 8, LANE), axis=0)

    def body(dgu_ref, gp_ref, up_ref, cw_ref, cb_ref, dgp_ref, dup_ref, dcw_ref, dcb_ref, head_ref, dc_ref):
        b = pl.program_id(1)
        _fill_head(gp_ref, head_ref)
        dc_ref[SEQ:SEQ + PAD, :] = jnp.zeros((PAD, CONV_LANES), F32)
        for ls in _lane_passes():
            w0, w1, w2, bias = cw_ref[0:1, ls], cw_ref[1:2, ls], cw_ref[2:3, ls], cb_ref[:, ls]
            sums = [jnp.zeros((8, LANE), F32) for _ in range(4)]
            for r0 in range(0, SEQ, CONV_ROWS):
                rows = slice(r0, r0 + CONV_ROWS)
                g0, g1, g2 = _conv_taps(gp_ref, head_ref, r0, ls)
                gg, dgg = _gelu_and_grad(bias + w0 * g2 + w1 * g1 + w2 * g0)
                dgu = dgu_ref[rows, ls].astype(F32)
                dup_ref[rows, ls] = (dgu * gg).astype(BF16)
                dc = dgu * up_ref[rows, ls] * dgg
                dc_ref[rows, ls] = dc
                sums = [sums[0] + fold(dc * g2), sums[1] + fold(dc * g1), sums[2] + fold(dc * g0), sums[3] + fold(dc)]
            for r0 in range(0, SEQ, CONV_ROWS):
                dgp_ref[r0:r0 + CONV_ROWS, ls] = (
                    w2 * dc_ref[r0:r0 + CONV_ROWS, ls] + w1 * dc_ref[r0 + 1:r0 + 1 + CONV_ROWS, ls]
                    + w0 * dc_ref[r0 + 2:r0 + 2 + CONV_ROWS, ls]).astype(BF16)
            dcw = jnp.concatenate([jnp.sum(s, axis=0, keepdims=True) for s in sums[:3]], axis=0)
            dcb = jnp.sum(sums[3], axis=0, keepdims=True)

            @pl.when(b == 0)
            def _(dcw=dcw, dcb=dcb, ls=ls):
                dcw_ref[:, ls] = dcw
                dcb_ref[:, ls] = dcb

            @pl.when(b > 0)
            def _(dcw=dcw, dcb=dcb, ls=ls):
                dcw_ref[:, ls] += dcw
                dcb_ref[:, ls] += dcb

    blk = pl.BlockSpec((SEQ, CONV_LANES), lambda j, b: (b, j))
    wspec = pl.BlockSpec((3, CONV_LANES), lambda j, b: (0, j))
    bspec = pl.BlockSpec((1, CONV_LANES), lambda j, b: (0, j))
    return _call(
        body, name=name, grid=(f // CONV_LANES, nb_local),
        in_specs=[blk, blk, blk, wspec, bspec], out_specs=[blk, blk, wspec, bspec],
        out_shape=[SDS((n, f), BF16), SDS((n, f), BF16), SDS((3, f), F32), SDS((1, f), F32)],
        scratch_shapes=[pltpu.VMEM((PAD + CONV_ROWS, CONV_LANES), F32), pltpu.VMEM((SEQ + PAD, CONV_LANES), F32)],
        args=[dgu, gp, up, cw, cb], sem=("parallel", "arbitrary"), ride=ride)


def norm_mid_epilogue(x1, dout, z2, g3, g2):
    n, d = x1.shape

    def fn(dh2, step, x1_ref, dout_ref, z2_ref, g3_ref, g2_ref, dx1_ref, dz2_ref, dg3_ref, dg2_ref):
        dxa, dg3r = _rms_bwd(dh2, x1_ref[...], g3_ref[...])
        dx1 = dout_ref[...] + dxa
        dx1_ref[...] = dx1
        dz2, dg2r = _rms_bwd(dx1, z2_ref[...], g2_ref[...])
        dz2_ref[...] = dz2.astype(BF16)
        _accumulate(dg3_ref, jnp.sum(dg3r, axis=0, keepdims=True), step)
        _accumulate(dg2_ref, jnp.sum(dg2r, axis=0, keepdims=True), step)

    return fn, [x1, dout, z2, g3, g2], [SDS((n, d), F32), SDS((n, d), BF16), SDS((1, d), F32), SDS((1, d), F32)]


def norm_in_epilogue(x, dx1, g1):
    n, d = x.shape

    def fn(dh1, step, x_ref, dx1_ref, g1_ref, dx_ref, dg1_ref):
        dxa, dgr = _rms_bwd(dh1, x_ref[...], g1_ref[...])
        dx_ref[...] = dx1_ref[...] + dxa
        _accumulate(dg1_ref, jnp.sum(dgr, axis=0, keepdims=True), step)

    return fn, [x, dx1, g1], [SDS((n, d), F32), SDS((1, d), F32)]


def cast_bf16(arrays, name):
    def body(*refs):
        for i_ref, o_ref in zip(refs[:len(arrays)], refs[len(arrays):]):
            o_ref[...] = i_ref[...].astype(BF16)

    return pl.pallas_call(body, name=name, out_shape=[SDS(a.shape, BF16) for a in arrays],
                          compiler_params=_params())(*arrays)


def adam_update(parts, w, m, v, name, tr=None):
    s, r, c = parts.shape
    tr = r if tr is None else tr
    bc1 = 1.0 - ADAM_B1 ** ADAM_STEP
    bc2 = 1.0 - ADAM_B2 ** ADAM_STEP

    def body(p_ref, w_ref, m_ref, v_ref, g_ref, d_ref, nm_ref, nv_ref):
        g = p_ref[0].astype(F32)
        for j in range(1, s):
            g = g + p_ref[j].astype(F32)
        nm = ADAM_B1 * m_ref[...] + (1.0 - ADAM_B1) * g
        nv = ADAM_B2 * v_ref[...] + (1.0 - ADAM_B2) * (g * g)
        g_ref[...] = g
        nm_ref[...] = nm
        nv_ref[...] = nv
        d_ref[...] = -ADAM_LR * ((nm / bc1) / (jnp.sqrt(nv / bc2) + ADAM_EPS) + ADAM_WD * w_ref[...])

    blk = pl.BlockSpec((tr, c), lambda i: (i, 0))
    return pl.pallas_call(
        body, name=name, grid=(r // tr,),
        in_specs=[pl.BlockSpec((s, tr, c), lambda i: (0, i, 0)), blk, blk, blk],
        out_specs=[blk] * 4, out_shape=[SDS((r, c), F32)] * 4,
        compiler_params=_params(("parallel",)),
    )(parts, w, m, v)


EARLY_NAMES = ("spatial_w", "norm_mix_post", "norm_ffn_pre", "norm_ffn_post", "conv_b", "ln_v_gain", "ln_v_bias",
               "spatial_b")
LATE_NAMES = ("norm_mix_pre", "rel_bias")
PACK_ROW_ALIGN = 8


def _pack_rows(size):
    rows = -(-size // LANE)
    return -(-rows // PACK_ROW_ALIGN) * PACK_ROW_ALIGN


def _pack(arrays):
    flat = []
    for a in arrays:
        rows = _pack_rows(a.size)
        flat.append(jnp.pad(a.reshape(-1), (0, rows * LANE - a.size)))
    return jnp.concatenate(flat).reshape(-1, LANE)


def _unpack(packed, shapes):
    out, row = [], 0
    for shp in shapes:
        size = int(np.prod(shp))
        out.append(packed[row:row + _pack_rows(size)].reshape(-1)[:size].reshape(shp))
        row += _pack_rows(size)
    return out


def kernel(x, norm_mix_pre, norm_mix_post, norm_ffn_pre, norm_ffn_post, w_in, ln_v_gain, ln_v_bias, spatial_w, spatial_b, rel_bias, w_out, w_gate, w_up, conv_w, conv_b, w_down, loss_target, m_norm_mix_pre, m_norm_mix_post, m_norm_ffn_pre, m_norm_ffn_post, m_w_in, m_ln_v_gain, m_ln_v_bias, m_spatial_w, m_spatial_b, m_rel_bias, m_w_out, m_w_gate, m_w_up, m_conv_w, m_conv_b, m_w_down, v_norm_mix_pre, v_norm_mix_post, v_norm_ffn_pre, v_norm_ffn_post, v_w_in, v_ln_v_gain, v_ln_v_bias, v_spatial_w, v_spatial_b, v_rel_bias, v_w_out, v_w_gate, v_w_up, v_conv_w, v_conv_b, v_w_down):
    given = dict(locals())
    nb_local, seq, d = x.shape
    n = nb_local * seq
    cols = w_in.shape[2]

    def by_columns(g):
        return g.transpose(1, 0, 2).reshape(g.shape[1], N_DEV * g.shape[2])

    def by_rows(g):
        return g.reshape(N_DEV * g.shape[1], g.shape[2])

    def blocks(g):
        return g.reshape(N_DEV, g.shape[0] // N_DEV, g.shape[1])

    xf, target = x.reshape(n, d), loss_target.reshape(n, d)
    ln_g, ln_b = ln_v_gain.reshape(1, A_WIDTH), ln_v_bias.reshape(1, A_WIDTH)
    spatial_bt, rel_bias_t = spatial_b[0].T, rel_bias.T

    s_in, s_out, s_gate, s_up, s_down = cast_bf16(
        [w_in[0].T, w_out[0], w_gate[0].T, w_up[0].T, w_down[0]], "cast_shards")
    (bias,), (g_in, g_cw) = bias_tables(rel_bias_t, "bias_tables", ride=([], [s_in, conv_w[0]]))
    bias = bias.reshape(2 * len(DILATIONS), B_HEADS, ATTN_BLOCK, 2 * ATTN_BLOCK)
    w_in_t, conv_w_f = by_rows(g_in), by_columns(g_cw)

    (h1, proj), _ = norm_mm(xf, norm_mix_pre, [w_in_t], "fwd_norm_in", tn=IN_COLS)
    a = gating_fwd(proj, ln_g, ln_b, spatial_w[0], spatial_bt, "fwd_gating")
    (b_out, lse_tot), (g_out, g_gate, g_up) = attn_fwd(proj, bias, nb_local, "fwd_attn",
                                                       ride=([], [s_out, s_gate, s_up]))
    w_out_f, w_gate_t, w_up_t = by_rows(g_out), by_rows(g_gate), by_rows(g_up)
    z2, x1 = mm_res_norm([a, b_out], w_out_f, xf, norm_mix_post, "fwd_out_norm")
    (h2, gp, up, gu), (g_down,) = norm_mm(x1, norm_ffn_pre, [w_gate_t, w_up_t], "fwd_norm_ffn_conv", tm=256, tn=D_FF,
                                          ride=([], [s_down]), conv=(conv_w_f, conv_b))
    w_down_f = by_rows(g_down)
    dy, dout, dg4, loss_part = down_loss(gu, w_down_f, x1, norm_ffn_post, target, "fwd_down_loss")

    p_down = mm_tn([gu], [dy], "bwd_dw_down", t1=256, t2=D_MODEL)
    (dgu,), _ = mm_nt([(dy, 0, 0)], [w_down_f], "bwd_dgu", out_dtype=BF16)
    (dgp, dup, p_conv_w, p_conv_b), (r_down,) = conv_gelu_bwd(
        dgu, gp, up, conv_w_f, conv_b, nb_local, "bwd_conv_gelu", ride=([blocks(p_down)], []))
    p_gate = mm_tn([dgp], [h2], "bwd_dw_gate", t1=256, t2=D_MODEL)
    p_up = mm_tn([dup], [h2], "bwd_dw_up", t1=256, t2=D_MODEL)
    (dx1, dz2, dg3, dg2), _ = mm_nt([(dgp, 0, 0), (dup, 1, 0)], [w_gate_t, w_up_t], "bwd_dh2_norm_mid", tm=256,
                                    by_rows=True,
                                    epilogue=norm_mid_epilogue(x1, dout, z2, norm_ffn_pre, norm_mix_post))
    p_out = mm_tn([a, b_out], [dz2], "bwd_dw_out", t1=256, t2=D_MODEL)
    (dmix,), _ = mm_nt([(dz2, 0, 0)], [w_out_f], "bwd_dmix")
    duv, p_ws, p_sbt, p_lng, p_lnb = gating_bwd(proj, dmix, ln_g, ln_b, spatial_w[0], spatial_bt, "bwd_gating")
    small = dict(spatial_w=p_ws, norm_mix_post=dg2, norm_ffn_pre=dg3, norm_ffn_post=dg4, conv_b=p_conv_b,
                 ln_v_gain=p_lng, ln_v_bias=p_lnb, spatial_b=p_sbt.T)
    pack_early = _pack([small[k] for k in EARLY_NAMES] + [p_conv_w, loss_part])
    (dq, dk, dv, dbias), (r_gate, r_up, r_out, r_early) = attn_bwd(
        proj, b_out, dmix, lse_tot, bias, nb_local, "bwd_attn",
        ride=([blocks(p_gate), blocks(p_up), blocks(p_out)], [pack_early]))
    p_rel_bias_t = rel_bias_grad(dbias.reshape(len(DILATIONS), B_HEADS, BIAS_SIZE), "bwd_rel_bias")
    p_in = mm_tn([duv, dq, dk, dv], [h1], "bwd_dw_in", t1=256, t2=D_MODEL)
    (grad_x, dg1), (r_in,) = mm_nt(
        [(duv, 0, 0), (dq, 0, Q_OFF), (dk, 0, K_OFF), (dv, 0, V_OFF)], [w_in_t], "bwd_dh1_norm_in", by_rows=True,
        epilogue=norm_in_epilogue(xf, dx1, norm_mix_pre), ride=([blocks(p_in)], []))
    small.update(norm_mix_pre=dg1, rel_bias=p_rel_bias_t.T)
    (r_late,) = exchange([], [_pack([small[k] for k in LATE_NAMES])], "exchange_late")

    res = {}
    for k, received in (("w_in", r_in), ("w_gate", r_gate), ("w_up", r_up)):
        res[k] = [o.T for o in adam_update(received, given[k][0].T, given["m_" + k][0].T, given["v_" + k][0].T,
                                           "adam_" + k, tr=cols // 2)]
    res["w_out"] = adam_update(r_out, w_out[0], m_w_out[0], v_w_out[0], "adam_w_out")
    res["w_down"] = adam_update(r_down, w_down[0], m_w_down[0], v_w_down[0], "adam_w_down", tr=cols // 2)

    def adam_packed(received, names, tail, name):
        zeros = [jnp.zeros_like(t) for t in tail]
        packs = [_pack([given[pre + k] for k in names] + zeros) for pre in ("", "m_", "v_")]
        shapes = [given[k].shape for k in names] + [t.shape for t in tail]
        unpacked = [_unpack(p, shapes) for p in adam_update(received, *packs, name)]
        for i, k in enumerate(names):
            res[k] = [u[i] for u in unpacked]
        return unpacked[0][len(names):]

    g_conv_w_full, loss_sum = adam_packed(r_early, EARLY_NAMES, [p_conv_w, loss_part], "adam_small_early")
    adam_packed(r_late, LATE_NAMES, [], "adam_small_late")
    g_conv_w = lax.dynamic_slice_in_dim(g_conv_w_full, _my_index() * cols, cols, axis=1)
    res["conv_w"] = adam_update(g_conv_w[None], conv_w[0], m_conv_w[0], v_conv_w[0], "adam_conv_w")
    loss = loss_sum[0, 0]

    names = ("norm_mix_pre", "norm_mix_post", "norm_ffn_pre", "norm_ffn_post", "w_in", "ln_v_gain", "ln_v_bias",
             "spatial_w", "spatial_b", "rel_bias", "w_out", "w_gate", "w_up", "conv_w", "conv_b", "w_down")
    outs = [loss, grad_x.reshape(x.shape)]
    for t in range(4):
        outs += [res[k][t].reshape(given[k].shape) for k in names]
    return tuple(outs)
```

```python
import functools
import math

import numpy as np
import jax
import jax.numpy as jnp
from jax import lax
from jax.experimental import pallas as pl
from jax.experimental.pallas import tpu as pltpu

F32 = jnp.float32
BF16 = jnp.bfloat16
SDS = jax.ShapeDtypeStruct

D_MODEL = 1024
SEQ = 2048
HEAD_DIM = 64
A_GROUPS = 4
A_WIDTH = A_GROUPS * HEAD_DIM
B_HEADS = 12
B_WIDTH = B_HEADS * HEAD_DIM
HEAD_PAIRS = B_HEADS // 2
CHUNK = 128
ATTN_BLOCK = 128
DILATIONS = (1, 4, 16)
NUM_BUCKETS = 32
MAX_DISTANCE = 2048
D_FF = 2816
IN_COLS = 2 * A_WIDTH + 3 * B_WIDTH
Q_OFF = 2 * A_WIDTH
K_OFF = Q_OFF + B_WIDTH
V_OFF = K_OFF + B_WIDTH
NORM_EPS = 1e-6
NEG_INF = -1e30
N_DEV = 8
LANE = 128

ADAM_LR = 0.001
ADAM_B1 = 0.9
ADAM_B2 = 0.999
ADAM_EPS = 1e-08
ADAM_WD = 0.01
ADAM_STEP = 10

GELU_C0 = math.sqrt(2.0 / math.pi)
GELU_C1 = 0.044715

VMEM_LIMIT = 56 * 1024 * 1024


def _params(sem=None):
    if sem is None:
        return pltpu.CompilerParams(vmem_limit_bytes=VMEM_LIMIT)
    return pltpu.CompilerParams(dimension_semantics=sem, vmem_limit_bytes=VMEM_LIMIT)


def _gelu(x):
    t = jnp.tanh(x * (GELU_C0 + (GELU_C0 * GELU_C1) * (x * x)))
    return x * (0.5 + 0.5 * t)


def _gelu_and_grad(x):
    x2 = x * x
    t = jnp.tanh(x * (GELU_C0 + (GELU_C0 * GELU_C1) * x2))
    half = 0.5 + 0.5 * t
    dg = half + x * (0.5 - 0.5 * (t * t)) * (GELU_C0 + (3.0 * GELU_C0 * GELU_C1) * x2)
    return x * half, dg


def _dot(a, b):
    return jnp.dot(a, b, preferred_element_type=F32)


def _dot_nt(a, b):
    return lax.dot_general(a, b, (((1,), (1,)), ((), ())), preferred_element_type=F32)


def _dot_tn(a, b):
    return lax.dot_general(a, b, (((0,), (0,)), ((), ())), preferred_element_type=F32)


def _rms_bwd(d, xin, g):
    r = lax.rsqrt(jnp.mean(xin * xin, axis=-1, keepdims=True) + NORM_EPS)
    xh = xin * r
    gd = g * d
    dx = r * (gd - xh * jnp.mean(gd * xh, axis=-1, keepdims=True))
    return dx, d * xh


MESH = pl.DeviceIdType.MESH
ANY = pl.BlockSpec(memory_space=pl.ANY)
PEER_MASKS = tuple(range(1, N_DEV))


def _my_index():
    return lax.axis_index("x") * 4 + lax.axis_index("y") * 2 + lax.axis_index("c")


def _peer(mask):
    x, y, c = lax.axis_index("x"), lax.axis_index("y"), lax.axis_index("c")
    px = 1 - x if mask & 4 else x
    py = 1 - y if mask & 2 else y
    pc = 1 - c if mask & 1 else c
    return (px, py, pc), px * 4 + py * 2 + pc


RELAY_AT = 3
SIBLING = 1
CHIP_MASKS = (2, 4, 6)


class _Exchange:
    def __init__(self, nblocked, in_refs, out_refs, sems):
        send_sems, recv_sems, local_sems = sems
        me = _my_index()
        sibling, _ = _peer(SIBLING)
        self.local, self.first, self.relays, self.relayed_in, self.last_in = [], [], [], [], []
        for a, (in_ref, out_ref) in enumerate(zip(in_refs, out_refs)):
            def copy(src, slot, mask, to):
                return pltpu.make_async_remote_copy(
                    src_ref=src, dst_ref=out_ref.at[slot], send_sem=send_sems.at[a, mask - 1],
                    recv_sem=recv_sems.at[a, mask - 1], device_id=to, device_id_type=MESH)

            if a < nblocked:
                self.local.append(pltpu.make_async_copy(in_ref.at[me], out_ref.at[me], local_sems.at[a]))
                for mask in PEER_MASKS:
                    peer, pidx = _peer(mask)
                    self.first.append(copy(in_ref.at[pidx], me, mask, peer))
                    self.last_in.append(copy(in_ref.at[pidx], pidx, mask, peer))
                continue
            self.local.append(pltpu.make_async_copy(in_ref, out_ref.at[me], local_sems.at[a]))
            for mask in (SIBLING,) + CHIP_MASKS:
                peer, pidx = _peer(mask)
                self.first.append(copy(in_ref, me, mask, peer))
                (self.last_in if mask == SIBLING else self.relayed_in).append(copy(in_ref, pidx, mask, peer))
            for mask in CHIP_MASKS:
                _, origin = _peer(mask)
                _, far = _peer(mask | SIBLING)
                self.relays.append(copy(out_ref.at[origin], origin, mask | SIBLING, sibling))
                self.last_in.append(copy(in_ref, far, mask | SIBLING, sibling))

    def start(self):
        for cp in self.local + self.first[::-1]:
            cp.start()

    def relay(self):
        for arrived, onward in zip(self.relayed_in, self.relays):
            arrived.wait_recv()
            onward.start()

    def finish(self):
        for cp in self.first + self.relays:
            cp.wait_send()
        for cp in self.last_in:
            cp.wait_recv()
        for cp in self.local:
            cp.wait()


def _exchange_out_shape(blocked, whole):
    return [SDS(b.shape, b.dtype) for b in blocked] + [SDS((N_DEV,) + w.shape, w.dtype) for w in whole]


def _exchange_sems(n):
    return [pltpu.SemaphoreType.DMA((n, N_DEV - 1)), pltpu.SemaphoreType.DMA((n, N_DEV - 1)),
            pltpu.SemaphoreType.DMA((n,))]


def exchange(blocked, whole, name):
    nb, n = len(blocked), len(blocked) + len(whole)

    def body(*refs):
        ex = _Exchange(nb, refs[:n], refs[n:2 * n], refs[2 * n:])
        ex.start()
        ex.relay()
        ex.finish()

    return pl.pallas_call(
        body, name=name, in_specs=[ANY] * n, out_specs=[ANY] * n, out_shape=_exchange_out_shape(blocked, whole),
        scratch_shapes=_exchange_sems(n),
    )(*blocked, *whole)


def _call(body, *, name, grid, in_specs, out_specs, out_shape, args, scratch_shapes=(), sem=None, ride=None):
    out_shape, out_specs, scratch_shapes = list(out_shape), list(out_specs), list(scratch_shapes)
    if ride is None:
        outs = pl.pallas_call(body, name=name, grid=grid, in_specs=list(in_specs), out_specs=out_specs,
                              out_shape=out_shape, scratch_shapes=scratch_shapes,
                              compiler_params=_params(sem))(*args)
        return list(outs), []
    blocked, whole = ride
    cargs = list(blocked) + list(whole)
    nb, nc = len(blocked), len(cargs)
    n_in, n_out, n_scr = len(args), len(out_shape), len(scratch_shapes)
    steps = math.prod(grid)
    assert steps >= 3, grid

    def riding(*refs):
        ins, refs = refs[:n_in], refs[n_in:]
        cins, refs = refs[:nc], refs[nc:]
        outs, refs = refs[:n_out], refs[n_out:]
        couts, refs = refs[:nc], refs[nc:]
        scr, sems = refs[:n_scr], refs[n_scr:]
        step = functools.reduce(lambda acc, k: acc * grid[k] + pl.program_id(k), range(len(grid)), 0)

        @pl.when(step == 0)
        def _():
            _Exchange(nb, cins, couts, sems).start()

        @pl.when(step == RELAY_AT * steps // 4)
        def _():
            _Exchange(nb, cins, couts, sems).relay()

        body(*ins, *outs, *scr)

        @pl.when(step == steps - 1)
        def _():
            _Exchange(nb, cins, couts, sems).finish()

    res = pl.pallas_call(
        riding, name=name, grid=grid, in_specs=list(in_specs) + [ANY] * nc, out_specs=out_specs + [ANY] * nc,
        out_shape=out_shape + _exchange_out_shape(blocked, whole),
        scratch_shapes=scratch_shapes + _exchange_sems(nc),
        compiler_params=_params(("arbitrary",) * len(grid)))(*args, *cargs)
    return list(res[:n_out]), list(res[n_out:])


def norm_mm(x, g, ws, name, tm=512, tn=1408, ride=None, conv=None):
    n, d = x.shape
    f = ws[0].shape[0]
    nw = len(ws)
    extra_in, extra_spec, extra_out, extra_out_spec, scratch = [], [], [], [], []
    if conv is not None:
        assert nw == 2 and tn == f and SEQ % tm == 0 and tm % CONV_ROWS == 0
        extra_in = list(conv)
        extra_spec = [pl.BlockSpec((3, f), lambda i, j: (0, 0)), pl.BlockSpec((1, f), lambda i, j: (0, 0))]
        extra_out, extra_out_spec = [SDS((n, f), BF16)], [pl.BlockSpec((tm, f), lambda i, j: (i, 0))]
        scratch = [pltpu.VMEM((PAD + CONV_ROWS, f), F32), pltpu.VMEM((PAD, f), F32)]

    def body(x_ref, g_ref, *refs):
        w_refs, refs = refs[:nw], refs[nw:]
        conv_refs, refs = refs[:len(extra_in)], refs[len(extra_in):]
        h_ref, o_refs, refs = refs[0], refs[1:1 + nw], refs[1 + nw:]

        @pl.when(pl.program_id(1) == 0)
        def _():
            xv = x_ref[...]
            r = lax.rsqrt(jnp.mean(xv * xv, axis=-1, keepdims=True) + NORM_EPS)
            h_ref[...] = (xv * r * g_ref[...]).astype(BF16)

        h = h_ref[...]
        for w_ref, o_ref in zip(w_refs, o_refs):
            o_ref[...] = _dot_nt(h, w_ref[...])
        if conv is None:
            return
        (cw_ref, cb_ref), (gp_ref, up_ref), (gu_ref, head_ref, carry_ref) = conv_refs, o_refs, refs

        @pl.when(pl.program_id(0) % (SEQ // tm) == 0)
        def _():
            carry_ref[...] = jnp.zeros_like(carry_ref)

        head_ref[0:PAD, :] = carry_ref[...]
        head_ref[PAD:PAD + CONV_ROWS, :] = gp_ref[0:CONV_ROWS, :]
        for l0 in range(0, f, LANE):
            ls = slice(l0, l0 + LANE)
            w0, w1, w2, bias = cw_ref[0:1, ls], cw_ref[1:2, ls], cw_ref[2:3, ls], cb_ref[:, ls]
            for r0 in range(0, tm, CONV_ROWS):
                g0, g1, g2 = _conv_taps(gp_ref, head_ref, r0, ls)
                c = bias + w0 * g2 + w1 * g1 + w2 * g0
                gu_ref[r0:r0 + CONV_ROWS, ls] = (_gelu(c) * up_ref[r0:r0 + CONV_ROWS, ls]).astype(BF16)
        carry_ref[...] = gp_ref[tm - PAD:tm, :]

    return _call(
        body, name=name, grid=(n // tm, f // tn),
        in_specs=[pl.BlockSpec((tm, d), lambda i, j: (i, 0)), pl.BlockSpec((1, d), lambda i, j: (0, 0))]
        + [pl.BlockSpec((tn, d), lambda i, j: (j, 0)) for _ in ws] + extra_spec,
        out_specs=[pl.BlockSpec((tm, d), lambda i, j: (i, 0))]
        + [pl.BlockSpec((tm, tn), lambda i, j: (i, j)) for _ in ws] + extra_out_spec,
        out_shape=[SDS((n, d), BF16)] + [SDS((n, f), F32) for _ in ws] + extra_out,
        scratch_shapes=scratch,
        args=[x, g, *ws, *extra_in], sem=("parallel" if conv is None else "arbitrary", "arbitrary"), ride=ride)


def _lane_concat(refs):
    vals = [r[...].astype(BF16) for r in refs]
    return vals[0] if len(vals) == 1 else jnp.concatenate(vals, axis=1)


def mm_res_norm(a_list, w, res, g, name, tm=512):
    n = a_list[0].shape[0]
    k, d = w.shape
    na = len(a_list)

    def body(*refs):
        w_ref, res_ref, g_ref, y_ref, o_ref = refs[na:]
        y = _dot(_lane_concat(refs[:na]), w_ref[...])
        r = lax.rsqrt(jnp.mean(y * y, axis=-1, keepdims=True) + NORM_EPS)
        y_ref[...] = y
        o_ref[...] = res_ref[...] + y * r * g_ref[...]

    return pl.pallas_call(
        body, name=name, grid=(n // tm,),
        in_specs=[pl.BlockSpec((tm, a.shape[1]), lambda i: (i, 0)) for a in a_list]
        + [pl.BlockSpec((k, d), lambda i: (0, 0)),
           pl.BlockSpec((tm, d), lambda i: (i, 0)), pl.BlockSpec((1, d), lambda i: (0, 0))],
        out_specs=[pl.BlockSpec((tm, d), lambda i: (i, 0)), pl.BlockSpec((tm, d), lambda i: (i, 0))],
        out_shape=[SDS((n, d), F32), SDS((n, d), F32)],
        compiler_params=_params(("parallel",)),
    )(*a_list, w, res, g)


def down_loss(a, w, res, g, target, name, tm=256):
    n, k = a.shape
    d = w.shape[1]
    inv_d = 1.0 / d

    def body(a_ref, w_ref, res_ref, g_ref, t_ref, dy_ref, dout_ref, dg_ref, loss_ref):
        i = pl.program_id(0)
        y = _dot(a_ref[...], w_ref[...])
        gv = g_ref[...]
        r = lax.rsqrt(jnp.mean(y * y, axis=-1, keepdims=True) + NORM_EPS)
        yh = y * r
        e = res_ref[...] + yh * gv - t_ref[...]
        part = 0.5 * inv_d * jnp.sum(jnp.sum(e * e, axis=-1, keepdims=True), axis=0, keepdims=True)
        dout = e * inv_d
        dout_ref[...] = dout
        gd = gv * dout
        dy_ref[...] = (r * (gd - yh * jnp.mean(gd * yh, axis=-1, keepdims=True))).astype(BF16)
        dgp = jnp.sum(dout * yh, axis=0, keepdims=True)
        lane0 = lax.broadcasted_iota(jnp.int32, (1, LANE), 1) == 0
        lp = jnp.where(lane0, part, 0.0)

        @pl.when(i == 0)
        def _():
            dg_ref[...] = dgp
            loss_ref[...] = lp

        @pl.when(i > 0)
        def _():
            dg_ref[...] += dgp
            loss_ref[...] += lp

    return pl.pallas_call(
        body, name=name, grid=(n // tm,),
        in_specs=[pl.BlockSpec((tm, k), lambda i: (i, 0)), pl.BlockSpec((k, d), lambda i: (0, 0)),
                  pl.BlockSpec((tm, d), lambda i: (i, 0)), pl.BlockSpec((1, d), lambda i: (0, 0)),
                  pl.BlockSpec((tm, d), lambda i: (i, 0))],
        out_specs=[pl.BlockSpec((tm, d), lambda i: (i, 0)), pl.BlockSpec((tm, d), lambda i: (i, 0)),
                   pl.BlockSpec((1, d), lambda i: (0, 0)), pl.BlockSpec((1, LANE), lambda i: (0, 0))],
        out_shape=[SDS((n, d), BF16), SDS((n, d), F32), SDS((1, d), F32), SDS((1, LANE), F32)],
        compiler_params=_params(("arbitrary",)),
    )(a, w, res, g, target)


def _accumulate(ref, val, step):
    @pl.when(step == 0)
    def _():
        ref[...] = val

    @pl.when(step > 0)
    def _():
        ref[...] += val


def mm_nt(terms, ws, name, tm=512, out_dtype=F32, ride=None, epilogue=None, by_rows=False):
    n = terms[0][0].shape[0]
    r = ws[0].shape[1 if by_rows else 0]
    na = len(terms)
    meta = [(widx, off, a.shape[1]) for a, widx, off in terms]
    fn, extras, out_shape = epilogue if epilogue else (None, [], [SDS((n, r), out_dtype)])
    n_fixed = na + len(ws)

    def body(*refs):
        a_refs = refs[:na]
        w_refs = refs[na:n_fixed]
        acc = None
        for a_ref, (widx, off, k) in zip(a_refs, meta):
            a = a_ref[...].astype(BF16)
            p = _dot(a, w_refs[widx][off:off + k, :]) if by_rows else _dot_nt(a, w_refs[widx][:, off:off + k])
            acc = p if acc is None else acc + p
        if fn is None:
            refs[-1][...] = acc.astype(out_dtype)
        else:
            fn(acc, pl.program_id(0), *refs[n_fixed:])

    def spec(a):
        if a.shape[0] == 1:
            return pl.BlockSpec(a.shape, lambda i: (0, 0))
        return pl.BlockSpec((tm, a.shape[1]), lambda i: (i, 0))

    return _call(
        body, name=name, grid=(n // tm,),
        in_specs=[spec(a) for a, _, _ in terms] + [pl.BlockSpec(w.shape, lambda i: (0, 0)) for w in ws]
        + [spec(e) for e in extras],
        out_specs=[spec(o) for o in out_shape], out_shape=out_shape,
        args=[a for a, _, _ in terms] + list(ws) + list(extras),
        sem=("parallel",) if fn is None else ("arbitrary",), ride=ride)


def _piece_blocks(pieces, tile):
    out, first = [], 0
    for p in pieces:
        nblk, rem = divmod(p.shape[1], tile)
        assert rem == 0, (p.shape, tile)
        out.append((first, nblk))
        first += nblk
    return out, first


def mm_tn(lhs_list, rhs_list, name, t1, t2, out_dtype=BF16):
    n = lhs_list[0].shape[0]
    lblocks, nbl = _piece_blocks(lhs_list, t1)
    rblocks, nbr = _piece_blocks(rhs_list, t2)
    nl = len(lhs_list)

    def body(*refs):
        l_refs, r_refs, o_ref = refs[:nl], refs[nl:-1], refs[-1]
        i, j = pl.program_id(0), pl.program_id(1)
        for l_ref, (ls, ln) in zip(l_refs, lblocks):
            for r_ref, (rs, rn) in zip(r_refs, rblocks):
                @pl.when((i >= ls) & (i < ls + ln) & (j >= rs) & (j < rs + rn))
                def _(l_ref=l_ref, r_ref=r_ref):
                    o_ref[...] = _dot_tn(l_ref[...].astype(BF16), r_ref[...].astype(BF16)).astype(out_dtype)

    def piece_spec(tile, axis, first, nblk):
        def index(i, j):
            return 0, jnp.clip((i, j)[axis] - first, 0, nblk - 1)
        return pl.BlockSpec((n, tile), index)

    return pl.pallas_call(
        body, name=name, grid=(nbl, nbr),
        in_specs=[piece_spec(t1, 0, *b) for b in lblocks] + [piece_spec(t2, 1, *b) for b in rblocks],
        out_specs=pl.BlockSpec((t1, t2), lambda i, j: (i, j)),
        out_shape=SDS((nbl * t1, nbr * t2), out_dtype),
        compiler_params=_params(("parallel", "arbitrary")),
    )(*lhs_list, *rhs_list)


GATE_ROWS = 512


def _tril_mask():
    row = lax.broadcasted_iota(jnp.int32, (CHUNK, CHUNK), 0)
    col = lax.broadcasted_iota(jnp.int32, (CHUNK, CHUNK), 1)
    return row >= col


def _group_of(shape, axis):
    return lax.broadcasted_iota(jnp.int32, shape, axis) // HEAD_DIM


def _group_mean_matrix():
    same = _group_of((A_WIDTH, A_WIDTH), 0) == _group_of((A_WIDTH, A_WIDTH), 1)
    return jnp.where(same, 1.0 / HEAD_DIM, 0.0).astype(BF16)


def _dot_sum(a, b):
    hi = a.astype(BF16)
    lo = (a - hi.astype(F32)).astype(BF16)
    return _dot(hi, b) + _dot(lo, b)


def _by_group(parts, lane_group):
    out = parts[A_GROUPS - 1]
    for g in range(A_GROUPS - 2, -1, -1):
        out = jnp.where(lane_group == g, parts[g], out)
    return out


def _group_norm(gv, gmean):
    xc = gv - _dot_sum(gv, gmean)
    rstd = lax.rsqrt(_dot_sum(xc * xc, gmean) + NORM_EPS)
    return xc * rstd, rstd


def gating_fwd(proj, lng, lnb, ws, sbt, name):
    n = proj.shape[0]

    def body(u_ref, v_ref, lng_ref, lnb_ref, ws_ref, sbt_ref, a_ref):
        tril = _tril_mask()
        lane_group = _group_of((CHUNK, A_WIDTH), 1)
        gmean = _group_mean_matrix()
        wts = [jnp.where(tril, ws_ref[g], 0.0).astype(BF16) for g in range(A_GROUPS)]
        sb = _by_group([sbt_ref[:, g:g + 1] for g in range(A_GROUPS)], lane_group)

        def chunk(c, carry):
            rows = pl.ds(pl.multiple_of(c * CHUNK, CHUNK), CHUNK)
            vhat, _ = _group_norm(_gelu(v_ref[rows, :]), gmean)
            vn = (vhat * lng_ref[...] + lnb_ref[...]).astype(BF16)
            z = _by_group([_dot(wt, vn) for wt in wts], lane_group) + sb
            a_ref[rows, :] = _gelu(u_ref[rows, :]) * z
            return carry

        lax.fori_loop(0, GATE_ROWS // CHUNK, chunk, 0)

    return pl.pallas_call(
        body, name=name, grid=(n // GATE_ROWS,),
        in_specs=[pl.BlockSpec((GATE_ROWS, A_WIDTH), lambda i: (i, 0)),
                  pl.BlockSpec((GATE_ROWS, A_WIDTH), lambda i: (i, 1)),
                  pl.BlockSpec((1, A_WIDTH), lambda i: (0, 0)), pl.BlockSpec((1, A_WIDTH), lambda i: (0, 0)),
                  pl.BlockSpec((A_GROUPS, CHUNK, CHUNK), lambda i: (0, 0, 0)),
                  pl.BlockSpec((CHUNK, A_GROUPS), lambda i: (0, 0))],
        out_specs=pl.BlockSpec((GATE_ROWS, A_WIDTH), lambda i: (i, 0)),
        out_shape=SDS((n, A_WIDTH), F32),
        compiler_params=_params(("parallel",)),
    )(proj, proj, lng, lnb, ws, sbt)


def gating_bwd(proj, dmix, lng, lnb, ws, sbt, name):
    n = proj.shape[0]

    def body(u_ref, v_ref, da_ref, lng_ref, lnb_ref, ws_ref, sbt_ref,
             duv_ref, dws_ref, dsbt_ref, dlng_ref, dlnb_ref):
        @pl.when(pl.program_id(0) == 0)
        def _():
            dws_ref[...] = jnp.zeros_like(dws_ref)
            dsbt_ref[...] = jnp.zeros_like(dsbt_ref)
            dlng_ref[...] = jnp.zeros_like(dlng_ref)
            dlnb_ref[...] = jnp.zeros_like(dlnb_ref)

        tril = _tril_mask()
        lane_group = _group_of((CHUNK, A_WIDTH), 1)
        gmean = _group_mean_matrix()
        gsum = (_group_of((A_WIDTH, LANE), 0) == lax.broadcasted_iota(jnp.int32, (A_WIDTH, LANE), 1)).astype(BF16)
        wts = [jnp.where(tril, ws_ref[g], 0.0) for g in range(A_GROUPS)]
        wts_b = [w.astype(BF16) for w in wts]
        wts_t = [w.T.astype(BF16) for w in wts]
        sb = _by_group([sbt_ref[:, g:g + 1] for g in range(A_GROUPS)], lane_group)
        lg = lng_ref[...]

        def chunk(c, carry):
            rows = pl.ds(pl.multiple_of(c * CHUNK, CHUNK), CHUNK)
            gu, dgu_dx = _gelu_and_grad(u_ref[rows, :])
            gv, dgv_dx = _gelu_and_grad(v_ref[rows, :])
            vhat, rstd = _group_norm(gv, gmean)
            vn = (vhat * lg + lnb_ref[...]).astype(BF16)
            z = _by_group([_dot(wt, vn) for wt in wts_b], lane_group) + sb
            da = da_ref[rows, :]
            dz = da * gu
            dzb = dz.astype(BF16)
            duv_ref[rows, 0:A_WIDTH] = (da * z * dgu_dx).astype(BF16)
            dsbt_ref[...] += _dot_sum(dz, gsum)[:, 0:A_GROUPS]
            for g in range(A_GROUPS):
                dz_g = jnp.where(lane_group == g, dzb, jnp.zeros_like(dzb))
                dws_ref[g] += jnp.where(tril, _dot_nt(dz_g, vn), 0.0)
            dvn = _by_group([_dot(wt, dzb) for wt in wts_t], lane_group)
            dlng_ref[...] += jnp.sum(dvn * vhat, axis=0, keepdims=True)
            dlnb_ref[...] += jnp.sum(dvn, axis=0, keepdims=True)
            dvh = dvn * lg
            dgv = rstd * (dvh - _dot_sum(dvh, gmean) - vhat * _dot_sum(dvh * vhat, gmean))
            duv_ref[rows, A_WIDTH:2 * A_WIDTH] = (dgv * dgv_dx).astype(BF16)
            return carry

        lax.fori_loop(0, GATE_ROWS // CHUNK, chunk, 0)

    return pl.pallas_call(
        body, name=name, grid=(n // GATE_ROWS,),
        in_specs=[pl.BlockSpec((GATE_ROWS, A_WIDTH), lambda i: (i, 0)),
                  pl.BlockSpec((GATE_ROWS, A_WIDTH), lambda i: (i, 1)),
                  pl.BlockSpec((GATE_ROWS, A_WIDTH), lambda i: (i, 0)),
                  pl.BlockSpec((1, A_WIDTH), lambda i: (0, 0)), pl.BlockSpec((1, A_WIDTH), lambda i: (0, 0)),
                  pl.BlockSpec((A_GROUPS, CHUNK, CHUNK), lambda i: (0, 0, 0)),
                  pl.BlockSpec((CHUNK, A_GROUPS), lambda i: (0, 0))],
        out_specs=[pl.BlockSpec((GATE_ROWS, 2 * A_WIDTH), lambda i: (i, 0)),
                   pl.BlockSpec((A_GROUPS, CHUNK, CHUNK), lambda i: (0, 0, 0)),
                   pl.BlockSpec((CHUNK, A_GROUPS), lambda i: (0, 0)),
                   pl.BlockSpec((1, A_WIDTH), lambda i: (0, 0)), pl.BlockSpec((1, A_WIDTH), lambda i: (0, 0))],
        out_shape=[SDS((n, 2 * A_WIDTH), BF16), SDS((A_GROUPS, CHUNK, CHUNK), F32), SDS((CHUNK, A_GROUPS), F32),
                   SDS((1, A_WIDTH), F32), SDS((1, A_WIDTH), F32)],
        compiler_params=_params(("arbitrary",)),
    )(proj, proj, dmix, lng, lnb, ws, sbt)


def _t5_bucket_np(dist):
    max_exact = NUM_BUCKETS // 2
    dd = np.maximum(dist, 1).astype(np.float64)
    large = max_exact + np.log(dd / max_exact) / math.log(MAX_DISTANCE / max_exact) * (NUM_BUCKETS - max_exact)
    large = np.minimum(large.astype(np.int64), NUM_BUCKETS - 1)
    return np.where(dist < max_exact, dist, large)


def _bucket_tables(with_first):
    i = np.arange(ATTN_BLOCK)[:, None]
    j = np.arange(2 * ATTN_BLOCK)[None, :]
    rel = ATTN_BLOCK + i - j
    band = (rel >= 0) & (rel <= ATTN_BLOCK)
    tabs = []
    for own_only in (False, True) if with_first else (False,):
        for dil in DILATIONS:
            b = _t5_bucket_np(np.maximum(rel, 0) * dil)
            tabs.append(np.where(band & (j >= ATTN_BLOCK) if own_only else band, b, -1).reshape(1, -1))
    return np.stack(tabs).astype(np.float32)


BIAS_SIZE = ATTN_BLOCK * 2 * ATTN_BLOCK


def bias_tables(rel_bias_t, name, ride=None):
    idx = jnp.asarray(_bucket_tables(True))
    ntab = idx.shape[0]

    def body(rb_ref, idx_ref, o_ref):
        iv = idx_ref[0]
        bk = lax.broadcasted_iota(jnp.int32, (NUM_BUCKETS, BIAS_SIZE), 0).astype(F32)
        onehot = (bk == iv).astype(F32)
        t = jnp.dot(rb_ref[...], onehot, preferred_element_type=F32, precision=lax.Precision.HIGHEST)
        o_ref[0] = jnp.where(iv < 0.0, NEG_INF, t)

    return _call(
        body, name=name, grid=(ntab,),
        in_specs=[pl.BlockSpec((B_HEADS, NUM_BUCKETS), lambda d: (0, 0)),
                  pl.BlockSpec((1, 1, BIAS_SIZE), lambda d: (d, 0, 0))],
        out_specs=[pl.BlockSpec((1, B_HEADS, BIAS_SIZE), lambda d: (d, 0, 0))],
        out_shape=[SDS((ntab, B_HEADS, BIAS_SIZE), F32)],
        args=[rel_bias_t, idx], sem=("parallel",), ride=ride)


def rel_bias_grad(dbias, name):
    idx = jnp.asarray(_bucket_tables(False))

    def body(db_ref, idx_ref, o_ref):
        d = pl.program_id(0)
        iv = idx_ref[0]
        bk = lax.broadcasted_iota(jnp.int32, (NUM_BUCKETS, BIAS_SIZE), 0).astype(F32)
        onehot = (bk == iv).astype(F32)
        part = lax.dot_general(db_ref[0], onehot, (((1,), (1,)), ((), ())),
                               preferred_element_type=F32, precision=lax.Precision.HIGHEST)

        @pl.when(d == 0)
        def _():
            o_ref[...] = part

        @pl.when(d > 0)
        def _():
            o_ref[...] += part

    return pl.pallas_call(
        body, name=name, grid=(len(DILATIONS),),
        in_specs=[pl.BlockSpec((1, B_HEADS, BIAS_SIZE), lambda d: (d, 0, 0)),
                  pl.BlockSpec((1, 1, BIAS_SIZE), lambda d: (d, 0, 0))],
        out_specs=pl.BlockSpec((B_HEADS, NUM_BUCKETS), lambda d: (0, 0)),
        out_shape=SDS((B_HEADS, NUM_BUCKETS), F32),
        compiler_params=_params(("arbitrary",)),
    )(dbias, idx)


QK_SCALE = 1.0 / math.sqrt(HEAD_DIM)


def _attn_scores(q_scaled, kk, bias):
    return _dot_nt(q_scaled, kk) + bias


def _head0_lanes():
    return lax.broadcasted_iota(jnp.int32, (ATTN_BLOCK, LANE), 1) < HEAD_DIM


def _one_head(x2, head0, hh):
    return jnp.where(head0 if hh == 0 else jnp.logical_not(head0), x2, 0.0).astype(BF16)


QUAD = 4
QUAD_ROWS = SEQ // QUAD


def _deinterleave(src_ref, dst_ref):
    for r in range(QUAD):
        for c in range(QUAD_ROWS // ATTN_BLOCK):
            dst_ref[r, c * ATTN_BLOCK:(c + 1) * ATTN_BLOCK, :] = src_ref[
                pl.ds(r + c * QUAD * ATTN_BLOCK, ATTN_BLOCK, stride=QUAD), :]


def _deinterleave_again(src_ref, dst_ref):
    for r in range(QUAD):
        for s in range(QUAD):
            dst_ref[r + QUAD * s] = src_ref[r, pl.ds(s, ATTN_BLOCK, stride=QUAD), :]


def _interleave_back(src_ref, dst_ref, slot0, accumulate=False):
    for r in range(QUAD):
        for s in range(QUAD):
            rows = pl.ds(s, ATTN_BLOCK, stride=QUAD)
            if accumulate:
                dst_ref[slot0 + r, rows, :] += src_ref[r + QUAD * s]
            else:
                dst_ref[slot0 + r, rows, :] = src_ref[r + QUAD * s]


def _quad_tiles():
    return [(r, pl.ds(r + c * QUAD * ATTN_BLOCK, ATTN_BLOCK, stride=QUAD), slice(c * ATTN_BLOCK, (c + 1) * ATTN_BLOCK))
            for r in range(QUAD) for c in range(QUAD_ROWS // ATTN_BLOCK)]


def _attn_schedule(op):
    def d16(i, carry):
        for t in range(2 * QUAD):
            op(2, 2 * QUAD * i + t, 0, True)
        return carry

    lax.fori_loop(0, QUAD // 2, d16, 0)

    def d4(i, carry):
        for u in range(2):
            for nq in range(QUAD_ROWS // ATTN_BLOCK):
                op(1, 2 * i + u, nq * ATTN_BLOCK, nq == 0)
        return carry

    lax.fori_loop(0, QUAD // 2, d4, 0)
    op(0, None, 0, True)
    per_pass = 5

    def d1(j, carry):
        for t in range(per_pass):
            op(0, None, pl.multiple_of((1 + per_pass * j + t) * ATTN_BLOCK, ATTN_BLOCK), False)
        return carry

    lax.fori_loop(0, (SEQ // ATTN_BLOCK - 1) // per_pass, d1, 0)


def _keys(src, krows, first):
    kb = src[krows, :].astype(BF16)
    return jnp.concatenate([kb, kb], axis=0) if first else kb


def _table(seg, first):
    return len(DILATIONS) + seg if first else seg


def _kv_rows(start, first):
    return pl.ds(start, ATTN_BLOCK) if first else pl.ds(start - ATTN_BLOCK, 2 * ATTN_BLOCK)


MERGE_ROWS = 256


def attn_fwd(proj, bias, nb_local, name, ride=None):
    n = proj.shape[0]
    nseg = len(DILATIONS)

    def body(q_ref, k_ref, v_ref, b_ref, o_ref, lse_ref, q4_ref, k4_ref, v4_ref, os0_ref, ls0_ref, os4_ref, ls4_ref,
             q16_ref, k16_ref, v16_ref, os16_ref, ls16_ref):
        for src, mid, dst in ((q_ref, q4_ref, q16_ref), (k_ref, k4_ref, k16_ref), (v_ref, v4_ref, v16_ref)):
            _deinterleave(src, mid)
            _deinterleave_again(mid, dst)

        def op(seg, r, start, first):
            qrows = pl.ds(start, ATTN_BLOCK)
            krows = _kv_rows(start, first)
            if seg == 0:
                q_src, k_src, v_src, o_dst, l_dst = q_ref, k_ref, v_ref, os0_ref, ls0_ref
            elif seg == 1:
                q_src, k_src, v_src = q4_ref.at[r], k4_ref.at[r], v4_ref.at[r]
                o_dst, l_dst = os4_ref.at[r], ls4_ref.at[r]
            else:
                q_src, k_src, v_src = q16_ref.at[r], k16_ref.at[r], v16_ref.at[r]
                o_dst, l_dst = os16_ref.at[r], ls16_ref.at[r]
            q2, kb, vb = q_src[qrows, :] * QK_SCALE, _keys(k_src, krows, first), _keys(v_src, krows, first)
            head0 = _head0_lanes()
            outs, lses = [], []
            for hh in range(2):
                s = _attn_scores(_one_head(q2, head0, hh), kb, b_ref[_table(seg, first), hh])
                m = jnp.max(s, axis=-1, keepdims=True)
                p = jnp.exp(s - m)
                l = jnp.sum(p, axis=-1, keepdims=True)
                outs.append(_dot(p.astype(BF16), vb) / l)
                lses.append(jnp.broadcast_to(m + jnp.log(l), (ATTN_BLOCK, LANE)))
            o_dst[qrows, :] = jnp.where(head0, outs[0], outs[1])
            l_dst[qrows, :] = jnp.where(head0, lses[0], lses[1])

        _attn_schedule(op)
        _interleave_back(os16_ref, os4_ref, QUAD)
        _interleave_back(ls16_ref, ls4_ref, QUAD)

        for r, nat, quad in _quad_tiles():
            ls = [ls0_ref[nat, :], ls4_ref[r, quad, :], ls4_ref[QUAD + r, quad, :]]
            m = functools.reduce(jnp.maximum, ls)
            ws = [jnp.exp(l - m) for l in ls]
            den = ws[0] + ws[1] + ws[2]
            num = ws[0] * os0_ref[nat, :] + ws[1] * os4_ref[r, quad, :] + ws[2] * os4_ref[QUAD + r, quad, :]
            o_ref[nat, :] = num / den
            lse_ref[nat, :] = m + jnp.log(den)

    def in_spec(off):
        return pl.BlockSpec((SEQ, LANE), lambda b, p: (b, off // LANE + p))

    out_spec = pl.BlockSpec((SEQ, LANE), lambda b, p: (b, p))
    return _call(
        body, name=name, grid=(nb_local, HEAD_PAIRS),
        in_specs=[in_spec(Q_OFF), in_spec(K_OFF), in_spec(V_OFF),
                  pl.BlockSpec((2 * nseg, 2, ATTN_BLOCK, 2 * ATTN_BLOCK), lambda b, p: (0, p, 0, 0))],
        out_specs=[out_spec, out_spec],
        out_shape=[SDS((n, B_WIDTH), F32), SDS((n, B_WIDTH), F32)],
        scratch_shapes=[pltpu.VMEM((QUAD, QUAD_ROWS, LANE), F32)] * 3 + [pltpu.VMEM((SEQ, LANE), F32)] * 2
        + [pltpu.VMEM((2 * QUAD, QUAD_ROWS, LANE), F32)] * 2 + [pltpu.VMEM((QUAD * QUAD, ATTN_BLOCK, LANE), F32)] * 5,
        args=[proj, proj, proj, bias], sem=("parallel", "arbitrary"), ride=ride)


def attn_bwd(proj, b_out, dmix, lse_tot, bias, nb_local, name, ride=None):
    n = proj.shape[0]
    nseg = len(DILATIONS)
    a_blocks = A_WIDTH // LANE

    def body(q_ref, k_ref, v_ref, o_ref, do_ref, lse_ref, b_ref, dq_ref, dk_ref, dv_ref, db_ref,
             dqs_ref, delta_ref, dka_ref, dva_ref, q4_ref, k4_ref, v4_ref, do4_ref, lse4_ref, delta4_ref,
             dqs4_ref, dk4_ref, dv4_ref, q16_ref, k16_ref, v16_ref, do16_ref, lse16_ref, delta16_ref,
             dqs16_ref, dk16_ref, dv16_ref):
        @pl.when(pl.program_id(1) == 0)
        def _():
            db_ref[...] = jnp.zeros_like(db_ref)

        for acc_ref in (dka_ref, dva_ref, dk4_ref, dv4_ref):
            acc_ref[...] = jnp.zeros_like(acc_ref)
        quads = (q4_ref, k4_ref, v4_ref, do4_ref, lse4_ref, delta4_ref)
        hexes = (q16_ref, k16_ref, v16_ref, do16_ref, lse16_ref, delta16_ref)

        head_sum = (lax.broadcasted_iota(jnp.int32, (LANE, LANE), 0) // HEAD_DIM
                    == lax.broadcasted_iota(jnp.int32, (LANE, LANE), 1) // HEAD_DIM).astype(BF16)

        def row_dots(i, carry):
            rows = pl.ds(pl.multiple_of(i * MERGE_ROWS, MERGE_ROWS), MERGE_ROWS)
            delta_ref[rows, :] = _dot_sum(do_ref[rows, :] * o_ref[rows, :], head_sum)
            return carry

        lax.fori_loop(0, SEQ // MERGE_ROWS, row_dots, 0)
        for src, mid, dst in zip((q_ref, k_ref, v_ref, do_ref, lse_ref, delta_ref), quads, hexes):
            _deinterleave(src, mid)
            _deinterleave_again(mid, dst)

        def op(seg, r, start, first):
            qrows = pl.ds(start, ATTN_BLOCK)
            krows = _kv_rows(start, first)
            if seg == 0:
                srcs = (q_ref, k_ref, v_ref, do_ref, lse_ref, delta_ref)
                dq_dst, dk_dst, dv_dst = dqs_ref, dka_ref, dva_ref
            elif seg == 1:
                srcs = tuple(x.at[r] for x in quads)
                dq_dst, dk_dst, dv_dst = dqs4_ref.at[r], dk4_ref.at[r], dv4_ref.at[r]
            else:
                srcs = tuple(x.at[r] for x in hexes)
                dq_dst, dk_dst, dv_dst = dqs16_ref.at[r], dk16_ref.at[r], dv16_ref.at[r]
            q_src, k_src, v_src, do_src, lse_src, delta_src = srcs
            q2, kb, vb = q_src[qrows, :] * QK_SCALE, _keys(k_src, krows, first), _keys(v_src, krows, first)
            do2, lse2, delta2 = do_src[qrows, :], lse_src[qrows, :], delta_src[qrows, :]
            head0 = _head0_lanes()
            dqs, dk, dv = [], None, None
            for hh in range(2):
                col = slice(hh * HEAD_DIM, hh * HEAD_DIM + 1)
                q, dob = _one_head(q2, head0, hh), _one_head(do2, head0, hh)
                p = jnp.exp(_attn_scores(q, kb, b_ref[_table(seg, first), hh]) - lse2[:, col])
                dvh = _dot_tn(p.astype(BF16), dob)
                ds = p * (_dot_nt(dob, vb) - delta2[:, col])
                if first:
                    db_ref[seg, hh, :, ATTN_BLOCK:] += ds[:, ATTN_BLOCK:]
                else:
                    db_ref[seg, hh] += ds
                dsb = ds.astype(BF16)
                dqs.append(_dot(dsb, kb))
                dkh = _dot_tn(dsb, q)
                dk = dkh if dk is None else dk + dkh
                dv = dvh if dv is None else dv + dvh
            if first:
                dk, dv = dk[ATTN_BLOCK:], dv[ATTN_BLOCK:]
            dq_dst[qrows, :] = jnp.where(head0, dqs[0], dqs[1]) * QK_SCALE
            if seg == 2:
                dk_dst[krows, :] = dk
                dv_dst[krows, :] = dv
            else:
                dk_dst[krows, :] += dk
                dv_dst[krows, :] += dv

        _attn_schedule(op)
        _interleave_back(dqs16_ref, dqs4_ref, QUAD)
        _interleave_back(dk16_ref, dk4_ref, 0, accumulate=True)
        _interleave_back(dv16_ref, dv4_ref, 0, accumulate=True)

        for r, nat, quad in _quad_tiles():
            dqs_ref[nat, :] += dqs4_ref[r, quad, :] + dqs4_ref[QUAD + r, quad, :]
            dka_ref[nat, :] += dk4_ref[r, quad, :]
            dva_ref[nat, :] += dv4_ref[r, quad, :]

        def merge(i, carry):
            rows = pl.ds(pl.multiple_of(i * MERGE_ROWS, MERGE_ROWS), MERGE_ROWS)
            dq_ref[rows, :] = dqs_ref[rows, :].astype(BF16)
            dk_ref[rows, :] = dka_ref[rows, :].astype(BF16)
            dv_ref[rows, :] = dva_ref[rows, :].astype(BF16)
            return carry

        lax.fori_loop(0, SEQ // MERGE_ROWS, merge, 0)

    def pspec(off):
        return pl.BlockSpec((SEQ, LANE), lambda p, b: (b, off // LANE + p))

    ospec = pl.BlockSpec((SEQ, LANE), lambda p, b: (b, p))
    bspec = pl.BlockSpec((nseg, 2, ATTN_BLOCK, 2 * ATTN_BLOCK), lambda p, b: (0, p, 0, 0))
    gshape = SDS((n, B_WIDTH), BF16)
    return _call(
        body, name=name, grid=(HEAD_PAIRS, nb_local),
        in_specs=[pspec(Q_OFF), pspec(K_OFF), pspec(V_OFF), ospec,
                  pl.BlockSpec((SEQ, LANE), lambda p, b: (b, a_blocks + p)), ospec,
                  pl.BlockSpec((2 * nseg, 2, ATTN_BLOCK, 2 * ATTN_BLOCK), lambda p, b: (0, p, 0, 0))],
        out_specs=[ospec, ospec, ospec, bspec],
        out_shape=[gshape, gshape, gshape, SDS((nseg, B_HEADS, ATTN_BLOCK, 2 * ATTN_BLOCK), F32)],
        scratch_shapes=[pltpu.VMEM((SEQ, LANE), F32)] * 4 + [pltpu.VMEM((QUAD, QUAD_ROWS, LANE), F32)] * 6
        + [pltpu.VMEM((2 * QUAD, QUAD_ROWS, LANE), F32)] + [pltpu.VMEM((QUAD, QUAD_ROWS, LANE), F32)] * 2
        + [pltpu.VMEM((QUAD * QUAD, ATTN_BLOCK, LANE), F32)] * 9,
        args=[proj, proj, proj, b_out, dmix, lse_tot, bias], sem=("arbitrary", "arbitrary"), ride=ride)


PAD = 8
CONV_ROWS = 64


CONV_LANES = 128


def _conv_taps(gp_ref, head_ref, r0, ls):
    g0 = gp_ref[r0:r0 + CONV_ROWS, ls]
    if r0 == 0:
        return g0, head_ref[PAD - 1:PAD - 1 + CONV_ROWS, ls], head_ref[PAD - 2:PAD - 2 + CONV_ROWS, ls]
    return g0, gp_ref[r0 - 1:r0 - 1 + CONV_ROWS, ls], gp_ref[r0 - 2:r0 - 2 + CONV_ROWS, ls]


def _fill_head(gp_ref, head_ref):
    head_ref[0:PAD, :] = jnp.zeros((PAD, CONV_LANES), F32)
    head_ref[PAD:PAD + CONV_ROWS, :] = gp_ref[0:CONV_ROWS, :]


def _lane_passes():
    return [slice(l0, l0 + LANE) for l0 in range(0, CONV_LANES, LANE)]


def conv_gelu_bwd(dgu, gp, up, cw, cb, nb_local, name, ride=None):
    n, f = gp.shape

    def fold(v):
        return jnp.sum(v.reshape(CONV_ROWS // 8, 8, LANE), axis=0)

    def body(dgu_ref, gp_ref, up_ref, cw_ref, cb_ref, dgp_ref, dup_ref, dcw_ref, dcb_ref, head_ref, dc_ref):
        b = pl.program_id(1)
        _fill_head(gp_ref, head_ref)
        dc_ref[SEQ:SEQ + PAD, :] = jnp.zeros((PAD, CONV_LANES), F32)
        for ls in _lane_passes():
            w0, w1, w2, bias = cw_ref[0:1, ls], cw_ref[1:2, ls], cw_ref[2:3, ls], cb_ref[:, ls]
            sums = [jnp.zeros((8, LANE), F32) for _ in range(4)]
            for r0 in range(0, SEQ, CONV_ROWS):
                rows = slice(r0, r0 + CONV_ROWS)
                g0, g1, g2 = _conv_taps(gp_ref, head_ref, r0, ls)
                gg, dgg = _gelu_and_grad(bias + w0 * g2 + w1 * g1 + w2 * g0)
                dgu = dgu_ref[rows, ls].astype(F32)
                dup_ref[rows, ls] = (dgu * gg).astype(BF16)
                dc = dgu * up_ref[rows, ls] * dgg
                dc_ref[rows, ls] = dc
                sums = [sums[0] + fold(dc * g2), sums[1] + fold(dc * g1), sums[2] + fold(dc * g0), sums[3] + fold(dc)]
            for r0 in range(0, SEQ, CONV_ROWS):
                dgp_ref[r0:r0 + CONV_ROWS, ls] = (
                    w2 * dc_ref[r0:r0 + CONV_ROWS, ls] + w1 * dc_ref[r0 + 1:r0 + 1 + CONV_ROWS, ls]
                    + w0 * dc_ref[r0 + 2:r0 + 2 + CONV_ROWS, ls]).astype(BF16)
            dcw = jnp.concatenate([jnp.sum(s, axis=0, keepdims=True) for s in sums[:3]], axis=0)
            dcb = jnp.sum(sums[3], axis=0, keepdims=True)

            @pl.when(b == 0)
            def _(dcw=dcw, dcb=dcb, ls=ls):
                dcw_ref[:, ls] = dcw
                dcb_ref[:, ls] = dcb

            @pl.when(b > 0)
            def _(dcw=dcw, dcb=dcb, ls=ls):
                dcw_ref[:, ls] += dcw
                dcb_ref[:, ls] += dcb

    blk = pl.BlockSpec((SEQ, CONV_LANES), lambda j, b: (b, j))
    wspec = pl.BlockSpec((3, CONV_LANES), lambda j, b: (0, j))
    bspec = pl.BlockSpec((1, CONV_LANES), lambda j, b: (0, j))
    return _call(
        body, name=name, grid=(f // CONV_LANES, nb_local),
        in_specs=[blk, blk, blk, wspec, bspec], out_specs=[blk, blk, wspec, bspec],
        out_shape=[SDS((n, f), BF16), SDS((n, f), BF16), SDS((3, f), F32), SDS((1, f), F32)],
        scratch_shapes=[pltpu.VMEM((PAD + CONV_ROWS, CONV_LANES), F32), pltpu.VMEM((SEQ + PAD, CONV_LANES), F32)],
        args=[dgu, gp, up, cw, cb], sem=("parallel", "arbitrary"), ride=ride)


def norm_mid_epilogue(x1, dout, z2, g3, g2):
    n, d = x1.shape

    def fn(dh2, step, x1_ref, dout_ref, z2_ref, g3_ref, g2_ref, dx1_ref, dz2_ref, dg3_ref, dg2_ref):
        dxa, dg3r = _rms_bwd(dh2, x1_ref[...], g3_ref[...])
        dx1 = dout_ref[...] + dxa
        dx1_ref[...] = dx1
        dz2, dg2r = _rms_bwd(dx1, z2_ref[...], g2_ref[...])
        dz2_ref[...] = dz2.astype(BF16)
        _accumulate(dg3_ref, jnp.sum(dg3r, axis=0, keepdims=True), step)
        _accumulate(dg2_ref, jnp.sum(dg2r, axis=0, keepdims=True), step)

    return fn, [x1, dout, z2, g3, g2], [SDS((n, d), F32), SDS((n, d), BF16), SDS((1, d), F32), SDS((1, d), F32)]


def norm_in_epilogue(x, dx1, g1):
    n, d = x.shape

    def fn(dh1, step, x_ref, dx1_ref, g1_ref, dx_ref, dg1_ref):
        dxa, dgr = _rms_bwd(dh1, x_ref[...], g1_ref[...])
        dx_ref[...] = dx1_ref[...] + dxa
        _accumulate(dg1_ref, jnp.sum(dgr, axis=0, keepdims=True), step)

    return fn, [x, dx1, g1], [SDS((n, d), F32), SDS((1, d), F32)]


def cast_bf16(arrays, name):
    def body(*refs):
        for i_ref, o_ref in zip(refs[:len(arrays)], refs[len(arrays):]):
            o_ref[...] = i_ref[...].astype(BF16)

    return pl.pallas_call(body, name=name, out_shape=[SDS(a.shape, BF16) for a in arrays],
                          compiler_params=_params())(*arrays)


def adam_update(parts, w, m, v, name, tr=None):
    s, r, c = parts.shape
    tr = r if tr is None else tr
    bc1 = 1.0 - ADAM_B1 ** ADAM_STEP
    bc2 = 1.0 - ADAM_B2 ** ADAM_STEP

    def body(p_ref, w_ref, m_ref, v_ref, g_ref, d_ref, nm_ref, nv_ref):
        g = p_ref[0].astype(F32)
        for j in range(1, s):
            g = g + p_ref[j].astype(F32)
        nm = ADAM_B1 * m_ref[...] + (1.0 - ADAM_B1) * g
        nv = ADAM_B2 * v_ref[...] + (1.0 - ADAM_B2) * (g * g)
        g_ref[...] = g
        nm_ref[...] = nm
        nv_ref[...] = nv
        d_ref[...] = -ADAM_LR * ((nm / bc1) / (jnp.sqrt(nv / bc2) + ADAM_EPS) + ADAM_WD * w_ref[...])

    blk = pl.BlockSpec((tr, c), lambda i: (i, 0))
    return pl.pallas_call(
        body, name=name, grid=(r // tr,),
        in_specs=[pl.BlockSpec((s, tr, c), lambda i: (0, i, 0)), blk, blk, blk],
        out_specs=[blk] * 4, out_shape=[SDS((r, c), F32)] * 4,
        compiler_params=_params(("parallel",)),
    )(parts, w, m, v)


EARLY_NAMES = ("spatial_w", "norm_mix_post", "norm_ffn_pre", "norm_ffn_post", "conv_b", "ln_v_gain", "ln_v_bias",
               "spatial_b")
LATE_NAMES = ("norm_mix_pre", "rel_bias")
PACK_ROW_ALIGN = 8


def _pack_rows(size):
    rows = -(-size // LANE)
    return -(-rows // PACK_ROW_ALIGN) * PACK_ROW_ALIGN


def _pack(arrays):
    flat = []
    for a in arrays:
        rows = _pack_rows(a.size)
        flat.append(jnp.pad(a.reshape(-1), (0, rows * LANE - a.size)))
    return jnp.concatenate(flat).reshape(-1, LANE)


def _unpack(packed, shapes):
    out, row = [], 0
    for shp in shapes:
        size = int(np.prod(shp))
        out.append(packed[row:row + _pack_rows(size)].reshape(-1)[:size].reshape(shp))
        row += _pack_rows(size)
    return out


def kernel(x, norm_mix_pre, norm_mix_post, norm_ffn_pre, norm_ffn_post, w_in, ln_v_gain, ln_v_bias, spatial_w, spatial_b, rel_bias, w_out, w_gate, w_up, conv_w, conv_b, w_down, loss_target, m_norm_mix_pre, m_norm_mix_post, m_norm_ffn_pre, m_norm_ffn_post, m_w_in, m_ln_v_gain, m_ln_v_bias, m_spatial_w, m_spatial_b, m_rel_bias, m_w_out, m_w_gate, m_w_up, m_conv_w, m_conv_b, m_w_down, v_norm_mix_pre, v_norm_mix_post, v_norm_ffn_pre, v_norm_ffn_post, v_w_in, v_ln_v_gain, v_ln_v_bias, v_spatial_w, v_spatial_b, v_rel_bias, v_w_out, v_w_gate, v_w_up, v_conv_w, v_conv_b, v_w_down):
    given = dict(locals())
    nb_local, seq, d = x.shape
    n = nb_local * seq
    cols = w_in.shape[2]

    def by_columns(g):
        return g.transpose(1, 0, 2).reshape(g.shape[1], N_DEV * g.shape[2])

    def by_rows(g):
        return g.reshape(N_DEV * g.shape[1], g.shape[2])

    def blocks(g):
        return g.reshape(N_DEV, g.shape[0] // N_DEV, g.shape[1])

    xf, target = x.reshape(n, d), loss_target.reshape(n, d)
    ln_g, ln_b = ln_v_gain.reshape(1, A_WIDTH), ln_v_bias.reshape(1, A_WIDTH)
    spatial_bt, rel_bias_t = spatial_b[0].T, rel_bias.T

    s_in, s_out, s_gate, s_up, s_down = cast_bf16(
        [w_in[0].T, w_out[0], w_gate[0].T, w_up[0].T, w_down[0]], "cast_shards")
    (bias,), (g_in, g_cw) = bias_tables(rel_bias_t, "bias_tables", ride=([], [s_in, conv_w[0]]))
    bias = bias.reshape(2 * len(DILATIONS), B_HEADS, ATTN_BLOCK, 2 * ATTN_BLOCK)
    w_in_t, conv_w_f = by_rows(g_in), by_columns(g_cw)

    (h1, proj), _ = norm_mm(xf, norm_mix_pre, [w_in_t], "fwd_norm_in", tn=IN_COLS)
    a = gating_fwd(proj, ln_g, ln_b, spatial_w[0], spatial_bt, "fwd_gating")
    (b_out, lse_tot), (g_out, g_gate, g_up) = attn_fwd(proj, bias, nb_local, "fwd_attn",
                                                       ride=([], [s_out, s_gate, s_up]))
    w_out_f, w_gate_t, w_up_t = by_rows(g_out), by_rows(g_gate), by_rows(g_up)
    z2, x1 = mm_res_norm([a, b_out], w_out_f, xf, norm_mix_post, "fwd_out_norm")
    (h2, gp, up, gu), (g_down,) = norm_mm(x1, norm_ffn_pre, [w_gate_t, w_up_t], "fwd_norm_ffn_conv", tm=256, tn=D_FF,
                                          ride=([], [s_down]), conv=(conv_w_f, conv_b))
    w_down_f = by_rows(g_down)
    dy, dout, dg4, loss_part = down_loss(gu, w_down_f, x1, norm_ffn_post, target, "fwd_down_loss")

    p_down = mm_tn([gu], [dy], "bwd_dw_down", t1=256, t2=D_MODEL)
    (dgu,), _ = mm_nt([(dy, 0, 0)], [w_down_f], "bwd_dgu", out_dtype=BF16)
    (dgp, dup, p_conv_w, p_conv_b), (r_down,) = conv_gelu_bwd(
        dgu, gp, up, conv_w_f, conv_b, nb_local, "bwd_conv_gelu", ride=([blocks(p_down)], []))
    p_gate = mm_tn([dgp], [h2], "bwd_dw_gate", t1=256, t2=D_MODEL)
    p_up = mm_tn([dup], [h2], "bwd_dw_up", t1=256, t2=D_MODEL)
    (dx1, dz2, dg3, dg2), _ = mm_nt([(dgp, 0, 0), (dup, 1, 0)], [w_gate_t, w_up_t], "bwd_dh2_norm_mid", tm=256,
                                    by_rows=True,
                                    epilogue=norm_mid_epilogue(x1, dout, z2, norm_ffn_pre, norm_mix_post))
    p_out = mm_tn([a, b_out], [dz2], "bwd_dw_out", t1=256, t2=D_MODEL)
    (dmix,), _ = mm_nt([(dz2, 0, 0)], [w_out_f], "bwd_dmix")
    duv, p_ws, p_sbt, p_lng, p_lnb = gating_bwd(proj, dmix, ln_g, ln_b, spatial_w[0], spatial_bt, "bwd_gating")
    small = dict(spatial_w=p_ws, norm_mix_post=dg2, norm_ffn_pre=dg3, norm_ffn_post=dg4, conv_b=p_conv_b,
                 ln_v_gain=p_lng, ln_v_bias=p_lnb, spatial_b=p_sbt.T)
    pack_early = _pack([small[k] for k in EARLY_NAMES] + [p_conv_w, loss_part])
    (dq, dk, dv, dbias), (r_gate, r_up, r_out, r_early) = attn_bwd(
        proj, b_out, dmix, lse_tot, bias, nb_local, "bwd_attn",
        ride=([blocks(p_gate), blocks(p_up), blocks(p_out)], [pack_early]))
    p_rel_bias_t = rel_bias_grad(dbias.reshape(len(DILATIONS), B_HEADS, BIAS_SIZE), "bwd_rel_bias")
    p_in = mm_tn([duv, dq, dk, dv], [h1], "bwd_dw_in", t1=256, t2=D_MODEL)
    (grad_x, dg1), (r_in,) = mm_nt(
        [(duv, 0, 0), (dq, 0, Q_OFF), (dk, 0, K_OFF), (dv, 0, V_OFF)], [w_in_t], "bwd_dh1_norm_in", by_rows=True,
        epilogue=norm_in_epilogue(xf, dx1, norm_mix_pre), ride=([blocks(p_in)], []))
    small.update(norm_mix_pre=dg1, rel_bias=p_rel_bias_t.T)
    (r_late,) = exchange([], [_pack([small[k] for k in LATE_NAMES])], "exchange_late")

    res = {}
    for k, received in (("w_in", r_in), ("w_gate", r_gate), ("w_up", r_up)):
        res[k] = [o.T for o in adam_update(received, given[k][0].T, given["m_" + k][0].T, given["v_" + k][0].T,
                                           "adam_" + k, tr=cols // 2)]
    res["w_out"] = adam_update(r_out, w_out[0], m_w_out[0], v_w_out[0], "adam_w_out")
    res["w_down"] = adam_update(r_down, w_down[0], m_w_down[0], v_w_down[0], "adam_w_down", tr=cols // 2)

    def adam_packed(received, names, tail, name):
        zeros = [jnp.zeros_like(t) for t in tail]
        packs = [_pack([given[pre + k] for k in names] + zeros) for pre in ("", "m_", "v_")]
        shapes = [given[k].shape for k in names] + [t.shape for t in tail]
        unpacked = [_unpack(p, shapes) for p in adam_update(received, *packs, name)]
        for i, k in enumerate(names):
            res[k] = [u[i] for u in unpacked]
        return unpacked[0][len(names):]

    g_conv_w_full, loss_sum = adam_packed(r_early, EARLY_NAMES, [p_conv_w, loss_part], "adam_small_early")
    adam_packed(r_late, LATE_NAMES, [], "adam_small_late")
    g_conv_w = lax.dynamic_slice_in_dim(g_conv_w_full, _my_index() * cols, cols, axis=1)
    res["conv_w"] = adam_update(g_conv_w[None], conv_w[0], m_conv_w[0], v_conv_w[0], "adam_conv_w")
    loss = loss_sum[0, 0]

    names = ("norm_mix_pre", "norm_mix_post", "norm_ffn_pre", "norm_ffn_post", "w_in", "ln_v_gain", "ln_v_bias",
             "spatial_w", "spatial_b", "rel_bias", "w_out", "w_gate", "w_up", "conv_w", "conv_b", "w_down")
    outs = [loss, grad_x.reshape(x.shape)]
    for t in range(4):
        outs += [res[k][t].reshape(given[k].shape) for k in names]
    return tuple(outs)
```

```python
import functools
import math

import numpy as np
import jax
import jax.numpy as jnp
from jax import lax
from jax.experimental import pallas as pl
from jax.experimental.pallas import tpu as pltpu

F32 = jnp.float32
BF16 = jnp.bfloat16
SDS = jax.ShapeDtypeStruct

D_MODEL = 1024
SEQ = 2048
HEAD_DIM = 64
A_GROUPS = 4
A_WIDTH = A_GROUPS * HEAD_DIM
B_HEADS = 12
B_WIDTH = B_HEADS * HEAD_DIM
HEAD_PAIRS = B_HEADS // 2
CHUNK = 128
ATTN_BLOCK = 128
DILATIONS = (1, 4, 16)
NUM_BUCKETS = 32
MAX_DISTANCE = 2048
D_FF = 2816
IN_COLS = 2 * A_WIDTH + 3 * B_WIDTH
Q_OFF = 2 * A_WIDTH
K_OFF = Q_OFF + B_WIDTH
V_OFF = K_OFF + B_WIDTH
NORM_EPS = 1e-6
NEG_INF = -1e30
N_DEV = 8
LANE = 128

ADAM_LR = 0.001
ADAM_B1 = 0.9
ADAM_B2 = 0.999
ADAM_EPS = 1e-08
ADAM_WD = 0.01
ADAM_STEP = 10

GELU_C0 = math.sqrt(2.0 / math.pi)
GELU_C1 = 0.044715

VMEM_LIMIT = 56 * 1024 * 1024


def _params(sem=None):
    if sem is None:
        return pltpu.CompilerParams(vmem_limit_bytes=VMEM_LIMIT)
    return pltpu.CompilerParams(dimension_semantics=sem, vmem_limit_bytes=VMEM_LIMIT)


def _gelu(x):
    t = jnp.tanh(x * (GELU_C0 + (GELU_C0 * GELU_C1) * (x * x)))
    return x * (0.5 + 0.5 * t)


def _gelu_and_grad(x):
    x2 = x * x
    t = jnp.tanh(x * (GELU_C0 + (GELU_C0 * GELU_C1) * x2))
    half = 0.5 + 0.5 * t
    dg = half + x * (0.5 - 0.5 * (t * t)) * (GELU_C0 + (3.0 * GELU_C0 * GELU_C1) * x2)
    return x * half, dg


def _dot(a, b):
    return jnp.dot(a, b, preferred_element_type=F32)


def _dot_nt(a, b):
    return lax.dot_general(a, b, (((1,), (1,)), ((), ())), preferred_element_type=F32)


def _dot_tn(a, b):
    return lax.dot_general(a, b, (((0,), (0,)), ((), ())), preferred_element_type=F32)


def _rms_bwd(d, xin, g):
    r = lax.rsqrt(jnp.mean(xin * xin, axis=-1, keepdims=True) + NORM_EPS)
    xh = xin * r
    gd = g * d
    dx = r * (gd - xh * jnp.mean(gd * xh, axis=-1, keepdims=True))
    return dx, d * xh


MESH = pl.DeviceIdType.MESH
ANY = pl.BlockSpec(memory_space=pl.ANY)
PEER_MASKS = tuple(range(1, N_DEV))


def _my_index():
    return lax.axis_index("x") * 4 + lax.axis_index("y") * 2 + lax.axis_index("c")


def _peer(mask):
    x, y, c = lax.axis_index("x"), lax.axis_index("y"), lax.axis_index("c")
    px = 1 - x if mask & 4 else x
    py = 1 - y if mask & 2 else y
    pc = 1 - c if mask & 1 else c
    return (px, py, pc), px * 4 + py * 2 + pc


RELAY_AT = 3
SIBLING = 1
CHIP_MASKS = (2, 4, 6)


class _Exchange:
    def __init__(self, nblocked, in_refs, out_refs, sems):
        send_sems, recv_sems, local_sems = sems
        me = _my_index()
        sibling, _ = _peer(SIBLING)
        self.local, self.first, self.relays, self.relayed_in, self.last_in = [], [], [], [], []
        for a, (in_ref, out_ref) in enumerate(zip(in_refs, out_refs)):
            def copy(src, slot, mask, to):
                return pltpu.make_async_remote_copy(
                    src_ref=src, dst_ref=out_ref.at[slot], send_sem=send_sems.at[a, mask - 1],
                    recv_sem=recv_sems.at[a, mask - 1], device_id=to, device_id_type=MESH)

            if a < nblocked:
                self.local.append(pltpu.make_async_copy(in_ref.at[me], out_ref.at[me], local_sems.at[a]))
                for mask in PEER_MASKS:
                    peer, pidx = _peer(mask)
                    self.first.append(copy(in_ref.at[pidx], me, mask, peer))
                    self.last_in.append(copy(in_ref.at[pidx], pidx, mask, peer))
                continue
            self.local.append(pltpu.make_async_copy(in_ref, out_ref.at[me], local_sems.at[a]))
            for mask in (SIBLING,) + CHIP_MASKS:
                peer, pidx = _peer(mask)
                self.first.append(copy(in_ref, me, mask, peer))
                (self.last_in if mask == SIBLING else self.relayed_in).append(copy(in_ref, pidx, mask, peer))
            for mask in CHIP_MASKS:
                _, origin = _peer(mask)
                _, far = _peer(mask | SIBLING)
                self.relays.append(copy(out_ref.at[origin], origin, mask | SIBLING, sibling))
                self.last_in.append(copy(in_ref, far, mask | SIBLING, sibling))

    def start(self):
        for cp in self.local + self.first[::-1]:
            cp.start()

    def relay(self):
        for arrived, onward in zip(self.relayed_in, self.relays):
            arrived.wait_recv()
            onward.start()

    def finish(self):
        for cp in self.first + self.relays:
            cp.wait_send()
        for cp in self.last_in:
            cp.wait_recv()
        for cp in self.local:
            cp.wait()


def _exchange_out_shape(blocked, whole):
    return [SDS(b.shape, b.dtype) for b in blocked] + [SDS((N_DEV,) + w.shape, w.dtype) for w in whole]


def _exchange_sems(n):
    return [pltpu.SemaphoreType.DMA((n, N_DEV - 1)), pltpu.SemaphoreType.DMA((n, N_DEV - 1)),
            pltpu.SemaphoreType.DMA((n,))]


def exchange(blocked, whole, name):
    nb, n = len(blocked), len(blocked) + len(whole)

    def body(*refs):
        ex = _Exchange(nb, refs[:n], refs[n:2 * n], refs[2 * n:])
        ex.start()
        ex.relay()
        ex.finish()

    return pl.pallas_call(
        body, name=name, in_specs=[ANY] * n, out_specs=[ANY] * n, out_shape=_exchange_out_shape(blocked, whole),
        scratch_shapes=_exchange_sems(n),
    )(*blocked, *whole)


def _call(body, *, name, grid, in_specs, out_specs, out_shape, args, scratch_shapes=(), sem=None, ride=None):
    out_shape, out_specs, scratch_shapes = list(out_shape), list(out_specs), list(scratch_shapes)
    if ride is None:
        outs = pl.pallas_call(body, name=name, grid=grid, in_specs=list(in_specs), out_specs=out_specs,
                              out_shape=out_shape, scratch_shapes=scratch_shapes,
                              compiler_params=_params(sem))(*args)
        return list(outs), []
    blocked, whole = ride
    cargs = list(blocked) + list(whole)
    nb, nc = len(blocked), len(cargs)
    n_in, n_out, n_scr = len(args), len(out_shape), len(scratch_shapes)
    steps = math.prod(grid)
    assert steps >= 3, grid

    def riding(*refs):
        ins, refs = refs[:n_in], refs[n_in:]
        cins, refs = refs[:nc], refs[nc:]
        outs, refs = refs[:n_out], refs[n_out:]
        couts, refs = refs[:nc], refs[nc:]
        scr, sems = refs[:n_scr], refs[n_scr:]
        step = functools.reduce(lambda acc, k: acc * grid[k] + pl.program_id(k), range(len(grid)), 0)

        @pl.when(step == 0)
        def _():
            _Exchange(nb, cins, couts, sems).start()

        @pl.when(step == RELAY_AT * steps // 4)
        def _():
            _Exchange(nb, cins, couts, sems).relay()

        body(*ins, *outs, *scr)

        @pl.when(step == steps - 1)
        def _():
            _Exchange(nb, cins, couts, sems).finish()

    res = pl.pallas_call(
        riding, name=name, grid=grid, in_specs=list(in_specs) + [ANY] * nc, out_specs=out_specs + [ANY] * nc,
        out_shape=out_shape + _exchange_out_shape(blocked, whole),
        scratch_shapes=scratch_shapes + _exchange_sems(nc),
        compiler_params=_params(("arbitrary",) * len(grid)))(*args, *cargs)
    return list(res[:n_out]), list(res[n_out:])


def norm_mm(x, g, ws, name, tm=512, tn=1408, ride=None, conv=None):
    n, d = x.shape
    f = ws[0].shape[0]
    nw = len(ws)
    extra_in, extra_spec, extra_out, extra_out_spec, scratch = [], [], [], [], []
    if conv is not None:
        assert nw == 2 and tn == f and SEQ % tm == 0 and tm % CONV_ROWS == 0
        extra_in = list(conv)
        extra_spec = [pl.BlockSpec((3, f), lambda i, j: (0, 0)), pl.BlockSpec((1, f), lambda i, j: (0, 0))]
        extra_out, extra_out_spec = [SDS((n, f), BF16)], [pl.BlockSpec((tm, f), lambda i, j: (i, 0))]
        scratch = [pltpu.VMEM((PAD + CONV_ROWS, f), F32), pltpu.VMEM((PAD, f), F32)]

    def body(x_ref, g_ref, *refs):
        w_refs, refs = refs[:nw], refs[nw:]
        conv_refs, refs = refs[:len(extra_in)], refs[len(extra_in):]
        h_ref, o_refs, refs = refs[0], refs[1:1 + nw], refs[1 + nw:]

        @pl.when(pl.program_id(1) == 0)
        def _():
            xv = x_ref[...]
            r = lax.rsqrt(jnp.mean(xv * xv, axis=-1, keepdims=True) + NORM_EPS)
            h_ref[...] = (xv * r * g_ref[...]).astype(BF16)

        h = h_ref[...]
        for w_ref, o_ref in zip(w_refs, o_refs):
            o_ref[...] = _dot_nt(h, w_ref[...])
        if conv is None:
            return
        (cw_ref, cb_ref), (gp_ref, up_ref), (gu_ref, head_ref, carry_ref) = conv_refs, o_refs, refs

        @pl.when(pl.program_id(0) % (SEQ // tm) == 0)
        def _():
            carry_ref[...] = jnp.zeros_like(carry_ref)

        head_ref[0:PAD, :] = carry_ref[...]
        head_ref[PAD:PAD + CONV_ROWS, :] = gp_ref[0:CONV_ROWS, :]
        for l0 in range(0, f, LANE):
            ls = slice(l0, l0 + LANE)
            w0, w1, w2, bias = cw_ref[0:1, ls], cw_ref[1:2, ls], cw_ref[2:3, ls], cb_ref[:, ls]
            for r0 in range(0, tm, CONV_ROWS):
                g0, g1, g2 = _conv_taps(gp_ref, head_ref, r0, ls)
                c = bias + w0 * g2 + w1 * g1 + w2 * g0
                gu_ref[r0:r0 + CONV_ROWS, ls] = (_gelu(c) * up_ref[r0:r0 + CONV_ROWS, ls]).astype(BF16)
        carry_ref[...] = gp_ref[tm - PAD:tm, :]

    return _call(
        body, name=name, grid=(n // tm, f // tn),
        in_specs=[pl.BlockSpec((tm, d), lambda i, j: (i, 0)), pl.BlockSpec((1, d), lambda i, j: (0, 0))]
        + [pl.BlockSpec((tn, d), lambda i, j: (j, 0)) for _ in ws] + extra_spec,
        out_specs=[pl.BlockSpec((tm, d), lambda i, j: (i, 0))]
        + [pl.BlockSpec((tm, tn), lambda i, j: (i, j)) for _ in ws] + extra_out_spec,
        out_shape=[SDS((n, d), BF16)] + [SDS((n, f), F32) for _ in ws] + extra_out,
        scratch_shapes=scratch,
        args=[x, g, *ws, *extra_in], sem=("parallel" if conv is None else "arbitrary", "arbitrary"), ride=ride)


def _lane_concat(refs):
    vals = [r[...].astype(BF16) for r in refs]
    return vals[0] if len(vals) == 1 else jnp.concatenate(vals, axis=1)


def mm_res_norm(a_list, w, res, g, name, tm=512):
    n = a_list[0].shape[0]
    k, d = w.shape
    na = len(a_list)

    def body(*refs):
        w_ref, res_ref, g_ref, y_ref, o_ref = refs[na:]
        y = _dot(_lane_concat(refs[:na]), w_ref[...])
        r = lax.rsqrt(jnp.mean(y * y, axis=-1, keepdims=True) + NORM_EPS)
        y_ref[...] = y
        o_ref[...] = res_ref[...] + y * r * g_ref[...]

    return pl.pallas_call(
        body, name=name, grid=(n // tm,),
        in_specs=[pl.BlockSpec((tm, a.shape[1]), lambda i: (i, 0)) for a in a_list]
        + [pl.BlockSpec((k, d), lambda i: (0, 0)),
           pl.BlockSpec((tm, d), lambda i: (i, 0)), pl.BlockSpec((1, d), lambda i: (0, 0))],
        out_specs=[pl.BlockSpec((tm, d), lambda i: (i, 0)), pl.BlockSpec((tm, d), lambda i: (i, 0))],
        out_shape=[SDS((n, d), F32), SDS((n, d), F32)],
        compiler_params=_params(("parallel",)),
    )(*a_list, w, res, g)


def down_loss(a, w, res, g, target, name, tm=256):
    n, k = a.shape
    d = w.shape[1]
    inv_d = 1.0 / d

    def body(a_ref, w_ref, res_ref, g_ref, t_ref, dy_ref, dout_ref, dg_ref, loss_ref):
        i = pl.program_id(0)
        y = _dot(a_ref[...], w_ref[...])
        gv = g_ref[...]
        r = lax.rsqrt(jnp.mean(y * y, axis=-1, keepdims=True) + NORM_EPS)
        yh = y * r
        e = res_ref[...] + yh * gv - t_ref[...]
        part = 0.5 * inv_d * jnp.sum(jnp.sum(e * e, axis=-1, keepdims=True), axis=0, keepdims=True)
        dout = e * inv_d
        dout_ref[...] = dout
        gd = gv * dout
        dy_ref[...] = (r * (gd - yh * jnp.mean(gd * yh, axis=-1, keepdims=True))).astype(BF16)
        dgp = jnp.sum(dout * yh, axis=0, keepdims=True)
        lane0 = lax.broadcasted_iota(jnp.int32, (1, LANE), 1) == 0
        lp = jnp.where(lane0, part, 0.0)

        @pl.when(i == 0)
        def _():
            dg_ref[...] = dgp
            loss_ref[...] = lp

        @pl.when(i > 0)
        def _():
            dg_ref[...] += dgp
            loss_ref[...] += lp

    return pl.pallas_call(
        body, name=name, grid=(n // tm,),
        in_specs=[pl.BlockSpec((tm, k), lambda i: (i, 0)), pl.BlockSpec((k, d), lambda i: (0, 0)),
                  pl.BlockSpec((tm, d), lambda i: (i, 0)), pl.BlockSpec((1, d), lambda i: (0, 0)),
                  pl.BlockSpec((tm, d), lambda i: (i, 0))],
        out_specs=[pl.BlockSpec((tm, d), lambda i: (i, 0)), pl.BlockSpec((tm, d), lambda i: (i, 0)),
                   pl.BlockSpec((1, d), lambda i: (0, 0)), pl.BlockSpec((1, LANE), lambda i: (0, 0))],
        out_shape=[SDS((n, d), BF16), SDS((n, d), F32), SDS((1, d), F32), SDS((1, LANE), F32)],
        compiler_params=_params(("arbitrary",)),
    )(a, w, res, g, target)


def _accumulate(ref, val, step):
    @pl.when(step == 0)
    def _():
        ref[...] = val

    @pl.when(step > 0)
    def _():
        ref[...] += val


def mm_nt(terms, ws, name, tm=512, out_dtype=F32, ride=None, epilogue=None, by_rows=False):
    n = terms[0][0].shape[0]
    r = ws[0].shape[1 if by_rows else 0]
    na = len(terms)
    meta = [(widx, off, a.shape[1]) for a, widx, off in terms]
    fn, extras, out_shape = epilogue if epilogue else (None, [], [SDS((n, r), out_dtype)])
    n_fixed = na + len(ws)

    def body(*refs):
        a_refs = refs[:na]
        w_refs = refs[na:n_fixed]
        acc = None
        for a_ref, (widx, off, k) in zip(a_refs, meta):
            a = a_ref[...].astype(BF16)
            p = _dot(a, w_refs[widx][off:off + k, :]) if by_rows else _dot_nt(a, w_refs[widx][:, off:off + k])
            acc = p if acc is None else acc + p
        if fn is None:
            refs[-1][...] = acc.astype(out_dtype)
        else:
            fn(acc, pl.program_id(0), *refs[n_fixed:])

    def spec(a):
        if a.shape[0] == 1:
            return pl.BlockSpec(a.shape, lambda i: (0, 0))
        return pl.BlockSpec((tm, a.shape[1]), lambda i: (i, 0))

    return _call(
        body, name=name, grid=(n // tm,),
        in_specs=[spec(a) for a, _, _ in terms] + [pl.BlockSpec(w.shape, lambda i: (0, 0)) for w in ws]
        + [spec(e) for e in extras],
        out_specs=[spec(o) for o in out_shape], out_shape=out_shape,
        args=[a for a, _, _ in terms] + list(ws) + list(extras),
        sem=("parallel",) if fn is None else ("arbitrary",), ride=ride)


def _piece_blocks(pieces, tile):
    out, first = [], 0
    for p in pieces:
        nblk, rem = divmod(p.shape[1], tile)
        assert rem == 0, (p.shape, tile)
        out.append((first, nblk))
        first += nblk
    return out, first


def mm_tn(lhs_list, rhs_list, name, t1, t2, out_dtype=BF16):
    n = lhs_list[0].shape[0]
    lblocks, nbl = _piece_blocks(lhs_list, t1)
    rblocks, nbr = _piece_blocks(rhs_list, t2)
    nl = len(lhs_list)

    def body(*refs):
        l_refs, r_refs, o_ref = refs[:nl], refs[nl:-1], refs[-1]
        i, j = pl.program_id(0), pl.program_id(1)
        for l_ref, (ls, ln) in zip(l_refs, lblocks):
            for r_ref, (rs, rn) in zip(r_refs, rblocks):
                @pl.when((i >= ls) & (i < ls + ln) & (j >= rs) & (j < rs + rn))
                def _(l_ref=l_ref, r_ref=r_ref):
                    o_ref[...] = _dot_tn(l_ref[...].astype(BF16), r_ref[...].astype(BF16)).astype(out_dtype)

    def piece_spec(tile, axis, first, nblk):
        def index(i, j):
            return 0, jnp.clip((i, j)[axis] - first, 0, nblk - 1)
        return pl.BlockSpec((n, tile), index)

    return pl.pallas_call(
        body, name=name, grid=(nbl, nbr),
        in_specs=[piece_spec(t1, 0, *b) for b in lblocks] + [piece_spec(t2, 1, *b) for b in rblocks],
        out_specs=pl.BlockSpec((t1, t2), lambda i, j: (i, j)),
        out_shape=SDS((nbl * t1, nbr * t2), out_dtype),
        compiler_params=_params(("parallel", "arbitrary")),
    )(*lhs_list, *rhs_list)


GATE_ROWS = 512


def _tril_mask():
    row = lax.broadcasted_iota(jnp.int32, (CHUNK, CHUNK), 0)
    col = lax.broadcasted_iota(jnp.int32, (CHUNK, CHUNK), 1)
    return row >= col


def _group_of(shape, axis):
    return lax.broadcasted_iota(jnp.int32, shape, axis) // HEAD_DIM


def _group_mean_matrix():
    same = _group_of((A_WIDTH, A_WIDTH), 0) == _group_of((A_WIDTH, A_WIDTH), 1)
    return jnp.where(same, 1.0 / HEAD_DIM, 0.0).astype(BF16)


def _dot_sum(a, b):
    hi = a.astype(BF16)
    lo = (a - hi.astype(F32)).astype(BF16)
    return _dot(hi, b) + _dot(lo, b)


def _by_group(parts, lane_group):
    out = parts[A_GROUPS - 1]
    for g in range(A_GROUPS - 2, -1, -1):
        out = jnp.where(lane_group == g, parts[g], out)
    return out


def _group_norm(gv, gmean):
    xc = gv - _dot_sum(gv, gmean)
    rstd = lax.rsqrt(_dot_sum(xc * xc, gmean) + NORM_EPS)
    return xc * rstd, rstd


def gating_fwd(proj, lng, lnb, ws, sbt, name):
    n = proj.shape[0]

    def body(u_ref, v_ref, lng_ref, lnb_ref, ws_ref, sbt_ref, a_ref):
        tril = _tril_mask()
        lane_group = _group_of((CHUNK, A_WIDTH), 1)
        gmean = _group_mean_matrix()
        wts = [jnp.where(tril, ws_ref[g], 0.0).astype(BF16) for g in range(A_GROUPS)]
        sb = _by_group([sbt_ref[:, g:g + 1] for g in range(A_GROUPS)], lane_group)

        def chunk(c, carry):
            rows = pl.ds(pl.multiple_of(c * CHUNK, CHUNK), CHUNK)
            vhat, _ = _group_norm(_gelu(v_ref[rows, :]), gmean)
            vn = (vhat * lng_ref[...] + lnb_ref[...]).astype(BF16)
            z = _by_group([_dot(wt, vn) for wt in wts], lane_group) + sb
            a_ref[rows, :] = _gelu(u_ref[rows, :]) * z
            return carry

        lax.fori_loop(0, GATE_ROWS // CHUNK, chunk, 0)

    return pl.pallas_call(
        body, name=name, grid=(n // GATE_ROWS,),
        in_specs=[pl.BlockSpec((GATE_ROWS, A_WIDTH), lambda i: (i, 0)),
                  pl.BlockSpec((GATE_ROWS, A_WIDTH), lambda i: (i, 1)),
                  pl.BlockSpec((1, A_WIDTH), lambda i: (0, 0)), pl.BlockSpec((1, A_WIDTH), lambda i: (0, 0)),
                  pl.BlockSpec((A_GROUPS, CHUNK, CHUNK), lambda i: (0, 0, 0)),
                  pl.BlockSpec((CHUNK, A_GROUPS), lambda i: (0, 0))],
        out_specs=pl.BlockSpec((GATE_ROWS, A_WIDTH), lambda i: (i, 0)),
        out_shape=SDS((n, A_WIDTH), F32),
        compiler_params=_params(("parallel",)),
    )(proj, proj, lng, lnb, ws, sbt)


def gating_bwd(proj, dmix, lng, lnb, ws, sbt, name):
    n = proj.shape[0]

    def body(u_ref, v_ref, da_ref, lng_ref, lnb_ref, ws_ref, sbt_ref,
             duv_ref, dws_ref, dsbt_ref, dlng_ref, dlnb_ref):
        @pl.when(pl.program_id(0) == 0)
        def _():
            dws_ref[...] = jnp.zeros_like(dws_ref)
            dsbt_ref[...] = jnp.zeros_like(dsbt_ref)
            dlng_ref[...] = jnp.zeros_like(dlng_ref)
            dlnb_ref[...] = jnp.zeros_like(dlnb_ref)

        tril = _tril_mask()
        lane_group = _group_of((CHUNK, A_WIDTH), 1)
        gmean = _group_mean_matrix()
        gsum = (_group_of((A_WIDTH, LANE), 0) == lax.broadcasted_iota(jnp.int32, (A_WIDTH, LANE), 1)).astype(BF16)
        wts = [jnp.where(tril, ws_ref[g], 0.0) for g in range(A_GROUPS)]
        wts_b = [w.astype(BF16) for w in wts]
        wts_t = [w.T.astype(BF16) for w in wts]
        sb = _by_group([sbt_ref[:, g:g + 1] for g in range(A_GROUPS)], lane_group)
        lg = lng_ref[...]

        def chunk(c, carry):
            rows = pl.ds(pl.multiple_of(c * CHUNK, CHUNK), CHUNK)
            gu, dgu_dx = _gelu_and_grad(u_ref[rows, :])
            gv, dgv_dx = _gelu_and_grad(v_ref[rows, :])
            vhat, rstd = _group_norm(gv, gmean)
            vn = (vhat * lg + lnb_ref[...]).astype(BF16)
            z = _by_group([_dot(wt, vn) for wt in wts_b], lane_group) + sb
            da = da_ref[rows, :]
            dz = da * gu
            dzb = dz.astype(BF16)
            duv_ref[rows, 0:A_WIDTH] = (da * z * dgu_dx).astype(BF16)
            dsbt_ref[...] += _dot_sum(dz, gsum)[:, 0:A_GROUPS]
            for g in range(A_GROUPS):
                dz_g = jnp.where(lane_group == g, dzb, jnp.zeros_like(dzb))
                dws_ref[g] += jnp.where(tril, _dot_nt(dz_g, vn), 0.0)
            dvn = _by_group([_dot(wt, dzb) for wt in wts_t], lane_group)
            dlng_ref[...] += jnp.sum(dvn * vhat, axis=0, keepdims=True)
            dlnb_ref[...] += jnp.sum(dvn, axis=0, keepdims=True)
            dvh = dvn * lg
            dgv = rstd * (dvh - _dot_sum(dvh, gmean) - vhat * _dot_sum(dvh * vhat, gmean))
            duv_ref[rows, A_WIDTH:2 * A_WIDTH] = (dgv * dgv_dx).astype(BF16)
            return carry

        lax.fori_loop(0, GATE_ROWS // CHUNK, chunk, 0)

    return pl.pallas_call(
        body, name=name, grid=(n // GATE_ROWS,),
        in_specs=[pl.BlockSpec((GATE_ROWS, A_WIDTH), lambda i: (i, 0)),
                  pl.BlockSpec((GATE_ROWS, A_WIDTH), lambda i: (i, 1)),
                  pl.BlockSpec((GATE_ROWS, A_WIDTH), lambda i: (i, 0)),
                  pl.BlockSpec((1, A_WIDTH), lambda i: (0, 0)), pl.BlockSpec((1, A_WIDTH), lambda i: (0, 0)),
                  pl.BlockSpec((A_GROUPS, CHUNK, CHUNK), lambda i: (0, 0, 0)),
                  pl.BlockSpec((CHUNK, A_GROUPS), lambda i: (0, 0))],
        out_specs=[pl.BlockSpec((GATE_ROWS, 2 * A_WIDTH), lambda i: (i, 0)),
                   pl.BlockSpec((A_GROUPS, CHUNK, CHUNK), lambda i: (0, 0, 0)),
                   pl.BlockSpec((CHUNK, A_GROUPS), lambda i: (0, 0)),
                   pl.BlockSpec((1, A_WIDTH), lambda i: (0, 0)), pl.BlockSpec((1, A_WIDTH), lambda i: (0, 0))],
        out_shape=[SDS((n, 2 * A_WIDTH), BF16), SDS((A_GROUPS, CHUNK, CHUNK), F32), SDS((CHUNK, A_GROUPS), F32),
                   SDS((1, A_WIDTH), F32), SDS((1, A_WIDTH), F32)],
        compiler_params=_params(("arbitrary",)),
    )(proj, proj, dmix, lng, lnb, ws, sbt)


def _t5_bucket_np(dist):
    max_exact = NUM_BUCKETS // 2
    dd = np.maximum(dist, 1).astype(np.float64)
    large = max_exact + np.log(dd / max_exact) / math.log(MAX_DISTANCE / max_exact) * (NUM_BUCKETS - max_exact)
    large = np.minimum(large.astype(np.int64), NUM_BUCKETS - 1)
    return np.where(dist < max_exact, dist, large)


def _bucket_tables(with_first):
    i = np.arange(ATTN_BLOCK)[:, None]
    j = np.arange(2 * ATTN_BLOCK)[None, :]
    rel = ATTN_BLOCK + i - j
    band = (rel >= 0) & (rel <= ATTN_BLOCK)
    tabs = []
    for own_only in (False, True) if with_first else (False,):
        for dil in DILATIONS:
            b = _t5_bucket_np(np.maximum(rel, 0) * dil)
            tabs.append(np.where(band & (j >= ATTN_BLOCK) if own_only else band, b, -1).reshape(1, -1))
    return np.stack(tabs).astype(np.float32)


BIAS_SIZE = ATTN_BLOCK * 2 * ATTN_BLOCK


def bias_tables(rel_bias_t, name, ride=None):
    idx = jnp.asarray(_bucket_tables(True).reshape(-1, ATTN_BLOCK, 2 * ATTN_BLOCK))
    ntab = idx.shape[0]

    def body(rb_ref, idx_ref, o_ref):
        iv = idx_ref[0]

        def head(h, carry):
            t = jnp.full(iv.shape, NEG_INF, F32)
            for b in range(NUM_BUCKETS):
                t = jnp.where(iv == float(b), rb_ref[h, b], t)
            o_ref[0, h] = t
            return carry

        lax.fori_loop(0, B_HEADS, head, 0)

    return _call(
        body, name=name, grid=(ntab,),
        in_specs=[pl.BlockSpec(memory_space=pltpu.SMEM),
                  pl.BlockSpec((1, ATTN_BLOCK, 2 * ATTN_BLOCK), lambda d: (d, 0, 0))],
        out_specs=[pl.BlockSpec((1, B_HEADS, ATTN_BLOCK, 2 * ATTN_BLOCK), lambda d: (d, 0, 0, 0))],
        out_shape=[SDS((ntab, B_HEADS, ATTN_BLOCK, 2 * ATTN_BLOCK), F32)],
        args=[rel_bias_t, idx], sem=("parallel",), ride=ride)


def rel_bias_grad(dbias, name):
    idx = jnp.asarray(_bucket_tables(False))

    def body(db_ref, idx_ref, o_ref):
        d = pl.program_id(0)
        iv = idx_ref[0]
        bk = lax.broadcasted_iota(jnp.int32, (NUM_BUCKETS, BIAS_SIZE), 0).astype(F32)
        onehot = (bk == iv).astype(F32)
        part = lax.dot_general(db_ref[0], onehot, (((1,), (1,)), ((), ())),
                               preferred_element_type=F32, precision=lax.Precision.HIGHEST)

        @pl.when(d == 0)
        def _():
            o_ref[...] = part

        @pl.when(d > 0)
        def _():
            o_ref[...] += part

    return pl.pallas_call(
        body, name=name, grid=(len(DILATIONS),),
        in_specs=[pl.BlockSpec((1, B_HEADS, BIAS_SIZE), lambda d: (d, 0, 0)),
                  pl.BlockSpec((1, 1, BIAS_SIZE), lambda d: (d, 0, 0))],
        out_specs=pl.BlockSpec((B_HEADS, NUM_BUCKETS), lambda d: (0, 0)),
        out_shape=SDS((B_HEADS, NUM_BUCKETS), F32),
        compiler_params=_params(("arbitrary",)),
    )(dbias, idx)


QK_SCALE = 1.0 / math.sqrt(HEAD_DIM)


def _attn_scores(q_scaled, kk, bias):
    return _dot_nt(q_scaled, kk) + bias


def _head0_lanes():
    return lax.broadcasted_iota(jnp.int32, (ATTN_BLOCK, LANE), 1) < HEAD_DIM


def _one_head(x2, head0, hh):
    return jnp.where(head0 if hh == 0 else jnp.logical_not(head0), x2, 0.0).astype(BF16)


QUAD = 4
QUAD_ROWS = SEQ // QUAD


def _deinterleave(src_ref, dst_ref):
    for r in range(QUAD):
        for c in range(QUAD_ROWS // ATTN_BLOCK):
            dst_ref[r, c * ATTN_BLOCK:(c + 1) * ATTN_BLOCK, :] = src_ref[
                pl.ds(r + c * QUAD * ATTN_BLOCK, ATTN_BLOCK, stride=QUAD), :]


def _deinterleave_again(src_ref, dst_ref):
    for r in range(QUAD):
        for s in range(QUAD):
            dst_ref[r + QUAD * s] = src_ref[r, pl.ds(s, ATTN_BLOCK, stride=QUAD), :]


def _interleave_back(src_ref, dst_ref, slot0, accumulate=False):
    for r in range(QUAD):
        for s in range(QUAD):
            rows = pl.ds(s, ATTN_BLOCK, stride=QUAD)
            if accumulate:
                dst_ref[slot0 + r, rows, :] += src_ref[r + QUAD * s]
            else:
                dst_ref[slot0 + r, rows, :] = src_ref[r + QUAD * s]


def _quad_tiles():
    return [(r, pl.ds(r + c * QUAD * ATTN_BLOCK, ATTN_BLOCK, stride=QUAD), slice(c * ATTN_BLOCK, (c + 1) * ATTN_BLOCK))
            for r in range(QUAD) for c in range(QUAD_ROWS // ATTN_BLOCK)]


def _attn_schedule(op):
    def d16(i, carry):
        for t in range(2 * QUAD):
            op(2, 2 * QUAD * i + t, 0, True)
        return carry

    lax.fori_loop(0, QUAD // 2, d16, 0)

    def d4(i, carry):
        for u in range(2):
            for nq in range(QUAD_ROWS // ATTN_BLOCK):
                op(1, 2 * i + u, nq * ATTN_BLOCK, nq == 0)
        return carry

    lax.fori_loop(0, QUAD // 2, d4, 0)
    op(0, None, 0, True)
    per_pass = 5

    def d1(j, carry):
        for t in range(per_pass):
            op(0, None, pl.multiple_of((1 + per_pass * j + t) * ATTN_BLOCK, ATTN_BLOCK), False)
        return carry

    lax.fori_loop(0, (SEQ // ATTN_BLOCK - 1) // per_pass, d1, 0)


def _keys(src, krows, first):
    kb = src[krows, :].astype(BF16)
    return jnp.concatenate([kb, kb], axis=0) if first else kb


def _table(seg, first):
    return len(DILATIONS) + seg if first else seg


def _kv_rows(start, first):
    return pl.ds(start, ATTN_BLOCK) if first else pl.ds(start - ATTN_BLOCK, 2 * ATTN_BLOCK)


MERGE_ROWS = 256


def attn_fwd(proj, bias, nb_local, name, ride=None):
    n = proj.shape[0]
    nseg = len(DILATIONS)

    def body(q_ref, k_ref, v_ref, b_ref, o_ref, lse_ref, q4_ref, k4_ref, v4_ref, os0_ref, ls0_ref, os4_ref, ls4_ref,
             q16_ref, k16_ref, v16_ref, os16_ref, ls16_ref):
        for src, mid, dst in ((q_ref, q4_ref, q16_ref), (k_ref, k4_ref, k16_ref), (v_ref, v4_ref, v16_ref)):
            _deinterleave(src, mid)
            _deinterleave_again(mid, dst)

        def op(seg, r, start, first):
            qrows = pl.ds(start, ATTN_BLOCK)
            krows = _kv_rows(start, first)
            if seg == 0:
                q_src, k_src, v_src, o_dst, l_dst = q_ref, k_ref, v_ref, os0_ref, ls0_ref
            elif seg == 1:
                q_src, k_src, v_src = q4_ref.at[r], k4_ref.at[r], v4_ref.at[r]
                o_dst, l_dst = os4_ref.at[r], ls4_ref.at[r]
            else:
                q_src, k_src, v_src = q16_ref.at[r], k16_ref.at[r], v16_ref.at[r]
                o_dst, l_dst = os16_ref.at[r], ls16_ref.at[r]
            q2, kb, vb = q_src[qrows, :] * QK_SCALE, _keys(k_src, krows, first), _keys(v_src, krows, first)
            head0 = _head0_lanes()
            outs, lses = [], []
            for hh in range(2):
                s = _attn_scores(_one_head(q2, head0, hh), kb, b_ref[_table(seg, first), hh])
                m = jnp.max(s, axis=-1, keepdims=True)
                p = jnp.exp(s - m)
                l = jnp.sum(p, axis=-1, keepdims=True)
                outs.append(_dot(p.astype(BF16), vb) / l)
                lses.append(jnp.broadcast_to(m + jnp.log(l), (ATTN_BLOCK, LANE)))
            o_dst[qrows, :] = jnp.where(head0, outs[0], outs[1])
            l_dst[qrows, :] = jnp.where(head0, lses[0], lses[1])

        _attn_schedule(op)
        _interleave_back(os16_ref, os4_ref, QUAD)
        _interleave_back(ls16_ref, ls4_ref, QUAD)

        for r, nat, quad in _quad_tiles():
            ls = [ls0_ref[nat, :], ls4_ref[r, quad, :], ls4_ref[QUAD + r, quad, :]]
            m = functools.reduce(jnp.maximum, ls)
            ws = [jnp.exp(l - m) for l in ls]
            den = ws[0] + ws[1] + ws[2]
            num = ws[0] * os0_ref[nat, :] + ws[1] * os4_ref[r, quad, :] + ws[2] * os4_ref[QUAD + r, quad, :]
            o_ref[nat, :] = num / den
            lse_ref[nat, :] = m + jnp.log(den)

    def in_spec(off):
        return pl.BlockSpec((SEQ, LANE), lambda b, p: (b, off // LANE + p))

    out_spec = pl.BlockSpec((SEQ, LANE), lambda b, p: (b, p))
    return _call(
        body, name=name, grid=(nb_local, HEAD_PAIRS),
        in_specs=[in_spec(Q_OFF), in_spec(K_OFF), in_spec(V_OFF),
                  pl.BlockSpec((2 * nseg, 2, ATTN_BLOCK, 2 * ATTN_BLOCK), lambda b, p: (0, p, 0, 0))],
        out_specs=[out_spec, out_spec],
        out_shape=[SDS((n, B_WIDTH), F32), SDS((n, B_WIDTH), F32)],
        scratch_shapes=[pltpu.VMEM((QUAD, QUAD_ROWS, LANE), F32)] * 3 + [pltpu.VMEM((SEQ, LANE), F32)] * 2
        + [pltpu.VMEM((2 * QUAD, QUAD_ROWS, LANE), F32)] * 2 + [pltpu.VMEM((QUAD * QUAD, ATTN_BLOCK, LANE), F32)] * 5,
        args=[proj, proj, proj, bias], sem=("parallel", "arbitrary"), ride=ride)


def attn_bwd(proj, b_out, dmix, lse_tot, bias, nb_local, name, ride=None):
    n = proj.shape[0]
    nseg = len(DILATIONS)
    a_blocks = A_WIDTH // LANE

    def body(q_ref, k_ref, v_ref, o_ref, do_ref, lse_ref, b_ref, dq_ref, dk_ref, dv_ref, db_ref,
             dqs_ref, delta_ref, dka_ref, dva_ref, q4_ref, k4_ref, v4_ref, do4_ref, lse4_ref, delta4_ref,
             dqs4_ref, dk4_ref, dv4_ref, q16_ref, k16_ref, v16_ref, do16_ref, lse16_ref, delta16_ref,
             dqs16_ref, dk16_ref, dv16_ref):
        @pl.when(pl.program_id(1) == 0)
        def _():
            db_ref[...] = jnp.zeros_like(db_ref)

        for acc_ref in (dka_ref, dva_ref, dk4_ref, dv4_ref):
            acc_ref[...] = jnp.zeros_like(acc_ref)
        quads = (q4_ref, k4_ref, v4_ref, do4_ref, lse4_ref, delta4_ref)
        hexes = (q16_ref, k16_ref, v16_ref, do16_ref, lse16_ref, delta16_ref)

        head_sum = (lax.broadcasted_iota(jnp.int32, (LANE, LANE), 0) // HEAD_DIM
                    == lax.broadcasted_iota(jnp.int32, (LANE, LANE), 1) // HEAD_DIM).astype(BF16)

        def row_dots(i, carry):
            rows = pl.ds(pl.multiple_of(i * MERGE_ROWS, MERGE_ROWS), MERGE_ROWS)
            delta_ref[rows, :] = _dot_sum(do_ref[rows, :] * o_ref[rows, :], head_sum)
            return carry

        lax.fori_loop(0, SEQ // MERGE_ROWS, row_dots, 0)
        for src, mid, dst in zip((q_ref, k_ref, v_ref, do_ref, lse_ref, delta_ref), quads, hexes):
            _deinterleave(src, mid)
            _deinterleave_again(mid, dst)

        def op(seg, r, start, first):
            qrows = pl.ds(start, ATTN_BLOCK)
            krows = _kv_rows(start, first)
            if seg == 0:
                srcs = (q_ref, k_ref, v_ref, do_ref, lse_ref, delta_ref)
                dq_dst, dk_dst, dv_dst = dqs_ref, dka_ref, dva_ref
            elif seg == 1:
                srcs = tuple(x.at[r] for x in quads)
                dq_dst, dk_dst, dv_dst = dqs4_ref.at[r], dk4_ref.at[r], dv4_ref.at[r]
            else:
                srcs = tuple(x.at[r] for x in hexes)
                dq_dst, dk_dst, dv_dst = dqs16_ref.at[r], dk16_ref.at[r], dv16_ref.at[r]
            q_src, k_src, v_src, do_src, lse_src, delta_src = srcs
            q2, kb, vb = q_src[qrows, :] * QK_SCALE, _keys(k_src, krows, first), _keys(v_src, krows, first)
            do2, lse2, delta2 = do_src[qrows, :], lse_src[qrows, :], delta_src[qrows, :]
            head0 = _head0_lanes()
            dqs, dk, dv = [], None, None
            for hh in range(2):
                col = slice(hh * HEAD_DIM, hh * HEAD_DIM + 1)
                q, dob = _one_head(q2, head0, hh), _one_head(do2, head0, hh)
                p = jnp.exp(_attn_scores(q, kb, b_ref[_table(seg, first), hh]) - lse2[:, col])
                dvh = _dot_tn(p.astype(BF16), dob)
                ds = p * (_dot_nt(dob, vb) - delta2[:, col])
                if first:
                    db_ref[seg, hh, :, ATTN_BLOCK:] += ds[:, ATTN_BLOCK:]
                else:
                    db_ref[seg, hh] += ds
                dsb = ds.astype(BF16)
                dqs.append(_dot(dsb, kb))
                dkh = _dot_tn(dsb, q)
                dk = dkh if dk is None else dk + dkh
                dv = dvh if dv is None else dv + dvh
            if first:
                dk, dv = dk[ATTN_BLOCK:], dv[ATTN_BLOCK:]
            dq_dst[qrows, :] = jnp.where(head0, dqs[0], dqs[1]) * QK_SCALE
            if seg == 2:
                dk_dst[krows, :] = dk
                dv_dst[krows, :] = dv
            else:
                dk_dst[krows, :] += dk
                dv_dst[krows, :] += dv

        _attn_schedule(op)
        _interleave_back(dqs16_ref, dqs4_ref, QUAD)
        _interleave_back(dk16_ref, dk4_ref, 0, accumulate=True)
        _interleave_back(dv16_ref, dv4_ref, 0, accumulate=True)

        for r, nat, quad in _quad_tiles():
            dqs_ref[nat, :] += dqs4_ref[r, quad, :] + dqs4_ref[QUAD + r, quad, :]
            dka_ref[nat, :] += dk4_ref[r, quad, :]
            dva_ref[nat, :] += dv4_ref[r, quad, :]

        def merge(i, carry):
            rows = pl.ds(pl.multiple_of(i * MERGE_ROWS, MERGE_ROWS), MERGE_ROWS)
            dq_ref[rows, :] = dqs_ref[rows, :].astype(BF16)
            dk_ref[rows, :] = dka_ref[rows, :].astype(BF16)
            dv_ref[rows, :] = dva_ref[rows, :].astype(BF16)
            return carry

        lax.fori_loop(0, SEQ // MERGE_ROWS, merge, 0)

    def pspec(off):
        return pl.BlockSpec((SEQ, LANE), lambda p, b: (b, off // LANE + p))

    ospec = pl.BlockSpec((SEQ, LANE), lambda p, b: (b, p))
    bspec = pl.BlockSpec((nseg, 2, ATTN_BLOCK, 2 * ATTN_BLOCK), lambda p, b: (0, p, 0, 0))
    gshape = SDS((n, B_WIDTH), BF16)
    return _call(
        body, name=name, grid=(HEAD_PAIRS, nb_local),
        in_specs=[pspec(Q_OFF), pspec(K_OFF), pspec(V_OFF), ospec,
                  pl.BlockSpec((SEQ, LANE), lambda p, b: (b, a_blocks + p)), ospec,
                  pl.BlockSpec((2 * nseg, 2, ATTN_BLOCK, 2 * ATTN_BLOCK), lambda p, b: (0, p, 0, 0))],
        out_specs=[ospec, ospec, ospec, bspec],
        out_shape=[gshape, gshape, gshape, SDS((nseg, B_HEADS, ATTN_BLOCK, 2 * ATTN_BLOCK), F32)],
        scratch_shapes=[pltpu.VMEM((SEQ, LANE), F32)] * 4 + [pltpu.VMEM((QUAD, QUAD_ROWS, LANE), F32)] * 6
        + [pltpu.VMEM((2 * QUAD, QUAD_ROWS, LANE), F32)] + [pltpu.VMEM((QUAD, QUAD_ROWS, LANE), F32)] * 2
        + [pltpu.VMEM((QUAD * QUAD, ATTN_BLOCK, LANE), F32)] * 9,
        args=[proj, proj, proj, b_out, dmix, lse_tot, bias], sem=("arbitrary", "arbitrary"), ride=ride)


PAD = 8
CONV_ROWS = 64


CONV_LANES = 128


def _conv_taps(gp_ref, head_ref, r0, ls):
    g0 = gp_ref[r0:r0 + CONV_ROWS, ls]
    if r0 == 0:
        return g0, head_ref[PAD - 1:PAD - 1 + CONV_ROWS, ls], head_ref[PAD - 2:PAD - 2 + CONV_ROWS, ls]
    return g0, gp_ref[r0 - 1:r0 - 1 + CONV_ROWS, ls], gp_ref[r0 - 2:r0 - 2 + CONV_ROWS, ls]


def _fill_head(gp_ref, head_ref):
    head_ref[0:PAD, :] = jnp.zeros((PAD, CONV_LANES), F32)
    head_ref[PAD:PAD + CONV_ROWS, :] = gp_ref[0:CONV_ROWS, :]


def _lane_passes():
    return [slice(l0, l0 + LANE) for l0 in range(0, CONV_LANES, LANE)]


def conv_gelu_bwd(dgu, gp, up, cw, cb, nb_local, name, ride=None):
    n, f = gp.shape

    def fold(v):
        return jnp.sum(v.reshape(CONV_ROWS // 8, 8, LANE), axis=0)

    def body(dgu_ref, gp_ref, up_ref, cw_ref, cb_ref, dgp_ref, dup_ref, dcw_ref, dcb_ref, head_ref, dc_ref):
        b = pl.program_id(1)
        _fill_head(gp_ref, head_ref)
        dc_ref[SEQ:SEQ + PAD, :] = jnp.zeros((PAD, CONV_LANES), F32)
        for ls in _lane_passes():
            w0, w1, w2, bias = cw_ref[0:1, ls], cw_ref[1:2, ls], cw_ref[2:3, ls], cb_ref[:, ls]
            sums = [jnp.zeros((8, LANE), F32) for _ in range(4)]
            for r0 in range(0, SEQ, CONV_ROWS):
                rows = slice(r0, r0 + CONV_ROWS)
                g0, g1, g2 = _conv_taps(gp_ref, head_ref, r0, ls)
                gg, dgg = _gelu_and_grad(bias + w0 * g2 + w1 * g1 + w2 * g0)
                dgu = dgu_ref[rows, ls].astype(F32)
                dup_ref[rows, ls] = (dgu * gg).astype(BF16)
                dc = dgu * up_ref[rows, ls] * dgg
                dc_ref[rows, ls] = dc
                sums = [sums[0] + fold(dc * g2), sums[1] + fold(dc * g1), sums[2] + fold(dc * g0), sums[3] + fold(dc)]
            for r0 in range(0, SEQ, CONV_ROWS):
                dgp_ref[r0:r0 + CONV_ROWS, ls] = (
                    w2 * dc_ref[r0:r0 + CONV_ROWS, ls] + w1 * dc_ref[r0 + 1:r0 + 1 + CONV_ROWS, ls]
                    + w0 * dc_ref[r0 + 2:r0 + 2 + CONV_ROWS, ls]).astype(BF16)
            dcw = jnp.concatenate([jnp.sum(s, axis=0, keepdims=True) for s in sums[:3]], axis=0)
            dcb = jnp.sum(sums[3], axis=0, keepdims=True)

            @pl.when(b == 0)
            def _(dcw=dcw, dcb=dcb, ls=ls):
                dcw_ref[:, ls] = dcw
                dcb_ref[:, ls] = dcb

            @pl.when(b > 0)
            def _(dcw=dcw, dcb=dcb, ls=ls):
                dcw_ref[:, ls] += dcw
                dcb_ref[:, ls] += dcb

    blk = pl.BlockSpec((SEQ, CONV_LANES), lambda j, b: (b, j))
    wspec = pl.BlockSpec((3, CONV_LANES), lambda j, b: (0, j))
    bspec = pl.BlockSpec((1, CONV_LANES), lambda j, b: (0, j))
    return _call(
        body, name=name, grid=(f // CONV_LANES, nb_local),
        in_specs=[blk, blk, blk, wspec, bspec], out_specs=[blk, blk, wspec, bspec],
        out_shape=[SDS((n, f), BF16), SDS((n, f), BF16), SDS((3, f), F32), SDS((1, f), F32)],
        scratch_shapes=[pltpu.VMEM((PAD + CONV_ROWS, CONV_LANES), F32), pltpu.VMEM((SEQ + PAD, CONV_LANES), F32)],
        args=[dgu, gp, up, cw, cb], sem=("parallel", "arbitrary"), ride=ride)


def norm_mid_epilogue(x1, dout, z2, g3, g2):
    n, d = x1.shape

    def fn(dh2, step, x1_ref, dout_ref, z2_ref, g3_ref, g2_ref, dx1_ref, dz2_ref, dg3_ref, dg2_ref):
        dxa, dg3r = _rms_bwd(dh2, x1_ref[...], g3_ref[...])
        dx1 = dout_ref[...] + dxa
        dx1_ref[...] = dx1
        dz2, dg2r = _rms_bwd(dx1, z2_ref[...], g2_ref[...])
        dz2_ref[...] = dz2.astype(BF16)
        _accumulate(dg3_ref, jnp.sum(dg3r, axis=0, keepdims=True), step)
        _accumulate(dg2_ref, jnp.sum(dg2r, axis=0, keepdims=True), step)

    return fn, [x1, dout, z2, g3, g2], [SDS((n, d), F32), SDS((n, d), BF16), SDS((1, d), F32), SDS((1, d), F32)]


def norm_in_epilogue(x, dx1, g1):
    n, d = x.shape

    def fn(dh1, step, x_ref, dx1_ref, g1_ref, dx_ref, dg1_ref):
        dxa, dgr = _rms_bwd(dh1, x_ref[...], g1_ref[...])
        dx_ref[...] = dx1_ref[...] + dxa
        _accumulate(dg1_ref, jnp.sum(dgr, axis=0, keepdims=True), step)

    return fn, [x, dx1, g1], [SDS((n, d), F32), SDS((1, d), F32)]


def cast_bf16(arrays, name):
    def body(*refs):
        for i_ref, o_ref in zip(refs[:len(arrays)], refs[len(arrays):]):
            o_ref[...] = i_ref[...].astype(BF16)

    return pl.pallas_call(body, name=name, out_shape=[SDS(a.shape, BF16) for a in arrays],
                          compiler_params=_params())(*arrays)


def adam_update(parts, w, m, v, name, tr=None):
    s, r, c = parts.shape
    tr = r if tr is None else tr
    bc1 = 1.0 - ADAM_B1 ** ADAM_STEP
    bc2 = 1.0 - ADAM_B2 ** ADAM_STEP

    def body(p_ref, w_ref, m_ref, v_ref, g_ref, d_ref, nm_ref, nv_ref):
        g = p_ref[0].astype(F32)
        for j in range(1, s):
            g = g + p_ref[j].astype(F32)
        nm = ADAM_B1 * m_ref[...] + (1.0 - ADAM_B1) * g
        nv = ADAM_B2 * v_ref[...] + (1.0 - ADAM_B2) * (g * g)
        g_ref[...] = g
        nm_ref[...] = nm
        nv_ref[...] = nv
        d_ref[...] = -ADAM_LR * ((nm / bc1) / (jnp.sqrt(nv / bc2) + ADAM_EPS) + ADAM_WD * w_ref[...])

    blk = pl.BlockSpec((tr, c), lambda i: (i, 0))
    return pl.pallas_call(
        body, name=name, grid=(r // tr,),
        in_specs=[pl.BlockSpec((s, tr, c), lambda i: (0, i, 0)), blk, blk, blk],
        out_specs=[blk] * 4, out_shape=[SDS((r, c), F32)] * 4,
        compiler_params=_params(("parallel",)),
    )(parts, w, m, v)


EARLY_NAMES = ("spatial_w", "norm_mix_post", "norm_ffn_pre", "norm_ffn_post", "conv_b", "ln_v_gain", "ln_v_bias",
               "spatial_b")
LATE_NAMES = ("norm_mix_pre", "rel_bias")
PACK_ROW_ALIGN = 8


def _pack_rows(size):
    rows = -(-size // LANE)
    return -(-rows // PACK_ROW_ALIGN) * PACK_ROW_ALIGN


def _pack(arrays):
    flat = []
    for a in arrays:
        rows = _pack_rows(a.size)
        flat.append(jnp.pad(a.reshape(-1), (0, rows * LANE - a.size)))
    return jnp.concatenate(flat).reshape(-1, LANE)


def _unpack(packed, shapes):
    out, row = [], 0
    for shp in shapes:
        size = int(np.prod(shp))
        out.append(packed[row:row + _pack_rows(size)].reshape(-1)[:size].reshape(shp))
        row += _pack_rows(size)
    return out


def kernel(x, norm_mix_pre, norm_mix_post, norm_ffn_pre, norm_ffn_post, w_in, ln_v_gain, ln_v_bias, spatial_w, spatial_b, rel_bias, w_out, w_gate, w_up, conv_w, conv_b, w_down, loss_target, m_norm_mix_pre, m_norm_mix_post, m_norm_ffn_pre, m_norm_ffn_post, m_w_in, m_ln_v_gain, m_ln_v_bias, m_spatial_w, m_spatial_b, m_rel_bias, m_w_out, m_w_gate, m_w_up, m_conv_w, m_conv_b, m_w_down, v_norm_mix_pre, v_norm_mix_post, v_norm_ffn_pre, v_norm_ffn_post, v_w_in, v_ln_v_gain, v_ln_v_bias, v_spatial_w, v_spatial_b, v_rel_bias, v_w_out, v_w_gate, v_w_up, v_conv_w, v_conv_b, v_w_down):
    given = dict(locals())
    nb_local, seq, d = x.shape
    n = nb_local * seq
    cols = w_in.shape[2]

    def by_columns(g):
        return g.transpose(1, 0, 2).reshape(g.shape[1], N_DEV * g.shape[2])

    def by_rows(g):
        return g.reshape(N_DEV * g.shape[1], g.shape[2])

    def blocks(g):
        return g.reshape(N_DEV, g.shape[0] // N_DEV, g.shape[1])

    xf, target = x.reshape(n, d), loss_target.reshape(n, d)
    ln_g, ln_b = ln_v_gain.reshape(1, A_WIDTH), ln_v_bias.reshape(1, A_WIDTH)
    spatial_bt, rel_bias_t = spatial_b[0].T, rel_bias.T

    s_in, s_out, s_gate, s_up, s_down = cast_bf16(
        [w_in[0].T, w_out[0], w_gate[0].T, w_up[0].T, w_down[0]], "cast_shards")
    (bias,), (g_in, g_cw) = bias_tables(rel_bias_t, "bias_tables", ride=([], [s_in, conv_w[0]]))
    w_in_t, conv_w_f = by_rows(g_in), by_columns(g_cw)

    (h1, proj), _ = norm_mm(xf, norm_mix_pre, [w_in_t], "fwd_norm_in", tn=IN_COLS)
    a = gating_fwd(proj, ln_g, ln_b, spatial_w[0], spatial_bt, "fwd_gating")
    (b_out, lse_tot), (g_out, g_gate, g_up) = attn_fwd(proj, bias, nb_local, "fwd_attn",
                                                       ride=([], [s_out, s_gate, s_up]))
    w_out_f, w_gate_t, w_up_t = by_rows(g_out), by_rows(g_gate), by_rows(g_up)
    z2, x1 = mm_res_norm([a, b_out], w_out_f, xf, norm_mix_post, "fwd_out_norm")
    (h2, gp, up, gu), (g_down,) = norm_mm(x1, norm_ffn_pre, [w_gate_t, w_up_t], "fwd_norm_ffn_conv", tm=256, tn=D_FF,
                                          ride=([], [s_down]), conv=(conv_w_f, conv_b))
    w_down_f = by_rows(g_down)
    dy, dout, dg4, loss_part = down_loss(gu, w_down_f, x1, norm_ffn_post, target, "fwd_down_loss")

    p_down = mm_tn([gu], [dy], "bwd_dw_down", t1=256, t2=D_MODEL)
    (dgu,), _ = mm_nt([(dy, 0, 0)], [w_down_f], "bwd_dgu", out_dtype=BF16)
    (dgp, dup, p_conv_w, p_conv_b), (r_down,) = conv_gelu_bwd(
        dgu, gp, up, conv_w_f, conv_b, nb_local, "bwd_conv_gelu", ride=([blocks(p_down)], []))
    p_gate = mm_tn([dgp], [h2], "bwd_dw_gate", t1=256, t2=D_MODEL)
    p_up = mm_tn([dup], [h2], "bwd_dw_up", t1=256, t2=D_MODEL)
    (dx1, dz2, dg3, dg2), _ = mm_nt([(dgp, 0, 0), (dup, 1, 0)], [w_gate_t, w_up_t], "bwd_dh2_norm_mid", tm=256,
                                    by_rows=True,
                                    epilogue=norm_mid_epilogue(x1, dout, z2, norm_ffn_pre, norm_mix_post))
    p_out = mm_tn([a, b_out], [dz2], "bwd_dw_out", t1=256, t2=D_MODEL)
    (dmix,), _ = mm_nt([(dz2, 0, 0)], [w_out_f], "bwd_dmix")
    duv, p_ws, p_sbt, p_lng, p_lnb = gating_bwd(proj, dmix, ln_g, ln_b, spatial_w[0], spatial_bt, "bwd_gating")
    small = dict(spatial_w=p_ws, norm_mix_post=dg2, norm_ffn_pre=dg3, norm_ffn_post=dg4, conv_b=p_conv_b,
                 ln_v_gain=p_lng, ln_v_bias=p_lnb, spatial_b=p_sbt.T)
    pack_early = _pack([small[k] for k in EARLY_NAMES] + [p_conv_w, loss_part])
    (dq, dk, dv, dbias), (r_gate, r_up, r_out, r_early) = attn_bwd(
        proj, b_out, dmix, lse_tot, bias, nb_local, "bwd_attn",
        ride=([blocks(p_gate), blocks(p_up), blocks(p_out)], [pack_early]))
    p_rel_bias_t = rel_bias_grad(dbias.reshape(len(DILATIONS), B_HEADS, BIAS_SIZE), "bwd_rel_bias")
    p_in = mm_tn([duv, dq, dk, dv], [h1], "bwd_dw_in", t1=256, t2=D_MODEL)
    (grad_x, dg1), (r_in,) = mm_nt(
        [(duv, 0, 0), (dq, 0, Q_OFF), (dk, 0, K_OFF), (dv, 0, V_OFF)], [w_in_t], "bwd_dh1_norm_in", by_rows=True,
        epilogue=norm_in_epilogue(xf, dx1, norm_mix_pre), ride=([blocks(p_in)], []))
    small.update(norm_mix_pre=dg1, rel_bias=p_rel_bias_t.T)
    (r_late,) = exchange([], [_pack([small[k] for k in LATE_NAMES])], "exchange_late")

    res = {}
    for k, received in (("w_in", r_in), ("w_gate", r_gate), ("w_up", r_up)):
        res[k] = [o.T for o in adam_update(received, given[k][0].T, given["m_" + k][0].T, given["v_" + k][0].T,
                                           "adam_" + k, tr=cols // 2)]
    res["w_out"] = adam_update(r_out, w_out[0], m_w_out[0], v_w_out[0], "adam_w_out")
    res["w_down"] = adam_update(r_down, w_down[0], m_w_down[0], v_w_down[0], "adam_w_down", tr=cols // 2)

    def adam_packed(received, names, tail, name):
        zeros = [jnp.zeros_like(t) for t in tail]
        packs = [_pack([given[pre + k] for k in names] + zeros) for pre in ("", "m_", "v_")]
        shapes = [given[k].shape for k in names] + [t.shape for t in tail]
        unpacked = [_unpack(p, shapes) for p in adam_update(received, *packs, name)]
        for i, k in enumerate(names):
            res[k] = [u[i] for u in unpacked]
        return unpacked[0][len(names):]

    g_conv_w_full, loss_sum = adam_packed(r_early, EARLY_NAMES, [p_conv_w, loss_part], "adam_small_early")
    adam_packed(r_late, LATE_NAMES, [], "adam_small_late")
    g_conv_w = lax.dynamic_slice_in_dim(g_conv_w_full, _my_index() * cols, cols, axis=1)
    res["conv_w"] = adam_update(g_conv_w[None], conv_w[0], m_conv_w[0], v_conv_w[0], "adam_conv_w")
    loss = loss_sum[0, 0]

    names = ("norm_mix_pre", "norm_mix_post", "norm_ffn_pre", "norm_ffn_post", "w_in", "ln_v_gain", "ln_v_bias",
             "spatial_w", "spatial_b", "rel_bias", "w_out", "w_gate", "w_up", "conv_w", "conv_b", "w_down")
    outs = [loss, grad_x.reshape(x.shape)]
    for t in range(4):
        outs += [res[k][t].reshape(given[k].shape) for k in names]
    return tuple(outs)
```

```python
import functools
import math

import numpy as np
import jax
import jax.numpy as jnp
from jax import lax
from jax.experimental import pallas as pl
from jax.experimental.pallas import tpu as pltpu

F32 = jnp.float32
BF16 = jnp.bfloat16
SDS = jax.ShapeDtypeStruct

D_MODEL = 1024
SEQ = 2048
HEAD_DIM = 64
A_GROUPS = 4
A_WIDTH = A_GROUPS * HEAD_DIM
B_HEADS = 12
B_WIDTH = B_HEADS * HEAD_DIM
HEAD_PAIRS = B_HEADS // 2
CHUNK = 128
ATTN_BLOCK = 128
DILATIONS = (1, 4, 16)
NUM_BUCKETS = 32
MAX_DISTANCE = 2048
D_FF = 2816
IN_COLS = 2 * A_WIDTH + 3 * B_WIDTH
Q_OFF = 2 * A_WIDTH
K_OFF = Q_OFF + B_WIDTH
V_OFF = K_OFF + B_WIDTH
NORM_EPS = 1e-6
NEG_INF = -1e30
N_DEV = 8
LANE = 128

ADAM_LR = 0.001
ADAM_B1 = 0.9
ADAM_B2 = 0.999
ADAM_EPS = 1e-08
ADAM_WD = 0.01
ADAM_STEP = 10

GELU_C0 = math.sqrt(2.0 / math.pi)
GELU_C1 = 0.044715

VMEM_LIMIT = 56 * 1024 * 1024


def _params(sem=None):
    if sem is None:
        return pltpu.CompilerParams(vmem_limit_bytes=VMEM_LIMIT)
    return pltpu.CompilerParams(dimension_semantics=sem, vmem_limit_bytes=VMEM_LIMIT)


def _gelu(x):
    t = jnp.tanh(x * (GELU_C0 + (GELU_C0 * GELU_C1) * (x * x)))
    return x * (0.5 + 0.5 * t)


def _gelu_and_grad(x):
    x2 = x * x
    t = jnp.tanh(x * (GELU_C0 + (GELU_C0 * GELU_C1) * x2))
    half = 0.5 + 0.5 * t
    dg = half + x * (0.5 - 0.5 * (t * t)) * (GELU_C0 + (3.0 * GELU_C0 * GELU_C1) * x2)
    return x * half, dg


def _dot(a, b):
    return jnp.dot(a, b, preferred_element_type=F32)


def _dot_nt(a, b):
    return lax.dot_general(a, b, (((1,), (1,)), ((), ())), preferred_element_type=F32)


def _dot_tn(a, b):
    return lax.dot_general(a, b, (((0,), (0,)), ((), ())), preferred_element_type=F32)


def _rms_bwd(d, xin, g):
    r = lax.rsqrt(jnp.mean(xin * xin, axis=-1, keepdims=True) + NORM_EPS)
    xh = xin * r
    gd = g * d
    dx = r * (gd - xh * jnp.mean(gd * xh, axis=-1, keepdims=True))
    return dx, d * xh


MESH = pl.DeviceIdType.MESH
ANY = pl.BlockSpec(memory_space=pl.ANY)
PEER_MASKS = tuple(range(1, N_DEV))


def _my_index():
    return lax.axis_index("x") * 4 + lax.axis_index("y") * 2 + lax.axis_index("c")


def _peer(mask):
    x, y, c = lax.axis_index("x"), lax.axis_index("y"), lax.axis_index("c")
    px = 1 - x if mask & 4 else x
    py = 1 - y if mask & 2 else y
    pc = 1 - c if mask & 1 else c
    return (px, py, pc), px * 4 + py * 2 + pc


RELAY_AT = 3
SIBLING = 1
CHIP_MASKS = (2, 4, 6)


class _Exchange:
    def __init__(self, nblocked, in_refs, out_refs, sems):
        send_sems, recv_sems, local_sems = sems
        me = _my_index()
        sibling, _ = _peer(SIBLING)
        self.local, self.first, self.relays, self.relayed_in, self.last_in = [], [], [], [], []
        for a, (in_ref, out_ref) in enumerate(zip(in_refs, out_refs)):
            def copy(src, slot, mask, to):
                return pltpu.make_async_remote_copy(
                    src_ref=src, dst_ref=out_ref.at[slot], send_sem=send_sems.at[a, mask - 1],
                    recv_sem=recv_sems.at[a, mask - 1], device_id=to, device_id_type=MESH)

            if a < nblocked:
                self.local.append(pltpu.make_async_copy(in_ref.at[me], out_ref.at[me], local_sems.at[a]))
                for mask in PEER_MASKS:
                    peer, pidx = _peer(mask)
                    self.first.append(copy(in_ref.at[pidx], me, mask, peer))
                    self.last_in.append(copy(in_ref.at[pidx], pidx, mask, peer))
                continue
            self.local.append(pltpu.make_async_copy(in_ref, out_ref.at[me], local_sems.at[a]))
            for mask in (SIBLING,) + CHIP_MASKS:
                peer, pidx = _peer(mask)
                self.first.append(copy(in_ref, me, mask, peer))
                (self.last_in if mask == SIBLING else self.relayed_in).append(copy(in_ref, pidx, mask, peer))
            for mask in CHIP_MASKS:
                _, origin = _peer(mask)
                _, far = _peer(mask | SIBLING)
                self.relays.append(copy(out_ref.at[origin], origin, mask | SIBLING, sibling))
                self.last_in.append(copy(in_ref, far, mask | SIBLING, sibling))

    def start(self):
        for cp in self.local + self.first[::-1]:
            cp.start()

    def relay(self):
        for arrived, onward in zip(self.relayed_in, self.relays):
            arrived.wait_recv()
            onward.start()

    def finish(self):
        for cp in self.first + self.relays:
            cp.wait_send()
        for cp in self.last_in:
            cp.wait_recv()
        for cp in self.local:
            cp.wait()


def _exchange_out_shape(blocked, whole):
    return [SDS(b.shape, b.dtype) for b in blocked] + [SDS((N_DEV,) + w.shape, w.dtype) for w in whole]


def _exchange_sems(n):
    return [pltpu.SemaphoreType.DMA((n, N_DEV - 1)), pltpu.SemaphoreType.DMA((n, N_DEV - 1)),
            pltpu.SemaphoreType.DMA((n,))]


def exchange(blocked, whole, name):
    nb, n = len(blocked), len(blocked) + len(whole)

    def body(*refs):
        ex = _Exchange(nb, refs[:n], refs[n:2 * n], refs[2 * n:])
        ex.start()
        ex.relay()
        ex.finish()

    return pl.pallas_call(
        body, name=name, in_specs=[ANY] * n, out_specs=[ANY] * n, out_shape=_exchange_out_shape(blocked, whole),
        scratch_shapes=_exchange_sems(n),
    )(*blocked, *whole)


def _call(body, *, name, grid, in_specs, out_specs, out_shape, args, scratch_shapes=(), sem=None, ride=None):
    out_shape, out_specs, scratch_shapes = list(out_shape), list(out_specs), list(scratch_shapes)
    if ride is None:
        outs = pl.pallas_call(body, name=name, grid=grid, in_specs=list(in_specs), out_specs=out_specs,
                              out_shape=out_shape, scratch_shapes=scratch_shapes,
                              compiler_params=_params(sem))(*args)
        return list(outs), []
    blocked, whole = ride
    cargs = list(blocked) + list(whole)
    nb, nc = len(blocked), len(cargs)
    n_in, n_out, n_scr = len(args), len(out_shape), len(scratch_shapes)
    steps = math.prod(grid)
    assert steps >= 3, grid

    def riding(*refs):
        ins, refs = refs[:n_in], refs[n_in:]
        cins, refs = refs[:nc], refs[nc:]
        outs, refs = refs[:n_out], refs[n_out:]
        couts, refs = refs[:nc], refs[nc:]
        scr, sems = refs[:n_scr], refs[n_scr:]
        step = functools.reduce(lambda acc, k: acc * grid[k] + pl.program_id(k), range(len(grid)), 0)

        @pl.when(step == 0)
        def _():
            _Exchange(nb, cins, couts, sems).start()

        @pl.when(step == RELAY_AT * steps // 4)
        def _():
            _Exchange(nb, cins, couts, sems).relay()

        body(*ins, *outs, *scr)

        @pl.when(step == steps - 1)
        def _():
            _Exchange(nb, cins, couts, sems).finish()

    res = pl.pallas_call(
        riding, name=name, grid=grid, in_specs=list(in_specs) + [ANY] * nc, out_specs=out_specs + [ANY] * nc,
        out_shape=out_shape + _exchange_out_shape(blocked, whole),
        scratch_shapes=scratch_shapes + _exchange_sems(nc),
        compiler_params=_params(("arbitrary",) * len(grid)))(*args, *cargs)
    return list(res[:n_out]), list(res[n_out:])


def norm_mm(x, g, ws, name, tm=512, tn=1408, ride=None, conv=None):
    n, d = x.shape
    f = ws[0].shape[0]
    nw = len(ws)
    extra_in, extra_spec, extra_out, extra_out_spec, scratch = [], [], [], [], []
    if conv is not None:
        assert nw == 2 and tn == f and SEQ % tm == 0 and tm % CONV_ROWS == 0
        extra_in = list(conv)
        extra_spec = [pl.BlockSpec((3, f), lambda i, j: (0, 0)), pl.BlockSpec((1, f), lambda i, j: (0, 0))]
        extra_out, extra_out_spec = [SDS((n, f), BF16)], [pl.BlockSpec((tm, f), lambda i, j: (i, 0))]
        scratch = [pltpu.VMEM((PAD + CONV_ROWS, f), F32), pltpu.VMEM((PAD, f), F32)]

    def body(x_ref, g_ref, *refs):
        w_refs, refs = refs[:nw], refs[nw:]
        conv_refs, refs = refs[:len(extra_in)], refs[len(extra_in):]
        h_ref, o_refs, refs = refs[0], refs[1:1 + nw], refs[1 + nw:]

        @pl.when(pl.program_id(1) == 0)
        def _():
            xv = x_ref[...]
            r = lax.rsqrt(jnp.mean(xv * xv, axis=-1, keepdims=True) + NORM_EPS)
            h_ref[...] = (xv * r * g_ref[...]).astype(BF16)

        h = h_ref[...]
        for w_ref, o_ref in zip(w_refs, o_refs):
            o_ref[...] = _dot_nt(h, w_ref[...])
        if conv is None:
            return
        (cw_ref, cb_ref), (gp_ref, up_ref), (gu_ref, head_ref, carry_ref) = conv_refs, o_refs, refs

        @pl.when(pl.program_id(0) % (SEQ // tm) == 0)
        def _():
            carry_ref[...] = jnp.zeros_like(carry_ref)

        head_ref[0:PAD, :] = carry_ref[...]
        head_ref[PAD:PAD + CONV_ROWS, :] = gp_ref[0:CONV_ROWS, :]
        for l0 in range(0, f, LANE):
            ls = slice(l0, l0 + LANE)
            w0, w1, w2, bias = cw_ref[0:1, ls], cw_ref[1:2, ls], cw_ref[2:3, ls], cb_ref[:, ls]
            for r0 in range(0, tm, CONV_ROWS):
                g0, g1, g2 = _conv_taps(gp_ref, head_ref, r0, ls)
                c = bias + w0 * g2 + w1 * g1 + w2 * g0
                gu_ref[r0:r0 + CONV_ROWS, ls] = (_gelu(c) * up_ref[r0:r0 + CONV_ROWS, ls]).astype(BF16)
        carry_ref[...] = gp_ref[tm - PAD:tm, :]

    return _call(
        body, name=name, grid=(n // tm, f // tn),
        in_specs=[pl.BlockSpec((tm, d), lambda i, j: (i, 0)), pl.BlockSpec((1, d), lambda i, j: (0, 0))]
        + [pl.BlockSpec((tn, d), lambda i, j: (j, 0)) for _ in ws] + extra_spec,
        out_specs=[pl.BlockSpec((tm, d), lambda i, j: (i, 0))]
        + [pl.BlockSpec((tm, tn), lambda i, j: (i, j)) for _ in ws] + extra_out_spec,
        out_shape=[SDS((n, d), BF16)] + [SDS((n, f), F32) for _ in ws] + extra_out,
        scratch_shapes=scratch,
        args=[x, g, *ws, *extra_in], sem=("parallel" if conv is None else "arbitrary", "arbitrary"), ride=ride)


def _lane_concat(refs):
    vals = [r[...].astype(BF16) for r in refs]
    return vals[0] if len(vals) == 1 else jnp.concatenate(vals, axis=1)


def mm_res_norm(a_list, w, res, g, name, tm=512):
    n = a_list[0].shape[0]
    k, d = w.shape
    na = len(a_list)

    def body(*refs):
        w_ref, res_ref, g_ref, y_ref, o_ref = refs[na:]
        y = _dot(_lane_concat(refs[:na]), w_ref[...])
        r = lax.rsqrt(jnp.mean(y * y, axis=-1, keepdims=True) + NORM_EPS)
        y_ref[...] = y
        o_ref[...] = res_ref[...] + y * r * g_ref[...]

    return pl.pallas_call(
        body, name=name, grid=(n // tm,),
        in_specs=[pl.BlockSpec((tm, a.shape[1]), lambda i: (i, 0)) for a in a_list]
        + [pl.BlockSpec((k, d), lambda i: (0, 0)),
           pl.BlockSpec((tm, d), lambda i: (i, 0)), pl.BlockSpec((1, d), lambda i: (0, 0))],
        out_specs=[pl.BlockSpec((tm, d), lambda i: (i, 0)), pl.BlockSpec((tm, d), lambda i: (i, 0))],
        out_shape=[SDS((n, d), F32), SDS((n, d), F32)],
        compiler_params=_params(("parallel",)),
    )(*a_list, w, res, g)


def down_loss(a, w, res, g, target, name, tm=512):
    n, k = a.shape
    d = w.shape[1]
    inv_d = 1.0 / d

    def body(a_ref, w_ref, res_ref, g_ref, t_ref, dy_ref, dout_ref, dg_ref, loss_ref):
        i = pl.program_id(0)
        y = _dot(a_ref[...], w_ref[...])
        gv = g_ref[...]
        r = lax.rsqrt(jnp.mean(y * y, axis=-1, keepdims=True) + NORM_EPS)
        yh = y * r
        e = res_ref[...] + yh * gv - t_ref[...]
        part = 0.5 * inv_d * jnp.sum(jnp.sum(e * e, axis=-1, keepdims=True), axis=0, keepdims=True)
        dout = e * inv_d
        dout_ref[...] = dout
        gd = gv * dout
        dy_ref[...] = (r * (gd - yh * jnp.mean(gd * yh, axis=-1, keepdims=True))).astype(BF16)
        dgp = jnp.sum(dout * yh, axis=0, keepdims=True)
        lane0 = lax.broadcasted_iota(jnp.int32, (1, LANE), 1) == 0
        lp = jnp.where(lane0, part, 0.0)

        @pl.when(i == 0)
        def _():
            dg_ref[...] = dgp
            loss_ref[...] = lp

        @pl.when(i > 0)
        def _():
            dg_ref[...] += dgp
            loss_ref[...] += lp

    return pl.pallas_call(
        body, name=name, grid=(n // tm,),
        in_specs=[pl.BlockSpec((tm, k), lambda i: (i, 0)), pl.BlockSpec((k, d), lambda i: (0, 0)),
                  pl.BlockSpec((tm, d), lambda i: (i, 0)), pl.BlockSpec((1, d), lambda i: (0, 0)),
                  pl.BlockSpec((tm, d), lambda i: (i, 0))],
        out_specs=[pl.BlockSpec((tm, d), lambda i: (i, 0)), pl.BlockSpec((tm, d), lambda i: (i, 0)),
                   pl.BlockSpec((1, d), lambda i: (0, 0)), pl.BlockSpec((1, LANE), lambda i: (0, 0))],
        out_shape=[SDS((n, d), BF16), SDS((n, d), F32), SDS((1, d), F32), SDS((1, LANE), F32)],
        compiler_params=_params(("arbitrary",)),
    )(a, w, res, g, target)


def _accumulate(ref, val, step):
    @pl.when(step == 0)
    def _():
        ref[...] = val

    @pl.when(step > 0)
    def _():
        ref[...] += val


def mm_nt(terms, ws, name, tm=512, out_dtype=F32, ride=None, epilogue=None, by_rows=False):
    n = terms[0][0].shape[0]
    r = ws[0].shape[1 if by_rows else 0]
    na = len(terms)
    meta = [(widx, off, a.shape[1]) for a, widx, off in terms]
    fn, extras, out_shape = epilogue if epilogue else (None, [], [SDS((n, r), out_dtype)])
    n_fixed = na + len(ws)

    def body(*refs):
        a_refs = refs[:na]
        w_refs = refs[na:n_fixed]
        acc = None
        for a_ref, (widx, off, k) in zip(a_refs, meta):
            a = a_ref[...].astype(BF16)
            p = _dot(a, w_refs[widx][off:off + k, :]) if by_rows else _dot_nt(a, w_refs[widx][:, off:off + k])
            acc = p if acc is None else acc + p
        if fn is None:
            refs[-1][...] = acc.astype(out_dtype)
        else:
            fn(acc, pl.program_id(0), *refs[n_fixed:])

    def spec(a):
        if a.shape[0] == 1:
            return pl.BlockSpec(a.shape, lambda i: (0, 0))
        return pl.BlockSpec((tm, a.shape[1]), lambda i: (i, 0))

    return _call(
        body, name=name, grid=(n // tm,),
        in_specs=[spec(a) for a, _, _ in terms] + [pl.BlockSpec(w.shape, lambda i: (0, 0)) for w in ws]
        + [spec(e) for e in extras],
        out_specs=[spec(o) for o in out_shape], out_shape=out_shape,
        args=[a for a, _, _ in terms] + list(ws) + list(extras),
        sem=("parallel",) if fn is None else ("arbitrary",), ride=ride)


def _piece_blocks(pieces, tile):
    out, first = [], 0
    for p in pieces:
        nblk, rem = divmod(p.shape[1], tile)
        assert rem == 0, (p.shape, tile)
        out.append((first, nblk))
        first += nblk
    return out, first


def mm_tn(lhs_list, rhs_list, name, t1, t2, out_dtype=BF16):
    n = lhs_list[0].shape[0]
    lblocks, nbl = _piece_blocks(lhs_list, t1)
    rblocks, nbr = _piece_blocks(rhs_list, t2)
    nl = len(lhs_list)

    def body(*refs):
        l_refs, r_refs, o_ref = refs[:nl], refs[nl:-1], refs[-1]
        i, j = pl.program_id(0), pl.program_id(1)
        for l_ref, (ls, ln) in zip(l_refs, lblocks):
            for r_ref, (rs, rn) in zip(r_refs, rblocks):
                @pl.when((i >= ls) & (i < ls + ln) & (j >= rs) & (j < rs + rn))
                def _(l_ref=l_ref, r_ref=r_ref):
                    o_ref[...] = _dot_tn(l_ref[...].astype(BF16), r_ref[...].astype(BF16)).astype(out_dtype)

    def piece_spec(tile, axis, first, nblk):
        def index(i, j):
            return 0, jnp.clip((i, j)[axis] - first, 0, nblk - 1)
        return pl.BlockSpec((n, tile), index)

    return pl.pallas_call(
        body, name=name, grid=(nbl, nbr),
        in_specs=[piece_spec(t1, 0, *b) for b in lblocks] + [piece_spec(t2, 1, *b) for b in rblocks],
        out_specs=pl.BlockSpec((t1, t2), lambda i, j: (i, j)),
        out_shape=SDS((nbl * t1, nbr * t2), out_dtype),
        compiler_params=_params(("parallel", "arbitrary")),
    )(*lhs_list, *rhs_list)


GATE_ROWS = 512


def _tril_mask():
    row = lax.broadcasted_iota(jnp.int32, (CHUNK, CHUNK), 0)
    col = lax.broadcasted_iota(jnp.int32, (CHUNK, CHUNK), 1)
    return row >= col


def _group_of(shape, axis):
    return lax.broadcasted_iota(jnp.int32, shape, axis) // HEAD_DIM


def _group_mean_matrix():
    same = _group_of((A_WIDTH, A_WIDTH), 0) == _group_of((A_WIDTH, A_WIDTH), 1)
    return jnp.where(same, 1.0 / HEAD_DIM, 0.0).astype(BF16)


def _dot_sum(a, b):
    hi = a.astype(BF16)
    lo = (a - hi.astype(F32)).astype(BF16)
    return _dot(hi, b) + _dot(lo, b)


def _by_group(parts, lane_group):
    out = parts[A_GROUPS - 1]
    for g in range(A_GROUPS - 2, -1, -1):
        out = jnp.where(lane_group == g, parts[g], out)
    return out


def _group_norm(gv, gmean):
    xc = gv - _dot_sum(gv, gmean)
    rstd = lax.rsqrt(_dot_sum(xc * xc, gmean) + NORM_EPS)
    return xc * rstd, rstd


def gating_fwd(proj, lng, lnb, ws, sbt, name):
    n = proj.shape[0]

    def body(u_ref, v_ref, lng_ref, lnb_ref, ws_ref, sbt_ref, a_ref):
        tril = _tril_mask()
        lane_group = _group_of((CHUNK, A_WIDTH), 1)
        gmean = _group_mean_matrix()
        wts = [jnp.where(tril, ws_ref[g], 0.0).astype(BF16) for g in range(A_GROUPS)]
        sb = _by_group([sbt_ref[:, g:g + 1] for g in range(A_GROUPS)], lane_group)

        def chunk(c, carry):
            rows = pl.ds(pl.multiple_of(c * CHUNK, CHUNK), CHUNK)
            vhat, _ = _group_norm(_gelu(v_ref[rows, :]), gmean)
            vn = (vhat * lng_ref[...] + lnb_ref[...]).astype(BF16)
            z = _by_group([_dot(wt, vn) for wt in wts], lane_group) + sb
            a_ref[rows, :] = (_gelu(u_ref[rows, :]) * z).astype(BF16)
            return carry

        lax.fori_loop(0, GATE_ROWS // CHUNK, chunk, 0)

    return pl.pallas_call(
        body, name=name, grid=(n // GATE_ROWS,),
        in_specs=[pl.BlockSpec((GATE_ROWS, A_WIDTH), lambda i: (i, 0)),
                  pl.BlockSpec((GATE_ROWS, A_WIDTH), lambda i: (i, 1)),
                  pl.BlockSpec((1, A_WIDTH), lambda i: (0, 0)), pl.BlockSpec((1, A_WIDTH), lambda i: (0, 0)),
                  pl.BlockSpec((A_GROUPS, CHUNK, CHUNK), lambda i: (0, 0, 0)),
                  pl.BlockSpec((CHUNK, A_GROUPS), lambda i: (0, 0))],
        out_specs=pl.BlockSpec((GATE_ROWS, A_WIDTH), lambda i: (i, 0)),
        out_shape=SDS((n, A_WIDTH), BF16),
        compiler_params=_params(("parallel",)),
    )(proj, proj, lng, lnb, ws, sbt)


def gating_bwd(proj, dmix, lng, lnb, ws, sbt, name):
    n = proj.shape[0]

    def body(u_ref, v_ref, da_ref, lng_ref, lnb_ref, ws_ref, sbt_ref,
             duv_ref, dws_ref, dsbt_ref, dlng_ref, dlnb_ref):
        @pl.when(pl.program_id(0) == 0)
        def _():
            dws_ref[...] = jnp.zeros_like(dws_ref)
            dsbt_ref[...] = jnp.zeros_like(dsbt_ref)
            dlng_ref[...] = jnp.zeros_like(dlng_ref)
            dlnb_ref[...] = jnp.zeros_like(dlnb_ref)

        tril = _tril_mask()
        lane_group = _group_of((CHUNK, A_WIDTH), 1)
        gmean = _group_mean_matrix()
        gsum = (_group_of((A_WIDTH, LANE), 0) == lax.broadcasted_iota(jnp.int32, (A_WIDTH, LANE), 1)).astype(BF16)
        wts = [jnp.where(tril, ws_ref[g], 0.0) for g in range(A_GROUPS)]
        wts_b = [w.astype(BF16) for w in wts]
        wts_t = [w.T.astype(BF16) for w in wts]
        sb = _by_group([sbt_ref[:, g:g + 1] for g in range(A_GROUPS)], lane_group)
        lg = lng_ref[...]

        def chunk(c, carry):
            rows = pl.ds(pl.multiple_of(c * CHUNK, CHUNK), CHUNK)
            gu, dgu_dx = _gelu_and_grad(u_ref[rows, :])
            gv, dgv_dx = _gelu_and_grad(v_ref[rows, :])
            vhat, rstd = _group_norm(gv, gmean)
            vn = (vhat * lg + lnb_ref[...]).astype(BF16)
            z = _by_group([_dot(wt, vn) for wt in wts_b], lane_group) + sb
            da = da_ref[rows, :]
            dz = da * gu
            dzb = dz.astype(BF16)
            duv_ref[rows, 0:A_WIDTH] = (da * z * dgu_dx).astype(BF16)
            dsbt_ref[...] += _dot_sum(dz, gsum)[:, 0:A_GROUPS]
            for g in range(A_GROUPS):
                dz_g = jnp.where(lane_group == g, dzb, jnp.zeros_like(dzb))
                dws_ref[g] += jnp.where(tril, _dot_nt(dz_g, vn), 0.0)
            dvn = _by_group([_dot(wt, dzb) for wt in wts_t], lane_group)
            dlng_ref[...] += jnp.sum(dvn * vhat, axis=0, keepdims=True)
            dlnb_ref[...] += jnp.sum(dvn, axis=0, keepdims=True)
            dvh = dvn * lg
            dgv = rstd * (dvh - _dot_sum(dvh, gmean) - vhat * _dot_sum(dvh * vhat, gmean))
            duv_ref[rows, A_WIDTH:2 * A_WIDTH] = (dgv * dgv_dx).astype(BF16)
            return carry

        lax.fori_loop(0, GATE_ROWS // CHUNK, chunk, 0)

    return pl.pallas_call(
        body, name=name, grid=(n // GATE_ROWS,),
        in_specs=[pl.BlockSpec((GATE_ROWS, A_WIDTH), lambda i: (i, 0)),
                  pl.BlockSpec((GATE_ROWS, A_WIDTH), lambda i: (i, 1)),
                  pl.BlockSpec((GATE_ROWS, A_WIDTH), lambda i: (i, 0)),
                  pl.BlockSpec((1, A_WIDTH), lambda i: (0, 0)), pl.BlockSpec((1, A_WIDTH), lambda i: (0, 0)),
                  pl.BlockSpec((A_GROUPS, CHUNK, CHUNK), lambda i: (0, 0, 0)),
                  pl.BlockSpec((CHUNK, A_GROUPS), lambda i: (0, 0))],
        out_specs=[pl.BlockSpec((GATE_ROWS, 2 * A_WIDTH), lambda i: (i, 0)),
                   pl.BlockSpec((A_GROUPS, CHUNK, CHUNK), lambda i: (0, 0, 0)),
                   pl.BlockSpec((CHUNK, A_GROUPS), lambda i: (0, 0)),
                   pl.BlockSpec((1, A_WIDTH), lambda i: (0, 0)), pl.BlockSpec((1, A_WIDTH), lambda i: (0, 0))],
        out_shape=[SDS((n, 2 * A_WIDTH), BF16), SDS((A_GROUPS, CHUNK, CHUNK), F32), SDS((CHUNK, A_GROUPS), F32),
                   SDS((1, A_WIDTH), F32), SDS((1, A_WIDTH), F32)],
        compiler_params=_params(("arbitrary",)),
    )(proj, proj, dmix, lng, lnb, ws, sbt)


def _t5_bucket_np(dist):
    max_exact = NUM_BUCKETS // 2
    dd = np.maximum(dist, 1).astype(np.float64)
    large = max_exact + np.log(dd / max_exact) / math.log(MAX_DISTANCE / max_exact) * (NUM_BUCKETS - max_exact)
    large = np.minimum(large.astype(np.int64), NUM_BUCKETS - 1)
    return np.where(dist < max_exact, dist, large)


def _bucket_tables(with_first):
    i = np.arange(ATTN_BLOCK)[:, None]
    j = np.arange(2 * ATTN_BLOCK)[None, :]
    rel = ATTN_BLOCK + i - j
    band = (rel >= 0) & (rel <= ATTN_BLOCK)
    tabs = []
    for own_only in (False, True) if with_first else (False,):
        for dil in DILATIONS:
            b = _t5_bucket_np(np.maximum(rel, 0) * dil)
            tabs.append(np.where(band & (j >= ATTN_BLOCK) if own_only else band, b, -1).reshape(1, -1))
    return np.stack(tabs).astype(np.float32)


BIAS_SIZE = ATTN_BLOCK * 2 * ATTN_BLOCK


def bias_tables(rel_bias_t, name, ride=None):
    idx = jnp.asarray(_bucket_tables(True).reshape(-1, ATTN_BLOCK, 2 * ATTN_BLOCK))
    ntab = idx.shape[0]

    def body(rb_ref, idx_ref, o_ref):
        iv = idx_ref[0]

        def head(h, carry):
            t = jnp.full(iv.shape, NEG_INF, F32)
            for b in range(NUM_BUCKETS):
                t = jnp.where(iv == float(b), rb_ref[h, b], t)
            o_ref[0, h] = t
            return carry

        lax.fori_loop(0, B_HEADS, head, 0)

    return _call(
        body, name=name, grid=(ntab,),
        in_specs=[pl.BlockSpec(memory_space=pltpu.SMEM),
                  pl.BlockSpec((1, ATTN_BLOCK, 2 * ATTN_BLOCK), lambda d: (d, 0, 0))],
        out_specs=[pl.BlockSpec((1, B_HEADS, ATTN_BLOCK, 2 * ATTN_BLOCK), lambda d: (d, 0, 0, 0))],
        out_shape=[SDS((ntab, B_HEADS, ATTN_BLOCK, 2 * ATTN_BLOCK), F32)],
        args=[rel_bias_t, idx], sem=("parallel",), ride=ride)


def rel_bias_grad(dbias, name):
    idx = jnp.asarray(_bucket_tables(False))

    def body(db_ref, idx_ref, o_ref):
        d = pl.program_id(0)
        iv = idx_ref[0]
        bk = lax.broadcasted_iota(jnp.int32, (NUM_BUCKETS, BIAS_SIZE), 0).astype(F32)
        onehot = (bk == iv).astype(F32)
        part = lax.dot_general(db_ref[0], onehot, (((1,), (1,)), ((), ())),
                               preferred_element_type=F32, precision=lax.Precision.HIGHEST)

        @pl.when(d == 0)
        def _():
            o_ref[...] = part

        @pl.when(d > 0)
        def _():
            o_ref[...] += part

    return pl.pallas_call(
        body, name=name, grid=(len(DILATIONS),),
        in_specs=[pl.BlockSpec((1, B_HEADS, BIAS_SIZE), lambda d: (d, 0, 0)),
                  pl.BlockSpec((1, 1, BIAS_SIZE), lambda d: (d, 0, 0))],
        out_specs=pl.BlockSpec((B_HEADS, NUM_BUCKETS), lambda d: (0, 0)),
        out_shape=SDS((B_HEADS, NUM_BUCKETS), F32),
        compiler_params=_params(("arbitrary",)),
    )(dbias, idx)


QK_SCALE = 1.0 / math.sqrt(HEAD_DIM)


def _attn_scores(q_scaled, kk, bias):
    return _dot_nt(q_scaled, kk) + bias


def _head0_lanes():
    return lax.broadcasted_iota(jnp.int32, (ATTN_BLOCK, LANE), 1) < HEAD_DIM


def _one_head(x2, head0, hh):
    return jnp.where(head0 if hh == 0 else jnp.logical_not(head0), x2, 0.0).astype(BF16)


QUAD = 4
QUAD_ROWS = SEQ // QUAD


def _deinterleave(src_ref, dst_ref):
    for r in range(QUAD):
        for c in range(QUAD_ROWS // ATTN_BLOCK):
            dst_ref[r, c * ATTN_BLOCK:(c + 1) * ATTN_BLOCK, :] = src_ref[
                pl.ds(r + c * QUAD * ATTN_BLOCK, ATTN_BLOCK, stride=QUAD), :]


def _deinterleave_again(src_ref, dst_ref):
    for r in range(QUAD):
        for s in range(QUAD):
            dst_ref[r + QUAD * s] = src_ref[r, pl.ds(s, ATTN_BLOCK, stride=QUAD), :]


def _interleave_back(src_ref, dst_ref, slot0, accumulate=False):
    for r in range(QUAD):
        for s in range(QUAD):
            rows = pl.ds(s, ATTN_BLOCK, stride=QUAD)
            if accumulate:
                dst_ref[slot0 + r, rows, :] += src_ref[r + QUAD * s]
            else:
                dst_ref[slot0 + r, rows, :] = src_ref[r + QUAD * s]


def _quad_tiles():
    return [(r, pl.ds(r + c * QUAD * ATTN_BLOCK, ATTN_BLOCK, stride=QUAD), slice(c * ATTN_BLOCK, (c + 1) * ATTN_BLOCK))
            for r in range(QUAD) for c in range(QUAD_ROWS // ATTN_BLOCK)]


def _attn_schedule(op):
    def d16(i, carry):
        for t in range(2 * QUAD):
            op(2, 2 * QUAD * i + t, 0, True)
        return carry

    lax.fori_loop(0, QUAD // 2, d16, 0)

    def d4(i, carry):
        for u in range(2):
            for nq in range(QUAD_ROWS // ATTN_BLOCK):
                op(1, 2 * i + u, nq * ATTN_BLOCK, nq == 0)
        return carry

    lax.fori_loop(0, QUAD // 2, d4, 0)
    op(0, None, 0, True)
    per_pass = 5

    def d1(j, carry):
        for t in range(per_pass):
            op(0, None, pl.multiple_of((1 + per_pass * j + t) * ATTN_BLOCK, ATTN_BLOCK), False)
        return carry

    lax.fori_loop(0, (SEQ // ATTN_BLOCK - 1) // per_pass, d1, 0)


def _keys(src, krows, first):
    kb = src[krows, :].astype(BF16)
    return jnp.concatenate([kb, kb], axis=0) if first else kb


def _table(seg, first):
    return len(DILATIONS) + seg if first else seg


def _kv_rows(start, first):
    return pl.ds(start, ATTN_BLOCK) if first else pl.ds(start - ATTN_BLOCK, 2 * ATTN_BLOCK)


MERGE_ROWS = 256


def attn_fwd(proj, bias, nb_local, name, ride=None):
    n = proj.shape[0]
    nseg = len(DILATIONS)

    def body(q_ref, k_ref, v_ref, b_ref, o_ref, lse_ref, q4_ref, k4_ref, v4_ref, os0_ref, ls0_ref, os4_ref, ls4_ref,
             q16_ref, k16_ref, v16_ref, os16_ref, ls16_ref):
        for src, mid, dst in ((q_ref, q4_ref, q16_ref), (k_ref, k4_ref, k16_ref), (v_ref, v4_ref, v16_ref)):
            _deinterleave(src, mid)
            _deinterleave_again(mid, dst)

        def op(seg, r, start, first):
            qrows = pl.ds(start, ATTN_BLOCK)
            krows = _kv_rows(start, first)
            if seg == 0:
                q_src, k_src, v_src, o_dst, l_dst = q_ref, k_ref, v_ref, os0_ref, ls0_ref
            elif seg == 1:
                q_src, k_src, v_src = q4_ref.at[r], k4_ref.at[r], v4_ref.at[r]
                o_dst, l_dst = os4_ref.at[r], ls4_ref.at[r]
            else:
                q_src, k_src, v_src = q16_ref.at[r], k16_ref.at[r], v16_ref.at[r]
                o_dst, l_dst = os16_ref.at[r], ls16_ref.at[r]
            q2, kb, vb = q_src[qrows, :] * QK_SCALE, _keys(k_src, krows, first), _keys(v_src, krows, first)
            head0 = _head0_lanes()
            outs, lses = [], []
            for hh in range(2):
                s = _attn_scores(_one_head(q2, head0, hh), kb, b_ref[_table(seg, first), hh])
                m = jnp.max(s, axis=-1, keepdims=True)
                p = jnp.exp(s - m)
                l = jnp.sum(p, axis=-1, keepdims=True)
                outs.append(_dot(p.astype(BF16), vb) / l)
                lses.append(jnp.broadcast_to(m + jnp.log(l), (ATTN_BLOCK, LANE)))
            o_dst[qrows, :] = jnp.where(head0, outs[0], outs[1])
            l_dst[qrows, :] = jnp.where(head0, lses[0], lses[1])

        _attn_schedule(op)
        _interleave_back(os16_ref, os4_ref, QUAD)
        _interleave_back(ls16_ref, ls4_ref, QUAD)

        for r, nat, quad in _quad_tiles():
            ls = [ls0_ref[nat, :], ls4_ref[r, quad, :], ls4_ref[QUAD + r, quad, :]]
            m = functools.reduce(jnp.maximum, ls)
            ws = [jnp.exp(l - m) for l in ls]
            den = ws[0] + ws[1] + ws[2]
            num = ws[0] * os0_ref[nat, :] + ws[1] * os4_ref[r, quad, :] + ws[2] * os4_ref[QUAD + r, quad, :]
            o_ref[nat, :] = num / den
            lse_ref[nat, :] = m + jnp.log(den)

    def in_spec(off):
        return pl.BlockSpec((SEQ, LANE), lambda b, p: (b, off // LANE + p))

    out_spec = pl.BlockSpec((SEQ, LANE), lambda b, p: (b, p))
    return _call(
        body, name=name, grid=(nb_local, HEAD_PAIRS),
        in_specs=[in_spec(Q_OFF), in_spec(K_OFF), in_spec(V_OFF),
                  pl.BlockSpec((2 * nseg, 2, ATTN_BLOCK, 2 * ATTN_BLOCK), lambda b, p: (0, p, 0, 0))],
        out_specs=[out_spec, out_spec],
        out_shape=[SDS((n, B_WIDTH), F32), SDS((n, B_WIDTH), F32)],
        scratch_shapes=[pltpu.VMEM((QUAD, QUAD_ROWS, LANE), F32)] * 3 + [pltpu.VMEM((SEQ, LANE), F32)] * 2
        + [pltpu.VMEM((2 * QUAD, QUAD_ROWS, LANE), F32)] * 2 + [pltpu.VMEM((QUAD * QUAD, ATTN_BLOCK, LANE), F32)] * 5,
        args=[proj, proj, proj, bias], sem=("parallel", "arbitrary"), ride=ride)


def attn_bwd(proj, b_out, dmix, lse_tot, bias, nb_local, name, ride=None):
    n = proj.shape[0]
    nseg = len(DILATIONS)
    a_blocks = A_WIDTH // LANE

    def body(q_ref, k_ref, v_ref, o_ref, do_ref, lse_ref, b_ref, dq_ref, dk_ref, dv_ref, db_ref,
             dqs_ref, delta_ref, dka_ref, dva_ref, q4_ref, k4_ref, v4_ref, do4_ref, lse4_ref, delta4_ref,
             dqs4_ref, dk4_ref, dv4_ref, q16_ref, k16_ref, v16_ref, do16_ref, lse16_ref, delta16_ref,
             dqs16_ref, dk16_ref, dv16_ref):
        @pl.when(pl.program_id(1) == 0)
        def _():
            db_ref[...] = jnp.zeros_like(db_ref)

        quads = (q4_ref, k4_ref, v4_ref, do4_ref, lse4_ref, delta4_ref)
        hexes = (q16_ref, k16_ref, v16_ref, do16_ref, lse16_ref, delta16_ref)

        head_sum = (lax.broadcasted_iota(jnp.int32, (LANE, LANE), 0) // HEAD_DIM
                    == lax.broadcasted_iota(jnp.int32, (LANE, LANE), 1) // HEAD_DIM).astype(BF16)

        def row_dots(i, carry):
            rows = pl.ds(pl.multiple_of(i * MERGE_ROWS, MERGE_ROWS), MERGE_ROWS)
            delta_ref[rows, :] = _dot_sum(do_ref[rows, :] * o_ref[rows, :], head_sum)
            return carry

        lax.fori_loop(0, SEQ // MERGE_ROWS, row_dots, 0)
        for src, mid, dst in zip((q_ref, k_ref, v_ref, do_ref, lse_ref, delta_ref), quads, hexes):
            _deinterleave(src, mid)
            _deinterleave_again(mid, dst)

        def op(seg, r, start, first):
            qrows = pl.ds(start, ATTN_BLOCK)
            krows = _kv_rows(start, first)
            if seg == 0:
                srcs = (q_ref, k_ref, v_ref, do_ref, lse_ref, delta_ref)
                dq_dst, dk_dst, dv_dst = dqs_ref, dka_ref, dva_ref
            elif seg == 1:
                srcs = tuple(x.at[r] for x in quads)
                dq_dst, dk_dst, dv_dst = dqs4_ref.at[r], dk4_ref.at[r], dv4_ref.at[r]
            else:
                srcs = tuple(x.at[r] for x in hexes)
                dq_dst, dk_dst, dv_dst = dqs16_ref.at[r], dk16_ref.at[r], dv16_ref.at[r]
            q_src, k_src, v_src, do_src, lse_src, delta_src = srcs
            q2, kb, vb = q_src[qrows, :] * QK_SCALE, _keys(k_src, krows, first), _keys(v_src, krows, first)
            do2, lse2, delta2 = do_src[qrows, :], lse_src[qrows, :], delta_src[qrows, :]
            head0 = _head0_lanes()
            dqs, dk, dv = [], None, None
            for hh in range(2):
                col = slice(hh * HEAD_DIM, hh * HEAD_DIM + 1)
                q, dob = _one_head(q2, head0, hh), _one_head(do2, head0, hh)
                p = jnp.exp(_attn_scores(q, kb, b_ref[_table(seg, first), hh]) - lse2[:, col])
                dvh = _dot_tn(p.astype(BF16), dob)
                ds = p * (_dot_nt(dob, vb) - delta2[:, col])
                if first:
                    db_ref[seg, hh, :, ATTN_BLOCK:] += ds[:, ATTN_BLOCK:]
                else:
                    db_ref[seg, hh] += ds
                dsb = ds.astype(BF16)
                dqs.append(_dot(dsb, kb))
                dkh = _dot_tn(dsb, q)
                dk = dkh if dk is None else dk + dkh
                dv = dvh if dv is None else dv + dvh
            dq_dst[qrows, :] = jnp.where(head0, dqs[0], dqs[1]) * QK_SCALE
            dk_dst[qrows, :] = dk[ATTN_BLOCK:]
            dv_dst[qrows, :] = dv[ATTN_BLOCK:]
            if not first:
                before = pl.ds(start - ATTN_BLOCK, ATTN_BLOCK)
                dk_dst[before, :] += dk[:ATTN_BLOCK]
                dv_dst[before, :] += dv[:ATTN_BLOCK]

        _attn_schedule(op)
        _interleave_back(dqs16_ref, dqs4_ref, QUAD)
        _interleave_back(dk16_ref, dk4_ref, 0, accumulate=True)
        _interleave_back(dv16_ref, dv4_ref, 0, accumulate=True)

        for r, nat, quad in _quad_tiles():
            dqs_ref[nat, :] += dqs4_ref[r, quad, :] + dqs4_ref[QUAD + r, quad, :]
            dka_ref[nat, :] += dk4_ref[r, quad, :]
            dva_ref[nat, :] += dv4_ref[r, quad, :]

        def merge(i, carry):
            rows = pl.ds(pl.multiple_of(i * MERGE_ROWS, MERGE_ROWS), MERGE_ROWS)
            dq_ref[rows, :] = dqs_ref[rows, :].astype(BF16)
            dk_ref[rows, :] = dka_ref[rows, :].astype(BF16)
            dv_ref[rows, :] = dva_ref[rows, :].astype(BF16)
            return carry

        lax.fori_loop(0, SEQ // MERGE_ROWS, merge, 0)

    def pspec(off):
        return pl.BlockSpec((SEQ, LANE), lambda p, b: (b, off // LANE + p))

    ospec = pl.BlockSpec((SEQ, LANE), lambda p, b: (b, p))
    bspec = pl.BlockSpec((nseg, 2, ATTN_BLOCK, 2 * ATTN_BLOCK), lambda p, b: (0, p, 0, 0))
    gshape = SDS((n, B_WIDTH), BF16)
    return _call(
        body, name=name, grid=(HEAD_PAIRS, nb_local),
        in_specs=[pspec(Q_OFF), pspec(K_OFF), pspec(V_OFF), ospec,
                  pl.BlockSpec((SEQ, LANE), lambda p, b: (b, a_blocks + p)), ospec,
                  pl.BlockSpec((2 * nseg, 2, ATTN_BLOCK, 2 * ATTN_BLOCK), lambda p, b: (0, p, 0, 0))],
        out_specs=[ospec, ospec, ospec, bspec],
        out_shape=[gshape, gshape, gshape, SDS((nseg, B_HEADS, ATTN_BLOCK, 2 * ATTN_BLOCK), F32)],
        scratch_shapes=[pltpu.VMEM((SEQ, LANE), F32)] * 4 + [pltpu.VMEM((QUAD, QUAD_ROWS, LANE), F32)] * 6
        + [pltpu.VMEM((2 * QUAD, QUAD_ROWS, LANE), F32)] + [pltpu.VMEM((QUAD, QUAD_ROWS, LANE), F32)] * 2
        + [pltpu.VMEM((QUAD * QUAD, ATTN_BLOCK, LANE), F32)] * 9,
        args=[proj, proj, proj, b_out, dmix, lse_tot, bias], sem=("arbitrary", "arbitrary"), ride=ride)


PAD = 8
CONV_ROWS = 64


CONV_LANES = 128


def _conv_taps(gp_ref, head_ref, r0, ls):
    g0 = gp_ref[r0:r0 + CONV_ROWS, ls]
    if r0 == 0:
        return g0, head_ref[PAD - 1:PAD - 1 + CONV_ROWS, ls], head_ref[PAD - 2:PAD - 2 + CONV_ROWS, ls]
    return g0, gp_ref[r0 - 1:r0 - 1 + CONV_ROWS, ls], gp_ref[r0 - 2:r0 - 2 + CONV_ROWS, ls]


def _fill_head(gp_ref, head_ref):
    head_ref[0:PAD, :] = jnp.zeros((PAD, CONV_LANES), F32)
    head_ref[PAD:PAD + CONV_ROWS, :] = gp_ref[0:CONV_ROWS, :]


def _lane_passes():
    return [slice(l0, l0 + LANE) for l0 in range(0, CONV_LANES, LANE)]


def conv_gelu_bwd(dgu, gp, up, cw, cb, nb_local, name, ride=None):
    n, f = gp.shape

    def fold(v):
        return jnp.sum(v.reshape(CONV_ROWS // 8, 8, LANE), axis=0)

    def body(dgu_ref, gp_ref, up_ref, cw_ref, cb_ref, dgp_ref, dup_ref, dcw_ref, dcb_ref, head_ref, dc_ref):
        b = pl.program_id(1)
        _fill_head(gp_ref, head_ref)
        dc_ref[SEQ:SEQ + PAD, :] = jnp.zeros((PAD, CONV_LANES), F32)
        for ls in _lane_passes():
            w0, w1, w2, bias = cw_ref[0:1, ls], cw_ref[1:2, ls], cw_ref[2:3, ls], cb_ref[:, ls]
            sums = [jnp.zeros((8, LANE), F32) for _ in range(4)]
            for r0 in range(0, SEQ, CONV_ROWS):
                rows = slice(r0, r0 + CONV_ROWS)
                g0, g1, g2 = _conv_taps(gp_ref, head_ref, r0, ls)
                gg, dgg = _gelu_and_grad(bias + w0 * g2 + w1 * g1 + w2 * g0)
                dgu = dgu_ref[rows, ls].astype(F32)
                dup_ref[rows, ls] = (dgu * gg).astype(BF16)
                dc = dgu * up_ref[rows, ls] * dgg
                dc_ref[rows, ls] = dc
                sums = [sums[0] + fold(dc * g2), sums[1] + fold(dc * g1), sums[2] + fold(dc * g0), sums[3] + fold(dc)]
            for r0 in range(0, SEQ, CONV_ROWS):
                dgp_ref[r0:r0 + CONV_ROWS, ls] = (
                    w2 * dc_ref[r0:r0 + CONV_ROWS, ls] + w1 * dc_ref[r0 + 1:r0 + 1 + CONV_ROWS, ls]
                    + w0 * dc_ref[r0 + 2:r0 + 2 + CONV_ROWS, ls]).astype(BF16)
            dcw = jnp.concatenate([jnp.sum(s, axis=0, keepdims=True) for s in sums[:3]], axis=0)
            dcb = jnp.sum(sums[3], axis=0, keepdims=True)

            @pl.when(b == 0)
            def _(dcw=dcw, dcb=dcb, ls=ls):
                dcw_ref[:, ls] = dcw
                dcb_ref[:, ls] = dcb

            @pl.when(b > 0)
            def _(dcw=dcw, dcb=dcb, ls=ls):
                dcw_ref[:, ls] += dcw
                dcb_ref[:, ls] += dcb

    blk = pl.BlockSpec((SEQ, CONV_LANES), lambda j, b: (b, j))
    wspec = pl.BlockSpec((3, CONV_LANES), lambda j, b: (0, j))
    bspec = pl.BlockSpec((1, CONV_LANES), lambda j, b: (0, j))
    return _call(
        body, name=name, grid=(f // CONV_LANES, nb_local),
        in_specs=[blk, blk, blk, wspec, bspec], out_specs=[blk, blk, wspec, bspec],
        out_shape=[SDS((n, f), BF16), SDS((n, f), BF16), SDS((3, f), F32), SDS((1, f), F32)],
        scratch_shapes=[pltpu.VMEM((PAD + CONV_ROWS, CONV_LANES), F32), pltpu.VMEM((SEQ + PAD, CONV_LANES), F32)],
        args=[dgu, gp, up, cw, cb], sem=("parallel", "arbitrary"), ride=ride)


def norm_mid_epilogue(x1, dout, z2, g3, g2):
    n, d = x1.shape

    def fn(dh2, step, x1_ref, dout_ref, z2_ref, g3_ref, g2_ref, dx1_ref, dz2_ref, dg3_ref, dg2_ref):
        dxa, dg3r = _rms_bwd(dh2, x1_ref[...], g3_ref[...])
        dx1 = dout_ref[...] + dxa
        dx1_ref[...] = dx1
        dz2, dg2r = _rms_bwd(dx1, z2_ref[...], g2_ref[...])
        dz2_ref[...] = dz2.astype(BF16)
        _accumulate(dg3_ref, jnp.sum(dg3r, axis=0, keepdims=True), step)
        _accumulate(dg2_ref, jnp.sum(dg2r, axis=0, keepdims=True), step)

    return fn, [x1, dout, z2, g3, g2], [SDS((n, d), F32), SDS((n, d), BF16), SDS((1, d), F32), SDS((1, d), F32)]


def norm_in_epilogue(x, dx1, g1):
    n, d = x.shape

    def fn(dh1, step, x_ref, dx1_ref, g1_ref, dx_ref, dg1_ref):
        dxa, dgr = _rms_bwd(dh1, x_ref[...], g1_ref[...])
        dx_ref[...] = dx1_ref[...] + dxa
        _accumulate(dg1_ref, jnp.sum(dgr, axis=0, keepdims=True), step)

    return fn, [x, dx1, g1], [SDS((n, d), F32), SDS((1, d), F32)]


def cast_bf16(arrays, name):
    def body(*refs):
        for i_ref, o_ref in zip(refs[:len(arrays)], refs[len(arrays):]):
            o_ref[...] = i_ref[...].astype(BF16)

    return pl.pallas_call(body, name=name, out_shape=[SDS(a.shape, BF16) for a in arrays],
                          compiler_params=_params())(*arrays)


def adam_update(parts, w, m, v, name, tr=None):
    s, r, c = parts.shape
    tr = r if tr is None else tr
    bc1 = 1.0 - ADAM_B1 ** ADAM_STEP
    bc2 = 1.0 - ADAM_B2 ** ADAM_STEP

    def body(p_ref, w_ref, m_ref, v_ref, g_ref, d_ref, nm_ref, nv_ref):
        g = p_ref[0].astype(F32)
        for j in range(1, s):
            g = g + p_ref[j].astype(F32)
        nm = ADAM_B1 * m_ref[...] + (1.0 - ADAM_B1) * g
        nv = ADAM_B2 * v_ref[...] + (1.0 - ADAM_B2) * (g * g)
        g_ref[...] = g
        nm_ref[...] = nm
        nv_ref[...] = nv
        d_ref[...] = -ADAM_LR * ((nm / bc1) / (jnp.sqrt(nv / bc2) + ADAM_EPS) + ADAM_WD * w_ref[...])

    blk = pl.BlockSpec((tr, c), lambda i: (i, 0))
    return pl.pallas_call(
        body, name=name, grid=(r // tr,),
        in_specs=[pl.BlockSpec((s, tr, c), lambda i: (0, i, 0)), blk, blk, blk],
        out_specs=[blk] * 4, out_shape=[SDS((r, c), F32)] * 4,
        compiler_params=_params(("parallel",)),
    )(parts, w, m, v)


EARLY_NAMES = ("spatial_w", "norm_mix_post", "norm_ffn_pre", "norm_ffn_post", "conv_b", "ln_v_gain", "ln_v_bias",
               "spatial_b")
LATE_NAMES = ("norm_mix_pre", "rel_bias")
PACK_ROW_ALIGN = 8


def _pack_rows(size):
    rows = -(-size // LANE)
    return -(-rows // PACK_ROW_ALIGN) * PACK_ROW_ALIGN


def _pack(arrays):
    flat = []
    for a in arrays:
        rows = _pack_rows(a.size)
        flat.append(jnp.pad(a.reshape(-1), (0, rows * LANE - a.size)))
    return jnp.concatenate(flat).reshape(-1, LANE)


def _unpack(packed, shapes):
    out, row = [], 0
    for shp in shapes:
        size = int(np.prod(shp))
        out.append(packed[row:row + _pack_rows(size)].reshape(-1)[:size].reshape(shp))
        row += _pack_rows(size)
    return out


def kernel(x, norm_mix_pre, norm_mix_post, norm_ffn_pre, norm_ffn_post, w_in, ln_v_gain, ln_v_bias, spatial_w, spatial_b, rel_bias, w_out, w_gate, w_up, conv_w, conv_b, w_down, loss_target, m_norm_mix_pre, m_norm_mix_post, m_norm_ffn_pre, m_norm_ffn_post, m_w_in, m_ln_v_gain, m_ln_v_bias, m_spatial_w, m_spatial_b, m_rel_bias, m_w_out, m_w_gate, m_w_up, m_conv_w, m_conv_b, m_w_down, v_norm_mix_pre, v_norm_mix_post, v_norm_ffn_pre, v_norm_ffn_post, v_w_in, v_ln_v_gain, v_ln_v_bias, v_spatial_w, v_spatial_b, v_rel_bias, v_w_out, v_w_gate, v_w_up, v_conv_w, v_conv_b, v_w_down):
    given = dict(locals())
    nb_local, seq, d = x.shape
    n = nb_local * seq
    cols = w_in.shape[2]

    def by_columns(g):
        return g.transpose(1, 0, 2).reshape(g.shape[1], N_DEV * g.shape[2])

    def by_rows(g):
        return g.reshape(N_DEV * g.shape[1], g.shape[2])

    def blocks(g):
        return g.reshape(N_DEV, g.shape[0] // N_DEV, g.shape[1])

    xf, target = x.reshape(n, d), loss_target.reshape(n, d)
    ln_g, ln_b = ln_v_gain.reshape(1, A_WIDTH), ln_v_bias.reshape(1, A_WIDTH)
    spatial_bt, rel_bias_t = spatial_b[0].T, rel_bias.T

    s_in, s_out, s_gate, s_up, s_down = cast_bf16(
        [w_in[0].T, w_out[0], w_gate[0].T, w_up[0].T, w_down[0]], "cast_shards")
    (bias,), (g_in, g_cw) = bias_tables(rel_bias_t, "bias_tables", ride=([], [s_in, conv_w[0]]))
    w_in_t, conv_w_f = by_rows(g_in), by_columns(g_cw)

    (h1, proj), _ = norm_mm(xf, norm_mix_pre, [w_in_t], "fwd_norm_in", tn=IN_COLS)
    a = gating_fwd(proj, ln_g, ln_b, spatial_w[0], spatial_bt, "fwd_gating")
    (b_out, lse_tot), (g_out, g_gate, g_up) = attn_fwd(proj, bias, nb_local, "fwd_attn",
                                                       ride=([], [s_out, s_gate, s_up]))
    w_out_f, w_gate_t, w_up_t = by_rows(g_out), by_rows(g_gate), by_rows(g_up)
    z2, x1 = mm_res_norm([a, b_out], w_out_f, xf, norm_mix_post, "fwd_out_norm")
    (h2, gp, up, gu), (g_down,) = norm_mm(x1, norm_ffn_pre, [w_gate_t, w_up_t], "fwd_norm_ffn_conv", tm=256, tn=D_FF,
                                          ride=([], [s_down]), conv=(conv_w_f, conv_b))
    w_down_f = by_rows(g_down)
    dy, dout, dg4, loss_part = down_loss(gu, w_down_f, x1, norm_ffn_post, target, "fwd_down_loss")

    p_down = mm_tn([gu], [dy], "bwd_dw_down", t1=256, t2=D_MODEL)
    (dgu,), _ = mm_nt([(dy, 0, 0)], [w_down_f], "bwd_dgu", out_dtype=BF16)
    (dgp, dup, p_conv_w, p_conv_b), (r_down,) = conv_gelu_bwd(
        dgu, gp, up, conv_w_f, conv_b, nb_local, "bwd_conv_gelu", ride=([blocks(p_down)], []))
    p_gate = mm_tn([dgp], [h2], "bwd_dw_gate", t1=256, t2=D_MODEL)
    p_up = mm_tn([dup], [h2], "bwd_dw_up", t1=256, t2=D_MODEL)
    (dx1, dz2, dg3, dg2), _ = mm_nt([(dgp, 0, 0), (dup, 1, 0)], [w_gate_t, w_up_t], "bwd_dh2_norm_mid", tm=256,
                                    by_rows=True,
                                    epilogue=norm_mid_epilogue(x1, dout, z2, norm_ffn_pre, norm_mix_post))
    p_out = mm_tn([a, b_out], [dz2], "bwd_dw_out", t1=256, t2=D_MODEL)
    (dmix,), _ = mm_nt([(dz2, 0, 0)], [w_out_f], "bwd_dmix")
    duv, p_ws, p_sbt, p_lng, p_lnb = gating_bwd(proj, dmix, ln_g, ln_b, spatial_w[0], spatial_bt, "bwd_gating")
    small = dict(spatial_w=p_ws, norm_mix_post=dg2, norm_ffn_pre=dg3, norm_ffn_post=dg4, conv_b=p_conv_b,
                 ln_v_gain=p_lng, ln_v_bias=p_lnb, spatial_b=p_sbt.T)
    pack_early = _pack([small[k] for k in EARLY_NAMES] + [p_conv_w, loss_part])
    (dq, dk, dv, dbias), (r_gate, r_up, r_out, r_early) = attn_bwd(
        proj, b_out, dmix, lse_tot, bias, nb_local, "bwd_attn",
        ride=([blocks(p_gate), blocks(p_up), blocks(p_out)], [pack_early]))
    p_rel_bias_t = rel_bias_grad(dbias.reshape(len(DILATIONS), B_HEADS, BIAS_SIZE), "bwd_rel_bias")
    p_in = mm_tn([duv, dq, dk, dv], [h1], "bwd_dw_in", t1=256, t2=D_MODEL)
    (grad_x, dg1), (r_in,) = mm_nt(
        [(duv, 0, 0), (dq, 0, Q_OFF), (dk, 0, K_OFF), (dv, 0, V_OFF)], [w_in_t], "bwd_dh1_norm_in", by_rows=True,
        epilogue=norm_in_epilogue(xf, dx1, norm_mix_pre), ride=([blocks(p_in)], []))
    small.update(norm_mix_pre=dg1, rel_bias=p_rel_bias_t.T)
    (r_late,) = exchange([], [_pack([small[k] for k in LATE_NAMES])], "exchange_late")

    res = {}
    for k, received in (("w_in", r_in), ("w_gate", r_gate), ("w_up", r_up)):
        res[k] = [o.T for o in adam_update(received, given[k][0].T, given["m_" + k][0].T, given["v_" + k][0].T,
                                           "adam_" + k, tr=cols // 2)]
    res["w_out"] = adam_update(r_out, w_out[0], m_w_out[0], v_w_out[0], "adam_w_out")
    res["w_down"] = adam_update(r_down, w_down[0], m_w_down[0], v_w_down[0], "adam_w_down", tr=cols // 2)

    def adam_packed(received, names, tail, name):
        zeros = [jnp.zeros_like(t) for t in tail]
        packs = [_pack([given[pre + k] for k in names] + zeros) for pre in ("", "m_", "v_")]
        shapes = [given[k].shape for k in names] + [t.shape for t in tail]
        unpacked = [_unpack(p, shapes) for p in adam_update(received, *packs, name)]
        for i, k in enumerate(names):
            res[k] = [u[i] for u in unpacked]
        return unpacked[0][len(names):]

    g_conv_w_full, loss_sum = adam_packed(r_early, EARLY_NAMES, [p_conv_w, loss_part], "adam_small_early")
    adam_packed(r_late, LATE_NAMES, [], "adam_small_late")
    g_conv_w = lax.dynamic_slice_in_dim(g_conv_w_full, _my_index() * cols, cols, axis=1)
    res["conv_w"] = adam_update(g_conv_w[None], conv_w[0], m_conv_w[0], v_conv_w[0], "adam_conv_w")
    loss = loss_sum[0, 0]

    names = ("norm_mix_pre", "norm_mix_post", "norm_ffn_pre", "norm_ffn_post", "w_in", "ln_v_gain", "ln_v_bias",
             "spatial_w", "spatial_b", "rel_bias", "w_out", "w_gate", "w_up", "conv_w", "conv_b", "w_down")
    outs = [loss, grad_x.reshape(x.shape)]
    for t in range(4):
        outs += [res[k][t].reshape(given[k].shape) for k in names]
    return tuple(outs)
```

```python
import functools
import math

import numpy as np
import jax
import jax.numpy as jnp
from jax import lax
from jax.experimental import pallas as pl
from jax.experimental.pallas import tpu as pltpu

F32 = jnp.float32
BF16 = jnp.bfloat16
SDS = jax.ShapeDtypeStruct

D_MODEL = 1024
SEQ = 2048
HEAD_DIM = 64
A_GROUPS = 4
A_WIDTH = A_GROUPS * HEAD_DIM
B_HEADS = 12
B_WIDTH = B_HEADS * HEAD_DIM
HEAD_PAIRS = B_HEADS // 2
CHUNK = 128
ATTN_BLOCK = 128
DILATIONS = (1, 4, 16)
NUM_BUCKETS = 32
MAX_DISTANCE = 2048
D_FF = 2816
IN_COLS = 2 * A_WIDTH + 3 * B_WIDTH
Q_OFF = 2 * A_WIDTH
K_OFF = Q_OFF + B_WIDTH
V_OFF = K_OFF + B_WIDTH
NORM_EPS = 1e-6
NEG_INF = -1e30
N_DEV = 8
LANE = 128

ADAM_LR = 0.001
ADAM_B1 = 0.9
ADAM_B2 = 0.999
ADAM_EPS = 1e-08
ADAM_WD = 0.01
ADAM_STEP = 10

GELU_C0 = math.sqrt(2.0 / math.pi)
GELU_C1 = 0.044715

VMEM_LIMIT = 56 * 1024 * 1024


def _params(sem=None):
    if sem is None:
        return pltpu.CompilerParams(vmem_limit_bytes=VMEM_LIMIT)
    return pltpu.CompilerParams(dimension_semantics=sem, vmem_limit_bytes=VMEM_LIMIT)


def _gelu(x):
    t = jnp.tanh(x * (GELU_C0 + (GELU_C0 * GELU_C1) * (x * x)))
    return x * (0.5 + 0.5 * t)


def _gelu_and_grad(x):
    x2 = x * x
    t = jnp.tanh(x * (GELU_C0 + (GELU_C0 * GELU_C1) * x2))
    half = 0.5 + 0.5 * t
    dg = half + x * (0.5 - 0.5 * (t * t)) * (GELU_C0 + (3.0 * GELU_C0 * GELU_C1) * x2)
    return x * half, dg


def _dot(a, b):
    return jnp.dot(a, b, preferred_element_type=F32)


def _dot_nt(a, b):
    return lax.dot_general(a, b, (((1,), (1,)), ((), ())), preferred_element_type=F32)


def _dot_tn(a, b):
    return lax.dot_general(a, b, (((0,), (0,)), ((), ())), preferred_element_type=F32)


def _rms_bwd(d, xin, g):
    r = lax.rsqrt(jnp.mean(xin * xin, axis=-1, keepdims=True) + NORM_EPS)
    xh = xin * r
    gd = g * d
    dx = r * (gd - xh * jnp.mean(gd * xh, axis=-1, keepdims=True))
    return dx, d * xh


MESH = pl.DeviceIdType.MESH
ANY = pl.BlockSpec(memory_space=pl.ANY)
PEER_MASKS = tuple(range(1, N_DEV))


def _my_index():
    return lax.axis_index("x") * 4 + lax.axis_index("y") * 2 + lax.axis_index("c")


def _peer(mask):
    x, y, c = lax.axis_index("x"), lax.axis_index("y"), lax.axis_index("c")
    px = 1 - x if mask & 4 else x
    py = 1 - y if mask & 2 else y
    pc = 1 - c if mask & 1 else c
    return (px, py, pc), px * 4 + py * 2 + pc


RELAY_AT = 3
SIBLING = 1
CHIP_MASKS = (2, 4, 6)


class _Exchange:
    def __init__(self, nblocked, in_refs, out_refs, sems):
        send_sems, recv_sems, local_sems = sems
        me = _my_index()
        sibling, _ = _peer(SIBLING)
        self.local, self.first, self.relays, self.relayed_in, self.last_in = [], [], [], [], []
        for a, (in_ref, out_ref) in enumerate(zip(in_refs, out_refs)):
            def copy(src, slot, mask, to):
                return pltpu.make_async_remote_copy(
                    src_ref=src, dst_ref=out_ref.at[slot], send_sem=send_sems.at[a, mask - 1],
                    recv_sem=recv_sems.at[a, mask - 1], device_id=to, device_id_type=MESH)

            if a < nblocked:
                self.local.append(pltpu.make_async_copy(in_ref.at[me], out_ref.at[me], local_sems.at[a]))
                for mask in PEER_MASKS:
                    peer, pidx = _peer(mask)
                    self.first.append(copy(in_ref.at[pidx], me, mask, peer))
                    self.last_in.append(copy(in_ref.at[pidx], pidx, mask, peer))
                continue
            self.local.append(pltpu.make_async_copy(in_ref, out_ref.at[me], local_sems.at[a]))
            for mask in (SIBLING,) + CHIP_MASKS:
                peer, pidx = _peer(mask)
                self.first.append(copy(in_ref, me, mask, peer))
                (self.last_in if mask == SIBLING else self.relayed_in).append(copy(in_ref, pidx, mask, peer))
            for mask in CHIP_MASKS:
                _, origin = _peer(mask)
                _, far = _peer(mask | SIBLING)
                self.relays.append(copy(out_ref.at[origin], origin, mask | SIBLING, sibling))
                self.last_in.append(copy(in_ref, far, mask | SIBLING, sibling))

    def start(self):
        for cp in self.local + self.first[::-1]:
            cp.start()

    def relay(self):
        for arrived, onward in zip(self.relayed_in, self.relays):
            arrived.wait_recv()
            onward.start()

    def finish(self):
        for cp in self.first + self.relays:
            cp.wait_send()
        for cp in self.last_in:
            cp.wait_recv()
        for cp in self.local:
            cp.wait()


def _exchange_out_shape(blocked, whole):
    return [SDS(b.shape, b.dtype) for b in blocked] + [SDS((N_DEV,) + w.shape, w.dtype) for w in whole]


def _exchange_sems(n):
    return [pltpu.SemaphoreType.DMA((n, N_DEV - 1)), pltpu.SemaphoreType.DMA((n, N_DEV - 1)),
            pltpu.SemaphoreType.DMA((n,))]


def exchange(blocked, whole, name):
    nb, n = len(blocked), len(blocked) + len(whole)

    def body(*refs):
        ex = _Exchange(nb, refs[:n], refs[n:2 * n], refs[2 * n:])
        ex.start()
        ex.relay()
        ex.finish()

    return pl.pallas_call(
        body, name=name, in_specs=[ANY] * n, out_specs=[ANY] * n, out_shape=_exchange_out_shape(blocked, whole),
        scratch_shapes=_exchange_sems(n),
    )(*blocked, *whole)


def _call(body, *, name, grid, in_specs, out_specs, out_shape, args, scratch_shapes=(), sem=None, ride=None):
    out_shape, out_specs, scratch_shapes = list(out_shape), list(out_specs), list(scratch_shapes)
    if ride is None:
        outs = pl.pallas_call(body, name=name, grid=grid, in_specs=list(in_specs), out_specs=out_specs,
                              out_shape=out_shape, scratch_shapes=scratch_shapes,
                              compiler_params=_params(sem))(*args)
        return list(outs), []
    blocked, whole = ride
    cargs = list(blocked) + list(whole)
    nb, nc = len(blocked), len(cargs)
    n_in, n_out, n_scr = len(args), len(out_shape), len(scratch_shapes)
    steps = math.prod(grid)
    assert steps >= 3, grid

    def riding(*refs):
        ins, refs = refs[:n_in], refs[n_in:]
        cins, refs = refs[:nc], refs[nc:]
        outs, refs = refs[:n_out], refs[n_out:]
        couts, refs = refs[:nc], refs[nc:]
        scr, sems = refs[:n_scr], refs[n_scr:]
        step = functools.reduce(lambda acc, k: acc * grid[k] + pl.program_id(k), range(len(grid)), 0)

        @pl.when(step == 0)
        def _():
            _Exchange(nb, cins, couts, sems).start()

        @pl.when(step == RELAY_AT * steps // 4)
        def _():
            _Exchange(nb, cins, couts, sems).relay()

        body(*ins, *outs, *scr)

        @pl.when(step == steps - 1)
        def _():
            _Exchange(nb, cins, couts, sems).finish()

    res = pl.pallas_call(
        riding, name=name, grid=grid, in_specs=list(in_specs) + [ANY] * nc, out_specs=out_specs + [ANY] * nc,
        out_shape=out_shape + _exchange_out_shape(blocked, whole),
        scratch_shapes=scratch_shapes + _exchange_sems(nc),
        compiler_params=_params(("arbitrary",) * len(grid)))(*args, *cargs)
    return list(res[:n_out]), list(res[n_out:])


def norm_mm(x, g, ws, name, tm=512, tn=1408, ride=None, conv=None):
    n, d = x.shape
    f = ws[0].shape[0]
    nw = len(ws)
    extra_in, extra_spec, extra_out, extra_out_spec, scratch = [], [], [], [], []
    if conv is not None:
        assert nw == 2 and tn == f and SEQ % tm == 0 and tm % CONV_ROWS == 0
        extra_in = list(conv)
        extra_spec = [pl.BlockSpec((3, f), lambda i, j: (0, 0)), pl.BlockSpec((1, f), lambda i, j: (0, 0))]
        extra_out, extra_out_spec = [SDS((n, f), BF16)], [pl.BlockSpec((tm, f), lambda i, j: (i, 0))]
        scratch = [pltpu.VMEM((PAD + CONV_ROWS, f), F32), pltpu.VMEM((PAD, f), F32)]

    def body(x_ref, g_ref, *refs):
        w_refs, refs = refs[:nw], refs[nw:]
        conv_refs, refs = refs[:len(extra_in)], refs[len(extra_in):]
        h_ref, o_refs, refs = refs[0], refs[1:1 + nw], refs[1 + nw:]

        @pl.when(pl.program_id(1) == 0)
        def _():
            xv = x_ref[...]
            r = lax.rsqrt(jnp.mean(xv * xv, axis=-1, keepdims=True) + NORM_EPS)
            h_ref[...] = (xv * r * g_ref[...]).astype(BF16)

        h = h_ref[...]
        for w_ref, o_ref in zip(w_refs, o_refs):
            o_ref[...] = _dot_nt(h, w_ref[...])
        if conv is None:
            return
        (cw_ref, cb_ref), (gp_ref, up_ref), (gu_ref, head_ref, carry_ref) = conv_refs, o_refs, refs

        @pl.when(pl.program_id(0) % (SEQ // tm) == 0)
        def _():
            carry_ref[...] = jnp.zeros_like(carry_ref)

        head_ref[0:PAD, :] = carry_ref[...]
        head_ref[PAD:PAD + CONV_ROWS, :] = gp_ref[0:CONV_ROWS, :]
        for l0 in range(0, f, LANE):
            ls = slice(l0, l0 + LANE)
            w0, w1, w2, bias = cw_ref[0:1, ls], cw_ref[1:2, ls], cw_ref[2:3, ls], cb_ref[:, ls]
            for r0 in range(0, tm, CONV_ROWS):
                g0, g1, g2 = _conv_taps(gp_ref, head_ref, r0, ls)
                c = bias + w0 * g2 + w1 * g1 + w2 * g0
                gu_ref[r0:r0 + CONV_ROWS, ls] = (_gelu(c) * up_ref[r0:r0 + CONV_ROWS, ls]).astype(BF16)
        carry_ref[...] = gp_ref[tm - PAD:tm, :]

    return _call(
        body, name=name, grid=(n // tm, f // tn),
        in_specs=[pl.BlockSpec((tm, d), lambda i, j: (i, 0)), pl.BlockSpec((1, d), lambda i, j: (0, 0))]
        + [pl.BlockSpec((tn, d), lambda i, j: (j, 0)) for _ in ws] + extra_spec,
        out_specs=[pl.BlockSpec((tm, d), lambda i, j: (i, 0))]
        + [pl.BlockSpec((tm, tn), lambda i, j: (i, j)) for _ in ws] + extra_out_spec,
        out_shape=[SDS((n, d), BF16)] + [SDS((n, f), F32) for _ in ws] + extra_out,
        scratch_shapes=scratch,
        args=[x, g, *ws, *extra_in], sem=("parallel" if conv is None else "arbitrary", "arbitrary"), ride=ride)


def _lane_concat(refs):
    vals = [r[...].astype(BF16) for r in refs]
    return vals[0] if len(vals) == 1 else jnp.concatenate(vals, axis=1)


def mm_res_norm(a_list, w, res, g, name, tm=512):
    n = a_list[0].shape[0]
    k, d = w.shape
    na = len(a_list)

    def body(*refs):
        w_ref, res_ref, g_ref, y_ref, o_ref = refs[na:]
        y = _dot(_lane_concat(refs[:na]), w_ref[...])
        r = lax.rsqrt(jnp.mean(y * y, axis=-1, keepdims=True) + NORM_EPS)
        y_ref[...] = y
        o_ref[...] = res_ref[...] + y * r * g_ref[...]

    return pl.pallas_call(
        body, name=name, grid=(n // tm,),
        in_specs=[pl.BlockSpec((tm, a.shape[1]), lambda i: (i, 0)) for a in a_list]
        + [pl.BlockSpec((k, d), lambda i: (0, 0)),
           pl.BlockSpec((tm, d), lambda i: (i, 0)), pl.BlockSpec((1, d), lambda i: (0, 0))],
        out_specs=[pl.BlockSpec((tm, d), lambda i: (i, 0)), pl.BlockSpec((tm, d), lambda i: (i, 0))],
        out_shape=[SDS((n, d), F32), SDS((n, d), F32)],
        compiler_params=_params(("parallel",)),
    )(*a_list, w, res, g)


def down_loss(a, w, res, g, target, name, tm=512):
    n, k = a.shape
    d = w.shape[1]
    inv_d = 1.0 / d

    def body(a_ref, w_ref, res_ref, g_ref, t_ref, dy_ref, dout_ref, dg_ref, loss_ref):
        i = pl.program_id(0)
        y = _dot(a_ref[...], w_ref[...])
        gv = g_ref[...]
        r = lax.rsqrt(jnp.mean(y * y, axis=-1, keepdims=True) + NORM_EPS)
        yh = y * r
        e = res_ref[...] + yh * gv - t_ref[...]
        part = 0.5 * inv_d * jnp.sum(jnp.sum(e * e, axis=-1, keepdims=True), axis=0, keepdims=True)
        dout = e * inv_d
        dout_ref[...] = dout
        gd = gv * dout
        dy_ref[...] = (r * (gd - yh * jnp.mean(gd * yh, axis=-1, keepdims=True))).astype(BF16)
        dgp = jnp.sum(dout * yh, axis=0, keepdims=True)
        lane0 = lax.broadcasted_iota(jnp.int32, (1, LANE), 1) == 0
        lp = jnp.where(lane0, part, 0.0)

        @pl.when(i == 0)
        def _():
            dg_ref[...] = dgp
            loss_ref[...] = lp

        @pl.when(i > 0)
        def _():
            dg_ref[...] += dgp
            loss_ref[...] += lp

    return pl.pallas_call(
        body, name=name, grid=(n // tm,),
        in_specs=[pl.BlockSpec((tm, k), lambda i: (i, 0)), pl.BlockSpec((k, d), lambda i: (0, 0)),
                  pl.BlockSpec((tm, d), lambda i: (i, 0)), pl.BlockSpec((1, d), lambda i: (0, 0)),
                  pl.BlockSpec((tm, d), lambda i: (i, 0))],
        out_specs=[pl.BlockSpec((tm, d), lambda i: (i, 0)), pl.BlockSpec((tm, d), lambda i: (i, 0)),
                   pl.BlockSpec((1, d), lambda i: (0, 0)), pl.BlockSpec((1, LANE), lambda i: (0, 0))],
        out_shape=[SDS((n, d), BF16), SDS((n, d), F32), SDS((1, d), F32), SDS((1, LANE), F32)],
        compiler_params=_params(("arbitrary",)),
    )(a, w, res, g, target)


def _accumulate(ref, val, step):
    @pl.when(step == 0)
    def _():
        ref[...] = val

    @pl.when(step > 0)
    def _():
        ref[...] += val


def mm_nt(terms, ws, name, tm=512, out_dtype=F32, ride=None, epilogue=None, by_rows=False):
    n = terms[0][0].shape[0]
    r = ws[0].shape[1 if by_rows else 0]
    na = len(terms)
    meta = [(widx, off, a.shape[1]) for a, widx, off in terms]
    fn, extras, out_shape = epilogue if epilogue else (None, [], [SDS((n, r), out_dtype)])
    n_fixed = na + len(ws)

    def body(*refs):
        a_refs = refs[:na]
        w_refs = refs[na:n_fixed]
        acc = None
        for a_ref, (widx, off, k) in zip(a_refs, meta):
            a = a_ref[...].astype(BF16)
            p = _dot(a, w_refs[widx][off:off + k, :]) if by_rows else _dot_nt(a, w_refs[widx][:, off:off + k])
            acc = p if acc is None else acc + p
        if fn is None:
            refs[-1][...] = acc.astype(out_dtype)
        else:
            fn(acc, pl.program_id(0), *refs[n_fixed:])

    def spec(a):
        if a.shape[0] == 1:
            return pl.BlockSpec(a.shape, lambda i: (0, 0))
        return pl.BlockSpec((tm, a.shape[1]), lambda i: (i, 0))

    return _call(
        body, name=name, grid=(n // tm,),
        in_specs=[spec(a) for a, _, _ in terms] + [pl.BlockSpec(w.shape, lambda i: (0, 0)) for w in ws]
        + [spec(e) for e in extras],
        out_specs=[spec(o) for o in out_shape], out_shape=out_shape,
        args=[a for a, _, _ in terms] + list(ws) + list(extras),
        sem=("parallel",) if fn is None else ("arbitrary",), ride=ride)


def _piece_blocks(pieces, tile):
    out, first = [], 0
    for p in pieces:
        nblk, rem = divmod(p.shape[1], tile)
        assert rem == 0, (p.shape, tile)
        out.append((first, nblk))
        first += nblk
    return out, first


def mm_tn(lhs_list, rhs_list, name, t1, t2, out_dtype=BF16):
    n = lhs_list[0].shape[0]
    lblocks, nbl = _piece_blocks(lhs_list, t1)
    rblocks, nbr = _piece_blocks(rhs_list, t2)
    nl = len(lhs_list)

    def body(*refs):
        l_refs, r_refs, o_ref = refs[:nl], refs[nl:-1], refs[-1]
        i, j = pl.program_id(0), pl.program_id(1)
        for l_ref, (ls, ln) in zip(l_refs, lblocks):
            for r_ref, (rs, rn) in zip(r_refs, rblocks):
                @pl.when((i >= ls) & (i < ls + ln) & (j >= rs) & (j < rs + rn))
                def _(l_ref=l_ref, r_ref=r_ref):
                    o_ref[...] = _dot_tn(l_ref[...].astype(BF16), r_ref[...].astype(BF16)).astype(out_dtype)

    def piece_spec(tile, axis, first, nblk):
        def index(i, j):
            return 0, jnp.clip((i, j)[axis] - first, 0, nblk - 1)
        return pl.BlockSpec((n, tile), index)

    return pl.pallas_call(
        body, name=name, grid=(nbl, nbr),
        in_specs=[piece_spec(t1, 0, *b) for b in lblocks] + [piece_spec(t2, 1, *b) for b in rblocks],
        out_specs=pl.BlockSpec((t1, t2), lambda i, j: (i, j)),
        out_shape=SDS((nbl * t1, nbr * t2), out_dtype),
        compiler_params=_params(("parallel", "arbitrary")),
    )(*lhs_list, *rhs_list)


GATE_ROWS = 512


def _tril_mask():
    row = lax.broadcasted_iota(jnp.int32, (CHUNK, CHUNK), 0)
    col = lax.broadcasted_iota(jnp.int32, (CHUNK, CHUNK), 1)
    return row >= col


def _group_of(shape, axis):
    return lax.broadcasted_iota(jnp.int32, shape, axis) // HEAD_DIM


def _group_mean_matrix():
    same = _group_of((A_WIDTH, A_WIDTH), 0) == _group_of((A_WIDTH, A_WIDTH), 1)
    return jnp.where(same, 1.0 / HEAD_DIM, 0.0).astype(BF16)


def _dot_sum(a, b):
    hi = a.astype(BF16)
    lo = (a - hi.astype(F32)).astype(BF16)
    return _dot(hi, b) + _dot(lo, b)


def _by_group(parts, lane_group):
    out = parts[A_GROUPS - 1]
    for g in range(A_GROUPS - 2, -1, -1):
        out = jnp.where(lane_group == g, parts[g], out)
    return out


def _group_norm(gv, gmean):
    xc = gv - _dot_sum(gv, gmean)
    rstd = lax.rsqrt(_dot_sum(xc * xc, gmean) + NORM_EPS)
    return xc * rstd, rstd


def gating_fwd(proj, lng, lnb, ws, sbt, name):
    n = proj.shape[0]

    def body(u_ref, v_ref, lng_ref, lnb_ref, ws_ref, sbt_ref, a_ref):
        tril = _tril_mask()
        lane_group = _group_of((CHUNK, A_WIDTH), 1)
        gmean = _group_mean_matrix()
        wts = [jnp.where(tril, ws_ref[g], 0.0).astype(BF16) for g in range(A_GROUPS)]
        sb = _by_group([sbt_ref[:, g:g + 1] for g in range(A_GROUPS)], lane_group)

        def chunk(c, carry):
            rows = pl.ds(pl.multiple_of(c * CHUNK, CHUNK), CHUNK)
            vhat, _ = _group_norm(_gelu(v_ref[rows, :]), gmean)
            vn = (vhat * lng_ref[...] + lnb_ref[...]).astype(BF16)
            z = _by_group([_dot(wt, vn) for wt in wts], lane_group) + sb
            a_ref[rows, :] = (_gelu(u_ref[rows, :]) * z).astype(BF16)
            return carry

        lax.fori_loop(0, GATE_ROWS // CHUNK, chunk, 0)

    return pl.pallas_call(
        body, name=name, grid=(n // GATE_ROWS,),
        in_specs=[pl.BlockSpec((GATE_ROWS, A_WIDTH), lambda i: (i, 0)),
                  pl.BlockSpec((GATE_ROWS, A_WIDTH), lambda i: (i, 1)),
                  pl.BlockSpec((1, A_WIDTH), lambda i: (0, 0)), pl.BlockSpec((1, A_WIDTH), lambda i: (0, 0)),
                  pl.BlockSpec((A_GROUPS, CHUNK, CHUNK), lambda i: (0, 0, 0)),
                  pl.BlockSpec((CHUNK, A_GROUPS), lambda i: (0, 0))],
        out_specs=pl.BlockSpec((GATE_ROWS, A_WIDTH), lambda i: (i, 0)),
        out_shape=SDS((n, A_WIDTH), BF16),
        compiler_params=_params(("parallel",)),
    )(proj, proj, lng, lnb, ws, sbt)


def gating_bwd(proj, dmix, lng, lnb, ws, sbt, name):
    n = proj.shape[0]

    def body(u_ref, v_ref, da_ref, lng_ref, lnb_ref, ws_ref, sbt_ref,
             duv_ref, dws_ref, dsbt_ref, dlng_ref, dlnb_ref):
        @pl.when(pl.program_id(0) == 0)
        def _():
            dws_ref[...] = jnp.zeros_like(dws_ref)
            dsbt_ref[...] = jnp.zeros_like(dsbt_ref)
            dlng_ref[...] = jnp.zeros_like(dlng_ref)
            dlnb_ref[...] = jnp.zeros_like(dlnb_ref)

        tril = _tril_mask()
        lane_group = _group_of((CHUNK, A_WIDTH), 1)
        gmean = _group_mean_matrix()
        gsum = (_group_of((A_WIDTH, LANE), 0) == lax.broadcasted_iota(jnp.int32, (A_WIDTH, LANE), 1)).astype(BF16)
        wts = [jnp.where(tril, ws_ref[g], 0.0) for g in range(A_GROUPS)]
        wts_b = [w.astype(BF16) for w in wts]
        wts_t = [w.T.astype(BF16) for w in wts]
        sb = _by_group([sbt_ref[:, g:g + 1] for g in range(A_GROUPS)], lane_group)
        lg = lng_ref[...]

        def chunk(c, carry):
            rows = pl.ds(pl.multiple_of(c * CHUNK, CHUNK), CHUNK)
            gu, dgu_dx = _gelu_and_grad(u_ref[rows, :])
            gv, dgv_dx = _gelu_and_grad(v_ref[rows, :])
            vhat, rstd = _group_norm(gv, gmean)
            vn = (vhat * lg + lnb_ref[...]).astype(BF16)
            z = _by_group([_dot(wt, vn) for wt in wts_b], lane_group) + sb
            da = da_ref[rows, :]
            dz = da * gu
            dzb = dz.astype(BF16)
            duv_ref[rows, 0:A_WIDTH] = (da * z * dgu_dx).astype(BF16)
            dsbt_ref[...] += _dot_sum(dz, gsum)[:, 0:A_GROUPS]
            for g in range(A_GROUPS):
                dz_g = jnp.where(lane_group == g, dzb, jnp.zeros_like(dzb))
                dws_ref[g] += jnp.where(tril, _dot_nt(dz_g, vn), 0.0)
            dvn = _by_group([_dot(wt, dzb) for wt in wts_t], lane_group)
            dlng_ref[...] += jnp.sum(dvn * vhat, axis=0, keepdims=True)
            dlnb_ref[...] += jnp.sum(dvn, axis=0, keepdims=True)
            dvh = dvn * lg
            dgv = rstd * (dvh - _dot_sum(dvh, gmean) - vhat * _dot_sum(dvh * vhat, gmean))
            duv_ref[rows, A_WIDTH:2 * A_WIDTH] = (dgv * dgv_dx).astype(BF16)
            return carry

        lax.fori_loop(0, GATE_ROWS // CHUNK, chunk, 0)

    return pl.pallas_call(
        body, name=name, grid=(n // GATE_ROWS,),
        in_specs=[pl.BlockSpec((GATE_ROWS, A_WIDTH), lambda i: (i, 0)),
                  pl.BlockSpec((GATE_ROWS, A_WIDTH), lambda i: (i, 1)),
                  pl.BlockSpec((GATE_ROWS, A_WIDTH), lambda i: (i, 0)),
                  pl.BlockSpec((1, A_WIDTH), lambda i: (0, 0)), pl.BlockSpec((1, A_WIDTH), lambda i: (0, 0)),
                  pl.BlockSpec((A_GROUPS, CHUNK, CHUNK), lambda i: (0, 0, 0)),
                  pl.BlockSpec((CHUNK, A_GROUPS), lambda i: (0, 0))],
        out_specs=[pl.BlockSpec((GATE_ROWS, 2 * A_WIDTH), lambda i: (i, 0)),
                   pl.BlockSpec((A_GROUPS, CHUNK, CHUNK), lambda i: (0, 0, 0)),
                   pl.BlockSpec((CHUNK, A_GROUPS), lambda i: (0, 0)),
                   pl.BlockSpec((1, A_WIDTH), lambda i: (0, 0)), pl.BlockSpec((1, A_WIDTH), lambda i: (0, 0))],
        out_shape=[SDS((n, 2 * A_WIDTH), BF16), SDS((A_GROUPS, CHUNK, CHUNK), F32), SDS((CHUNK, A_GROUPS), F32),
                   SDS((1, A_WIDTH), F32), SDS((1, A_WIDTH), F32)],
        compiler_params=_params(("arbitrary",)),
    )(proj, proj, dmix, lng, lnb, ws, sbt)


def _t5_bucket_np(dist):
    max_exact = NUM_BUCKETS // 2
    dd = np.maximum(dist, 1).astype(np.float64)
    large = max_exact + np.log(dd / max_exact) / math.log(MAX_DISTANCE / max_exact) * (NUM_BUCKETS - max_exact)
    large = np.minimum(large.astype(np.int64), NUM_BUCKETS - 1)
    return np.where(dist < max_exact, dist, large)


def _bucket_tables(with_first):
    i = np.arange(ATTN_BLOCK)[:, None]
    j = np.arange(2 * ATTN_BLOCK)[None, :]
    rel = ATTN_BLOCK + i - j
    band = (rel >= 0) & (rel <= ATTN_BLOCK)
    tabs = []
    for own_only in (False, True) if with_first else (False,):
        for dil in DILATIONS:
            b = _t5_bucket_np(np.maximum(rel, 0) * dil)
            tabs.append(np.where(band & (j >= ATTN_BLOCK) if own_only else band, b, -1).reshape(1, -1))
    return np.stack(tabs).astype(np.float32)


BIAS_SIZE = ATTN_BLOCK * 2 * ATTN_BLOCK


def bias_tables(rel_bias_t, name, ride=None):
    idx = jnp.asarray(_bucket_tables(True).reshape(-1, ATTN_BLOCK, 2 * ATTN_BLOCK))
    ntab = idx.shape[0]

    def body(rb_ref, idx_ref, o_ref):
        iv = idx_ref[0]

        def head(h, carry):
            t = jnp.full(iv.shape, NEG_INF, F32)
            for b in range(NUM_BUCKETS):
                t = jnp.where(iv == float(b), rb_ref[h, b], t)
            o_ref[0, h] = t
            return carry

        lax.fori_loop(0, B_HEADS, head, 0)

    return _call(
        body, name=name, grid=(ntab,),
        in_specs=[pl.BlockSpec(memory_space=pltpu.SMEM),
                  pl.BlockSpec((1, ATTN_BLOCK, 2 * ATTN_BLOCK), lambda d: (d, 0, 0))],
        out_specs=[pl.BlockSpec((1, B_HEADS, ATTN_BLOCK, 2 * ATTN_BLOCK), lambda d: (d, 0, 0, 0))],
        out_shape=[SDS((ntab, B_HEADS, ATTN_BLOCK, 2 * ATTN_BLOCK), F32)],
        args=[rel_bias_t, idx], sem=("parallel",), ride=ride)


def rel_bias_grad(dbias, name):
    idx = jnp.asarray(_bucket_tables(False))

    def body(db_ref, idx_ref, o_ref):
        d = pl.program_id(0)
        iv = idx_ref[0]
        bk = lax.broadcasted_iota(jnp.int32, (NUM_BUCKETS, BIAS_SIZE), 0).astype(F32)
        onehot = (bk == iv).astype(BF16)
        rest, part = db_ref[0], None
        for _ in range(3):
            term = rest.astype(BF16)
            rest = rest - term.astype(F32)
            p = _dot_nt(term, onehot)
            part = p if part is None else part + p

        @pl.when(d == 0)
        def _():
            o_ref[...] = part

        @pl.when(d > 0)
        def _():
            o_ref[...] += part

    return pl.pallas_call(
        body, name=name, grid=(len(DILATIONS),),
        in_specs=[pl.BlockSpec((1, B_HEADS, BIAS_SIZE), lambda d: (d, 0, 0)),
                  pl.BlockSpec((1, 1, BIAS_SIZE), lambda d: (d, 0, 0))],
        out_specs=pl.BlockSpec((B_HEADS, NUM_BUCKETS), lambda d: (0, 0)),
        out_shape=SDS((B_HEADS, NUM_BUCKETS), F32),
        compiler_params=_params(("arbitrary",)),
    )(dbias, idx)


QK_SCALE = 1.0 / math.sqrt(HEAD_DIM)


def _attn_scores(q_scaled, kk, bias):
    return _dot_nt(q_scaled, kk) + bias


def _head0_lanes():
    return lax.broadcasted_iota(jnp.int32, (ATTN_BLOCK, LANE), 1) < HEAD_DIM


def _one_head(x2, head0, hh):
    return jnp.where(head0 if hh == 0 else jnp.logical_not(head0), x2, 0.0).astype(BF16)


QUAD = 4
QUAD_ROWS = SEQ // QUAD


def _deinterleave(src_ref, dst_ref):
    for r in range(QUAD):
        for c in range(QUAD_ROWS // ATTN_BLOCK):
            dst_ref[r, c * ATTN_BLOCK:(c + 1) * ATTN_BLOCK, :] = src_ref[
                pl.ds(r + c * QUAD * ATTN_BLOCK, ATTN_BLOCK, stride=QUAD), :]


def _deinterleave_again(src_ref, dst_ref):
    for r in range(QUAD):
        for s in range(QUAD):
            dst_ref[r + QUAD * s] = src_ref[r, pl.ds(s, ATTN_BLOCK, stride=QUAD), :]


def _interleave_back(src_ref, dst_ref, slot0, accumulate=False):
    for r in range(QUAD):
        for s in range(QUAD):
            rows = pl.ds(s, ATTN_BLOCK, stride=QUAD)
            if accumulate:
                dst_ref[slot0 + r, rows, :] += src_ref[r + QUAD * s]
            else:
                dst_ref[slot0 + r, rows, :] = src_ref[r + QUAD * s]


def _quad_tiles():
    return [(r, pl.ds(r + c * QUAD * ATTN_BLOCK, ATTN_BLOCK, stride=QUAD), slice(c * ATTN_BLOCK, (c + 1) * ATTN_BLOCK))
            for r in range(QUAD) for c in range(QUAD_ROWS // ATTN_BLOCK)]


def _attn_schedule(op):
    def d16(i, carry):
        for t in range(2 * QUAD):
            op(2, 2 * QUAD * i + t, 0, True)
        return carry

    lax.fori_loop(0, QUAD // 2, d16, 0)

    def d4(i, carry):
        for u in range(2):
            for nq in range(QUAD_ROWS // ATTN_BLOCK):
                op(1, 2 * i + u, nq * ATTN_BLOCK, nq == 0)
        return carry

    lax.fori_loop(0, QUAD // 2, d4, 0)
    op(0, None, 0, True)
    per_pass = 5

    def d1(j, carry):
        for t in range(per_pass):
            op(0, None, pl.multiple_of((1 + per_pass * j + t) * ATTN_BLOCK, ATTN_BLOCK), False)
        return carry

    lax.fori_loop(0, (SEQ // ATTN_BLOCK - 1) // per_pass, d1, 0)


def _keys(src, krows, first):
    kb = src[krows, :].astype(BF16)
    return jnp.concatenate([kb, kb], axis=0) if first else kb


def _table(seg, first):
    return len(DILATIONS) + seg if first else seg


def _kv_rows(start, first):
    return pl.ds(start, ATTN_BLOCK) if first else pl.ds(start - ATTN_BLOCK, 2 * ATTN_BLOCK)


MERGE_ROWS = 256


def attn_fwd(proj, bias, nb_local, name, ride=None):
    n = proj.shape[0]
    nseg = len(DILATIONS)

    def body(q_ref, k_ref, v_ref, b_ref, o_ref, lse_ref, q4_ref, k4_ref, v4_ref, os0_ref, ls0_ref, os4_ref, ls4_ref,
             q16_ref, k16_ref, v16_ref, os16_ref, ls16_ref):
        for src, mid, dst in ((q_ref, q4_ref, q16_ref), (k_ref, k4_ref, k16_ref), (v_ref, v4_ref, v16_ref)):
            _deinterleave(src, mid)
            _deinterleave_again(mid, dst)

        def op(seg, r, start, first):
            qrows = pl.ds(start, ATTN_BLOCK)
            krows = _kv_rows(start, first)
            if seg == 0:
                q_src, k_src, v_src, o_dst, l_dst = q_ref, k_ref, v_ref, os0_ref, ls0_ref
            elif seg == 1:
                q_src, k_src, v_src = q4_ref.at[r], k4_ref.at[r], v4_ref.at[r]
                o_dst, l_dst = os4_ref.at[r], ls4_ref.at[r]
            else:
                q_src, k_src, v_src = q16_ref.at[r], k16_ref.at[r], v16_ref.at[r]
                o_dst, l_dst = os16_ref.at[r], ls16_ref.at[r]
            q2, kb, vb = q_src[qrows, :] * QK_SCALE, _keys(k_src, krows, first), _keys(v_src, krows, first)
            head0 = _head0_lanes()
            outs, lses = [], []
            for hh in range(2):
                s = _attn_scores(_one_head(q2, head0, hh), kb, b_ref[_table(seg, first), hh])
                m = jnp.max(s, axis=-1, keepdims=True)
                p = jnp.exp(s - m)
                l = jnp.sum(p, axis=-1, keepdims=True)
                outs.append(_dot(p.astype(BF16), vb) / l)
                lses.append(jnp.broadcast_to(m + jnp.log(l), (ATTN_BLOCK, LANE)))
            o_dst[qrows, :] = jnp.where(head0, outs[0], outs[1])
            l_dst[qrows, :] = jnp.where(head0, lses[0], lses[1])

        _attn_schedule(op)
        _interleave_back(os16_ref, os4_ref, QUAD)
        _interleave_back(ls16_ref, ls4_ref, QUAD)

        for r, nat, quad in _quad_tiles():
            ls = [ls0_ref[nat, :], ls4_ref[r, quad, :], ls4_ref[QUAD + r, quad, :]]
            m = functools.reduce(jnp.maximum, ls)
            ws = [jnp.exp(l - m) for l in ls]
            den = ws[0] + ws[1] + ws[2]
            num = ws[0] * os0_ref[nat, :] + ws[1] * os4_ref[r, quad, :] + ws[2] * os4_ref[QUAD + r, quad, :]
            o_ref[nat, :] = num / den
            lse_ref[nat, :] = m + jnp.log(den)

    def in_spec(off):
        return pl.BlockSpec((SEQ, LANE), lambda b, p: (b, off // LANE + p))

    out_spec = pl.BlockSpec((SEQ, LANE), lambda b, p: (b, p))
    return _call(
        body, name=name, grid=(nb_local, HEAD_PAIRS),
        in_specs=[in_spec(Q_OFF), in_spec(K_OFF), in_spec(V_OFF),
                  pl.BlockSpec((2 * nseg, 2, ATTN_BLOCK, 2 * ATTN_BLOCK), lambda b, p: (0, p, 0, 0))],
        out_specs=[out_spec, out_spec],
        out_shape=[SDS((n, B_WIDTH), F32), SDS((n, B_WIDTH), F32)],
        scratch_shapes=[pltpu.VMEM((QUAD, QUAD_ROWS, LANE), F32)] * 3 + [pltpu.VMEM((SEQ, LANE), F32)] * 2
        + [pltpu.VMEM((2 * QUAD, QUAD_ROWS, LANE), F32)] * 2 + [pltpu.VMEM((QUAD * QUAD, ATTN_BLOCK, LANE), F32)] * 5,
        args=[proj, proj, proj, bias], sem=("parallel", "arbitrary"), ride=ride)


def attn_bwd(proj, b_out, dmix, lse_tot, bias, nb_local, name, ride=None):
    n = proj.shape[0]
    nseg = len(DILATIONS)
    a_blocks = A_WIDTH // LANE

    def body(q_ref, k_ref, v_ref, o_ref, do_ref, lse_ref, b_ref, dq_ref, dk_ref, dv_ref, db_ref,
             dqs_ref, delta_ref, dka_ref, dva_ref, q4_ref, k4_ref, v4_ref, do4_ref, lse4_ref, delta4_ref,
             dqs4_ref, dk4_ref, dv4_ref, q16_ref, k16_ref, v16_ref, do16_ref, lse16_ref, delta16_ref,
             dqs16_ref, dk16_ref, dv16_ref):
        @pl.when(pl.program_id(1) == 0)
        def _():
            db_ref[...] = jnp.zeros_like(db_ref)

        quads = (q4_ref, k4_ref, v4_ref, do4_ref, lse4_ref, delta4_ref)
        hexes = (q16_ref, k16_ref, v16_ref, do16_ref, lse16_ref, delta16_ref)

        head_sum = (lax.broadcasted_iota(jnp.int32, (LANE, LANE), 0) // HEAD_DIM
                    == lax.broadcasted_iota(jnp.int32, (LANE, LANE), 1) // HEAD_DIM).astype(BF16)

        def row_dots(i, carry):
            rows = pl.ds(pl.multiple_of(i * MERGE_ROWS, MERGE_ROWS), MERGE_ROWS)
            delta_ref[rows, :] = _dot_sum(do_ref[rows, :] * o_ref[rows, :], head_sum)
            return carry

        lax.fori_loop(0, SEQ // MERGE_ROWS, row_dots, 0)
        for src, mid, dst in zip((q_ref, k_ref, v_ref, do_ref, lse_ref, delta_ref), quads, hexes):
            _deinterleave(src, mid)
            _deinterleave_again(mid, dst)

        def op(seg, r, start, first):
            qrows = pl.ds(start, ATTN_BLOCK)
            krows = _kv_rows(start, first)
            if seg == 0:
                srcs = (q_ref, k_ref, v_ref, do_ref, lse_ref, delta_ref)
                dq_dst, dk_dst, dv_dst = dqs_ref, dka_ref, dva_ref
            elif seg == 1:
                srcs = tuple(x.at[r] for x in quads)
                dq_dst, dk_dst, dv_dst = dqs4_ref.at[r], dk4_ref.at[r], dv4_ref.at[r]
            else:
                srcs = tuple(x.at[r] for x in hexes)
                dq_dst, dk_dst, dv_dst = dqs16_ref.at[r], dk16_ref.at[r], dv16_ref.at[r]
            q_src, k_src, v_src, do_src, lse_src, delta_src = srcs
            q2, kb, vb = q_src[qrows, :] * QK_SCALE, _keys(k_src, krows, first), _keys(v_src, krows, first)
            do2, lse2, delta2 = do_src[qrows, :], lse_src[qrows, :], delta_src[qrows, :]
            head0 = _head0_lanes()
            dqs, dk, dv = [], None, None
            for hh in range(2):
                col = slice(hh * HEAD_DIM, hh * HEAD_DIM + 1)
                q, dob = _one_head(q2, head0, hh), _one_head(do2, head0, hh)
                p = jnp.exp(_attn_scores(q, kb, b_ref[_table(seg, first), hh]) - lse2[:, col])
                dvh = _dot_tn(p.astype(BF16), dob)
                ds = p * (_dot_nt(dob, vb) - delta2[:, col])
                if first:
                    db_ref[seg, hh, :, ATTN_BLOCK:] += ds[:, ATTN_BLOCK:]
                else:
                    db_ref[seg, hh] += ds
                dsb = ds.astype(BF16)
                dqs.append(_dot(dsb, kb))
                dkh = _dot_tn(dsb, q)
                dk = dkh if dk is None else dk + dkh
                dv = dvh if dv is None else dv + dvh
            dq_dst[qrows, :] = jnp.where(head0, dqs[0], dqs[1]) * QK_SCALE
            dk_dst[qrows, :] = dk[ATTN_BLOCK:]
            dv_dst[qrows, :] = dv[ATTN_BLOCK:]
            if not first:
                before = pl.ds(start - ATTN_BLOCK, ATTN_BLOCK)
                dk_dst[before, :] += dk[:ATTN_BLOCK]
                dv_dst[before, :] += dv[:ATTN_BLOCK]

        _attn_schedule(op)
        _interleave_back(dqs16_ref, dqs4_ref, QUAD)
        _interleave_back(dk16_ref, dk4_ref, 0, accumulate=True)
        _interleave_back(dv16_ref, dv4_ref, 0, accumulate=True)

        for r, nat, quad in _quad_tiles():
            dqs_ref[nat, :] += dqs4_ref[r, quad, :] + dqs4_ref[QUAD + r, quad, :]
            dka_ref[nat, :] += dk4_ref[r, quad, :]
            dva_ref[nat, :] += dv4_ref[r, quad, :]

        def merge(i, carry):
            rows = pl.ds(pl.multiple_of(i * MERGE_ROWS, MERGE_ROWS), MERGE_ROWS)
            dq_ref[rows, :] = dqs_ref[rows, :].astype(BF16)
            dk_ref[rows, :] = dka_ref[rows, :].astype(BF16)
            dv_ref[rows, :] = dva_ref[rows, :].astype(BF16)
            return carry

        lax.fori_loop(0, SEQ // MERGE_ROWS, merge, 0)

    def pspec(off):
        return pl.BlockSpec((SEQ, LANE), lambda p, b: (b, off // LANE + p))

    ospec = pl.BlockSpec((SEQ, LANE), lambda p, b: (b, p))
    bspec = pl.BlockSpec((nseg, 2, ATTN_BLOCK, 2 * ATTN_BLOCK), lambda p, b: (0, p, 0, 0))
    gshape = SDS((n, B_WIDTH), BF16)
    return _call(
        body, name=name, grid=(HEAD_PAIRS, nb_local),
        in_specs=[pspec(Q_OFF), pspec(K_OFF), pspec(V_OFF), ospec,
                  pl.BlockSpec((SEQ, LANE), lambda p, b: (b, a_blocks + p)), ospec,
                  pl.BlockSpec((2 * nseg, 2, ATTN_BLOCK, 2 * ATTN_BLOCK), lambda p, b: (0, p, 0, 0))],
        out_specs=[ospec, ospec, ospec, bspec],
        out_shape=[gshape, gshape, gshape, SDS((nseg, B_HEADS, ATTN_BLOCK, 2 * ATTN_BLOCK), F32)],
        scratch_shapes=[pltpu.VMEM((SEQ, LANE), F32)] * 4 + [pltpu.VMEM((QUAD, QUAD_ROWS, LANE), F32)] * 6
        + [pltpu.VMEM((2 * QUAD, QUAD_ROWS, LANE), F32)] + [pltpu.VMEM((QUAD, QUAD_ROWS, LANE), F32)] * 2
        + [pltpu.VMEM((QUAD * QUAD, ATTN_BLOCK, LANE), F32)] * 9,
        args=[proj, proj, proj, b_out, dmix, lse_tot, bias], sem=("arbitrary", "arbitrary"), ride=ride)


PAD = 8
CONV_ROWS = 64


CONV_LANES = 128


def _conv_taps(gp_ref, head_ref, r0, ls):
    g0 = gp_ref[r0:r0 + CONV_ROWS, ls]
    if r0 == 0:
        return g0, head_ref[PAD - 1:PAD - 1 + CONV_ROWS, ls], head_ref[PAD - 2:PAD - 2 + CONV_ROWS, ls]
    return g0, gp_ref[r0 - 1:r0 - 1 + CONV_ROWS, ls], gp_ref[r0 - 2:r0 - 2 + CONV_ROWS, ls]


def _fill_head(gp_ref, head_ref):
    head_ref[0:PAD, :] = jnp.zeros((PAD, CONV_LANES), F32)
    head_ref[PAD:PAD + CONV_ROWS, :] = gp_ref[0:CONV_ROWS, :]


def _lane_passes():
    return [slice(l0, l0 + LANE) for l0 in range(0, CONV_LANES, LANE)]


def conv_gelu_bwd(dgu, gp, up, cw, cb, nb_local, name, ride=None):
    n, f = gp.shape

    def fold(v):
        return jnp.sum(v.reshape(CONV_ROWS // 8, 8, LANE), axis=0)

    def body(dgu_ref, gp_ref, up_ref, cw_ref, cb_ref, dgp_ref, dup_ref, dcw_ref, dcb_ref, head_ref, dc_ref):
        b = pl.program_id(1)
        _fill_head(gp_ref, head_ref)
        dc_ref[SEQ:SEQ + PAD, :] = jnp.zeros((PAD, CONV_LANES), F32)
        for ls in _lane_passes():
            w0, w1, w2, bias = cw_ref[0:1, ls], cw_ref[1:2, ls], cw_ref[2:3, ls], cb_ref[:, ls]
            sums = [jnp.zeros((8, LANE), F32) for _ in range(4)]
            for r0 in range(0, SEQ, CONV_ROWS):
                rows = slice(r0, r0 + CONV_ROWS)
                g0, g1, g2 = _conv_taps(gp_ref, head_ref, r0, ls)
                gg, dgg = _gelu_and_grad(bias + w0 * g2 + w1 * g1 + w2 * g0)
                dgu = dgu_ref[rows, ls].astype(F32)
                dup_ref[rows, ls] = (dgu * gg).astype(BF16)
                dc = dgu * up_ref[rows, ls] * dgg
                dc_ref[rows, ls] = dc
                sums = [sums[0] + fold(dc * g2), sums[1] + fold(dc * g1), sums[2] + fold(dc * g0), sums[3] + fold(dc)]
            for r0 in range(0, SEQ, CONV_ROWS):
                dgp_ref[r0:r0 + CONV_ROWS, ls] = (
                    w2 * dc_ref[r0:r0 + CONV_ROWS, ls] + w1 * dc_ref[r0 + 1:r0 + 1 + CONV_ROWS, ls]
                    + w0 * dc_ref[r0 + 2:r0 + 2 + CONV_ROWS, ls]).astype(BF16)
            dcw = jnp.concatenate([jnp.sum(s, axis=0, keepdims=True) for s in sums[:3]], axis=0)
            dcb = jnp.sum(sums[3], axis=0, keepdims=True)

            @pl.when(b == 0)
            def _(dcw=dcw, dcb=dcb, ls=ls):
                dcw_ref[:, ls] = dcw
                dcb_ref[:, ls] = dcb

            @pl.when(b > 0)
            def _(dcw=dcw, dcb=dcb, ls=ls):
                dcw_ref[:, ls] += dcw
                dcb_ref[:, ls] += dcb

    blk = pl.BlockSpec((SEQ, CONV_LANES), lambda j, b: (b, j))
    wspec = pl.BlockSpec((3, CONV_LANES), lambda j, b: (0, j))
    bspec = pl.BlockSpec((1, CONV_LANES), lambda j, b: (0, j))
    return _call(
        body, name=name, grid=(f // CONV_LANES, nb_local),
        in_specs=[blk, blk, blk, wspec, bspec], out_specs=[blk, blk, wspec, bspec],
        out_shape=[SDS((n, f), BF16), SDS((n, f), BF16), SDS((3, f), F32), SDS((1, f), F32)],
        scratch_shapes=[pltpu.VMEM((PAD + CONV_ROWS, CONV_LANES), F32), pltpu.VMEM((SEQ + PAD, CONV_LANES), F32)],
        args=[dgu, gp, up, cw, cb], sem=("parallel", "arbitrary"), ride=ride)


def norm_mid_epilogue(x1, dout, z2, g3, g2):
    n, d = x1.shape

    def fn(dh2, step, x1_ref, dout_ref, z2_ref, g3_ref, g2_ref, dx1_ref, dz2_ref, dg3_ref, dg2_ref):
        dxa, dg3r = _rms_bwd(dh2, x1_ref[...], g3_ref[...])
        dx1 = dout_ref[...] + dxa
        dx1_ref[...] = dx1
        dz2, dg2r = _rms_bwd(dx1, z2_ref[...], g2_ref[...])
        dz2_ref[...] = dz2.astype(BF16)
        _accumulate(dg3_ref, jnp.sum(dg3r, axis=0, keepdims=True), step)
        _accumulate(dg2_ref, jnp.sum(dg2r, axis=0, keepdims=True), step)

    return fn, [x1, dout, z2, g3, g2], [SDS((n, d), F32), SDS((n, d), BF16), SDS((1, d), F32), SDS((1, d), F32)]


def norm_in_epilogue(x, dx1, g1):
    n, d = x.shape

    def fn(dh1, step, x_ref, dx1_ref, g1_ref, dx_ref, dg1_ref):
        dxa, dgr = _rms_bwd(dh1, x_ref[...], g1_ref[...])
        dx_ref[...] = dx1_ref[...] + dxa
        _accumulate(dg1_ref, jnp.sum(dgr, axis=0, keepdims=True), step)

    return fn, [x, dx1, g1], [SDS((n, d), F32), SDS((1, d), F32)]


def cast_bf16(arrays, name):
    def body(*refs):
        for i_ref, o_ref in zip(refs[:len(arrays)], refs[len(arrays):]):
            o_ref[...] = i_ref[...].astype(BF16)

    return pl.pallas_call(body, name=name, out_shape=[SDS(a.shape, BF16) for a in arrays],
                          compiler_params=_params())(*arrays)


def adam_update(parts, w, m, v, name, tr=None):
    s, r, c = parts.shape
    tr = r if tr is None else tr
    bc1 = 1.0 - ADAM_B1 ** ADAM_STEP
    bc2 = 1.0 - ADAM_B2 ** ADAM_STEP

    def body(p_ref, w_ref, m_ref, v_ref, g_ref, d_ref, nm_ref, nv_ref):
        g = p_ref[0].astype(F32)
        for j in range(1, s):
            g = g + p_ref[j].astype(F32)
        nm = ADAM_B1 * m_ref[...] + (1.0 - ADAM_B1) * g
        nv = ADAM_B2 * v_ref[...] + (1.0 - ADAM_B2) * (g * g)
        g_ref[...] = g
        nm_ref[...] = nm
        nv_ref[...] = nv
        d_ref[...] = -ADAM_LR * ((nm / bc1) / (jnp.sqrt(nv / bc2) + ADAM_EPS) + ADAM_WD * w_ref[...])

    blk = pl.BlockSpec((tr, c), lambda i: (i, 0))
    return pl.pallas_call(
        body, name=name, grid=(r // tr,),
        in_specs=[pl.BlockSpec((s, tr, c), lambda i: (0, i, 0)), blk, blk, blk],
        out_specs=[blk] * 4, out_shape=[SDS((r, c), F32)] * 4,
        compiler_params=_params(("parallel",)),
    )(parts, w, m, v)


EARLY_NAMES = ("spatial_w", "norm_mix_post", "norm_ffn_pre", "norm_ffn_post", "conv_b", "ln_v_gain", "ln_v_bias",
               "spatial_b")
LATE_NAMES = ("norm_mix_pre", "rel_bias")
PACK_ROW_ALIGN = 8


def _pack_rows(size):
    rows = -(-size // LANE)
    return -(-rows // PACK_ROW_ALIGN) * PACK_ROW_ALIGN


def _pack(arrays):
    flat = []
    for a in arrays:
        rows = _pack_rows(a.size)
        flat.append(jnp.pad(a.reshape(-1), (0, rows * LANE - a.size)))
    return jnp.concatenate(flat).reshape(-1, LANE)


def _unpack(packed, shapes):
    out, row = [], 0
    for shp in shapes:
        size = int(np.prod(shp))
        out.append(packed[row:row + _pack_rows(size)].reshape(-1)[:size].reshape(shp))
        row += _pack_rows(size)
    return out


def kernel(x, norm_mix_pre, norm_mix_post, norm_ffn_pre, norm_ffn_post, w_in, ln_v_gain, ln_v_bias, spatial_w, spatial_b, rel_bias, w_out, w_gate, w_up, conv_w, conv_b, w_down, loss_target, m_norm_mix_pre, m_norm_mix_post, m_norm_ffn_pre, m_norm_ffn_post, m_w_in, m_ln_v_gain, m_ln_v_bias, m_spatial_w, m_spatial_b, m_rel_bias, m_w_out, m_w_gate, m_w_up, m_conv_w, m_conv_b, m_w_down, v_norm_mix_pre, v_norm_mix_post, v_norm_ffn_pre, v_norm_ffn_post, v_w_in, v_ln_v_gain, v_ln_v_bias, v_spatial_w, v_spatial_b, v_rel_bias, v_w_out, v_w_gate, v_w_up, v_conv_w, v_conv_b, v_w_down):
    given = dict(locals())
    nb_local, seq, d = x.shape
    n = nb_local * seq
    cols = w_in.shape[2]

    def by_columns(g):
        return g.transpose(1, 0, 2).reshape(g.shape[1], N_DEV * g.shape[2])

    def by_rows(g):
        return g.reshape(N_DEV * g.shape[1], g.shape[2])

    def blocks(g):
        return g.reshape(N_DEV, g.shape[0] // N_DEV, g.shape[1])

    xf, target = x.reshape(n, d), loss_target.reshape(n, d)
    ln_g, ln_b = ln_v_gain.reshape(1, A_WIDTH), ln_v_bias.reshape(1, A_WIDTH)
    spatial_bt, rel_bias_t = spatial_b[0].T, rel_bias.T

    s_in, s_out, s_gate, s_up, s_down = cast_bf16(
        [w_in[0].T, w_out[0], w_gate[0].T, w_up[0].T, w_down[0]], "cast_shards")
    (bias,), (g_in, g_cw) = bias_tables(rel_bias_t, "bias_tables", ride=([], [s_in, conv_w[0]]))
    w_in_t, conv_w_f = by_rows(g_in), by_columns(g_cw)

    (h1, proj), _ = norm_mm(xf, norm_mix_pre, [w_in_t], "fwd_norm_in", tn=IN_COLS)
    a = gating_fwd(proj, ln_g, ln_b, spatial_w[0], spatial_bt, "fwd_gating")
    (b_out, lse_tot), (g_out, g_gate, g_up) = attn_fwd(proj, bias, nb_local, "fwd_attn",
                                                       ride=([], [s_out, s_gate, s_up]))
    w_out_f, w_gate_t, w_up_t = by_rows(g_out), by_rows(g_gate), by_rows(g_up)
    z2, x1 = mm_res_norm([a, b_out], w_out_f, xf, norm_mix_post, "fwd_out_norm")
    (h2, gp, up, gu), (g_down,) = norm_mm(x1, norm_ffn_pre, [w_gate_t, w_up_t], "fwd_norm_ffn_conv", tm=256, tn=D_FF,
                                          ride=([], [s_down]), conv=(conv_w_f, conv_b))
    w_down_f = by_rows(g_down)
    dy, dout, dg4, loss_part = down_loss(gu, w_down_f, x1, norm_ffn_post, target, "fwd_down_loss")

    p_down = mm_tn([gu], [dy], "bwd_dw_down", t1=256, t2=D_MODEL)
    (dgu,), _ = mm_nt([(dy, 0, 0)], [w_down_f], "bwd_dgu", out_dtype=BF16)
    (dgp, dup, p_conv_w, p_conv_b), (r_down,) = conv_gelu_bwd(
        dgu, gp, up, conv_w_f, conv_b, nb_local, "bwd_conv_gelu", ride=([blocks(p_down)], []))
    p_gate = mm_tn([dgp], [h2], "bwd_dw_gate", t1=256, t2=D_MODEL)
    p_up = mm_tn([dup], [h2], "bwd_dw_up", t1=256, t2=D_MODEL)
    (dx1, dz2, dg3, dg2), _ = mm_nt([(dgp, 0, 0), (dup, 1, 0)], [w_gate_t, w_up_t], "bwd_dh2_norm_mid", tm=256,
                                    by_rows=True,
                                    epilogue=norm_mid_epilogue(x1, dout, z2, norm_ffn_pre, norm_mix_post))
    p_out = mm_tn([a, b_out], [dz2], "bwd_dw_out", t1=256, t2=D_MODEL)
    (dmix,), _ = mm_nt([(dz2, 0, 0)], [w_out_f], "bwd_dmix")
    duv, p_ws, p_sbt, p_lng, p_lnb = gating_bwd(proj, dmix, ln_g, ln_b, spatial_w[0], spatial_bt, "bwd_gating")
    small = dict(spatial_w=p_ws, norm_mix_post=dg2, norm_ffn_pre=dg3, norm_ffn_post=dg4, conv_b=p_conv_b,
                 ln_v_gain=p_lng, ln_v_bias=p_lnb, spatial_b=p_sbt.T)
    pack_early = _pack([small[k] for k in EARLY_NAMES] + [p_conv_w, loss_part])
    (dq, dk, dv, dbias), (r_gate, r_up, r_out, r_early) = attn_bwd(
        proj, b_out, dmix, lse_tot, bias, nb_local, "bwd_attn",
        ride=([blocks(p_gate), blocks(p_up), blocks(p_out)], [pack_early]))
    p_rel_bias_t = rel_bias_grad(dbias.reshape(len(DILATIONS), B_HEADS, BIAS_SIZE), "bwd_rel_bias")
    p_in = mm_tn([duv, dq, dk, dv], [h1], "bwd_dw_in", t1=256, t2=D_MODEL)
    (grad_x, dg1), (r_in,) = mm_nt(
        [(duv, 0, 0), (dq, 0, Q_OFF), (dk, 0, K_OFF), (dv, 0, V_OFF)], [w_in_t], "bwd_dh1_norm_in", by_rows=True,
        epilogue=norm_in_epilogue(xf, dx1, norm_mix_pre), ride=([blocks(p_in)], []))
    small.update(norm_mix_pre=dg1, rel_bias=p_rel_bias_t.T)
    (r_late,) = exchange([], [_pack([small[k] for k in LATE_NAMES])], "exchange_late")

    res = {}
    for k, received in (("w_in", r_in), ("w_gate", r_gate), ("w_up", r_up)):
        res[k] = [o.T for o in adam_update(received, given[k][0].T, given["m_" + k][0].T, given["v_" + k][0].T,
                                           "adam_" + k, tr=cols // 2)]
    res["w_out"] = adam_update(r_out, w_out[0], m_w_out[0], v_w_out[0], "adam_w_out")
    res["w_down"] = adam_update(r_down, w_down[0], m_w_down[0], v_w_down[0], "adam_w_down", tr=cols // 2)

    def adam_packed(received, names, tail, name):
        zeros = [jnp.zeros_like(t) for t in tail]
        packs = [_pack([given[pre + k] for k in names] + zeros) for pre in ("", "m_", "v_")]
        shapes = [given[k].shape for k in names] + [t.shape for t in tail]
        unpacked = [_unpack(p, shapes) for p in adam_update(received, *packs, name)]
        for i, k in enumerate(names):
            res[k] = [u[i] for u in unpacked]
        return unpacked[0][len(names):]

    g_conv_w_full, loss_sum = adam_packed(r_early, EARLY_NAMES, [p_conv_w, loss_part], "adam_small_early")
    adam_packed(r_late, LATE_NAMES, [], "adam_small_late")
    g_conv_w = lax.dynamic_slice_in_dim(g_conv_w_full, _my_index() * cols, cols, axis=1)
    res["conv_w"] = adam_update(g_conv_w[None], conv_w[0], m_conv_w[0], v_conv_w[0], "adam_conv_w")
    loss = loss_sum[0, 0]

    names = ("norm_mix_pre", "norm_mix_post", "norm_ffn_pre", "norm_ffn_post", "w_in", "ln_v_gain", "ln_v_bias",
             "spatial_w", "spatial_b", "rel_bias", "w_out", "w_gate", "w_up", "conv_w", "conv_b", "w_down")
    outs = [loss, grad_x.reshape(x.shape)]
    for t in range(4):
        outs += [res[k][t].reshape(given[k].shape) for k in names]
    return tuple(outs)
```

```python
import functools
import math

import numpy as np
import jax
import jax.numpy as jnp
from jax import lax
from jax.experimental import pallas as pl
from jax.experimental.pallas import tpu as pltpu

F32 = jnp.float32
BF16 = jnp.bfloat16
SDS = jax.ShapeDtypeStruct

D_MODEL = 1024
SEQ = 2048
HEAD_DIM = 64
A_GROUPS = 4
A_WIDTH = A_GROUPS * HEAD_DIM
B_HEADS = 12
B_WIDTH = B_HEADS * HEAD_DIM
HEAD_PAIRS = B_HEADS // 2
CHUNK = 128
ATTN_BLOCK = 128
DILATIONS = (1, 4, 16)
NUM_BUCKETS = 32
MAX_DISTANCE = 2048
D_FF = 2816
IN_COLS = 2 * A_WIDTH + 3 * B_WIDTH
Q_OFF = 2 * A_WIDTH
K_OFF = Q_OFF + B_WIDTH
V_OFF = K_OFF + B_WIDTH
NORM_EPS = 1e-6
NEG_INF = -1e30
N_DEV = 8
LANE = 128

ADAM_LR = 0.001
ADAM_B1 = 0.9
ADAM_B2 = 0.999
ADAM_EPS = 1e-08
ADAM_WD = 0.01
ADAM_STEP = 10

GELU_C0 = math.sqrt(2.0 / math.pi)
GELU_C1 = 0.044715

VMEM_LIMIT = 56 * 1024 * 1024


def _params(sem=None):
    if sem is None:
        return pltpu.CompilerParams(vmem_limit_bytes=VMEM_LIMIT)
    return pltpu.CompilerParams(dimension_semantics=sem, vmem_limit_bytes=VMEM_LIMIT)


def _gelu(x):
    t = jnp.tanh(x * (GELU_C0 + (GELU_C0 * GELU_C1) * (x * x)))
    return x * (0.5 + 0.5 * t)


def _gelu_and_grad(x):
    x2 = x * x
    t = jnp.tanh(x * (GELU_C0 + (GELU_C0 * GELU_C1) * x2))
    half = 0.5 + 0.5 * t
    dg = half + x * (0.5 - 0.5 * (t * t)) * (GELU_C0 + (3.0 * GELU_C0 * GELU_C1) * x2)
    return x * half, dg


def _dot(a, b):
    return jnp.dot(a, b, preferred_element_type=F32)


def _dot_nt(a, b):
    return lax.dot_general(a, b, (((1,), (1,)), ((), ())), preferred_element_type=F32)


def _dot_tn(a, b):
    return lax.dot_general(a, b, (((0,), (0,)), ((), ())), preferred_element_type=F32)


def _rms_bwd(d, xin, g):
    r = lax.rsqrt(jnp.mean(xin * xin, axis=-1, keepdims=True) + NORM_EPS)
    xh = xin * r
    gd = g * d
    dx = r * (gd - xh * jnp.mean(gd * xh, axis=-1, keepdims=True))
    return dx, d * xh


MESH = pl.DeviceIdType.MESH
ANY = pl.BlockSpec(memory_space=pl.ANY)
PEER_MASKS = tuple(range(1, N_DEV))


def _my_index():
    return lax.axis_index("x") * 4 + lax.axis_index("y") * 2 + lax.axis_index("c")


def _peer(mask):
    x, y, c = lax.axis_index("x"), lax.axis_index("y"), lax.axis_index("c")
    px = 1 - x if mask & 4 else x
    py = 1 - y if mask & 2 else y
    pc = 1 - c if mask & 1 else c
    return (px, py, pc), px * 4 + py * 2 + pc


RELAY_AT = 3
SIBLING = 1
CHIP_MASKS = (2, 4, 6)


class _Exchange:
    def __init__(self, nblocked, in_refs, out_refs, sems):
        send_sems, recv_sems, local_sems = sems
        me = _my_index()
        sibling, _ = _peer(SIBLING)
        self.local, self.first, self.relays, self.relayed_in, self.last_in = [], [], [], [], []
        for a, (in_ref, out_ref) in enumerate(zip(in_refs, out_refs)):
            def copy(src, slot, mask, to):
                return pltpu.make_async_remote_copy(
                    src_ref=src, dst_ref=out_ref.at[slot], send_sem=send_sems.at[a, mask - 1],
                    recv_sem=recv_sems.at[a, mask - 1], device_id=to, device_id_type=MESH)

            if a < nblocked:
                self.local.append(pltpu.make_async_copy(in_ref.at[me], out_ref.at[me], local_sems.at[a]))
                for mask in PEER_MASKS:
                    peer, pidx = _peer(mask)
                    self.first.append(copy(in_ref.at[pidx], me, mask, peer))
                    self.last_in.append(copy(in_ref.at[pidx], pidx, mask, peer))
                continue
            self.local.append(pltpu.make_async_copy(in_ref, out_ref.at[me], local_sems.at[a]))
            for mask in (SIBLING,) + CHIP_MASKS:
                peer, pidx = _peer(mask)
                self.first.append(copy(in_ref, me, mask, peer))
                (self.last_in if mask == SIBLING else self.relayed_in).append(copy(in_ref, pidx, mask, peer))
            for mask in CHIP_MASKS:
                _, origin = _peer(mask)
                _, far = _peer(mask | SIBLING)
                self.relays.append(copy(out_ref.at[origin], origin, mask | SIBLING, sibling))
                self.last_in.append(copy(in_ref, far, mask | SIBLING, sibling))

    def start(self):
        for cp in self.local + self.first[::-1]:
            cp.start()

    def relay(self):
        for arrived, onward in zip(self.relayed_in, self.relays):
            arrived.wait_recv()
            onward.start()

    def finish(self):
        for cp in self.first + self.relays:
            cp.wait_send()
        for cp in self.last_in:
            cp.wait_recv()
        for cp in self.local:
            cp.wait()


def _exchange_out_shape(blocked, whole):
    return [SDS(b.shape, b.dtype) for b in blocked] + [SDS((N_DEV,) + w.shape, w.dtype) for w in whole]


def _exchange_sems(n):
    return [pltpu.SemaphoreType.DMA((n, N_DEV - 1)), pltpu.SemaphoreType.DMA((n, N_DEV - 1)),
            pltpu.SemaphoreType.DMA((n,))]


def exchange(blocked, whole, name):
    nb, n = len(blocked), len(blocked) + len(whole)

    def body(*refs):
        ex = _Exchange(nb, refs[:n], refs[n:2 * n], refs[2 * n:])
        ex.start()
        ex.relay()
        ex.finish()

    return pl.pallas_call(
        body, name=name, in_specs=[ANY] * n, out_specs=[ANY] * n, out_shape=_exchange_out_shape(blocked, whole),
        scratch_shapes=_exchange_sems(n),
    )(*blocked, *whole)


def _call(body, *, name, grid, in_specs, out_specs, out_shape, args, scratch_shapes=(), sem=None, ride=None):
    out_shape, out_specs, scratch_shapes = list(out_shape), list(out_specs), list(scratch_shapes)
    if ride is None:
        outs = pl.pallas_call(body, name=name, grid=grid, in_specs=list(in_specs), out_specs=out_specs,
                              out_shape=out_shape, scratch_shapes=scratch_shapes,
                              compiler_params=_params(sem))(*args)
        return list(outs), []
    blocked, whole = ride
    cargs = list(blocked) + list(whole)
    nb, nc = len(blocked), len(cargs)
    n_in, n_out, n_scr = len(args), len(out_shape), len(scratch_shapes)
    steps = math.prod(grid)
    assert steps >= 3, grid

    def riding(*refs):
        ins, refs = refs[:n_in], refs[n_in:]
        cins, refs = refs[:nc], refs[nc:]
        outs, refs = refs[:n_out], refs[n_out:]
        couts, refs = refs[:nc], refs[nc:]
        scr, sems = refs[:n_scr], refs[n_scr:]
        step = functools.reduce(lambda acc, k: acc * grid[k] + pl.program_id(k), range(len(grid)), 0)

        @pl.when(step == 0)
        def _():
            _Exchange(nb, cins, couts, sems).start()

        @pl.when(step == RELAY_AT * steps // 4)
        def _():
            _Exchange(nb, cins, couts, sems).relay()

        body(*ins, *outs, *scr)

        @pl.when(step == steps - 1)
        def _():
            _Exchange(nb, cins, couts, sems).finish()

    res = pl.pallas_call(
        riding, name=name, grid=grid, in_specs=list(in_specs) + [ANY] * nc, out_specs=out_specs + [ANY] * nc,
        out_shape=out_shape + _exchange_out_shape(blocked, whole),
        scratch_shapes=scratch_shapes + _exchange_sems(nc),
        compiler_params=_params(("arbitrary",) * len(grid)))(*args, *cargs)
    return list(res[:n_out]), list(res[n_out:])


def norm_mm(x, g, ws, name, tm=512, tn=1408, ride=None, conv=None):
    n, d = x.shape
    f = ws[0].shape[0]
    nw = len(ws)
    extra_in, extra_spec, extra_out, extra_out_spec, scratch = [], [], [], [], []
    if conv is not None:
        assert nw == 2 and tn == f and SEQ % tm == 0 and tm % CONV_ROWS == 0
        extra_in = list(conv)
        extra_spec = [pl.BlockSpec((3, f), lambda i, j: (0, 0)), pl.BlockSpec((1, f), lambda i, j: (0, 0))]
        extra_out, extra_out_spec = [SDS((n, f), BF16)], [pl.BlockSpec((tm, f), lambda i, j: (i, 0))]
        scratch = [pltpu.VMEM((PAD + CONV_ROWS, f), F32), pltpu.VMEM((PAD, f), F32)]

    def body(x_ref, g_ref, *refs):
        w_refs, refs = refs[:nw], refs[nw:]
        conv_refs, refs = refs[:len(extra_in)], refs[len(extra_in):]
        h_ref, o_refs, refs = refs[0], refs[1:1 + nw], refs[1 + nw:]

        @pl.when(pl.program_id(1) == 0)
        def _():
            xv = x_ref[...]
            r = lax.rsqrt(jnp.mean(xv * xv, axis=-1, keepdims=True) + NORM_EPS)
            h_ref[...] = (xv * r * g_ref[...]).astype(BF16)

        h = h_ref[...]
        for w_ref, o_ref in zip(w_refs, o_refs):
            o_ref[...] = _dot_nt(h, w_ref[...])
        if conv is None:
            return
        (cw_ref, cb_ref), (gp_ref, up_ref), (gu_ref, head_ref, carry_ref) = conv_refs, o_refs, refs

        @pl.when(pl.program_id(0) % (SEQ // tm) == 0)
        def _():
            carry_ref[...] = jnp.zeros_like(carry_ref)

        head_ref[0:PAD, :] = carry_ref[...]
        head_ref[PAD:PAD + CONV_ROWS, :] = gp_ref[0:CONV_ROWS, :]
        for l0 in range(0, f, LANE):
            ls = slice(l0, l0 + LANE)
            w0, w1, w2, bias = cw_ref[0:1, ls], cw_ref[1:2, ls], cw_ref[2:3, ls], cb_ref[:, ls]
            for r0 in range(0, tm, CONV_ROWS):
                g0, g1, g2 = _conv_taps(gp_ref, head_ref, r0, ls)
                c = bias + w0 * g2 + w1 * g1 + w2 * g0
                gu_ref[r0:r0 + CONV_ROWS, ls] = (_gelu(c) * up_ref[r0:r0 + CONV_ROWS, ls]).astype(BF16)
        carry_ref[...] = gp_ref[tm - PAD:tm, :]

    return _call(
        body, name=name, grid=(n // tm, f // tn),
        in_specs=[pl.BlockSpec((tm, d), lambda i, j: (i, 0)), pl.BlockSpec((1, d), lambda i, j: (0, 0))]
        + [pl.BlockSpec((tn, d), lambda i, j: (j, 0)) for _ in ws] + extra_spec,
        out_specs=[pl.BlockSpec((tm, d), lambda i, j: (i, 0))]
        + [pl.BlockSpec((tm, tn), lambda i, j: (i, j)) for _ in ws] + extra_out_spec,
        out_shape=[SDS((n, d), BF16)] + [SDS((n, f), F32) for _ in ws] + extra_out,
        scratch_shapes=scratch,
        args=[x, g, *ws, *extra_in], sem=("parallel" if conv is None else "arbitrary", "arbitrary"), ride=ride)


def _lane_concat(refs):
    vals = [r[...].astype(BF16) for r in refs]
    return vals[0] if len(vals) == 1 else jnp.concatenate(vals, axis=1)


def mm_res_norm(a_list, w, res, g, name, tm=512):
    n = a_list[0].shape[0]
    k, d = w.shape
    na = len(a_list)

    def body(*refs):
        w_ref, res_ref, g_ref, y_ref, o_ref = refs[na:]
        y = _dot(_lane_concat(refs[:na]), w_ref[...])
        r = lax.rsqrt(jnp.mean(y * y, axis=-1, keepdims=True) + NORM_EPS)
        y_ref[...] = y
        o_ref[...] = res_ref[...] + y * r * g_ref[...]

    return pl.pallas_call(
        body, name=name, grid=(n // tm,),
        in_specs=[pl.BlockSpec((tm, a.shape[1]), lambda i: (i, 0)) for a in a_list]
        + [pl.BlockSpec((k, d), lambda i: (0, 0)),
           pl.BlockSpec((tm, d), lambda i: (i, 0)), pl.BlockSpec((1, d), lambda i: (0, 0))],
        out_specs=[pl.BlockSpec((tm, d), lambda i: (i, 0)), pl.BlockSpec((tm, d), lambda i: (i, 0))],
        out_shape=[SDS((n, d), F32), SDS((n, d), F32)],
        compiler_params=_params(("parallel",)),
    )(*a_list, w, res, g)


def down_loss(a, w, res, g, target, name, tm=512):
    n, k = a.shape
    d = w.shape[1]
    inv_d = 1.0 / d

    def body(a_ref, w_ref, res_ref, g_ref, t_ref, dy_ref, dout_ref, dg_ref, loss_ref):
        i = pl.program_id(0)
        y = _dot(a_ref[...], w_ref[...])
        gv = g_ref[...]
        r = lax.rsqrt(jnp.mean(y * y, axis=-1, keepdims=True) + NORM_EPS)
        yh = y * r
        e = res_ref[...] + yh * gv - t_ref[...]
        part = 0.5 * inv_d * jnp.sum(jnp.sum(e * e, axis=-1, keepdims=True), axis=0, keepdims=True)
        dout = e * inv_d
        dout_ref[...] = dout
        gd = gv * dout
        dy_ref[...] = (r * (gd - yh * jnp.mean(gd * yh, axis=-1, keepdims=True))).astype(BF16)
        dgp = jnp.sum(dout * yh, axis=0, keepdims=True)
        lane0 = lax.broadcasted_iota(jnp.int32, (1, LANE), 1) == 0
        lp = jnp.where(lane0, part, 0.0)

        @pl.when(i == 0)
        def _():
            dg_ref[...] = dgp
            loss_ref[...] = lp

        @pl.when(i > 0)
        def _():
            dg_ref[...] += dgp
            loss_ref[...] += lp

    return pl.pallas_call(
        body, name=name, grid=(n // tm,),
        in_specs=[pl.BlockSpec((tm, k), lambda i: (i, 0)), pl.BlockSpec((k, d), lambda i: (0, 0)),
                  pl.BlockSpec((tm, d), lambda i: (i, 0)), pl.BlockSpec((1, d), lambda i: (0, 0)),
                  pl.BlockSpec((tm, d), lambda i: (i, 0))],
        out_specs=[pl.BlockSpec((tm, d), lambda i: (i, 0)), pl.BlockSpec((tm, d), lambda i: (i, 0)),
                   pl.BlockSpec((1, d), lambda i: (0, 0)), pl.BlockSpec((1, LANE), lambda i: (0, 0))],
        out_shape=[SDS((n, d), BF16), SDS((n, d), F32), SDS((1, d), F32), SDS((1, LANE), F32)],
        compiler_params=_params(("arbitrary",)),
    )(a, w, res, g, target)


def _accumulate(ref, val, step):
    @pl.when(step == 0)
    def _():
        ref[...] = val

    @pl.when(step > 0)
    def _():
        ref[...] += val


def mm_nt(terms, ws, name, tm=512, out_dtype=F32, ride=None, epilogue=None, by_rows=False):
    n = terms[0][0].shape[0]
    r = ws[0].shape[1 if by_rows else 0]
    na = len(terms)
    meta = [(widx, off, a.shape[1]) for a, widx, off in terms]
    fn, extras, out_shape = epilogue if epilogue else (None, [], [SDS((n, r), out_dtype)])
    n_fixed = na + len(ws)

    def body(*refs):
        a_refs = refs[:na]
        w_refs = refs[na:n_fixed]
        acc = None
        for a_ref, (widx, off, k) in zip(a_refs, meta):
            a = a_ref[...].astype(BF16)
            p = _dot(a, w_refs[widx][off:off + k, :]) if by_rows else _dot_nt(a, w_refs[widx][:, off:off + k])
            acc = p if acc is None else acc + p
        if fn is None:
            refs[-1][...] = acc.astype(out_dtype)
        else:
            fn(acc, pl.program_id(0), *refs[n_fixed:])

    def spec(a):
        if a.shape[0] == 1:
            return pl.BlockSpec(a.shape, lambda i: (0, 0))
        return pl.BlockSpec((tm, a.shape[1]), lambda i: (i, 0))

    return _call(
        body, name=name, grid=(n // tm,),
        in_specs=[spec(a) for a, _, _ in terms] + [pl.BlockSpec(w.shape, lambda i: (0, 0)) for w in ws]
        + [spec(e) for e in extras],
        out_specs=[spec(o) for o in out_shape], out_shape=out_shape,
        args=[a for a, _, _ in terms] + list(ws) + list(extras),
        sem=("parallel",) if fn is None else ("arbitrary",), ride=ride)


def _piece_blocks(pieces, tile):
    out, first = [], 0
    for p in pieces:
        nblk, rem = divmod(p.shape[1], tile)
        assert rem == 0, (p.shape, tile)
        out.append((first, nblk))
        first += nblk
    return out, first


def mm_tn(lhs_list, rhs_list, name, t1, t2, out_dtype=BF16):
    n = lhs_list[0].shape[0]
    lblocks, nbl = _piece_blocks(lhs_list, t1)
    rblocks, nbr = _piece_blocks(rhs_list, t2)
    nl = len(lhs_list)

    def body(*refs):
        l_refs, r_refs, o_ref = refs[:nl], refs[nl:-1], refs[-1]
        i, j = pl.program_id(0), pl.program_id(1)
        for l_ref, (ls, ln) in zip(l_refs, lblocks):
            for r_ref, (rs, rn) in zip(r_refs, rblocks):
                @pl.when((i >= ls) & (i < ls + ln) & (j >= rs) & (j < rs + rn))
                def _(l_ref=l_ref, r_ref=r_ref):
                    o_ref[...] = _dot_tn(l_ref[...].astype(BF16), r_ref[...].astype(BF16)).astype(out_dtype)

    def piece_spec(tile, axis, first, nblk):
        def index(i, j):
            return 0, jnp.clip((i, j)[axis] - first, 0, nblk - 1)
        return pl.BlockSpec((n, tile), index)

    return pl.pallas_call(
        body, name=name, grid=(nbl, nbr),
        in_specs=[piece_spec(t1, 0, *b) for b in lblocks] + [piece_spec(t2, 1, *b) for b in rblocks],
        out_specs=pl.BlockSpec((t1, t2), lambda i, j: (i, j)),
        out_shape=SDS((nbl * t1, nbr * t2), out_dtype),
        compiler_params=_params(("parallel", "arbitrary")),
    )(*lhs_list, *rhs_list)


GATE_ROWS = 512


def _tril_mask():
    row = lax.broadcasted_iota(jnp.int32, (CHUNK, CHUNK), 0)
    col = lax.broadcasted_iota(jnp.int32, (CHUNK, CHUNK), 1)
    return row >= col


def _group_of(shape, axis):
    return lax.broadcasted_iota(jnp.int32, shape, axis) // HEAD_DIM


def _group_mean_matrix():
    same = _group_of((A_WIDTH, A_WIDTH), 0) == _group_of((A_WIDTH, A_WIDTH), 1)
    return jnp.where(same, 1.0 / HEAD_DIM, 0.0).astype(BF16)


def _dot_sum(a, b):
    hi = a.astype(BF16)
    lo = (a - hi.astype(F32)).astype(BF16)
    return _dot(hi, b) + _dot(lo, b)


def _by_group(parts, lane_group):
    out = parts[A_GROUPS - 1]
    for g in range(A_GROUPS - 2, -1, -1):
        out = jnp.where(lane_group == g, parts[g], out)
    return out


def _group_norm(gv, gmean):
    xc = gv - _dot_sum(gv, gmean)
    rstd = lax.rsqrt(_dot_sum(xc * xc, gmean) + NORM_EPS)
    return xc * rstd, rstd


def gating_fwd(proj, lng, lnb, ws, sbt, name):
    n = proj.shape[0]

    def body(u_ref, v_ref, lng_ref, lnb_ref, ws_ref, sbt_ref, a_ref):
        tril = _tril_mask()
        lane_group = _group_of((CHUNK, A_WIDTH), 1)
        gmean = _group_mean_matrix()
        wts = [jnp.where(tril, ws_ref[g], 0.0).astype(BF16) for g in range(A_GROUPS)]
        sb = _by_group([sbt_ref[:, g:g + 1] for g in range(A_GROUPS)], lane_group)

        def chunk(c, carry):
            rows = pl.ds(pl.multiple_of(c * CHUNK, CHUNK), CHUNK)
            vhat, _ = _group_norm(_gelu(v_ref[rows, :]), gmean)
            vn = (vhat * lng_ref[...] + lnb_ref[...]).astype(BF16)
            z = _by_group([_dot(wt, vn) for wt in wts], lane_group) + sb
            a_ref[rows, :] = (_gelu(u_ref[rows, :]) * z).astype(BF16)
            return carry

        lax.fori_loop(0, GATE_ROWS // CHUNK, chunk, 0)

    return pl.pallas_call(
        body, name=name, grid=(n // GATE_ROWS,),
        in_specs=[pl.BlockSpec((GATE_ROWS, A_WIDTH), lambda i: (i, 0)),
                  pl.BlockSpec((GATE_ROWS, A_WIDTH), lambda i: (i, 1)),
                  pl.BlockSpec((1, A_WIDTH), lambda i: (0, 0)), pl.BlockSpec((1, A_WIDTH), lambda i: (0, 0)),
                  pl.BlockSpec((A_GROUPS, CHUNK, CHUNK), lambda i: (0, 0, 0)),
                  pl.BlockSpec((CHUNK, A_GROUPS), lambda i: (0, 0))],
        out_specs=pl.BlockSpec((GATE_ROWS, A_WIDTH), lambda i: (i, 0)),
        out_shape=SDS((n, A_WIDTH), BF16),
        compiler_params=_params(("parallel",)),
    )(proj, proj, lng, lnb, ws, sbt)


def gating_bwd(proj, dmix, lng, lnb, ws, sbt, name):
    n = proj.shape[0]

    def body(u_ref, v_ref, da_ref, lng_ref, lnb_ref, ws_ref, sbt_ref,
             duv_ref, dws_ref, dsbt_ref, dlng_ref, dlnb_ref):
        @pl.when(pl.program_id(0) == 0)
        def _():
            dws_ref[...] = jnp.zeros_like(dws_ref)
            dsbt_ref[...] = jnp.zeros_like(dsbt_ref)
            dlng_ref[...] = jnp.zeros_like(dlng_ref)
            dlnb_ref[...] = jnp.zeros_like(dlnb_ref)

        tril = _tril_mask()
        lane_group = _group_of((CHUNK, A_WIDTH), 1)
        gmean = _group_mean_matrix()
        gsum = (_group_of((A_WIDTH, LANE), 0) == lax.broadcasted_iota(jnp.int32, (A_WIDTH, LANE), 1)).astype(BF16)
        wts = [jnp.where(tril, ws_ref[g], 0.0) for g in range(A_GROUPS)]
        wts_b = [w.astype(BF16) for w in wts]
        wts_t = [w.T.astype(BF16) for w in wts]
        sb = _by_group([sbt_ref[:, g:g + 1] for g in range(A_GROUPS)], lane_group)
        lg = lng_ref[...]

        def chunk(c, carry):
            rows = pl.ds(pl.multiple_of(c * CHUNK, CHUNK), CHUNK)
            gu, dgu_dx = _gelu_and_grad(u_ref[rows, :])
            gv, dgv_dx = _gelu_and_grad(v_ref[rows, :])
            vhat, rstd = _group_norm(gv, gmean)
            vn = (vhat * lg + lnb_ref[...]).astype(BF16)
            z = _by_group([_dot(wt, vn) for wt in wts_b], lane_group) + sb
            da = da_ref[rows, :]
            dz = da * gu
            dzb = dz.astype(BF16)
            duv_ref[rows, 0:A_WIDTH] = (da * z * dgu_dx).astype(BF16)
            dsbt_ref[...] += _dot_sum(dz, gsum)[:, 0:A_GROUPS]
            for g in range(A_GROUPS):
                dz_g = jnp.where(lane_group == g, dzb, jnp.zeros_like(dzb))
                dws_ref[g] += jnp.where(tril, _dot_nt(dz_g, vn), 0.0)
            dvn = _by_group([_dot(wt, dzb) for wt in wts_t], lane_group)
            dlng_ref[...] += jnp.sum(dvn * vhat, axis=0, keepdims=True)
            dlnb_ref[...] += jnp.sum(dvn, axis=0, keepdims=True)
            dvh = dvn * lg
            dgv = rstd * (dvh - _dot_sum(dvh, gmean) - vhat * _dot_sum(dvh * vhat, gmean))
            duv_ref[rows, A_WIDTH:2 * A_WIDTH] = (dgv * dgv_dx).astype(BF16)
            return carry

        lax.fori_loop(0, GATE_ROWS // CHUNK, chunk, 0)

    return pl.pallas_call(
        body, name=name, grid=(n // GATE_ROWS,),
        in_specs=[pl.BlockSpec((GATE_ROWS, A_WIDTH), lambda i: (i, 0)),
                  pl.BlockSpec((GATE_ROWS, A_WIDTH), lambda i: (i, 1)),
                  pl.BlockSpec((GATE_ROWS, A_WIDTH), lambda i: (i, 0)),
                  pl.BlockSpec((1, A_WIDTH), lambda i: (0, 0)), pl.BlockSpec((1, A_WIDTH), lambda i: (0, 0)),
                  pl.BlockSpec((A_GROUPS, CHUNK, CHUNK), lambda i: (0, 0, 0)),
                  pl.BlockSpec((CHUNK, A_GROUPS), lambda i: (0, 0))],
        out_specs=[pl.BlockSpec((GATE_ROWS, 2 * A_WIDTH), lambda i: (i, 0)),
                   pl.BlockSpec((A_GROUPS, CHUNK, CHUNK), lambda i: (0, 0, 0)),
                   pl.BlockSpec((CHUNK, A_GROUPS), lambda i: (0, 0)),
                   pl.BlockSpec((1, A_WIDTH), lambda i: (0, 0)), pl.BlockSpec((1, A_WIDTH), lambda i: (0, 0))],
        out_shape=[SDS((n, 2 * A_WIDTH), BF16), SDS((A_GROUPS, CHUNK, CHUNK), F32), SDS((CHUNK, A_GROUPS), F32),
                   SDS((1, A_WIDTH), F32), SDS((1, A_WIDTH), F32)],
        compiler_params=_params(("arbitrary",)),
    )(proj, proj, dmix, lng, lnb, ws, sbt)


def _t5_bucket_np(dist):
    max_exact = NUM_BUCKETS // 2
    dd = np.maximum(dist, 1).astype(np.float64)
    large = max_exact + np.log(dd / max_exact) / math.log(MAX_DISTANCE / max_exact) * (NUM_BUCKETS - max_exact)
    large = np.minimum(large.astype(np.int64), NUM_BUCKETS - 1)
    return np.where(dist < max_exact, dist, large)


def _bucket_tables(with_first):
    i = np.arange(ATTN_BLOCK)[:, None]
    j = np.arange(2 * ATTN_BLOCK)[None, :]
    rel = ATTN_BLOCK + i - j
    band = (rel >= 0) & (rel <= ATTN_BLOCK)
    tabs = []
    for own_only in (False, True) if with_first else (False,):
        for dil in DILATIONS:
            b = _t5_bucket_np(np.maximum(rel, 0) * dil)
            tabs.append(np.where(band & (j >= ATTN_BLOCK) if own_only else band, b, -1).reshape(1, -1))
    return np.stack(tabs).astype(np.float32)


BIAS_SIZE = ATTN_BLOCK * 2 * ATTN_BLOCK


def bias_tables(rel_bias_t, name, ride=None):
    idx = jnp.asarray(_bucket_tables(True).reshape(-1, ATTN_BLOCK, 2 * ATTN_BLOCK))
    ntab = idx.shape[0]

    def body(rb_ref, idx_ref, o_ref):
        iv = idx_ref[0]

        def head(h, carry):
            t = jnp.full(iv.shape, NEG_INF, F32)
            for b in range(NUM_BUCKETS):
                t = jnp.where(iv == float(b), rb_ref[h, b], t)
            o_ref[0, h] = t
            return carry

        lax.fori_loop(0, B_HEADS, head, 0)

    return _call(
        body, name=name, grid=(ntab,),
        in_specs=[pl.BlockSpec(memory_space=pltpu.SMEM),
                  pl.BlockSpec((1, ATTN_BLOCK, 2 * ATTN_BLOCK), lambda d: (d, 0, 0))],
        out_specs=[pl.BlockSpec((1, B_HEADS, ATTN_BLOCK, 2 * ATTN_BLOCK), lambda d: (d, 0, 0, 0))],
        out_shape=[SDS((ntab, B_HEADS, ATTN_BLOCK, 2 * ATTN_BLOCK), F32)],
        args=[rel_bias_t, idx], sem=("parallel",), ride=ride)


def rel_bias_grad(dbias, name):
    idx = jnp.asarray(_bucket_tables(False))

    def body(db_ref, idx_ref, o_ref):
        d = pl.program_id(0)
        iv = idx_ref[0]
        bk = lax.broadcasted_iota(jnp.int32, (NUM_BUCKETS, BIAS_SIZE), 0).astype(F32)
        onehot = (bk == iv).astype(BF16)
        rest, part = db_ref[0], None
        for _ in range(3):
            term = rest.astype(BF16)
            rest = rest - term.astype(F32)
            p = _dot_nt(term, onehot)
            part = p if part is None else part + p

        @pl.when(d == 0)
        def _():
            o_ref[...] = part

        @pl.when(d > 0)
        def _():
            o_ref[...] += part

    return pl.pallas_call(
        body, name=name, grid=(len(DILATIONS),),
        in_specs=[pl.BlockSpec((1, B_HEADS, BIAS_SIZE), lambda d: (d, 0, 0)),
                  pl.BlockSpec((1, 1, BIAS_SIZE), lambda d: (d, 0, 0))],
        out_specs=pl.BlockSpec((B_HEADS, NUM_BUCKETS), lambda d: (0, 0)),
        out_shape=SDS((B_HEADS, NUM_BUCKETS), F32),
        compiler_params=_params(("arbitrary",)),
    )(dbias, idx)


QK_SCALE = 1.0 / math.sqrt(HEAD_DIM)


def _attn_scores(q_scaled, kk, bias):
    return _dot_nt(q_scaled, kk) + bias


def _head0_lanes():
    return lax.broadcasted_iota(jnp.int32, (ATTN_BLOCK, LANE), 1) < HEAD_DIM


def _one_head(x2, head0, hh):
    return jnp.where(head0 if hh == 0 else jnp.logical_not(head0), x2, 0.0).astype(BF16)


QUAD = 4
QUAD_ROWS = SEQ // QUAD


def _deinterleave(src_ref, dst_ref):
    for r in range(QUAD):
        for c in range(QUAD_ROWS // ATTN_BLOCK):
            dst_ref[r, c * ATTN_BLOCK:(c + 1) * ATTN_BLOCK, :] = src_ref[
                pl.ds(r + c * QUAD * ATTN_BLOCK, ATTN_BLOCK, stride=QUAD), :]


def _deinterleave_again(src_ref, dst_ref):
    for r in range(QUAD):
        for s in range(QUAD):
            dst_ref[r + QUAD * s] = src_ref[r, pl.ds(s, ATTN_BLOCK, stride=QUAD), :]


def _interleave_back(src_ref, dst_ref, slot0, accumulate=False):
    for r in range(QUAD):
        for s in range(QUAD):
            rows = pl.ds(s, ATTN_BLOCK, stride=QUAD)
            if accumulate:
                dst_ref[slot0 + r, rows, :] += src_ref[r + QUAD * s]
            else:
                dst_ref[slot0 + r, rows, :] = src_ref[r + QUAD * s]


def _quad_tiles():
    return [(r, pl.ds(r + c * QUAD * ATTN_BLOCK, ATTN_BLOCK, stride=QUAD), slice(c * ATTN_BLOCK, (c + 1) * ATTN_BLOCK))
            for r in range(QUAD) for c in range(QUAD_ROWS // ATTN_BLOCK)]


def _attn_schedule(op):
    def d16(i, carry):
        for t in range(2 * QUAD):
            op(2, 2 * QUAD * i + t, 0, True)
        return carry

    lax.fori_loop(0, QUAD // 2, d16, 0)

    def d4(i, carry):
        for u in range(2):
            for nq in range(QUAD_ROWS // ATTN_BLOCK):
                op(1, 2 * i + u, nq * ATTN_BLOCK, nq == 0)
        return carry

    lax.fori_loop(0, QUAD // 2, d4, 0)
    op(0, None, 0, True)
    per_pass = 5

    def d1(j, carry):
        for t in range(per_pass):
            op(0, None, pl.multiple_of((1 + per_pass * j + t) * ATTN_BLOCK, ATTN_BLOCK), False)
        return carry

    lax.fori_loop(0, (SEQ // ATTN_BLOCK - 1) // per_pass, d1, 0)


def _keys(src, krows, first):
    kb = src[krows, :].astype(BF16)
    return jnp.concatenate([kb, kb], axis=0) if first else kb


def _table(seg, first):
    return len(DILATIONS) + seg if first else seg


def _kv_rows(start, first):
    return pl.ds(start, ATTN_BLOCK) if first else pl.ds(start - ATTN_BLOCK, 2 * ATTN_BLOCK)


MERGE_ROWS = 256


def attn_fwd(proj, bias, nb_local, name, ride=None):
    n = proj.shape[0]
    nseg = len(DILATIONS)

    def body(q_ref, k_ref, v_ref, b_ref, o_ref, lse_ref, q4_ref, k4_ref, v4_ref, os0_ref, ls0_ref, os4_ref, ls4_ref,
             q16_ref, k16_ref, v16_ref, os16_ref, ls16_ref):
        for src, mid, dst in ((q_ref, q4_ref, q16_ref), (k_ref, k4_ref, k16_ref), (v_ref, v4_ref, v16_ref)):
            _deinterleave(src, mid)
            _deinterleave_again(mid, dst)

        def op(seg, r, start, first):
            qrows = pl.ds(start, ATTN_BLOCK)
            krows = _kv_rows(start, first)
            if seg == 0:
                q_src, k_src, v_src, o_dst, l_dst = q_ref, k_ref, v_ref, os0_ref, ls0_ref
            elif seg == 1:
                q_src, k_src, v_src = q4_ref.at[r], k4_ref.at[r], v4_ref.at[r]
                o_dst, l_dst = os4_ref.at[r], ls4_ref.at[r]
            else:
                q_src, k_src, v_src = q16_ref.at[r], k16_ref.at[r], v16_ref.at[r]
                o_dst, l_dst = os16_ref.at[r], ls16_ref.at[r]
            q2, kb, vb = q_src[qrows, :] * QK_SCALE, _keys(k_src, krows, first), _keys(v_src, krows, first)
            head0 = _head0_lanes()
            outs, lses = [], []
            for hh in range(2):
                s = _attn_scores(_one_head(q2, head0, hh), kb, b_ref[_table(seg, first), hh])
                m = jnp.max(s, axis=-1, keepdims=True)
                p = jnp.exp(s - m)
                l = jnp.sum(p, axis=-1, keepdims=True)
                outs.append(_dot(p.astype(BF16), vb) / l)
                lses.append(jnp.broadcast_to(m + jnp.log(l), (ATTN_BLOCK, LANE)))
            o_dst[qrows, :] = jnp.where(head0, outs[0], outs[1])
            l_dst[qrows, :] = jnp.where(head0, lses[0], lses[1])

        _attn_schedule(op)
        _interleave_back(os16_ref, os4_ref, QUAD)
        _interleave_back(ls16_ref, ls4_ref, QUAD)

        for r, nat, quad in _quad_tiles():
            ls = [ls0_ref[nat, :], ls4_ref[r, quad, :], ls4_ref[QUAD + r, quad, :]]
            m = functools.reduce(jnp.maximum, ls)
            ws = [jnp.exp(l - m) for l in ls]
            den = ws[0] + ws[1] + ws[2]
            num = ws[0] * os0_ref[nat, :] + ws[1] * os4_ref[r, quad, :] + ws[2] * os4_ref[QUAD + r, quad, :]
            o_ref[nat, :] = num / den
            lse_ref[nat, :] = m + jnp.log(den)

    def in_spec(off):
        return pl.BlockSpec((SEQ, LANE), lambda b, p: (b, off // LANE + p))

    out_spec = pl.BlockSpec((SEQ, LANE), lambda b, p: (b, p))
    return _call(
        body, name=name, grid=(nb_local, HEAD_PAIRS),
        in_specs=[in_spec(Q_OFF), in_spec(K_OFF), in_spec(V_OFF),
                  pl.BlockSpec((2 * nseg, 2, ATTN_BLOCK, 2 * ATTN_BLOCK), lambda b, p: (0, p, 0, 0))],
        out_specs=[out_spec, out_spec],
        out_shape=[SDS((n, B_WIDTH), F32), SDS((n, B_WIDTH), F32)],
        scratch_shapes=[pltpu.VMEM((QUAD, QUAD_ROWS, LANE), F32)] * 3 + [pltpu.VMEM((SEQ, LANE), F32)] * 2
        + [pltpu.VMEM((2 * QUAD, QUAD_ROWS, LANE), F32)] * 2 + [pltpu.VMEM((QUAD * QUAD, ATTN_BLOCK, LANE), F32)] * 5,
        args=[proj, proj, proj, bias], sem=("parallel", "arbitrary"), ride=ride)


def attn_bwd(proj, b_out, dmix, lse_tot, bias, nb_local, name, ride=None):
    n = proj.shape[0]
    nseg = len(DILATIONS)
    a_blocks = A_WIDTH // LANE

    def body(q_ref, k_ref, v_ref, o_ref, do_ref, lse_ref, b_ref, dq_ref, dk_ref, dv_ref, db_ref,
             dqs_ref, delta_ref, dka_ref, dva_ref, q4_ref, k4_ref, v4_ref, do4_ref, lse4_ref, delta4_ref,
             dqs4_ref, dk4_ref, dv4_ref, q16_ref, k16_ref, v16_ref, do16_ref, lse16_ref, delta16_ref,
             dqs16_ref, dk16_ref, dv16_ref):
        @pl.when(pl.program_id(1) == 0)
        def _():
            db_ref[...] = jnp.zeros_like(db_ref)

        quads = (q4_ref, k4_ref, v4_ref, do4_ref, lse4_ref, delta4_ref)
        hexes = (q16_ref, k16_ref, v16_ref, do16_ref, lse16_ref, delta16_ref)

        head_sum = (lax.broadcasted_iota(jnp.int32, (LANE, LANE), 0) // HEAD_DIM
                    == lax.broadcasted_iota(jnp.int32, (LANE, LANE), 1) // HEAD_DIM).astype(BF16)

        def row_dots(i, carry):
            rows = pl.ds(pl.multiple_of(i * MERGE_ROWS, MERGE_ROWS), MERGE_ROWS)
            delta_ref[rows, :] = _dot_sum(do_ref[rows, :] * o_ref[rows, :], head_sum)
            return carry

        lax.fori_loop(0, SEQ // MERGE_ROWS, row_dots, 0)
        for src, mid, dst in zip((q_ref, k_ref, v_ref, do_ref, lse_ref, delta_ref), quads, hexes):
            _deinterleave(src, mid)
            _deinterleave_again(mid, dst)

        def op(seg, r, start, first):
            qrows = pl.ds(start, ATTN_BLOCK)
            krows = _kv_rows(start, first)
            if seg == 0:
                srcs = (q_ref, k_ref, v_ref, do_ref, lse_ref, delta_ref)
                dq_dst, dk_dst, dv_dst = dqs_ref, dka_ref, dva_ref
            elif seg == 1:
                srcs = tuple(x.at[r] for x in quads)
                dq_dst, dk_dst, dv_dst = dqs4_ref.at[r], dk4_ref.at[r], dv4_ref.at[r]
            else:
                srcs = tuple(x.at[r] for x in hexes)
                dq_dst, dk_dst, dv_dst = dqs16_ref.at[r], dk16_ref.at[r], dv16_ref.at[r]
            q_src, k_src, v_src, do_src, lse_src, delta_src = srcs
            q2, kb, vb = q_src[qrows, :] * QK_SCALE, _keys(k_src, krows, first), _keys(v_src, krows, first)
            do2, lse2, delta2 = do_src[qrows, :], lse_src[qrows, :], delta_src[qrows, :]
            head0 = _head0_lanes()
            dqs, dk, dv = [], None, None
            for hh in range(2):
                col = slice(hh * HEAD_DIM, hh * HEAD_DIM + 1)
                q, dob = _one_head(q2, head0, hh), _one_head(do2, head0, hh)
                p = jnp.exp(_attn_scores(q, kb, b_ref[_table(seg, first), hh]) - lse2[:, col])
                dvh = _dot_tn(p.astype(BF16), dob)
                ds = p * (_dot_nt(dob, vb) - delta2[:, col])
                if first:
                    db_ref[seg, hh, :, ATTN_BLOCK:] += ds[:, ATTN_BLOCK:]
                else:
                    db_ref[seg, hh] += ds
                dsb = ds.astype(BF16)
                dqs.append(_dot(dsb, kb))
                dkh = _dot_tn(dsb, q)
                dk = dkh if dk is None else dk + dkh
                dv = dvh if dv is None else dv + dvh
            dq_dst[qrows, :] = jnp.where(head0, dqs[0], dqs[1]) * QK_SCALE
            dk_dst[qrows, :] = dk[ATTN_BLOCK:]
            dv_dst[qrows, :] = dv[ATTN_BLOCK:]
            if not first:
                before = pl.ds(start - ATTN_BLOCK, ATTN_BLOCK)
                dk_dst[before, :] += dk[:ATTN_BLOCK]
                dv_dst[before, :] += dv[:ATTN_BLOCK]

        _attn_schedule(op)
        _interleave_back(dqs16_ref, dqs4_ref, QUAD)
        _interleave_back(dk16_ref, dk4_ref, 0, accumulate=True)
        _interleave_back(dv16_ref, dv4_ref, 0, accumulate=True)

        for r, nat, quad in _quad_tiles():
            dqs_ref[nat, :] += dqs4_ref[r, quad, :] + dqs4_ref[QUAD + r, quad, :]
            dka_ref[nat, :] += dk4_ref[r, quad, :]
            dva_ref[nat, :] += dv4_ref[r, quad, :]

        def merge(i, carry):
            rows = pl.ds(pl.multiple_of(i * MERGE_ROWS, MERGE_ROWS), MERGE_ROWS)
            dq_ref[rows, :] = dqs_ref[rows, :].astype(BF16)
            dk_ref[rows, :] = dka_ref[rows, :].astype(BF16)
            dv_ref[rows, :] = dva_ref[rows, :].astype(BF16)
            return carry

        lax.fori_loop(0, SEQ // MERGE_ROWS, merge, 0)

    def pspec(off):
        return pl.BlockSpec((SEQ, LANE), lambda p, b: (b, off // LANE + p))

    ospec = pl.BlockSpec((SEQ, LANE), lambda p, b: (b, p))
    bspec = pl.BlockSpec((nseg, 2, ATTN_BLOCK, 2 * ATTN_BLOCK), lambda p, b: (0, p, 0, 0))
    gshape = SDS((n, B_WIDTH), BF16)
    return _call(
        body, name=name, grid=(HEAD_PAIRS, nb_local),
        in_specs=[pspec(Q_OFF), pspec(K_OFF), pspec(V_OFF), ospec,
                  pl.BlockSpec((SEQ, LANE), lambda p, b: (b, a_blocks + p)), ospec,
                  pl.BlockSpec((2 * nseg, 2, ATTN_BLOCK, 2 * ATTN_BLOCK), lambda p, b: (0, p, 0, 0))],
        out_specs=[ospec, ospec, ospec, bspec],
        out_shape=[gshape, gshape, gshape, SDS((nseg, B_HEADS, ATTN_BLOCK, 2 * ATTN_BLOCK), F32)],
        scratch_shapes=[pltpu.VMEM((SEQ, LANE), F32)] * 4 + [pltpu.VMEM((QUAD, QUAD_ROWS, LANE), F32)] * 6
        + [pltpu.VMEM((2 * QUAD, QUAD_ROWS, LANE), F32)] + [pltpu.VMEM((QUAD, QUAD_ROWS, LANE), F32)] * 2
        + [pltpu.VMEM((QUAD * QUAD, ATTN_BLOCK, LANE), F32)] * 9,
        args=[proj, proj, proj, b_out, dmix, lse_tot, bias], sem=("arbitrary", "arbitrary"), ride=ride)


PAD = 8
CONV_ROWS = 64


CONV_LANES = 128


def _conv_taps(gp_ref, head_ref, r0, ls):
    g0 = gp_ref[r0:r0 + CONV_ROWS, ls]
    if r0 == 0:
        return g0, head_ref[PAD - 1:PAD - 1 + CONV_ROWS, ls], head_ref[PAD - 2:PAD - 2 + CONV_ROWS, ls]
    return g0, gp_ref[r0 - 1:r0 - 1 + CONV_ROWS, ls], gp_ref[r0 - 2:r0 - 2 + CONV_ROWS, ls]


def _fill_head(gp_ref, head_ref):
    head_ref[0:PAD, :] = jnp.zeros((PAD, CONV_LANES), F32)
    head_ref[PAD:PAD + CONV_ROWS, :] = gp_ref[0:CONV_ROWS, :]


def _lane_passes():
    return [slice(l0, l0 + LANE) for l0 in range(0, CONV_LANES, LANE)]


def conv_gelu_bwd(dgu, gp, up, cw, cb, nb_local, name, ride=None):
    n, f = gp.shape

    def fold(v):
        return jnp.sum(v.reshape(CONV_ROWS // 8, 8, LANE), axis=0)

    def body(dgu_ref, gp_ref, up_ref, cw_ref, cb_ref, dgp_ref, dup_ref, dcw_ref, dcb_ref, head_ref, dc_ref):
        b = pl.program_id(1)
        _fill_head(gp_ref, head_ref)
        dc_ref[SEQ:SEQ + PAD, :] = jnp.zeros((PAD, CONV_LANES), F32)
        for ls in _lane_passes():
            w0, w1, w2, bias = cw_ref[0:1, ls], cw_ref[1:2, ls], cw_ref[2:3, ls], cb_ref[:, ls]
            sums = [jnp.zeros((8, LANE), F32) for _ in range(4)]
            for r0 in range(0, SEQ, CONV_ROWS):
                rows = slice(r0, r0 + CONV_ROWS)
                g0, g1, g2 = _conv_taps(gp_ref, head_ref, r0, ls)
                gg, dgg = _gelu_and_grad(bias + w0 * g2 + w1 * g1 + w2 * g0)
                dgu = dgu_ref[rows, ls].astype(F32)
                dup_ref[rows, ls] = (dgu * gg).astype(BF16)
                dc = dgu * up_ref[rows, ls] * dgg
                dc_ref[rows, ls] = dc
                sums = [sums[0] + fold(dc * g2), sums[1] + fold(dc * g1), sums[2] + fold(dc * g0), sums[3] + fold(dc)]
            for r0 in range(0, SEQ, CONV_ROWS):
                dgp_ref[r0:r0 + CONV_ROWS, ls] = (
                    w2 * dc_ref[r0:r0 + CONV_ROWS, ls] + w1 * dc_ref[r0 + 1:r0 + 1 + CONV_ROWS, ls]
                    + w0 * dc_ref[r0 + 2:r0 + 2 + CONV_ROWS, ls]).astype(BF16)
            dcw = jnp.concatenate([jnp.sum(s, axis=0, keepdims=True) for s in sums[:3]], axis=0)
            dcb = jnp.sum(sums[3], axis=0, keepdims=True)

            @pl.when(b == 0)
            def _(dcw=dcw, dcb=dcb, ls=ls):
                dcw_ref[:, ls] = dcw
                dcb_ref[:, ls] = dcb

            @pl.when(b > 0)
            def _(dcw=dcw, dcb=dcb, ls=ls):
                dcw_ref[:, ls] += dcw
                dcb_ref[:, ls] += dcb

    blk = pl.BlockSpec((SEQ, CONV_LANES), lambda j, b: (b, j))
    wspec = pl.BlockSpec((3, CONV_LANES), lambda j, b: (0, j))
    bspec = pl.BlockSpec((1, CONV_LANES), lambda j, b: (0, j))
    return _call(
        body, name=name, grid=(f // CONV_LANES, nb_local),
        in_specs=[blk, blk, blk, wspec, bspec], out_specs=[blk, blk, wspec, bspec],
        out_shape=[SDS((n, f), BF16), SDS((n, f), BF16), SDS((3, f), F32), SDS((1, f), F32)],
        scratch_shapes=[pltpu.VMEM((PAD + CONV_ROWS, CONV_LANES), F32), pltpu.VMEM((SEQ + PAD, CONV_LANES), F32)],
        args=[dgu, gp, up, cw, cb], sem=("parallel", "arbitrary"), ride=ride)


def norm_mid_epilogue(x1, dout, z2, g3, g2):
    n, d = x1.shape

    def fn(dh2, step, x1_ref, dout_ref, z2_ref, g3_ref, g2_ref, dx1_ref, dz2_ref, dg3_ref, dg2_ref):
        dxa, dg3r = _rms_bwd(dh2, x1_ref[...], g3_ref[...])
        dx1 = dout_ref[...] + dxa
        dx1_ref[...] = dx1
        dz2, dg2r = _rms_bwd(dx1, z2_ref[...], g2_ref[...])
        dz2_ref[...] = dz2.astype(BF16)
        _accumulate(dg3_ref, jnp.sum(dg3r, axis=0, keepdims=True), step)
        _accumulate(dg2_ref, jnp.sum(dg2r, axis=0, keepdims=True), step)

    return fn, [x1, dout, z2, g3, g2], [SDS((n, d), F32), SDS((n, d), BF16), SDS((1, d), F32), SDS((1, d), F32)]


def norm_in_epilogue(x, dx1, g1):
    n, d = x.shape

    def fn(dh1, step, x_ref, dx1_ref, g1_ref, dx_ref, dg1_ref):
        dxa, dgr = _rms_bwd(dh1, x_ref[...], g1_ref[...])
        dx_ref[...] = dx1_ref[...] + dxa
        _accumulate(dg1_ref, jnp.sum(dgr, axis=0, keepdims=True), step)

    return fn, [x, dx1, g1], [SDS((n, d), F32), SDS((1, d), F32)]


def cast_bf16(arrays, name):
    def body(*refs):
        for i_ref, o_ref in zip(refs[:len(arrays)], refs[len(arrays):]):
            o_ref[...] = i_ref[...].astype(BF16)

    return pl.pallas_call(body, name=name, out_shape=[SDS(a.shape, BF16) for a in arrays],
                          compiler_params=_params())(*arrays)


def adam_update(parts, w, m, v, name, tr=None):
    s, r, c = parts.shape
    tr = r if tr is None else tr
    bc1 = 1.0 - ADAM_B1 ** ADAM_STEP
    bc2 = 1.0 - ADAM_B2 ** ADAM_STEP

    def body(p_ref, w_ref, m_ref, v_ref, g_ref, d_ref, nm_ref, nv_ref):
        g = p_ref[0].astype(F32)
        for j in range(1, s):
            g = g + p_ref[j].astype(F32)
        nm = ADAM_B1 * m_ref[...] + (1.0 - ADAM_B1) * g
        nv = ADAM_B2 * v_ref[...] + (1.0 - ADAM_B2) * (g * g)
        g_ref[...] = g
        nm_ref[...] = nm
        nv_ref[...] = nv
        d_ref[...] = -ADAM_LR * ((nm / bc1) / (jnp.sqrt(nv / bc2) + ADAM_EPS) + ADAM_WD * w_ref[...])

    blk = pl.BlockSpec((tr, c), lambda i: (i, 0))
    return pl.pallas_call(
        body, name=name, grid=(r // tr,),
        in_specs=[pl.BlockSpec((s, tr, c), lambda i: (0, i, 0)), blk, blk, blk],
        out_specs=[blk] * 4, out_shape=[SDS((r, c), F32)] * 4,
        compiler_params=_params(("parallel",)),
    )(parts, w, m, v)


EARLY_NAMES = ("spatial_w", "norm_mix_post", "norm_ffn_pre", "norm_ffn_post", "conv_b", "ln_v_gain", "ln_v_bias",
               "spatial_b")
LATE_NAMES = ("norm_mix_pre", "rel_bias")
PACK_ROW_ALIGN = 8


def _pack_rows(size):
    rows = -(-size // LANE)
    return -(-rows // PACK_ROW_ALIGN) * PACK_ROW_ALIGN


def _pack(arrays):
    flat = []
    for a in arrays:
        rows = _pack_rows(a.size)
        flat.append(jnp.pad(a.reshape(-1), (0, rows * LANE - a.size)))
    return jnp.concatenate(flat).reshape(-1, LANE)


def _unpack(packed, shapes):
    out, row = [], 0
    for shp in shapes:
        size = int(np.prod(shp))
        out.append(packed[row:row + _pack_rows(size)].reshape(-1)[:size].reshape(shp))
        row += _pack_rows(size)
    return out


def kernel(x, norm_mix_pre, norm_mix_post, norm_ffn_pre, norm_ffn_post, w_in, ln_v_gain, ln_v_bias, spatial_w, spatial_b, rel_bias, w_out, w_gate, w_up, conv_w, conv_b, w_down, loss_target, m_norm_mix_pre, m_norm_mix_post, m_norm_ffn_pre, m_norm_ffn_post, m_w_in, m_ln_v_gain, m_ln_v_bias, m_spatial_w, m_spatial_b, m_rel_bias, m_w_out, m_w_gate, m_w_up, m_conv_w, m_conv_b, m_w_down, v_norm_mix_pre, v_norm_mix_post, v_norm_ffn_pre, v_norm_ffn_post, v_w_in, v_ln_v_gain, v_ln_v_bias, v_spatial_w, v_spatial_b, v_rel_bias, v_w_out, v_w_gate, v_w_up, v_conv_w, v_conv_b, v_w_down):
    given = dict(locals())
    nb_local, seq, d = x.shape
    n = nb_local * seq
    cols = w_in.shape[2]

    def by_columns(g):
        return g.transpose(1, 0, 2).reshape(g.shape[1], N_DEV * g.shape[2])

    def by_rows(g):
        return g.reshape(N_DEV * g.shape[1], g.shape[2])

    def blocks(g):
        return g.reshape(N_DEV, g.shape[0] // N_DEV, g.shape[1])

    xf, target = x.reshape(n, d), loss_target.reshape(n, d)
    ln_g, ln_b = ln_v_gain.reshape(1, A_WIDTH), ln_v_bias.reshape(1, A_WIDTH)
    spatial_bt, rel_bias_t = spatial_b[0].T, rel_bias.T

    s_in, s_out, s_gate, s_up, s_down = cast_bf16(
        [w_in[0].T, w_out[0], w_gate[0].T, w_up[0].T, w_down[0]], "cast_shards")
    (bias,), (g_in, g_cw) = bias_tables(rel_bias_t, "bias_tables", ride=([], [s_in, conv_w[0]]))
    w_in_t, conv_w_f = by_rows(g_in), by_columns(g_cw)

    (h1, proj), _ = norm_mm(xf, norm_mix_pre, [w_in_t], "fwd_norm_in", tn=IN_COLS)
    a = gating_fwd(proj, ln_g, ln_b, spatial_w[0], spatial_bt, "fwd_gating")
    (b_out, lse_tot), (g_out, g_gate, g_up) = attn_fwd(proj, bias, nb_local, "fwd_attn",
                                                       ride=([], [s_out, s_gate, s_up]))
    w_out_f, w_gate_t, w_up_t = by_rows(g_out), by_rows(g_gate), by_rows(g_up)
    z2, x1 = mm_res_norm([a, b_out], w_out_f, xf, norm_mix_post, "fwd_out_norm")
    (h2, gp, up, gu), (g_down,) = norm_mm(x1, norm_ffn_pre, [w_gate_t, w_up_t], "fwd_norm_ffn_conv", tm=256, tn=D_FF,
                                          ride=([], [s_down]), conv=(conv_w_f, conv_b))
    w_down_f = by_rows(g_down)
    dy, dout, dg4, loss_part = down_loss(gu, w_down_f, x1, norm_ffn_post, target, "fwd_down_loss")

    p_down = mm_tn([gu], [dy], "bwd_dw_down", t1=256, t2=D_MODEL)
    (dgu,), _ = mm_nt([(dy, 0, 0)], [w_down_f], "bwd_dgu", out_dtype=BF16)
    (dgp, dup, p_conv_w, p_conv_b), (r_down,) = conv_gelu_bwd(
        dgu, gp, up, conv_w_f, conv_b, nb_local, "bwd_conv_gelu", ride=([blocks(p_down)], []))
    p_gate = mm_tn([dgp], [h2], "bwd_dw_gate", t1=256, t2=D_MODEL)
    p_up = mm_tn([dup], [h2], "bwd_dw_up", t1=256, t2=D_MODEL)
    (dx1, dz2, dg3, dg2), _ = mm_nt([(dgp, 0, 0), (dup, 1, 0)], [w_gate_t, w_up_t], "bwd_dh2_norm_mid", tm=256,
                                    by_rows=True,
                                    epilogue=norm_mid_epilogue(x1, dout, z2, norm_ffn_pre, norm_mix_post))
    p_out = mm_tn([a, b_out], [dz2], "bwd_dw_out", t1=256, t2=D_MODEL)
    (dmix,), _ = mm_nt([(dz2, 0, 0)], [w_out_f], "bwd_dmix")
    duv, p_ws, p_sbt, p_lng, p_lnb = gating_bwd(proj, dmix, ln_g, ln_b, spatial_w[0], spatial_bt, "bwd_gating")
    small = dict(spatial_w=p_ws, norm_mix_post=dg2, norm_ffn_pre=dg3, norm_ffn_post=dg4, conv_b=p_conv_b,
                 ln_v_gain=p_lng, ln_v_bias=p_lnb, spatial_b=p_sbt.T)
    pack_early = _pack([small[k] for k in EARLY_NAMES] + [p_conv_w, loss_part])
    (dq, dk, dv, dbias), (r_gate, r_up, r_out, r_early) = attn_bwd(
        proj, b_out, dmix, lse_tot, bias, nb_local, "bwd_attn",
        ride=([blocks(p_gate), blocks(p_up), blocks(p_out)], [pack_early]))
    p_rel_bias_t = rel_bias_grad(dbias.reshape(len(DILATIONS), B_HEADS, BIAS_SIZE), "bwd_rel_bias")
    p_in = mm_tn([duv, dq, dk, dv], [h1], "bwd_dw_in", t1=256, t2=D_MODEL)
    (grad_x, dg1), (r_in,) = mm_nt(
        [(duv, 0, 0), (dq, 0, Q_OFF), (dk, 0, K_OFF), (dv, 0, V_OFF)], [w_in_t], "bwd_dh1_norm_in", by_rows=True,
        epilogue=norm_in_epilogue(xf, dx1, norm_mix_pre), ride=([blocks(p_in)], []))
    small.update(norm_mix_pre=dg1, rel_bias=p_rel_bias_t.T)
    pack_late = _pack([small[k] for k in LATE_NAMES])
    (r_late,) = exchange([jnp.broadcast_to(pack_late, (N_DEV,) + pack_late.shape)], [], "exchange_late")

    res = {}
    for k, received in (("w_in", r_in), ("w_gate", r_gate), ("w_up", r_up)):
        res[k] = [o.T for o in adam_update(received, given[k][0].T, given["m_" + k][0].T, given["v_" + k][0].T,
                                           "adam_" + k)]
    res["w_out"] = adam_update(r_out, w_out[0], m_w_out[0], v_w_out[0], "adam_w_out")
    res["w_down"] = adam_update(r_down, w_down[0], m_w_down[0], v_w_down[0], "adam_w_down")

    def adam_packed(received, names, tail, name):
        zeros = [jnp.zeros_like(t) for t in tail]
        packs = [_pack([given[pre + k] for k in names] + zeros) for pre in ("", "m_", "v_")]
        shapes = [given[k].shape for k in names] + [t.shape for t in tail]
        unpacked = [_unpack(p, shapes) for p in adam_update(received, *packs, name)]
        for i, k in enumerate(names):
            res[k] = [u[i] for u in unpacked]
        return unpacked[0][len(names):]

    g_conv_w_full, loss_sum = adam_packed(r_early, EARLY_NAMES, [p_conv_w, loss_part], "adam_small_early")
    adam_packed(r_late, LATE_NAMES, [], "adam_small_late")
    g_conv_w = lax.dynamic_slice_in_dim(g_conv_w_full, _my_index() * cols, cols, axis=1)
    res["conv_w"] = adam_update(g_conv_w[None], conv_w[0], m_conv_w[0], v_conv_w[0], "adam_conv_w")
    loss = loss_sum[0, 0]

    names = ("norm_mix_pre", "norm_mix_post", "norm_ffn_pre", "norm_ffn_post", "w_in", "ln_v_gain", "ln_v_bias",
             "spatial_w", "spatial_b", "rel_bias", "w_out", "w_gate", "w_up", "conv_w", "conv_b", "w_down")
    outs = [loss, grad_x.reshape(x.shape)]
    for t in range(4):
        outs += [res[k][t].reshape(given[k].shape) for k in names]
    return tuple(outs)
```

```python
import functools
import math

import numpy as np
import jax
import jax.numpy as jnp
from jax import lax
from jax.experimental import pallas as pl
from jax.experimental.pallas import tpu as pltpu

F32 = jnp.float32
BF16 = jnp.bfloat16
SDS = jax.ShapeDtypeStruct

D_MODEL = 1024
SEQ = 2048
HEAD_DIM = 64
A_GROUPS = 4
A_WIDTH = A_GROUPS * HEAD_DIM
B_HEADS = 12
B_WIDTH = B_HEADS * HEAD_DIM
HEAD_PAIRS = B_HEADS // 2
CHUNK = 128
ATTN_BLOCK = 128
DILATIONS = (1, 4, 16)
NUM_BUCKETS = 32
MAX_DISTANCE = 2048
D_FF = 2816
IN_COLS = 2 * A_WIDTH + 3 * B_WIDTH
Q_OFF = 2 * A_WIDTH
K_OFF = Q_OFF + B_WIDTH
V_OFF = K_OFF + B_WIDTH
NORM_EPS = 1e-6
NEG_INF = -1e30
N_DEV = 8
LANE = 128

ADAM_LR = 0.001
ADAM_B1 = 0.9
ADAM_B2 = 0.999
ADAM_EPS = 1e-08
ADAM_WD = 0.01
ADAM_STEP = 10

GELU_C0 = math.sqrt(2.0 / math.pi)
GELU_C1 = 0.044715

VMEM_LIMIT = 56 * 1024 * 1024


def _params(sem=None):
    if sem is None:
        return pltpu.CompilerParams(vmem_limit_bytes=VMEM_LIMIT)
    return pltpu.CompilerParams(dimension_semantics=sem, vmem_limit_bytes=VMEM_LIMIT)


def _gelu(x):
    t = jnp.tanh(x * (GELU_C0 + (GELU_C0 * GELU_C1) * (x * x)))
    return x * (0.5 + 0.5 * t)


def _gelu_and_grad(x):
    x2 = x * x
    t = jnp.tanh(x * (GELU_C0 + (GELU_C0 * GELU_C1) * x2))
    half = 0.5 + 0.5 * t
    dg = half + x * (0.5 - 0.5 * (t * t)) * (GELU_C0 + (3.0 * GELU_C0 * GELU_C1) * x2)
    return x * half, dg


def _dot(a, b):
    return jnp.dot(a, b, preferred_element_type=F32)


def _dot_nt(a, b):
    return lax.dot_general(a, b, (((1,), (1,)), ((), ())), preferred_element_type=F32)


def _dot_tn(a, b):
    return lax.dot_general(a, b, (((0,), (0,)), ((), ())), preferred_element_type=F32)


def _rms_bwd(d, xin, g):
    r = lax.rsqrt(jnp.mean(xin * xin, axis=-1, keepdims=True) + NORM_EPS)
    xh = xin * r
    gd = g * d
    dx = r * (gd - xh * jnp.mean(gd * xh, axis=-1, keepdims=True))
    return dx, d * xh


MESH = pl.DeviceIdType.MESH
ANY = pl.BlockSpec(memory_space=pl.ANY)
PEER_MASKS = tuple(range(1, N_DEV))


def _my_index():
    return lax.axis_index("x") * 4 + lax.axis_index("y") * 2 + lax.axis_index("c")


def _peer(mask):
    x, y, c = lax.axis_index("x"), lax.axis_index("y"), lax.axis_index("c")
    px = 1 - x if mask & 4 else x
    py = 1 - y if mask & 2 else y
    pc = 1 - c if mask & 1 else c
    return (px, py, pc), px * 4 + py * 2 + pc


RELAY_AT = 3
SIBLING = 1
CHIP_MASKS = (2, 4, 6)


class _Exchange:
    def __init__(self, nblocked, in_refs, out_refs, sems):
        send_sems, recv_sems, local_sems = sems
        me = _my_index()
        sibling, _ = _peer(SIBLING)
        self.local, self.first, self.relays, self.relayed_in, self.last_in = [], [], [], [], []
        for a, (in_ref, out_ref) in enumerate(zip(in_refs, out_refs)):
            def copy(src, slot, mask, to):
                return pltpu.make_async_remote_copy(
                    src_ref=src, dst_ref=out_ref.at[slot], send_sem=send_sems.at[a, mask - 1],
                    recv_sem=recv_sems.at[a, mask - 1], device_id=to, device_id_type=MESH)

            if a < nblocked:
                self.local.append(pltpu.make_async_copy(in_ref.at[me], out_ref.at[me], local_sems.at[a]))
                for mask in PEER_MASKS:
                    peer, pidx = _peer(mask)
                    self.first.append(copy(in_ref.at[pidx], me, mask, peer))
                    self.last_in.append(copy(in_ref.at[pidx], pidx, mask, peer))
                continue
            self.local.append(pltpu.make_async_copy(in_ref, out_ref.at[me], local_sems.at[a]))
            for mask in (SIBLING,) + CHIP_MASKS:
                peer, pidx = _peer(mask)
                self.first.append(copy(in_ref, me, mask, peer))
                (self.last_in if mask == SIBLING else self.relayed_in).append(copy(in_ref, pidx, mask, peer))
            for mask in CHIP_MASKS:
                _, origin = _peer(mask)
                _, far = _peer(mask | SIBLING)
                self.relays.append(copy(out_ref.at[origin], origin, mask | SIBLING, sibling))
                self.last_in.append(copy(in_ref, far, mask | SIBLING, sibling))

    def start(self):
        for cp in self.local + self.first[::-1]:
            cp.start()

    def relay(self):
        for arrived, onward in zip(self.relayed_in, self.relays):
            arrived.wait_recv()
            onward.start()

    def finish(self):
        for cp in self.first + self.relays:
            cp.wait_send()
        for cp in self.last_in:
            cp.wait_recv()
        for cp in self.local:
            cp.wait()


def _exchange_out_shape(blocked, whole):
    return [SDS(b.shape, b.dtype) for b in blocked] + [SDS((N_DEV,) + w.shape, w.dtype) for w in whole]


def _exchange_sems(n):
    return [pltpu.SemaphoreType.DMA((n, N_DEV - 1)), pltpu.SemaphoreType.DMA((n, N_DEV - 1)),
            pltpu.SemaphoreType.DMA((n,))]


def exchange(blocked, whole, name):
    nb, n = len(blocked), len(blocked) + len(whole)

    def body(*refs):
        ex = _Exchange(nb, refs[:n], refs[n:2 * n], refs[2 * n:])
        ex.start()
        ex.relay()
        ex.finish()

    return pl.pallas_call(
        body, name=name, in_specs=[ANY] * n, out_specs=[ANY] * n, out_shape=_exchange_out_shape(blocked, whole),
        scratch_shapes=_exchange_sems(n),
    )(*blocked, *whole)


def _call(body, *, name, grid, in_specs, out_specs, out_shape, args, scratch_shapes=(), sem=None, ride=None):
    out_shape, out_specs, scratch_shapes = list(out_shape), list(out_specs), list(scratch_shapes)
    if ride is None:
        outs = pl.pallas_call(body, name=name, grid=grid, in_specs=list(in_specs), out_specs=out_specs,
                              out_shape=out_shape, scratch_shapes=scratch_shapes,
                              compiler_params=_params(sem))(*args)
        return list(outs), []
    blocked, whole = ride
    cargs = list(blocked) + list(whole)
    nb, nc = len(blocked), len(cargs)
    n_in, n_out, n_scr = len(args), len(out_shape), len(scratch_shapes)
    steps = math.prod(grid)
    assert steps >= 3, grid

    def riding(*refs):
        ins, refs = refs[:n_in], refs[n_in:]
        cins, refs = refs[:nc], refs[nc:]
        outs, refs = refs[:n_out], refs[n_out:]
        couts, refs = refs[:nc], refs[nc:]
        scr, sems = refs[:n_scr], refs[n_scr:]
        step = functools.reduce(lambda acc, k: acc * grid[k] + pl.program_id(k), range(len(grid)), 0)

        @pl.when(step == 0)
        def _():
            _Exchange(nb, cins, couts, sems).start()

        @pl.when(step == RELAY_AT * steps // 4)
        def _():
            _Exchange(nb, cins, couts, sems).relay()

        body(*ins, *outs, *scr)

        @pl.when(step == steps - 1)
        def _():
            _Exchange(nb, cins, couts, sems).finish()

    res = pl.pallas_call(
        riding, name=name, grid=grid, in_specs=list(in_specs) + [ANY] * nc, out_specs=out_specs + [ANY] * nc,
        out_shape=out_shape + _exchange_out_shape(blocked, whole),
        scratch_shapes=scratch_shapes + _exchange_sems(nc),
        compiler_params=_params(("arbitrary",) * len(grid)))(*args, *cargs)
    return list(res[:n_out]), list(res[n_out:])


def norm_mm(x, g, ws, name, tm=512, tn=1408, ride=None, conv=None):
    n, d = x.shape
    f = ws[0].shape[0]
    nw = len(ws)
    extra_in, extra_spec, extra_out, extra_out_spec, scratch = [], [], [], [], []
    if conv is not None:
        assert nw == 2 and tn == f and SEQ % tm == 0 and tm % CONV_ROWS == 0
        extra_in = list(conv)
        extra_spec = [pl.BlockSpec((3, f), lambda i, j: (0, 0)), pl.BlockSpec((1, f), lambda i, j: (0, 0))]
        extra_out, extra_out_spec = [SDS((n, f), BF16)], [pl.BlockSpec((tm, f), lambda i, j: (i, 0))]
        scratch = [pltpu.VMEM((PAD + CONV_ROWS, f), F32), pltpu.VMEM((PAD, f), F32)]

    def body(x_ref, g_ref, *refs):
        w_refs, refs = refs[:nw], refs[nw:]
        conv_refs, refs = refs[:len(extra_in)], refs[len(extra_in):]
        h_ref, o_refs, refs = refs[0], refs[1:1 + nw], refs[1 + nw:]

        @pl.when(pl.program_id(1) == 0)
        def _():
            xv = x_ref[...]
            r = lax.rsqrt(jnp.mean(xv * xv, axis=-1, keepdims=True) + NORM_EPS)
            h_ref[...] = (xv * r * g_ref[...]).astype(BF16)

        h = h_ref[...]
        for w_ref, o_ref in zip(w_refs, o_refs):
            o_ref[...] = _dot_nt(h, w_ref[...])
        if conv is None:
            return
        (cw_ref, cb_ref), (gp_ref, up_ref), (gu_ref, head_ref, carry_ref) = conv_refs, o_refs, refs

        @pl.when(pl.program_id(0) % (SEQ // tm) == 0)
        def _():
            carry_ref[...] = jnp.zeros_like(carry_ref)

        head_ref[0:PAD, :] = carry_ref[...]
        head_ref[PAD:PAD + CONV_ROWS, :] = gp_ref[0:CONV_ROWS, :]
        for l0 in range(0, f, LANE):
            ls = slice(l0, l0 + LANE)
            w0, w1, w2, bias = cw_ref[0:1, ls], cw_ref[1:2, ls], cw_ref[2:3, ls], cb_ref[:, ls]
            for r0 in range(0, tm, CONV_ROWS):
                g0, g1, g2 = _conv_taps(gp_ref, head_ref, r0, ls)
                c = bias + w0 * g2 + w1 * g1 + w2 * g0
                gu_ref[r0:r0 + CONV_ROWS, ls] = (_gelu(c) * up_ref[r0:r0 + CONV_ROWS, ls]).astype(BF16)
        carry_ref[...] = gp_ref[tm - PAD:tm, :]

    return _call(
        body, name=name, grid=(n // tm, f // tn),
        in_specs=[pl.BlockSpec((tm, d), lambda i, j: (i, 0)), pl.BlockSpec((1, d), lambda i, j: (0, 0))]
        + [pl.BlockSpec((tn, d), lambda i, j: (j, 0)) for _ in ws] + extra_spec,
        out_specs=[pl.BlockSpec((tm, d), lambda i, j: (i, 0))]
        + [pl.BlockSpec((tm, tn), lambda i, j: (i, j)) for _ in ws] + extra_out_spec,
        out_shape=[SDS((n, d), BF16)] + [SDS((n, f), F32) for _ in ws] + extra_out,
        scratch_shapes=scratch,
        args=[x, g, *ws, *extra_in], sem=("parallel" if conv is None else "arbitrary", "arbitrary"), ride=ride)


def _lane_concat(refs):
    vals = [r[...].astype(BF16) for r in refs]
    return vals[0] if len(vals) == 1 else jnp.concatenate(vals, axis=1)


def mm_res_norm(a_list, w, res, g, name, tm=512):
    n = a_list[0].shape[0]
    k, d = w.shape
    na = len(a_list)

    def body(*refs):
        w_ref, res_ref, g_ref, y_ref, o_ref = refs[na:]
        y = _dot(_lane_concat(refs[:na]), w_ref[...])
        r = lax.rsqrt(jnp.mean(y * y, axis=-1, keepdims=True) + NORM_EPS)
        y_ref[...] = y
        o_ref[...] = res_ref[...] + y * r * g_ref[...]

    return pl.pallas_call(
        body, name=name, grid=(n // tm,),
        in_specs=[pl.BlockSpec((tm, a.shape[1]), lambda i: (i, 0)) for a in a_list]
        + [pl.BlockSpec((k, d), lambda i: (0, 0)),
           pl.BlockSpec((tm, d), lambda i: (i, 0)), pl.BlockSpec((1, d), lambda i: (0, 0))],
        out_specs=[pl.BlockSpec((tm, d), lambda i: (i, 0)), pl.BlockSpec((tm, d), lambda i: (i, 0))],
        out_shape=[SDS((n, d), F32), SDS((n, d), F32)],
        compiler_params=_params(("parallel",)),
    )(*a_list, w, res, g)


def down_loss(a, w, res, g, target, name, tm=512):
    n, k = a.shape
    d = w.shape[1]
    inv_d = 1.0 / d

    def body(a_ref, w_ref, res_ref, g_ref, t_ref, dy_ref, dout_ref, dg_ref, loss_ref):
        i = pl.program_id(0)
        y = _dot(a_ref[...], w_ref[...])
        gv = g_ref[...]
        r = lax.rsqrt(jnp.mean(y * y, axis=-1, keepdims=True) + NORM_EPS)
        yh = y * r
        e = res_ref[...] + yh * gv - t_ref[...]
        part = 0.5 * inv_d * jnp.sum(jnp.sum(e * e, axis=-1, keepdims=True), axis=0, keepdims=True)
        dout = e * inv_d
        dout_ref[...] = dout
        gd = gv * dout
        dy_ref[...] = (r * (gd - yh * jnp.mean(gd * yh, axis=-1, keepdims=True))).astype(BF16)
        dgp = jnp.sum(dout * yh, axis=0, keepdims=True)
        lane0 = lax.broadcasted_iota(jnp.int32, (1, LANE), 1) == 0
        lp = jnp.where(lane0, part, 0.0)

        @pl.when(i == 0)
        def _():
            dg_ref[...] = dgp
            loss_ref[...] = lp

        @pl.when(i > 0)
        def _():
            dg_ref[...] += dgp
            loss_ref[...] += lp

    return pl.pallas_call(
        body, name=name, grid=(n // tm,),
        in_specs=[pl.BlockSpec((tm, k), lambda i: (i, 0)), pl.BlockSpec((k, d), lambda i: (0, 0)),
                  pl.BlockSpec((tm, d), lambda i: (i, 0)), pl.BlockSpec((1, d), lambda i: (0, 0)),
                  pl.BlockSpec((tm, d), lambda i: (i, 0))],
        out_specs=[pl.BlockSpec((tm, d), lambda i: (i, 0)), pl.BlockSpec((tm, d), lambda i: (i, 0)),
                   pl.BlockSpec((1, d), lambda i: (0, 0)), pl.BlockSpec((1, LANE), lambda i: (0, 0))],
        out_shape=[SDS((n, d), BF16), SDS((n, d), F32), SDS((1, d), F32), SDS((1, LANE), F32)],
        compiler_params=_params(("arbitrary",)),
    )(a, w, res, g, target)


def _accumulate(ref, val, step):
    @pl.when(step == 0)
    def _():
        ref[...] = val

    @pl.when(step > 0)
    def _():
        ref[...] += val


def mm_nt(terms, ws, name, tm=512, out_dtype=F32, ride=None, epilogue=None, by_rows=False):
    n = terms[0][0].shape[0]
    r = ws[0].shape[1 if by_rows else 0]
    na = len(terms)
    meta = [(widx, off, a.shape[1]) for a, widx, off in terms]
    fn, extras, out_shape = epilogue if epilogue else (None, [], [SDS((n, r), out_dtype)])
    n_fixed = na + len(ws)

    def body(*refs):
        a_refs = refs[:na]
        w_refs = refs[na:n_fixed]
        acc = None
        for a_ref, (widx, off, k) in zip(a_refs, meta):
            a = a_ref[...].astype(BF16)
            p = _dot(a, w_refs[widx][off:off + k, :]) if by_rows else _dot_nt(a, w_refs[widx][:, off:off + k])
            acc = p if acc is None else acc + p
        if fn is None:
            refs[-1][...] = acc.astype(out_dtype)
        else:
            fn(acc, pl.program_id(0), *refs[n_fixed:])

    def spec(a):
        if a.shape[0] == 1:
            return pl.BlockSpec(a.shape, lambda i: (0, 0))
        return pl.BlockSpec((tm, a.shape[1]), lambda i: (i, 0))

    return _call(
        body, name=name, grid=(n // tm,),
        in_specs=[spec(a) for a, _, _ in terms] + [pl.BlockSpec(w.shape, lambda i: (0, 0)) for w in ws]
        + [spec(e) for e in extras],
        out_specs=[spec(o) for o in out_shape], out_shape=out_shape,
        args=[a for a, _, _ in terms] + list(ws) + list(extras),
        sem=("parallel",) if fn is None else ("arbitrary",), ride=ride)


def _piece_blocks(pieces, tile):
    out, first = [], 0
    for p in pieces:
        nblk, rem = divmod(p.shape[1], tile)
        assert rem == 0, (p.shape, tile)
        out.append((first, nblk))
        first += nblk
    return out, first


def mm_tn(lhs_list, rhs_list, name, t1, t2, out_dtype=BF16):
    n = lhs_list[0].shape[0]
    lblocks, nbl = _piece_blocks(lhs_list, t1)
    rblocks, nbr = _piece_blocks(rhs_list, t2)
    nl = len(lhs_list)

    def body(*refs):
        l_refs, r_refs, o_ref = refs[:nl], refs[nl:-1], refs[-1]
        i, j = pl.program_id(0), pl.program_id(1)
        for l_ref, (ls, ln) in zip(l_refs, lblocks):
            for r_ref, (rs, rn) in zip(r_refs, rblocks):
                @pl.when((i >= ls) & (i < ls + ln) & (j >= rs) & (j < rs + rn))
                def _(l_ref=l_ref, r_ref=r_ref):
                    o_ref[...] = _dot_tn(l_ref[...].astype(BF16), r_ref[...].astype(BF16)).astype(out_dtype)

    def piece_spec(tile, axis, first, nblk):
        def index(i, j):
            return 0, jnp.clip((i, j)[axis] - first, 0, nblk - 1)
        return pl.BlockSpec((n, tile), index)

    return pl.pallas_call(
        body, name=name, grid=(nbl, nbr),
        in_specs=[piece_spec(t1, 0, *b) for b in lblocks] + [piece_spec(t2, 1, *b) for b in rblocks],
        out_specs=pl.BlockSpec((t1, t2), lambda i, j: (i, j)),
        out_shape=SDS((nbl * t1, nbr * t2), out_dtype),
        compiler_params=_params(("parallel", "arbitrary")),
    )(*lhs_list, *rhs_list)


GATE_ROWS = 512


def _tril_mask():
    row = lax.broadcasted_iota(jnp.int32, (CHUNK, CHUNK), 0)
    col = lax.broadcasted_iota(jnp.int32, (CHUNK, CHUNK), 1)
    return row >= col


def _group_of(shape, axis):
    return lax.broadcasted_iota(jnp.int32, shape, axis) // HEAD_DIM


def _group_mean_matrix():
    same = _group_of((A_WIDTH, A_WIDTH), 0) == _group_of((A_WIDTH, A_WIDTH), 1)
    return jnp.where(same, 1.0 / HEAD_DIM, 0.0).astype(BF16)


def _dot_sum(a, b):
    hi = a.astype(BF16)
    lo = (a - hi.astype(F32)).astype(BF16)
    return _dot(hi, b) + _dot(lo, b)


def _by_group(parts, lane_group):
    out = parts[A_GROUPS - 1]
    for g in range(A_GROUPS - 2, -1, -1):
        out = jnp.where(lane_group == g, parts[g], out)
    return out


def _group_norm(gv, gmean):
    xc = gv - _dot_sum(gv, gmean)
    rstd = lax.rsqrt(_dot_sum(xc * xc, gmean) + NORM_EPS)
    return xc * rstd, rstd


def gating_fwd(proj, lng, lnb, ws, sbt, name):
    n = proj.shape[0]

    def body(u_ref, v_ref, lng_ref, lnb_ref, ws_ref, sbt_ref, a_ref):
        tril = _tril_mask()
        lane_group = _group_of((CHUNK, A_WIDTH), 1)
        gmean = _group_mean_matrix()
        wts = [jnp.where(tril, ws_ref[g], 0.0).astype(BF16) for g in range(A_GROUPS)]
        sb = _by_group([sbt_ref[:, g:g + 1] for g in range(A_GROUPS)], lane_group)

        def chunk(c, carry):
            rows = pl.ds(pl.multiple_of(c * CHUNK, CHUNK), CHUNK)
            vhat, _ = _group_norm(_gelu(v_ref[rows, :]), gmean)
            vn = (vhat * lng_ref[...] + lnb_ref[...]).astype(BF16)
            z = _by_group([_dot(wt, vn) for wt in wts], lane_group) + sb
            a_ref[rows, :] = (_gelu(u_ref[rows, :]) * z).astype(BF16)
            return carry

        lax.fori_loop(0, GATE_ROWS // CHUNK, chunk, 0)

    return pl.pallas_call(
        body, name=name, grid=(n // GATE_ROWS,),
        in_specs=[pl.BlockSpec((GATE_ROWS, A_WIDTH), lambda i: (i, 0)),
                  pl.BlockSpec((GATE_ROWS, A_WIDTH), lambda i: (i, 1)),
                  pl.BlockSpec((1, A_WIDTH), lambda i: (0, 0)), pl.BlockSpec((1, A_WIDTH), lambda i: (0, 0)),
                  pl.BlockSpec((A_GROUPS, CHUNK, CHUNK), lambda i: (0, 0, 0)),
                  pl.BlockSpec((CHUNK, A_GROUPS), lambda i: (0, 0))],
        out_specs=pl.BlockSpec((GATE_ROWS, A_WIDTH), lambda i: (i, 0)),
        out_shape=SDS((n, A_WIDTH), BF16),
        compiler_params=_params(("parallel",)),
    )(proj, proj, lng, lnb, ws, sbt)


def gating_bwd(proj, dmix, lng, lnb, ws, sbt, name):
    n = proj.shape[0]

    def body(u_ref, v_ref, da_ref, lng_ref, lnb_ref, ws_ref, sbt_ref,
             duv_ref, dws_ref, dsbt_ref, dlng_ref, dlnb_ref):
        @pl.when(pl.program_id(0) == 0)
        def _():
            dws_ref[...] = jnp.zeros_like(dws_ref)
            dsbt_ref[...] = jnp.zeros_like(dsbt_ref)
            dlng_ref[...] = jnp.zeros_like(dlng_ref)
            dlnb_ref[...] = jnp.zeros_like(dlnb_ref)

        tril = _tril_mask()
        lane_group = _group_of((CHUNK, A_WIDTH), 1)
        gmean = _group_mean_matrix()
        gsum = (_group_of((A_WIDTH, LANE), 0) == lax.broadcasted_iota(jnp.int32, (A_WIDTH, LANE), 1)).astype(BF16)
        wts = [jnp.where(tril, ws_ref[g], 0.0) for g in range(A_GROUPS)]
        wts_b = [w.astype(BF16) for w in wts]
        wts_t = [w.T.astype(BF16) for w in wts]
        sb = _by_group([sbt_ref[:, g:g + 1] for g in range(A_GROUPS)], lane_group)
        lg = lng_ref[...]

        def chunk(c, carry):
            rows = pl.ds(pl.multiple_of(c * CHUNK, CHUNK), CHUNK)
            gu, dgu_dx = _gelu_and_grad(u_ref[rows, :])
            gv, dgv_dx = _gelu_and_grad(v_ref[rows, :])
            vhat, rstd = _group_norm(gv, gmean)
            vn = (vhat * lg + lnb_ref[...]).astype(BF16)
            z = _by_group([_dot(wt, vn) for wt in wts_b], lane_group) + sb
            da = da_ref[rows, :]
            dz = da * gu
            dzb = dz.astype(BF16)
            duv_ref[rows, 0:A_WIDTH] = (da * z * dgu_dx).astype(BF16)
            dsbt_ref[...] += _dot_sum(dz, gsum)[:, 0:A_GROUPS]
            for g in range(A_GROUPS):
                dz_g = jnp.where(lane_group == g, dzb, jnp.zeros_like(dzb))
                dws_ref[g] += jnp.where(tril, _dot_nt(dz_g, vn), 0.0)
            dvn = _by_group([_dot(wt, dzb) for wt in wts_t], lane_group)
            dlng_ref[...] += jnp.sum(dvn * vhat, axis=0, keepdims=True)
            dlnb_ref[...] += jnp.sum(dvn, axis=0, keepdims=True)
            dvh = dvn * lg
            dgv = rstd * (dvh - _dot_sum(dvh, gmean) - vhat * _dot_sum(dvh * vhat, gmean))
            duv_ref[rows, A_WIDTH:2 * A_WIDTH] = (dgv * dgv_dx).astype(BF16)
            return carry

        lax.fori_loop(0, GATE_ROWS // CHUNK, chunk, 0)

    return pl.pallas_call(
        body, name=name, grid=(n // GATE_ROWS,),
        in_specs=[pl.BlockSpec((GATE_ROWS, A_WIDTH), lambda i: (i, 0)),
                  pl.BlockSpec((GATE_ROWS, A_WIDTH), lambda i: (i, 1)),
                  pl.BlockSpec((GATE_ROWS, A_WIDTH), lambda i: (i, 0)),
                  pl.BlockSpec((1, A_WIDTH), lambda i: (0, 0)), pl.BlockSpec((1, A_WIDTH), lambda i: (0, 0)),
                  pl.BlockSpec((A_GROUPS, CHUNK, CHUNK), lambda i: (0, 0, 0)),
                  pl.BlockSpec((CHUNK, A_GROUPS), lambda i: (0, 0))],
        out_specs=[pl.BlockSpec((GATE_ROWS, 2 * A_WIDTH), lambda i: (i, 0)),
                   pl.BlockSpec((A_GROUPS, CHUNK, CHUNK), lambda i: (0, 0, 0)),
                   pl.BlockSpec((CHUNK, A_GROUPS), lambda i: (0, 0)),
                   pl.BlockSpec((1, A_WIDTH), lambda i: (0, 0)), pl.BlockSpec((1, A_WIDTH), lambda i: (0, 0))],
        out_shape=[SDS((n, 2 * A_WIDTH), BF16), SDS((A_GROUPS, CHUNK, CHUNK), F32), SDS((CHUNK, A_GROUPS), F32),
                   SDS((1, A_WIDTH), F32), SDS((1, A_WIDTH), F32)],
        compiler_params=_params(("arbitrary",)),
    )(proj, proj, dmix, lng, lnb, ws, sbt)


def _t5_bucket_np(dist):
    max_exact = NUM_BUCKETS // 2
    dd = np.maximum(dist, 1).astype(np.float64)
    large = max_exact + np.log(dd / max_exact) / math.log(MAX_DISTANCE / max_exact) * (NUM_BUCKETS - max_exact)
    large = np.minimum(large.astype(np.int64), NUM_BUCKETS - 1)
    return np.where(dist < max_exact, dist, large)


def _bucket_tables(with_first):
    i = np.arange(ATTN_BLOCK)[:, None]
    j = np.arange(2 * ATTN_BLOCK)[None, :]
    rel = ATTN_BLOCK + i - j
    band = (rel >= 0) & (rel <= ATTN_BLOCK)
    tabs = []
    for own_only in (False, True) if with_first else (False,):
        for dil in DILATIONS:
            b = _t5_bucket_np(np.maximum(rel, 0) * dil)
            tabs.append(np.where(band & (j >= ATTN_BLOCK) if own_only else band, b, -1).reshape(1, -1))
    return np.stack(tabs).astype(np.float32)


BIAS_SIZE = ATTN_BLOCK * 2 * ATTN_BLOCK


def bias_tables(rel_bias_t, name, ride=None):
    idx = jnp.asarray(_bucket_tables(True).reshape(-1, ATTN_BLOCK, 2 * ATTN_BLOCK))
    ntab = idx.shape[0]

    def body(rb_ref, idx_ref, o_ref):
        iv = idx_ref[0]

        def head(h, carry):
            t = jnp.full(iv.shape, NEG_INF, F32)
            for b in range(NUM_BUCKETS):
                t = jnp.where(iv == float(b), rb_ref[h, b], t)
            o_ref[0, h] = t
            return carry

        lax.fori_loop(0, B_HEADS, head, 0)

    return _call(
        body, name=name, grid=(ntab,),
        in_specs=[pl.BlockSpec(memory_space=pltpu.SMEM),
                  pl.BlockSpec((1, ATTN_BLOCK, 2 * ATTN_BLOCK), lambda d: (d, 0, 0))],
        out_specs=[pl.BlockSpec((1, B_HEADS, ATTN_BLOCK, 2 * ATTN_BLOCK), lambda d: (d, 0, 0, 0))],
        out_shape=[SDS((ntab, B_HEADS, ATTN_BLOCK, 2 * ATTN_BLOCK), F32)],
        args=[rel_bias_t, idx], sem=("parallel",), ride=ride)


def rel_bias_grad(dbias, name):
    idx = jnp.asarray(_bucket_tables(False))

    def body(db_ref, idx_ref, o_ref):
        d = pl.program_id(0)
        iv = idx_ref[0]
        bk = lax.broadcasted_iota(jnp.int32, (NUM_BUCKETS, BIAS_SIZE), 0).astype(F32)
        onehot = (bk == iv).astype(BF16)
        rest, part = db_ref[0], None
        for _ in range(3):
            term = rest.astype(BF16)
            rest = rest - term.astype(F32)
            p = _dot_nt(term, onehot)
            part = p if part is None else part + p

        @pl.when(d == 0)
        def _():
            o_ref[...] = part

        @pl.when(d > 0)
        def _():
            o_ref[...] += part

    return pl.pallas_call(
        body, name=name, grid=(len(DILATIONS),),
        in_specs=[pl.BlockSpec((1, B_HEADS, BIAS_SIZE), lambda d: (d, 0, 0)),
                  pl.BlockSpec((1, 1, BIAS_SIZE), lambda d: (d, 0, 0))],
        out_specs=pl.BlockSpec((B_HEADS, NUM_BUCKETS), lambda d: (0, 0)),
        out_shape=SDS((B_HEADS, NUM_BUCKETS), F32),
        compiler_params=_params(("arbitrary",)),
    )(dbias, idx)


QK_SCALE = 1.0 / math.sqrt(HEAD_DIM)


def _attn_scores(q_scaled, kk, bias):
    return _dot_nt(q_scaled, kk) + bias


def _head0_lanes():
    return lax.broadcasted_iota(jnp.int32, (ATTN_BLOCK, LANE), 1) < HEAD_DIM


def _one_head(x2, head0, hh):
    return jnp.where(head0 if hh == 0 else jnp.logical_not(head0), x2, 0.0).astype(BF16)


QUAD = 4
QUAD_ROWS = SEQ // QUAD


def _deinterleave(src_ref, dst_ref):
    for r in range(QUAD):
        for c in range(QUAD_ROWS // ATTN_BLOCK):
            dst_ref[r, c * ATTN_BLOCK:(c + 1) * ATTN_BLOCK, :] = src_ref[
                pl.ds(r + c * QUAD * ATTN_BLOCK, ATTN_BLOCK, stride=QUAD), :]


def _deinterleave_again(src_ref, dst_ref):
    for r in range(QUAD):
        for s in range(QUAD):
            dst_ref[r + QUAD * s] = src_ref[r, pl.ds(s, ATTN_BLOCK, stride=QUAD), :]


def _interleave_back(src_ref, dst_ref, slot0, accumulate=False):
    for r in range(QUAD):
        for s in range(QUAD):
            rows = pl.ds(s, ATTN_BLOCK, stride=QUAD)
            if accumulate:
                dst_ref[slot0 + r, rows, :] += src_ref[r + QUAD * s]
            else:
                dst_ref[slot0 + r, rows, :] = src_ref[r + QUAD * s]


def _quad_tiles():
    return [(r, pl.ds(r + c * QUAD * ATTN_BLOCK, ATTN_BLOCK, stride=QUAD), slice(c * ATTN_BLOCK, (c + 1) * ATTN_BLOCK))
            for r in range(QUAD) for c in range(QUAD_ROWS // ATTN_BLOCK)]


def _attn_schedule(op):
    for r in range(QUAD * QUAD):
        op(2, r, 0, True)
    for r in range(QUAD):
        for nq in range(QUAD_ROWS // ATTN_BLOCK):
            op(1, r, nq * ATTN_BLOCK, nq == 0)
    for nq in range(SEQ // ATTN_BLOCK):
        op(0, None, nq * ATTN_BLOCK, nq == 0)


def _keys(src, krows, first):
    kb = src[krows, :].astype(BF16)
    return jnp.concatenate([kb, kb], axis=0) if first else kb


def _table(seg, first):
    return len(DILATIONS) + seg if first else seg


def _kv_rows(start, first):
    return pl.ds(start, ATTN_BLOCK) if first else pl.ds(start - ATTN_BLOCK, 2 * ATTN_BLOCK)


MERGE_ROWS = 256


def attn_fwd(proj, bias, nb_local, name, ride=None):
    n = proj.shape[0]
    nseg = len(DILATIONS)

    def body(q_ref, k_ref, v_ref, b_ref, o_ref, lse_ref, q4_ref, k4_ref, v4_ref, os0_ref, ls0_ref, os4_ref, ls4_ref,
             q16_ref, k16_ref, v16_ref, os16_ref, ls16_ref):
        for src, mid, dst in ((q_ref, q4_ref, q16_ref), (k_ref, k4_ref, k16_ref), (v_ref, v4_ref, v16_ref)):
            _deinterleave(src, mid)
            _deinterleave_again(mid, dst)

        def op(seg, r, start, first):
            qrows = pl.ds(start, ATTN_BLOCK)
            krows = _kv_rows(start, first)
            if seg == 0:
                q_src, k_src, v_src, o_dst, l_dst = q_ref, k_ref, v_ref, os0_ref, ls0_ref
            elif seg == 1:
                q_src, k_src, v_src = q4_ref.at[r], k4_ref.at[r], v4_ref.at[r]
                o_dst, l_dst = os4_ref.at[r], ls4_ref.at[r]
            else:
                q_src, k_src, v_src = q16_ref.at[r], k16_ref.at[r], v16_ref.at[r]
                o_dst, l_dst = os16_ref.at[r], ls16_ref.at[r]
            q2, kb, vb = q_src[qrows, :] * QK_SCALE, _keys(k_src, krows, first), _keys(v_src, krows, first)
            head0 = _head0_lanes()
            outs, lses = [], []
            for hh in range(2):
                s = _attn_scores(_one_head(q2, head0, hh), kb, b_ref[_table(seg, first), hh])
                m = jnp.max(s, axis=-1, keepdims=True)
                p = jnp.exp(s - m)
                l = jnp.sum(p, axis=-1, keepdims=True)
                outs.append(_dot(p.astype(BF16), vb) / l)
                lses.append(jnp.broadcast_to(m + jnp.log(l), (ATTN_BLOCK, LANE)))
            o_dst[qrows, :] = jnp.where(head0, outs[0], outs[1])
            l_dst[qrows, :] = jnp.where(head0, lses[0], lses[1])

        _attn_schedule(op)
        _interleave_back(os16_ref, os4_ref, QUAD)
        _interleave_back(ls16_ref, ls4_ref, QUAD)

        for r, nat, quad in _quad_tiles():
            ls = [ls0_ref[nat, :], ls4_ref[r, quad, :], ls4_ref[QUAD + r, quad, :]]
            m = functools.reduce(jnp.maximum, ls)
            ws = [jnp.exp(l - m) for l in ls]
            den = ws[0] + ws[1] + ws[2]
            num = ws[0] * os0_ref[nat, :] + ws[1] * os4_ref[r, quad, :] + ws[2] * os4_ref[QUAD + r, quad, :]
            o_ref[nat, :] = num / den
            lse_ref[nat, :] = m + jnp.log(den)

    def in_spec(off):
        return pl.BlockSpec((SEQ, LANE), lambda b, p: (b, off // LANE + p))

    out_spec = pl.BlockSpec((SEQ, LANE), lambda b, p: (b, p))
    return _call(
        body, name=name, grid=(nb_local, HEAD_PAIRS),
        in_specs=[in_spec(Q_OFF), in_spec(K_OFF), in_spec(V_OFF),
                  pl.BlockSpec((2 * nseg, 2, ATTN_BLOCK, 2 * ATTN_BLOCK), lambda b, p: (0, p, 0, 0))],
        out_specs=[out_spec, out_spec],
        out_shape=[SDS((n, B_WIDTH), F32), SDS((n, B_WIDTH), F32)],
        scratch_shapes=[pltpu.VMEM((QUAD, QUAD_ROWS, LANE), F32)] * 3 + [pltpu.VMEM((SEQ, LANE), F32)] * 2
        + [pltpu.VMEM((2 * QUAD, QUAD_ROWS, LANE), F32)] * 2 + [pltpu.VMEM((QUAD * QUAD, ATTN_BLOCK, LANE), F32)] * 5,
        args=[proj, proj, proj, bias], sem=("parallel", "arbitrary"), ride=ride)


def attn_bwd(proj, b_out, dmix, lse_tot, bias, nb_local, name, ride=None):
    n = proj.shape[0]
    nseg = len(DILATIONS)
    a_blocks = A_WIDTH // LANE

    def body(q_ref, k_ref, v_ref, o_ref, do_ref, lse_ref, b_ref, dq_ref, dk_ref, dv_ref, db_ref,
             dqs_ref, delta_ref, dka_ref, dva_ref, q4_ref, k4_ref, v4_ref, do4_ref, lse4_ref, delta4_ref,
             dqs4_ref, dk4_ref, dv4_ref, q16_ref, k16_ref, v16_ref, do16_ref, lse16_ref, delta16_ref,
             dqs16_ref, dk16_ref, dv16_ref):
        @pl.when(pl.program_id(1) == 0)
        def _():
            db_ref[...] = jnp.zeros_like(db_ref)

        quads = (q4_ref, k4_ref, v4_ref, do4_ref, lse4_ref, delta4_ref)
        hexes = (q16_ref, k16_ref, v16_ref, do16_ref, lse16_ref, delta16_ref)

        head_sum = (lax.broadcasted_iota(jnp.int32, (LANE, LANE), 0) // HEAD_DIM
                    == lax.broadcasted_iota(jnp.int32, (LANE, LANE), 1) // HEAD_DIM).astype(BF16)

        def row_dots(i, carry):
            rows = pl.ds(pl.multiple_of(i * MERGE_ROWS, MERGE_ROWS), MERGE_ROWS)
            delta_ref[rows, :] = _dot_sum(do_ref[rows, :] * o_ref[rows, :], head_sum)
            return carry

        lax.fori_loop(0, SEQ // MERGE_ROWS, row_dots, 0)
        for src, mid, dst in zip((q_ref, k_ref, v_ref, do_ref, lse_ref, delta_ref), quads, hexes):
            _deinterleave(src, mid)
            _deinterleave_again(mid, dst)

        def op(seg, r, start, first):
            qrows = pl.ds(start, ATTN_BLOCK)
            krows = _kv_rows(start, first)
            if seg == 0:
                srcs = (q_ref, k_ref, v_ref, do_ref, lse_ref, delta_ref)
                dq_dst, dk_dst, dv_dst = dqs_ref, dka_ref, dva_ref
            elif seg == 1:
                srcs = tuple(x.at[r] for x in quads)
                dq_dst, dk_dst, dv_dst = dqs4_ref.at[r], dk4_ref.at[r], dv4_ref.at[r]
            else:
                srcs = tuple(x.at[r] for x in hexes)
                dq_dst, dk_dst, dv_dst = dqs16_ref.at[r], dk16_ref.at[r], dv16_ref.at[r]
            q_src, k_src, v_src, do_src, lse_src, delta_src = srcs
            q2, kb, vb = q_src[qrows, :] * QK_SCALE, _keys(k_src, krows, first), _keys(v_src, krows, first)
            do2, lse2, delta2 = do_src[qrows, :], lse_src[qrows, :], delta_src[qrows, :]
            head0 = _head0_lanes()
            dqs, dk, dv = [], None, None
            for hh in range(2):
                col = slice(hh * HEAD_DIM, hh * HEAD_DIM + 1)
                q, dob = _one_head(q2, head0, hh), _one_head(do2, head0, hh)
                p = jnp.exp(_attn_scores(q, kb, b_ref[_table(seg, first), hh]) - lse2[:, col])
                dvh = _dot_tn(p.astype(BF16), dob)
                ds = p * (_dot_nt(dob, vb) - delta2[:, col])
                if first:
                    db_ref[seg, hh, :, ATTN_BLOCK:] += ds[:, ATTN_BLOCK:]
                else:
                    db_ref[seg, hh] += ds
                dsb = ds.astype(BF16)
                dqs.append(_dot(dsb, kb))
                dkh = _dot_tn(dsb, q)
                dk = dkh if dk is None else dk + dkh
                dv = dvh if dv is None else dv + dvh
            dq_dst[qrows, :] = jnp.where(head0, dqs[0], dqs[1]) * QK_SCALE
            dk_dst[qrows, :] = dk[ATTN_BLOCK:]
            dv_dst[qrows, :] = dv[ATTN_BLOCK:]
            if not first:
                before = pl.ds(start - ATTN_BLOCK, ATTN_BLOCK)
                dk_dst[before, :] += dk[:ATTN_BLOCK]
                dv_dst[before, :] += dv[:ATTN_BLOCK]

        _attn_schedule(op)
        _interleave_back(dqs16_ref, dqs4_ref, QUAD)
        _interleave_back(dk16_ref, dk4_ref, 0, accumulate=True)
        _interleave_back(dv16_ref, dv4_ref, 0, accumulate=True)

        for r, nat, quad in _quad_tiles():
            dqs_ref[nat, :] += dqs4_ref[r, quad, :] + dqs4_ref[QUAD + r, quad, :]
            dka_ref[nat, :] += dk4_ref[r, quad, :]
            dva_ref[nat, :] += dv4_ref[r, quad, :]

        def merge(i, carry):
            rows = pl.ds(pl.multiple_of(i * MERGE_ROWS, MERGE_ROWS), MERGE_ROWS)
            dq_ref[rows, :] = dqs_ref[rows, :].astype(BF16)
            dk_ref[rows, :] = dka_ref[rows, :].astype(BF16)
            dv_ref[rows, :] = dva_ref[rows, :].astype(BF16)
            return carry

        lax.fori_loop(0, SEQ // MERGE_ROWS, merge, 0)

    def pspec(off):
        return pl.BlockSpec((SEQ, LANE), lambda p, b: (b, off // LANE + p))

    ospec = pl.BlockSpec((SEQ, LANE), lambda p, b: (b, p))
    bspec = pl.BlockSpec((nseg, 2, ATTN_BLOCK, 2 * ATTN_BLOCK), lambda p, b: (0, p, 0, 0))
    gshape = SDS((n, B_WIDTH), BF16)
    return _call(
        body, name=name, grid=(HEAD_PAIRS, nb_local),
        in_specs=[pspec(Q_OFF), pspec(K_OFF), pspec(V_OFF), ospec,
                  pl.BlockSpec((SEQ, LANE), lambda p, b: (b, a_blocks + p)), ospec,
                  pl.BlockSpec((2 * nseg, 2, ATTN_BLOCK, 2 * ATTN_BLOCK), lambda p, b: (0, p, 0, 0))],
        out_specs=[ospec, ospec, ospec, bspec],
        out_shape=[gshape, gshape, gshape, SDS((nseg, B_HEADS, ATTN_BLOCK, 2 * ATTN_BLOCK), F32)],
        scratch_shapes=[pltpu.VMEM((SEQ, LANE), F32)] * 4 + [pltpu.VMEM((QUAD, QUAD_ROWS, LANE), F32)] * 6
        + [pltpu.VMEM((2 * QUAD, QUAD_ROWS, LANE), F32)] + [pltpu.VMEM((QUAD, QUAD_ROWS, LANE), F32)] * 2
        + [pltpu.VMEM((QUAD * QUAD, ATTN_BLOCK, LANE), F32)] * 9,
        args=[proj, proj, proj, b_out, dmix, lse_tot, bias], sem=("arbitrary", "arbitrary"), ride=ride)


PAD = 8
CONV_ROWS = 64


CONV_LANES = 128


def _conv_taps(gp_ref, head_ref, r0, ls):
    g0 = gp_ref[r0:r0 + CONV_ROWS, ls]
    if r0 == 0:
        return g0, head_ref[PAD - 1:PAD - 1 + CONV_ROWS, ls], head_ref[PAD - 2:PAD - 2 + CONV_ROWS, ls]
    return g0, gp_ref[r0 - 1:r0 - 1 + CONV_ROWS, ls], gp_ref[r0 - 2:r0 - 2 + CONV_ROWS, ls]


def _fill_head(gp_ref, head_ref):
    head_ref[0:PAD, :] = jnp.zeros((PAD, CONV_LANES), F32)
    head_ref[PAD:PAD + CONV_ROWS, :] = gp_ref[0:CONV_ROWS, :]


def _lane_passes():
    return [slice(l0, l0 + LANE) for l0 in range(0, CONV_LANES, LANE)]


def conv_gelu_bwd(dgu, gp, up, cw, cb, nb_local, name, ride=None):
    n, f = gp.shape

    def fold(v):
        return jnp.sum(v.reshape(CONV_ROWS // 8, 8, LANE), axis=0)

    def body(dgu_ref, gp_ref, up_ref, cw_ref, cb_ref, dgp_ref, dup_ref, dcw_ref, dcb_ref, head_ref, dc_ref):
        b = pl.program_id(1)
        _fill_head(gp_ref, head_ref)
        dc_ref[SEQ:SEQ + PAD, :] = jnp.zeros((PAD, CONV_LANES), F32)
        for ls in _lane_passes():
            w0, w1, w2, bias = cw_ref[0:1, ls], cw_ref[1:2, ls], cw_ref[2:3, ls], cb_ref[:, ls]
            sums = [jnp.zeros((8, LANE), F32) for _ in range(4)]
            for r0 in range(0, SEQ, CONV_ROWS):
                rows = slice(r0, r0 + CONV_ROWS)
                g0, g1, g2 = _conv_taps(gp_ref, head_ref, r0, ls)
                gg, dgg = _gelu_and_grad(bias + w0 * g2 + w1 * g1 + w2 * g0)
                dgu = dgu_ref[rows, ls].astype(F32)
                dup_ref[rows, ls] = (dgu * gg).astype(BF16)
                dc = dgu * up_ref[rows, ls] * dgg
                dc_ref[rows, ls] = dc
                sums = [sums[0] + fold(dc * g2), sums[1] + fold(dc * g1), sums[2] + fold(dc * g0), sums[3] + fold(dc)]
            for r0 in range(0, SEQ, CONV_ROWS):
                dgp_ref[r0:r0 + CONV_ROWS, ls] = (
                    w2 * dc_ref[r0:r0 + CONV_ROWS, ls] + w1 * dc_ref[r0 + 1:r0 + 1 + CONV_ROWS, ls]
                    + w0 * dc_ref[r0 + 2:r0 + 2 + CONV_ROWS, ls]).astype(BF16)
            dcw = jnp.concatenate([jnp.sum(s, axis=0, keepdims=True) for s in sums[:3]], axis=0)
            dcb = jnp.sum(sums[3], axis=0, keepdims=True)

            @pl.when(b == 0)
            def _(dcw=dcw, dcb=dcb, ls=ls):
                dcw_ref[:, ls] = dcw
                dcb_ref[:, ls] = dcb

            @pl.when(b > 0)
            def _(dcw=dcw, dcb=dcb, ls=ls):
                dcw_ref[:, ls] += dcw
                dcb_ref[:, ls] += dcb

    blk = pl.BlockSpec((SEQ, CONV_LANES), lambda j, b: (b, j))
    wspec = pl.BlockSpec((3, CONV_LANES), lambda j, b: (0, j))
    bspec = pl.BlockSpec((1, CONV_LANES), lambda j, b: (0, j))
    return _call(
        body, name=name, grid=(f // CONV_LANES, nb_local),
        in_specs=[blk, blk, blk, wspec, bspec], out_specs=[blk, blk, wspec, bspec],
        out_shape=[SDS((n, f), BF16), SDS((n, f), BF16), SDS((3, f), F32), SDS((1, f), F32)],
        scratch_shapes=[pltpu.VMEM((PAD + CONV_ROWS, CONV_LANES), F32), pltpu.VMEM((SEQ + PAD, CONV_LANES), F32)],
        args=[dgu, gp, up, cw, cb], sem=("parallel", "arbitrary"), ride=ride)


def norm_mid_epilogue(x1, dout, z2, g3, g2):
    n, d = x1.shape

    def fn(dh2, step, x1_ref, dout_ref, z2_ref, g3_ref, g2_ref, dx1_ref, dz2_ref, dg3_ref, dg2_ref):
        dxa, dg3r = _rms_bwd(dh2, x1_ref[...], g3_ref[...])
        dx1 = dout_ref[...] + dxa
        dx1_ref[...] = dx1
        dz2, dg2r = _rms_bwd(dx1, z2_ref[...], g2_ref[...])
        dz2_ref[...] = dz2.astype(BF16)
        _accumulate(dg3_ref, jnp.sum(dg3r, axis=0, keepdims=True), step)
        _accumulate(dg2_ref, jnp.sum(dg2r, axis=0, keepdims=True), step)

    return fn, [x1, dout, z2, g3, g2], [SDS((n, d), F32), SDS((n, d), BF16), SDS((1, d), F32), SDS((1, d), F32)]


def norm_in_epilogue(x, dx1, g1):
    n, d = x.shape

    def fn(dh1, step, x_ref, dx1_ref, g1_ref, dx_ref, dg1_ref):
        dxa, dgr = _rms_bwd(dh1, x_ref[...], g1_ref[...])
        dx_ref[...] = dx1_ref[...] + dxa
        _accumulate(dg1_ref, jnp.sum(dgr, axis=0, keepdims=True), step)

    return fn, [x, dx1, g1], [SDS((n, d), F32), SDS((1, d), F32)]


def cast_bf16(arrays, name):
    def body(*refs):
        for i_ref, o_ref in zip(refs[:len(arrays)], refs[len(arrays):]):
            o_ref[...] = i_ref[...].astype(BF16)

    return pl.pallas_call(body, name=name, out_shape=[SDS(a.shape, BF16) for a in arrays],
                          compiler_params=_params())(*arrays)


def adam_update(parts, w, m, v, name, tr=None):
    s, r, c = parts.shape
    tr = r if tr is None else tr
    bc1 = 1.0 - ADAM_B1 ** ADAM_STEP
    bc2 = 1.0 - ADAM_B2 ** ADAM_STEP

    def body(p_ref, w_ref, m_ref, v_ref, g_ref, d_ref, nm_ref, nv_ref):
        g = p_ref[0].astype(F32)
        for j in range(1, s):
            g = g + p_ref[j].astype(F32)
        nm = ADAM_B1 * m_ref[...] + (1.0 - ADAM_B1) * g
        nv = ADAM_B2 * v_ref[...] + (1.0 - ADAM_B2) * (g * g)
        g_ref[...] = g
        nm_ref[...] = nm
        nv_ref[...] = nv
        d_ref[...] = -ADAM_LR * ((nm / bc1) / (jnp.sqrt(nv / bc2) + ADAM_EPS) + ADAM_WD * w_ref[...])

    blk = pl.BlockSpec((tr, c), lambda i: (i, 0))
    return pl.pallas_call(
        body, name=name, grid=(r // tr,),
        in_specs=[pl.BlockSpec((s, tr, c), lambda i: (0, i, 0)), blk, blk, blk],
        out_specs=[blk] * 4, out_shape=[SDS((r, c), F32)] * 4,
        compiler_params=_params(("parallel",)),
    )(parts, w, m, v)


EARLY_NAMES = ("spatial_w", "norm_mix_post", "norm_ffn_pre", "norm_ffn_post", "conv_b", "ln_v_gain", "ln_v_bias",
               "spatial_b")
LATE_NAMES = ("norm_mix_pre", "rel_bias")
PACK_ROW_ALIGN = 8


def _pack_rows(size):
    rows = -(-size // LANE)
    return -(-rows // PACK_ROW_ALIGN) * PACK_ROW_ALIGN


def _pack(arrays):
    flat = []
    for a in arrays:
        rows = _pack_rows(a.size)
        flat.append(jnp.pad(a.reshape(-1), (0, rows * LANE - a.size)))
    return jnp.concatenate(flat).reshape(-1, LANE)


def _unpack(packed, shapes):
    out, row = [], 0
    for shp in shapes:
        size = int(np.prod(shp))
        out.append(packed[row:row + _pack_rows(size)].reshape(-1)[:size].reshape(shp))
        row += _pack_rows(size)
    return out


def kernel(x, norm_mix_pre, norm_mix_post, norm_ffn_pre, norm_ffn_post, w_in, ln_v_gain, ln_v_bias, spatial_w, spatial_b, rel_bias, w_out, w_gate, w_up, conv_w, conv_b, w_down, loss_target, m_norm_mix_pre, m_norm_mix_post, m_norm_ffn_pre, m_norm_ffn_post, m_w_in, m_ln_v_gain, m_ln_v_bias, m_spatial_w, m_spatial_b, m_rel_bias, m_w_out, m_w_gate, m_w_up, m_conv_w, m_conv_b, m_w_down, v_norm_mix_pre, v_norm_mix_post, v_norm_ffn_pre, v_norm_ffn_post, v_w_in, v_ln_v_gain, v_ln_v_bias, v_spatial_w, v_spatial_b, v_rel_bias, v_w_out, v_w_gate, v_w_up, v_conv_w, v_conv_b, v_w_down):
    given = dict(locals())
    nb_local, seq, d = x.shape
    n = nb_local * seq
    cols = w_in.shape[2]

    def by_columns(g):
        return g.transpose(1, 0, 2).reshape(g.shape[1], N_DEV * g.shape[2])

    def by_rows(g):
        return g.reshape(N_DEV * g.shape[1], g.shape[2])

    def blocks(g):
        return g.reshape(N_DEV, g.shape[0] // N_DEV, g.shape[1])

    xf, target = x.reshape(n, d), loss_target.reshape(n, d)
    ln_g, ln_b = ln_v_gain.reshape(1, A_WIDTH), ln_v_bias.reshape(1, A_WIDTH)
    spatial_bt, rel_bias_t = spatial_b[0].T, rel_bias.T

    s_in, s_out, s_gate, s_up, s_down = cast_bf16(
        [w_in[0].T, w_out[0], w_gate[0].T, w_up[0].T, w_down[0]], "cast_shards")
    (bias,), (g_in, g_cw) = bias_tables(rel_bias_t, "bias_tables", ride=([], [s_in, conv_w[0]]))
    w_in_t, conv_w_f = by_rows(g_in), by_columns(g_cw)

    (h1, proj), _ = norm_mm(xf, norm_mix_pre, [w_in_t], "fwd_norm_in", tn=IN_COLS)
    a = gating_fwd(proj, ln_g, ln_b, spatial_w[0], spatial_bt, "fwd_gating")
    (b_out, lse_tot), (g_out, g_gate, g_up) = attn_fwd(proj, bias, nb_local, "fwd_attn",
                                                       ride=([], [s_out, s_gate, s_up]))
    w_out_f, w_gate_t, w_up_t = by_rows(g_out), by_rows(g_gate), by_rows(g_up)
    z2, x1 = mm_res_norm([a, b_out], w_out_f, xf, norm_mix_post, "fwd_out_norm")
    (h2, gp, up, gu), (g_down,) = norm_mm(x1, norm_ffn_pre, [w_gate_t, w_up_t], "fwd_norm_ffn_conv", tm=256, tn=D_FF,
                                          ride=([], [s_down]), conv=(conv_w_f, conv_b))
    w_down_f = by_rows(g_down)
    dy, dout, dg4, loss_part = down_loss(gu, w_down_f, x1, norm_ffn_post, target, "fwd_down_loss")

    p_down = mm_tn([gu], [dy], "bwd_dw_down", t1=256, t2=D_MODEL)
    (dgu,), _ = mm_nt([(dy, 0, 0)], [w_down_f], "bwd_dgu", out_dtype=BF16)
    (dgp, dup, p_conv_w, p_conv_b), (r_down,) = conv_gelu_bwd(
        dgu, gp, up, conv_w_f, conv_b, nb_local, "bwd_conv_gelu", ride=([blocks(p_down)], []))
    p_gate = mm_tn([dgp], [h2], "bwd_dw_gate", t1=256, t2=D_MODEL)
    p_up = mm_tn([dup], [h2], "bwd_dw_up", t1=256, t2=D_MODEL)
    (dx1, dz2, dg3, dg2), _ = mm_nt([(dgp, 0, 0), (dup, 1, 0)], [w_gate_t, w_up_t], "bwd_dh2_norm_mid", tm=256,
                                    by_rows=True,
                                    epilogue=norm_mid_epilogue(x1, dout, z2, norm_ffn_pre, norm_mix_post))
    p_out = mm_tn([a, b_out], [dz2], "bwd_dw_out", t1=256, t2=D_MODEL)
    (dmix,), _ = mm_nt([(dz2, 0, 0)], [w_out_f], "bwd_dmix")
    duv, p_ws, p_sbt, p_lng, p_lnb = gating_bwd(proj, dmix, ln_g, ln_b, spatial_w[0], spatial_bt, "bwd_gating")
    small = dict(spatial_w=p_ws, norm_mix_post=dg2, norm_ffn_pre=dg3, norm_ffn_post=dg4, conv_b=p_conv_b,
                 ln_v_gain=p_lng, ln_v_bias=p_lnb, spatial_b=p_sbt.T)
    pack_early = _pack([small[k] for k in EARLY_NAMES] + [p_conv_w, loss_part])
    (dq, dk, dv, dbias), (r_gate, r_up, r_out, r_early) = attn_bwd(
        proj, b_out, dmix, lse_tot, bias, nb_local, "bwd_attn",
        ride=([blocks(p_gate), blocks(p_up), blocks(p_out)], [pack_early]))
    p_rel_bias_t = rel_bias_grad(dbias.reshape(len(DILATIONS), B_HEADS, BIAS_SIZE), "bwd_rel_bias")
    p_in = mm_tn([duv, dq, dk, dv], [h1], "bwd_dw_in", t1=256, t2=D_MODEL)
    (grad_x, dg1), (r_in,) = mm_nt(
        [(duv, 0, 0), (dq, 0, Q_OFF), (dk, 0, K_OFF), (dv, 0, V_OFF)], [w_in_t], "bwd_dh1_norm_in", by_rows=True,
        epilogue=norm_in_epilogue(xf, dx1, norm_mix_pre), ride=([blocks(p_in)], []))
    small.update(norm_mix_pre=dg1, rel_bias=p_rel_bias_t.T)
    (r_late,) = exchange([], [_pack([small[k] for k in LATE_NAMES])], "exchange_late")

    res = {}
    for k, received in (("w_in", r_in), ("w_gate", r_gate), ("w_up", r_up)):
        res[k] = [o.T for o in adam_update(received, given[k][0].T, given["m_" + k][0].T, given["v_" + k][0].T,
                                           "adam_" + k, tr=cols // 2)]
    res["w_out"] = adam_update(r_out, w_out[0], m_w_out[0], v_w_out[0], "adam_w_out")
    res["w_down"] = adam_update(r_down, w_down[0], m_w_down[0], v_w_down[0], "adam_w_down", tr=cols // 2)

    def adam_packed(received, names, tail, name):
        zeros = [jnp.zeros_like(t) for t in tail]
        packs = [_pack([given[pre + k] for k in names] + zeros) for pre in ("", "m_", "v_")]
        shapes = [given[k].shape for k in names] + [t.shape for t in tail]
        unpacked = [_unpack(p, shapes) for p in adam_update(received, *packs, name)]
        for i, k in enumerate(names):
            res[k] = [u[i] for u in unpacked]
        return unpacked[0][len(names):]

    g_conv_w_full, loss_sum = adam_packed(r_early, EARLY_NAMES, [p_conv_w, loss_part], "adam_small_early")
    adam_packed(r_late, LATE_NAMES, [], "adam_small_late")
    g_conv_w = lax.dynamic_slice_in_dim(g_conv_w_full, _my_index() * cols, cols, axis=1)
    res["conv_w"] = adam_update(g_conv_w[None], conv_w[0], m_conv_w[0], v_conv_w[0], "adam_conv_w")
    loss = loss_sum[0, 0]

    names = ("norm_mix_pre", "norm_mix_post", "norm_ffn_pre", "norm_ffn_post", "w_in", "ln_v_gain", "ln_v_bias",
             "spatial_w", "spatial_b", "rel_bias", "w_out", "w_gate", "w_up", "conv_w", "conv_b", "w_down")
    outs = [loss, grad_x.reshape(x.shape)]
    for t in range(4):
        outs += [res[k][t].reshape(given[k].shape) for k in names]
    return tuple(outs)
```

```python
import functools
import math

import numpy as np
import jax
import jax.numpy as jnp
from jax import lax
from jax.experimental import pallas as pl
from jax.experimental.pallas import tpu as pltpu

F32 = jnp.float32
BF16 = jnp.bfloat16
SDS = jax.ShapeDtypeStruct

D_MODEL = 1024
SEQ = 2048
HEAD_DIM = 64
A_GROUPS = 4
A_WIDTH = A_GROUPS * HEAD_DIM
B_HEADS = 12
B_WIDTH = B_HEADS * HEAD_DIM
HEAD_PAIRS = B_HEADS // 2
CHUNK = 128
ATTN_BLOCK = 128
DILATIONS = (1, 4, 16)
NUM_BUCKETS = 32
MAX_DISTANCE = 2048
D_FF = 2816
IN_COLS = 2 * A_WIDTH + 3 * B_WIDTH
Q_OFF = 2 * A_WIDTH
K_OFF = Q_OFF + B_WIDTH
V_OFF = K_OFF + B_WIDTH
NORM_EPS = 1e-6
NEG_INF = -1e30
N_DEV = 8
LANE = 128

ADAM_LR = 0.001
ADAM_B1 = 0.9
ADAM_B2 = 0.999
ADAM_EPS = 1e-08
ADAM_WD = 0.01
ADAM_STEP = 10

GELU_C0 = math.sqrt(2.0 / math.pi)
GELU_C1 = 0.044715

VMEM_LIMIT = 56 * 1024 * 1024


def _params(sem=None):
    if sem is None:
        return pltpu.CompilerParams(vmem_limit_bytes=VMEM_LIMIT)
    return pltpu.CompilerParams(dimension_semantics=sem, vmem_limit_bytes=VMEM_LIMIT)


def _gelu(x):
    t = jnp.tanh(x * (GELU_C0 + (GELU_C0 * GELU_C1) * (x * x)))
    return x * (0.5 + 0.5 * t)


def _gelu_and_grad(x):
    x2 = x * x
    t = jnp.tanh(x * (GELU_C0 + (GELU_C0 * GELU_C1) * x2))
    half = 0.5 + 0.5 * t
    dg = half + x * (0.5 - 0.5 * (t * t)) * (GELU_C0 + (3.0 * GELU_C0 * GELU_C1) * x2)
    return x * half, dg


def _dot(a, b):
    return jnp.dot(a, b, preferred_element_type=F32)


def _dot_nt(a, b):
    return lax.dot_general(a, b, (((1,), (1,)), ((), ())), preferred_element_type=F32)


def _dot_tn(a, b):
    return lax.dot_general(a, b, (((0,), (0,)), ((), ())), preferred_element_type=F32)


def _rms_bwd(d, xin, g):
    r = lax.rsqrt(jnp.mean(xin * xin, axis=-1, keepdims=True) + NORM_EPS)
    xh = xin * r
    gd = g * d
    dx = r * (gd - xh * jnp.mean(gd * xh, axis=-1, keepdims=True))
    return dx, d * xh


MESH = pl.DeviceIdType.MESH
ANY = pl.BlockSpec(memory_space=pl.ANY)
PEER_MASKS = tuple(range(1, N_DEV))


def _my_index():
    return lax.axis_index("x") * 4 + lax.axis_index("y") * 2 + lax.axis_index("c")


def _peer(mask):
    x, y, c = lax.axis_index("x"), lax.axis_index("y"), lax.axis_index("c")
    px = 1 - x if mask & 4 else x
    py = 1 - y if mask & 2 else y
    pc = 1 - c if mask & 1 else c
    return (px, py, pc), px * 4 + py * 2 + pc


RELAY_AT = 3
SIBLING = 1
CHIP_MASKS = (2, 4, 6)


class _Exchange:
    def __init__(self, nblocked, in_refs, out_refs, sems):
        send_sems, recv_sems, local_sems = sems
        me = _my_index()
        sibling, _ = _peer(SIBLING)
        self.local, self.first, self.relays, self.relayed_in, self.last_in = [], [], [], [], []
        for a, (in_ref, out_ref) in enumerate(zip(in_refs, out_refs)):
            def copy(src, slot, mask, to):
                return pltpu.make_async_remote_copy(
                    src_ref=src, dst_ref=out_ref.at[slot], send_sem=send_sems.at[a, mask - 1],
                    recv_sem=recv_sems.at[a, mask - 1], device_id=to, device_id_type=MESH)

            if a < nblocked:
                self.local.append(pltpu.make_async_copy(in_ref.at[me], out_ref.at[me], local_sems.at[a]))
                for mask in PEER_MASKS:
                    peer, pidx = _peer(mask)
                    self.first.append(copy(in_ref.at[pidx], me, mask, peer))
                    self.last_in.append(copy(in_ref.at[pidx], pidx, mask, peer))
                continue
            self.local.append(pltpu.make_async_copy(in_ref, out_ref.at[me], local_sems.at[a]))
            for mask in (SIBLING,) + CHIP_MASKS:
                peer, pidx = _peer(mask)
                self.first.append(copy(in_ref, me, mask, peer))
                (self.last_in if mask == SIBLING else self.relayed_in).append(copy(in_ref, pidx, mask, peer))
            for mask in CHIP_MASKS:
                _, origin = _peer(mask)
                _, far = _peer(mask | SIBLING)
                self.relays.append(copy(out_ref.at[origin], origin, mask | SIBLING, sibling))
                self.last_in.append(copy(in_ref, far, mask | SIBLING, sibling))

    def start(self):
        for cp in self.local + self.first[::-1]:
            cp.start()

    def relay(self):
        for arrived, onward in zip(self.relayed_in, self.relays):
            arrived.wait_recv()
            onward.start()

    def finish(self):
        for cp in self.first + self.relays:
            cp.wait_send()
        for cp in self.last_in:
            cp.wait_recv()
        for cp in self.local:
            cp.wait()


def _exchange_out_shape(blocked, whole):
    return [SDS(b.shape, b.dtype) for b in blocked] + [SDS((N_DEV,) + w.shape, w.dtype) for w in whole]


def _exchange_sems(n):
    return [pltpu.SemaphoreType.DMA((n, N_DEV - 1)), pltpu.SemaphoreType.DMA((n, N_DEV - 1)),
            pltpu.SemaphoreType.DMA((n,))]


def exchange(blocked, whole, name):
    nb, n = len(blocked), len(blocked) + len(whole)

    def body(*refs):
        ex = _Exchange(nb, refs[:n], refs[n:2 * n], refs[2 * n:])
        ex.start()
        ex.relay()
        ex.finish()

    return pl.pallas_call(
        body, name=name, in_specs=[ANY] * n, out_specs=[ANY] * n, out_shape=_exchange_out_shape(blocked, whole),
        scratch_shapes=_exchange_sems(n),
    )(*blocked, *whole)


def _call(body, *, name, grid, in_specs, out_specs, out_shape, args, scratch_shapes=(), sem=None, ride=None):
    out_shape, out_specs, scratch_shapes = list(out_shape), list(out_specs), list(scratch_shapes)
    if ride is None:
        outs = pl.pallas_call(body, name=name, grid=grid, in_specs=list(in_specs), out_specs=out_specs,
                              out_shape=out_shape, scratch_shapes=scratch_shapes,
                              compiler_params=_params(sem))(*args)
        return list(outs), []
    blocked, whole = ride
    cargs = list(blocked) + list(whole)
    nb, nc = len(blocked), len(cargs)
    n_in, n_out, n_scr = len(args), len(out_shape), len(scratch_shapes)
    steps = math.prod(grid)
    assert steps >= 3, grid

    def riding(*refs):
        ins, refs = refs[:n_in], refs[n_in:]
        cins, refs = refs[:nc], refs[nc:]
        outs, refs = refs[:n_out], refs[n_out:]
        couts, refs = refs[:nc], refs[nc:]
        scr, sems = refs[:n_scr], refs[n_scr:]
        step = functools.reduce(lambda acc, k: acc * grid[k] + pl.program_id(k), range(len(grid)), 0)

        @pl.when(step == 0)
        def _():
            _Exchange(nb, cins, couts, sems).start()

        @pl.when(step == RELAY_AT * steps // 4)
        def _():
            _Exchange(nb, cins, couts, sems).relay()

        body(*ins, *outs, *scr)

        @pl.when(step == steps - 1)
        def _():
            _Exchange(nb, cins, couts, sems).finish()

    res = pl.pallas_call(
        riding, name=name, grid=grid, in_specs=list(in_specs) + [ANY] * nc, out_specs=out_specs + [ANY] * nc,
        out_shape=out_shape + _exchange_out_shape(blocked, whole),
        scratch_shapes=scratch_shapes + _exchange_sems(nc),
        compiler_params=_params(("arbitrary",) * len(grid)))(*args, *cargs)
    return list(res[:n_out]), list(res[n_out:])


def norm_mm(x, g, ws, name, tm=512, tn=1408, ride=None, conv=None):
    n, d = x.shape
    f = ws[0].shape[0]
    nw = len(ws)
    extra_in, extra_spec, extra_out, extra_out_spec, scratch = [], [], [], [], []
    if conv is not None:
        assert nw == 2 and tn == f and SEQ % tm == 0 and tm % CONV_ROWS == 0
        extra_in = list(conv)
        extra_spec = [pl.BlockSpec((3, f), lambda i, j: (0, 0)), pl.BlockSpec((1, f), lambda i, j: (0, 0))]
        extra_out, extra_out_spec = [SDS((n, f), BF16)], [pl.BlockSpec((tm, f), lambda i, j: (i, 0))]
        scratch = [pltpu.VMEM((PAD + CONV_ROWS, f), F32), pltpu.VMEM((PAD, f), F32)]

    def body(x_ref, g_ref, *refs):
        w_refs, refs = refs[:nw], refs[nw:]
        conv_refs, refs = refs[:len(extra_in)], refs[len(extra_in):]
        h_ref, o_refs, refs = refs[0], refs[1:1 + nw], refs[1 + nw:]

        @pl.when(pl.program_id(1) == 0)
        def _():
            xv = x_ref[...]
            r = lax.rsqrt(jnp.mean(xv * xv, axis=-1, keepdims=True) + NORM_EPS)
            h_ref[...] = (xv * r * g_ref[...]).astype(BF16)

        h = h_ref[...]
        for w_ref, o_ref in zip(w_refs, o_refs):
            o_ref[...] = _dot_nt(h, w_ref[...])
        if conv is None:
            return
        (cw_ref, cb_ref), (gp_ref, up_ref), (gu_ref, head_ref, carry_ref) = conv_refs, o_refs, refs

        @pl.when(pl.program_id(0) % (SEQ // tm) == 0)
        def _():
            carry_ref[...] = jnp.zeros_like(carry_ref)

        head_ref[0:PAD, :] = carry_ref[...]
        head_ref[PAD:PAD + CONV_ROWS, :] = gp_ref[0:CONV_ROWS, :]
        for l0 in range(0, f, LANE):
            ls = slice(l0, l0 + LANE)
            w0, w1, w2, bias = cw_ref[0:1, ls], cw_ref[1:2, ls], cw_ref[2:3, ls], cb_ref[:, ls]
            for r0 in range(0, tm, CONV_ROWS):
                g0, g1, g2 = _conv_taps(gp_ref, head_ref, r0, ls)
                c = bias + w0 * g2 + w1 * g1 + w2 * g0
                gu_ref[r0:r0 + CONV_ROWS, ls] = (_gelu(c) * up_ref[r0:r0 + CONV_ROWS, ls]).astype(BF16)
        carry_ref[...] = gp_ref[tm - PAD:tm, :]

    return _call(
        body, name=name, grid=(n // tm, f // tn),
        in_specs=[pl.BlockSpec((tm, d), lambda i, j: (i, 0)), pl.BlockSpec((1, d), lambda i, j: (0, 0))]
        + [pl.BlockSpec((tn, d), lambda i, j: (j, 0)) for _ in ws] + extra_spec,
        out_specs=[pl.BlockSpec((tm, d), lambda i, j: (i, 0))]
        + [pl.BlockSpec((tm, tn), lambda i, j: (i, j)) for _ in ws] + extra_out_spec,
        out_shape=[SDS((n, d), BF16)] + [SDS((n, f), F32) for _ in ws] + extra_out,
        scratch_shapes=scratch,
        args=[x, g, *ws, *extra_in], sem=("parallel" if conv is None else "arbitrary", "arbitrary"), ride=ride)


def _lane_concat(refs):
    vals = [r[...].astype(BF16) for r in refs]
    return vals[0] if len(vals) == 1 else jnp.concatenate(vals, axis=1)


def mm_res_norm(a_list, w, res, g, name, tm=512):
    n = a_list[0].shape[0]
    k, d = w.shape
    na = len(a_list)

    def body(*refs):
        w_ref, res_ref, g_ref, y_ref, o_ref = refs[na:]
        y = _dot(_lane_concat(refs[:na]), w_ref[...])
        r = lax.rsqrt(jnp.mean(y * y, axis=-1, keepdims=True) + NORM_EPS)
        y_ref[...] = y
        o_ref[...] = res_ref[...] + y * r * g_ref[...]

    return pl.pallas_call(
        body, name=name, grid=(n // tm,),
        in_specs=[pl.BlockSpec((tm, a.shape[1]), lambda i: (i, 0)) for a in a_list]
        + [pl.BlockSpec((k, d), lambda i: (0, 0)),
           pl.BlockSpec((tm, d), lambda i: (i, 0)), pl.BlockSpec((1, d), lambda i: (0, 0))],
        out_specs=[pl.BlockSpec((tm, d), lambda i: (i, 0)), pl.BlockSpec((tm, d), lambda i: (i, 0))],
        out_shape=[SDS((n, d), F32), SDS((n, d), F32)],
        compiler_params=_params(("parallel",)),
    )(*a_list, w, res, g)


def down_loss(a, w, res, g, target, name, tm=512):
    n, k = a.shape
    d = w.shape[1]
    inv_d = 1.0 / d

    def body(a_ref, w_ref, res_ref, g_ref, t_ref, dy_ref, dout_ref, dg_ref, loss_ref):
        i = pl.program_id(0)
        y = _dot(a_ref[...], w_ref[...])
        gv = g_ref[...]
        r = lax.rsqrt(jnp.mean(y * y, axis=-1, keepdims=True) + NORM_EPS)
        yh = y * r
        e = res_ref[...] + yh * gv - t_ref[...]
        part = 0.5 * inv_d * jnp.sum(jnp.sum(e * e, axis=-1, keepdims=True), axis=0, keepdims=True)
        dout = e * inv_d
        dout_ref[...] = dout
        gd = gv * dout
        dy_ref[...] = (r * (gd - yh * jnp.mean(gd * yh, axis=-1, keepdims=True))).astype(BF16)
        dgp = jnp.sum(dout * yh, axis=0, keepdims=True)
        lane0 = lax.broadcasted_iota(jnp.int32, (1, LANE), 1) == 0
        lp = jnp.where(lane0, part, 0.0)

        @pl.when(i == 0)
        def _():
            dg_ref[...] = dgp
            loss_ref[...] = lp

        @pl.when(i > 0)
        def _():
            dg_ref[...] += dgp
            loss_ref[...] += lp

    return pl.pallas_call(
        body, name=name, grid=(n // tm,),
        in_specs=[pl.BlockSpec((tm, k), lambda i: (i, 0)), pl.BlockSpec((k, d), lambda i: (0, 0)),
                  pl.BlockSpec((tm, d), lambda i: (i, 0)), pl.BlockSpec((1, d), lambda i: (0, 0)),
                  pl.BlockSpec((tm, d), lambda i: (i, 0))],
        out_specs=[pl.BlockSpec((tm, d), lambda i: (i, 0)), pl.BlockSpec((tm, d), lambda i: (i, 0)),
                   pl.BlockSpec((1, d), lambda i: (0, 0)), pl.BlockSpec((1, LANE), lambda i: (0, 0))],
        out_shape=[SDS((n, d), BF16), SDS((n, d), F32), SDS((1, d), F32), SDS((1, LANE), F32)],
        compiler_params=_params(("arbitrary",)),
    )(a, w, res, g, target)


def _accumulate(ref, val, step):
    @pl.when(step == 0)
    def _():
        ref[...] = val

    @pl.when(step > 0)
    def _():
        ref[...] += val


def mm_nt(terms, ws, name, tm=512, out_dtype=F32, ride=None, epilogue=None, by_rows=False):
    n = terms[0][0].shape[0]
    r = ws[0].shape[1 if by_rows else 0]
    na = len(terms)
    meta = [(widx, off, a.shape[1]) for a, widx, off in terms]
    fn, extras, out_shape = epilogue if epilogue else (None, [], [SDS((n, r), out_dtype)])
    n_fixed = na + len(ws)

    def body(*refs):
        a_refs = refs[:na]
        w_refs = refs[na:n_fixed]
        acc = None
        for a_ref, (widx, off, k) in zip(a_refs, meta):
            a = a_ref[...].astype(BF16)
            p = _dot(a, w_refs[widx][off:off + k, :]) if by_rows else _dot_nt(a, w_refs[widx][:, off:off + k])
            acc = p if acc is None else acc + p
        if fn is None:
            refs[-1][...] = acc.astype(out_dtype)
        else:
            fn(acc, pl.program_id(0), *refs[n_fixed:])

    def spec(a):
        if a.shape[0] == 1:
            return pl.BlockSpec(a.shape, lambda i: (0, 0))
        return pl.BlockSpec((tm, a.shape[1]), lambda i: (i, 0))

    return _call(
        body, name=name, grid=(n // tm,),
        in_specs=[spec(a) for a, _, _ in terms] + [pl.BlockSpec(w.shape, lambda i: (0, 0)) for w in ws]
        + [spec(e) for e in extras],
        out_specs=[spec(o) for o in out_shape], out_shape=out_shape,
        args=[a for a, _, _ in terms] + list(ws) + list(extras),
        sem=("parallel",) if fn is None else ("arbitrary",), ride=ride)


def _piece_blocks(pieces, tile):
    out, first = [], 0
    for p in pieces:
        nblk, rem = divmod(p.shape[1], tile)
        assert rem == 0, (p.shape, tile)
        out.append((first, nblk))
        first += nblk
    return out, first


def mm_tn(lhs_list, rhs_list, name, t1, t2, out_dtype=BF16):
    n = lhs_list[0].shape[0]
    lblocks, nbl = _piece_blocks(lhs_list, t1)
    rblocks, nbr = _piece_blocks(rhs_list, t2)
    nl = len(lhs_list)

    def body(*refs):
        l_refs, r_refs, o_ref = refs[:nl], refs[nl:-1], refs[-1]
        i, j = pl.program_id(0), pl.program_id(1)
        for l_ref, (ls, ln) in zip(l_refs, lblocks):
            for r_ref, (rs, rn) in zip(r_refs, rblocks):
                @pl.when((i >= ls) & (i < ls + ln) & (j >= rs) & (j < rs + rn))
                def _(l_ref=l_ref, r_ref=r_ref):
                    o_ref[...] = _dot_tn(l_ref[...].astype(BF16), r_ref[...].astype(BF16)).astype(out_dtype)

    def piece_spec(tile, axis, first, nblk):
        def index(i, j):
            return 0, jnp.clip((i, j)[axis] - first, 0, nblk - 1)
        return pl.BlockSpec((n, tile), index)

    return pl.pallas_call(
        body, name=name, grid=(nbl, nbr),
        in_specs=[piece_spec(t1, 0, *b) for b in lblocks] + [piece_spec(t2, 1, *b) for b in rblocks],
        out_specs=pl.BlockSpec((t1, t2), lambda i, j: (i, j)),
        out_shape=SDS((nbl * t1, nbr * t2), out_dtype),
        compiler_params=_params(("parallel", "arbitrary")),
    )(*lhs_list, *rhs_list)


GATE_ROWS = 512


def _tril_mask():
    row = lax.broadcasted_iota(jnp.int32, (CHUNK, CHUNK), 0)
    col = lax.broadcasted_iota(jnp.int32, (CHUNK, CHUNK), 1)
    return row >= col


def _group_of(shape, axis):
    return lax.broadcasted_iota(jnp.int32, shape, axis) // HEAD_DIM


def _group_mean_matrix():
    same = _group_of((A_WIDTH, A_WIDTH), 0) == _group_of((A_WIDTH, A_WIDTH), 1)
    return jnp.where(same, 1.0 / HEAD_DIM, 0.0).astype(BF16)


def _dot_sum(a, b):
    hi = a.astype(BF16)
    lo = (a - hi.astype(F32)).astype(BF16)
    return _dot(hi, b) + _dot(lo, b)


def _by_group(parts, lane_group):
    out = parts[A_GROUPS - 1]
    for g in range(A_GROUPS - 2, -1, -1):
        out = jnp.where(lane_group == g, parts[g], out)
    return out


def _group_norm(gv, gmean):
    xc = gv - _dot_sum(gv, gmean)
    rstd = lax.rsqrt(_dot_sum(xc * xc, gmean) + NORM_EPS)
    return xc * rstd, rstd


def gating_fwd(proj, lng, lnb, ws, sbt, name):
    n = proj.shape[0]

    def body(u_ref, v_ref, lng_ref, lnb_ref, ws_ref, sbt_ref, a_ref):
        tril = _tril_mask()
        lane_group = _group_of((CHUNK, A_WIDTH), 1)
        gmean = _group_mean_matrix()
        wts = [jnp.where(tril, ws_ref[g], 0.0).astype(BF16) for g in range(A_GROUPS)]
        sb = _by_group([sbt_ref[:, g:g + 1] for g in range(A_GROUPS)], lane_group)

        def chunk(c, carry):
            rows = pl.ds(c * CHUNK, CHUNK)
            vhat, _ = _group_norm(_gelu(v_ref[rows, :]), gmean)
            vn = (vhat * lng_ref[...] + lnb_ref[...]).astype(BF16)
            z = _by_group([_dot(wt, vn) for wt in wts], lane_group) + sb
            a_ref[rows, :] = (_gelu(u_ref[rows, :]) * z).astype(BF16)
            return carry

        for c in range(GATE_ROWS // CHUNK):
            chunk(c, 0)

    return pl.pallas_call(
        body, name=name, grid=(n // GATE_ROWS,),
        in_specs=[pl.BlockSpec((GATE_ROWS, A_WIDTH), lambda i: (i, 0)),
                  pl.BlockSpec((GATE_ROWS, A_WIDTH), lambda i: (i, 1)),
                  pl.BlockSpec((1, A_WIDTH), lambda i: (0, 0)), pl.BlockSpec((1, A_WIDTH), lambda i: (0, 0)),
                  pl.BlockSpec((A_GROUPS, CHUNK, CHUNK), lambda i: (0, 0, 0)),
                  pl.BlockSpec((CHUNK, A_GROUPS), lambda i: (0, 0))],
        out_specs=pl.BlockSpec((GATE_ROWS, A_WIDTH), lambda i: (i, 0)),
        out_shape=SDS((n, A_WIDTH), BF16),
        compiler_params=_params(("parallel",)),
    )(proj, proj, lng, lnb, ws, sbt)


def gating_bwd(proj, dmix, lng, lnb, ws, sbt, name):
    n = proj.shape[0]

    def body(u_ref, v_ref, da_ref, lng_ref, lnb_ref, ws_ref, sbt_ref,
             duv_ref, dws_ref, dsbt_ref, dlng_ref, dlnb_ref):
        @pl.when(pl.program_id(0) == 0)
        def _():
            dws_ref[...] = jnp.zeros_like(dws_ref)
            dsbt_ref[...] = jnp.zeros_like(dsbt_ref)
            dlng_ref[...] = jnp.zeros_like(dlng_ref)
            dlnb_ref[...] = jnp.zeros_like(dlnb_ref)

        tril = _tril_mask()
        lane_group = _group_of((CHUNK, A_WIDTH), 1)
        gmean = _group_mean_matrix()
        gsum = (_group_of((A_WIDTH, LANE), 0) == lax.broadcasted_iota(jnp.int32, (A_WIDTH, LANE), 1)).astype(BF16)
        wts = [jnp.where(tril, ws_ref[g], 0.0) for g in range(A_GROUPS)]
        wts_b = [w.astype(BF16) for w in wts]
        wts_t = [w.T.astype(BF16) for w in wts]
        sb = _by_group([sbt_ref[:, g:g + 1] for g in range(A_GROUPS)], lane_group)
        lg = lng_ref[...]

        def chunk(c, carry):
            rows = pl.ds(c * CHUNK, CHUNK)
            gu, dgu_dx = _gelu_and_grad(u_ref[rows, :])
            gv, dgv_dx = _gelu_and_grad(v_ref[rows, :])
            vhat, rstd = _group_norm(gv, gmean)
            vn = (vhat * lg + lnb_ref[...]).astype(BF16)
            z = _by_group([_dot(wt, vn) for wt in wts_b], lane_group) + sb
            da = da_ref[rows, :]
            dz = da * gu
            dzb = dz.astype(BF16)
            duv_ref[rows, 0:A_WIDTH] = (da * z * dgu_dx).astype(BF16)
            dsbt_ref[...] += _dot_sum(dz, gsum)[:, 0:A_GROUPS]
            for g in range(A_GROUPS):
                dz_g = jnp.where(lane_group == g, dzb, jnp.zeros_like(dzb))
                dws_ref[g] += jnp.where(tril, _dot_nt(dz_g, vn), 0.0)
            dvn = _by_group([_dot(wt, dzb) for wt in wts_t], lane_group)
            dlng_ref[...] += jnp.sum(dvn * vhat, axis=0, keepdims=True)
            dlnb_ref[...] += jnp.sum(dvn, axis=0, keepdims=True)
            dvh = dvn * lg
            dgv = rstd * (dvh - _dot_sum(dvh, gmean) - vhat * _dot_sum(dvh * vhat, gmean))
            duv_ref[rows, A_WIDTH:2 * A_WIDTH] = (dgv * dgv_dx).astype(BF16)
            return carry

        for c in range(GATE_ROWS // CHUNK):
            chunk(c, 0)

    return pl.pallas_call(
        body, name=name, grid=(n // GATE_ROWS,),
        in_specs=[pl.BlockSpec((GATE_ROWS, A_WIDTH), lambda i: (i, 0)),
                  pl.BlockSpec((GATE_ROWS, A_WIDTH), lambda i: (i, 1)),
                  pl.BlockSpec((GATE_ROWS, A_WIDTH), lambda i: (i, 0)),
                  pl.BlockSpec((1, A_WIDTH), lambda i: (0, 0)), pl.BlockSpec((1, A_WIDTH), lambda i: (0, 0)),
                  pl.BlockSpec((A_GROUPS, CHUNK, CHUNK), lambda i: (0, 0, 0)),
                  pl.BlockSpec((CHUNK, A_GROUPS), lambda i: (0, 0))],
        out_specs=[pl.BlockSpec((GATE_ROWS, 2 * A_WIDTH), lambda i: (i, 0)),
                   pl.BlockSpec((A_GROUPS, CHUNK, CHUNK), lambda i: (0, 0, 0)),
                   pl.BlockSpec((CHUNK, A_GROUPS), lambda i: (0, 0)),
                   pl.BlockSpec((1, A_WIDTH), lambda i: (0, 0)), pl.BlockSpec((1, A_WIDTH), lambda i: (0, 0))],
        out_shape=[SDS((n, 2 * A_WIDTH), BF16), SDS((A_GROUPS, CHUNK, CHUNK), F32), SDS((CHUNK, A_GROUPS), F32),
                   SDS((1, A_WIDTH), F32), SDS((1, A_WIDTH), F32)],
        compiler_params=_params(("arbitrary",)),
    )(proj, proj, dmix, lng, lnb, ws, sbt)


def _t5_bucket_np(dist):
    max_exact = NUM_BUCKETS // 2
    dd = np.maximum(dist, 1).astype(np.float64)
    large = max_exact + np.log(dd / max_exact) / math.log(MAX_DISTANCE / max_exact) * (NUM_BUCKETS - max_exact)
    large = np.minimum(large.astype(np.int64), NUM_BUCKETS - 1)
    return np.where(dist < max_exact, dist, large)


def _bucket_tables(with_first):
    i = np.arange(ATTN_BLOCK)[:, None]
    j = np.arange(2 * ATTN_BLOCK)[None, :]
    rel = ATTN_BLOCK + i - j
    band = (rel >= 0) & (rel <= ATTN_BLOCK)
    tabs = []
    for own_only in (False, True) if with_first else (False,):
        for dil in DILATIONS:
            b = _t5_bucket_np(np.maximum(rel, 0) * dil)
            tabs.append(np.where(band & (j >= ATTN_BLOCK) if own_only else band, b, -1).reshape(1, -1))
    return np.stack(tabs).astype(np.float32)


BIAS_SIZE = ATTN_BLOCK * 2 * ATTN_BLOCK


def bias_tables(rel_bias_t, name, ride=None):
    idx = jnp.asarray(_bucket_tables(True).reshape(-1, ATTN_BLOCK, 2 * ATTN_BLOCK))
    ntab = idx.shape[0]

    def body(rb_ref, idx_ref, o_ref):
        iv = idx_ref[0]

        def head(h, carry):
            t = jnp.full(iv.shape, NEG_INF, F32)
            for b in range(NUM_BUCKETS):
                t = jnp.where(iv == float(b), rb_ref[h, b], t)
            o_ref[0, h] = t
            return carry

        lax.fori_loop(0, B_HEADS, head, 0)

    return _call(
        body, name=name, grid=(ntab,),
        in_specs=[pl.BlockSpec(memory_space=pltpu.SMEM),
                  pl.BlockSpec((1, ATTN_BLOCK, 2 * ATTN_BLOCK), lambda d: (d, 0, 0))],
        out_specs=[pl.BlockSpec((1, B_HEADS, ATTN_BLOCK, 2 * ATTN_BLOCK), lambda d: (d, 0, 0, 0))],
        out_shape=[SDS((ntab, B_HEADS, ATTN_BLOCK, 2 * ATTN_BLOCK), F32)],
        args=[rel_bias_t, idx], sem=("parallel",), ride=ride)


def rel_bias_grad(dbias, name):
    idx = jnp.asarray(_bucket_tables(False))

    def body(db_ref, idx_ref, o_ref):
        d = pl.program_id(0)
        iv = idx_ref[0]
        bk = lax.broadcasted_iota(jnp.int32, (NUM_BUCKETS, BIAS_SIZE), 0).astype(F32)
        onehot = (bk == iv).astype(BF16)
        rest, part = db_ref[0], None
        for _ in range(3):
            term = rest.astype(BF16)
            rest = rest - term.astype(F32)
            p = _dot_nt(term, onehot)
            part = p if part is None else part + p

        @pl.when(d == 0)
        def _():
            o_ref[...] = part

        @pl.when(d > 0)
        def _():
            o_ref[...] += part

    return pl.pallas_call(
        body, name=name, grid=(len(DILATIONS),),
        in_specs=[pl.BlockSpec((1, B_HEADS, BIAS_SIZE), lambda d: (d, 0, 0)),
                  pl.BlockSpec((1, 1, BIAS_SIZE), lambda d: (d, 0, 0))],
        out_specs=pl.BlockSpec((B_HEADS, NUM_BUCKETS), lambda d: (0, 0)),
        out_shape=SDS((B_HEADS, NUM_BUCKETS), F32),
        compiler_params=_params(("arbitrary",)),
    )(dbias, idx)


QK_SCALE = 1.0 / math.sqrt(HEAD_DIM)


def _attn_scores(q_scaled, kk, bias):
    return _dot_nt(q_scaled, kk) + bias


def _head0_lanes():
    return lax.broadcasted_iota(jnp.int32, (ATTN_BLOCK, LANE), 1) < HEAD_DIM


def _one_head(x2, head0, hh):
    return jnp.where(head0 if hh == 0 else jnp.logical_not(head0), x2, 0.0).astype(BF16)


QUAD = 4
QUAD_ROWS = SEQ // QUAD


def _deinterleave(src_ref, dst_ref):
    for r in range(QUAD):
        for c in range(QUAD_ROWS // ATTN_BLOCK):
            dst_ref[r, c * ATTN_BLOCK:(c + 1) * ATTN_BLOCK, :] = src_ref[
                pl.ds(r + c * QUAD * ATTN_BLOCK, ATTN_BLOCK, stride=QUAD), :]


def _deinterleave_again(src_ref, dst_ref):
    for r in range(QUAD):
        for s in range(QUAD):
            dst_ref[r + QUAD * s] = src_ref[r, pl.ds(s, ATTN_BLOCK, stride=QUAD), :]


def _interleave_back(src_ref, dst_ref, slot0, accumulate=False):
    for r in range(QUAD):
        for s in range(QUAD):
            rows = pl.ds(s, ATTN_BLOCK, stride=QUAD)
            if accumulate:
                dst_ref[slot0 + r, rows, :] += src_ref[r + QUAD * s]
            else:
                dst_ref[slot0 + r, rows, :] = src_ref[r + QUAD * s]


def _quad_tiles():
    return [(r, pl.ds(r + c * QUAD * ATTN_BLOCK, ATTN_BLOCK, stride=QUAD), slice(c * ATTN_BLOCK, (c + 1) * ATTN_BLOCK))
            for r in range(QUAD) for c in range(QUAD_ROWS // ATTN_BLOCK)]


def _attn_schedule(op):
    for r in range(QUAD * QUAD):
        op(2, r, 0, True)
    for r in range(QUAD):
        for nq in range(QUAD_ROWS // ATTN_BLOCK):
            op(1, r, nq * ATTN_BLOCK, nq == 0)
    for nq in range(SEQ // ATTN_BLOCK):
        op(0, None, nq * ATTN_BLOCK, nq == 0)


def _keys(src, krows, first):
    kb = src[krows, :].astype(BF16)
    return jnp.concatenate([kb, kb], axis=0) if first else kb


def _table(seg, first):
    return len(DILATIONS) + seg if first else seg


def _kv_rows(start, first):
    return pl.ds(start, ATTN_BLOCK) if first else pl.ds(start - ATTN_BLOCK, 2 * ATTN_BLOCK)


MERGE_ROWS = 256


def attn_fwd(proj, bias, nb_local, name, ride=None):
    n = proj.shape[0]
    nseg = len(DILATIONS)

    def body(q_ref, k_ref, v_ref, b_ref, o_ref, lse_ref, q4_ref, k4_ref, v4_ref, os0_ref, ls0_ref, os4_ref, ls4_ref,
             q16_ref, k16_ref, v16_ref, os16_ref, ls16_ref):
        for src, mid, dst in ((q_ref, q4_ref, q16_ref), (k_ref, k4_ref, k16_ref), (v_ref, v4_ref, v16_ref)):
            _deinterleave(src, mid)
            _deinterleave_again(mid, dst)

        def op(seg, r, start, first):
            qrows = pl.ds(start, ATTN_BLOCK)
            krows = _kv_rows(start, first)
            if seg == 0:
                q_src, k_src, v_src, o_dst, l_dst = q_ref, k_ref, v_ref, os0_ref, ls0_ref
            elif seg == 1:
                q_src, k_src, v_src = q4_ref.at[r], k4_ref.at[r], v4_ref.at[r]
                o_dst, l_dst = os4_ref.at[r], ls4_ref.at[r]
            else:
                q_src, k_src, v_src = q16_ref.at[r], k16_ref.at[r], v16_ref.at[r]
                o_dst, l_dst = os16_ref.at[r], ls16_ref.at[r]
            q2, kb, vb = q_src[qrows, :] * QK_SCALE, _keys(k_src, krows, first), _keys(v_src, krows, first)
            head0 = _head0_lanes()
            outs, lses = [], []
            for hh in range(2):
                s = _attn_scores(_one_head(q2, head0, hh), kb, b_ref[_table(seg, first), hh])
                m = jnp.max(s, axis=-1, keepdims=True)
                p = jnp.exp(s - m)
                l = jnp.sum(p, axis=-1, keepdims=True)
                outs.append(_dot(p.astype(BF16), vb) / l)
                lses.append(jnp.broadcast_to(m + jnp.log(l), (ATTN_BLOCK, LANE)))
            o_dst[qrows, :] = jnp.where(head0, outs[0], outs[1])
            l_dst[qrows, :] = jnp.where(head0, lses[0], lses[1])

        _attn_schedule(op)
        _interleave_back(os16_ref, os4_ref, QUAD)
        _interleave_back(ls16_ref, ls4_ref, QUAD)

        for r, nat, quad in _quad_tiles():
            ls = [ls0_ref[nat, :], ls4_ref[r, quad, :], ls4_ref[QUAD + r, quad, :]]
            m = functools.reduce(jnp.maximum, ls)
            ws = [jnp.exp(l - m) for l in ls]
            den = ws[0] + ws[1] + ws[2]
            num = ws[0] * os0_ref[nat, :] + ws[1] * os4_ref[r, quad, :] + ws[2] * os4_ref[QUAD + r, quad, :]
            o_ref[nat, :] = num / den
            lse_ref[nat, :] = m + jnp.log(den)

    def in_spec(off):
        return pl.BlockSpec((SEQ, LANE), lambda b, p: (b, off // LANE + p))

    out_spec = pl.BlockSpec((SEQ, LANE), lambda b, p: (b, p))
    return _call(
        body, name=name, grid=(nb_local, HEAD_PAIRS),
        in_specs=[in_spec(Q_OFF), in_spec(K_OFF), in_spec(V_OFF),
                  pl.BlockSpec((2 * nseg, 2, ATTN_BLOCK, 2 * ATTN_BLOCK), lambda b, p: (0, p, 0, 0))],
        out_specs=[out_spec, out_spec],
        out_shape=[SDS((n, B_WIDTH), F32), SDS((n, B_WIDTH), F32)],
        scratch_shapes=[pltpu.VMEM((QUAD, QUAD_ROWS, LANE), F32)] * 3 + [pltpu.VMEM((SEQ, LANE), F32)] * 2
        + [pltpu.VMEM((2 * QUAD, QUAD_ROWS, LANE), F32)] * 2 + [pltpu.VMEM((QUAD * QUAD, ATTN_BLOCK, LANE), F32)] * 5,
        args=[proj, proj, proj, bias], sem=("parallel", "arbitrary"), ride=ride)


def attn_bwd(proj, b_out, dmix, lse_tot, bias, nb_local, name, ride=None):
    n = proj.shape[0]
    nseg = len(DILATIONS)
    a_blocks = A_WIDTH // LANE

    def body(q_ref, k_ref, v_ref, o_ref, do_ref, lse_ref, b_ref, dq_ref, dk_ref, dv_ref, db_ref,
             dqs_ref, delta_ref, dka_ref, dva_ref, q4_ref, k4_ref, v4_ref, do4_ref, lse4_ref, delta4_ref,
             dqs4_ref, dk4_ref, dv4_ref, q16_ref, k16_ref, v16_ref, do16_ref, lse16_ref, delta16_ref,
             dqs16_ref, dk16_ref, dv16_ref):
        @pl.when(pl.program_id(1) == 0)
        def _():
            db_ref[...] = jnp.zeros_like(db_ref)

        quads = (q4_ref, k4_ref, v4_ref, do4_ref, lse4_ref, delta4_ref)
        hexes = (q16_ref, k16_ref, v16_ref, do16_ref, lse16_ref, delta16_ref)

        head_sum = (lax.broadcasted_iota(jnp.int32, (LANE, LANE), 0) // HEAD_DIM
                    == lax.broadcasted_iota(jnp.int32, (LANE, LANE), 1) // HEAD_DIM).astype(BF16)

        def row_dots(i, carry):
            rows = pl.ds(pl.multiple_of(i * MERGE_ROWS, MERGE_ROWS), MERGE_ROWS)
            delta_ref[rows, :] = _dot_sum(do_ref[rows, :] * o_ref[rows, :], head_sum)
            return carry

        lax.fori_loop(0, SEQ // MERGE_ROWS, row_dots, 0)
        for src, mid, dst in zip((q_ref, k_ref, v_ref, do_ref, lse_ref, delta_ref), quads, hexes):
            _deinterleave(src, mid)
            _deinterleave_again(mid, dst)

        def op(seg, r, start, first):
            qrows = pl.ds(start, ATTN_BLOCK)
            krows = _kv_rows(start, first)
            if seg == 0:
                srcs = (q_ref, k_ref, v_ref, do_ref, lse_ref, delta_ref)
                dq_dst, dk_dst, dv_dst = dqs_ref, dka_ref, dva_ref
            elif seg == 1:
                srcs = tuple(x.at[r] for x in quads)
                dq_dst, dk_dst, dv_dst = dqs4_ref.at[r], dk4_ref.at[r], dv4_ref.at[r]
            else:
                srcs = tuple(x.at[r] for x in hexes)
                dq_dst, dk_dst, dv_dst = dqs16_ref.at[r], dk16_ref.at[r], dv16_ref.at[r]
            q_src, k_src, v_src, do_src, lse_src, delta_src = srcs
            q2, kb, vb = q_src[qrows, :] * QK_SCALE, _keys(k_src, krows, first), _keys(v_src, krows, first)
            do2, lse2, delta2 = do_src[qrows, :], lse_src[qrows, :], delta_src[qrows, :]
            head0 = _head0_lanes()
            dqs, dk, dv = [], None, None
            for hh in range(2):
                col = slice(hh * HEAD_DIM, hh * HEAD_DIM + 1)
                q, dob = _one_head(q2, head0, hh), _one_head(do2, head0, hh)
                p = jnp.exp(_attn_scores(q, kb, b_ref[_table(seg, first), hh]) - lse2[:, col])
                dvh = _dot_tn(p.astype(BF16), dob)
                ds = p * (_dot_nt(dob, vb) - delta2[:, col])
                if first:
                    db_ref[seg, hh, :, ATTN_BLOCK:] += ds[:, ATTN_BLOCK:]
                else:
                    db_ref[seg, hh] += ds
                dsb = ds.astype(BF16)
                dqs.append(_dot(dsb, kb))
                dkh = _dot_tn(dsb, q)
                dk = dkh if dk is None else dk + dkh
                dv = dvh if dv is None else dv + dvh
            dq_dst[qrows, :] = jnp.where(head0, dqs[0], dqs[1]) * QK_SCALE
            dk_dst[qrows, :] = dk[ATTN_BLOCK:]
            dv_dst[qrows, :] = dv[ATTN_BLOCK:]
            if not first:
                before = pl.ds(start - ATTN_BLOCK, ATTN_BLOCK)
                dk_dst[before, :] += dk[:ATTN_BLOCK]
                dv_dst[before, :] += dv[:ATTN_BLOCK]

        _attn_schedule(op)
        _interleave_back(dqs16_ref, dqs4_ref, QUAD)
        _interleave_back(dk16_ref, dk4_ref, 0, accumulate=True)
        _interleave_back(dv16_ref, dv4_ref, 0, accumulate=True)

        for r, nat, quad in _quad_tiles():
            dqs_ref[nat, :] += dqs4_ref[r, quad, :] + dqs4_ref[QUAD + r, quad, :]
            dka_ref[nat, :] += dk4_ref[r, quad, :]
            dva_ref[nat, :] += dv4_ref[r, quad, :]

        def merge(i, carry):
            rows = pl.ds(pl.multiple_of(i * MERGE_ROWS, MERGE_ROWS), MERGE_ROWS)
            dq_ref[rows, :] = dqs_ref[rows, :].astype(BF16)
            dk_ref[rows, :] = dka_ref[rows, :].astype(BF16)
            dv_ref[rows, :] = dva_ref[rows, :].astype(BF16)
            return carry

        lax.fori_loop(0, SEQ // MERGE_ROWS, merge, 0)

    def pspec(off):
        return pl.BlockSpec((SEQ, LANE), lambda p, b: (b, off // LANE + p))

    ospec = pl.BlockSpec((SEQ, LANE), lambda p, b: (b, p))
    bspec = pl.BlockSpec((nseg, 2, ATTN_BLOCK, 2 * ATTN_BLOCK), lambda p, b: (0, p, 0, 0))
    gshape = SDS((n, B_WIDTH), BF16)
    return _call(
        body, name=name, grid=(HEAD_PAIRS, nb_local),
        in_specs=[pspec(Q_OFF), pspec(K_OFF), pspec(V_OFF), ospec,
                  pl.BlockSpec((SEQ, LANE), lambda p, b: (b, a_blocks + p)), ospec,
                  pl.BlockSpec((2 * nseg, 2, ATTN_BLOCK, 2 * ATTN_BLOCK), lambda p, b: (0, p, 0, 0))],
        out_specs=[ospec, ospec, ospec, bspec],
        out_shape=[gshape, gshape, gshape, SDS((nseg, B_HEADS, ATTN_BLOCK, 2 * ATTN_BLOCK), F32)],
        scratch_shapes=[pltpu.VMEM((SEQ, LANE), F32)] * 4 + [pltpu.VMEM((QUAD, QUAD_ROWS, LANE), F32)] * 6
        + [pltpu.VMEM((2 * QUAD, QUAD_ROWS, LANE), F32)] + [pltpu.VMEM((QUAD, QUAD_ROWS, LANE), F32)] * 2
        + [pltpu.VMEM((QUAD * QUAD, ATTN_BLOCK, LANE), F32)] * 9,
        args=[proj, proj, proj, b_out, dmix, lse_tot, bias], sem=("arbitrary", "arbitrary"), ride=ride)


PAD = 8
CONV_ROWS = 64


CONV_LANES = 128


def _conv_taps(gp_ref, head_ref, r0, ls):
    g0 = gp_ref[r0:r0 + CONV_ROWS, ls]
    if r0 == 0:
        return g0, head_ref[PAD - 1:PAD - 1 + CONV_ROWS, ls], head_ref[PAD - 2:PAD - 2 + CONV_ROWS, ls]
    return g0, gp_ref[r0 - 1:r0 - 1 + CONV_ROWS, ls], gp_ref[r0 - 2:r0 - 2 + CONV_ROWS, ls]


def _fill_head(gp_ref, head_ref):
    head_ref[0:PAD, :] = jnp.zeros((PAD, CONV_LANES), F32)
    head_ref[PAD:PAD + CONV_ROWS, :] = gp_ref[0:CONV_ROWS, :]


def _lane_passes():
    return [slice(l0, l0 + LANE) for l0 in range(0, CONV_LANES, LANE)]


def conv_gelu_bwd(dgu, gp, up, cw, cb, nb_local, name, ride=None):
    n, f = gp.shape

    def fold(v):
        return jnp.sum(v.reshape(CONV_ROWS // 8, 8, LANE), axis=0)

    def body(dgu_ref, gp_ref, up_ref, cw_ref, cb_ref, dgp_ref, dup_ref, dcw_ref, dcb_ref, head_ref, dc_ref):
        b = pl.program_id(1)
        _fill_head(gp_ref, head_ref)
        dc_ref[SEQ:SEQ + PAD, :] = jnp.zeros((PAD, CONV_LANES), F32)
        for ls in _lane_passes():
            w0, w1, w2, bias = cw_ref[0:1, ls], cw_ref[1:2, ls], cw_ref[2:3, ls], cb_ref[:, ls]
            sums = [jnp.zeros((8, LANE), F32) for _ in range(4)]
            for r0 in range(0, SEQ, CONV_ROWS):
                rows = slice(r0, r0 + CONV_ROWS)
                g0, g1, g2 = _conv_taps(gp_ref, head_ref, r0, ls)
                gg, dgg = _gelu_and_grad(bias + w0 * g2 + w1 * g1 + w2 * g0)
                dgu = dgu_ref[rows, ls].astype(F32)
                dup_ref[rows, ls] = (dgu * gg).astype(BF16)
                dc = dgu * up_ref[rows, ls] * dgg
                dc_ref[rows, ls] = dc
                sums = [sums[0] + fold(dc * g2), sums[1] + fold(dc * g1), sums[2] + fold(dc * g0), sums[3] + fold(dc)]
            for r0 in range(0, SEQ, CONV_ROWS):
                dgp_ref[r0:r0 + CONV_ROWS, ls] = (
                    w2 * dc_ref[r0:r0 + CONV_ROWS, ls] + w1 * dc_ref[r0 + 1:r0 + 1 + CONV_ROWS, ls]
                    + w0 * dc_ref[r0 + 2:r0 + 2 + CONV_ROWS, ls]).astype(BF16)
            dcw = jnp.concatenate([jnp.sum(s, axis=0, keepdims=True) for s in sums[:3]], axis=0)
            dcb = jnp.sum(sums[3], axis=0, keepdims=True)

            @pl.when(b == 0)
            def _(dcw=dcw, dcb=dcb, ls=ls):
                dcw_ref[:, ls] = dcw
                dcb_ref[:, ls] = dcb

            @pl.when(b > 0)
            def _(dcw=dcw, dcb=dcb, ls=ls):
                dcw_ref[:, ls] += dcw
                dcb_ref[:, ls] += dcb

    blk = pl.BlockSpec((SEQ, CONV_LANES), lambda j, b: (b, j))
    wspec = pl.BlockSpec((3, CONV_LANES), lambda j, b: (0, j))
    bspec = pl.BlockSpec((1, CONV_LANES), lambda j, b: (0, j))
    return _call(
        body, name=name, grid=(f // CONV_LANES, nb_local),
        in_specs=[blk, blk, blk, wspec, bspec], out_specs=[blk, blk, wspec, bspec],
        out_shape=[SDS((n, f), BF16), SDS((n, f), BF16), SDS((3, f), F32), SDS((1, f), F32)],
        scratch_shapes=[pltpu.VMEM((PAD + CONV_ROWS, CONV_LANES), F32), pltpu.VMEM((SEQ + PAD, CONV_LANES), F32)],
        args=[dgu, gp, up, cw, cb], sem=("parallel", "arbitrary"), ride=ride)


def norm_mid_epilogue(x1, dout, z2, g3, g2):
    n, d = x1.shape

    def fn(dh2, step, x1_ref, dout_ref, z2_ref, g3_ref, g2_ref, dx1_ref, dz2_ref, dg3_ref, dg2_ref):
        dxa, dg3r = _rms_bwd(dh2, x1_ref[...], g3_ref[...])
        dx1 = dout_ref[...] + dxa
        dx1_ref[...] = dx1
        dz2, dg2r = _rms_bwd(dx1, z2_ref[...], g2_ref[...])
        dz2_ref[...] = dz2.astype(BF16)
        _accumulate(dg3_ref, jnp.sum(dg3r, axis=0, keepdims=True), step)
        _accumulate(dg2_ref, jnp.sum(dg2r, axis=0, keepdims=True), step)

    return fn, [x1, dout, z2, g3, g2], [SDS((n, d), F32), SDS((n, d), BF16), SDS((1, d), F32), SDS((1, d), F32)]


def norm_in_epilogue(x, dx1, g1):
    n, d = x.shape

    def fn(dh1, step, x_ref, dx1_ref, g1_ref, dx_ref, dg1_ref):
        dxa, dgr = _rms_bwd(dh1, x_ref[...], g1_ref[...])
        dx_ref[...] = dx1_ref[...] + dxa
        _accumulate(dg1_ref, jnp.sum(dgr, axis=0, keepdims=True), step)

    return fn, [x, dx1, g1], [SDS((n, d), F32), SDS((1, d), F32)]


def cast_bf16(arrays, name):
    def body(*refs):
        for i_ref, o_ref in zip(refs[:len(arrays)], refs[len(arrays):]):
            o_ref[...] = i_ref[...].astype(BF16)

    return pl.pallas_call(body, name=name, out_shape=[SDS(a.shape, BF16) for a in arrays],
                          compiler_params=_params())(*arrays)


def adam_update(parts, w, m, v, name, tr=None):
    s, r, c = parts.shape
    tr = r if tr is None else tr
    bc1 = 1.0 - ADAM_B1 ** ADAM_STEP
    bc2 = 1.0 - ADAM_B2 ** ADAM_STEP

    def body(p_ref, w_ref, m_ref, v_ref, g_ref, d_ref, nm_ref, nv_ref):
        g = p_ref[0].astype(F32)
        for j in range(1, s):
            g = g + p_ref[j].astype(F32)
        nm = ADAM_B1 * m_ref[...] + (1.0 - ADAM_B1) * g
        nv = ADAM_B2 * v_ref[...] + (1.0 - ADAM_B2) * (g * g)
        g_ref[...] = g
        nm_ref[...] = nm
        nv_ref[...] = nv
        d_ref[...] = -ADAM_LR * ((nm / bc1) / (jnp.sqrt(nv / bc2) + ADAM_EPS) + ADAM_WD * w_ref[...])

    blk = pl.BlockSpec((tr, c), lambda i: (i, 0))
    return pl.pallas_call(
        body, name=name, grid=(r // tr,),
        in_specs=[pl.BlockSpec((s, tr, c), lambda i: (0, i, 0)), blk, blk, blk],
        out_specs=[blk] * 4, out_shape=[SDS((r, c), F32)] * 4,
        compiler_params=_params(("parallel",)),
    )(parts, w, m, v)


EARLY_NAMES = ("spatial_w", "norm_mix_post", "norm_ffn_pre", "norm_ffn_post", "conv_b", "ln_v_gain", "ln_v_bias",
               "spatial_b")
LATE_NAMES = ("norm_mix_pre", "rel_bias")
PACK_ROW_ALIGN = 8


def _pack_rows(size):
    rows = -(-size // LANE)
    return -(-rows // PACK_ROW_ALIGN) * PACK_ROW_ALIGN


def _pack(arrays):
    flat = []
    for a in arrays:
        rows = _pack_rows(a.size)
        flat.append(jnp.pad(a.reshape(-1), (0, rows * LANE - a.size)))
    return jnp.concatenate(flat).reshape(-1, LANE)


def _unpack(packed, shapes):
    out, row = [], 0
    for shp in shapes:
        size = int(np.prod(shp))
        out.append(packed[row:row + _pack_rows(size)].reshape(-1)[:size].reshape(shp))
        row += _pack_rows(size)
    return out


def kernel(x, norm_mix_pre, norm_mix_post, norm_ffn_pre, norm_ffn_post, w_in, ln_v_gain, ln_v_bias, spatial_w, spatial_b, rel_bias, w_out, w_gate, w_up, conv_w, conv_b, w_down, loss_target, m_norm_mix_pre, m_norm_mix_post, m_norm_ffn_pre, m_norm_ffn_post, m_w_in, m_ln_v_gain, m_ln_v_bias, m_spatial_w, m_spatial_b, m_rel_bias, m_w_out, m_w_gate, m_w_up, m_conv_w, m_conv_b, m_w_down, v_norm_mix_pre, v_norm_mix_post, v_norm_ffn_pre, v_norm_ffn_post, v_w_in, v_ln_v_gain, v_ln_v_bias, v_spatial_w, v_spatial_b, v_rel_bias, v_w_out, v_w_gate, v_w_up, v_conv_w, v_conv_b, v_w_down):
    given = dict(locals())
    nb_local, seq, d = x.shape
    n = nb_local * seq
    cols = w_in.shape[2]

    def by_columns(g):
        return g.transpose(1, 0, 2).reshape(g.shape[1], N_DEV * g.shape[2])

    def by_rows(g):
        return g.reshape(N_DEV * g.shape[1], g.shape[2])

    def blocks(g):
        return g.reshape(N_DEV, g.shape[0] // N_DEV, g.shape[1])

    xf, target = x.reshape(n, d), loss_target.reshape(n, d)
    ln_g, ln_b = ln_v_gain.reshape(1, A_WIDTH), ln_v_bias.reshape(1, A_WIDTH)
    spatial_bt, rel_bias_t = spatial_b[0].T, rel_bias.T

    s_in, s_out, s_gate, s_up, s_down = cast_bf16(
        [w_in[0].T, w_out[0], w_gate[0].T, w_up[0].T, w_down[0]], "cast_shards")
    (bias,), (g_in, g_cw) = bias_tables(rel_bias_t, "bias_tables", ride=([], [s_in, conv_w[0]]))
    w_in_t, conv_w_f = by_rows(g_in), by_columns(g_cw)

    (h1, proj), _ = norm_mm(xf, norm_mix_pre, [w_in_t], "fwd_norm_in", tn=IN_COLS)
    a = gating_fwd(proj, ln_g, ln_b, spatial_w[0], spatial_bt, "fwd_gating")
    (b_out, lse_tot), (g_out, g_gate, g_up) = attn_fwd(proj, bias, nb_local, "fwd_attn",
                                                       ride=([], [s_out, s_gate, s_up]))
    w_out_f, w_gate_t, w_up_t = by_rows(g_out), by_rows(g_gate), by_rows(g_up)
    z2, x1 = mm_res_norm([a, b_out], w_out_f, xf, norm_mix_post, "fwd_out_norm")
    (h2, gp, up, gu), (g_down,) = norm_mm(x1, norm_ffn_pre, [w_gate_t, w_up_t], "fwd_norm_ffn_conv", tm=256, tn=D_FF,
                                          ride=([], [s_down]), conv=(conv_w_f, conv_b))
    w_down_f = by_rows(g_down)
    dy, dout, dg4, loss_part = down_loss(gu, w_down_f, x1, norm_ffn_post, target, "fwd_down_loss")

    p_down = mm_tn([gu], [dy], "bwd_dw_down", t1=256, t2=D_MODEL)
    (dgu,), _ = mm_nt([(dy, 0, 0)], [w_down_f], "bwd_dgu", out_dtype=BF16)
    (dgp, dup, p_conv_w, p_conv_b), (r_down,) = conv_gelu_bwd(
        dgu, gp, up, conv_w_f, conv_b, nb_local, "bwd_conv_gelu", ride=([blocks(p_down)], []))
    p_gate = mm_tn([dgp], [h2], "bwd_dw_gate", t1=256, t2=D_MODEL)
    p_up = mm_tn([dup], [h2], "bwd_dw_up", t1=256, t2=D_MODEL)
    (dx1, dz2, dg3, dg2), _ = mm_nt([(dgp, 0, 0), (dup, 1, 0)], [w_gate_t, w_up_t], "bwd_dh2_norm_mid", tm=256,
                                    by_rows=True,
                                    epilogue=norm_mid_epilogue(x1, dout, z2, norm_ffn_pre, norm_mix_post))
    p_out = mm_tn([a, b_out], [dz2], "bwd_dw_out", t1=256, t2=D_MODEL)
    (dmix,), _ = mm_nt([(dz2, 0, 0)], [w_out_f], "bwd_dmix")
    duv, p_ws, p_sbt, p_lng, p_lnb = gating_bwd(proj, dmix, ln_g, ln_b, spatial_w[0], spatial_bt, "bwd_gating")
    small = dict(spatial_w=p_ws, norm_mix_post=dg2, norm_ffn_pre=dg3, norm_ffn_post=dg4, conv_b=p_conv_b,
                 ln_v_gain=p_lng, ln_v_bias=p_lnb, spatial_b=p_sbt.T)
    pack_early = _pack([small[k] for k in EARLY_NAMES] + [p_conv_w, loss_part])
    (dq, dk, dv, dbias), (r_gate, r_up, r_out, r_early) = attn_bwd(
        proj, b_out, dmix, lse_tot, bias, nb_local, "bwd_attn",
        ride=([blocks(p_gate), blocks(p_up), blocks(p_out)], [pack_early]))
    p_rel_bias_t = rel_bias_grad(dbias.reshape(len(DILATIONS), B_HEADS, BIAS_SIZE), "bwd_rel_bias")
    p_in = mm_tn([duv, dq, dk, dv], [h1], "bwd_dw_in", t1=256, t2=D_MODEL)
    (grad_x, dg1), (r_in,) = mm_nt(
        [(duv, 0, 0), (dq, 0, Q_OFF), (dk, 0, K_OFF), (dv, 0, V_OFF)], [w_in_t], "bwd_dh1_norm_in", by_rows=True,
        epilogue=norm_in_epilogue(xf, dx1, norm_mix_pre), ride=([blocks(p_in)], []))
    small.update(norm_mix_pre=dg1, rel_bias=p_rel_bias_t.T)
    (r_late,) = exchange([], [_pack([small[k] for k in LATE_NAMES])], "exchange_late")

    res = {}
    for k, received in (("w_in", r_in), ("w_gate", r_gate), ("w_up", r_up)):
        res[k] = [o.T for o in adam_update(received, given[k][0].T, given["m_" + k][0].T, given["v_" + k][0].T,
                                           "adam_" + k, tr=cols // 2)]
    res["w_out"] = adam_update(r_out, w_out[0], m_w_out[0], v_w_out[0], "adam_w_out")
    res["w_down"] = adam_update(r_down, w_down[0], m_w_down[0], v_w_down[0], "adam_w_down", tr=cols // 2)

    def adam_packed(received, names, tail, name):
        zeros = [jnp.zeros_like(t) for t in tail]
        packs = [_pack([given[pre + k] for k in names] + zeros) for pre in ("", "m_", "v_")]
        shapes = [given[k].shape for k in names] + [t.shape for t in tail]
        unpacked = [_unpack(p, shapes) for p in adam_update(received, *packs, name)]
        for i, k in enumerate(names):
            res[k] = [u[i] for u in unpacked]
        return unpacked[0][len(names):]

    g_conv_w_full, loss_sum = adam_packed(r_early, EARLY_NAMES, [p_conv_w, loss_part], "adam_small_early")
    adam_packed(r_late, LATE_NAMES, [], "adam_small_late")
    g_conv_w = lax.dynamic_slice_in_dim(g_conv_w_full, _my_index() * cols, cols, axis=1)
    res["conv_w"] = adam_update(g_conv_w[None], conv_w[0], m_conv_w[0], v_conv_w[0], "adam_conv_w")
    loss = loss_sum[0, 0]

    names = ("norm_mix_pre", "norm_mix_post", "norm_ffn_pre", "norm_ffn_post", "w_in", "ln_v_gain", "ln_v_bias",
             "spatial_w", "spatial_b", "rel_bias", "w_out", "w_gate", "w_up", "conv_w", "conv_b", "w_down")
    outs = [loss, grad_x.reshape(x.shape)]
    for t in range(4):
        outs += [res[k][t].reshape(given[k].shape) for k in names]
    return tuple(outs)
```
